```python
import jax, jax.numpy as jnp
from jax import lax
import numpy as np

D_MODEL = 1024
BATCH = 32
SEQ = 2048
DEPTH = 4

N_MIXERS = 2
SSD_EXPAND = 2
D_INNER = SSD_EXPAND * D_MODEL
HEAD_DIM = 64
N_SSD_HEADS = D_INNER // HEAD_DIM
N_SSD_GROUPS = 4
HEADS_PER_GROUP = N_SSD_HEADS // N_SSD_GROUPS
D_STATE = 128
SSD_CONV = 4
CHUNK = 128
D_XBC = D_INNER + 2 * N_SSD_GROUPS * D_STATE
D_IN_PROJ = D_INNER + D_XBC + N_SSD_HEADS
DT_MIN = 1e-3
DT_MAX = 1e-1
A_MIN = 1.0
A_MAX = 16.0
POOL_WINDOWS = (2, 4, 8, 16)
N_POOL_GROUPS = len(POOL_WINDOWS)
POOL_GROUP_DIM = D_MODEL // N_POOL_GROUPS
D_FF = 2816
FFN_CONV = 3
EPS = 1e-6

kernel_name = 'hybrid_ssd_pool_convffn_trunk'


def rms_norm(x, w):
    xf = x.astype(jnp.float32)
    xf = xf * lax.rsqrt(jnp.mean(xf * xf, axis=-1, keepdims=True) + EPS)
    return (xf * w.astype(jnp.float32)).astype(x.dtype)


def causal_depthwise_conv(x, w, b):
    k = w.shape[0]
    y = lax.conv_general_dilated(x, w[:, None, :].astype(x.dtype), window_strides=(1,),
                                 padding=[(k - 1, 0)],
                                 dimension_numbers=('NWC', 'WIO', 'NWC'),
                                 feature_group_count=x.shape[-1])
    return y + b.astype(x.dtype)


def segsum_decay(a_cum):
    q = a_cum.shape[-1]
    mask = jnp.tril(jnp.ones((q, q), dtype=bool))
    diff = a_cum[..., :, None] - a_cum[..., None, :]
    return jnp.exp(jnp.where(mask, diff, -jnp.inf))


def ssd_chunked(xh, dt, a, bm, cm):
    b, l, h, p = xh.shape
    g, n = bm.shape[2], bm.shape[3]
    r = h // g
    nc = l // CHUNK
    xf = xh.astype(jnp.float32).reshape(b, nc, CHUNK, g, r, p)
    dtc = dt.reshape(b, nc, CHUNK, g, r)
    bc = bm.astype(jnp.float32).reshape(b, nc, CHUNK, g, n)
    cc = cm.astype(jnp.float32).reshape(b, nc, CHUNK, g, n)
    a_dt = jnp.moveaxis(dtc * a.reshape(g, r), 2, -1)
    a_cum = jnp.cumsum(a_dt, axis=-1)
    xdt = xf * dtc[..., None]
    decay = segsum_decay(a_cum)
    cb = jnp.einsum('bclgn,bcsgn->bcgls', cc, bc)
    y_diag = jnp.einsum('bcgls,bcgrls,bcsgrp->bclgrp', cb, decay, xdt)
    decay_to_end = jnp.exp(a_cum[..., -1:] - a_cum)
    states = jnp.einsum('bcsgn,bcgrs,bcsgrp->bcgrpn', bc, decay_to_end, xdt)
    chunk_decay = jnp.exp(a_cum[..., -1])

    def step(carry, inp):
        st, dec = inp
        return carry * dec[..., None, None] + st, carry

    h0 = jnp.zeros((b, g, r, p, n), jnp.float32)
    _, prev = lax.scan(step, h0, (jnp.moveaxis(states, 1, 0), jnp.moveaxis(chunk_decay, 1, 0)))
    prev = jnp.moveaxis(prev, 0, 1)
    y_off = jnp.einsum('bclgn,bcgrpn,bcgrl->bclgrp', cc, prev, jnp.exp(a_cum))
    return (y_diag + y_off).reshape(b, l, h, p)


def ssd_mixer(u, w_in, conv_w, conv_b, dt_bias, a_log, d_skip, norm_w, w_out):
    b, l, _ = u.shape
    zxbcdt = u @ w_in.astype(u.dtype)
    z, xbc, dt = jnp.split(zxbcdt, [D_INNER, D_INNER + D_XBC], axis=-1)
    xbc = jax.nn.silu(causal_depthwise_conv(xbc, conv_w, conv_b))
    xs, bm, cm = jnp.split(xbc, [D_INNER, D_INNER + N_SSD_GROUPS * D_STATE], axis=-1)
    xh = xs.reshape(b, l, N_SSD_HEADS, HEAD_DIM)
    bm = bm.reshape(b, l, N_SSD_GROUPS, D_STATE)
    cm = cm.reshape(b, l, N_SSD_GROUPS, D_STATE)
    dt = jax.nn.softplus(dt.astype(jnp.float32) + dt_bias.astype(jnp.float32))
    a = -jnp.exp(a_log.astype(jnp.float32))
    y = ssd_chunked(xh, dt, a, bm, cm)
    y = y + d_skip.astype(jnp.float32)[:, None] * xh.astype(jnp.float32)
    y = y.reshape(b, l, D_INNER) * jax.nn.silu(z.astype(jnp.float32))
    yg = y.reshape(b, l, N_SSD_GROUPS, D_INNER // N_SSD_GROUPS)
    yg = yg * lax.rsqrt(jnp.mean(yg * yg, axis=-1, keepdims=True) + EPS)
    y = (yg.reshape(b, l, D_INNER) * norm_w.astype(jnp.float32)).astype(u.dtype)
    return y @ w_out.astype(u.dtype)


def pool_mixer(u, w_pool, scale):
    b, l, d = u.shape
    uf = u.astype(jnp.float32)
    cs = jnp.pad(jnp.cumsum(uf, axis=1), ((0, 0), (1, 0), (0, 0)))
    pos = jnp.arange(1, l + 1, dtype=jnp.float32)[None, :, None]
    groups = []
    for k, w in enumerate(POOL_WINDOWS):
        cg = cs[..., k * POOL_GROUP_DIM:(k + 1) * POOL_GROUP_DIM]
        lagged = jnp.pad(cg, ((0, 0), (w - 1, 0), (0, 0)))[:, :l]
        mean = (cg[:, 1:] - lagged) / jnp.minimum(pos, float(w))
        groups.append(mean - uf[..., k * POOL_GROUP_DIM:(k + 1) * POOL_GROUP_DIM])
    mixed = jnp.stack(groups, axis=2)
    out = jnp.einsum('blgc,gcd->blgd', mixed, w_pool.astype(jnp.float32)).reshape(b, l, d)
    return (out * scale.astype(jnp.float32)).astype(u.dtype)


def conv_ffn(u, w_up, conv_w, conv_b, w_down):
    h = u @ w_up.astype(u.dtype)
    h = causal_depthwise_conv(h, conv_w, conv_b)
    gate, val = jnp.split(h, 2, axis=-1)
    return (jax.nn.silu(gate) * val) @ w_down.astype(u.dtype)


def _fwd_setup_inputs(seed: int = 0) -> dict:
    key = jax.random.key(seed)
    ks = jax.random.split(key, 24)
    n_ssd = (DEPTH + N_MIXERS - 1) // N_MIXERS
    n_pool = DEPTH // N_MIXERS
    nrm = jax.random.normal
    x = nrm(ks[0], (BATCH, SEQ, D_MODEL), jnp.float32)
    ssd_w_in = nrm(ks[1], (n_ssd, D_MODEL, D_IN_PROJ), jnp.float32) * D_MODEL ** -0.5
    ssd_conv_w = nrm(ks[2], (n_ssd, SSD_CONV, D_XBC), jnp.float32) * SSD_CONV ** -0.5
    ssd_conv_b = nrm(ks[3], (n_ssd, D_XBC), jnp.float32) * 0.01
    u = jax.random.uniform(ks[4], (n_ssd, N_SSD_HEADS), jnp.float32)
    dt0 = jnp.exp(u * (np.log(DT_MAX) - np.log(DT_MIN)) + np.log(DT_MIN))
    ssd_dt_bias = dt0 + jnp.log(-jnp.expm1(-dt0))
    ssd_a_log = jnp.log(jax.random.uniform(ks[5], (n_ssd, N_SSD_HEADS), jnp.float32, A_MIN, A_MAX))
    ssd_d = 1.0 + 0.1 * nrm(ks[6], (n_ssd, N_SSD_HEADS), jnp.float32)
    ssd_norm_w = 1.0 + 0.1 * nrm(ks[7], (n_ssd, D_INNER), jnp.float32)
    ssd_w_out = nrm(ks[8], (n_ssd, D_INNER, D_MODEL), jnp.float32) * D_INNER ** -0.5
    pool_w = nrm(ks[9], (n_pool, N_POOL_GROUPS, POOL_GROUP_DIM, POOL_GROUP_DIM), jnp.float32) * POOL_GROUP_DIM ** -0.5
    pool_scale = 1.0 + 0.1 * nrm(ks[10], (n_pool, D_MODEL), jnp.float32)
    ffn_w_up = nrm(ks[11], (DEPTH, D_MODEL, 2 * D_FF), jnp.float32) * D_MODEL ** -0.5
    ffn_conv_w = nrm(ks[12], (DEPTH, FFN_CONV, 2 * D_FF), jnp.float32) * FFN_CONV ** -0.5
    ffn_conv_b = nrm(ks[13], (DEPTH, 2 * D_FF), jnp.float32) * 0.01
    ffn_w_down = nrm(ks[14], (DEPTH, D_FF, D_MODEL), jnp.float32) * D_FF ** -0.5
    norm_mix_pre = 1.0 + 0.1 * nrm(ks[15], (DEPTH, D_MODEL), jnp.float32)
    norm_mix_post = 1.0 + 0.1 * nrm(ks[16], (DEPTH, D_MODEL), jnp.float32)
    norm_ffn_pre = 1.0 + 0.1 * nrm(ks[17], (DEPTH, D_MODEL), jnp.float32)
    norm_ffn_post = 1.0 + 0.1 * nrm(ks[18], (DEPTH, D_MODEL), jnp.float32)
    return {'x': x, 'ssd_w_in': ssd_w_in, 'ssd_conv_w': ssd_conv_w, 'ssd_conv_b': ssd_conv_b,
            'ssd_dt_bias': ssd_dt_bias, 'ssd_a_log': ssd_a_log, 'ssd_d': ssd_d,
            'ssd_norm_w': ssd_norm_w, 'ssd_w_out': ssd_w_out, 'pool_w': pool_w,
            'pool_scale': pool_scale, 'ffn_w_up': ffn_w_up, 'ffn_conv_w': ffn_conv_w,
            'ffn_conv_b': ffn_conv_b, 'ffn_w_down': ffn_w_down, 'norm_mix_pre': norm_mix_pre,
            'norm_mix_post': norm_mix_post, 'norm_ffn_pre': norm_ffn_pre,
            'norm_ffn_post': norm_ffn_post}


def _fwd_reference(x, ssd_w_in, ssd_conv_w, ssd_conv_b, ssd_dt_bias, ssd_a_log, ssd_d,
              ssd_norm_w, ssd_w_out, pool_w, pool_scale, ffn_w_up, ffn_conv_w, ffn_conv_b,
              ffn_w_down, norm_mix_pre, norm_mix_post, norm_ffn_pre, norm_ffn_post):
    for i in range(DEPTH):
        j = i // N_MIXERS
        h = rms_norm(x, norm_mix_pre[i])
        if i % N_MIXERS == 0:
            mix = ssd_mixer(h, ssd_w_in[j], ssd_conv_w[j], ssd_conv_b[j], ssd_dt_bias[j],
                            ssd_a_log[j], ssd_d[j], ssd_norm_w[j], ssd_w_out[j])
        else:
            mix = pool_mixer(h, pool_w[j], pool_scale[j])
        x = x + rms_norm(mix, norm_mix_post[i])
        f = conv_ffn(rms_norm(x, norm_ffn_pre[i]), ffn_w_up[i], ffn_conv_w[i], ffn_conv_b[i],
                     ffn_w_down[i])
        x = x + rms_norm(f, norm_ffn_post[i])
    return x


import jax as _jax
import jax.numpy as _jnp

TWIN_FORMAT = 'train_step'
FWD_PARAMS = ['x', 'ssd_w_in', 'ssd_conv_w', 'ssd_conv_b', 'ssd_dt_bias', 'ssd_a_log', 'ssd_d', 'ssd_norm_w', 'ssd_w_out', 'pool_w', 'pool_scale', 'ffn_w_up', 'ffn_conv_w', 'ffn_conv_b', 'ffn_w_down', 'norm_mix_pre', 'norm_mix_post', 'norm_ffn_pre', 'norm_ffn_post']
TWIN_WEIGHTS = ['ssd_w_in', 'ssd_conv_w', 'ssd_conv_b', 'ssd_dt_bias', 'ssd_a_log', 'ssd_d', 'ssd_norm_w', 'ssd_w_out', 'pool_w', 'pool_scale', 'ffn_w_up', 'ffn_conv_w', 'ffn_conv_b', 'ffn_w_down', 'norm_mix_pre', 'norm_mix_post', 'norm_ffn_pre', 'norm_ffn_post']
TWIN_DIFF_INPUT = 'x'
TWIN_INPUTS = ['x', 'ssd_w_in', 'ssd_conv_w', 'ssd_conv_b', 'ssd_dt_bias', 'ssd_a_log', 'ssd_d', 'ssd_norm_w', 'ssd_w_out', 'pool_w', 'pool_scale', 'ffn_w_up', 'ffn_conv_w', 'ffn_conv_b', 'ffn_w_down', 'norm_mix_pre', 'norm_mix_post', 'norm_ffn_pre', 'norm_ffn_post', 'loss_target', 'm_ssd_w_in', 'm_ssd_conv_w', 'm_ssd_conv_b', 'm_ssd_dt_bias', 'm_ssd_a_log', 'm_ssd_d', 'm_ssd_norm_w', 'm_ssd_w_out', 'm_pool_w', 'm_pool_scale', 'm_ffn_w_up', 'm_ffn_conv_w', 'm_ffn_conv_b', 'm_ffn_w_down', 'm_norm_mix_pre', 'm_norm_mix_post', 'm_norm_ffn_pre', 'm_norm_ffn_post', 'v_ssd_w_in', 'v_ssd_conv_w', 'v_ssd_conv_b', 'v_ssd_dt_bias', 'v_ssd_a_log', 'v_ssd_d', 'v_ssd_norm_w', 'v_ssd_w_out', 'v_pool_w', 'v_pool_scale', 'v_ffn_w_up', 'v_ffn_conv_w', 'v_ffn_conv_b', 'v_ffn_w_down', 'v_norm_mix_pre', 'v_norm_mix_post', 'v_norm_ffn_pre', 'v_norm_ffn_post']
TWIN_OUTPUTS = ['loss', 'grad_x', 'grad_ssd_w_in', 'grad_ssd_conv_w', 'grad_ssd_conv_b', 'grad_ssd_dt_bias', 'grad_ssd_a_log', 'grad_ssd_d', 'grad_ssd_norm_w', 'grad_ssd_w_out', 'grad_pool_w', 'grad_pool_scale', 'grad_ffn_w_up', 'grad_ffn_conv_w', 'grad_ffn_conv_b', 'grad_ffn_w_down', 'grad_norm_mix_pre', 'grad_norm_mix_post', 'grad_norm_ffn_pre', 'grad_norm_ffn_post', 'delta_ssd_w_in', 'delta_ssd_conv_w', 'delta_ssd_conv_b', 'delta_ssd_dt_bias', 'delta_ssd_a_log', 'delta_ssd_d', 'delta_ssd_norm_w', 'delta_ssd_w_out', 'delta_pool_w', 'delta_pool_scale', 'delta_ffn_w_up', 'delta_ffn_conv_w', 'delta_ffn_conv_b', 'delta_ffn_w_down', 'delta_norm_mix_pre', 'delta_norm_mix_post', 'delta_norm_ffn_pre', 'delta_norm_ffn_post', 'new_m_ssd_w_in', 'new_m_ssd_conv_w', 'new_m_ssd_conv_b', 'new_m_ssd_dt_bias', 'new_m_ssd_a_log', 'new_m_ssd_d', 'new_m_ssd_norm_w', 'new_m_ssd_w_out', 'new_m_pool_w', 'new_m_pool_scale', 'new_m_ffn_w_up', 'new_m_ffn_conv_w', 'new_m_ffn_conv_b', 'new_m_ffn_w_down', 'new_m_norm_mix_pre', 'new_m_norm_mix_post', 'new_m_norm_ffn_pre', 'new_m_norm_ffn_post', 'new_v_ssd_w_in', 'new_v_ssd_conv_w', 'new_v_ssd_conv_b', 'new_v_ssd_dt_bias', 'new_v_ssd_a_log', 'new_v_ssd_d', 'new_v_ssd_norm_w', 'new_v_ssd_w_out', 'new_v_pool_w', 'new_v_pool_scale', 'new_v_ffn_w_up', 'new_v_ffn_conv_w', 'new_v_ffn_conv_b', 'new_v_ffn_w_down', 'new_v_norm_mix_pre', 'new_v_norm_mix_post', 'new_v_norm_ffn_pre', 'new_v_norm_ffn_post']
TWIN_LEAF_KINDS = {'loss': 'loss', 'grad_x': 'grad_x', 'grad_ssd_w_in': 'grad_w', 'grad_ssd_conv_w': 'grad_w', 'grad_ssd_conv_b': 'grad_w', 'grad_ssd_dt_bias': 'grad_w', 'grad_ssd_a_log': 'grad_w', 'grad_ssd_d': 'grad_w', 'grad_ssd_norm_w': 'grad_w', 'grad_ssd_w_out': 'grad_w', 'grad_pool_w': 'grad_w', 'grad_pool_scale': 'grad_w', 'grad_ffn_w_up': 'grad_w', 'grad_ffn_conv_w': 'grad_w', 'grad_ffn_conv_b': 'grad_w', 'grad_ffn_w_down': 'grad_w', 'grad_norm_mix_pre': 'grad_w', 'grad_norm_mix_post': 'grad_w', 'grad_norm_ffn_pre': 'grad_w', 'grad_norm_ffn_post': 'grad_w', 'delta_ssd_w_in': 'delta_w', 'delta_ssd_conv_w': 'delta_w', 'delta_ssd_conv_b': 'delta_w', 'delta_ssd_dt_bias': 'delta_w', 'delta_ssd_a_log': 'delta_w', 'delta_ssd_d': 'delta_w', 'delta_ssd_norm_w': 'delta_w', 'delta_ssd_w_out': 'delta_w', 'delta_pool_w': 'delta_w', 'delta_pool_scale': 'delta_w', 'delta_ffn_w_up': 'delta_w', 'delta_ffn_conv_w': 'delta_w', 'delta_ffn_conv_b': 'delta_w', 'delta_ffn_w_down': 'delta_w', 'delta_norm_mix_pre': 'delta_w', 'delta_norm_mix_post': 'delta_w', 'delta_norm_ffn_pre': 'delta_w', 'delta_norm_ffn_post': 'delta_w', 'new_m_ssd_w_in': 'new_m', 'new_m_ssd_conv_w': 'new_m', 'new_m_ssd_conv_b': 'new_m', 'new_m_ssd_dt_bias': 'new_m', 'new_m_ssd_a_log': 'new_m', 'new_m_ssd_d': 'new_m', 'new_m_ssd_norm_w': 'new_m', 'new_m_ssd_w_out': 'new_m', 'new_m_pool_w': 'new_m', 'new_m_pool_scale': 'new_m', 'new_m_ffn_w_up': 'new_m', 'new_m_ffn_conv_w': 'new_m', 'new_m_ffn_conv_b': 'new_m', 'new_m_ffn_w_down': 'new_m', 'new_m_norm_mix_pre': 'new_m', 'new_m_norm_mix_post': 'new_m', 'new_m_norm_ffn_pre': 'new_m', 'new_m_norm_ffn_post': 'new_m', 'new_v_ssd_w_in': 'new_v', 'new_v_ssd_conv_w': 'new_v', 'new_v_ssd_conv_b': 'new_v', 'new_v_ssd_dt_bias': 'new_v', 'new_v_ssd_a_log': 'new_v', 'new_v_ssd_d': 'new_v', 'new_v_ssd_norm_w': 'new_v', 'new_v_ssd_w_out': 'new_v', 'new_v_pool_w': 'new_v', 'new_v_pool_scale': 'new_v', 'new_v_ffn_w_up': 'new_v', 'new_v_ffn_conv_w': 'new_v', 'new_v_ffn_conv_b': 'new_v', 'new_v_ffn_w_down': 'new_v', 'new_v_norm_mix_pre': 'new_v', 'new_v_norm_mix_post': 'new_v', 'new_v_norm_ffn_pre': 'new_v', 'new_v_norm_ffn_post': 'new_v'}


def _forward(args):
    return _fwd_reference(*[args[k] for k in FWD_PARAMS])


def _output_shape():
    out = _jax.eval_shape(lambda: _forward(_fwd_setup_inputs(0)))
    return out.shape, out.dtype

N_MICROBATCH = 1
ADAM_LR = 0.001
ADAM_B1 = 0.9
ADAM_B2 = 0.999
ADAM_EPS = 1e-08
ADAM_WD = 0.01
ADAM_STEP = 10
PER_EXAMPLE_BATCH_AXIS = {'x': 0, 'loss_target': 0}
SHARED_INPUTS = []
_WEIGHT_DTYPES = {'ssd_w_in': _jnp.float32, 'ssd_conv_w': _jnp.float32, 'ssd_conv_b': _jnp.float32, 'ssd_dt_bias': _jnp.float32, 'ssd_a_log': _jnp.float32, 'ssd_d': _jnp.float32, 'ssd_norm_w': _jnp.float32, 'ssd_w_out': _jnp.float32, 'pool_w': _jnp.float32, 'pool_scale': _jnp.float32, 'ffn_w_up': _jnp.float32, 'ffn_conv_w': _jnp.float32, 'ffn_conv_b': _jnp.float32, 'ffn_w_down': _jnp.float32, 'norm_mix_pre': _jnp.float32, 'norm_mix_post': _jnp.float32, 'norm_ffn_pre': _jnp.float32, 'norm_ffn_post': _jnp.float32}
MOMENT_SCALE = {'ssd_w_in': 1.369491e+00, 'ssd_conv_w': 1.438774e+00, 'ssd_conv_b': 3.469513e+00, 'ssd_dt_bias': 4.781426e+00, 'ssd_a_log': 8.162431e+00, 'ssd_d': 1.653320e+01, 'ssd_norm_w': 2.191728e+00, 'ssd_w_out': 2.867687e+00, 'pool_w': 6.557904e+00, 'pool_scale': 1.363101e+01, 'ffn_w_up': 8.447096e-01, 'ffn_conv_w': 9.140119e-01, 'ffn_conv_b': 2.348103e+00, 'ffn_w_down': 1.627178e+00, 'norm_mix_pre': 3.995890e+00, 'norm_mix_post': 6.515438e+01, 'norm_ffn_pre': 1.957309e+00, 'norm_ffn_post': 6.413520e+01}


def _to_microbatches(a, axis):
    t = _jnp.moveaxis(a, axis, 0)
    t = t.reshape((N_MICROBATCH, t.shape[0] // N_MICROBATCH) + t.shape[1:])
    return _jnp.moveaxis(t, 1, axis + 1)


def setup_inputs(seed: int = 0) -> dict:
    inp = _fwd_setup_inputs(seed)
    key = _jax.random.fold_in(_jax.random.key(seed), 7919)
    shape, _ = _output_shape()
    out = dict(inp)
    out["loss_target"] = _jax.random.normal(_jax.random.fold_in(key, 0), shape, _jnp.float32)
    for i, name in enumerate(TWIN_WEIGHTS):
        w = inp[name].astype(_jnp.float32)
        if MOMENT_SCALE is None:
            s = _jnp.sqrt(_jnp.mean(_jnp.square(w)) + 1e-30)
        else:
            s = MOMENT_SCALE[name]
        km, kv = _jax.random.split(_jax.random.fold_in(key, i + 1))
        out[name] = w
        out["m_" + name] = s * _jax.random.normal(km, w.shape, _jnp.float32)
        out["v_" + name] = (s * s) * _jax.random.uniform(kv, w.shape, _jnp.float32, 0.5, 1.5)
    if N_MICROBATCH > 1:
        for name, axis in PER_EXAMPLE_BATCH_AXIS.items():
            out[name] = _to_microbatches(out[name], axis)
    return {'x': out['x'], 'ssd_w_in': out['ssd_w_in'], 'ssd_conv_w': out['ssd_conv_w'], 'ssd_conv_b': out['ssd_conv_b'], 'ssd_dt_bias': out['ssd_dt_bias'], 'ssd_a_log': out['ssd_a_log'], 'ssd_d': out['ssd_d'], 'ssd_norm_w': out['ssd_norm_w'], 'ssd_w_out': out['ssd_w_out'], 'pool_w': out['pool_w'], 'pool_scale': out['pool_scale'], 'ffn_w_up': out['ffn_w_up'], 'ffn_conv_w': out['ffn_conv_w'], 'ffn_conv_b': out['ffn_conv_b'], 'ffn_w_down': out['ffn_w_down'], 'norm_mix_pre': out['norm_mix_pre'], 'norm_mix_post': out['norm_mix_post'], 'norm_ffn_pre': out['norm_ffn_pre'], 'norm_ffn_post': out['norm_ffn_post'], 'loss_target': out['loss_target'], 'm_ssd_w_in': out['m_ssd_w_in'], 'm_ssd_conv_w': out['m_ssd_conv_w'], 'm_ssd_conv_b': out['m_ssd_conv_b'], 'm_ssd_dt_bias': out['m_ssd_dt_bias'], 'm_ssd_a_log': out['m_ssd_a_log'], 'm_ssd_d': out['m_ssd_d'], 'm_ssd_norm_w': out['m_ssd_norm_w'], 'm_ssd_w_out': out['m_ssd_w_out'], 'm_pool_w': out['m_pool_w'], 'm_pool_scale': out['m_pool_scale'], 'm_ffn_w_up': out['m_ffn_w_up'], 'm_ffn_conv_w': out['m_ffn_conv_w'], 'm_ffn_conv_b': out['m_ffn_conv_b'], 'm_ffn_w_down': out['m_ffn_w_down'], 'm_norm_mix_pre': out['m_norm_mix_pre'], 'm_norm_mix_post': out['m_norm_mix_post'], 'm_norm_ffn_pre': out['m_norm_ffn_pre'], 'm_norm_ffn_post': out['m_norm_ffn_post'], 'v_ssd_w_in': out['v_ssd_w_in'], 'v_ssd_conv_w': out['v_ssd_conv_w'], 'v_ssd_conv_b': out['v_ssd_conv_b'], 'v_ssd_dt_bias': out['v_ssd_dt_bias'], 'v_ssd_a_log': out['v_ssd_a_log'], 'v_ssd_d': out['v_ssd_d'], 'v_ssd_norm_w': out['v_ssd_norm_w'], 'v_ssd_w_out': out['v_ssd_w_out'], 'v_pool_w': out['v_pool_w'], 'v_pool_scale': out['v_pool_scale'], 'v_ffn_w_up': out['v_ffn_w_up'], 'v_ffn_conv_w': out['v_ffn_conv_w'], 'v_ffn_conv_b': out['v_ffn_conv_b'], 'v_ffn_w_down': out['v_ffn_w_down'], 'v_norm_mix_pre': out['v_norm_mix_pre'], 'v_norm_mix_post': out['v_norm_mix_post'], 'v_norm_ffn_pre': out['v_norm_ffn_pre'], 'v_norm_ffn_post': out['v_norm_ffn_post']}


def _loss(weights, diff, rest, loss_target):
    with _jax.named_scope("forward"):
        args = {**rest, TWIN_DIFF_INPUT: diff, **{k: w.astype(_WEIGHT_DTYPES[k]) for k, w in weights.items()}}
        y = _forward(args)
    with _jax.named_scope("loss_head"):
        err = _jnp.square(y.astype(_jnp.float32) - loss_target)
        return 0.5 * _jnp.sum(_jnp.mean(err, axis=-1)) if err.ndim else 0.5 * err


def _adamw(w, g, m, v):
    m = ADAM_B1 * m + (1.0 - ADAM_B1) * g
    v = ADAM_B2 * v + (1.0 - ADAM_B2) * _jnp.square(g)
    m_hat = m / (1.0 - ADAM_B1 ** ADAM_STEP)
    v_hat = v / (1.0 - ADAM_B2 ** ADAM_STEP)
    delta = -ADAM_LR * (m_hat / (_jnp.sqrt(v_hat) + ADAM_EPS) + ADAM_WD * w)
    return delta, m, v


def reference(x, ssd_w_in, ssd_conv_w, ssd_conv_b, ssd_dt_bias, ssd_a_log, ssd_d, ssd_norm_w, ssd_w_out, pool_w, pool_scale, ffn_w_up, ffn_conv_w, ffn_conv_b, ffn_w_down, norm_mix_pre, norm_mix_post, norm_ffn_pre, norm_ffn_post, loss_target, m_ssd_w_in, m_ssd_conv_w, m_ssd_conv_b, m_ssd_dt_bias, m_ssd_a_log, m_ssd_d, m_ssd_norm_w, m_ssd_w_out, m_pool_w, m_pool_scale, m_ffn_w_up, m_ffn_conv_w, m_ffn_conv_b, m_ffn_w_down, m_norm_mix_pre, m_norm_mix_post, m_norm_ffn_pre, m_norm_ffn_post, v_ssd_w_in, v_ssd_conv_w, v_ssd_conv_b, v_ssd_dt_bias, v_ssd_a_log, v_ssd_d, v_ssd_norm_w, v_ssd_w_out, v_pool_w, v_pool_scale, v_ffn_w_up, v_ffn_conv_w, v_ffn_conv_b, v_ffn_w_down, v_norm_mix_pre, v_norm_mix_post, v_norm_ffn_pre, v_norm_ffn_post):
    given = dict(x=x, ssd_w_in=ssd_w_in, ssd_conv_w=ssd_conv_w, ssd_conv_b=ssd_conv_b, ssd_dt_bias=ssd_dt_bias, ssd_a_log=ssd_a_log, ssd_d=ssd_d, ssd_norm_w=ssd_norm_w, ssd_w_out=ssd_w_out, pool_w=pool_w, pool_scale=pool_scale, ffn_w_up=ffn_w_up, ffn_conv_w=ffn_conv_w, ffn_conv_b=ffn_conv_b, ffn_w_down=ffn_w_down, norm_mix_pre=norm_mix_pre, norm_mix_post=norm_mix_post, norm_ffn_pre=norm_ffn_pre, norm_ffn_post=norm_ffn_post, loss_target=loss_target, m_ssd_w_in=m_ssd_w_in, m_ssd_conv_w=m_ssd_conv_w, m_ssd_conv_b=m_ssd_conv_b, m_ssd_dt_bias=m_ssd_dt_bias, m_ssd_a_log=m_ssd_a_log, m_ssd_d=m_ssd_d, m_ssd_norm_w=m_ssd_norm_w, m_ssd_w_out=m_ssd_w_out, m_pool_w=m_pool_w, m_pool_scale=m_pool_scale, m_ffn_w_up=m_ffn_w_up, m_ffn_conv_w=m_ffn_conv_w, m_ffn_conv_b=m_ffn_conv_b, m_ffn_w_down=m_ffn_w_down, m_norm_mix_pre=m_norm_mix_pre, m_norm_mix_post=m_norm_mix_post, m_norm_ffn_pre=m_norm_ffn_pre, m_norm_ffn_post=m_norm_ffn_post, v_ssd_w_in=v_ssd_w_in, v_ssd_conv_w=v_ssd_conv_w, v_ssd_conv_b=v_ssd_conv_b, v_ssd_dt_bias=v_ssd_dt_bias, v_ssd_a_log=v_ssd_a_log, v_ssd_d=v_ssd_d, v_ssd_norm_w=v_ssd_norm_w, v_ssd_w_out=v_ssd_w_out, v_pool_w=v_pool_w, v_pool_scale=v_pool_scale, v_ffn_w_up=v_ffn_w_up, v_ffn_conv_w=v_ffn_conv_w, v_ffn_conv_b=v_ffn_conv_b, v_ffn_w_down=v_ffn_w_down, v_norm_mix_pre=v_norm_mix_pre, v_norm_mix_post=v_norm_mix_post, v_norm_ffn_pre=v_norm_ffn_pre, v_norm_ffn_post=v_norm_ffn_post)
    weights = {n: given[n] for n in TWIN_WEIGHTS}
    shared = {n: given[n] for n in SHARED_INPUTS}
    per_example = {n: given[n] for n in ['x']}
    grad_fn = _jax.value_and_grad(_loss, argnums=(0, 1))

    def one_microbatch(ex, loss_target):
        ex = dict(ex)
        diff = ex.pop(TWIN_DIFF_INPUT)
        return grad_fn(weights, diff, {**shared, **ex}, loss_target)

    if N_MICROBATCH == 1:
        loss, (grad_w, grad_x) = one_microbatch(per_example, given["loss_target"])
    else:
        def body(carry, xs):
            loss_sum, grad_sum = carry
            l_k, (gw_k, gx_k) = one_microbatch(xs[0], xs[1])
            with _jax.named_scope("update"):
                return (loss_sum + l_k, _jax.tree.map(_jnp.add, grad_sum, gw_k)), gx_k

        init = (_jnp.zeros((), _jnp.float32), _jax.tree.map(_jnp.zeros_like, weights))
        (loss, grad_w), grad_x = _jax.lax.scan(body, init, (per_example, given["loss_target"]))
    with _jax.named_scope("update"):
        delta_w, new_m, new_v = {}, {}, {}
        for n in TWIN_WEIGHTS:
            delta_w[n], new_m[n], new_v[n] = _adamw(weights[n], grad_w[n], given["m_" + n], given["v_" + n])
    return (loss, grad_x, *[grad_w[n] for n in TWIN_WEIGHTS], *[delta_w[n] for n in TWIN_WEIGHTS],
            *[new_m[n] for n in TWIN_WEIGHTS], *[new_v[n] for n in TWIN_WEIGHTS])
```

```python
import functools

import jax
import jax.numpy as jnp
from jax import lax
from jax.experimental import pallas as pl
from jax.experimental.pallas import tpu as pltpu

F32 = jnp.float32
BF16 = jnp.bfloat16

N_DEV = 8
HEAD_DIM = 64
N_SSD_GROUPS = 4
D_STATE = 128
CHUNK = 128
POOL_WINDOWS = (2, 4, 8, 16)
EPS = 1e-6
LANE = 128
ADAM_LR = 0.001
ADAM_B1 = 0.9
ADAM_B2 = 0.999
ADAM_EPS = 1e-08
ADAM_WD = 0.01
ADAM_STEP = 10
VMEM_LIMIT = 56 * 1024 * 1024


def _pick(n, cands):
    for c in cands:
        if n % c == 0:
            return c
    return n


def _params(sem):
    return pltpu.CompilerParams(dimension_semantics=sem, vmem_limit_bytes=VMEM_LIMIT)


def _sigmoid(x):
    return 1.0 / (1.0 + jnp.exp(-x))


def _silu(x):
    return x * _sigmoid(x)


def _dsilu(x):
    s = _sigmoid(x)
    return s * (1.0 + x * (1.0 - s))


def _shift_down(x, s):
    rows = lax.broadcasted_iota(jnp.int32, x.shape, 0)
    return jnp.where(rows >= s, pltpu.roll(x, s, 0), 0.0)


def _shift_up(x, s):
    n = x.shape[0]
    rows = lax.broadcasted_iota(jnp.int32, x.shape, 0)
    return jnp.where(rows < n - s, pltpu.roll(x, n - s, 0), 0.0)


def _mm(a, b, *, ta=False, tb=False, out_dtype=F32, name="mm"):
    m, k = (a.shape[1], a.shape[0]) if ta else a.shape
    n = b.shape[0] if tb else b.shape[1]
    tm = _pick(m, (512, 256, 128))
    tn = _pick(n, (512, 256, 128))
    tk = _pick(k, (1024, 512, 256, 128))
    nk = k // tk
    a_spec = pl.BlockSpec((tk, tm), lambda i, j, kk: (kk, i)) if ta else pl.BlockSpec((tm, tk), lambda i, j, kk: (i, kk))
    b_spec = pl.BlockSpec((tn, tk), lambda i, j, kk: (j, kk)) if tb else pl.BlockSpec((tk, tn), lambda i, j, kk: (kk, j))
    dims = (((0 if ta else 1,), (1 if tb else 0,)), ((), ()))

    def body(a_ref, b_ref, o_ref, acc_ref):
        kk = pl.program_id(2)
        part = lax.dot_general(a_ref[...].astype(BF16), b_ref[...].astype(BF16), dims, preferred_element_type=F32)

        @pl.when(kk == 0)
        def _():
            acc_ref[...] = part

        @pl.when(kk > 0)
        def _():
            acc_ref[...] += part

        @pl.when(kk == nk - 1)
        def _():
            o_ref[...] = acc_ref[...].astype(out_dtype)

    return pl.pallas_call(
        body,
        name=name,
        grid=(m // tm, n // tn, nk),
        in_specs=[a_spec, b_spec],
        out_specs=pl.BlockSpec((tm, tn), lambda i, j, kk: (i, j)),
        out_shape=jax.ShapeDtypeStruct((m, n), out_dtype),
        scratch_shapes=[pltpu.VMEM((tm, tn), F32)],
        compiler_params=_params(("parallel", "parallel", "arbitrary")),
    )(a, b)


def _rms_fwd(x, w, *, out_dtype, name):
    t, d = x.shape
    tm = _pick(t, (512, 256, 128))

    def body(x_ref, w_ref, o_ref):
        xv = x_ref[...]
        rstd = lax.rsqrt(jnp.mean(xv * xv, axis=-1, keepdims=True) + EPS)
        o_ref[...] = (xv * rstd * w_ref[...]).astype(out_dtype)

    return pl.pallas_call(
        body,
        name=name,
        grid=(t // tm,),
        in_specs=[pl.BlockSpec((tm, d), lambda i: (i, 0)), pl.BlockSpec((1, d), lambda i: (0, 0))],
        out_specs=pl.BlockSpec((tm, d), lambda i: (i, 0)),
        out_shape=jax.ShapeDtypeStruct((t, d), out_dtype),
        compiler_params=_params(("parallel",)),
    )(x, w)


def _res_rms_fwd(x, f, w, *, name):
    t, d = x.shape
    tm = _pick(t, (512, 256, 128))

    def body(x_ref, f_ref, w_ref, o_ref):
        fv = f_ref[...]
        rstd = lax.rsqrt(jnp.mean(fv * fv, axis=-1, keepdims=True) + EPS)
        o_ref[...] = x_ref[...] + fv * rstd * w_ref[...]

    row = pl.BlockSpec((tm, d), lambda i: (i, 0))
    return pl.pallas_call(
        body,
        name=name,
        grid=(t // tm,),
        in_specs=[row, row, pl.BlockSpec((1, d), lambda i: (0, 0))],
        out_specs=row,
        out_shape=jax.ShapeDtypeStruct((t, d), F32),
        compiler_params=_params(("parallel",)),
    )(x, f, w)


def _rms_bwd(x, w, dy, resid, *, name):
    t, d = x.shape
    tm = _pick(t, (512, 256, 128))
    has_res = resid is not None

    def body(*refs):
        if has_res:
            x_ref, w_ref, dy_ref, r_ref, dx_ref, dw_ref = refs
        else:
            x_ref, w_ref, dy_ref, dx_ref, dw_ref = refs
        xv = x_ref[...]
        dyv = dy_ref[...].astype(F32)
        rstd = lax.rsqrt(jnp.mean(xv * xv, axis=-1, keepdims=True) + EPS)
        xn = xv * rstd
        g = dyv * w_ref[...]
        dx = rstd * (g - xn * jnp.mean(g * xn, axis=-1, keepdims=True))
        if has_res:
            dx = dx + r_ref[...]
        dx_ref[...] = dx
        part = jnp.sum(dyv * xn, axis=0, keepdims=True)

        @pl.when(pl.program_id(0) == 0)
        def _():
            dw_ref[...] = part

        @pl.when(pl.program_id(0) > 0)
        def _():
            dw_ref[...] += part

    row = pl.BlockSpec((tm, d), lambda i: (i, 0))
    vec = pl.BlockSpec((1, d), lambda i: (0, 0))
    ins = [x, w, dy] + ([resid] if has_res else [])
    return pl.pallas_call(
        body,
        name=name,
        grid=(t // tm,),
        in_specs=[row, vec, row] + ([row] if has_res else []),
        out_specs=[row, vec],
        out_shape=[jax.ShapeDtypeStruct((t, d), F32), jax.ShapeDtypeStruct((1, d), F32)],
        compiler_params=_params(("arbitrary",)),
    )(*ins)


def _loss_head(y, target, *, name="loss_head"):
    t, d = y.shape
    tm = _pick(t, (512, 256, 128))

    def body(y_ref, t_ref, dy_ref, l_ref):
        err = y_ref[...] - t_ref[...]
        dy_ref[...] = err * (1.0 / d)
        part = jnp.sum(jnp.sum(err * err, axis=-1, keepdims=True), axis=0, keepdims=True) * (0.5 / d)
        part = jnp.broadcast_to(part, (1, LANE))

        @pl.when(pl.program_id(0) == 0)
        def _():
            l_ref[...] = part

        @pl.when(pl.program_id(0) > 0)
        def _():
            l_ref[...] += part

    row = pl.BlockSpec((tm, d), lambda i: (i, 0))
    dy, l = pl.pallas_call(
        body,
        name=name,
        grid=(t // tm,),
        in_specs=[row, row],
        out_specs=[row, pl.BlockSpec((1, LANE), lambda i: (0, 0))],
        out_shape=[jax.ShapeDtypeStruct((t, d), F32), jax.ShapeDtypeStruct((1, LANE), F32)],
        compiler_params=_params(("arbitrary",)),
    )(y, target)
    return l[0, 0], dy


def _conv_taps(h, w_ref, k_taps):
    out = h * w_ref[k_taps - 1:k_taps, :]
    for k in range(k_taps - 1):
        out = out + _shift_down(h, k_taps - 1 - k) * w_ref[k:k + 1, :]
    return out


def _conv_taps_bwd(h, dhc, w_ref, k_taps):
    dh = dhc * w_ref[k_taps - 1:k_taps, :]
    dws = []
    for k in range(k_taps - 1):
        s = k_taps - 1 - k
        dh = dh + _shift_up(dhc, s) * w_ref[k:k + 1, :]
        dws.append(jnp.sum(dhc * _shift_down(h, s), axis=0, keepdims=True))
    dws.append(jnp.sum(dhc * h, axis=0, keepdims=True))
    return dh, jnp.concatenate(dws, axis=0)


FFN_TC = 256


def _interleave(w, tc=FFN_TC):
    f = w.shape[-1] // 2
    lead = w.shape[:-1]
    return jnp.swapaxes(w.reshape(lead + (2, f // tc, tc)), -3, -2).reshape(lead + (2 * f,))


def _deinterleave(w, tc=FFN_TC):
    f = w.shape[-1] // 2
    lead = w.shape[:-1]
    return jnp.swapaxes(w.reshape(lead + (f // tc, 2, tc)), -3, -2).reshape(lead + (2 * f,))


def _ffn_act_fwd(h, conv_w, conv_b, n_seq, *, name):
    t, f2 = h.shape
    seq = t // n_seq
    tc = FFN_TC
    nj = f2 // (2 * tc)
    k_taps = conv_w.shape[0]

    def body(h_ref, w_ref, b_ref, o_ref):
        hc = _conv_taps(h_ref[...].astype(F32), w_ref, k_taps) + b_ref[...]
        o_ref[...] = (_silu(hc[:, :tc]) * hc[:, tc:]).astype(BF16)

    return pl.pallas_call(
        body,
        name=name,
        grid=(n_seq, nj),
        in_specs=[
            pl.BlockSpec((seq, 2 * tc), lambda b, j: (b, j)),
            pl.BlockSpec((k_taps, 2 * tc), lambda b, j: (0, j)),
            pl.BlockSpec((1, 2 * tc), lambda b, j: (0, j)),
        ],
        out_specs=pl.BlockSpec((seq, tc), lambda b, j: (b, j)),
        out_shape=jax.ShapeDtypeStruct((t, f2 // 2), BF16),
        compiler_params=_params(("parallel", "parallel")),
    )(h, conv_w, conv_b)


def _ffn_act_bwd(h, conv_w, conv_b, da, n_seq, *, name):
    t, f2 = h.shape
    seq = t // n_seq
    tc = FFN_TC
    nj = f2 // (2 * tc)
    k_taps = conv_w.shape[0]

    def body(h_ref, w_ref, b_ref, da_ref, dh_ref, dw_ref, db_ref):
        hv = h_ref[...].astype(F32)
        hc = _conv_taps(hv, w_ref, k_taps) + b_ref[...]
        gate, val = hc[:, :tc], hc[:, tc:]
        dav = da_ref[...].astype(F32)
        dhc = jnp.concatenate([dav * val * _dsilu(gate), dav * _silu(gate)], axis=1)
        dh, dw = _conv_taps_bwd(hv, dhc, w_ref, k_taps)
        dh_ref[...] = dh.astype(BF16)
        db = jnp.sum(dhc, axis=0, keepdims=True)

        @pl.when(pl.program_id(1) == 0)
        def _():
            dw_ref[...] = dw
            db_ref[...] = db

        @pl.when(pl.program_id(1) > 0)
        def _():
            dw_ref[...] += dw
            db_ref[...] += db

    return pl.pallas_call(
        body,
        name=name,
        grid=(nj, n_seq),
        in_specs=[
            pl.BlockSpec((seq, 2 * tc), lambda j, b: (b, j)),
            pl.BlockSpec((k_taps, 2 * tc), lambda j, b: (0, j)),
            pl.BlockSpec((1, 2 * tc), lambda j, b: (0, j)),
            pl.BlockSpec((seq, tc), lambda j, b: (b, j)),
        ],
        out_specs=[
            pl.BlockSpec((seq, 2 * tc), lambda j, b: (b, j)),
            pl.BlockSpec((k_taps, 2 * tc), lambda j, b: (0, j)),
            pl.BlockSpec((1, 2 * tc), lambda j, b: (0, j)),
        ],
        out_shape=[
            jax.ShapeDtypeStruct((t, f2), BF16),
            jax.ShapeDtypeStruct((k_taps, f2), F32),
            jax.ShapeDtypeStruct((1, f2), F32),
        ],
        compiler_params=_params(("parallel", "arbitrary")),
    )(h, conv_w, conv_b, da)


def _window_mixed(u, window):
    s = u
    step = 1
    while step < window:
        s = s + _shift_down(s, step)
        step *= 2
    rows = lax.broadcasted_iota(jnp.int32, u.shape, 0)
    inv_cnt = 1.0 / jnp.minimum(rows + 1, window).astype(F32)
    return s * inv_cnt - u, inv_cnt


def _window_mixed_bwd(dmixed, inv_cnt, window):
    r = dmixed * inv_cnt
    s = r
    step = 1
    while step < window:
        s = s + _shift_up(s, step)
        step *= 2
    return s - dmixed


def _pool_fwd(u, w, scale, n_seq, *, name):
    t, d = u.shape
    seq = t // n_seq
    n_g, dg, _ = w.shape

    def body(u_ref, w_ref, s_ref, o_ref):
        for k, window in enumerate(POOL_WINDOWS):
            @pl.when(pl.program_id(1) == k)
            def _(window=window):
                mixed, _ = _window_mixed(u_ref[...], window)
                pre = jnp.dot(mixed.astype(BF16), w_ref[0].astype(BF16), preferred_element_type=F32)
                o_ref[...] = pre * s_ref[...]

    return pl.pallas_call(
        body,
        name=name,
        grid=(n_seq, n_g),
        in_specs=[
            pl.BlockSpec((seq, dg), lambda b, g: (b, g)),
            pl.BlockSpec((1, dg, dg), lambda b, g: (g, 0, 0)),
            pl.BlockSpec((1, dg), lambda b, g: (0, g)),
        ],
        out_specs=pl.BlockSpec((seq, dg), lambda b, g: (b, g)),
        out_shape=jax.ShapeDtypeStruct((t, d), F32),
        compiler_params=_params(("parallel", "parallel")),
    )(u, w, scale)


def _pool_bwd(u, w, scale, dout, n_seq, *, name):
    t, d = u.shape
    seq = t // n_seq
    n_g, dg, _ = w.shape

    def body(u_ref, w_ref, s_ref, do_ref, du_ref, dw_ref, ds_ref):
        group = pl.program_id(0)
        first = pl.program_id(1) == 0
        for k, window in enumerate(POOL_WINDOWS):
            @pl.when(group == k)
            def _(window=window):
                mixed, inv_cnt = _window_mixed(u_ref[...], window)
                mixed_b = mixed.astype(BF16)
                w_b = w_ref[0].astype(BF16)
                dov = do_ref[...]
                pre = jnp.dot(mixed_b, w_b, preferred_element_type=F32)
                dsc = jnp.sum(dov * pre, axis=0, keepdims=True)
                dpre = (dov * s_ref[...]).astype(BF16)
                dw = lax.dot_general(mixed_b, dpre, (((0,), (0,)), ((), ())), preferred_element_type=F32)
                dmixed = lax.dot_general(dpre, w_b, (((1,), (1,)), ((), ())), preferred_element_type=F32)
                du_ref[...] = _window_mixed_bwd(dmixed, inv_cnt, window)

                @pl.when(first)
                def _():
                    dw_ref[0] = dw
                    ds_ref[...] = dsc

                @pl.when(jnp.logical_not(first))
                def _():
                    dw_ref[0] += dw
                    ds_ref[...] += dsc

    return pl.pallas_call(
        body,
        name=name,
        grid=(n_g, n_seq),
        in_specs=[
            pl.BlockSpec((seq, dg), lambda g, b: (b, g)),
            pl.BlockSpec((1, dg, dg), lambda g, b: (g, 0, 0)),
            pl.BlockSpec((1, dg), lambda g, b: (0, g)),
            pl.BlockSpec((seq, dg), lambda g, b: (b, g)),
        ],
        out_specs=[
            pl.BlockSpec((seq, dg), lambda g, b: (b, g)),
            pl.BlockSpec((1, dg, dg), lambda g, b: (g, 0, 0)),
            pl.BlockSpec((1, dg), lambda g, b: (0, g)),
        ],
        out_shape=[
            jax.ShapeDtypeStruct((t, d), F32),
            jax.ShapeDtypeStruct((n_g, dg, dg), F32),
            jax.ShapeDtypeStruct((1, d), F32),
        ],
        compiler_params=_params(("parallel", "arbitrary")),
    )(u, w, scale, dout)


def _adamw(w, g, m, v, *, name):
    shape = w.shape
    c = shape[-1]
    r = w.size // c
    tm = _pick(r, (512, 256, 128, 64, 32, 16, 8))

    def body(w_ref, g_ref, m_ref, v_ref, d_ref, nm_ref, nv_ref):
        gv = g_ref[...]
        nm = ADAM_B1 * m_ref[...] + (1.0 - ADAM_B1) * gv
        nv = ADAM_B2 * v_ref[...] + (1.0 - ADAM_B2) * (gv * gv)
        m_hat = nm / (1.0 - ADAM_B1 ** ADAM_STEP)
        v_hat = nv / (1.0 - ADAM_B2 ** ADAM_STEP)
        d_ref[...] = -ADAM_LR * (m_hat / (jnp.sqrt(v_hat) + ADAM_EPS) + ADAM_WD * w_ref[...])
        nm_ref[...] = nm
        nv_ref[...] = nv

    blk = pl.BlockSpec((tm, c), lambda i: (i, 0))
    out = jax.ShapeDtypeStruct((r, c), F32)
    res = pl.pallas_call(
        body,
        name=name,
        grid=(r // tm,),
        in_specs=[blk] * 4,
        out_specs=[blk] * 3,
        out_shape=[out] * 3,
        compiler_params=_params(("parallel",)),
    )(w.reshape(r, c), g.reshape(r, c), m.reshape(r, c), v.reshape(r, c))
    return tuple(a.reshape(shape) for a in res)


CONV_TC = 256


def _ssd_conv_fwd(proj, col0, n_cols, conv_w, conv_b, n_seq, *, name):
    t = proj.shape[0]
    seq = t // n_seq
    tc = CONV_TC
    off = col0 // tc
    k_taps = conv_w.shape[0]

    def body(h_ref, w_ref, b_ref, o_ref):
        o_ref[...] = _silu(_conv_taps(h_ref[...], w_ref, k_taps) + b_ref[...])

    return pl.pallas_call(
        body,
        name=name,
        grid=(n_seq, n_cols // tc),
        in_specs=[
            pl.BlockSpec((seq, tc), lambda b, j: (b, j + off)),
            pl.BlockSpec((k_taps, tc), lambda b, j: (0, j)),
            pl.BlockSpec((1, tc), lambda b, j: (0, j)),
        ],
        out_specs=pl.BlockSpec((seq, tc), lambda b, j: (b, j)),
        out_shape=jax.ShapeDtypeStruct((t, n_cols), F32),
        compiler_params=_params(("parallel", "parallel")),
    )(proj, conv_w, conv_b)


def _ssd_conv_bwd(proj, col0, n_cols, conv_w, conv_b, dact, n_seq, *, name):
    t = proj.shape[0]
    seq = t // n_seq
    tc = CONV_TC
    off = col0 // tc
    k_taps = conv_w.shape[0]

    def body(h_ref, w_ref, b_ref, da_ref, dh_ref, dw_ref, db_ref):
        hv = h_ref[...]
        dhc = da_ref[...] * _dsilu(_conv_taps(hv, w_ref, k_taps) + b_ref[...])
        dh, dw = _conv_taps_bwd(hv, dhc, w_ref, k_taps)
        dh_ref[...] = dh.astype(BF16)
        db = jnp.sum(dhc, axis=0, keepdims=True)

        @pl.when(pl.program_id(1) == 0)
        def _():
            dw_ref[...] = dw
            db_ref[...] = db

        @pl.when(pl.program_id(1) > 0)
        def _():
            dw_ref[...] += dw
            db_ref[...] += db

    return pl.pallas_call(
        body,
        name=name,
        grid=(n_cols // tc, n_seq),
        in_specs=[
            pl.BlockSpec((seq, tc), lambda j, b: (b, j + off)),
            pl.BlockSpec((k_taps, tc), lambda j, b: (0, j)),
            pl.BlockSpec((1, tc), lambda j, b: (0, j)),
            pl.BlockSpec((seq, tc), lambda j, b: (b, j)),
        ],
        out_specs=[
            pl.BlockSpec((seq, tc), lambda j, b: (b, j)),
            pl.BlockSpec((k_taps, tc), lambda j, b: (0, j)),
            pl.BlockSpec((1, tc), lambda j, b: (0, j)),
        ],
        out_shape=[
            jax.ShapeDtypeStruct((t, n_cols), BF16),
            jax.ShapeDtypeStruct((k_taps, n_cols), F32),
            jax.ShapeDtypeStruct((1, n_cols), F32),
        ],
        compiler_params=_params(("parallel", "arbitrary")),
    )(proj, conv_w, conv_b, dact)


def _softplus(x):
    return jnp.maximum(x, 0.0) + jnp.log(1.0 + jnp.exp(-jnp.abs(x)))


def _chunk_decay(dtraw, bias, alog):
    q = dtraw.shape[0]
    dt = _softplus(dtraw + bias)
    a = -jnp.exp(alog)
    rows = lax.broadcasted_iota(jnp.int32, (q, q), 0)
    cols = lax.broadcasted_iota(jnp.int32, (q, q), 1)
    lower = rows >= cols
    acum = jnp.dot(lower.astype(F32), dt * a, precision=lax.Precision.HIGHEST, preferred_element_type=F32)
    return dt, a, acum, acum.T, lower


def _lane_spread(vals, width):
    r = len(vals)
    return jnp.concatenate([jnp.broadcast_to(v, (v.shape[0], width)) for v in vals], axis=1)


def _head_pad(v, r_heads):
    lead = v.shape[:-1]
    vg = v.reshape(lead + (N_SSD_GROUPS, r_heads))
    vg = jnp.pad(vg, [(0, 0)] * len(lead) + [(0, 0), (0, LANE - r_heads)])
    out = vg.reshape(lead + (N_SSD_GROUPS * LANE,))
    return out[None] if out.ndim == 1 else out


def _head_unpad(v, r_heads):
    lead = v.shape[:-1]
    out = v.reshape(lead + (N_SSD_GROUPS, LANE))[..., :r_heads].reshape(lead + (N_SSD_GROUPS * r_heads,))
    return out[0] if (len(lead) == 1 and lead[0] == 1) else out


def _ssd_w_in_layout(w_in, d_inner, d_xbc, r_heads):
    main = w_in[:, :d_inner + d_xbc]
    return jnp.concatenate([main, _head_pad(w_in[:, d_inner + d_xbc:], r_heads)], axis=1)


def _ssd_w_in_unlayout(w, d_inner, d_xbc, r_heads):
    main = w[:, :d_inner + d_xbc]
    return jnp.concatenate([main, _head_unpad(w[:, d_inner + d_xbc:], r_heads)], axis=1)


def _ssd_dims(proj, xbc):
    d_xbc = xbc.shape[1]
    d_inner = d_xbc - 2 * N_SSD_GROUPS * D_STATE
    gw = d_inner // N_SSD_GROUPS
    return d_inner, d_xbc, gw, gw // HEAD_DIM


def _ssd_fwd(proj, xbc, bias_p, alog_p, dskip_p, norm_w, n_seq, *, name):
    t = proj.shape[0]
    d_inner, d_xbc, gw, r_heads = _ssd_dims(proj, xbc)
    q, n, n_g, p = CHUNK, D_STATE, N_SSD_GROUPS, HEAD_DIM
    seq = t // n_seq
    nc = seq // q
    dt_blk0 = (d_inner + d_xbc) // LANE

    def body(x_ref, b_ref, c_ref, z_ref, dtr_ref, bias_ref, alog_ref, dsk_ref, nw_ref, yn_ref, y_ref, hs_ref, h_scr):
        @pl.when(pl.program_id(2) == 0)
        def _():
            h_scr[...] = jnp.zeros_like(h_scr)

        dt, a, acum, acum_t, lower = _chunk_decay(dtr_ref[...], bias_ref[...], alog_ref[...])
        x = x_ref[...]
        bb = b_ref[...].astype(BF16)
        cb = c_ref[...].astype(BF16)
        g_mat = lax.dot_general(cb, bb, (((1,), (1,)), ((), ())), preferred_element_type=F32)
        h_prev = h_scr[...]
        hs_ref[...] = h_prev
        c_h = jnp.dot(cb, h_prev.astype(BF16), preferred_element_type=F32)
        a_last = acum[q - 1:q, :]
        dsk = dsk_ref[...]
        ys, xds = [], []
        for h in range(r_heads):
            ac = acum[:, h:h + 1]
            decay = jnp.exp(jnp.where(lower, ac - acum_t[h:h + 1, :], -jnp.inf))
            xh = x[:, h * p:(h + 1) * p]
            xdt = xh * dt[:, h:h + 1]
            y_diag = jnp.dot((g_mat * decay).astype(BF16), xdt.astype(BF16), preferred_element_type=F32)
            ys.append(y_diag + jnp.exp(ac) * c_h[:, h * p:(h + 1) * p] + dsk[:, h:h + 1] * xh)
            xds.append(xdt * jnp.exp(a_last[:, h:h + 1] - ac))
        y = jnp.concatenate(ys, axis=1)
        xd = jnp.concatenate(xds, axis=1)
        states = lax.dot_general(bb, xd.astype(BF16), (((0,), (0,)), ((), ())), preferred_element_type=F32)
        cd = _lane_spread([jnp.exp(a_last[:, h:h + 1]) for h in range(r_heads)], p)
        h_scr[...] = h_prev * cd + states
        y_ref[...] = y
        gated = y * _silu(z_ref[...])
        rstd = lax.rsqrt(jnp.mean(gated * gated, axis=-1, keepdims=True) + EPS)
        yn_ref[...] = (gated * rstd * nw_ref[...]).astype(BF16)

    row = lambda b, g, c: b * nc + c
    vec = pl.BlockSpec((1, LANE), lambda b, g, c: (0, g))
    return pl.pallas_call(
        body,
        name=name,
        grid=(n_seq, n_g, nc),
        in_specs=[
            pl.BlockSpec((q, gw), lambda b, g, c: (row(b, g, c), g)),
            pl.BlockSpec((q, n), lambda b, g, c: (row(b, g, c), d_inner // n + g)),
            pl.BlockSpec((q, n), lambda b, g, c: (row(b, g, c), d_inner // n + n_g + g)),
            pl.BlockSpec((q, gw), lambda b, g, c: (row(b, g, c), g)),
            pl.BlockSpec((q, LANE), lambda b, g, c: (row(b, g, c), dt_blk0 + g)),
            vec, vec, vec,
            pl.BlockSpec((1, gw), lambda b, g, c: (0, g)),
        ],
        out_specs=[
            pl.BlockSpec((q, gw), lambda b, g, c: (row(b, g, c), g)),
            pl.BlockSpec((q, gw), lambda b, g, c: (row(b, g, c), g)),
            pl.BlockSpec((n, gw), lambda b, g, c: (row(b, g, c), g)),
        ],
        out_shape=[
            jax.ShapeDtypeStruct((t, d_inner), BF16),
            jax.ShapeDtypeStruct((t, d_inner), F32),
            jax.ShapeDtypeStruct((n_seq * nc * n, d_inner), F32),
        ],
        scratch_shapes=[pltpu.VMEM((n, gw), F32)],
        compiler_params=_params(("parallel", "parallel", "arbitrary")),
    )(xbc, xbc, xbc, proj, proj, bias_p, alog_p, dskip_p, norm_w)


def _ssd_bwd(proj, xbc, hs, y, dyn, bias_p, alog_p, dskip_p, norm_w, n_seq, *, name):
    t = proj.shape[0]
    d_inner, d_xbc, gw, r_heads = _ssd_dims(proj, xbc)
    q, n, n_g, p = CHUNK, D_STATE, N_SSD_GROUPS, HEAD_DIM
    seq = t // n_seq
    nc = seq // q
    dt_blk0 = (d_inner + d_xbc) // LANE

    def body(x_ref, b_ref, c_ref, z_ref, dtr_ref, bias_ref, alog_ref, dsk_ref, nw_ref, hs_ref, y_ref, dyn_ref,
             dx_ref, db_ref, dc_ref, dz_ref, ddtr_ref, dnw_ref, dbias_ref, dalog_ref, ddsk_ref, dh_scr):
        first = jnp.logical_and(pl.program_id(1) == 0, pl.program_id(2) == 0)

        @pl.when(pl.program_id(2) == 0)
        def _():
            dh_scr[...] = jnp.zeros_like(dh_scr)

        dtraw = dtr_ref[...]
        dt, a, acum, acum_t, lower = _chunk_decay(dtraw, bias_ref[...], alog_ref[...])
        x = x_ref[...]
        bb = b_ref[...].astype(BF16)
        cb = c_ref[...].astype(BF16)
        g_mat = lax.dot_general(cb, bb, (((1,), (1,)), ((), ())), preferred_element_type=F32)

        yv = y_ref[...]
        z = z_ref[...]
        sz = _silu(z)
        gated = yv * sz
        rstd = lax.rsqrt(jnp.mean(gated * gated, axis=-1, keepdims=True) + EPS)
        gn = gated * rstd
        dynv = dyn_ref[...]
        gwt = dynv * nw_ref[...]
        dgated = rstd * (gwt - gn * jnp.mean(gwt * gn, axis=-1, keepdims=True))
        dnw = jnp.sum(dynv * gn, axis=0, keepdims=True)
        dy = dgated * sz
        dz_ref[...] = dgated * yv * _dsilu(z)

        h_prev = hs_ref[...]
        h_prev_b = h_prev.astype(BF16)
        ds = dh_scr[...]
        ds_b = ds.astype(BF16)
        c_h = jnp.dot(cb, h_prev_b, preferred_element_type=F32)
        dxd = jnp.dot(bb, ds_b, preferred_element_type=F32)
        a_last = acum[q - 1:q, :]
        dsk = dsk_ref[...]
        lane = lax.broadcasted_iota(jnp.int32, (q, LANE), 1)
        sub = lax.broadcasted_iota(jnp.int32, (LANE, q), 0)
        lane1 = lax.broadcasted_iota(jnp.int32, (1, LANE), 1)
        dacum_cols = jnp.zeros((q, LANE), F32)
        dacum_rows = jnp.zeros((LANE, q), F32)
        ddt_cols = jnp.zeros((q, LANE), F32)
        dlast = jnp.zeros((1, LANE), F32)
        ddsk = jnp.zeros((1, LANE), F32)
        dg = jnp.zeros((q, q), F32)
        dxs, dyes, xds, cds = [], [], [], []
        for h in range(r_heads):
            hsl = slice(h * p, (h + 1) * p)
            ac = acum[:, h:h + 1]
            decay = jnp.exp(jnp.where(lower, ac - acum_t[h:h + 1, :], -jnp.inf))
            m_mat = g_mat * decay
            xh = x[:, hsl]
            dth = dt[:, h:h + 1]
            xdt = xh * dth
            xdt_b = xdt.astype(BF16)
            dyh = dy[:, hsl]
            dyh_b = dyh.astype(BF16)
            ea = jnp.exp(ac)
            al = a_last[:, h:h + 1]
            dte = jnp.exp(al - ac)
            cdh = jnp.exp(al)
            dm = lax.dot_general(dyh_b, xdt_b, (((1,), (1,)), ((), ())), preferred_element_type=F32)
            w_mat = dm * m_mat
            dac = jnp.sum(w_mat, axis=1, keepdims=True)
            dar = jnp.sum(w_mat, axis=0, keepdims=True)
            dg = dg + dm * decay
            dxdt = lax.dot_general(m_mat.astype(BF16), dyh_b, (((0,), (0,)), ((), ())), preferred_element_type=F32)
            dac = dac + jnp.sum(dyh * (ea * c_h[:, hsl]), axis=1, keepdims=True)
            dyes.append(dyh * ea)
            dxd_h = dxd[:, hsl]
            dxdt = dxdt + dxd_h * dte
            tt = jnp.sum(dxd_h * xdt, axis=1, keepdims=True) * dte
            dac = dac - tt
            dl = jnp.sum(tt, axis=0, keepdims=True)
            dcd = jnp.sum(jnp.sum(ds[:, hsl] * h_prev[:, hsl], axis=1, keepdims=True), axis=0, keepdims=True)
            dl = dl + dcd * cdh
            dxs.append(dxdt * dth + dsk[:, h:h + 1] * dyh)
            ddt_h = jnp.sum(dxdt * xh, axis=1, keepdims=True)
            dd_h = jnp.sum(jnp.sum(dyh * xh, axis=1, keepdims=True), axis=0, keepdims=True)
            xds.append(xdt * dte)
            cds.append(cdh)
            dacum_cols = dacum_cols + jnp.where(lane == h, dac, 0.0)
            dacum_rows = dacum_rows + jnp.where(sub == h, dar, 0.0)
            ddt_cols = ddt_cols + jnp.where(lane == h, ddt_h, 0.0)
            dlast = dlast + jnp.where(lane1 == h, dl, 0.0)
            ddsk = ddsk + jnp.where(lane1 == h, dd_h, 0.0)
        dye_b = jnp.concatenate(dyes, axis=1).astype(BF16)
        xd_b = jnp.concatenate(xds, axis=1).astype(BF16)
        dg_b = dg.astype(BF16)
        dx_ref[...] = jnp.concatenate(dxs, axis=1)
        dc_ref[...] = (jnp.dot(dg_b, bb, preferred_element_type=F32)
                       + lax.dot_general(dye_b, h_prev_b, (((1,), (1,)), ((), ())), preferred_element_type=F32))
        db_ref[...] = (lax.dot_general(dg_b, cb, (((0,), (0,)), ((), ())), preferred_element_type=F32)
                       + lax.dot_general(xd_b, ds_b, (((1,), (1,)), ((), ())), preferred_element_type=F32))
        dh_scr[...] = ds * _lane_spread(cds, p) + lax.dot_general(cb, dye_b, (((0,), (0,)), ((), ())),
                                                                    preferred_element_type=F32)
        rows_q = lax.broadcasted_iota(jnp.int32, (q, LANE), 0)
        dacum = dacum_cols - dacum_rows.T + jnp.where(rows_q == q - 1, dlast, 0.0)
        upper = lax.broadcasted_iota(jnp.int32, (q, q), 0) <= lax.broadcasted_iota(jnp.int32, (q, q), 1)
        dadt = jnp.dot(upper.astype(F32), dacum, precision=lax.Precision.HIGHEST, preferred_element_type=F32)
        ddt = dadt * a + ddt_cols
        ddtr = ddt * _sigmoid(dtraw + bias_ref[...])
        ddtr_ref[...] = ddtr
        dbias = jnp.sum(ddtr, axis=0, keepdims=True)
        dalog = jnp.sum(dadt * dt, axis=0, keepdims=True) * a

        @pl.when(first)
        def _():
            dnw_ref[...] = dnw
            dbias_ref[...] = dbias
            dalog_ref[...] = dalog
            ddsk_ref[...] = ddsk

        @pl.when(jnp.logical_not(first))
        def _():
            dnw_ref[...] += dnw
            dbias_ref[...] += dbias
            dalog_ref[...] += dalog
            ddsk_ref[...] += ddsk

    row = lambda g, b, c: b * nc + (nc - 1 - c)
    vec = pl.BlockSpec((1, LANE), lambda g, b, c: (0, g))
    wide = pl.BlockSpec((q, gw), lambda g, b, c: (row(g, b, c), g))
    narrow = pl.BlockSpec((q, n), lambda g, b, c: (row(g, b, c), g))
    return pl.pallas_call(
        body,
        name=name,
        grid=(n_g, n_seq, nc),
        in_specs=[
            wide,
            pl.BlockSpec((q, n), lambda g, b, c: (row(g, b, c), d_inner // n + g)),
            pl.BlockSpec((q, n), lambda g, b, c: (row(g, b, c), d_inner // n + n_g + g)),
            wide,
            pl.BlockSpec((q, LANE), lambda g, b, c: (row(g, b, c), dt_blk0 + g)),
            vec, vec, vec,
            pl.BlockSpec((1, gw), lambda g, b, c: (0, g)),
            pl.BlockSpec((n, gw), lambda g, b, c: (row(g, b, c), g)),
            wide, wide,
        ],
        out_specs=[
            wide, narrow, narrow, wide, narrow,
            pl.BlockSpec((1, gw), lambda g, b, c: (0, g)),
            vec, vec, vec,
        ],
        out_shape=[
            jax.ShapeDtypeStruct((t, d_inner), F32),
            jax.ShapeDtypeStruct((t, n_g * n), F32),
            jax.ShapeDtypeStruct((t, n_g * n), F32),
            jax.ShapeDtypeStruct((t, d_inner), F32),
            jax.ShapeDtypeStruct((t, n_g * LANE), F32),
            jax.ShapeDtypeStruct((1, d_inner), F32),
            jax.ShapeDtypeStruct((1, n_g * LANE), F32),
            jax.ShapeDtypeStruct((1, n_g * LANE), F32),
            jax.ShapeDtypeStruct((1, n_g * LANE), F32),
        ],
        scratch_shapes=[pltpu.VMEM((n, gw), F32)],
        compiler_params=_params(("parallel", "arbitrary", "arbitrary")),
    )(xbc, xbc, xbc, proj, proj, bias_p, alog_p, dskip_p, norm_w, hs, y, dyn)


MESH_IDS = pl.DeviceIdType.MESH
ANY = pl.BlockSpec(memory_space=pl.ANY)


def _my_index():
    return 4 * lax.axis_index("x") + 2 * lax.axis_index("y") + lax.axis_index("c")


def _all_gather(shard, *, name):
    def body(x_ref, out_ref, send_sems, recv_sems, local_sem):
        x, y, c = lax.axis_index("x"), lax.axis_index("y"), lax.axis_index("c")
        me, sibling = (x, y, c), (x, y, 1 - c)
        chips = [(1 - x, y), (x, 1 - y), (1 - x, 1 - y)]

        def blk(px, py, pc):
            return out_ref.at[4 * px + 2 * py + pc]

        def copy(k, block, to, src=None):
            return pltpu.make_async_remote_copy(
                src_ref=blk(*block) if src is None else src, dst_ref=blk(*block),
                send_sem=send_sems.at[k], recv_sem=recv_sems.at[k], device_id=to, device_id_type=MESH_IDS)

        mine = pltpu.make_async_copy(x_ref, blk(*me), local_sem)
        mine.start()
        first = [copy(0, me, sibling, src=x_ref)]
        first += [copy(1 + j, me, (*chip, c), src=x_ref) for j, chip in enumerate(chips)]
        for cp in first:
            cp.start()
        passed = [copy(4 + j, (*chip, c), sibling) for j, chip in enumerate(chips)]
        for j, chip in enumerate(chips):
            copy(1 + j, (*chip, c), me).wait_recv()
            passed[j].start()
        copy(0, sibling, me).wait_recv()
        for j, chip in enumerate(chips):
            copy(4 + j, (*chip, 1 - c), me).wait_recv()
        for cp in first + passed:
            cp.wait_send()
        mine.wait()

    return pl.pallas_call(
        body,
        name=name,
        in_specs=[ANY],
        out_specs=ANY,
        out_shape=jax.ShapeDtypeStruct((N_DEV,) + shard.shape, shard.dtype),
        scratch_shapes=[pltpu.SemaphoreType.DMA((7,)), pltpu.SemaphoreType.DMA((7,)), pltpu.SemaphoreType.DMA],
    )(shard)


def _exchange(full, *, name):
    def body(in_ref, out_ref, send_sems, recv_sems, local_sem):
        x, y, c = lax.axis_index("x"), lax.axis_index("y"), lax.axis_index("c")
        me = 4 * x + 2 * y + c
        mine = pltpu.make_async_copy(in_ref.at[me], out_ref.at[me], local_sem)
        mine.start()
        copies = []
        for k in range(1, N_DEV):
            px = 1 - x if k & 4 else x
            py = 1 - y if k & 2 else y
            pc = 1 - c if k & 1 else c
            peer = 4 * px + 2 * py + pc
            copies.append((
                pltpu.make_async_remote_copy(
                    src_ref=in_ref.at[peer], dst_ref=out_ref.at[me], send_sem=send_sems.at[k - 1],
                    recv_sem=recv_sems.at[k - 1], device_id=(px, py, pc), device_id_type=MESH_IDS),
                pltpu.make_async_remote_copy(
                    src_ref=in_ref.at[peer], dst_ref=out_ref.at[peer], send_sem=send_sems.at[k - 1],
                    recv_sem=recv_sems.at[k - 1], device_id=(px, py, pc), device_id_type=MESH_IDS)))
        for send, _ in copies:
            send.start()
        for send, recv in copies:
            recv.wait_recv()
            send.wait_send()
        mine.wait()

    return pl.pallas_call(
        body,
        name=name,
        in_specs=[ANY],
        out_specs=ANY,
        out_shape=jax.ShapeDtypeStruct(full.shape, full.dtype),
        scratch_shapes=[pltpu.SemaphoreType.DMA((7,)), pltpu.SemaphoreType.DMA((7,)), pltpu.SemaphoreType.DMA],
    )(full)


def _sum_slots(parts, *, name):
    n, r, c = parts.shape
    tm = _pick(r, (256, 128, 64, 32, 16, 8))

    def body(p_ref, o_ref):
        acc = p_ref[0].astype(F32)
        for s in range(1, n):
            acc = acc + p_ref[s].astype(F32)
        o_ref[...] = acc

    return pl.pallas_call(
        body,
        name=name,
        grid=(r // tm,),
        in_specs=[pl.BlockSpec((n, tm, c), lambda i: (0, i, 0))],
        out_specs=pl.BlockSpec((tm, c), lambda i: (i, 0)),
        out_shape=jax.ShapeDtypeStruct((r, c), F32),
        compiler_params=_params(("parallel",)),
    )(parts)


PACK_LANES = 512


def _pack(arrays, dtype, lead=()):
    nl = len(lead)
    flats = [a.astype(dtype).reshape(lead + (-1,)) for a in arrays]
    sizes = [f.shape[-1] for f in flats]
    total = sum(sizes)
    unit = PACK_LANES * 16
    padded = -(-total // unit) * unit
    if padded > total:
        flats.append(jnp.zeros(lead + (padded - total,), dtype))
    return jnp.concatenate(flats, axis=nl).reshape(lead + (padded // PACK_LANES, PACK_LANES)), sizes


def _unpack(packed, shapes, lead=()):
    flat = packed.reshape(lead + (-1,))
    out, off = [], 0
    for shp in shapes:
        size = 1
        for s in shp:
            size *= s
        out.append(flat[..., off:off + size].reshape(lead + tuple(shp)))
        off += size
    return out


def _unshard(stacked, axis):
    moved = jnp.moveaxis(stacked, 0, axis)
    shp = moved.shape
    return moved.reshape(shp[:axis] + (shp[axis] * shp[axis + 1],) + shp[axis + 2:])


def _shard_major(full, axis):
    shp = full.shape
    split = full.reshape(shp[:axis] + (N_DEV, shp[axis] // N_DEV) + shp[axis + 1:])
    return jnp.moveaxis(split, axis, 0)


def _my_shard(full, axis):
    size = full.shape[axis] // N_DEV
    return lax.dynamic_slice_in_dim(full, _my_index() * size, size, axis)


def _local_step(x, target, w, n_seq):
    depth = w["norm_mix_pre"].shape[0]
    d_inner = w["ssd_w_out"].shape[1]
    d_xbc = w["ssd_conv_w"].shape[2]
    saved = []
    for i in range(depth):
        j = i // 2
        s = {"x": x}
        if i % 2 == 0:
            u = _rms_fwd(x, w["norm_mix_pre"][i:i + 1], out_dtype=BF16, name=f"l{i}_mix_pre")
            proj = _mm(u, w["ssd_w_in"][j], name=f"l{i}_ssd_in")
            xbc = _ssd_conv_fwd(proj, d_inner, d_xbc, w["ssd_conv_w"][j], w["ssd_conv_b"][j:j + 1], n_seq,
                                name=f"l{i}_ssd_conv")
            yn, y, hs = _ssd_fwd(proj, xbc, w["ssd_dt_bias"][j:j + 1], w["ssd_a_log"][j:j + 1], w["ssd_d"][j:j + 1],
                                 w["ssd_norm_w"][j:j + 1], n_seq, name=f"l{i}_ssd_scan")
            mix = _mm(yn, w["ssd_w_out"][j], name=f"l{i}_ssd_out")
            s.update(u=u, proj=proj, xbc=xbc, yn=yn, y=y, hs=hs)
        else:
            u = _rms_fwd(x, w["norm_mix_pre"][i:i + 1], out_dtype=F32, name=f"l{i}_mix_pre")
            mix = _pool_fwd(u, w["pool_w"][j], w["pool_scale"][j:j + 1], n_seq, name=f"l{i}_pool")
            s.update(u=u)
        x1 = _res_rms_fwd(x, mix, w["norm_mix_post"][i:i + 1], name=f"l{i}_mix_post")
        n = _rms_fwd(x1, w["norm_ffn_pre"][i:i + 1], out_dtype=BF16, name=f"l{i}_ffn_pre")
        h = _mm(n, w["ffn_w_up"][i], name=f"l{i}_ffn_up")
        a = _ffn_act_fwd(h, w["ffn_conv_w"][i], w["ffn_conv_b"][i:i + 1], n_seq, name=f"l{i}_ffn_act")
        f = _mm(a, w["ffn_w_down"][i], name=f"l{i}_ffn_down")
        x = _res_rms_fwd(x1, f, w["norm_ffn_post"][i:i + 1], name=f"l{i}_ffn_post")
        s.update(mix=mix, x1=x1, n=n, h=h, a=a, f=f)
        saved.append(s)

    loss, dx = _loss_head(x, target)
    grads = {k: [None] * v.shape[0] for k, v in w.items()}
    for i in reversed(range(depth)):
        j = i // 2
        s = saved[i]
        df, grads["norm_ffn_post"][i] = _rms_bwd(s["f"], w["norm_ffn_post"][i:i + 1], dx, None, name=f"l{i}_ffn_post_b")
        da = _mm(df, w["ffn_w_down"][i], tb=True, name=f"l{i}_ffn_down_bx")
        grads["ffn_w_down"][i] = _mm(s["a"], df, ta=True, name=f"l{i}_ffn_down_bw")
        dh, grads["ffn_conv_w"][i], grads["ffn_conv_b"][i] = _ffn_act_bwd(
            s["h"], w["ffn_conv_w"][i], w["ffn_conv_b"][i:i + 1], da, n_seq, name=f"l{i}_ffn_act_b")
        dn = _mm(dh, w["ffn_w_up"][i], tb=True, name=f"l{i}_ffn_up_bx")
        grads["ffn_w_up"][i] = _mm(s["n"], dh, ta=True, name=f"l{i}_ffn_up_bw")
        dx1, grads["norm_ffn_pre"][i] = _rms_bwd(s["x1"], w["norm_ffn_pre"][i:i + 1], dn, dx, name=f"l{i}_ffn_pre_b")
        dmix, grads["norm_mix_post"][i] = _rms_bwd(s["mix"], w["norm_mix_post"][i:i + 1], dx1, None,
                                                   name=f"l{i}_mix_post_b")
        if i % 2 == 0:
            dyn = _mm(dmix, w["ssd_w_out"][j], tb=True, name=f"l{i}_ssd_out_bx")
            grads["ssd_w_out"][j] = _mm(s["yn"], dmix, ta=True, name=f"l{i}_ssd_out_bw")
            dxs, db, dc, dz, ddtr, dnw, dbias, dalog, ddsk = _ssd_bwd(
                s["proj"], s["xbc"], s["hs"], s["y"], dyn, w["ssd_dt_bias"][j:j + 1], w["ssd_a_log"][j:j + 1],
                w["ssd_d"][j:j + 1], w["ssd_norm_w"][j:j + 1], n_seq, name=f"l{i}_ssd_scan_b")
            grads["ssd_norm_w"][j], grads["ssd_dt_bias"][j], grads["ssd_a_log"][j], grads["ssd_d"][j] = (
                dnw, dbias, dalog, ddsk)
            dact = jnp.concatenate([dxs, db, dc], axis=1)
            dpre, grads["ssd_conv_w"][j], grads["ssd_conv_b"][j] = _ssd_conv_bwd(
                s["proj"], d_inner, d_xbc, w["ssd_conv_w"][j], w["ssd_conv_b"][j:j + 1], dact, n_seq,
                name=f"l{i}_ssd_conv_b")
            dproj = jnp.concatenate([dz.astype(BF16), dpre, ddtr.astype(BF16)], axis=1)
            du = _mm(dproj, w["ssd_w_in"][j], tb=True, name=f"l{i}_ssd_in_bx")
            grads["ssd_w_in"][j] = _mm(s["u"], dproj, ta=True, name=f"l{i}_ssd_in_bw")
        else:
            du, grads["pool_w"][j], grads["pool_scale"][j] = _pool_bwd(
                s["u"], w["pool_w"][j], w["pool_scale"][j:j + 1], dmix, n_seq, name=f"l{i}_pool_b")
        dx, grads["norm_mix_pre"][i] = _rms_bwd(s["x"], w["norm_mix_pre"][i:i + 1], du, dx1, name=f"l{i}_mix_pre_b")
    return loss, dx, grads


BIG = (("ssd_w_in", 2), ("ssd_w_out", 1), ("pool_w", 2), ("ffn_w_up", 2), ("ffn_w_down", 1))
SMALL_SHARDED = (("ssd_conv_w", 2), ("ffn_conv_w", 2), ("pool_scale", 1))
SMALL = ("ssd_conv_w", "ssd_conv_b", "ssd_dt_bias", "ssd_a_log", "ssd_d", "ssd_norm_w", "pool_scale", "ffn_conv_w",
         "ffn_conv_b", "norm_mix_pre", "norm_mix_post", "norm_ffn_pre", "norm_ffn_post")
WEIGHTS = ("ssd_w_in", "ssd_conv_w", "ssd_conv_b", "ssd_dt_bias", "ssd_a_log", "ssd_d", "ssd_norm_w", "ssd_w_out",
           "pool_w", "pool_scale", "ffn_w_up", "ffn_conv_w", "ffn_conv_b", "ffn_w_down", "norm_mix_pre",
           "norm_mix_post", "norm_ffn_pre", "norm_ffn_post")


def _compute_layout(full):
    d_inner = full["ssd_w_out"].shape[1]
    d_xbc = full["ssd_conv_w"].shape[2]
    r_heads = d_inner // HEAD_DIM // N_SSD_GROUPS
    w = dict(full)
    w["ssd_w_in"] = jax.vmap(lambda a: _ssd_w_in_layout(a, d_inner, d_xbc, r_heads))(full["ssd_w_in"])
    for k in ("ssd_dt_bias", "ssd_a_log", "ssd_d"):
        w[k] = _head_pad(full[k], r_heads)
    for k in ("ffn_w_up", "ffn_conv_w", "ffn_conv_b"):
        w[k] = _interleave(full[k])
    return w


def _reference_layout(grads, full):
    d_inner = full["ssd_w_out"].shape[1]
    d_xbc = full["ssd_conv_w"].shape[2]
    r_heads = d_inner // HEAD_DIM // N_SSD_GROUPS
    g = {k: jnp.stack(v).reshape((len(v),) + v[0].shape) for k, v in grads.items()}
    g["ssd_w_in"] = jax.vmap(lambda a: _ssd_w_in_unlayout(a, d_inner, d_xbc, r_heads))(g["ssd_w_in"])
    for k in ("ssd_dt_bias", "ssd_a_log", "ssd_d"):
        g[k] = _head_unpad(g[k][:, 0], r_heads)
    for k in ("ffn_w_up", "ffn_conv_w", "ffn_conv_b"):
        g[k] = _deinterleave(g[k])
    return {k: v.reshape(full[k].shape) for k, v in g.items()}


def kernel(x, ssd_w_in, ssd_conv_w, ssd_conv_b, ssd_dt_bias, ssd_a_log, ssd_d, ssd_norm_w, ssd_w_out, pool_w, pool_scale, ffn_w_up, ffn_conv_w, ffn_conv_b, ffn_w_down, norm_mix_pre, norm_mix_post, norm_ffn_pre, norm_ffn_post, loss_target, m_ssd_w_in, m_ssd_conv_w, m_ssd_conv_b, m_ssd_dt_bias, m_ssd_a_log, m_ssd_d, m_ssd_norm_w, m_ssd_w_out, m_pool_w, m_pool_scale, m_ffn_w_up, m_ffn_conv_w, m_ffn_conv_b, m_ffn_w_down, m_norm_mix_pre, m_norm_mix_post, m_norm_ffn_pre, m_norm_ffn_post, v_ssd_w_in, v_ssd_conv_w, v_ssd_conv_b, v_ssd_dt_bias, v_ssd_a_log, v_ssd_d, v_ssd_norm_w, v_ssd_w_out, v_pool_w, v_pool_scale, v_ffn_w_up, v_ffn_conv_w, v_ffn_conv_b, v_ffn_w_down, v_norm_mix_pre, v_norm_mix_post, v_norm_ffn_pre, v_norm_ffn_post):
    shards = dict(ssd_w_in=ssd_w_in, ssd_conv_w=ssd_conv_w, ssd_conv_b=ssd_conv_b, ssd_dt_bias=ssd_dt_bias,
                  ssd_a_log=ssd_a_log, ssd_d=ssd_d, ssd_norm_w=ssd_norm_w, ssd_w_out=ssd_w_out, pool_w=pool_w,
                  pool_scale=pool_scale, ffn_w_up=ffn_w_up, ffn_conv_w=ffn_conv_w, ffn_conv_b=ffn_conv_b,
                  ffn_w_down=ffn_w_down, norm_mix_pre=norm_mix_pre, norm_mix_post=norm_mix_post,
                  norm_ffn_pre=norm_ffn_pre, norm_ffn_post=norm_ffn_post)
    moments_m = dict(zip(WEIGHTS, (m_ssd_w_in, m_ssd_conv_w, m_ssd_conv_b, m_ssd_dt_bias, m_ssd_a_log, m_ssd_d, m_ssd_norm_w, m_ssd_w_out, m_pool_w, m_pool_scale, m_ffn_w_up, m_ffn_conv_w, m_ffn_conv_b, m_ffn_w_down, m_norm_mix_pre, m_norm_mix_post, m_norm_ffn_pre, m_norm_ffn_post)))
    moments_v = dict(zip(WEIGHTS, (v_ssd_w_in, v_ssd_conv_w, v_ssd_conv_b, v_ssd_dt_bias, v_ssd_a_log, v_ssd_d, v_ssd_norm_w, v_ssd_w_out, v_pool_w, v_pool_scale, v_ffn_w_up, v_ffn_conv_w, v_ffn_conv_b, v_ffn_w_down, v_norm_mix_pre, v_norm_mix_post, v_norm_ffn_pre, v_norm_ffn_post)))
    n_seq, seq, d_model = x.shape
    t = n_seq * seq

    big_packed, _ = _pack([shards[k] for k, _ in BIG], BF16)
    big_all = _all_gather(big_packed, name="gather_weights")
    big_stacked = _unpack(big_all, [shards[k].shape for k, _ in BIG], lead=(N_DEV,))
    small_packed, _ = _pack([shards[k] for k, _ in SMALL_SHARDED], F32)
    small_all = _all_gather(small_packed, name="gather_small_weights")
    small_stacked = _unpack(small_all, [shards[k].shape for k, _ in SMALL_SHARDED], lead=(N_DEV,))
    full = dict(shards)
    for (k, axis), st in zip(BIG + SMALL_SHARDED, big_stacked + small_stacked):
        full[k] = _unshard(st, axis)
    w = _compute_layout(full)

    loss, dx, grads = _local_step(x.reshape(t, d_model), loss_target.reshape(t, d_model), w, n_seq)
    loss = lax.psum(loss, ("x", "y", "c"))
    g_full = _reference_layout(grads, full)

    g_packed, _ = _pack([_shard_major(g_full[k], axis) for k, axis in BIG], F32, lead=(N_DEV,))
    g_mine = _sum_slots(_exchange(g_packed, name="exchange_grads"), name="sum_grads")
    g_shard = dict(zip([k for k, _ in BIG], _unpack(g_mine, [shards[k].shape for k, _ in BIG])))

    s_packed, _ = _pack([g_full[k] for k in SMALL], F32)
    s_sum = _sum_slots(_all_gather(s_packed, name="gather_small_grads"), name="sum_small_grads")
    for k, g in zip(SMALL, _unpack(s_sum, [g_full[k].shape for k in SMALL])):
        g_shard[k] = g
    for k, axis in SMALL_SHARDED:
        g_shard[k] = _my_shard(g_shard[k], axis)

    deltas, new_m, new_v = {}, {}, {}
    for k in WEIGHTS:
        deltas[k], new_m[k], new_v[k] = _adamw(shards[k], g_shard[k], moments_m[k], moments_v[k], name=f"adamw_{k}")
    return (loss, dx.reshape(x.shape), *[g_shard[k] for k in WEIGHTS], *[deltas[k] for k in WEIGHTS],
            *[new_m[k] for k in WEIGHTS], *[new_v[k] for k in WEIGHTS])
```

```python
import functools

import jax
import jax.numpy as jnp
from jax import lax
from jax.experimental import pallas as pl
from jax.experimental.pallas import tpu as pltpu

F32 = jnp.float32
BF16 = jnp.bfloat16

N_DEV = 8
HEAD_DIM = 64
N_SSD_GROUPS = 4
D_STATE = 128
CHUNK = 128
POOL_WINDOWS = (2, 4, 8, 16)
EPS = 1e-6
LANE = 128
ADAM_LR = 0.001
ADAM_B1 = 0.9
ADAM_B2 = 0.999
ADAM_EPS = 1e-08
ADAM_WD = 0.01
ADAM_STEP = 10
VMEM_LIMIT = 56 * 1024 * 1024


def _pick(n, cands):
    for c in cands:
        if n % c == 0:
            return c
    return n


def _params(sem):
    return pltpu.CompilerParams(dimension_semantics=sem, vmem_limit_bytes=VMEM_LIMIT)


def _sigmoid(x):
    return 1.0 / (1.0 + jnp.exp(-x))


def _silu(x):
    return x * _sigmoid(x)


def _dsilu(x):
    s = _sigmoid(x)
    return s * (1.0 + x * (1.0 - s))


def _shift_down(x, s):
    rows = lax.broadcasted_iota(jnp.int32, x.shape, 0)
    return jnp.where(rows >= s, pltpu.roll(x, s, 0), 0.0)


def _shift_up(x, s):
    n = x.shape[0]
    rows = lax.broadcasted_iota(jnp.int32, x.shape, 0)
    return jnp.where(rows < n - s, pltpu.roll(x, n - s, 0), 0.0)


MM_VMEM_BUDGET = 40 * 1024 * 1024
MM_STEP_BYTES = 1_300_000
MM_SUB = 512


def _mm_tiles(m, n, k, a_bytes, b_bytes, o_bytes):
    def cands(dim, sizes):
        out = [s for s in sizes if s <= dim and dim % s == 0]
        return out or [dim]

    best = None
    for tm in cands(m, (2048, 1024, 512, 256, 128)):
        for tn in cands(n, (n, n // 2, n // 4, 2048, 1024, 512, 256, 128)):
            if tn % (2 * LANE) and tn != n:
                continue
            for tk in cands(k, (k, k // 2, 2048, 1024, 512)):
                if tk % LANE:
                    continue
                nk = k // tk
                acc = tm * tn * 4 if (nk > 1 and o_bytes != 4) else 0
                temps = tm * min(tn, MM_SUB) * 4 + (tm * tk * 2 if a_bytes == 4 else 0) + (tk * tn * 2 if b_bytes == 4 else 0)
                vmem = 2 * (tm * tk * a_bytes + tk * tn * b_bytes + tm * tn * o_bytes) + acc + temps
                if vmem > MM_VMEM_BUDGET:
                    continue
                steps = (m // tm) * (n // tn) * nk
                cost = (m * k * a_bytes * (n // tn) + k * n * b_bytes * (m // tm) + m * n * o_bytes
                        + steps * MM_STEP_BYTES)
                if best is None or cost < best[0]:
                    best = (cost, tm, tn, tk)
    return best[1:]


def _mm(a, b, *, ta=False, tb=False, out_dtype=F32, name="mm"):
    m, k = (a.shape[1], a.shape[0]) if ta else a.shape
    n = b.shape[0] if tb else b.shape[1]
    o_bytes = jnp.dtype(out_dtype).itemsize
    tm, tn, tk = _mm_tiles(m, n, k, a.dtype.itemsize, b.dtype.itemsize, o_bytes)
    nk = k // tk
    sub = _pick(tn, (MM_SUB, 256))
    use_acc = nk > 1 and o_bytes != 4
    a_spec = pl.BlockSpec((tk, tm), lambda i, j, kk: (kk, i)) if ta else pl.BlockSpec((tm, tk), lambda i, j, kk: (i, kk))
    b_spec = pl.BlockSpec((tn, tk), lambda i, j, kk: (j, kk)) if tb else pl.BlockSpec((tk, tn), lambda i, j, kk: (kk, j))
    dims = (((0 if ta else 1,), (1 if tb else 0,)), ((), ()))

    def body(a_ref, b_ref, o_ref, *scratch):
        kk = pl.program_id(2)
        av = a_ref[...].astype(BF16)
        for s in range(tn // sub):
            cols = slice(s * sub, (s + 1) * sub)
            bv = (b_ref[cols, :] if tb else b_ref[:, cols]).astype(BF16)
            part = lax.dot_general(av, bv, dims, preferred_element_type=F32)
            if nk == 1:
                o_ref[:, cols] = part.astype(out_dtype)
            else:
                acc_ref = scratch[0] if use_acc else o_ref

                @pl.when(kk == 0)
                def _():
                    acc_ref[:, cols] = part

                @pl.when(kk > 0)
                def _():
                    acc_ref[:, cols] += part

                if use_acc:
                    @pl.when(kk == nk - 1)
                    def _():
                        o_ref[:, cols] = acc_ref[:, cols].astype(out_dtype)

    return pl.pallas_call(
        body,
        name=name,
        grid=(m // tm, n // tn, nk),
        in_specs=[a_spec, b_spec],
        out_specs=pl.BlockSpec((tm, tn), lambda i, j, kk: (i, j)),
        out_shape=jax.ShapeDtypeStruct((m, n), out_dtype),
        scratch_shapes=[pltpu.VMEM((tm, tn), F32)] if use_acc else [],
        compiler_params=_params(("parallel", "parallel", "arbitrary")),
    )(a, b)


def _rms_fwd(x, w, *, out_dtype, name):
    t, d = x.shape
    tm = _pick(t, (512, 256, 128))

    def body(x_ref, w_ref, o_ref):
        xv = x_ref[...]
        rstd = lax.rsqrt(jnp.mean(xv * xv, axis=-1, keepdims=True) + EPS)
        o_ref[...] = (xv * rstd * w_ref[...]).astype(out_dtype)

    return pl.pallas_call(
        body,
        name=name,
        grid=(t // tm,),
        in_specs=[pl.BlockSpec((tm, d), lambda i: (i, 0)), pl.BlockSpec((1, d), lambda i: (0, 0))],
        out_specs=pl.BlockSpec((tm, d), lambda i: (i, 0)),
        out_shape=jax.ShapeDtypeStruct((t, d), out_dtype),
        compiler_params=_params(("parallel",)),
    )(x, w)


def _res_rms_fwd(x, f, w, *, name):
    t, d = x.shape
    tm = _pick(t, (512, 256, 128))

    def body(x_ref, f_ref, w_ref, o_ref):
        fv = f_ref[...]
        rstd = lax.rsqrt(jnp.mean(fv * fv, axis=-1, keepdims=True) + EPS)
        o_ref[...] = x_ref[...] + fv * rstd * w_ref[...]

    row = pl.BlockSpec((tm, d), lambda i: (i, 0))
    return pl.pallas_call(
        body,
        name=name,
        grid=(t // tm,),
        in_specs=[row, row, pl.BlockSpec((1, d), lambda i: (0, 0))],
        out_specs=row,
        out_shape=jax.ShapeDtypeStruct((t, d), F32),
        compiler_params=_params(("parallel",)),
    )(x, f, w)


def _rms_bwd(x, w, dy, resid, *, name):
    t, d = x.shape
    tm = _pick(t, (512, 256, 128))
    has_res = resid is not None

    def body(*refs):
        if has_res:
            x_ref, w_ref, dy_ref, r_ref, dx_ref, dw_ref = refs
        else:
            x_ref, w_ref, dy_ref, dx_ref, dw_ref = refs
        xv = x_ref[...]
        dyv = dy_ref[...].astype(F32)
        rstd = lax.rsqrt(jnp.mean(xv * xv, axis=-1, keepdims=True) + EPS)
        xn = xv * rstd
        g = dyv * w_ref[...]
        dx = rstd * (g - xn * jnp.mean(g * xn, axis=-1, keepdims=True))
        if has_res:
            dx = dx + r_ref[...]
        dx_ref[...] = dx
        part = jnp.sum(dyv * xn, axis=0, keepdims=True)

        @pl.when(pl.program_id(0) == 0)
        def _():
            dw_ref[...] = part

        @pl.when(pl.program_id(0) > 0)
        def _():
            dw_ref[...] += part

    row = pl.BlockSpec((tm, d), lambda i: (i, 0))
    vec = pl.BlockSpec((1, d), lambda i: (0, 0))
    ins = [x, w, dy] + ([resid] if has_res else [])
    return pl.pallas_call(
        body,
        name=name,
        grid=(t // tm,),
        in_specs=[row, vec, row] + ([row] if has_res else []),
        out_specs=[row, vec],
        out_shape=[jax.ShapeDtypeStruct((t, d), F32), jax.ShapeDtypeStruct((1, d), F32)],
        compiler_params=_params(("arbitrary",)),
    )(*ins)


def _loss_head(y, target, *, name="loss_head"):
    t, d = y.shape
    tm = _pick(t, (512, 256, 128))

    def body(y_ref, t_ref, dy_ref, l_ref):
        err = y_ref[...] - t_ref[...]
        dy_ref[...] = err * (1.0 / d)
        part = jnp.sum(jnp.sum(err * err, axis=-1, keepdims=True), axis=0, keepdims=True) * (0.5 / d)
        part = jnp.broadcast_to(part, (1, LANE))

        @pl.when(pl.program_id(0) == 0)
        def _():
            l_ref[...] = part

        @pl.when(pl.program_id(0) > 0)
        def _():
            l_ref[...] += part

    row = pl.BlockSpec((tm, d), lambda i: (i, 0))
    dy, l = pl.pallas_call(
        body,
        name=name,
        grid=(t // tm,),
        in_specs=[row, row],
        out_specs=[row, pl.BlockSpec((1, LANE), lambda i: (0, 0))],
        out_shape=[jax.ShapeDtypeStruct((t, d), F32), jax.ShapeDtypeStruct((1, LANE), F32)],
        compiler_params=_params(("arbitrary",)),
    )(y, target)
    return l[0, 0], dy


def _conv_taps(h, w_ref, k_taps):
    out = h * w_ref[k_taps - 1:k_taps, :]
    for k in range(k_taps - 1):
        out = out + _shift_down(h, k_taps - 1 - k) * w_ref[k:k + 1, :]
    return out


def _conv_taps_bwd(h, dhc, w_ref, k_taps):
    dh = dhc * w_ref[k_taps - 1:k_taps, :]
    dws = []
    for k in range(k_taps - 1):
        s = k_taps - 1 - k
        dh = dh + _shift_up(dhc, s) * w_ref[k:k + 1, :]
        dws.append(jnp.sum(dhc * _shift_down(h, s), axis=0, keepdims=True))
    dws.append(jnp.sum(dhc * h, axis=0, keepdims=True))
    return dh, jnp.concatenate(dws, axis=0)


FFN_TC = 256


def _interleave(w, tc=FFN_TC):
    f = w.shape[-1] // 2
    lead = w.shape[:-1]
    return jnp.swapaxes(w.reshape(lead + (2, f // tc, tc)), -3, -2).reshape(lead + (2 * f,))


def _deinterleave(w, tc=FFN_TC):
    f = w.shape[-1] // 2
    lead = w.shape[:-1]
    return jnp.swapaxes(w.reshape(lead + (f // tc, 2, tc)), -3, -2).reshape(lead + (2 * f,))


def _ffn_act_fwd(h, conv_w, conv_b, n_seq, *, name):
    t, f2 = h.shape
    seq = t // n_seq
    tc = FFN_TC
    nj = f2 // (2 * tc)
    k_taps = conv_w.shape[0]

    def body(h_ref, w_ref, b_ref, o_ref):
        hc = _conv_taps(h_ref[...].astype(F32), w_ref, k_taps) + b_ref[...]
        o_ref[...] = (_silu(hc[:, :tc]) * hc[:, tc:]).astype(BF16)

    return pl.pallas_call(
        body,
        name=name,
        grid=(n_seq, nj),
        in_specs=[
            pl.BlockSpec((seq, 2 * tc), lambda b, j: (b, j)),
            pl.BlockSpec((k_taps, 2 * tc), lambda b, j: (0, j)),
            pl.BlockSpec((1, 2 * tc), lambda b, j: (0, j)),
        ],
        out_specs=pl.BlockSpec((seq, tc), lambda b, j: (b, j)),
        out_shape=jax.ShapeDtypeStruct((t, f2 // 2), BF16),
        compiler_params=_params(("parallel", "parallel")),
    )(h, conv_w, conv_b)


def _ffn_act_bwd(h, conv_w, conv_b, da, n_seq, *, name):
    t, f2 = h.shape
    seq = t // n_seq
    tc = FFN_TC
    nj = f2 // (2 * tc)
    k_taps = conv_w.shape[0]

    def body(h_ref, w_ref, b_ref, da_ref, dh_ref, dw_ref, db_ref):
        hv = h_ref[...].astype(F32)
        hc = _conv_taps(hv, w_ref, k_taps) + b_ref[...]
        gate, val = hc[:, :tc], hc[:, tc:]
        dav = da_ref[...].astype(F32)
        dhc = jnp.concatenate([dav * val * _dsilu(gate), dav * _silu(gate)], axis=1)
        dh, dw = _conv_taps_bwd(hv, dhc, w_ref, k_taps)
        dh_ref[...] = dh.astype(BF16)
        db = jnp.sum(dhc, axis=0, keepdims=True)

        @pl.when(pl.program_id(1) == 0)
        def _():
            dw_ref[...] = dw
            db_ref[...] = db

        @pl.when(pl.program_id(1) > 0)
        def _():
            dw_ref[...] += dw
            db_ref[...] += db

    return pl.pallas_call(
        body,
        name=name,
        grid=(nj, n_seq),
        in_specs=[
            pl.BlockSpec((seq, 2 * tc), lambda j, b: (b, j)),
            pl.BlockSpec((k_taps, 2 * tc), lambda j, b: (0, j)),
            pl.BlockSpec((1, 2 * tc), lambda j, b: (0, j)),
            pl.BlockSpec((seq, tc), lambda j, b: (b, j)),
        ],
        out_specs=[
            pl.BlockSpec((seq, 2 * tc), lambda j, b: (b, j)),
            pl.BlockSpec((k_taps, 2 * tc), lambda j, b: (0, j)),
            pl.BlockSpec((1, 2 * tc), lambda j, b: (0, j)),
        ],
        out_shape=[
            jax.ShapeDtypeStruct((t, f2), BF16),
            jax.ShapeDtypeStruct((k_taps, f2), F32),
            jax.ShapeDtypeStruct((1, f2), F32),
        ],
        compiler_params=_params(("parallel", "arbitrary")),
    )(h, conv_w, conv_b, da)


def _window_mixed(u, window):
    s = u
    step = 1
    while step < window:
        s = s + _shift_down(s, step)
        step *= 2
    rows = lax.broadcasted_iota(jnp.int32, u.shape, 0)
    inv_cnt = 1.0 / jnp.minimum(rows + 1, window).astype(F32)
    return s * inv_cnt - u, inv_cnt


def _window_mixed_bwd(dmixed, inv_cnt, window):
    r = dmixed * inv_cnt
    s = r
    step = 1
    while step < window:
        s = s + _shift_up(s, step)
        step *= 2
    return s - dmixed


def _pool_fwd(u, w, scale, n_seq, *, name):
    t, d = u.shape
    seq = t // n_seq
    n_g, dg, _ = w.shape

    def body(u_ref, w_ref, s_ref, o_ref):
        for k, window in enumerate(POOL_WINDOWS):
            @pl.when(pl.program_id(1) == k)
            def _(window=window):
                mixed, _ = _window_mixed(u_ref[...], window)
                pre = jnp.dot(mixed.astype(BF16), w_ref[0].astype(BF16), preferred_element_type=F32)
                o_ref[...] = pre * s_ref[...]

    return pl.pallas_call(
        body,
        name=name,
        grid=(n_seq, n_g),
        in_specs=[
            pl.BlockSpec((seq, dg), lambda b, g: (b, g)),
            pl.BlockSpec((1, dg, dg), lambda b, g: (g, 0, 0)),
            pl.BlockSpec((1, dg), lambda b, g: (0, g)),
        ],
        out_specs=pl.BlockSpec((seq, dg), lambda b, g: (b, g)),
        out_shape=jax.ShapeDtypeStruct((t, d), F32),
        compiler_params=_params(("parallel", "parallel")),
    )(u, w, scale)


def _pool_bwd(u, w, scale, dout, n_seq, *, name):
    t, d = u.shape
    seq = t // n_seq
    n_g, dg, _ = w.shape

    def body(u_ref, w_ref, s_ref, do_ref, du_ref, dw_ref, ds_ref):
        group = pl.program_id(0)
        first = pl.program_id(1) == 0
        for k, window in enumerate(POOL_WINDOWS):
            @pl.when(group == k)
            def _(window=window):
                mixed, inv_cnt = _window_mixed(u_ref[...], window)
                mixed_b = mixed.astype(BF16)
                w_b = w_ref[0].astype(BF16)
                dov = do_ref[...]
                pre = jnp.dot(mixed_b, w_b, preferred_element_type=F32)
                dsc = jnp.sum(dov * pre, axis=0, keepdims=True)
                dpre = (dov * s_ref[...]).astype(BF16)
                dw = lax.dot_general(mixed_b, dpre, (((0,), (0,)), ((), ())), preferred_element_type=F32)
                dmixed = lax.dot_general(dpre, w_b, (((1,), (1,)), ((), ())), preferred_element_type=F32)
                du_ref[...] = _window_mixed_bwd(dmixed, inv_cnt, window)

                @pl.when(first)
                def _():
                    dw_ref[0] = dw
                    ds_ref[...] = dsc

                @pl.when(jnp.logical_not(first))
                def _():
                    dw_ref[0] += dw
                    ds_ref[...] += dsc

    return pl.pallas_call(
        body,
        name=name,
        grid=(n_g, n_seq),
        in_specs=[
            pl.BlockSpec((seq, dg), lambda g, b: (b, g)),
            pl.BlockSpec((1, dg, dg), lambda g, b: (g, 0, 0)),
            pl.BlockSpec((1, dg), lambda g, b: (0, g)),
            pl.BlockSpec((seq, dg), lambda g, b: (b, g)),
        ],
        out_specs=[
            pl.BlockSpec((seq, dg), lambda g, b: (b, g)),
            pl.BlockSpec((1, dg, dg), lambda g, b: (g, 0, 0)),
            pl.BlockSpec((1, dg), lambda g, b: (0, g)),
        ],
        out_shape=[
            jax.ShapeDtypeStruct((t, d), F32),
            jax.ShapeDtypeStruct((n_g, dg, dg), F32),
            jax.ShapeDtypeStruct((1, d), F32),
        ],
        compiler_params=_params(("parallel", "arbitrary")),
    )(u, w, scale, dout)


def _adamw(w, g, m, v, *, name):
    shape = w.shape
    c = shape[-1]
    r = w.size // c
    tm = _pick(r, (512, 256, 128, 64, 32, 16, 8))

    def body(w_ref, g_ref, m_ref, v_ref, d_ref, nm_ref, nv_ref):
        gv = g_ref[...]
        nm = ADAM_B1 * m_ref[...] + (1.0 - ADAM_B1) * gv
        nv = ADAM_B2 * v_ref[...] + (1.0 - ADAM_B2) * (gv * gv)
        m_hat = nm / (1.0 - ADAM_B1 ** ADAM_STEP)
        v_hat = nv / (1.0 - ADAM_B2 ** ADAM_STEP)
        d_ref[...] = -ADAM_LR * (m_hat / (jnp.sqrt(v_hat) + ADAM_EPS) + ADAM_WD * w_ref[...])
        nm_ref[...] = nm
        nv_ref[...] = nv

    blk = pl.BlockSpec((tm, c), lambda i: (i, 0))
    out = jax.ShapeDtypeStruct((r, c), F32)
    res = pl.pallas_call(
        body,
        name=name,
        grid=(r // tm,),
        in_specs=[blk] * 4,
        out_specs=[blk] * 3,
        out_shape=[out] * 3,
        compiler_params=_params(("parallel",)),
    )(w.reshape(r, c), g.reshape(r, c), m.reshape(r, c), v.reshape(r, c))
    return tuple(a.reshape(shape) for a in res)


CONV_TC = 256


def _ssd_conv_fwd(proj, col0, n_cols, conv_w, conv_b, n_seq, *, name):
    t = proj.shape[0]
    seq = t // n_seq
    tc = CONV_TC
    off = col0 // tc
    k_taps = conv_w.shape[0]

    def body(h_ref, w_ref, b_ref, o_ref):
        o_ref[...] = _silu(_conv_taps(h_ref[...], w_ref, k_taps) + b_ref[...])

    return pl.pallas_call(
        body,
        name=name,
        grid=(n_seq, n_cols // tc),
        in_specs=[
            pl.BlockSpec((seq, tc), lambda b, j: (b, j + off)),
            pl.BlockSpec((k_taps, tc), lambda b, j: (0, j)),
            pl.BlockSpec((1, tc), lambda b, j: (0, j)),
        ],
        out_specs=pl.BlockSpec((seq, tc), lambda b, j: (b, j)),
        out_shape=jax.ShapeDtypeStruct((t, n_cols), F32),
        compiler_params=_params(("parallel", "parallel")),
    )(proj, conv_w, conv_b)


def _ssd_conv_bwd(proj, col0, n_cols, conv_w, conv_b, dact, n_seq, *, name):
    t = proj.shape[0]
    seq = t // n_seq
    tc = CONV_TC
    off = col0 // tc
    k_taps = conv_w.shape[0]

    def body(h_ref, w_ref, b_ref, da_ref, dh_ref, dw_ref, db_ref):
        hv = h_ref[...]
        dhc = da_ref[...] * _dsilu(_conv_taps(hv, w_ref, k_taps) + b_ref[...])
        dh, dw = _conv_taps_bwd(hv, dhc, w_ref, k_taps)
        dh_ref[...] = dh.astype(BF16)
        db = jnp.sum(dhc, axis=0, keepdims=True)

        @pl.when(pl.program_id(1) == 0)
        def _():
            dw_ref[...] = dw
            db_ref[...] = db

        @pl.when(pl.program_id(1) > 0)
        def _():
            dw_ref[...] += dw
            db_ref[...] += db

    return pl.pallas_call(
        body,
        name=name,
        grid=(n_cols // tc, n_seq),
        in_specs=[
            pl.BlockSpec((seq, tc), lambda j, b: (b, j + off)),
            pl.BlockSpec((k_taps, tc), lambda j, b: (0, j)),
            pl.BlockSpec((1, tc), lambda j, b: (0, j)),
            pl.BlockSpec((seq, tc), lambda j, b: (b, j)),
        ],
        out_specs=[
            pl.BlockSpec((seq, tc), lambda j, b: (b, j)),
            pl.BlockSpec((k_taps, tc), lambda j, b: (0, j)),
            pl.BlockSpec((1, tc), lambda j, b: (0, j)),
        ],
        out_shape=[
            jax.ShapeDtypeStruct((t, n_cols), BF16),
            jax.ShapeDtypeStruct((k_taps, n_cols), F32),
            jax.ShapeDtypeStruct((1, n_cols), F32),
        ],
        compiler_params=_params(("parallel", "arbitrary")),
    )(proj, conv_w, conv_b, dact)


def _softplus(x):
    return jnp.maximum(x, 0.0) + jnp.log(1.0 + jnp.exp(-jnp.abs(x)))


def _chunk_decay(dtraw, bias, alog):
    q = dtraw.shape[0]
    dt = _softplus(dtraw + bias)
    a = -jnp.exp(alog)
    rows = lax.broadcasted_iota(jnp.int32, (q, q), 0)
    cols = lax.broadcasted_iota(jnp.int32, (q, q), 1)
    lower = rows >= cols
    acum = jnp.dot(lower.astype(F32), dt * a, precision=lax.Precision.HIGHEST, preferred_element_type=F32)
    return dt, a, acum, acum.T, lower


def _lane_spread(vals, width):
    r = len(vals)
    return jnp.concatenate([jnp.broadcast_to(v, (v.shape[0], width)) for v in vals], axis=1)


def _head_pad(v, r_heads):
    lead = v.shape[:-1]
    vg = v.reshape(lead + (N_SSD_GROUPS, r_heads))
    vg = jnp.pad(vg, [(0, 0)] * len(lead) + [(0, 0), (0, LANE - r_heads)])
    out = vg.reshape(lead + (N_SSD_GROUPS * LANE,))
    return out[None] if out.ndim == 1 else out


def _head_unpad(v, r_heads):
    lead = v.shape[:-1]
    out = v.reshape(lead + (N_SSD_GROUPS, LANE))[..., :r_heads].reshape(lead + (N_SSD_GROUPS * r_heads,))
    return out[0] if (len(lead) == 1 and lead[0] == 1) else out


def _ssd_w_in_layout(w_in, d_inner, d_xbc, r_heads):
    main = w_in[:, :d_inner + d_xbc]
    return jnp.concatenate([main, _head_pad(w_in[:, d_inner + d_xbc:], r_heads)], axis=1)


def _ssd_w_in_unlayout(w, d_inner, d_xbc, r_heads):
    main = w[:, :d_inner + d_xbc]
    return jnp.concatenate([main, _head_unpad(w[:, d_inner + d_xbc:], r_heads)], axis=1)


def _ssd_dims(proj, xbc):
    d_xbc = xbc.shape[1]
    d_inner = d_xbc - 2 * N_SSD_GROUPS * D_STATE
    gw = d_inner // N_SSD_GROUPS
    return d_inner, d_xbc, gw, gw // HEAD_DIM


def _ssd_fwd(proj, xbc, bias_p, alog_p, dskip_p, norm_w, n_seq, *, name):
    t = proj.shape[0]
    d_inner, d_xbc, gw, r_heads = _ssd_dims(proj, xbc)
    q, n, n_g, p = CHUNK, D_STATE, N_SSD_GROUPS, HEAD_DIM
    seq = t // n_seq
    nc = seq // q
    dt_blk0 = (d_inner + d_xbc) // LANE

    def body(x_ref, b_ref, c_ref, z_ref, dtr_ref, bias_ref, alog_ref, dsk_ref, nw_ref, yn_ref, y_ref, hs_ref, h_scr):
        @pl.when(pl.program_id(2) == 0)
        def _():
            h_scr[...] = jnp.zeros_like(h_scr)

        dt, a, acum, acum_t, lower = _chunk_decay(dtr_ref[...], bias_ref[...], alog_ref[...])
        x = x_ref[...]
        bb = b_ref[...].astype(BF16)
        cb = c_ref[...].astype(BF16)
        g_mat = lax.dot_general(cb, bb, (((1,), (1,)), ((), ())), preferred_element_type=F32)
        h_prev = h_scr[...]
        hs_ref[...] = h_prev
        c_h = jnp.dot(cb, h_prev.astype(BF16), preferred_element_type=F32)
        a_last = acum[q - 1:q, :]
        dsk = dsk_ref[...]
        ys, xds = [], []
        for h in range(r_heads):
            ac = acum[:, h:h + 1]
            decay = jnp.exp(jnp.where(lower, ac - acum_t[h:h + 1, :], -jnp.inf))
            xh = x[:, h * p:(h + 1) * p]
            xdt = xh * dt[:, h:h + 1]
            y_diag = jnp.dot((g_mat * decay).astype(BF16), xdt.astype(BF16), preferred_element_type=F32)
            ys.append(y_diag + jnp.exp(ac) * c_h[:, h * p:(h + 1) * p] + dsk[:, h:h + 1] * xh)
            xds.append(xdt * jnp.exp(a_last[:, h:h + 1] - ac))
        y = jnp.concatenate(ys, axis=1)
        xd = jnp.concatenate(xds, axis=1)
        states = lax.dot_general(bb, xd.astype(BF16), (((0,), (0,)), ((), ())), preferred_element_type=F32)
        cd = _lane_spread([jnp.exp(a_last[:, h:h + 1]) for h in range(r_heads)], p)
        h_scr[...] = h_prev * cd + states
        y_ref[...] = y
        gated = y * _silu(z_ref[...])
        rstd = lax.rsqrt(jnp.mean(gated * gated, axis=-1, keepdims=True) + EPS)
        yn_ref[...] = (gated * rstd * nw_ref[...]).astype(BF16)

    row = lambda b, g, c: b * nc + c
    vec = pl.BlockSpec((1, LANE), lambda b, g, c: (0, g))
    return pl.pallas_call(
        body,
        name=name,
        grid=(n_seq, n_g, nc),
        in_specs=[
            pl.BlockSpec((q, gw), lambda b, g, c: (row(b, g, c), g)),
            pl.BlockSpec((q, n), lambda b, g, c: (row(b, g, c), d_inner // n + g)),
            pl.BlockSpec((q, n), lambda b, g, c: (row(b, g, c), d_inner // n + n_g + g)),
            pl.BlockSpec((q, gw), lambda b, g, c: (row(b, g, c), g)),
            pl.BlockSpec((q, LANE), lambda b, g, c: (row(b, g, c), dt_blk0 + g)),
            vec, vec, vec,
            pl.BlockSpec((1, gw), lambda b, g, c: (0, g)),
        ],
        out_specs=[
            pl.BlockSpec((q, gw), lambda b, g, c: (row(b, g, c), g)),
            pl.BlockSpec((q, gw), lambda b, g, c: (row(b, g, c), g)),
            pl.BlockSpec((n, gw), lambda b, g, c: (row(b, g, c), g)),
        ],
        out_shape=[
            jax.ShapeDtypeStruct((t, d_inner), BF16),
            jax.ShapeDtypeStruct((t, d_inner), F32),
            jax.ShapeDtypeStruct((n_seq * nc * n, d_inner), F32),
        ],
        scratch_shapes=[pltpu.VMEM((n, gw), F32)],
        compiler_params=_params(("parallel", "parallel", "arbitrary")),
    )(xbc, xbc, xbc, proj, proj, bias_p, alog_p, dskip_p, norm_w)


def _ssd_bwd(proj, xbc, hs, y, dyn, bias_p, alog_p, dskip_p, norm_w, n_seq, *, name):
    t = proj.shape[0]
    d_inner, d_xbc, gw, r_heads = _ssd_dims(proj, xbc)
    q, n, n_g, p = CHUNK, D_STATE, N_SSD_GROUPS, HEAD_DIM
    seq = t // n_seq
    nc = seq // q
    dt_blk0 = (d_inner + d_xbc) // LANE

    def body(x_ref, b_ref, c_ref, z_ref, dtr_ref, bias_ref, alog_ref, dsk_ref, nw_ref, hs_ref, y_ref, dyn_ref,
             dx_ref, db_ref, dc_ref, dz_ref, ddtr_ref, dnw_ref, dbias_ref, dalog_ref, ddsk_ref, dh_scr):
        first = jnp.logical_and(pl.program_id(1) == 0, pl.program_id(2) == 0)

        @pl.when(pl.program_id(2) == 0)
        def _():
            dh_scr[...] = jnp.zeros_like(dh_scr)

        dtraw = dtr_ref[...]
        dt, a, acum, acum_t, lower = _chunk_decay(dtraw, bias_ref[...], alog_ref[...])
        x = x_ref[...]
        bb = b_ref[...].astype(BF16)
        cb = c_ref[...].astype(BF16)
        g_mat = lax.dot_general(cb, bb, (((1,), (1,)), ((), ())), preferred_element_type=F32)

        yv = y_ref[...]
        z = z_ref[...]
        sz = _silu(z)
        gated = yv * sz
        rstd = lax.rsqrt(jnp.mean(gated * gated, axis=-1, keepdims=True) + EPS)
        gn = gated * rstd
        dynv = dyn_ref[...]
        gwt = dynv * nw_ref[...]
        dgated = rstd * (gwt - gn * jnp.mean(gwt * gn, axis=-1, keepdims=True))
        dnw = jnp.sum(dynv * gn, axis=0, keepdims=True)
        dy = dgated * sz
        dz_ref[...] = dgated * yv * _dsilu(z)

        h_prev = hs_ref[...]
        h_prev_b = h_prev.astype(BF16)
        ds = dh_scr[...]
        ds_b = ds.astype(BF16)
        c_h = jnp.dot(cb, h_prev_b, preferred_element_type=F32)
        dxd = jnp.dot(bb, ds_b, preferred_element_type=F32)
        a_last = acum[q - 1:q, :]
        dsk = dsk_ref[...]
        lane = lax.broadcasted_iota(jnp.int32, (q, LANE), 1)
        sub = lax.broadcasted_iota(jnp.int32, (LANE, q), 0)
        lane1 = lax.broadcasted_iota(jnp.int32, (1, LANE), 1)
        dacum_cols = jnp.zeros((q, LANE), F32)
        dacum_rows = jnp.zeros((LANE, q), F32)
        ddt_cols = jnp.zeros((q, LANE), F32)
        dlast = jnp.zeros((1, LANE), F32)
        ddsk = jnp.zeros((1, LANE), F32)
        dg = jnp.zeros((q, q), F32)
        dxs, dyes, xds, cds = [], [], [], []
        for h in range(r_heads):
            hsl = slice(h * p, (h + 1) * p)
            ac = acum[:, h:h + 1]
            decay = jnp.exp(jnp.where(lower, ac - acum_t[h:h + 1, :], -jnp.inf))
            m_mat = g_mat * decay
            xh = x[:, hsl]
            dth = dt[:, h:h + 1]
            xdt = xh * dth
            xdt_b = xdt.astype(BF16)
            dyh = dy[:, hsl]
            dyh_b = dyh.astype(BF16)
            ea = jnp.exp(ac)
            al = a_last[:, h:h + 1]
            dte = jnp.exp(al - ac)
            cdh = jnp.exp(al)
            dm = lax.dot_general(dyh_b, xdt_b, (((1,), (1,)), ((), ())), preferred_element_type=F32)
            w_mat = dm * m_mat
            dac = jnp.sum(w_mat, axis=1, keepdims=True)
            dar = jnp.sum(w_mat, axis=0, keepdims=True)
            dg = dg + dm * decay
            dxdt = lax.dot_general(m_mat.astype(BF16), dyh_b, (((0,), (0,)), ((), ())), preferred_element_type=F32)
            dac = dac + jnp.sum(dyh * (ea * c_h[:, hsl]), axis=1, keepdims=True)
            dyes.append(dyh * ea)
            dxd_h = dxd[:, hsl]
            dxdt = dxdt + dxd_h * dte
            tt = jnp.sum(dxd_h * xdt, axis=1, keepdims=True) * dte
            dac = dac - tt
            dl = jnp.sum(tt, axis=0, keepdims=True)
            dcd = jnp.sum(jnp.sum(ds[:, hsl] * h_prev[:, hsl], axis=1, keepdims=True), axis=0, keepdims=True)
            dl = dl + dcd * cdh
            dxs.append(dxdt * dth + dsk[:, h:h + 1] * dyh)
            ddt_h = jnp.sum(dxdt * xh, axis=1, keepdims=True)
            dd_h = jnp.sum(jnp.sum(dyh * xh, axis=1, keepdims=True), axis=0, keepdims=True)
            xds.append(xdt * dte)
            cds.append(cdh)
            dacum_cols = dacum_cols + jnp.where(lane == h, dac, 0.0)
            dacum_rows = dacum_rows + jnp.where(sub == h, dar, 0.0)
            ddt_cols = ddt_cols + jnp.where(lane == h, ddt_h, 0.0)
            dlast = dlast + jnp.where(lane1 == h, dl, 0.0)
            ddsk = ddsk + jnp.where(lane1 == h, dd_h, 0.0)
        dye_b = jnp.concatenate(dyes, axis=1).astype(BF16)
        xd_b = jnp.concatenate(xds, axis=1).astype(BF16)
        dg_b = dg.astype(BF16)
        dx_ref[...] = jnp.concatenate(dxs, axis=1)
        dc_ref[...] = (jnp.dot(dg_b, bb, preferred_element_type=F32)
                       + lax.dot_general(dye_b, h_prev_b, (((1,), (1,)), ((), ())), preferred_element_type=F32))
        db_ref[...] = (lax.dot_general(dg_b, cb, (((0,), (0,)), ((), ())), preferred_element_type=F32)
                       + lax.dot_general(xd_b, ds_b, (((1,), (1,)), ((), ())), preferred_element_type=F32))
        dh_scr[...] = ds * _lane_spread(cds, p) + lax.dot_general(cb, dye_b, (((0,), (0,)), ((), ())),
                                                                    preferred_element_type=F32)
        rows_q = lax.broadcasted_iota(jnp.int32, (q, LANE), 0)
        dacum = dacum_cols - dacum_rows.T + jnp.where(rows_q == q - 1, dlast, 0.0)
        upper = lax.broadcasted_iota(jnp.int32, (q, q), 0) <= lax.broadcasted_iota(jnp.int32, (q, q), 1)
        dadt = jnp.dot(upper.astype(F32), dacum, precision=lax.Precision.HIGHEST, preferred_element_type=F32)
        ddt = dadt * a + ddt_cols
        ddtr = ddt * _sigmoid(dtraw + bias_ref[...])
        ddtr_ref[...] = ddtr
        dbias = jnp.sum(ddtr, axis=0, keepdims=True)
        dalog = jnp.sum(dadt * dt, axis=0, keepdims=True) * a

        @pl.when(first)
        def _():
            dnw_ref[...] = dnw
            dbias_ref[...] = dbias
            dalog_ref[...] = dalog
            ddsk_ref[...] = ddsk

        @pl.when(jnp.logical_not(first))
        def _():
            dnw_ref[...] += dnw
            dbias_ref[...] += dbias
            dalog_ref[...] += dalog
            ddsk_ref[...] += ddsk

    row = lambda g, b, c: b * nc + (nc - 1 - c)
    vec = pl.BlockSpec((1, LANE), lambda g, b, c: (0, g))
    wide = pl.BlockSpec((q, gw), lambda g, b, c: (row(g, b, c), g))
    narrow = pl.BlockSpec((q, n), lambda g, b, c: (row(g, b, c), g))
    return pl.pallas_call(
        body,
        name=name,
        grid=(n_g, n_seq, nc),
        in_specs=[
            wide,
            pl.BlockSpec((q, n), lambda g, b, c: (row(g, b, c), d_inner // n + g)),
            pl.BlockSpec((q, n), lambda g, b, c: (row(g, b, c), d_inner // n + n_g + g)),
            wide,
            pl.BlockSpec((q, LANE), lambda g, b, c: (row(g, b, c), dt_blk0 + g)),
            vec, vec, vec,
            pl.BlockSpec((1, gw), lambda g, b, c: (0, g)),
            pl.BlockSpec((n, gw), lambda g, b, c: (row(g, b, c), g)),
            wide, wide,
        ],
        out_specs=[
            wide, narrow, narrow, wide, narrow,
            pl.BlockSpec((1, gw), lambda g, b, c: (0, g)),
            vec, vec, vec,
        ],
        out_shape=[
            jax.ShapeDtypeStruct((t, d_inner), F32),
            jax.ShapeDtypeStruct((t, n_g * n), F32),
            jax.ShapeDtypeStruct((t, n_g * n), F32),
            jax.ShapeDtypeStruct((t, d_inner), F32),
            jax.ShapeDtypeStruct((t, n_g * LANE), F32),
            jax.ShapeDtypeStruct((1, d_inner), F32),
            jax.ShapeDtypeStruct((1, n_g * LANE), F32),
            jax.ShapeDtypeStruct((1, n_g * LANE), F32),
            jax.ShapeDtypeStruct((1, n_g * LANE), F32),
        ],
        scratch_shapes=[pltpu.VMEM((n, gw), F32)],
        compiler_params=_params(("parallel", "arbitrary", "arbitrary")),
    )(xbc, xbc, xbc, proj, proj, bias_p, alog_p, dskip_p, norm_w, hs, y, dyn)


MESH_IDS = pl.DeviceIdType.MESH
ANY = pl.BlockSpec(memory_space=pl.ANY)


def _my_index():
    return 4 * lax.axis_index("x") + 2 * lax.axis_index("y") + lax.axis_index("c")


def _all_gather(shard, *, name):
    def body(x_ref, out_ref, send_sems, recv_sems, local_sem):
        x, y, c = lax.axis_index("x"), lax.axis_index("y"), lax.axis_index("c")
        me, sibling = (x, y, c), (x, y, 1 - c)
        chips = [(1 - x, y), (x, 1 - y), (1 - x, 1 - y)]

        def blk(px, py, pc):
            return out_ref.at[4 * px + 2 * py + pc]

        def copy(k, block, to, src=None):
            return pltpu.make_async_remote_copy(
                src_ref=blk(*block) if src is None else src, dst_ref=blk(*block),
                send_sem=send_sems.at[k], recv_sem=recv_sems.at[k], device_id=to, device_id_type=MESH_IDS)

        mine = pltpu.make_async_copy(x_ref, blk(*me), local_sem)
        mine.start()
        first = [copy(0, me, sibling, src=x_ref)]
        first += [copy(1 + j, me, (*chip, c), src=x_ref) for j, chip in enumerate(chips)]
        for cp in first:
            cp.start()
        passed = [copy(4 + j, (*chip, c), sibling) for j, chip in enumerate(chips)]
        for j, chip in enumerate(chips):
            copy(1 + j, (*chip, c), me).wait_recv()
            passed[j].start()
        copy(0, sibling, me).wait_recv()
        for j, chip in enumerate(chips):
            copy(4 + j, (*chip, 1 - c), me).wait_recv()
        for cp in first + passed:
            cp.wait_send()
        mine.wait()

    return pl.pallas_call(
        body,
        name=name,
        in_specs=[ANY],
        out_specs=ANY,
        out_shape=jax.ShapeDtypeStruct((N_DEV,) + shard.shape, shard.dtype),
        scratch_shapes=[pltpu.SemaphoreType.DMA((7,)), pltpu.SemaphoreType.DMA((7,)), pltpu.SemaphoreType.DMA],
    )(shard)


def _exchange(full, *, name):
    def body(in_ref, out_ref, send_sems, recv_sems, local_sem):
        x, y, c = lax.axis_index("x"), lax.axis_index("y"), lax.axis_index("c")
        me = 4 * x + 2 * y + c
        mine = pltpu.make_async_copy(in_ref.at[me], out_ref.at[me], local_sem)
        mine.start()
        copies = []
        for k in range(1, N_DEV):
            px = 1 - x if k & 4 else x
            py = 1 - y if k & 2 else y
            pc = 1 - c if k & 1 else c
            peer = 4 * px + 2 * py + pc
            copies.append((
                pltpu.make_async_remote_copy(
                    src_ref=in_ref.at[peer], dst_ref=out_ref.at[me], send_sem=send_sems.at[k - 1],
                    recv_sem=recv_sems.at[k - 1], device_id=(px, py, pc), device_id_type=MESH_IDS),
                pltpu.make_async_remote_copy(
                    src_ref=in_ref.at[peer], dst_ref=out_ref.at[peer], send_sem=send_sems.at[k - 1],
                    recv_sem=recv_sems.at[k - 1], device_id=(px, py, pc), device_id_type=MESH_IDS)))
        for send, _ in copies:
            send.start()
        for send, recv in copies:
            recv.wait_recv()
            send.wait_send()
        mine.wait()

    return pl.pallas_call(
        body,
        name=name,
        in_specs=[ANY],
        out_specs=ANY,
        out_shape=jax.ShapeDtypeStruct(full.shape, full.dtype),
        scratch_shapes=[pltpu.SemaphoreType.DMA((7,)), pltpu.SemaphoreType.DMA((7,)), pltpu.SemaphoreType.DMA],
    )(full)


def _sum_slots(parts, *, name):
    shape = parts.shape[1:]
    n, c = parts.shape[0], parts.shape[-1]
    r = parts.size // (n * c)
    tm = _pick(r, (256, 128, 64, 32, 16, 8))

    def body(p_ref, o_ref):
        acc = p_ref[0].astype(F32)
        for s in range(1, n):
            acc = acc + p_ref[s].astype(F32)
        o_ref[...] = acc

    return pl.pallas_call(
        body,
        name=name,
        grid=(r // tm,),
        in_specs=[pl.BlockSpec((n, tm, c), lambda i: (0, i, 0))],
        out_specs=pl.BlockSpec((tm, c), lambda i: (i, 0)),
        out_shape=jax.ShapeDtypeStruct((r, c), F32),
        compiler_params=_params(("parallel",)),
    )(parts.reshape(n, r, c)).reshape(shape)


def _row_count(shape):
    c = shape[-1]
    rows = 1
    for s in shape[:-1]:
        rows *= s
    return rows, c, c + (-c) % LANE


def _pack_rows(arrays):
    pieces = []
    for a in arrays:
        rows, c, cp = _row_count(a.shape)
        a2 = a.reshape(rows, c)
        if cp > c:
            a2 = jnp.pad(a2, ((0, 0), (0, cp - c)))
        pieces.append(a2.reshape(rows * cp // LANE, LANE))
    total = sum(p.shape[0] for p in pieces)
    if total % 8:
        pieces.append(jnp.zeros((8 - total % 8, LANE), F32))
    return jnp.concatenate(pieces, axis=0)


def _unpack_rows(packed, shapes, lead=()):
    out, off = [], 0
    for shp in shapes:
        rows, c, cp = _row_count(shp)
        n_rows = rows * cp // LANE
        seg = packed[..., off:off + n_rows, :].reshape(lead + (rows, cp))
        out.append(seg[..., :c].reshape(lead + tuple(shp)))
        off += n_rows
    return out


def _unshard(stacked, axis):
    moved = jnp.moveaxis(stacked, 0, axis)
    shp = moved.shape
    return moved.reshape(shp[:axis] + (shp[axis] * shp[axis + 1],) + shp[axis + 2:])


def _shard_major(full, axis):
    shp = full.shape
    split = full.reshape(shp[:axis] + (N_DEV, shp[axis] // N_DEV) + shp[axis + 1:])
    return jnp.moveaxis(split, axis, 0)


def _my_shard(full, axis):
    size = full.shape[axis] // N_DEV
    return lax.dynamic_slice_in_dim(full, _my_index() * size, size, axis)


def _local_step(x, target, w, n_seq):
    depth = w["norm_mix_pre"].shape[0]
    d_inner = w["ssd_w_out"].shape[1]
    d_xbc = w["ssd_conv_w"].shape[2]
    saved = []
    for i in range(depth):
        j = i // 2
        s = {"x": x}
        if i % 2 == 0:
            u = _rms_fwd(x, w["norm_mix_pre"][i:i + 1], out_dtype=BF16, name=f"l{i}_mix_pre")
            proj = _mm(u, w["ssd_w_in"][j], name=f"l{i}_ssd_in")
            xbc = _ssd_conv_fwd(proj, d_inner, d_xbc, w["ssd_conv_w"][j], w["ssd_conv_b"][j:j + 1], n_seq,
                                name=f"l{i}_ssd_conv")
            yn, y, hs = _ssd_fwd(proj, xbc, w["ssd_dt_bias"][j:j + 1], w["ssd_a_log"][j:j + 1], w["ssd_d"][j:j + 1],
                                 w["ssd_norm_w"][j:j + 1], n_seq, name=f"l{i}_ssd_scan")
            mix = _mm(yn, w["ssd_w_out"][j], name=f"l{i}_ssd_out")
            s.update(u=u, proj=proj, xbc=xbc, yn=yn, y=y, hs=hs)
        else:
            u = _rms_fwd(x, w["norm_mix_pre"][i:i + 1], out_dtype=F32, name=f"l{i}_mix_pre")
            mix = _pool_fwd(u, w["pool_w"][j], w["pool_scale"][j:j + 1], n_seq, name=f"l{i}_pool")
            s.update(u=u)
        x1 = _res_rms_fwd(x, mix, w["norm_mix_post"][i:i + 1], name=f"l{i}_mix_post")
        n = _rms_fwd(x1, w["norm_ffn_pre"][i:i + 1], out_dtype=BF16, name=f"l{i}_ffn_pre")
        h = _mm(n, w["ffn_w_up"][i], out_dtype=BF16, name=f"l{i}_ffn_up")
        a = _ffn_act_fwd(h, w["ffn_conv_w"][i], w["ffn_conv_b"][i:i + 1], n_seq, name=f"l{i}_ffn_act")
        f = _mm(a, w["ffn_w_down"][i], name=f"l{i}_ffn_down")
        x = _res_rms_fwd(x1, f, w["norm_ffn_post"][i:i + 1], name=f"l{i}_ffn_post")
        s.update(mix=mix, x1=x1, n=n, h=h, a=a, f=f)
        saved.append(s)

    loss, dx = _loss_head(x, target)
    grads = {k: [None] * v.shape[0] for k, v in w.items()}
    for i in reversed(range(depth)):
        j = i // 2
        s = saved[i]
        df, grads["norm_ffn_post"][i] = _rms_bwd(s["f"], w["norm_ffn_post"][i:i + 1], dx, None, name=f"l{i}_ffn_post_b")
        da = _mm(df, w["ffn_w_down"][i], tb=True, out_dtype=BF16, name=f"l{i}_ffn_down_bx")
        grads["ffn_w_down"][i] = _mm(s["a"], df, ta=True, name=f"l{i}_ffn_down_bw")
        dh, grads["ffn_conv_w"][i], grads["ffn_conv_b"][i] = _ffn_act_bwd(
            s["h"], w["ffn_conv_w"][i], w["ffn_conv_b"][i:i + 1], da, n_seq, name=f"l{i}_ffn_act_b")
        dn = _mm(dh, w["ffn_w_up"][i], tb=True, name=f"l{i}_ffn_up_bx")
        grads["ffn_w_up"][i] = _mm(s["n"], dh, ta=True, name=f"l{i}_ffn_up_bw")
        dx1, grads["norm_ffn_pre"][i] = _rms_bwd(s["x1"], w["norm_ffn_pre"][i:i + 1], dn, dx, name=f"l{i}_ffn_pre_b")
        dmix, grads["norm_mix_post"][i] = _rms_bwd(s["mix"], w["norm_mix_post"][i:i + 1], dx1, None,
                                                   name=f"l{i}_mix_post_b")
        if i % 2 == 0:
            dyn = _mm(dmix, w["ssd_w_out"][j], tb=True, name=f"l{i}_ssd_out_bx")
            grads["ssd_w_out"][j] = _mm(s["yn"], dmix, ta=True, name=f"l{i}_ssd_out_bw")
            dxs, db, dc, dz, ddtr, dnw, dbias, dalog, ddsk = _ssd_bwd(
                s["proj"], s["xbc"], s["hs"], s["y"], dyn, w["ssd_dt_bias"][j:j + 1], w["ssd_a_log"][j:j + 1],
                w["ssd_d"][j:j + 1], w["ssd_norm_w"][j:j + 1], n_seq, name=f"l{i}_ssd_scan_b")
            grads["ssd_norm_w"][j], grads["ssd_dt_bias"][j], grads["ssd_a_log"][j], grads["ssd_d"][j] = (
                dnw, dbias, dalog, ddsk)
            dact = jnp.concatenate([dxs, db, dc], axis=1)
            dpre, grads["ssd_conv_w"][j], grads["ssd_conv_b"][j] = _ssd_conv_bwd(
                s["proj"], d_inner, d_xbc, w["ssd_conv_w"][j], w["ssd_conv_b"][j:j + 1], dact, n_seq,
                name=f"l{i}_ssd_conv_b")
            dproj = jnp.concatenate([dz.astype(BF16), dpre, ddtr.astype(BF16)], axis=1)
            du = _mm(dproj, w["ssd_w_in"][j], tb=True, name=f"l{i}_ssd_in_bx")
            grads["ssd_w_in"][j] = _mm(s["u"], dproj, ta=True, name=f"l{i}_ssd_in_bw")
        else:
            du, grads["pool_w"][j], grads["pool_scale"][j] = _pool_bwd(
                s["u"], w["pool_w"][j], w["pool_scale"][j:j + 1], dmix, n_seq, name=f"l{i}_pool_b")
        dx, grads["norm_mix_pre"][i] = _rms_bwd(s["x"], w["norm_mix_pre"][i:i + 1], du, dx1, name=f"l{i}_mix_pre_b")
    return loss, dx, grads


BIG = (("ssd_w_in", 2), ("ssd_w_out", 1), ("pool_w", 2), ("ffn_w_up", 2), ("ffn_w_down", 1))
SMALL_SHARDED = (("ssd_conv_w", 2), ("ffn_conv_w", 2), ("pool_scale", 1))
SMALL = ("ssd_conv_w", "ssd_conv_b", "ssd_dt_bias", "ssd_a_log", "ssd_d", "ssd_norm_w", "pool_scale", "ffn_conv_w",
         "ffn_conv_b", "norm_mix_pre", "norm_mix_post", "norm_ffn_pre", "norm_ffn_post")
WEIGHTS = ("ssd_w_in", "ssd_conv_w", "ssd_conv_b", "ssd_dt_bias", "ssd_a_log", "ssd_d", "ssd_norm_w", "ssd_w_out",
           "pool_w", "pool_scale", "ffn_w_up", "ffn_conv_w", "ffn_conv_b", "ffn_w_down", "norm_mix_pre",
           "norm_mix_post", "norm_ffn_pre", "norm_ffn_post")


def _compute_layout(full):
    d_inner = full["ssd_w_out"].shape[1]
    d_xbc = full["ssd_conv_w"].shape[2]
    r_heads = d_inner // HEAD_DIM // N_SSD_GROUPS
    w = dict(full)
    w["ssd_w_in"] = jax.vmap(lambda a: _ssd_w_in_layout(a, d_inner, d_xbc, r_heads))(full["ssd_w_in"])
    for k in ("ssd_dt_bias", "ssd_a_log", "ssd_d"):
        w[k] = _head_pad(full[k], r_heads)
    for k in ("ffn_w_up", "ffn_conv_w", "ffn_conv_b"):
        w[k] = _interleave(full[k])
    return w


def _reference_layout(grads, full):
    d_inner = full["ssd_w_out"].shape[1]
    d_xbc = full["ssd_conv_w"].shape[2]
    r_heads = d_inner // HEAD_DIM // N_SSD_GROUPS
    g = {k: jnp.stack(v).reshape((len(v),) + v[0].shape) for k, v in grads.items()}
    g["ssd_w_in"] = jax.vmap(lambda a: _ssd_w_in_unlayout(a, d_inner, d_xbc, r_heads))(g["ssd_w_in"])
    for k in ("ssd_dt_bias", "ssd_a_log", "ssd_d"):
        g[k] = _head_unpad(g[k][:, 0], r_heads)
    for k in ("ffn_w_up", "ffn_conv_w", "ffn_conv_b"):
        g[k] = _deinterleave(g[k])
    return {k: v.reshape(full[k].shape) for k, v in g.items()}


def kernel(x, ssd_w_in, ssd_conv_w, ssd_conv_b, ssd_dt_bias, ssd_a_log, ssd_d, ssd_norm_w, ssd_w_out, pool_w, pool_scale, ffn_w_up, ffn_conv_w, ffn_conv_b, ffn_w_down, norm_mix_pre, norm_mix_post, norm_ffn_pre, norm_ffn_post, loss_target, m_ssd_w_in, m_ssd_conv_w, m_ssd_conv_b, m_ssd_dt_bias, m_ssd_a_log, m_ssd_d, m_ssd_norm_w, m_ssd_w_out, m_pool_w, m_pool_scale, m_ffn_w_up, m_ffn_conv_w, m_ffn_conv_b, m_ffn_w_down, m_norm_mix_pre, m_norm_mix_post, m_norm_ffn_pre, m_norm_ffn_post, v_ssd_w_in, v_ssd_conv_w, v_ssd_conv_b, v_ssd_dt_bias, v_ssd_a_log, v_ssd_d, v_ssd_norm_w, v_ssd_w_out, v_pool_w, v_pool_scale, v_ffn_w_up, v_ffn_conv_w, v_ffn_conv_b, v_ffn_w_down, v_norm_mix_pre, v_norm_mix_post, v_norm_ffn_pre, v_norm_ffn_post):
    shards = dict(ssd_w_in=ssd_w_in, ssd_conv_w=ssd_conv_w, ssd_conv_b=ssd_conv_b, ssd_dt_bias=ssd_dt_bias,
                  ssd_a_log=ssd_a_log, ssd_d=ssd_d, ssd_norm_w=ssd_norm_w, ssd_w_out=ssd_w_out, pool_w=pool_w,
                  pool_scale=pool_scale, ffn_w_up=ffn_w_up, ffn_conv_w=ffn_conv_w, ffn_conv_b=ffn_conv_b,
                  ffn_w_down=ffn_w_down, norm_mix_pre=norm_mix_pre, norm_mix_post=norm_mix_post,
                  norm_ffn_pre=norm_ffn_pre, norm_ffn_post=norm_ffn_post)
    moments_m = dict(zip(WEIGHTS, (m_ssd_w_in, m_ssd_conv_w, m_ssd_conv_b, m_ssd_dt_bias, m_ssd_a_log, m_ssd_d, m_ssd_norm_w, m_ssd_w_out, m_pool_w, m_pool_scale, m_ffn_w_up, m_ffn_conv_w, m_ffn_conv_b, m_ffn_w_down, m_norm_mix_pre, m_norm_mix_post, m_norm_ffn_pre, m_norm_ffn_post)))
    moments_v = dict(zip(WEIGHTS, (v_ssd_w_in, v_ssd_conv_w, v_ssd_conv_b, v_ssd_dt_bias, v_ssd_a_log, v_ssd_d, v_ssd_norm_w, v_ssd_w_out, v_pool_w, v_pool_scale, v_ffn_w_up, v_ffn_conv_w, v_ffn_conv_b, v_ffn_w_down, v_norm_mix_pre, v_norm_mix_post, v_norm_ffn_pre, v_norm_ffn_post)))
    n_seq, seq, d_model = x.shape
    t = n_seq * seq

    full = dict(shards)
    for k, axis in BIG:
        full[k] = _unshard(_all_gather(shards[k].astype(BF16), name=f"gather_{k}"), axis)
    small_all = _all_gather(_pack_rows([shards[k] for k, _ in SMALL_SHARDED]), name="gather_small_weights")
    small_stacked = _unpack_rows(small_all, [shards[k].shape for k, _ in SMALL_SHARDED], lead=(N_DEV,))
    for (k, axis), st in zip(SMALL_SHARDED, small_stacked):
        full[k] = _unshard(st, axis)
    w = _compute_layout(full)

    loss, dx, grads = _local_step(x.reshape(t, d_model), loss_target.reshape(t, d_model), w, n_seq)
    loss = lax.psum(loss, ("x", "y", "c"))
    g_full = _reference_layout(grads, full)

    g_shard = {}
    for k, axis in BIG:
        blocks = _exchange(_shard_major(g_full[k], axis).astype(BF16), name=f"exchange_{k}")
        g_shard[k] = _sum_slots(blocks, name=f"sum_{k}")

    s_all = _all_gather(_pack_rows([g_full[k] for k in SMALL]), name="gather_small_grads")
    for k, g in zip(SMALL, _unpack_rows(_sum_slots(s_all, name="sum_small_grads"), [g_full[k].shape for k in SMALL])):
        g_shard[k] = g
    for k, axis in SMALL_SHARDED:
        g_shard[k] = _my_shard(g_shard[k], axis)

    deltas, new_m, new_v = {}, {}, {}
    for k in WEIGHTS:
        deltas[k], new_m[k], new_v[k] = _adamw(shards[k], g_shard[k], moments_m[k], moments_v[k], name=f"adamw_{k}")
    return (loss, dx.reshape(x.shape), *[g_shard[k] for k in WEIGHTS], *[deltas[k] for k in WEIGHTS],
            *[new_m[k] for k in WEIGHTS], *[new_v[k] for k in WEIGHTS])
```

```python
import functools

import jax
import jax.numpy as jnp
from jax import lax
from jax.experimental import pallas as pl
from jax.experimental.pallas import tpu as pltpu

F32 = jnp.float32
BF16 = jnp.bfloat16

N_DEV = 8
HEAD_DIM = 64
N_SSD_GROUPS = 4
D_STATE = 128
CHUNK = 128
POOL_WINDOWS = (2, 4, 8, 16)
EPS = 1e-6
LANE = 128
ADAM_LR = 0.001
ADAM_B1 = 0.9
ADAM_B2 = 0.999
ADAM_EPS = 1e-08
ADAM_WD = 0.01
ADAM_STEP = 10
VMEM_LIMIT = 56 * 1024 * 1024
ANY = pl.BlockSpec(memory_space=pl.ANY)


def _pick(n, cands):
    for c in cands:
        if n % c == 0:
            return c
    return n


def _params(sem):
    return pltpu.CompilerParams(dimension_semantics=sem, vmem_limit_bytes=VMEM_LIMIT)


def _sigmoid(x):
    return 1.0 / (1.0 + jnp.exp(-x))


def _silu(x):
    return x * _sigmoid(x)


def _dsilu(x):
    s = _sigmoid(x)
    return s * (1.0 + x * (1.0 - s))


def _shift_down(x, s):
    rows = lax.broadcasted_iota(jnp.int32, x.shape, 0)
    return jnp.where(rows >= s, pltpu.roll(x, s, 0), 0.0)


def _shift_up(x, s):
    n = x.shape[0]
    rows = lax.broadcasted_iota(jnp.int32, x.shape, 0)
    return jnp.where(rows < n - s, pltpu.roll(x, n - s, 0), 0.0)


MM_VMEM_BUDGET = 40 * 1024 * 1024
MM_STEP_BYTES = 1_300_000
MM_SUB = 512


def _mm_tiles(m, n, k, a_bytes, b_bytes, o_bytes):
    def cands(dim, sizes):
        out = [s for s in sizes if s <= dim and dim % s == 0]
        return out or [dim]

    best = None
    for tm in cands(m, (2048, 1024, 512, 256, 128)):
        for tn in cands(n, (n, n // 2, n // 4, 2048, 1024, 512, 256, 128)):
            if tn % (2 * LANE) and tn != n:
                continue
            for tk in cands(k, (k, k // 2, 2048, 1024, 512)):
                if tk % LANE:
                    continue
                nk = k // tk
                acc = tm * tn * 4 if (nk > 1 and o_bytes != 4) else 0
                temps = tm * min(tn, MM_SUB) * 4 + (tm * tk * 2 if a_bytes == 4 else 0) + (tk * tn * 2 if b_bytes == 4 else 0)
                vmem = 2 * (tm * tk * a_bytes + tk * tn * b_bytes + tm * tn * o_bytes) + acc + temps
                if vmem > MM_VMEM_BUDGET:
                    continue
                steps = (m // tm) * (n // tn) * nk
                cost = (m * k * a_bytes * (n // tn) + k * n * b_bytes * (m // tm) + m * n * o_bytes
                        + steps * MM_STEP_BYTES)
                if best is None or cost < best[0]:
                    best = (cost, tm, tn, tk)
    return best[1:]


def _mm(a, b, *, ta=False, tb=False, out_dtype=F32, name="mm"):
    m, k = (a.shape[1], a.shape[0]) if ta else a.shape
    n = b.shape[0] if tb else b.shape[1]
    o_bytes = jnp.dtype(out_dtype).itemsize
    tm, tn, tk = _mm_tiles(m, n, k, a.dtype.itemsize, b.dtype.itemsize, o_bytes)
    nk = k // tk
    sub = _pick(tn, (MM_SUB, 256))
    use_acc = nk > 1 and o_bytes != 4
    a_spec = pl.BlockSpec((tk, tm), lambda i, j, kk: (kk, i)) if ta else pl.BlockSpec((tm, tk), lambda i, j, kk: (i, kk))
    b_spec = pl.BlockSpec((tn, tk), lambda i, j, kk: (j, kk)) if tb else pl.BlockSpec((tk, tn), lambda i, j, kk: (kk, j))
    dims = (((1,), (1 if tb else 0,)), ((), ()))

    def body(a_ref, b_ref, o_ref, *scratch):
        kk = pl.program_id(2)
        av = a_ref[...].astype(BF16)
        if ta:
            av = av.T
        for s in range(tn // sub):
            cols = slice(s * sub, (s + 1) * sub)
            bv = (b_ref[cols, :] if tb else b_ref[:, cols]).astype(BF16)
            part = lax.dot_general(av, bv, dims, preferred_element_type=F32)
            if nk == 1:
                o_ref[:, cols] = part.astype(out_dtype)
            else:
                acc_ref = scratch[0] if use_acc else o_ref

                @pl.when(kk == 0)
                def _():
                    acc_ref[:, cols] = part

                @pl.when(kk > 0)
                def _():
                    acc_ref[:, cols] += part

                if use_acc:
                    @pl.when(kk == nk - 1)
                    def _():
                        o_ref[:, cols] = acc_ref[:, cols].astype(out_dtype)

    return pl.pallas_call(
        body,
        name=name,
        grid=(m // tm, n // tn, nk),
        in_specs=[a_spec, b_spec],
        out_specs=pl.BlockSpec((tm, tn), lambda i, j, kk: (i, j)),
        out_shape=jax.ShapeDtypeStruct((m, n), out_dtype),
        scratch_shapes=[pltpu.VMEM((tm, tn), F32)] if use_acc else [],
        compiler_params=_params(("parallel", "parallel", "arbitrary")),
    )(a, b)


def _rms_fwd(x, w, *, out_dtype, name):
    t, d = x.shape
    tm = _pick(t, (512, 256, 128))

    def body(x_ref, w_ref, o_ref):
        xv = x_ref[...]
        rstd = lax.rsqrt(jnp.mean(xv * xv, axis=-1, keepdims=True) + EPS)
        o_ref[...] = (xv * rstd * w_ref[...]).astype(out_dtype)

    return pl.pallas_call(
        body,
        name=name,
        grid=(t // tm,),
        in_specs=[pl.BlockSpec((tm, d), lambda i: (i, 0)), pl.BlockSpec((1, d), lambda i: (0, 0))],
        out_specs=pl.BlockSpec((tm, d), lambda i: (i, 0)),
        out_shape=jax.ShapeDtypeStruct((t, d), out_dtype),
        compiler_params=_params(("parallel",)),
    )(x, w)


def _res_rms_fwd(x, f, w, *, name):
    t, d = x.shape
    tm = _pick(t, (512, 256, 128))

    def body(x_ref, f_ref, w_ref, o_ref):
        fv = f_ref[...]
        rstd = lax.rsqrt(jnp.mean(fv * fv, axis=-1, keepdims=True) + EPS)
        o_ref[...] = x_ref[...] + fv * rstd * w_ref[...]

    row = pl.BlockSpec((tm, d), lambda i: (i, 0))
    return pl.pallas_call(
        body,
        name=name,
        grid=(t // tm,),
        in_specs=[row, row, pl.BlockSpec((1, d), lambda i: (0, 0))],
        out_specs=row,
        out_shape=jax.ShapeDtypeStruct((t, d), F32),
        compiler_params=_params(("parallel",)),
    )(x, f, w)


def _rms_bwd(x, w, dy, resid, *, name):
    t, d = x.shape
    tm = _pick(t, (512, 256, 128))
    has_res = resid is not None

    def body(*refs):
        if has_res:
            x_ref, w_ref, dy_ref, r_ref, dx_ref, dw_ref = refs
        else:
            x_ref, w_ref, dy_ref, dx_ref, dw_ref = refs
        xv = x_ref[...]
        dyv = dy_ref[...].astype(F32)
        rstd = lax.rsqrt(jnp.mean(xv * xv, axis=-1, keepdims=True) + EPS)
        xn = xv * rstd
        g = dyv * w_ref[...]
        dx = rstd * (g - xn * jnp.mean(g * xn, axis=-1, keepdims=True))
        if has_res:
            dx = dx + r_ref[...]
        dx_ref[...] = dx
        part = jnp.sum(dyv * xn, axis=0, keepdims=True)

        @pl.when(pl.program_id(0) == 0)
        def _():
            dw_ref[...] = part

        @pl.when(pl.program_id(0) > 0)
        def _():
            dw_ref[...] += part

    row = pl.BlockSpec((tm, d), lambda i: (i, 0))
    vec = pl.BlockSpec((1, d), lambda i: (0, 0))
    ins = [x, w, dy] + ([resid] if has_res else [])
    return pl.pallas_call(
        body,
        name=name,
        grid=(t // tm,),
        in_specs=[row, vec, row] + ([row] if has_res else []),
        out_specs=[row, vec],
        out_shape=[jax.ShapeDtypeStruct((t, d), F32), jax.ShapeDtypeStruct((1, d), F32)],
        compiler_params=_params(("arbitrary",)),
    )(*ins)


def _loss_head(y, target, *, name="loss_head"):
    t, d = y.shape
    tm = _pick(t, (512, 256, 128))

    def body(y_ref, t_ref, dy_ref, l_ref):
        err = y_ref[...] - t_ref[...]
        dy_ref[...] = err * (1.0 / d)
        part = jnp.sum(jnp.sum(err * err, axis=-1, keepdims=True), axis=0, keepdims=True) * (0.5 / d)
        part = jnp.broadcast_to(part, (1, LANE))

        @pl.when(pl.program_id(0) == 0)
        def _():
            l_ref[...] = part

        @pl.when(pl.program_id(0) > 0)
        def _():
            l_ref[...] += part

    row = pl.BlockSpec((tm, d), lambda i: (i, 0))
    dy, l = pl.pallas_call(
        body,
        name=name,
        grid=(t // tm,),
        in_specs=[row, row],
        out_specs=[row, pl.BlockSpec((1, LANE), lambda i: (0, 0))],
        out_shape=[jax.ShapeDtypeStruct((t, d), F32), jax.ShapeDtypeStruct((1, LANE), F32)],
        compiler_params=_params(("arbitrary",)),
    )(y, target)
    return l[0, 0], dy


def _conv_taps(h, w_ref, k_taps):
    out = h * w_ref[k_taps - 1:k_taps, :]
    for k in range(k_taps - 1):
        out = out + _shift_down(h, k_taps - 1 - k) * w_ref[k:k + 1, :]
    return out


def _conv_taps_bwd(h, dhc, w_ref, k_taps):
    dh = dhc * w_ref[k_taps - 1:k_taps, :]
    dws = []
    for k in range(k_taps - 1):
        s = k_taps - 1 - k
        dh = dh + _shift_up(dhc, s) * w_ref[k:k + 1, :]
        dws.append(jnp.sum(dhc * _shift_down(h, s), axis=0, keepdims=True))
    dws.append(jnp.sum(dhc * h, axis=0, keepdims=True))
    return dh, jnp.concatenate(dws, axis=0)


FFN_TC = 256


def _interleave(w, tc=FFN_TC):
    f = w.shape[-1] // 2
    lead = w.shape[:-1]
    return jnp.swapaxes(w.reshape(lead + (2, f // tc, tc)), -3, -2).reshape(lead + (2 * f,))


def _deinterleave(w, tc=FFN_TC):
    f = w.shape[-1] // 2
    lead = w.shape[:-1]
    return jnp.swapaxes(w.reshape(lead + (f // tc, 2, tc)), -3, -2).reshape(lead + (2 * f,))


def _ffn_act_fwd(h, conv_w, conv_b, n_seq, *, name):
    t, f2 = h.shape
    seq = t // n_seq
    tc = FFN_TC
    nj = f2 // (2 * tc)
    k_taps = conv_w.shape[0]

    def body(h_ref, w_ref, b_ref, o_ref):
        hc = _conv_taps(h_ref[...].astype(F32), w_ref, k_taps) + b_ref[...]
        o_ref[...] = (_silu(hc[:, :tc]) * hc[:, tc:]).astype(BF16)

    return pl.pallas_call(
        body,
        name=name,
        grid=(n_seq, nj),
        in_specs=[
            pl.BlockSpec((seq, 2 * tc), lambda b, j: (b, j)),
            pl.BlockSpec((k_taps, 2 * tc), lambda b, j: (0, j)),
            pl.BlockSpec((1, 2 * tc), lambda b, j: (0, j)),
        ],
        out_specs=pl.BlockSpec((seq, tc), lambda b, j: (b, j)),
        out_shape=jax.ShapeDtypeStruct((t, f2 // 2), BF16),
        compiler_params=_params(("parallel", "parallel")),
    )(h, conv_w, conv_b)


def _ffn_act_bwd(h, conv_w, conv_b, da, n_seq, *, name):
    t, f2 = h.shape
    seq = t // n_seq
    tc = FFN_TC
    nj = f2 // (2 * tc)
    k_taps = conv_w.shape[0]

    def body(h_ref, w_ref, b_ref, da_ref, dh_ref, dw_ref, db_ref):
        hv = h_ref[...].astype(F32)
        hc = _conv_taps(hv, w_ref, k_taps) + b_ref[...]
        gate, val = hc[:, :tc], hc[:, tc:]
        dav = da_ref[...].astype(F32)
        dhc = jnp.concatenate([dav * val * _dsilu(gate), dav * _silu(gate)], axis=1)
        dh, dw = _conv_taps_bwd(hv, dhc, w_ref, k_taps)
        dh_ref[...] = dh.astype(BF16)
        db = jnp.sum(dhc, axis=0, keepdims=True)

        @pl.when(pl.program_id(1) == 0)
        def _():
            dw_ref[...] = dw
            db_ref[...] = db

        @pl.when(pl.program_id(1) > 0)
        def _():
            dw_ref[...] += dw
            db_ref[...] += db

    return pl.pallas_call(
        body,
        name=name,
        grid=(nj, n_seq),
        in_specs=[
            pl.BlockSpec((seq, 2 * tc), lambda j, b: (b, j)),
            pl.BlockSpec((k_taps, 2 * tc), lambda j, b: (0, j)),
            pl.BlockSpec((1, 2 * tc), lambda j, b: (0, j)),
            pl.BlockSpec((seq, tc), lambda j, b: (b, j)),
        ],
        out_specs=[
            pl.BlockSpec((seq, 2 * tc), lambda j, b: (b, j)),
            pl.BlockSpec((k_taps, 2 * tc), lambda j, b: (0, j)),
            pl.BlockSpec((1, 2 * tc), lambda j, b: (0, j)),
        ],
        out_shape=[
            jax.ShapeDtypeStruct((t, f2), BF16),
            jax.ShapeDtypeStruct((k_taps, f2), F32),
            jax.ShapeDtypeStruct((1, f2), F32),
        ],
        compiler_params=_params(("parallel", "arbitrary")),
    )(h, conv_w, conv_b, da)


def _window_mixed(u, window):
    s = u
    step = 1
    while step < window:
        s = s + _shift_down(s, step)
        step *= 2
    rows = lax.broadcasted_iota(jnp.int32, u.shape, 0)
    inv_cnt = 1.0 / jnp.minimum(rows + 1, window).astype(F32)
    return s * inv_cnt - u, inv_cnt


def _window_mixed_bwd(dmixed, inv_cnt, window):
    r = dmixed * inv_cnt
    s = r
    step = 1
    while step < window:
        s = s + _shift_up(s, step)
        step *= 2
    return s - dmixed


def _pool_fwd(u, w, scale, n_seq, *, name):
    t, d = u.shape
    seq = t // n_seq
    n_g, dg, _ = w.shape

    def body(u_ref, w_ref, s_ref, o_ref):
        for k, window in enumerate(POOL_WINDOWS):
            @pl.when(pl.program_id(1) == k)
            def _(window=window):
                mixed, _ = _window_mixed(u_ref[...], window)
                pre = jnp.dot(mixed.astype(BF16), w_ref[0].astype(BF16), preferred_element_type=F32)
                o_ref[...] = pre * s_ref[...]

    return pl.pallas_call(
        body,
        name=name,
        grid=(n_seq, n_g),
        in_specs=[
            pl.BlockSpec((seq, dg), lambda b, g: (b, g)),
            pl.BlockSpec((1, dg, dg), lambda b, g: (g, 0, 0)),
            pl.BlockSpec((1, dg), lambda b, g: (0, g)),
        ],
        out_specs=pl.BlockSpec((seq, dg), lambda b, g: (b, g)),
        out_shape=jax.ShapeDtypeStruct((t, d), F32),
        compiler_params=_params(("parallel", "parallel")),
    )(u, w, scale)


def _pool_bwd(u, w, scale, dout, n_seq, *, name):
    t, d = u.shape
    seq = t // n_seq
    n_g, dg, _ = w.shape

    def body(u_ref, w_ref, s_ref, do_ref, du_ref, dw_ref, ds_ref):
        group = pl.program_id(0)
        first = pl.program_id(1) == 0
        for k, window in enumerate(POOL_WINDOWS):
            @pl.when(group == k)
            def _(window=window):
                mixed, inv_cnt = _window_mixed(u_ref[...], window)
                mixed_b = mixed.astype(BF16)
                w_b = w_ref[0].astype(BF16)
                dov = do_ref[...]
                pre = jnp.dot(mixed_b, w_b, preferred_element_type=F32)
                dsc = jnp.sum(dov * pre, axis=0, keepdims=True)
                dpre = (dov * s_ref[...]).astype(BF16)
                dw = lax.dot_general(mixed_b, dpre, (((0,), (0,)), ((), ())), preferred_element_type=F32)
                dmixed = lax.dot_general(dpre, w_b, (((1,), (1,)), ((), ())), preferred_element_type=F32)
                du_ref[...] = _window_mixed_bwd(dmixed, inv_cnt, window)

                @pl.when(first)
                def _():
                    dw_ref[0] = dw
                    ds_ref[...] = dsc

                @pl.when(jnp.logical_not(first))
                def _():
                    dw_ref[0] += dw
                    ds_ref[...] += dsc

    return pl.pallas_call(
        body,
        name=name,
        grid=(n_g, n_seq),
        in_specs=[
            pl.BlockSpec((seq, dg), lambda g, b: (b, g)),
            pl.BlockSpec((1, dg, dg), lambda g, b: (g, 0, 0)),
            pl.BlockSpec((1, dg), lambda g, b: (0, g)),
            pl.BlockSpec((seq, dg), lambda g, b: (b, g)),
        ],
        out_specs=[
            pl.BlockSpec((seq, dg), lambda g, b: (b, g)),
            pl.BlockSpec((1, dg, dg), lambda g, b: (g, 0, 0)),
            pl.BlockSpec((1, dg), lambda g, b: (0, g)),
        ],
        out_shape=[
            jax.ShapeDtypeStruct((t, d), F32),
            jax.ShapeDtypeStruct((n_g, dg, dg), F32),
            jax.ShapeDtypeStruct((1, d), F32),
        ],
        compiler_params=_params(("parallel", "arbitrary")),
    )(u, w, scale, dout)


def _adamw(w, g, m, v, *, name):
    shape = w.shape
    c = shape[-1]
    r = w.size // c
    tm = _pick(r, (512, 256, 128, 64, 32, 16, 8))

    def body(w_ref, g_ref, m_ref, v_ref, d_ref, nm_ref, nv_ref):
        gv = g_ref[...]
        nm = ADAM_B1 * m_ref[...] + (1.0 - ADAM_B1) * gv
        nv = ADAM_B2 * v_ref[...] + (1.0 - ADAM_B2) * (gv * gv)
        m_hat = nm / (1.0 - ADAM_B1 ** ADAM_STEP)
        v_hat = nv / (1.0 - ADAM_B2 ** ADAM_STEP)
        d_ref[...] = -ADAM_LR * (m_hat / (jnp.sqrt(v_hat) + ADAM_EPS) + ADAM_WD * w_ref[...])
        nm_ref[...] = nm
        nv_ref[...] = nv

    blk = pl.BlockSpec((tm, c), lambda i: (i, 0))
    out = jax.ShapeDtypeStruct((r, c), F32)
    res = pl.pallas_call(
        body,
        name=name,
        grid=(r // tm,),
        in_specs=[blk] * 4,
        out_specs=[blk] * 3,
        out_shape=[out] * 3,
        compiler_params=_params(("parallel",)),
    )(w.reshape(r, c), g.reshape(r, c), m.reshape(r, c), v.reshape(r, c))
    return tuple(a.reshape(shape) for a in res)


CONV_TC = 256


def _ssd_conv_fwd(proj, col0, n_cols, conv_w, conv_b, n_seq, *, name):
    t = proj.shape[0]
    seq = t // n_seq
    tc = CONV_TC
    off = col0 // tc
    k_taps = conv_w.shape[0]

    def body(h_ref, w_ref, b_ref, o_ref):
        o_ref[...] = _silu(_conv_taps(h_ref[...], w_ref, k_taps) + b_ref[...])

    return pl.pallas_call(
        body,
        name=name,
        grid=(n_seq, n_cols // tc),
        in_specs=[
            pl.BlockSpec((seq, tc), lambda b, j: (b, j + off)),
            pl.BlockSpec((k_taps, tc), lambda b, j: (0, j)),
            pl.BlockSpec((1, tc), lambda b, j: (0, j)),
        ],
        out_specs=pl.BlockSpec((seq, tc), lambda b, j: (b, j)),
        out_shape=jax.ShapeDtypeStruct((t, n_cols), F32),
        compiler_params=_params(("parallel", "parallel")),
    )(proj, conv_w, conv_b)


def _ssd_conv_bwd(proj, col0, conv_w, conv_b, dparts, dproj, n_seq, *, name):
    t = proj.shape[0]
    seq = t // n_seq
    tc = CONV_TC
    off = col0 // tc
    k_taps = conv_w.shape[0]
    widths = [d.shape[1] // tc for d in dparts]
    starts = [sum(widths[:i]) for i in range(len(widths))]
    n_blocks = sum(widths)
    n_parts = len(dparts)

    def body(h_ref, w_ref, b_ref, *rest):
        part_refs = rest[:n_parts]
        dh_ref, dw_ref, db_ref = rest[n_parts + 1:]
        j = pl.program_id(0)
        da = part_refs[-1][...]
        for i in reversed(range(n_parts - 1)):
            da = jnp.where(j < starts[i + 1], part_refs[i][...], da)
        hv = h_ref[...]
        dhc = da * _dsilu(_conv_taps(hv, w_ref, k_taps) + b_ref[...])
        dh, dw = _conv_taps_bwd(hv, dhc, w_ref, k_taps)
        dh_ref[...] = dh.astype(BF16)
        db = jnp.sum(dhc, axis=0, keepdims=True)

        @pl.when(pl.program_id(1) == 0)
        def _():
            dw_ref[...] = dw
            db_ref[...] = db

        @pl.when(pl.program_id(1) > 0)
        def _():
            dw_ref[...] += dw
            db_ref[...] += db

    def part_spec(start, width):
        return pl.BlockSpec((seq, tc), lambda j, b: (b, jnp.clip(j - start, 0, width - 1)))

    n_cols = n_blocks * tc
    return pl.pallas_call(
        body,
        name=name,
        grid=(n_blocks, n_seq),
        in_specs=[
            pl.BlockSpec((seq, tc), lambda j, b: (b, j + off)),
            pl.BlockSpec((k_taps, tc), lambda j, b: (0, j)),
            pl.BlockSpec((1, tc), lambda j, b: (0, j)),
        ] + [part_spec(st, wd) for st, wd in zip(starts, widths)] + [ANY],
        out_specs=[
            pl.BlockSpec((seq, tc), lambda j, b: (b, j + off)),
            pl.BlockSpec((k_taps, tc), lambda j, b: (0, j)),
            pl.BlockSpec((1, tc), lambda j, b: (0, j)),
        ],
        out_shape=[
            jax.ShapeDtypeStruct(dproj.shape, BF16),
            jax.ShapeDtypeStruct((k_taps, n_cols), F32),
            jax.ShapeDtypeStruct((1, n_cols), F32),
        ],
        input_output_aliases={3 + n_parts: 0},
        compiler_params=_params(("parallel", "arbitrary")),
    )(proj, conv_w, conv_b, *dparts, dproj)


def _fill_cols(buf, src, col0, *, name):
    t, c = src.shape
    tm = _pick(t, (1024, 512, 256, 128))

    def body(s_ref, b_ref, o_ref):
        o_ref[...] = s_ref[...].astype(o_ref.dtype)

    return pl.pallas_call(
        body,
        name=name,
        grid=(t // tm,),
        in_specs=[pl.BlockSpec((tm, c), lambda i: (i, 0)), ANY],
        out_specs=pl.BlockSpec((tm, c), lambda i: (i, col0 // c)),
        out_shape=jax.ShapeDtypeStruct(buf.shape, buf.dtype),
        input_output_aliases={1: 0},
        compiler_params=_params(("parallel",)),
    )(src, buf)


def _softplus(x):
    return jnp.maximum(x, 0.0) + jnp.log(1.0 + jnp.exp(-jnp.abs(x)))


def _chunk_decay(dtraw, bias, alog):
    q = dtraw.shape[0]
    dt = _softplus(dtraw + bias)
    a = -jnp.exp(alog)
    rows = lax.broadcasted_iota(jnp.int32, (q, q), 0)
    cols = lax.broadcasted_iota(jnp.int32, (q, q), 1)
    lower = rows >= cols
    acum = jnp.dot(lower.astype(F32), dt * a, precision=lax.Precision.HIGHEST, preferred_element_type=F32)
    return dt, a, acum, acum.T, lower


def _dot_exact(v, sel):
    hi = v.astype(BF16)
    r1 = v - hi.astype(F32)
    mid = r1.astype(BF16)
    lo = (r1 - mid.astype(F32)).astype(BF16)
    return (jnp.dot(hi, sel, preferred_element_type=F32) + jnp.dot(mid, sel, preferred_element_type=F32)
            + jnp.dot(lo, sel, preferred_element_type=F32))


def _head_selectors(gw, p):
    sum_heads = (lax.broadcasted_iota(jnp.int32, (gw, LANE), 0) // p == lax.broadcasted_iota(jnp.int32, (gw, LANE), 1))
    spread = (lax.broadcasted_iota(jnp.int32, (LANE, gw), 0) == lax.broadcasted_iota(jnp.int32, (LANE, gw), 1) // p)
    return sum_heads.astype(BF16), spread.astype(BF16)


def _row_spread(v, spread):
    return _dot_exact(jnp.broadcast_to(v, (8, v.shape[1])), spread)[0:1, :]


def _head_pad(v, r_heads):
    lead = v.shape[:-1]
    vg = v.reshape(lead + (N_SSD_GROUPS, r_heads))
    vg = jnp.pad(vg, [(0, 0)] * len(lead) + [(0, 0), (0, LANE - r_heads)])
    out = vg.reshape(lead + (N_SSD_GROUPS * LANE,))
    return out[None] if out.ndim == 1 else out


def _head_unpad(v, r_heads):
    lead = v.shape[:-1]
    out = v.reshape(lead + (N_SSD_GROUPS, LANE))[..., :r_heads].reshape(lead + (N_SSD_GROUPS * r_heads,))
    return out[0] if (len(lead) == 1 and lead[0] == 1) else out


def _ssd_w_in_layout(w_in, d_inner, d_xbc, r_heads):
    main = w_in[:, :d_inner + d_xbc]
    return jnp.concatenate([main, _head_pad(w_in[:, d_inner + d_xbc:], r_heads)], axis=1)


def _ssd_w_in_unlayout(w, d_inner, d_xbc, r_heads):
    main = w[:, :d_inner + d_xbc]
    return jnp.concatenate([main, _head_unpad(w[:, d_inner + d_xbc:], r_heads)], axis=1)


def _ssd_dims(proj, xbc):
    d_xbc = xbc.shape[1]
    d_inner = d_xbc - 2 * N_SSD_GROUPS * D_STATE
    gw = d_inner // N_SSD_GROUPS
    return d_inner, d_xbc, gw, gw // HEAD_DIM


def _ssd_fwd(proj, xbc, bias_p, alog_p, dskip_p, norm_w, n_seq, *, name):
    t = proj.shape[0]
    d_inner, d_xbc, gw, r_heads = _ssd_dims(proj, xbc)
    q, n, n_g, p = CHUNK, D_STATE, N_SSD_GROUPS, HEAD_DIM
    seq = t // n_seq
    nc = seq // q
    dt_blk0 = (d_inner + d_xbc) // LANE

    def body(x_ref, b_ref, c_ref, z_ref, dtr_ref, bias_ref, alog_ref, dsk_ref, nw_ref, yn_ref, y_ref, hs_ref, h_scr):
        @pl.when(pl.program_id(2) == 0)
        def _():
            h_scr[...] = jnp.zeros_like(h_scr)

        dt, a, acum, acum_t, lower = _chunk_decay(dtr_ref[...], bias_ref[...], alog_ref[...])
        x = x_ref[...]
        bb = b_ref[...].astype(BF16)
        cb = c_ref[...].astype(BF16)
        g_mat = lax.dot_general(cb, bb, (((1,), (1,)), ((), ())), preferred_element_type=F32)
        h_prev = h_scr[...]
        hs_ref[...] = h_prev
        c_h = jnp.dot(cb, h_prev.astype(BF16), preferred_element_type=F32)
        _, spread = _head_selectors(gw, p)
        acum_s = _dot_exact(acum, spread)
        a_last_s = acum_s[q - 1:q, :]
        xdt = x * _dot_exact(dt, spread)
        xdt_b = xdt.astype(BF16)
        ys = []
        for h in range(r_heads):
            decay = jnp.exp(jnp.where(lower, acum[:, h:h + 1] - acum_t[h:h + 1, :], -jnp.inf))
            ys.append(jnp.dot((g_mat * decay).astype(BF16), xdt_b[:, h * p:(h + 1) * p], preferred_element_type=F32))
        y = jnp.concatenate(ys, axis=1) + jnp.exp(acum_s) * c_h + _row_spread(dsk_ref[...], spread) * x
        xd = xdt * jnp.exp(a_last_s - acum_s)
        states = lax.dot_general(bb, xd.astype(BF16), (((0,), (0,)), ((), ())), preferred_element_type=F32)
        h_scr[...] = h_prev * jnp.exp(a_last_s) + states
        y_ref[...] = y
        gated = y * _silu(z_ref[...])
        rstd = lax.rsqrt(jnp.mean(gated * gated, axis=-1, keepdims=True) + EPS)
        yn_ref[...] = (gated * rstd * nw_ref[...]).astype(BF16)

    row = lambda b, g, c: b * nc + c
    vec = pl.BlockSpec((1, LANE), lambda b, g, c: (0, g))
    return pl.pallas_call(
        body,
        name=name,
        grid=(n_seq, n_g, nc),
        in_specs=[
            pl.BlockSpec((q, gw), lambda b, g, c: (row(b, g, c), g)),
            pl.BlockSpec((q, n), lambda b, g, c: (row(b, g, c), d_inner // n + g)),
            pl.BlockSpec((q, n), lambda b, g, c: (row(b, g, c), d_inner // n + n_g + g)),
            pl.BlockSpec((q, gw), lambda b, g, c: (row(b, g, c), g)),
            pl.BlockSpec((q, LANE), lambda b, g, c: (row(b, g, c), dt_blk0 + g)),
            vec, vec, vec,
            pl.BlockSpec((1, gw), lambda b, g, c: (0, g)),
        ],
        out_specs=[
            pl.BlockSpec((q, gw), lambda b, g, c: (row(b, g, c), g)),
            pl.BlockSpec((q, gw), lambda b, g, c: (row(b, g, c), g)),
            pl.BlockSpec((n, gw), lambda b, g, c: (row(b, g, c), g)),
        ],
        out_shape=[
            jax.ShapeDtypeStruct((t, d_inner), BF16),
            jax.ShapeDtypeStruct((t, d_inner), F32),
            jax.ShapeDtypeStruct((n_seq * nc * n, d_inner), F32),
        ],
        scratch_shapes=[pltpu.VMEM((n, gw), F32)],
        compiler_params=_params(("parallel", "parallel", "arbitrary")),
    )(xbc, xbc, xbc, proj, proj, bias_p, alog_p, dskip_p, norm_w)


def _ssd_bwd(proj, xbc, hs, y, dyn, bias_p, alog_p, dskip_p, norm_w, n_seq, *, name):
    t = proj.shape[0]
    d_inner, d_xbc, gw, r_heads = _ssd_dims(proj, xbc)
    q, n, n_g, p = CHUNK, D_STATE, N_SSD_GROUPS, HEAD_DIM
    seq = t // n_seq
    nc = seq // q
    dt_blk0 = (d_inner + d_xbc) // LANE

    def body(x_ref, b_ref, c_ref, z_ref, dtr_ref, bias_ref, alog_ref, dsk_ref, nw_ref, hs_ref, y_ref, dyn_ref,
             dx_ref, db_ref, dc_ref, dz_ref, ddtr_ref, dnw_ref, dbias_ref, dalog_ref, ddsk_ref, dh_scr):
        first = jnp.logical_and(pl.program_id(1) == 0, pl.program_id(2) == 0)

        @pl.when(pl.program_id(2) == 0)
        def _():
            dh_scr[...] = jnp.zeros_like(dh_scr)

        dtraw = dtr_ref[...]
        dt, a, acum, acum_t, lower = _chunk_decay(dtraw, bias_ref[...], alog_ref[...])
        x = x_ref[...]
        bb = b_ref[...].astype(BF16)
        cb = c_ref[...].astype(BF16)
        g_mat = lax.dot_general(cb, bb, (((1,), (1,)), ((), ())), preferred_element_type=F32)

        yv = y_ref[...]
        z = z_ref[...]
        sz = _silu(z)
        gated = yv * sz
        rstd = lax.rsqrt(jnp.mean(gated * gated, axis=-1, keepdims=True) + EPS)
        gn = gated * rstd
        dynv = dyn_ref[...]
        gwt = dynv * nw_ref[...]
        dgated = rstd * (gwt - gn * jnp.mean(gwt * gn, axis=-1, keepdims=True))
        dnw = jnp.sum(dynv * gn, axis=0, keepdims=True)
        dy = dgated * sz
        dz_ref[...] = (dgated * yv * _dsilu(z)).astype(BF16)

        h_prev = hs_ref[...]
        h_prev_b = h_prev.astype(BF16)
        ds = dh_scr[...]
        ds_b = ds.astype(BF16)
        sum_heads, spread = _head_selectors(gw, p)
        acum_s = _dot_exact(acum, spread)
        a_last_s = acum_s[q - 1:q, :]
        dt_s = _dot_exact(dt, spread)
        dsk_s = _row_spread(dsk_ref[...], spread)
        dte_s = jnp.exp(a_last_s - acum_s)
        cd_s = jnp.exp(a_last_s)
        xdt = x * dt_s
        xdt_b = xdt.astype(BF16)
        dy_b = dy.astype(BF16)
        gt_mat = lax.dot_general(bb, cb, (((1,), (1,)), ((), ())), preferred_element_type=F32)
        upper = lax.broadcasted_iota(jnp.int32, (q, q), 0) <= lax.broadcasted_iota(jnp.int32, (q, q), 1)
        dg = jnp.zeros((q, q), F32)
        dxdts, w_diffs = [], []
        for h in range(r_heads):
            hsl = slice(h * p, (h + 1) * p)
            diff = acum[:, h:h + 1] - acum_t[h:h + 1, :]
            decay = jnp.exp(jnp.where(lower, diff, -jnp.inf))
            decay_t = jnp.exp(jnp.where(upper, -diff, -jnp.inf))
            mt_mat = gt_mat * decay_t
            dm = lax.dot_general(dy_b[:, hsl], xdt_b[:, hsl], (((1,), (1,)), ((), ())), preferred_element_type=F32)
            dm_t = lax.dot_general(xdt_b[:, hsl], dy_b[:, hsl], (((1,), (1,)), ((), ())), preferred_element_type=F32)
            dg = dg + dm * decay
            dxdts.append(jnp.dot(mt_mat.astype(BF16), dy_b[:, hsl], preferred_element_type=F32))
            w_diffs.append(dm * (g_mat * decay) - dm_t * mt_mat)
        sel_q = (lax.broadcasted_iota(jnp.int32, (r_heads * q, LANE), 0) // q
                 == lax.broadcasted_iota(jnp.int32, (r_heads * q, LANE), 1)).astype(BF16)
        dacum_diag = _dot_exact(jnp.concatenate(w_diffs, axis=1), sel_q)
        c_h = jnp.dot(cb, h_prev_b, preferred_element_type=F32)
        dxd = jnp.dot(bb, ds_b, preferred_element_type=F32)
        dxdt = jnp.concatenate(dxdts, axis=1) + dxd * dte_s
        dye = dy * jnp.exp(acum_s)
        dye_b = dye.astype(BF16)
        xd = xdt * dte_s
        xd_b = xd.astype(BF16)
        dg_b = dg.astype(BF16)
        dx_ref[...] = dxdt * dt_s + dsk_s * dy
        dc_ref[...] = (jnp.dot(dg_b, bb, preferred_element_type=F32)
                       + lax.dot_general(dye_b, h_prev_b, (((1,), (1,)), ((), ())), preferred_element_type=F32))
        db_ref[...] = (lax.dot_general(dg_b, cb, (((0,), (0,)), ((), ())), preferred_element_type=F32)
                       + lax.dot_general(xd_b, ds_b, (((1,), (1,)), ((), ())), preferred_element_type=F32))
        dh_scr[...] = ds * cd_s + lax.dot_general(cb, dye_b, (((0,), (0,)), ((), ())), preferred_element_type=F32)
        ddt_cols = _dot_exact(x * dxdt, sum_heads)
        dacum_y = _dot_exact(dye * c_h - dxd * xd, sum_heads)
        col_sums = jnp.concatenate([
            jnp.sum(dxd * xd, axis=0, keepdims=True) + jnp.sum(ds * h_prev, axis=0, keepdims=True) * cd_s,
            jnp.sum(dy * x, axis=0, keepdims=True),
            jnp.zeros((6, gw), F32)], axis=0)
        col_sums = _dot_exact(col_sums, sum_heads)
        ddsk = col_sums[1:2, :]
        rows_q = lax.broadcasted_iota(jnp.int32, (q, LANE), 0)
        dacum = dacum_diag + dacum_y + jnp.where(rows_q == q - 1, col_sums[0:1, :], 0.0)
        dadt = jnp.dot(upper.astype(F32), dacum, precision=lax.Precision.HIGHEST, preferred_element_type=F32)
        ddt = dadt * a + ddt_cols
        ddtr = ddt * _sigmoid(dtraw + bias_ref[...])
        ddtr_ref[...] = ddtr
        dbias = jnp.sum(ddtr, axis=0, keepdims=True)
        dalog = jnp.sum(dadt * dt, axis=0, keepdims=True) * a

        @pl.when(first)
        def _():
            dnw_ref[...] = dnw
            dbias_ref[...] = dbias
            dalog_ref[...] = dalog
            ddsk_ref[...] = ddsk

        @pl.when(jnp.logical_not(first))
        def _():
            dnw_ref[...] += dnw
            dbias_ref[...] += dbias
            dalog_ref[...] += dalog
            ddsk_ref[...] += ddsk

    row = lambda g, b, c: b * nc + (nc - 1 - c)
    vec = pl.BlockSpec((1, LANE), lambda g, b, c: (0, g))
    wide = pl.BlockSpec((q, gw), lambda g, b, c: (row(g, b, c), g))
    narrow = pl.BlockSpec((q, n), lambda g, b, c: (row(g, b, c), g))
    return pl.pallas_call(
        body,
        name=name,
        grid=(n_g, n_seq, nc),
        in_specs=[
            wide,
            pl.BlockSpec((q, n), lambda g, b, c: (row(g, b, c), d_inner // n + g)),
            pl.BlockSpec((q, n), lambda g, b, c: (row(g, b, c), d_inner // n + n_g + g)),
            wide,
            pl.BlockSpec((q, LANE), lambda g, b, c: (row(g, b, c), dt_blk0 + g)),
            vec, vec, vec,
            pl.BlockSpec((1, gw), lambda g, b, c: (0, g)),
            pl.BlockSpec((n, gw), lambda g, b, c: (row(g, b, c), g)),
            wide, wide,
        ],
        out_specs=[
            wide, narrow, narrow, wide, narrow,
            pl.BlockSpec((1, gw), lambda g, b, c: (0, g)),
            vec, vec, vec,
        ],
        out_shape=[
            jax.ShapeDtypeStruct((t, d_inner), F32),
            jax.ShapeDtypeStruct((t, n_g * n), F32),
            jax.ShapeDtypeStruct((t, n_g * n), F32),
            jax.ShapeDtypeStruct(proj.shape, BF16),
            jax.ShapeDtypeStruct((t, n_g * LANE), F32),
            jax.ShapeDtypeStruct((1, d_inner), F32),
            jax.ShapeDtypeStruct((1, n_g * LANE), F32),
            jax.ShapeDtypeStruct((1, n_g * LANE), F32),
            jax.ShapeDtypeStruct((1, n_g * LANE), F32),
        ],
        scratch_shapes=[pltpu.VMEM((n, gw), F32)],
        compiler_params=_params(("parallel", "arbitrary", "arbitrary")),
    )(xbc, xbc, xbc, proj, proj, bias_p, alog_p, dskip_p, norm_w, hs, y, dyn)


MESH_IDS = pl.DeviceIdType.MESH


def _my_index():
    return 4 * lax.axis_index("x") + 2 * lax.axis_index("y") + lax.axis_index("c")


def _all_gather(shard, *, name):
    def body(x_ref, out_ref, send_sems, recv_sems, local_sem):
        x, y, c = lax.axis_index("x"), lax.axis_index("y"), lax.axis_index("c")
        me, sibling = (x, y, c), (x, y, 1 - c)
        chips = [(1 - x, y), (x, 1 - y), (1 - x, 1 - y)]

        def blk(px, py, pc):
            return out_ref.at[4 * px + 2 * py + pc]

        def copy(k, block, to, src=None):
            return pltpu.make_async_remote_copy(
                src_ref=blk(*block) if src is None else src, dst_ref=blk(*block),
                send_sem=send_sems.at[k], recv_sem=recv_sems.at[k], device_id=to, device_id_type=MESH_IDS)

        mine = pltpu.make_async_copy(x_ref, blk(*me), local_sem)
        mine.start()
        first = [copy(0, me, sibling, src=x_ref)]
        first += [copy(1 + j, me, (*chip, c), src=x_ref) for j, chip in enumerate(chips)]
        for cp in first:
            cp.start()
        passed = [copy(4 + j, (*chip, c), sibling) for j, chip in enumerate(chips)]
        for j, chip in enumerate(chips):
            copy(1 + j, (*chip, c), me).wait_recv()
            passed[j].start()
        copy(0, sibling, me).wait_recv()
        for j, chip in enumerate(chips):
            copy(4 + j, (*chip, 1 - c), me).wait_recv()
        for cp in first + passed:
            cp.wait_send()
        mine.wait()

    return pl.pallas_call(
        body,
        name=name,
        in_specs=[ANY],
        out_specs=ANY,
        out_shape=jax.ShapeDtypeStruct((N_DEV,) + shard.shape, shard.dtype),
        scratch_shapes=[pltpu.SemaphoreType.DMA((7,)), pltpu.SemaphoreType.DMA((7,)), pltpu.SemaphoreType.DMA],
    )(shard)


def _exchange(fulls, *, name):
    n_arr = len(fulls)

    def body(*refs):
        in_refs, out_refs = refs[:n_arr], refs[n_arr:2 * n_arr]
        send_sems, recv_sems, local_sems = refs[2 * n_arr:]
        x, y, c = lax.axis_index("x"), lax.axis_index("y"), lax.axis_index("c")
        me = 4 * x + 2 * y + c
        mine = [pltpu.make_async_copy(in_refs[a].at[me], out_refs[a].at[me], local_sems.at[a]) for a in range(n_arr)]
        for cp in mine:
            cp.start()
        copies = []
        for a in range(n_arr):
            for k in range(1, N_DEV):
                px = 1 - x if k & 4 else x
                py = 1 - y if k & 2 else y
                pc = 1 - c if k & 1 else c
                peer = 4 * px + 2 * py + pc
                sem = a * (N_DEV - 1) + k - 1
                copies.append((
                    pltpu.make_async_remote_copy(
                        src_ref=in_refs[a].at[peer], dst_ref=out_refs[a].at[me], send_sem=send_sems.at[sem],
                        recv_sem=recv_sems.at[sem], device_id=(px, py, pc), device_id_type=MESH_IDS),
                    pltpu.make_async_remote_copy(
                        src_ref=in_refs[a].at[peer], dst_ref=out_refs[a].at[peer], send_sem=send_sems.at[sem],
                        recv_sem=recv_sems.at[sem], device_id=(px, py, pc), device_id_type=MESH_IDS)))
        for send, _ in copies:
            send.start()
        for send, recv in copies:
            recv.wait_recv()
            send.wait_send()
        for cp in mine:
            cp.wait()

    n_sem = n_arr * (N_DEV - 1)
    return pl.pallas_call(
        body,
        name=name,
        in_specs=[ANY] * n_arr,
        out_specs=[ANY] * n_arr,
        out_shape=[jax.ShapeDtypeStruct(f.shape, f.dtype) for f in fulls],
        scratch_shapes=[pltpu.SemaphoreType.DMA((n_sem,)), pltpu.SemaphoreType.DMA((n_sem,)),
                        pltpu.SemaphoreType.DMA((n_arr,))],
    )(*fulls)


def _sum_slots(parts, *, name):
    shape = parts.shape[1:]
    n, c = parts.shape[0], parts.shape[-1]
    r = parts.size // (n * c)
    tm = _pick(r, (256, 128, 64, 32, 16, 8))

    def body(p_ref, o_ref):
        acc = p_ref[0].astype(F32)
        for s in range(1, n):
            acc = acc + p_ref[s].astype(F32)
        o_ref[...] = acc

    return pl.pallas_call(
        body,
        name=name,
        grid=(r // tm,),
        in_specs=[pl.BlockSpec((n, tm, c), lambda i: (0, i, 0))],
        out_specs=pl.BlockSpec((tm, c), lambda i: (i, 0)),
        out_shape=jax.ShapeDtypeStruct((r, c), F32),
        compiler_params=_params(("parallel",)),
    )(parts.reshape(n, r, c)).reshape(shape)


def _row_count(shape):
    c = shape[-1]
    rows = 1
    for s in shape[:-1]:
        rows *= s
    return rows, c, c + (-c) % LANE


def _pack_rows(arrays):
    pieces = []
    for a in arrays:
        rows, c, cp = _row_count(a.shape)
        a2 = a.reshape(rows, c)
        if cp > c:
            a2 = jnp.pad(a2, ((0, 0), (0, cp - c)))
        pieces.append(a2.reshape(rows * cp // LANE, LANE))
    total = sum(p.shape[0] for p in pieces)
    if total % 8:
        pieces.append(jnp.zeros((8 - total % 8, LANE), F32))
    return jnp.concatenate(pieces, axis=0)


def _unpack_rows(packed, shapes, lead=()):
    out, off = [], 0
    for shp in shapes:
        rows, c, cp = _row_count(shp)
        n_rows = rows * cp // LANE
        seg = packed[..., off:off + n_rows, :].reshape(lead + (rows, cp))
        out.append(seg[..., :c].reshape(lead + tuple(shp)))
        off += n_rows
    return out


def _unshard(stacked, axis):
    moved = jnp.moveaxis(stacked, 0, axis)
    shp = moved.shape
    return moved.reshape(shp[:axis] + (shp[axis] * shp[axis + 1],) + shp[axis + 2:])


def _shard_major(full, axis):
    shp = full.shape
    split = full.reshape(shp[:axis] + (N_DEV, shp[axis] // N_DEV) + shp[axis + 1:])
    return jnp.moveaxis(split, axis, 0)


def _my_shard(full, axis):
    size = full.shape[axis] // N_DEV
    return lax.dynamic_slice_in_dim(full, _my_index() * size, size, axis)


def _local_step(x, target, w, n_seq):
    depth = w["norm_mix_pre"].shape[0]
    d_inner = w["ssd_w_out"][0].shape[0]
    d_xbc = w["ssd_conv_w"].shape[2]
    saved = []
    for i in range(depth):
        j = i // 2
        s = {"x": x}
        if i % 2 == 0:
            u = _rms_fwd(x, w["norm_mix_pre"][i:i + 1], out_dtype=BF16, name=f"l{i}_mix_pre")
            proj = _mm(u, w["ssd_w_in"][j], name=f"l{i}_ssd_in")
            xbc = _ssd_conv_fwd(proj, d_inner, d_xbc, w["ssd_conv_w"][j], w["ssd_conv_b"][j:j + 1], n_seq,
                                name=f"l{i}_ssd_conv")
            yn, y, hs = _ssd_fwd(proj, xbc, w["ssd_dt_bias"][j:j + 1], w["ssd_a_log"][j:j + 1], w["ssd_d"][j:j + 1],
                                 w["ssd_norm_w"][j:j + 1], n_seq, name=f"l{i}_ssd_scan")
            mix = _mm(yn, w["ssd_w_out"][j], name=f"l{i}_ssd_out")
            s.update(u=u, proj=proj, xbc=xbc, yn=yn, y=y, hs=hs)
        else:
            u = _rms_fwd(x, w["norm_mix_pre"][i:i + 1], out_dtype=F32, name=f"l{i}_mix_pre")
            mix = _pool_fwd(u, w["pool_w"][j], w["pool_scale"][j:j + 1], n_seq, name=f"l{i}_pool")
            s.update(u=u)
        x1 = _res_rms_fwd(x, mix, w["norm_mix_post"][i:i + 1], name=f"l{i}_mix_post")
        n = _rms_fwd(x1, w["norm_ffn_pre"][i:i + 1], out_dtype=BF16, name=f"l{i}_ffn_pre")
        h = _mm(n, w["ffn_w_up"][i], out_dtype=BF16, name=f"l{i}_ffn_up")
        a = _ffn_act_fwd(h, w["ffn_conv_w"][i], w["ffn_conv_b"][i:i + 1], n_seq, name=f"l{i}_ffn_act")
        f = _mm(a, w["ffn_w_down"][i], name=f"l{i}_ffn_down")
        x = _res_rms_fwd(x1, f, w["norm_ffn_post"][i:i + 1], name=f"l{i}_ffn_post")
        s.update(mix=mix, x1=x1, n=n, h=h, a=a, f=f)
        saved.append(s)

    loss, dx = _loss_head(x, target)
    grads = {k: [None] * len(v) for k, v in w.items()}
    for i in reversed(range(depth)):
        j = i // 2
        s = saved[i]
        df, grads["norm_ffn_post"][i] = _rms_bwd(s["f"], w["norm_ffn_post"][i:i + 1], dx, None, name=f"l{i}_ffn_post_b")
        da = _mm(df, w["ffn_w_down"][i], tb=True, out_dtype=BF16, name=f"l{i}_ffn_down_bx")
        grads["ffn_w_down"][i] = _mm(s["a"], df, ta=True, name=f"l{i}_ffn_down_bw")
        dh, grads["ffn_conv_w"][i], grads["ffn_conv_b"][i] = _ffn_act_bwd(
            s["h"], w["ffn_conv_w"][i], w["ffn_conv_b"][i:i + 1], da, n_seq, name=f"l{i}_ffn_act_b")
        dn = _mm(dh, w["ffn_w_up"][i], tb=True, name=f"l{i}_ffn_up_bx")
        grads["ffn_w_up"][i] = _mm(s["n"], dh, ta=True, name=f"l{i}_ffn_up_bw")
        dx1, grads["norm_ffn_pre"][i] = _rms_bwd(s["x1"], w["norm_ffn_pre"][i:i + 1], dn, dx, name=f"l{i}_ffn_pre_b")
        dmix, grads["norm_mix_post"][i] = _rms_bwd(s["mix"], w["norm_mix_post"][i:i + 1], dx1, None,
                                                   name=f"l{i}_mix_post_b")
        if i % 2 == 0:
            dyn = _mm(dmix, w["ssd_w_out"][j], tb=True, name=f"l{i}_ssd_out_bx")
            grads["ssd_w_out"][j] = _mm(s["yn"], dmix, ta=True, name=f"l{i}_ssd_out_bw")
            dxs, db, dc, dz, ddtr, dnw, dbias, dalog, ddsk = _ssd_bwd(
                s["proj"], s["xbc"], s["hs"], s["y"], dyn, w["ssd_dt_bias"][j:j + 1], w["ssd_a_log"][j:j + 1],
                w["ssd_d"][j:j + 1], w["ssd_norm_w"][j:j + 1], n_seq, name=f"l{i}_ssd_scan_b")
            grads["ssd_norm_w"][j], grads["ssd_dt_bias"][j], grads["ssd_a_log"][j], grads["ssd_d"][j] = (
                dnw, dbias, dalog, ddsk)
            dproj, grads["ssd_conv_w"][j], grads["ssd_conv_b"][j] = _ssd_conv_bwd(
                s["proj"], d_inner, w["ssd_conv_w"][j], w["ssd_conv_b"][j:j + 1], (dxs, db, dc), dz, n_seq,
                name=f"l{i}_ssd_conv_b")
            dproj = _fill_cols(dproj, ddtr, d_inner + d_xbc, name=f"l{i}_ssd_dt_b")
            du = _mm(dproj, w["ssd_w_in"][j], tb=True, name=f"l{i}_ssd_in_bx")
            grads["ssd_w_in"][j] = _mm(s["u"], dproj, ta=True, name=f"l{i}_ssd_in_bw")
        else:
            du, grads["pool_w"][j], grads["pool_scale"][j] = _pool_bwd(
                s["u"], w["pool_w"][j], w["pool_scale"][j:j + 1], dmix, n_seq, name=f"l{i}_pool_b")
        dx, grads["norm_mix_pre"][i] = _rms_bwd(s["x"], w["norm_mix_pre"][i:i + 1], du, dx1, name=f"l{i}_mix_pre_b")
    return loss, dx, grads


BIG = (("ssd_w_in", 2), ("ssd_w_out", 1), ("pool_w", 2), ("ffn_w_up", 2), ("ffn_w_down", 1))
SMALL_SHARDED = (("ssd_conv_w", 2), ("ffn_conv_w", 2), ("pool_scale", 1))
SMALL = ("ssd_conv_w", "ssd_conv_b", "ssd_dt_bias", "ssd_a_log", "ssd_d", "ssd_norm_w", "pool_scale", "ffn_conv_w",
         "ffn_conv_b", "norm_mix_pre", "norm_mix_post", "norm_ffn_pre", "norm_ffn_post")
WEIGHTS = ("ssd_w_in", "ssd_conv_w", "ssd_conv_b", "ssd_dt_bias", "ssd_a_log", "ssd_d", "ssd_norm_w", "ssd_w_out",
           "pool_w", "pool_scale", "ffn_w_up", "ffn_conv_w", "ffn_conv_b", "ffn_w_down", "norm_mix_pre",
           "norm_mix_post", "norm_ffn_pre", "norm_ffn_post")


def _ssd_sizes(d_inner):
    return d_inner + 2 * N_SSD_GROUPS * D_STATE, d_inner // HEAD_DIM // N_SSD_GROUPS


def _compute_layout(full):
    d_inner = full["ssd_w_out"][0].shape[0]
    d_xbc, r_heads = _ssd_sizes(d_inner)
    w = dict(full)
    w["ssd_w_in"] = [_ssd_w_in_layout(a, d_inner, d_xbc, r_heads) for a in full["ssd_w_in"]]
    w["ffn_w_up"] = [_interleave(a) for a in full["ffn_w_up"]]
    for k in ("ssd_dt_bias", "ssd_a_log", "ssd_d"):
        w[k] = _head_pad(full[k], r_heads)
    for k in ("ffn_conv_w", "ffn_conv_b"):
        w[k] = _interleave(full[k])
    return w


def _matmul_grad_reference_layout(k, g, d_inner):
    d_xbc, r_heads = _ssd_sizes(d_inner)
    if k == "ssd_w_in":
        return _ssd_w_in_unlayout(g, d_inner, d_xbc, r_heads)
    if k == "ffn_w_up":
        return _deinterleave(g)
    return g


def _small_grads_reference_layout(grads, shapes, d_inner):
    _, r_heads = _ssd_sizes(d_inner)
    g = {k: jnp.stack(grads[k]) for k in SMALL}
    for k in ("ssd_dt_bias", "ssd_a_log", "ssd_d"):
        g[k] = _head_unpad(g[k][:, 0], r_heads)
    for k in ("ffn_conv_w", "ffn_conv_b"):
        g[k] = _deinterleave(g[k])
    return {k: v.reshape(shapes[k]) for k, v in g.items()}


def kernel(x, ssd_w_in, ssd_conv_w, ssd_conv_b, ssd_dt_bias, ssd_a_log, ssd_d, ssd_norm_w, ssd_w_out, pool_w, pool_scale, ffn_w_up, ffn_conv_w, ffn_conv_b, ffn_w_down, norm_mix_pre, norm_mix_post, norm_ffn_pre, norm_ffn_post, loss_target, m_ssd_w_in, m_ssd_conv_w, m_ssd_conv_b, m_ssd_dt_bias, m_ssd_a_log, m_ssd_d, m_ssd_norm_w, m_ssd_w_out, m_pool_w, m_pool_scale, m_ffn_w_up, m_ffn_conv_w, m_ffn_conv_b, m_ffn_w_down, m_norm_mix_pre, m_norm_mix_post, m_norm_ffn_pre, m_norm_ffn_post, v_ssd_w_in, v_ssd_conv_w, v_ssd_conv_b, v_ssd_dt_bias, v_ssd_a_log, v_ssd_d, v_ssd_norm_w, v_ssd_w_out, v_pool_w, v_pool_scale, v_ffn_w_up, v_ffn_conv_w, v_ffn_conv_b, v_ffn_w_down, v_norm_mix_pre, v_norm_mix_post, v_norm_ffn_pre, v_norm_ffn_post):
    shards = dict(ssd_w_in=ssd_w_in, ssd_conv_w=ssd_conv_w, ssd_conv_b=ssd_conv_b, ssd_dt_bias=ssd_dt_bias,
                  ssd_a_log=ssd_a_log, ssd_d=ssd_d, ssd_norm_w=ssd_norm_w, ssd_w_out=ssd_w_out, pool_w=pool_w,
                  pool_scale=pool_scale, ffn_w_up=ffn_w_up, ffn_conv_w=ffn_conv_w, ffn_conv_b=ffn_conv_b,
                  ffn_w_down=ffn_w_down, norm_mix_pre=norm_mix_pre, norm_mix_post=norm_mix_post,
                  norm_ffn_pre=norm_ffn_pre, norm_ffn_post=norm_ffn_post)
    moments_m = dict(zip(WEIGHTS, (m_ssd_w_in, m_ssd_conv_w, m_ssd_conv_b, m_ssd_dt_bias, m_ssd_a_log, m_ssd_d, m_ssd_norm_w, m_ssd_w_out, m_pool_w, m_pool_scale, m_ffn_w_up, m_ffn_conv_w, m_ffn_conv_b, m_ffn_w_down, m_norm_mix_pre, m_norm_mix_post, m_norm_ffn_pre, m_norm_ffn_post)))
    moments_v = dict(zip(WEIGHTS, (v_ssd_w_in, v_ssd_conv_w, v_ssd_conv_b, v_ssd_dt_bias, v_ssd_a_log, v_ssd_d, v_ssd_norm_w, v_ssd_w_out, v_pool_w, v_pool_scale, v_ffn_w_up, v_ffn_conv_w, v_ffn_conv_b, v_ffn_w_down, v_norm_mix_pre, v_norm_mix_post, v_norm_ffn_pre, v_norm_ffn_post)))
    n_seq, seq, d_model = x.shape
    t = n_seq * seq

    full = dict(shards)
    for k, axis in BIG:
        stacked = _all_gather(shards[k].astype(BF16), name=f"gather_{k}")
        full[k] = [_unshard(stacked[:, l], axis - 1) for l in range(shards[k].shape[0])]
    small_all = _all_gather(_pack_rows([shards[k] for k, _ in SMALL_SHARDED]), name="gather_small_weights")
    small_stacked = _unpack_rows(small_all, [shards[k].shape for k, _ in SMALL_SHARDED], lead=(N_DEV,))
    for (k, axis), st in zip(SMALL_SHARDED, small_stacked):
        full[k] = _unshard(st, axis)
    w = _compute_layout(full)
    d_inner = full["ssd_w_out"][0].shape[0]

    loss, dx, grads = _local_step(x.reshape(t, d_model), loss_target.reshape(t, d_model), w, n_seq)
    loss = lax.psum(loss, ("x", "y", "c"))

    g_shard = {}
    for k, axis in BIG:
        blocks = [_shard_major(_matmul_grad_reference_layout(k, g.astype(BF16), d_inner), axis - 1) for g in grads[k]]
        received = _exchange(blocks, name=f"exchange_{k}")
        g_shard[k] = jnp.stack([_sum_slots(r, name=f"sum_{k}_{l}") for l, r in enumerate(received)])

    small_shapes = {k: full[k].shape for k in SMALL}
    g_small = _small_grads_reference_layout(grads, small_shapes, d_inner)
    s_all = _all_gather(_pack_rows([g_small[k] for k in SMALL]), name="gather_small_grads")
    for k, g in zip(SMALL, _unpack_rows(_sum_slots(s_all, name="sum_small_grads"), [small_shapes[k] for k in SMALL])):
        g_shard[k] = g
    for k, axis in SMALL_SHARDED:
        g_shard[k] = _my_shard(g_shard[k], axis)

    deltas, new_m, new_v = {}, {}, {}
    for k in WEIGHTS:
        deltas[k], new_m[k], new_v[k] = _adamw(shards[k], g_shard[k], moments_m[k], moments_v[k], name=f"adamw_{k}")
    return (loss, dx.reshape(x.shape), *[g_shard[k] for k in WEIGHTS], *[deltas[k] for k in WEIGHTS],
            *[new_m[k] for k in WEIGHTS], *[new_v[k] for k in WEIGHTS])
```

```python
import functools

import jax
import jax.numpy as jnp
from jax import lax
from jax.experimental import pallas as pl
from jax.experimental.pallas import tpu as pltpu

F32 = jnp.float32
BF16 = jnp.bfloat16

N_DEV = 8
HEAD_DIM = 64
N_SSD_GROUPS = 4
D_STATE = 128
CHUNK = 128
POOL_WINDOWS = (2, 4, 8, 16)
EPS = 1e-6
LANE = 128
ADAM_LR = 0.001
ADAM_B1 = 0.9
ADAM_B2 = 0.999
ADAM_EPS = 1e-08
ADAM_WD = 0.01
ADAM_STEP = 10
VMEM_LIMIT = 56 * 1024 * 1024
ANY = pl.BlockSpec(memory_space=pl.ANY)


def _pick(n, cands):
    for c in cands:
        if n % c == 0:
            return c
    return n


def _params(sem):
    return pltpu.CompilerParams(dimension_semantics=sem, vmem_limit_bytes=VMEM_LIMIT)


def _sigmoid(x):
    return 0.5 * jnp.tanh(0.5 * x) + 0.5


def _silu(x):
    return x * _sigmoid(x)


def _dsilu(x):
    s = _sigmoid(x)
    return s * (1.0 + x * (1.0 - s))


def _shift_down(x, s):
    rows = lax.broadcasted_iota(jnp.int32, x.shape, 0)
    return jnp.where(rows >= s, pltpu.roll(x, s, 0), 0.0)


def _shift_up(x, s):
    n = x.shape[0]
    rows = lax.broadcasted_iota(jnp.int32, x.shape, 0)
    return jnp.where(rows < n - s, pltpu.roll(x, n - s, 0), 0.0)


MM_VMEM_BUDGET = 40 * 1024 * 1024
MM_STEP_BYTES = 1_300_000
MM_SUB = 512


def _mm_tiles(m, n, k, a_bytes, b_bytes, o_bytes):
    def cands(dim, sizes):
        out = [s for s in sizes if s <= dim and dim % s == 0]
        return out or [dim]

    best = None
    for tm in cands(m, (2048, 1024, 512, 256, 128)):
        for tn in cands(n, (n, n // 2, n // 4, 2048, 1024, 512, 256, 128)):
            if tn % (2 * LANE) and tn != n:
                continue
            for tk in cands(k, (k, k // 2, 2048, 1024, 512)):
                if tk % LANE:
                    continue
                nk = k // tk
                acc = tm * tn * 4 if (nk > 1 and o_bytes != 4) else 0
                temps = tm * min(tn, MM_SUB) * 4 + (tm * tk * 2 if a_bytes == 4 else 0) + (tk * tn * 2 if b_bytes == 4 else 0)
                vmem = 2 * (tm * tk * a_bytes + tk * tn * b_bytes + tm * tn * o_bytes) + acc + temps
                if vmem > MM_VMEM_BUDGET:
                    continue
                steps = (m // tm) * (n // tn) * nk
                cost = (m * k * a_bytes * (n // tn) + k * n * b_bytes * (m // tm) + m * n * o_bytes
                        + steps * MM_STEP_BYTES)
                if best is None or cost < best[0]:
                    best = (cost, tm, tn, tk)
    return best[1:]


def _mm(a, b, *, ta=False, tb=False, out_dtype=F32, name="mm"):
    m, k = (a.shape[1], a.shape[0]) if ta else a.shape
    n = b.shape[0] if tb else b.shape[1]
    o_bytes = jnp.dtype(out_dtype).itemsize
    tm, tn, tk = _mm_tiles(m, n, k, a.dtype.itemsize, b.dtype.itemsize, o_bytes)
    nk = k // tk
    sub = _pick(tn, (MM_SUB, 256))
    use_acc = nk > 1 and o_bytes != 4
    a_spec = pl.BlockSpec((tk, tm), lambda i, j, kk: (kk, i)) if ta else pl.BlockSpec((tm, tk), lambda i, j, kk: (i, kk))
    b_spec = pl.BlockSpec((tn, tk), lambda i, j, kk: (j, kk)) if tb else pl.BlockSpec((tk, tn), lambda i, j, kk: (kk, j))
    dims = (((1,), (1 if tb else 0,)), ((), ()))

    def body(a_ref, b_ref, o_ref, *scratch):
        kk = pl.program_id(2)
        acc_ref = scratch[0] if use_acc else o_ref
        if nk > 1:
            @pl.when(kk == 0)
            def _():
                acc_ref[...] = jnp.zeros_like(acc_ref)

        av = a_ref[...].astype(BF16)
        if ta:
            av = av.T
        for s in range(tn // sub):
            cols = slice(s * sub, (s + 1) * sub)
            bv = (b_ref[cols, :] if tb else b_ref[:, cols]).astype(BF16)
            part = lax.dot_general(av, bv, dims, preferred_element_type=F32)
            if nk == 1:
                o_ref[:, cols] = part.astype(out_dtype)
            else:
                acc_ref[:, cols] += part
        if use_acc:
            @pl.when(kk == nk - 1)
            def _():
                o_ref[...] = acc_ref[...].astype(out_dtype)

    return pl.pallas_call(
        body,
        name=name,
        grid=(m // tm, n // tn, nk),
        in_specs=[a_spec, b_spec],
        out_specs=pl.BlockSpec((tm, tn), lambda i, j, kk: (i, j)),
        out_shape=jax.ShapeDtypeStruct((m, n), out_dtype),
        scratch_shapes=[pltpu.VMEM((tm, tn), F32)] if use_acc else [],
        compiler_params=_params(("parallel", "parallel", "arbitrary")),
    )(a, b)


def _rms_fwd(x, w, *, out_dtype, name):
    t, d = x.shape
    tm = _pick(t, (512, 256, 128))

    def body(x_ref, w_ref, o_ref):
        xv = x_ref[...]
        rstd = lax.rsqrt(jnp.mean(xv * xv, axis=-1, keepdims=True) + EPS)
        o_ref[...] = (xv * rstd * w_ref[...]).astype(out_dtype)

    return pl.pallas_call(
        body,
        name=name,
        grid=(t // tm,),
        in_specs=[pl.BlockSpec((tm, d), lambda i: (i, 0)), pl.BlockSpec((1, d), lambda i: (0, 0))],
        out_specs=pl.BlockSpec((tm, d), lambda i: (i, 0)),
        out_shape=jax.ShapeDtypeStruct((t, d), out_dtype),
        compiler_params=_params(("parallel",)),
    )(x, w)


def _res_rms_fwd(x, f, w, *, name):
    t, d = x.shape
    tm = _pick(t, (512, 256, 128))

    def body(x_ref, f_ref, w_ref, o_ref):
        fv = f_ref[...]
        rstd = lax.rsqrt(jnp.mean(fv * fv, axis=-1, keepdims=True) + EPS)
        o_ref[...] = x_ref[...] + fv * rstd * w_ref[...]

    row = pl.BlockSpec((tm, d), lambda i: (i, 0))
    return pl.pallas_call(
        body,
        name=name,
        grid=(t // tm,),
        in_specs=[row, row, pl.BlockSpec((1, d), lambda i: (0, 0))],
        out_specs=row,
        out_shape=jax.ShapeDtypeStruct((t, d), F32),
        compiler_params=_params(("parallel",)),
    )(x, f, w)


def _rms_bwd(x, w, dy, resid, *, name):
    t, d = x.shape
    tm = _pick(t, (512, 256, 128))
    has_res = resid is not None

    def body(*refs):
        if has_res:
            x_ref, w_ref, dy_ref, r_ref, dx_ref, dw_ref = refs
        else:
            x_ref, w_ref, dy_ref, dx_ref, dw_ref = refs
        xv = x_ref[...]
        dyv = dy_ref[...].astype(F32)
        rstd = lax.rsqrt(jnp.mean(xv * xv, axis=-1, keepdims=True) + EPS)
        xn = xv * rstd
        g = dyv * w_ref[...]
        dx = rstd * (g - xn * jnp.mean(g * xn, axis=-1, keepdims=True))
        if has_res:
            dx = dx + r_ref[...]
        dx_ref[...] = dx
        part = jnp.sum(dyv * xn, axis=0, keepdims=True)

        @pl.when(pl.program_id(0) == 0)
        def _():
            dw_ref[...] = part

        @pl.when(pl.program_id(0) > 0)
        def _():
            dw_ref[...] += part

    row = pl.BlockSpec((tm, d), lambda i: (i, 0))
    vec = pl.BlockSpec((1, d), lambda i: (0, 0))
    ins = [x, w, dy] + ([resid] if has_res else [])
    return pl.pallas_call(
        body,
        name=name,
        grid=(t // tm,),
        in_specs=[row, vec, row] + ([row] if has_res else []),
        out_specs=[row, vec],
        out_shape=[jax.ShapeDtypeStruct((t, d), F32), jax.ShapeDtypeStruct((1, d), F32)],
        compiler_params=_params(("arbitrary",)),
    )(*ins)


def _loss_head(y, target, *, name="loss_head"):
    t, d = y.shape
    tm = _pick(t, (512, 256, 128))

    def body(y_ref, t_ref, dy_ref, l_ref):
        err = y_ref[...] - t_ref[...]
        dy_ref[...] = err * (1.0 / d)
        part = jnp.sum(jnp.sum(err * err, axis=-1, keepdims=True), axis=0, keepdims=True) * (0.5 / d)
        part = jnp.broadcast_to(part, (1, LANE))

        @pl.when(pl.program_id(0) == 0)
        def _():
            l_ref[...] = part

        @pl.when(pl.program_id(0) > 0)
        def _():
            l_ref[...] += part

    row = pl.BlockSpec((tm, d), lambda i: (i, 0))
    dy, l = pl.pallas_call(
        body,
        name=name,
        grid=(t // tm,),
        in_specs=[row, row],
        out_specs=[row, pl.BlockSpec((1, LANE), lambda i: (0, 0))],
        out_shape=[jax.ShapeDtypeStruct((t, d), F32), jax.ShapeDtypeStruct((1, LANE), F32)],
        compiler_params=_params(("arbitrary",)),
    )(y, target)
    return l[0, 0], dy


def _conv_taps(h, w_ref, k_taps):
    out = h * w_ref[k_taps - 1:k_taps, :]
    for k in range(k_taps - 1):
        out = out + _shift_down(h, k_taps - 1 - k) * w_ref[k:k + 1, :]
    return out


def _conv_taps_bwd(h, dhc, w_ref, k_taps):
    dh = dhc * w_ref[k_taps - 1:k_taps, :]
    dws = []
    for k in range(k_taps - 1):
        up = _shift_up(dhc, k_taps - 1 - k)
        dh = dh + up * w_ref[k:k + 1, :]
        dws.append(jnp.sum(up * h, axis=0, keepdims=True))
    dws.append(jnp.sum(dhc * h, axis=0, keepdims=True))
    return dh, jnp.concatenate(dws, axis=0)


FFN_TC = 256


def _interleave(w, tc=FFN_TC):
    f = w.shape[-1] // 2
    lead = w.shape[:-1]
    return jnp.swapaxes(w.reshape(lead + (2, f // tc, tc)), -3, -2).reshape(lead + (2 * f,))


def _deinterleave(w, tc=FFN_TC):
    f = w.shape[-1] // 2
    lead = w.shape[:-1]
    return jnp.swapaxes(w.reshape(lead + (f // tc, 2, tc)), -3, -2).reshape(lead + (2 * f,))


def _ffn_act_fwd(h, conv_w, conv_b, n_seq, *, name):
    t, f2 = h.shape
    seq = t // n_seq
    tc = FFN_TC
    nj = f2 // (2 * tc)
    k_taps = conv_w.shape[0]

    def body(h_ref, w_ref, b_ref, o_ref, hc_ref):
        hc = _conv_taps(h_ref[...].astype(F32), w_ref, k_taps) + b_ref[...]
        hc_ref[...] = hc.astype(BF16)
        o_ref[...] = (_silu(hc[:, :tc]) * hc[:, tc:]).astype(BF16)

    return pl.pallas_call(
        body,
        name=name,
        grid=(n_seq, nj),
        in_specs=[
            pl.BlockSpec((seq, 2 * tc), lambda b, j: (b, j)),
            pl.BlockSpec((k_taps, 2 * tc), lambda b, j: (0, j)),
            pl.BlockSpec((1, 2 * tc), lambda b, j: (0, j)),
        ],
        out_specs=[pl.BlockSpec((seq, tc), lambda b, j: (b, j)), pl.BlockSpec((seq, 2 * tc), lambda b, j: (b, j))],
        out_shape=[jax.ShapeDtypeStruct((t, f2 // 2), BF16), jax.ShapeDtypeStruct((t, f2), BF16)],
        compiler_params=_params(("parallel", "parallel")),
    )(h, conv_w, conv_b)


def _ffn_act_bwd(h, hc, conv_w, da, n_seq, *, name):
    t, f2 = h.shape
    seq = t // n_seq
    tc = FFN_TC
    nj = f2 // (2 * tc)
    k_taps = conv_w.shape[0]

    def body(h_ref, hc_ref, w_ref, da_ref, dh_ref, dw_ref, db_ref):
        hcv = hc_ref[...].astype(F32)
        gate, val = hcv[:, :tc], hcv[:, tc:]
        dav = da_ref[...].astype(F32)
        dhc = jnp.concatenate([dav * val * _dsilu(gate), dav * _silu(gate)], axis=1)
        dh, dw = _conv_taps_bwd(h_ref[...].astype(F32), dhc, w_ref, k_taps)
        dh_ref[...] = dh.astype(BF16)
        db = jnp.sum(dhc, axis=0, keepdims=True)

        @pl.when(pl.program_id(1) == 0)
        def _():
            dw_ref[...] = dw
            db_ref[...] = db

        @pl.when(pl.program_id(1) > 0)
        def _():
            dw_ref[...] += dw
            db_ref[...] += db

    wide = pl.BlockSpec((seq, 2 * tc), lambda j, b: (b, j))
    return pl.pallas_call(
        body,
        name=name,
        grid=(nj, n_seq),
        in_specs=[wide, wide, pl.BlockSpec((k_taps, 2 * tc), lambda j, b: (0, j)),
                  pl.BlockSpec((seq, tc), lambda j, b: (b, j))],
        out_specs=[
            wide,
            pl.BlockSpec((k_taps, 2 * tc), lambda j, b: (0, j)),
            pl.BlockSpec((1, 2 * tc), lambda j, b: (0, j)),
        ],
        out_shape=[
            jax.ShapeDtypeStruct((t, f2), BF16),
            jax.ShapeDtypeStruct((k_taps, f2), F32),
            jax.ShapeDtypeStruct((1, f2), F32),
        ],
        compiler_params=_params(("parallel", "arbitrary")),
    )(h, hc, conv_w, da)


def _window_mixed(u, window):
    s = u
    step = 1
    while step < window:
        s = s + _shift_down(s, step)
        step *= 2
    rows = lax.broadcasted_iota(jnp.int32, u.shape, 0)
    inv_cnt = 1.0 / jnp.minimum(rows + 1, window).astype(F32)
    return s * inv_cnt - u, inv_cnt


def _window_mixed_bwd(dmixed, inv_cnt, window):
    r = dmixed * inv_cnt
    s = r
    step = 1
    while step < window:
        s = s + _shift_up(s, step)
        step *= 2
    return s - dmixed


def _pool_fwd(u, w, scale, n_seq, *, name):
    t, d = u.shape
    seq = t // n_seq
    n_g, dg, _ = w.shape

    def body(u_ref, w_ref, s_ref, o_ref):
        for k, window in enumerate(POOL_WINDOWS):
            @pl.when(pl.program_id(1) == k)
            def _(window=window):
                mixed, _ = _window_mixed(u_ref[...], window)
                pre = jnp.dot(mixed.astype(BF16), w_ref[0].astype(BF16), preferred_element_type=F32)
                o_ref[...] = pre * s_ref[...]

    return pl.pallas_call(
        body,
        name=name,
        grid=(n_seq, n_g),
        in_specs=[
            pl.BlockSpec((seq, dg), lambda b, g: (b, g)),
            pl.BlockSpec((1, dg, dg), lambda b, g: (g, 0, 0)),
            pl.BlockSpec((1, dg), lambda b, g: (0, g)),
        ],
        out_specs=pl.BlockSpec((seq, dg), lambda b, g: (b, g)),
        out_shape=jax.ShapeDtypeStruct((t, d), F32),
        compiler_params=_params(("parallel", "parallel")),
    )(u, w, scale)


def _pool_bwd(u, w, scale, dout, n_seq, *, name):
    t, d = u.shape
    seq = t // n_seq
    n_g, dg, _ = w.shape

    def body(u_ref, w_ref, s_ref, do_ref, du_ref, dw_ref, ds_ref):
        group = pl.program_id(0)
        first = pl.program_id(1) == 0
        for k, window in enumerate(POOL_WINDOWS):
            @pl.when(group == k)
            def _(window=window):
                mixed, inv_cnt = _window_mixed(u_ref[...], window)
                mixed_b = mixed.astype(BF16)
                w_b = w_ref[0].astype(BF16)
                dov = do_ref[...]
                pre = jnp.dot(mixed_b, w_b, preferred_element_type=F32)
                dsc = jnp.sum(dov * pre, axis=0, keepdims=True)
                dpre = (dov * s_ref[...]).astype(BF16)
                dw = lax.dot_general(mixed_b, dpre, (((0,), (0,)), ((), ())), preferred_element_type=F32)
                dmixed = lax.dot_general(dpre, w_b, (((1,), (1,)), ((), ())), preferred_element_type=F32)
                du_ref[...] = _window_mixed_bwd(dmixed, inv_cnt, window)

                @pl.when(first)
                def _():
                    dw_ref[0] = dw
                    ds_ref[...] = dsc

                @pl.when(jnp.logical_not(first))
                def _():
                    dw_ref[0] += dw
                    ds_ref[...] += dsc

    return pl.pallas_call(
        body,
        name=name,
        grid=(n_g, n_seq),
        in_specs=[
            pl.BlockSpec((seq, dg), lambda g, b: (b, g)),
            pl.BlockSpec((1, dg, dg), lambda g, b: (g, 0, 0)),
            pl.BlockSpec((1, dg), lambda g, b: (0, g)),
            pl.BlockSpec((seq, dg), lambda g, b: (b, g)),
        ],
        out_specs=[
            pl.BlockSpec((seq, dg), lambda g, b: (b, g)),
            pl.BlockSpec((1, dg, dg), lambda g, b: (g, 0, 0)),
            pl.BlockSpec((1, dg), lambda g, b: (0, g)),
        ],
        out_shape=[
            jax.ShapeDtypeStruct((t, d), F32),
            jax.ShapeDtypeStruct((n_g, dg, dg), F32),
            jax.ShapeDtypeStruct((1, d), F32),
        ],
        compiler_params=_params(("parallel", "arbitrary")),
    )(u, w, scale, dout)


def _adamw(w, g, m, v, *, name):
    shape = w.shape
    c = shape[-1]
    r = w.size // c
    tm = _pick(r, (512, 256, 128, 64, 32, 16, 8))

    def body(w_ref, g_ref, m_ref, v_ref, d_ref, nm_ref, nv_ref):
        gv = g_ref[...]
        nm = ADAM_B1 * m_ref[...] + (1.0 - ADAM_B1) * gv
        nv = ADAM_B2 * v_ref[...] + (1.0 - ADAM_B2) * (gv * gv)
        m_hat = nm / (1.0 - ADAM_B1 ** ADAM_STEP)
        v_hat = nv / (1.0 - ADAM_B2 ** ADAM_STEP)
        d_ref[...] = -ADAM_LR * (m_hat / (jnp.sqrt(v_hat) + ADAM_EPS) + ADAM_WD * w_ref[...])
        nm_ref[...] = nm
        nv_ref[...] = nv

    blk = pl.BlockSpec((tm, c), lambda i: (i, 0))
    out = jax.ShapeDtypeStruct((r, c), F32)
    res = pl.pallas_call(
        body,
        name=name,
        grid=(r // tm,),
        in_specs=[blk] * 4,
        out_specs=[blk] * 3,
        out_shape=[out] * 3,
        compiler_params=_params(("parallel",)),
    )(w.reshape(r, c), g.reshape(r, c), m.reshape(r, c), v.reshape(r, c))
    return tuple(a.reshape(shape) for a in res)


CONV_TC = 256


def _ssd_conv_fwd(proj, col0, n_cols, conv_w, conv_b, n_seq, *, name):
    t = proj.shape[0]
    seq = t // n_seq
    tc = CONV_TC
    off = col0 // tc
    k_taps = conv_w.shape[0]

    def body(h_ref, w_ref, b_ref, o_ref):
        o_ref[...] = _silu(_conv_taps(h_ref[...], w_ref, k_taps) + b_ref[...])

    return pl.pallas_call(
        body,
        name=name,
        grid=(n_seq, n_cols // tc),
        in_specs=[
            pl.BlockSpec((seq, tc), lambda b, j: (b, j + off)),
            pl.BlockSpec((k_taps, tc), lambda b, j: (0, j)),
            pl.BlockSpec((1, tc), lambda b, j: (0, j)),
        ],
        out_specs=pl.BlockSpec((seq, tc), lambda b, j: (b, j)),
        out_shape=jax.ShapeDtypeStruct((t, n_cols), F32),
        compiler_params=_params(("parallel", "parallel")),
    )(proj, conv_w, conv_b)


def _ssd_conv_bwd(proj, col0, conv_w, conv_b, dparts, dproj, n_seq, *, name):
    t = proj.shape[0]
    seq = t // n_seq
    tc = CONV_TC
    off = col0 // tc
    k_taps = conv_w.shape[0]
    widths = [d.shape[1] // tc for d in dparts]
    starts = [sum(widths[:i]) for i in range(len(widths))]
    n_blocks = sum(widths)
    n_parts = len(dparts)

    def body(h_ref, w_ref, b_ref, *rest):
        part_refs = rest[:n_parts]
        dh_ref, dw_ref, db_ref = rest[n_parts + 1:]
        j = pl.program_id(0)
        da = part_refs[-1][...]
        for i in reversed(range(n_parts - 1)):
            da = jnp.where(j < starts[i + 1], part_refs[i][...], da)
        hv = h_ref[...]
        dhc = da * _dsilu(_conv_taps(hv, w_ref, k_taps) + b_ref[...])
        dh, dw = _conv_taps_bwd(hv, dhc, w_ref, k_taps)
        dh_ref[...] = dh.astype(BF16)
        db = jnp.sum(dhc, axis=0, keepdims=True)

        @pl.when(pl.program_id(1) == 0)
        def _():
            dw_ref[...] = dw
            db_ref[...] = db

        @pl.when(pl.program_id(1) > 0)
        def _():
            dw_ref[...] += dw
            db_ref[...] += db

    def part_spec(start, width):
        return pl.BlockSpec((seq, tc), lambda j, b: (b, jnp.clip(j - start, 0, width - 1)))

    n_cols = n_blocks * tc
    return pl.pallas_call(
        body,
        name=name,
        grid=(n_blocks, n_seq),
        in_specs=[
            pl.BlockSpec((seq, tc), lambda j, b: (b, j + off)),
            pl.BlockSpec((k_taps, tc), lambda j, b: (0, j)),
            pl.BlockSpec((1, tc), lambda j, b: (0, j)),
        ] + [part_spec(st, wd) for st, wd in zip(starts, widths)] + [ANY],
        out_specs=[
            pl.BlockSpec((seq, tc), lambda j, b: (b, j + off)),
            pl.BlockSpec((k_taps, tc), lambda j, b: (0, j)),
            pl.BlockSpec((1, tc), lambda j, b: (0, j)),
        ],
        out_shape=[
            jax.ShapeDtypeStruct(dproj.shape, BF16),
            jax.ShapeDtypeStruct((k_taps, n_cols), F32),
            jax.ShapeDtypeStruct((1, n_cols), F32),
        ],
        input_output_aliases={3 + n_parts: 0},
        compiler_params=_params(("parallel", "arbitrary")),
    )(proj, conv_w, conv_b, *dparts, dproj)


def _fill_cols(buf, src, col0, *, name):
    t, c = src.shape
    tm = _pick(t, (1024, 512, 256, 128))

    def body(s_ref, b_ref, o_ref):
        o_ref[...] = s_ref[...].astype(o_ref.dtype)

    return pl.pallas_call(
        body,
        name=name,
        grid=(t // tm,),
        in_specs=[pl.BlockSpec((tm, c), lambda i: (i, 0)), ANY],
        out_specs=pl.BlockSpec((tm, c), lambda i: (i, col0 // c)),
        out_shape=jax.ShapeDtypeStruct(buf.shape, buf.dtype),
        input_output_aliases={1: 0},
        compiler_params=_params(("parallel",)),
    )(src, buf)


def _softplus(x):
    return jnp.maximum(x, 0.0) + jnp.log(1.0 + jnp.exp(-jnp.abs(x)))


def _chunk_decay(dtraw, bias, alog):
    q = dtraw.shape[0]
    dt = _softplus(dtraw + bias)
    a = -jnp.exp(alog)
    rows = lax.broadcasted_iota(jnp.int32, (q, q), 0)
    cols = lax.broadcasted_iota(jnp.int32, (q, q), 1)
    lower = rows >= cols
    acum = jnp.dot(lower.astype(F32), dt * a, precision=lax.Precision.HIGHEST, preferred_element_type=F32)
    return dt, a, acum, acum.T, lower


def _dot_exact(v, sel):
    hi = v.astype(BF16)
    r1 = v - hi.astype(F32)
    mid = r1.astype(BF16)
    lo = (r1 - mid.astype(F32)).astype(BF16)
    return (jnp.dot(hi, sel, preferred_element_type=F32) + jnp.dot(mid, sel, preferred_element_type=F32)
            + jnp.dot(lo, sel, preferred_element_type=F32))


def _head_selectors(gw, p):
    sum_heads = (lax.broadcasted_iota(jnp.int32, (gw, LANE), 0) // p == lax.broadcasted_iota(jnp.int32, (gw, LANE), 1))
    spread = (lax.broadcasted_iota(jnp.int32, (LANE, gw), 0) == lax.broadcasted_iota(jnp.int32, (LANE, gw), 1) // p)
    return sum_heads.astype(BF16), spread.astype(BF16)


def _row_spread(v, spread):
    return _dot_exact(jnp.broadcast_to(v, (8, v.shape[1])), spread)[0:1, :]


def _head_pad(v, r_heads):
    lead = v.shape[:-1]
    vg = v.reshape(lead + (N_SSD_GROUPS, r_heads))
    vg = jnp.pad(vg, [(0, 0)] * len(lead) + [(0, 0), (0, LANE - r_heads)])
    out = vg.reshape(lead + (N_SSD_GROUPS * LANE,))
    return out[None] if out.ndim == 1 else out


def _head_unpad(v, r_heads):
    lead = v.shape[:-1]
    out = v.reshape(lead + (N_SSD_GROUPS, LANE))[..., :r_heads].reshape(lead + (N_SSD_GROUPS * r_heads,))
    return out[0] if (len(lead) == 1 and lead[0] == 1) else out


def _ssd_w_in_layout(w_in, d_inner, d_xbc, r_heads):
    main = w_in[:, :d_inner + d_xbc]
    return jnp.concatenate([main, _head_pad(w_in[:, d_inner + d_xbc:], r_heads)], axis=1)


def _ssd_w_in_unlayout(w, d_inner, d_xbc, r_heads):
    main = w[:, :d_inner + d_xbc]
    return jnp.concatenate([main, _head_unpad(w[:, d_inner + d_xbc:], r_heads)], axis=1)


def _ssd_dims(proj, xbc):
    d_xbc = xbc.shape[1]
    d_inner = d_xbc - 2 * N_SSD_GROUPS * D_STATE
    gw = d_inner // N_SSD_GROUPS
    return d_inner, d_xbc, gw, gw // HEAD_DIM


def _ssd_fwd(proj, xbc, bias_p, alog_p, dskip_p, norm_w, n_seq, *, name):
    t = proj.shape[0]
    d_inner, d_xbc, gw, r_heads = _ssd_dims(proj, xbc)
    q, n, n_g, p = CHUNK, D_STATE, N_SSD_GROUPS, HEAD_DIM
    seq = t // n_seq
    nc = seq // q
    dt_blk0 = (d_inner + d_xbc) // LANE

    def body(x_ref, b_ref, c_ref, z_ref, dtr_ref, bias_ref, alog_ref, dsk_ref, nw_ref, yn_ref, y_ref, hs_ref, h_scr):
        @pl.when(pl.program_id(2) == 0)
        def _():
            h_scr[...] = jnp.zeros_like(h_scr)

        dt, a, acum, acum_t, lower = _chunk_decay(dtr_ref[...], bias_ref[...], alog_ref[...])
        x = x_ref[...]
        bb = b_ref[...].astype(BF16)
        cb = c_ref[...].astype(BF16)
        g_mat = lax.dot_general(cb, bb, (((1,), (1,)), ((), ())), preferred_element_type=F32)
        h_prev = h_scr[...]
        hs_ref[...] = h_prev
        c_h = jnp.dot(cb, h_prev.astype(BF16), preferred_element_type=F32)
        _, spread = _head_selectors(gw, p)
        acum_s = _dot_exact(acum, spread)
        a_last_s = acum_s[q - 1:q, :]
        xdt = x * _dot_exact(dt, spread)
        xdt_b = xdt.astype(BF16)
        ys = []
        for h in range(r_heads):
            decay = jnp.exp(jnp.where(lower, acum[:, h:h + 1] - acum_t[h:h + 1, :], -jnp.inf))
            ys.append(jnp.dot((g_mat * decay).astype(BF16), xdt_b[:, h * p:(h + 1) * p], preferred_element_type=F32))
        y = jnp.concatenate(ys, axis=1) + jnp.exp(acum_s) * c_h + _row_spread(dsk_ref[...], spread) * x
        xd = xdt * jnp.exp(a_last_s - acum_s)
        states = lax.dot_general(bb, xd.astype(BF16), (((0,), (0,)), ((), ())), preferred_element_type=F32)
        h_scr[...] = h_prev * jnp.exp(a_last_s) + states
        y_ref[...] = y
        gated = y * _silu(z_ref[...])
        rstd = lax.rsqrt(jnp.mean(gated * gated, axis=-1, keepdims=True) + EPS)
        yn_ref[...] = (gated * rstd * nw_ref[...]).astype(BF16)

    row = lambda b, g, c: b * nc + c
    vec = pl.BlockSpec((1, LANE), lambda b, g, c: (0, g))
    return pl.pallas_call(
        body,
        name=name,
        grid=(n_seq, n_g, nc),
        in_specs=[
            pl.BlockSpec((q, gw), lambda b, g, c: (row(b, g, c), g)),
            pl.BlockSpec((q, n), lambda b, g, c: (row(b, g, c), d_inner // n + g)),
            pl.BlockSpec((q, n), lambda b, g, c: (row(b, g, c), d_inner // n + n_g + g)),
            pl.BlockSpec((q, gw), lambda b, g, c: (row(b, g, c), g)),
            pl.BlockSpec((q, LANE), lambda b, g, c: (row(b, g, c), dt_blk0 + g)),
            vec, vec, vec,
            pl.BlockSpec((1, gw), lambda b, g, c: (0, g)),
        ],
        out_specs=[
            pl.BlockSpec((q, gw), lambda b, g, c: (row(b, g, c), g)),
            pl.BlockSpec((q, gw), lambda b, g, c: (row(b, g, c), g)),
            pl.BlockSpec((n, gw), lambda b, g, c: (row(b, g, c), g)),
        ],
        out_shape=[
            jax.ShapeDtypeStruct((t, d_inner), BF16),
            jax.ShapeDtypeStruct((t, d_inner), F32),
            jax.ShapeDtypeStruct((n_seq * nc * n, d_inner), F32),
        ],
        scratch_shapes=[pltpu.VMEM((n, gw), F32)],
        compiler_params=_params(("parallel", "parallel", "arbitrary")),
    )(xbc, xbc, xbc, proj, proj, bias_p, alog_p, dskip_p, norm_w)


def _ssd_bwd(proj, xbc, hs, y, dyn, bias_p, alog_p, dskip_p, norm_w, n_seq, *, name):
    t = proj.shape[0]
    d_inner, d_xbc, gw, r_heads = _ssd_dims(proj, xbc)
    q, n, n_g, p = CHUNK, D_STATE, N_SSD_GROUPS, HEAD_DIM
    seq = t // n_seq
    nc = seq // q
    dt_blk0 = (d_inner + d_xbc) // LANE

    def body(x_ref, b_ref, c_ref, z_ref, dtr_ref, bias_ref, alog_ref, dsk_ref, nw_ref, hs_ref, y_ref, dyn_ref,
             dx_ref, db_ref, dc_ref, dz_ref, ddtr_ref, dnw_ref, dbias_ref, dalog_ref, ddsk_ref, dh_scr):
        first = jnp.logical_and(pl.program_id(1) == 0, pl.program_id(2) == 0)

        @pl.when(pl.program_id(2) == 0)
        def _():
            dh_scr[...] = jnp.zeros_like(dh_scr)

        dtraw = dtr_ref[...]
        dt, a, acum, acum_t, lower = _chunk_decay(dtraw, bias_ref[...], alog_ref[...])
        x = x_ref[...]
        bb = b_ref[...].astype(BF16)
        cb = c_ref[...].astype(BF16)
        g_mat = lax.dot_general(cb, bb, (((1,), (1,)), ((), ())), preferred_element_type=F32)

        yv = y_ref[...]
        z = z_ref[...]
        sz = _silu(z)
        gated = yv * sz
        rstd = lax.rsqrt(jnp.mean(gated * gated, axis=-1, keepdims=True) + EPS)
        gn = gated * rstd
        dynv = dyn_ref[...]
        gwt = dynv * nw_ref[...]
        dgated = rstd * (gwt - gn * jnp.mean(gwt * gn, axis=-1, keepdims=True))
        dnw = jnp.sum(dynv * gn, axis=0, keepdims=True)
        dy = dgated * sz
        dz_ref[...] = (dgated * yv * _dsilu(z)).astype(BF16)

        h_prev = hs_ref[...]
        h_prev_b = h_prev.astype(BF16)
        ds = dh_scr[...]
        ds_b = ds.astype(BF16)
        sum_heads, spread = _head_selectors(gw, p)
        acum_s = _dot_exact(acum, spread)
        a_last_s = acum_s[q - 1:q, :]
        dt_s = _dot_exact(dt, spread)
        dsk_s = _row_spread(dsk_ref[...], spread)
        dte_s = jnp.exp(a_last_s - acum_s)
        cd_s = jnp.exp(a_last_s)
        xdt = x * dt_s
        xdt_b = xdt.astype(BF16)
        dy_b = dy.astype(BF16)
        gt_mat = lax.dot_general(bb, cb, (((1,), (1,)), ((), ())), preferred_element_type=F32)
        upper = lax.broadcasted_iota(jnp.int32, (q, q), 0) <= lax.broadcasted_iota(jnp.int32, (q, q), 1)
        dg = jnp.zeros((q, q), F32)
        dxdts, w_diffs = [], []
        for h in range(r_heads):
            hsl = slice(h * p, (h + 1) * p)
            diff = acum[:, h:h + 1] - acum_t[h:h + 1, :]
            decay = jnp.exp(jnp.where(lower, diff, -jnp.inf))
            decay_t = jnp.exp(jnp.where(upper, -diff, -jnp.inf))
            mt_mat = gt_mat * decay_t
            dm = lax.dot_general(dy_b[:, hsl], xdt_b[:, hsl], (((1,), (1,)), ((), ())), preferred_element_type=F32)
            dm_t = lax.dot_general(xdt_b[:, hsl], dy_b[:, hsl], (((1,), (1,)), ((), ())), preferred_element_type=F32)
            dg = dg + dm * decay
            dxdts.append(jnp.dot(mt_mat.astype(BF16), dy_b[:, hsl], preferred_element_type=F32))
            w_diffs.append(dm * (g_mat * decay) - dm_t * mt_mat)
        sel_q = (lax.broadcasted_iota(jnp.int32, (r_heads * q, LANE), 0) // q
                 == lax.broadcasted_iota(jnp.int32, (r_heads * q, LANE), 1)).astype(BF16)
        dacum_diag = _dot_exact(jnp.concatenate(w_diffs, axis=1), sel_q)
        c_h = jnp.dot(cb, h_prev_b, preferred_element_type=F32)
        dxd = jnp.dot(bb, ds_b, preferred_element_type=F32)
        dxdt = jnp.concatenate(dxdts, axis=1) + dxd * dte_s
        dye = dy * jnp.exp(acum_s)
        dye_b = dye.astype(BF16)
        xd = xdt * dte_s
        xd_b = xd.astype(BF16)
        dg_b = dg.astype(BF16)
        dx_ref[...] = dxdt * dt_s + dsk_s * dy
        dc_ref[...] = (jnp.dot(dg_b, bb, preferred_element_type=F32)
                       + lax.dot_general(dye_b, h_prev_b, (((1,), (1,)), ((), ())), preferred_element_type=F32))
        db_ref[...] = (lax.dot_general(dg_b, cb, (((0,), (0,)), ((), ())), preferred_element_type=F32)
                       + lax.dot_general(xd_b, ds_b, (((1,), (1,)), ((), ())), preferred_element_type=F32))
        dh_scr[...] = ds * cd_s + lax.dot_general(cb, dye_b, (((0,), (0,)), ((), ())), preferred_element_type=F32)
        ddt_cols = _dot_exact(x * dxdt, sum_heads)
        dacum_y = _dot_exact(dye * c_h - dxd * xd, sum_heads)
        col_sums = jnp.concatenate([
            jnp.sum(dxd * xd, axis=0, keepdims=True) + jnp.sum(ds * h_prev, axis=0, keepdims=True) * cd_s,
            jnp.sum(dy * x, axis=0, keepdims=True),
            jnp.zeros((6, gw), F32)], axis=0)
        col_sums = _dot_exact(col_sums, sum_heads)
        ddsk = col_sums[1:2, :]
        rows_q = lax.broadcasted_iota(jnp.int32, (q, LANE), 0)
        dacum = dacum_diag + dacum_y + jnp.where(rows_q == q - 1, col_sums[0:1, :], 0.0)
        dadt = jnp.dot(upper.astype(F32), dacum, precision=lax.Precision.HIGHEST, preferred_element_type=F32)
        ddt = dadt * a + ddt_cols
        ddtr = ddt * _sigmoid(dtraw + bias_ref[...])
        ddtr_ref[...] = ddtr
        dbias = jnp.sum(ddtr, axis=0, keepdims=True)
        dalog = jnp.sum(dadt * dt, axis=0, keepdims=True) * a

        @pl.when(first)
        def _():
            dnw_ref[...] = dnw
            dbias_ref[...] = dbias
            dalog_ref[...] = dalog
            ddsk_ref[...] = ddsk

        @pl.when(jnp.logical_not(first))
        def _():
            dnw_ref[...] += dnw
            dbias_ref[...] += dbias
            dalog_ref[...] += dalog
            ddsk_ref[...] += ddsk

    row = lambda g, b, c: b * nc + (nc - 1 - c)
    vec = pl.BlockSpec((1, LANE), lambda g, b, c: (0, g))
    wide = pl.BlockSpec((q, gw), lambda g, b, c: (row(g, b, c), g))
    narrow = pl.BlockSpec((q, n), lambda g, b, c: (row(g, b, c), g))
    return pl.pallas_call(
        body,
        name=name,
        grid=(n_g, n_seq, nc),
        in_specs=[
            wide,
            pl.BlockSpec((q, n), lambda g, b, c: (row(g, b, c), d_inner // n + g)),
            pl.BlockSpec((q, n), lambda g, b, c: (row(g, b, c), d_inner // n + n_g + g)),
            wide,
            pl.BlockSpec((q, LANE), lambda g, b, c: (row(g, b, c), dt_blk0 + g)),
            vec, vec, vec,
            pl.BlockSpec((1, gw), lambda g, b, c: (0, g)),
            pl.BlockSpec((n, gw), lambda g, b, c: (row(g, b, c), g)),
            wide, wide,
        ],
        out_specs=[
            wide, narrow, narrow, wide, narrow,
            pl.BlockSpec((1, gw), lambda g, b, c: (0, g)),
            vec, vec, vec,
        ],
        out_shape=[
            jax.ShapeDtypeStruct((t, d_inner), F32),
            jax.ShapeDtypeStruct((t, n_g * n), F32),
            jax.ShapeDtypeStruct((t, n_g * n), F32),
            jax.ShapeDtypeStruct(proj.shape, BF16),
            jax.ShapeDtypeStruct((t, n_g * LANE), F32),
            jax.ShapeDtypeStruct((1, d_inner), F32),
            jax.ShapeDtypeStruct((1, n_g * LANE), F32),
            jax.ShapeDtypeStruct((1, n_g * LANE), F32),
            jax.ShapeDtypeStruct((1, n_g * LANE), F32),
        ],
        scratch_shapes=[pltpu.VMEM((n, gw), F32)],
        compiler_params=_params(("parallel", "arbitrary", "arbitrary")),
    )(xbc, xbc, xbc, proj, proj, bias_p, alog_p, dskip_p, norm_w, hs, y, dyn)


MESH_IDS = pl.DeviceIdType.MESH


def _my_index():
    return 4 * lax.axis_index("x") + 2 * lax.axis_index("y") + lax.axis_index("c")


def _all_gather(shard, *, name):
    def body(x_ref, out_ref, send_sems, recv_sems, local_sem):
        x, y, c = lax.axis_index("x"), lax.axis_index("y"), lax.axis_index("c")
        me, sibling = (x, y, c), (x, y, 1 - c)
        chips = [(1 - x, y), (x, 1 - y), (1 - x, 1 - y)]

        def blk(px, py, pc):
            return out_ref.at[4 * px + 2 * py + pc]

        def copy(k, block, to, src=None):
            return pltpu.make_async_remote_copy(
                src_ref=blk(*block) if src is None else src, dst_ref=blk(*block),
                send_sem=send_sems.at[k], recv_sem=recv_sems.at[k], device_id=to, device_id_type=MESH_IDS)

        mine = pltpu.make_async_copy(x_ref, blk(*me), local_sem)
        mine.start()
        first = [copy(0, me, sibling, src=x_ref)]
        first += [copy(1 + j, me, (*chip, c), src=x_ref) for j, chip in enumerate(chips)]
        for cp in first:
            cp.start()
        passed = [copy(4 + j, (*chip, c), sibling) for j, chip in enumerate(chips)]
        for j, chip in enumerate(chips):
            copy(1 + j, (*chip, c), me).wait_recv()
            passed[j].start()
        copy(0, sibling, me).wait_recv()
        for j, chip in enumerate(chips):
            copy(4 + j, (*chip, 1 - c), me).wait_recv()
        for cp in first + passed:
            cp.wait_send()
        mine.wait()

    return pl.pallas_call(
        body,
        name=name,
        in_specs=[ANY],
        out_specs=ANY,
        out_shape=jax.ShapeDtypeStruct((N_DEV,) + shard.shape, shard.dtype),
        scratch_shapes=[pltpu.SemaphoreType.DMA((7,)), pltpu.SemaphoreType.DMA((7,)), pltpu.SemaphoreType.DMA],
    )(shard)


def _exchange(fulls, *, name):
    n_arr = len(fulls)

    def body(*refs):
        in_refs, out_refs = refs[:n_arr], refs[n_arr:2 * n_arr]
        send_sems, recv_sems, local_sems = refs[2 * n_arr:]
        x, y, c = lax.axis_index("x"), lax.axis_index("y"), lax.axis_index("c")
        me = 4 * x + 2 * y + c
        mine = [pltpu.make_async_copy(in_refs[a].at[me], out_refs[a].at[me], local_sems.at[a]) for a in range(n_arr)]
        for cp in mine:
            cp.start()
        copies = []
        for a in range(n_arr):
            for k in range(1, N_DEV):
                px = 1 - x if k & 4 else x
                py = 1 - y if k & 2 else y
                pc = 1 - c if k & 1 else c
                peer = 4 * px + 2 * py + pc
                sem = a * (N_DEV - 1) + k - 1
                copies.append((
                    pltpu.make_async_remote_copy(
                        src_ref=in_refs[a].at[peer], dst_ref=out_refs[a].at[me], send_sem=send_sems.at[sem],
                        recv_sem=recv_sems.at[sem], device_id=(px, py, pc), device_id_type=MESH_IDS),
                    pltpu.make_async_remote_copy(
                        src_ref=in_refs[a].at[peer], dst_ref=out_refs[a].at[peer], send_sem=send_sems.at[sem],
                        recv_sem=recv_sems.at[sem], device_id=(px, py, pc), device_id_type=MESH_IDS)))
        for send, _ in copies:
            send.start()
        for send, recv in copies:
            recv.wait_recv()
            send.wait_send()
        for cp in mine:
            cp.wait()

    n_sem = n_arr * (N_DEV - 1)
    return pl.pallas_call(
        body,
        name=name,
        in_specs=[ANY] * n_arr,
        out_specs=[ANY] * n_arr,
        out_shape=[jax.ShapeDtypeStruct(f.shape, f.dtype) for f in fulls],
        scratch_shapes=[pltpu.SemaphoreType.DMA((n_sem,)), pltpu.SemaphoreType.DMA((n_sem,)),
                        pltpu.SemaphoreType.DMA((n_arr,))],
    )(*fulls)


def _sum_slots(parts, *, name):
    shape = parts.shape[1:]
    n, c = parts.shape[0], parts.shape[-1]
    r = parts.size // (n * c)
    tm = _pick(r, (256, 128, 64, 32, 16, 8))

    def body(p_ref, o_ref):
        acc = p_ref[0].astype(F32)
        for s in range(1, n):
            acc = acc + p_ref[s].astype(F32)
        o_ref[...] = acc

    return pl.pallas_call(
        body,
        name=name,
        grid=(r // tm,),
        in_specs=[pl.BlockSpec((n, tm, c), lambda i: (0, i, 0))],
        out_specs=pl.BlockSpec((tm, c), lambda i: (i, 0)),
        out_shape=jax.ShapeDtypeStruct((r, c), F32),
        compiler_params=_params(("parallel",)),
    )(parts.reshape(n, r, c)).reshape(shape)


def _row_count(shape):
    c = shape[-1]
    rows = 1
    for s in shape[:-1]:
        rows *= s
    return rows, c, c + (-c) % LANE


PACK_ROWS = 256


def _pack_rows(arrays):
    pieces = []
    for a in arrays:
        rows, c, cp = _row_count(a.shape)
        a2 = a.reshape(rows, c)
        if cp > c:
            a2 = jnp.pad(a2, ((0, 0), (0, cp - c)))
        pieces.append(a2.reshape(rows * cp // LANE, LANE))
    total = sum(p.shape[0] for p in pieces)
    if total % PACK_ROWS:
        pieces.append(jnp.zeros((PACK_ROWS - total % PACK_ROWS, LANE), F32))
    return jnp.concatenate(pieces, axis=0)


def _unpack_rows(packed, shapes, lead=()):
    out, off = [], 0
    for shp in shapes:
        rows, c, cp = _row_count(shp)
        n_rows = rows * cp // LANE
        seg = packed[..., off:off + n_rows, :].reshape(lead + (rows, cp))
        out.append(seg[..., :c].reshape(lead + tuple(shp)))
        off += n_rows
    return out


def _unshard(stacked, axis):
    moved = jnp.moveaxis(stacked, 0, axis)
    shp = moved.shape
    return moved.reshape(shp[:axis] + (shp[axis] * shp[axis + 1],) + shp[axis + 2:])


def _shard_major(full, axis):
    shp = full.shape
    split = full.reshape(shp[:axis] + (N_DEV, shp[axis] // N_DEV) + shp[axis + 1:])
    return jnp.moveaxis(split, axis, 0)


def _my_shard(full, axis):
    size = full.shape[axis] // N_DEV
    return lax.dynamic_slice_in_dim(full, _my_index() * size, size, axis)


def _local_step(x, target, w, n_seq):
    depth = w["norm_mix_pre"].shape[0]
    d_inner = w["ssd_w_out"][0].shape[0]
    d_xbc = w["ssd_conv_w"].shape[2]
    saved = []
    for i in range(depth):
        j = i // 2
        s = {"x": x}
        if i % 2 == 0:
            u = _rms_fwd(x, w["norm_mix_pre"][i:i + 1], out_dtype=BF16, name=f"l{i}_mix_pre")
            proj = _mm(u, w["ssd_w_in"][j], name=f"l{i}_ssd_in")
            xbc = _ssd_conv_fwd(proj, d_inner, d_xbc, w["ssd_conv_w"][j], w["ssd_conv_b"][j:j + 1], n_seq,
                                name=f"l{i}_ssd_conv")
            yn, y, hs = _ssd_fwd(proj, xbc, w["ssd_dt_bias"][j:j + 1], w["ssd_a_log"][j:j + 1], w["ssd_d"][j:j + 1],
                                 w["ssd_norm_w"][j:j + 1], n_seq, name=f"l{i}_ssd_scan")
            mix = _mm(yn, w["ssd_w_out"][j], name=f"l{i}_ssd_out")
            s.update(u=u, proj=proj, xbc=xbc, yn=yn, y=y, hs=hs)
        else:
            u = _rms_fwd(x, w["norm_mix_pre"][i:i + 1], out_dtype=F32, name=f"l{i}_mix_pre")
            mix = _pool_fwd(u, w["pool_w"][j], w["pool_scale"][j:j + 1], n_seq, name=f"l{i}_pool")
            s.update(u=u)
        x1 = _res_rms_fwd(x, mix, w["norm_mix_post"][i:i + 1], name=f"l{i}_mix_post")
        n = _rms_fwd(x1, w["norm_ffn_pre"][i:i + 1], out_dtype=BF16, name=f"l{i}_ffn_pre")
        h = _mm(n, w["ffn_w_up"][i], out_dtype=BF16, name=f"l{i}_ffn_up")
        a, hc = _ffn_act_fwd(h, w["ffn_conv_w"][i], w["ffn_conv_b"][i:i + 1], n_seq, name=f"l{i}_ffn_act")
        f = _mm(a, w["ffn_w_down"][i], name=f"l{i}_ffn_down")
        x = _res_rms_fwd(x1, f, w["norm_ffn_post"][i:i + 1], name=f"l{i}_ffn_post")
        s.update(mix=mix, x1=x1, n=n, h=h, hc=hc, a=a, f=f)
        saved.append(s)

    loss, dx = _loss_head(x, target)
    grads = {k: [None] * len(v) for k, v in w.items()}
    for i in reversed(range(depth)):
        j = i // 2
        s = saved[i]
        df, grads["norm_ffn_post"][i] = _rms_bwd(s["f"], w["norm_ffn_post"][i:i + 1], dx, None, name=f"l{i}_ffn_post_b")
        da = _mm(df, w["ffn_w_down"][i], tb=True, out_dtype=BF16, name=f"l{i}_ffn_down_bx")
        grads["ffn_w_down"][i] = _mm(s["a"], df, ta=True, name=f"l{i}_ffn_down_bw")
        dh, grads["ffn_conv_w"][i], grads["ffn_conv_b"][i] = _ffn_act_bwd(
            s["h"], s["hc"], w["ffn_conv_w"][i], da, n_seq, name=f"l{i}_ffn_act_b")
        dn = _mm(dh, w["ffn_w_up"][i], tb=True, name=f"l{i}_ffn_up_bx")
        grads["ffn_w_up"][i] = _mm(s["n"], dh, ta=True, name=f"l{i}_ffn_up_bw")
        dx1, grads["norm_ffn_pre"][i] = _rms_bwd(s["x1"], w["norm_ffn_pre"][i:i + 1], dn, dx, name=f"l{i}_ffn_pre_b")
        dmix, grads["norm_mix_post"][i] = _rms_bwd(s["mix"], w["norm_mix_post"][i:i + 1], dx1, None,
                                                   name=f"l{i}_mix_post_b")
        if i % 2 == 0:
            dyn = _mm(dmix, w["ssd_w_out"][j], tb=True, name=f"l{i}_ssd_out_bx")
            grads["ssd_w_out"][j] = _mm(s["yn"], dmix, ta=True, name=f"l{i}_ssd_out_bw")
            dxs, db, dc, dz, ddtr, dnw, dbias, dalog, ddsk = _ssd_bwd(
                s["proj"], s["xbc"], s["hs"], s["y"], dyn, w["ssd_dt_bias"][j:j + 1], w["ssd_a_log"][j:j + 1],
                w["ssd_d"][j:j + 1], w["ssd_norm_w"][j:j + 1], n_seq, name=f"l{i}_ssd_scan_b")
            grads["ssd_norm_w"][j], grads["ssd_dt_bias"][j], grads["ssd_a_log"][j], grads["ssd_d"][j] = (
                dnw, dbias, dalog, ddsk)
            dproj, grads["ssd_conv_w"][j], grads["ssd_conv_b"][j] = _ssd_conv_bwd(
                s["proj"], d_inner, w["ssd_conv_w"][j], w["ssd_conv_b"][j:j + 1], (dxs, db, dc), dz, n_seq,
                name=f"l{i}_ssd_conv_b")
            dproj = _fill_cols(dproj, ddtr, d_inner + d_xbc, name=f"l{i}_ssd_dt_b")
            du = _mm(dproj, w["ssd_w_in"][j], tb=True, name=f"l{i}_ssd_in_bx")
            grads["ssd_w_in"][j] = _mm(s["u"], dproj, ta=True, name=f"l{i}_ssd_in_bw")
        else:
            du, grads["pool_w"][j], grads["pool_scale"][j] = _pool_bwd(
                s["u"], w["pool_w"][j], w["pool_scale"][j:j + 1], dmix, n_seq, name=f"l{i}_pool_b")
        dx, grads["norm_mix_pre"][i] = _rms_bwd(s["x"], w["norm_mix_pre"][i:i + 1], du, dx1, name=f"l{i}_mix_pre_b")
    return loss, dx, grads


BIG = (("ssd_w_in", 2), ("ssd_w_out", 1), ("pool_w", 2), ("ffn_w_up", 2), ("ffn_w_down", 1))
SMALL_SHARDED = (("ssd_conv_w", 2), ("ffn_conv_w", 2), ("pool_scale", 1))
SMALL = ("ssd_conv_w", "ssd_conv_b", "ssd_dt_bias", "ssd_a_log", "ssd_d", "ssd_norm_w", "pool_scale", "ffn_conv_w",
         "ffn_conv_b", "norm_mix_pre", "norm_mix_post", "norm_ffn_pre", "norm_ffn_post")
WEIGHTS = ("ssd_w_in", "ssd_conv_w", "ssd_conv_b", "ssd_dt_bias", "ssd_a_log", "ssd_d", "ssd_norm_w", "ssd_w_out",
           "pool_w", "pool_scale", "ffn_w_up", "ffn_conv_w", "ffn_conv_b", "ffn_w_down", "norm_mix_pre",
           "norm_mix_post", "norm_ffn_pre", "norm_ffn_post")


def _ssd_sizes(d_inner):
    return d_inner + 2 * N_SSD_GROUPS * D_STATE, d_inner // HEAD_DIM // N_SSD_GROUPS


def _compute_layout(full):
    d_inner = full["ssd_w_out"][0].shape[0]
    d_xbc, r_heads = _ssd_sizes(d_inner)
    w = dict(full)
    w["ssd_w_in"] = [_ssd_w_in_layout(a, d_inner, d_xbc, r_heads) for a in full["ssd_w_in"]]
    w["ffn_w_up"] = [_interleave(a) for a in full["ffn_w_up"]]
    for k in ("ssd_dt_bias", "ssd_a_log", "ssd_d"):
        w[k] = _head_pad(full[k], r_heads)
    for k in ("ffn_conv_w", "ffn_conv_b"):
        w[k] = _interleave(full[k])
    return w


def _matmul_grad_reference_layout(k, g, d_inner):
    d_xbc, r_heads = _ssd_sizes(d_inner)
    if k == "ssd_w_in":
        return _ssd_w_in_unlayout(g, d_inner, d_xbc, r_heads)
    if k == "ffn_w_up":
        return _deinterleave(g)
    return g


def _small_grads_reference_layout(grads, shapes, d_inner):
    _, r_heads = _ssd_sizes(d_inner)
    g = {k: jnp.stack(grads[k]) for k in SMALL}
    for k in ("ssd_dt_bias", "ssd_a_log", "ssd_d"):
        g[k] = _head_unpad(g[k][:, 0], r_heads)
    for k in ("ffn_conv_w", "ffn_conv_b"):
        g[k] = _deinterleave(g[k])
    return {k: v.reshape(shapes[k]) for k, v in g.items()}


def kernel(x, ssd_w_in, ssd_conv_w, ssd_conv_b, ssd_dt_bias, ssd_a_log, ssd_d, ssd_norm_w, ssd_w_out, pool_w, pool_scale, ffn_w_up, ffn_conv_w, ffn_conv_b, ffn_w_down, norm_mix_pre, norm_mix_post, norm_ffn_pre, norm_ffn_post, loss_target, m_ssd_w_in, m_ssd_conv_w, m_ssd_conv_b, m_ssd_dt_bias, m_ssd_a_log, m_ssd_d, m_ssd_norm_w, m_ssd_w_out, m_pool_w, m_pool_scale, m_ffn_w_up, m_ffn_conv_w, m_ffn_conv_b, m_ffn_w_down, m_norm_mix_pre, m_norm_mix_post, m_norm_ffn_pre, m_norm_ffn_post, v_ssd_w_in, v_ssd_conv_w, v_ssd_conv_b, v_ssd_dt_bias, v_ssd_a_log, v_ssd_d, v_ssd_norm_w, v_ssd_w_out, v_pool_w, v_pool_scale, v_ffn_w_up, v_ffn_conv_w, v_ffn_conv_b, v_ffn_w_down, v_norm_mix_pre, v_norm_mix_post, v_norm_ffn_pre, v_norm_ffn_post):
    shards = dict(ssd_w_in=ssd_w_in, ssd_conv_w=ssd_conv_w, ssd_conv_b=ssd_conv_b, ssd_dt_bias=ssd_dt_bias,
                  ssd_a_log=ssd_a_log, ssd_d=ssd_d, ssd_norm_w=ssd_norm_w, ssd_w_out=ssd_w_out, pool_w=pool_w,
                  pool_scale=pool_scale, ffn_w_up=ffn_w_up, ffn_conv_w=ffn_conv_w, ffn_conv_b=ffn_conv_b,
                  ffn_w_down=ffn_w_down, norm_mix_pre=norm_mix_pre, norm_mix_post=norm_mix_post,
                  norm_ffn_pre=norm_ffn_pre, norm_ffn_post=norm_ffn_post)
    moments_m = dict(zip(WEIGHTS, (m_ssd_w_in, m_ssd_conv_w, m_ssd_conv_b, m_ssd_dt_bias, m_ssd_a_log, m_ssd_d, m_ssd_norm_w, m_ssd_w_out, m_pool_w, m_pool_scale, m_ffn_w_up, m_ffn_conv_w, m_ffn_conv_b, m_ffn_w_down, m_norm_mix_pre, m_norm_mix_post, m_norm_ffn_pre, m_norm_ffn_post)))
    moments_v = dict(zip(WEIGHTS, (v_ssd_w_in, v_ssd_conv_w, v_ssd_conv_b, v_ssd_dt_bias, v_ssd_a_log, v_ssd_d, v_ssd_norm_w, v_ssd_w_out, v_pool_w, v_pool_scale, v_ffn_w_up, v_ffn_conv_w, v_ffn_conv_b, v_ffn_w_down, v_norm_mix_pre, v_norm_mix_post, v_norm_ffn_pre, v_norm_ffn_post)))
    n_seq, seq, d_model = x.shape
    t = n_seq * seq

    full = dict(shards)
    for k, axis in BIG:
        stacked = _all_gather(shards[k].astype(BF16), name=f"gather_{k}")
        full[k] = [_unshard(stacked[:, l], axis - 1) for l in range(shards[k].shape[0])]
    small_all = _all_gather(_pack_rows([shards[k] for k, _ in SMALL_SHARDED]), name="gather_small_weights")
    small_stacked = _unpack_rows(small_all, [shards[k].shape for k, _ in SMALL_SHARDED], lead=(N_DEV,))
    for (k, axis), st in zip(SMALL_SHARDED, small_stacked):
        full[k] = _unshard(st, axis)
    w = _compute_layout(full)
    d_inner = full["ssd_w_out"][0].shape[0]

    loss, dx, grads = _local_step(x.reshape(t, d_model), loss_target.reshape(t, d_model), w, n_seq)
    loss = lax.psum(loss, ("x", "y", "c"))

    g_shard = {}
    for k, axis in BIG:
        blocks = [_shard_major(_matmul_grad_reference_layout(k, g.astype(BF16), d_inner), axis - 1) for g in grads[k]]
        received = _exchange(blocks, name=f"exchange_{k}")
        g_shard[k] = jnp.stack([_sum_slots(r, name=f"sum_{k}_{l}") for l, r in enumerate(received)])

    small_shapes = {k: full[k].shape for k in SMALL}
    g_small = _small_grads_reference_layout(grads, small_shapes, d_inner)
    s_all = _all_gather(_pack_rows([g_small[k] for k in SMALL]), name="gather_small_grads")
    for k, g in zip(SMALL, _unpack_rows(_sum_slots(s_all, name="sum_small_grads"), [small_shapes[k] for k in SMALL])):
        g_shard[k] = g
    for k, axis in SMALL_SHARDED:
        g_shard[k] = _my_shard(g_shard[k], axis)

    deltas, new_m, new_v = {}, {}, {}
    for k in WEIGHTS:
        deltas[k], new_m[k], new_v[k] = _adamw(shards[k], g_shard[k], moments_m[k], moments_v[k], name=f"adamw_{k}")
    return (loss, dx.reshape(x.shape), *[g_shard[k] for k in WEIGHTS], *[deltas[k] for k in WEIGHTS],
            *[new_m[k] for k in WEIGHTS], *[new_v[k] for k in WEIGHTS])
```

```python
import functools

import jax
import jax.numpy as jnp
from jax import lax
from jax.experimental import pallas as pl
from jax.experimental.pallas import tpu as pltpu

F32 = jnp.float32
BF16 = jnp.bfloat16

N_DEV = 8
HEAD_DIM = 64
N_SSD_GROUPS = 4
D_STATE = 128
CHUNK = 128
POOL_WINDOWS = (2, 4, 8, 16)
EPS = 1e-6
LANE = 128
ADAM_LR = 0.001
ADAM_B1 = 0.9
ADAM_B2 = 0.999
ADAM_EPS = 1e-08
ADAM_WD = 0.01
ADAM_STEP = 10
VMEM_LIMIT = 56 * 1024 * 1024
ANY = pl.BlockSpec(memory_space=pl.ANY)


def _pick(n, cands):
    for c in cands:
        if n % c == 0:
            return c
    return n


def _params(sem):
    return pltpu.CompilerParams(dimension_semantics=sem, vmem_limit_bytes=VMEM_LIMIT)


def _sigmoid(x):
    return 0.5 * jnp.tanh(0.5 * x) + 0.5


def _silu(x):
    return x * _sigmoid(x)


def _dsilu(x):
    s = _sigmoid(x)
    return s * (1.0 + x * (1.0 - s))


def _shift_down(x, s):
    rows = lax.broadcasted_iota(jnp.int32, x.shape, 0)
    return jnp.where(rows >= s, pltpu.roll(x, s, 0), 0.0)


def _shift_up(x, s):
    n = x.shape[0]
    rows = lax.broadcasted_iota(jnp.int32, x.shape, 0)
    return jnp.where(rows < n - s, pltpu.roll(x, n - s, 0), 0.0)


MM_VMEM_BUDGET = 40 * 1024 * 1024
MM_STEP_BYTES = 1_300_000
MM_SUB = 512


def _mm_tiles(m, n, k, a_bytes, b_bytes, o_bytes):
    def cands(dim, sizes):
        out = [s for s in sizes if s <= dim and dim % s == 0]
        return out or [dim]

    best = None
    for tm in cands(m, (m, m // 2, 2048, 1024, 512, 256, 128)):
        if tm % LANE:
            continue
        for tn in cands(n, (n, n // 2, n // 4, 2048, 1024, 512, 256, 128)):
            if tn % (2 * LANE) and tn != n:
                continue
            for tk in cands(k, (k, k // 2, 2048, 1024, 512)):
                if tk % LANE:
                    continue
                nk = k // tk
                acc = tm * tn * 4 if (nk > 1 and o_bytes != 4) else 0
                temps = tm * min(tn, MM_SUB) * 4 + (tm * tk * 2 if a_bytes == 4 else 0) + (tk * tn * 2 if b_bytes == 4 else 0)
                vmem = 2 * (tm * tk * a_bytes + tk * tn * b_bytes + tm * tn * o_bytes) + acc + temps
                if vmem > MM_VMEM_BUDGET:
                    continue
                steps = (m // tm) * (n // tn) * nk
                cost = (m * k * a_bytes * (n // tn) + k * n * b_bytes * (m // tm) + m * n * o_bytes
                        + steps * MM_STEP_BYTES)
                if best is None or cost < best[0]:
                    best = (cost, tm, tn, tk)
    return best[1:]


def _mm(a, b, *, ta=False, tb=False, out_dtype=F32, name="mm"):
    m, k = (a.shape[1], a.shape[0]) if ta else a.shape
    n = b.shape[0] if tb else b.shape[1]
    o_bytes = jnp.dtype(out_dtype).itemsize
    tm, tn, tk = _mm_tiles(m, n, k, a.dtype.itemsize, b.dtype.itemsize, o_bytes)
    nk = k // tk
    sub = _pick(tn, (MM_SUB, 256))
    use_acc = nk > 1 and o_bytes != 4
    a_spec = pl.BlockSpec((tk, tm), lambda i, j, kk: (kk, i)) if ta else pl.BlockSpec((tm, tk), lambda i, j, kk: (i, kk))
    b_spec = pl.BlockSpec((tn, tk), lambda i, j, kk: (j, kk)) if tb else pl.BlockSpec((tk, tn), lambda i, j, kk: (kk, j))
    dims = (((1,), (1 if tb else 0,)), ((), ()))

    def body(a_ref, b_ref, o_ref, *scratch):
        kk = pl.program_id(2)
        acc_ref = scratch[0] if use_acc else o_ref
        if nk > 1:
            @pl.when(kk == 0)
            def _():
                acc_ref[...] = jnp.zeros_like(acc_ref)

        av = a_ref[...].astype(BF16)
        if ta:
            av = av.T
        for s in range(tn // sub):
            cols = slice(s * sub, (s + 1) * sub)
            bv = (b_ref[cols, :] if tb else b_ref[:, cols]).astype(BF16)
            part = lax.dot_general(av, bv, dims, preferred_element_type=F32)
            if nk == 1:
                o_ref[:, cols] = part.astype(out_dtype)
            else:
                acc_ref[:, cols] += part
        if use_acc:
            @pl.when(kk == nk - 1)
            def _():
                o_ref[...] = acc_ref[...].astype(out_dtype)

    return pl.pallas_call(
        body,
        name=name,
        grid=(m // tm, n // tn, nk),
        in_specs=[a_spec, b_spec],
        out_specs=pl.BlockSpec((tm, tn), lambda i, j, kk: (i, j)),
        out_shape=jax.ShapeDtypeStruct((m, n), out_dtype),
        scratch_shapes=[pltpu.VMEM((tm, tn), F32)] if use_acc else [],
        compiler_params=_params(("parallel", "parallel", "arbitrary")),
    )(a, b)


def _rms_fwd(x, w, *, out_dtype, name):
    t, d = x.shape
    tm = _pick(t, (512, 256, 128))

    def body(x_ref, w_ref, o_ref):
        xv = x_ref[...]
        rstd = lax.rsqrt(jnp.mean(xv * xv, axis=-1, keepdims=True) + EPS)
        o_ref[...] = (xv * rstd * w_ref[...]).astype(out_dtype)

    return pl.pallas_call(
        body,
        name=name,
        grid=(t // tm,),
        in_specs=[pl.BlockSpec((tm, d), lambda i: (i, 0)), pl.BlockSpec((1, d), lambda i: (0, 0))],
        out_specs=pl.BlockSpec((tm, d), lambda i: (i, 0)),
        out_shape=jax.ShapeDtypeStruct((t, d), out_dtype),
        compiler_params=_params(("parallel",)),
    )(x, w)


def _res_rms_fwd(x, f, w, *, name):
    t, d = x.shape
    tm = _pick(t, (512, 256, 128))

    def body(x_ref, f_ref, w_ref, o_ref):
        fv = f_ref[...]
        rstd = lax.rsqrt(jnp.mean(fv * fv, axis=-1, keepdims=True) + EPS)
        o_ref[...] = x_ref[...] + fv * rstd * w_ref[...]

    row = pl.BlockSpec((tm, d), lambda i: (i, 0))
    return pl.pallas_call(
        body,
        name=name,
        grid=(t // tm,),
        in_specs=[row, row, pl.BlockSpec((1, d), lambda i: (0, 0))],
        out_specs=row,
        out_shape=jax.ShapeDtypeStruct((t, d), F32),
        compiler_params=_params(("parallel",)),
    )(x, f, w)


def _rms_bwd(x, w, dy, resid, *, name):
    t, d = x.shape
    tm = _pick(t, (512, 256, 128))
    has_res = resid is not None

    def body(*refs):
        if has_res:
            x_ref, w_ref, dy_ref, r_ref, dx_ref, dw_ref = refs
        else:
            x_ref, w_ref, dy_ref, dx_ref, dw_ref = refs
        xv = x_ref[...]
        dyv = dy_ref[...].astype(F32)
        rstd = lax.rsqrt(jnp.mean(xv * xv, axis=-1, keepdims=True) + EPS)
        xn = xv * rstd
        g = dyv * w_ref[...]
        dx = rstd * (g - xn * jnp.mean(g * xn, axis=-1, keepdims=True))
        if has_res:
            dx = dx + r_ref[...]
        dx_ref[...] = dx
        part = jnp.sum(dyv * xn, axis=0, keepdims=True)

        @pl.when(pl.program_id(0) == 0)
        def _():
            dw_ref[...] = part

        @pl.when(pl.program_id(0) > 0)
        def _():
            dw_ref[...] += part

    row = pl.BlockSpec((tm, d), lambda i: (i, 0))
    vec = pl.BlockSpec((1, d), lambda i: (0, 0))
    ins = [x, w, dy] + ([resid] if has_res else [])
    return pl.pallas_call(
        body,
        name=name,
        grid=(t // tm,),
        in_specs=[row, vec, row] + ([row] if has_res else []),
        out_specs=[row, vec],
        out_shape=[jax.ShapeDtypeStruct((t, d), F32), jax.ShapeDtypeStruct((1, d), F32)],
        compiler_params=_params(("arbitrary",)),
    )(*ins)


def _loss_head(y, target, *, name="loss_head"):
    t, d = y.shape
    tm = _pick(t, (512, 256, 128))

    def body(y_ref, t_ref, dy_ref, l_ref):
        err = y_ref[...] - t_ref[...]
        dy_ref[...] = err * (1.0 / d)
        part = jnp.sum(jnp.sum(err * err, axis=-1, keepdims=True), axis=0, keepdims=True) * (0.5 / d)
        part = jnp.broadcast_to(part, (1, LANE))

        @pl.when(pl.program_id(0) == 0)
        def _():
            l_ref[...] = part

        @pl.when(pl.program_id(0) > 0)
        def _():
            l_ref[...] += part

    row = pl.BlockSpec((tm, d), lambda i: (i, 0))
    dy, l = pl.pallas_call(
        body,
        name=name,
        grid=(t // tm,),
        in_specs=[row, row],
        out_specs=[row, pl.BlockSpec((1, LANE), lambda i: (0, 0))],
        out_shape=[jax.ShapeDtypeStruct((t, d), F32), jax.ShapeDtypeStruct((1, LANE), F32)],
        compiler_params=_params(("arbitrary",)),
    )(y, target)
    return l[0, 0], dy


def _conv_taps(h, w_ref, k_taps):
    out = h * w_ref[k_taps - 1:k_taps, :]
    for k in range(k_taps - 1):
        out = out + _shift_down(h, k_taps - 1 - k) * w_ref[k:k + 1, :]
    return out


def _conv_taps_bwd(h, dhc, w_ref, k_taps):
    dh = dhc * w_ref[k_taps - 1:k_taps, :]
    dws = []
    for k in range(k_taps - 1):
        up = _shift_up(dhc, k_taps - 1 - k)
        dh = dh + up * w_ref[k:k + 1, :]
        dws.append(jnp.sum(up * h, axis=0, keepdims=True))
    dws.append(jnp.sum(dhc * h, axis=0, keepdims=True))
    return dh, jnp.concatenate(dws, axis=0)


FFN_TC = 256


def _interleave(w, tc=FFN_TC):
    f = w.shape[-1] // 2
    lead = w.shape[:-1]
    return jnp.swapaxes(w.reshape(lead + (2, f // tc, tc)), -3, -2).reshape(lead + (2 * f,))


def _deinterleave(w, tc=FFN_TC):
    f = w.shape[-1] // 2
    lead = w.shape[:-1]
    return jnp.swapaxes(w.reshape(lead + (f // tc, 2, tc)), -3, -2).reshape(lead + (2 * f,))


def _ffn_act_fwd(h, conv_w, conv_b, n_seq, *, name):
    t, f2 = h.shape
    seq = t // n_seq
    tc = FFN_TC
    nj = f2 // (2 * tc)
    k_taps = conv_w.shape[0]

    def body(h_ref, w_ref, b_ref, o_ref, hc_ref):
        hc = _conv_taps(h_ref[...].astype(F32), w_ref, k_taps) + b_ref[...]
        hc_ref[...] = hc.astype(BF16)
        o_ref[...] = (_silu(hc[:, :tc]) * hc[:, tc:]).astype(BF16)

    return pl.pallas_call(
        body,
        name=name,
        grid=(n_seq, nj),
        in_specs=[
            pl.BlockSpec((seq, 2 * tc), lambda b, j: (b, j)),
            pl.BlockSpec((k_taps, 2 * tc), lambda b, j: (0, j)),
            pl.BlockSpec((1, 2 * tc), lambda b, j: (0, j)),
        ],
        out_specs=[pl.BlockSpec((seq, tc), lambda b, j: (b, j)), pl.BlockSpec((seq, 2 * tc), lambda b, j: (b, j))],
        out_shape=[jax.ShapeDtypeStruct((t, f2 // 2), BF16), jax.ShapeDtypeStruct((t, f2), BF16)],
        compiler_params=_params(("parallel", "parallel")),
    )(h, conv_w, conv_b)


def _ffn_act_bwd(h, hc, conv_w, da, n_seq, *, name):
    t, f2 = h.shape
    seq = t // n_seq
    tc = FFN_TC
    nj = f2 // (2 * tc)
    k_taps = conv_w.shape[0]

    def body(h_ref, hc_ref, w_ref, da_ref, dh_ref, dw_ref, db_ref):
        hcv = hc_ref[...].astype(F32)
        gate, val = hcv[:, :tc], hcv[:, tc:]
        dav = da_ref[...].astype(F32)
        dhc = jnp.concatenate([dav * val * _dsilu(gate), dav * _silu(gate)], axis=1)
        dh, dw = _conv_taps_bwd(h_ref[...].astype(F32), dhc, w_ref, k_taps)
        dh_ref[...] = dh.astype(BF16)
        db = jnp.sum(dhc, axis=0, keepdims=True)

        @pl.when(pl.program_id(1) == 0)
        def _():
            dw_ref[...] = dw
            db_ref[...] = db

        @pl.when(pl.program_id(1) > 0)
        def _():
            dw_ref[...] += dw
            db_ref[...] += db

    wide = pl.BlockSpec((seq, 2 * tc), lambda j, b: (b, j))
    return pl.pallas_call(
        body,
        name=name,
        grid=(nj, n_seq),
        in_specs=[wide, wide, pl.BlockSpec((k_taps, 2 * tc), lambda j, b: (0, j)),
                  pl.BlockSpec((seq, tc), lambda j, b: (b, j))],
        out_specs=[
            wide,
            pl.BlockSpec((k_taps, 2 * tc), lambda j, b: (0, j)),
            pl.BlockSpec((1, 2 * tc), lambda j, b: (0, j)),
        ],
        out_shape=[
            jax.ShapeDtypeStruct((t, f2), BF16),
            jax.ShapeDtypeStruct((k_taps, f2), F32),
            jax.ShapeDtypeStruct((1, f2), F32),
        ],
        compiler_params=_params(("parallel", "arbitrary")),
    )(h, hc, conv_w, da)


def _window_mixed(u, window):
    s = u
    step = 1
    while step < window:
        s = s + _shift_down(s, step)
        step *= 2
    rows = lax.broadcasted_iota(jnp.int32, u.shape, 0)
    inv_cnt = 1.0 / jnp.minimum(rows + 1, window).astype(F32)
    return s * inv_cnt - u, inv_cnt


def _window_mixed_bwd(dmixed, inv_cnt, window):
    r = dmixed * inv_cnt
    s = r
    step = 1
    while step < window:
        s = s + _shift_up(s, step)
        step *= 2
    return s - dmixed


def _pool_fwd(u, w, scale, n_seq, *, name):
    t, d = u.shape
    seq = t // n_seq
    n_g, dg, _ = w.shape

    def body(u_ref, w_ref, s_ref, o_ref):
        for k, window in enumerate(POOL_WINDOWS):
            @pl.when(pl.program_id(1) == k)
            def _(window=window):
                mixed, _ = _window_mixed(u_ref[...], window)
                pre = jnp.dot(mixed.astype(BF16), w_ref[0].astype(BF16), preferred_element_type=F32)
                o_ref[...] = pre * s_ref[...]

    return pl.pallas_call(
        body,
        name=name,
        grid=(n_seq, n_g),
        in_specs=[
            pl.BlockSpec((seq, dg), lambda b, g: (b, g)),
            pl.BlockSpec((1, dg, dg), lambda b, g: (g, 0, 0)),
            pl.BlockSpec((1, dg), lambda b, g: (0, g)),
        ],
        out_specs=pl.BlockSpec((seq, dg), lambda b, g: (b, g)),
        out_shape=jax.ShapeDtypeStruct((t, d), F32),
        compiler_params=_params(("parallel", "parallel")),
    )(u, w, scale)


def _pool_bwd(u, w, scale, dout, n_seq, *, name):
    t, d = u.shape
    seq = t // n_seq
    n_g, dg, _ = w.shape

    def body(u_ref, w_ref, s_ref, do_ref, du_ref, dw_ref, ds_ref):
        group = pl.program_id(0)
        first = pl.program_id(1) == 0
        for k, window in enumerate(POOL_WINDOWS):
            @pl.when(group == k)
            def _(window=window):
                mixed, inv_cnt = _window_mixed(u_ref[...], window)
                mixed_b = mixed.astype(BF16)
                w_b = w_ref[0].astype(BF16)
                dov = do_ref[...]
                pre = jnp.dot(mixed_b, w_b, preferred_element_type=F32)
                dsc = jnp.sum(dov * pre, axis=0, keepdims=True)
                dpre = (dov * s_ref[...]).astype(BF16)
                dw = lax.dot_general(mixed_b, dpre, (((0,), (0,)), ((), ())), preferred_element_type=F32)
                dmixed = lax.dot_general(dpre, w_b, (((1,), (1,)), ((), ())), preferred_element_type=F32)
                du_ref[...] = _window_mixed_bwd(dmixed, inv_cnt, window)

                @pl.when(first)
                def _():
                    dw_ref[0] = dw
                    ds_ref[...] = dsc

                @pl.when(jnp.logical_not(first))
                def _():
                    dw_ref[0] += dw
                    ds_ref[...] += dsc

    return pl.pallas_call(
        body,
        name=name,
        grid=(n_g, n_seq),
        in_specs=[
            pl.BlockSpec((seq, dg), lambda g, b: (b, g)),
            pl.BlockSpec((1, dg, dg), lambda g, b: (g, 0, 0)),
            pl.BlockSpec((1, dg), lambda g, b: (0, g)),
            pl.BlockSpec((seq, dg), lambda g, b: (b, g)),
        ],
        out_specs=[
            pl.BlockSpec((seq, dg), lambda g, b: (b, g)),
            pl.BlockSpec((1, dg, dg), lambda g, b: (g, 0, 0)),
            pl.BlockSpec((1, dg), lambda g, b: (0, g)),
        ],
        out_shape=[
            jax.ShapeDtypeStruct((t, d), F32),
            jax.ShapeDtypeStruct((n_g, dg, dg), F32),
            jax.ShapeDtypeStruct((1, d), F32),
        ],
        compiler_params=_params(("parallel", "arbitrary")),
    )(u, w, scale, dout)


def _adamw(w, g, m, v, *, name):
    shape = w.shape
    c = shape[-1]
    r = w.size // c
    tm = _pick(r, (512, 256, 128, 64, 32, 16, 8))

    def body(w_ref, g_ref, m_ref, v_ref, d_ref, nm_ref, nv_ref):
        gv = g_ref[...]
        nm = ADAM_B1 * m_ref[...] + (1.0 - ADAM_B1) * gv
        nv = ADAM_B2 * v_ref[...] + (1.0 - ADAM_B2) * (gv * gv)
        m_hat = nm / (1.0 - ADAM_B1 ** ADAM_STEP)
        v_hat = nv / (1.0 - ADAM_B2 ** ADAM_STEP)
        d_ref[...] = -ADAM_LR * (m_hat / (jnp.sqrt(v_hat) + ADAM_EPS) + ADAM_WD * w_ref[...])
        nm_ref[...] = nm
        nv_ref[...] = nv

    blk = pl.BlockSpec((tm, c), lambda i: (i, 0))
    out = jax.ShapeDtypeStruct((r, c), F32)
    res = pl.pallas_call(
        body,
        name=name,
        grid=(r // tm,),
        in_specs=[blk] * 4,
        out_specs=[blk] * 3,
        out_shape=[out] * 3,
        compiler_params=_params(("parallel",)),
    )(w.reshape(r, c), g.reshape(r, c), m.reshape(r, c), v.reshape(r, c))
    return tuple(a.reshape(shape) for a in res)


CONV_TC = 256


def _ssd_conv_fwd(proj, col0, n_cols, conv_w, conv_b, n_seq, *, name):
    t = proj.shape[0]
    seq = t // n_seq
    tc = CONV_TC
    off = col0 // tc
    k_taps = conv_w.shape[0]

    def body(h_ref, w_ref, b_ref, o_ref):
        o_ref[...] = _silu(_conv_taps(h_ref[...], w_ref, k_taps) + b_ref[...])

    return pl.pallas_call(
        body,
        name=name,
        grid=(n_seq, n_cols // tc),
        in_specs=[
            pl.BlockSpec((seq, tc), lambda b, j: (b, j + off)),
            pl.BlockSpec((k_taps, tc), lambda b, j: (0, j)),
            pl.BlockSpec((1, tc), lambda b, j: (0, j)),
        ],
        out_specs=pl.BlockSpec((seq, tc), lambda b, j: (b, j)),
        out_shape=jax.ShapeDtypeStruct((t, n_cols), F32),
        compiler_params=_params(("parallel", "parallel")),
    )(proj, conv_w, conv_b)


def _ssd_conv_bwd(proj, col0, conv_w, conv_b, dparts, dproj, n_seq, *, name):
    t = proj.shape[0]
    seq = t // n_seq
    tc = CONV_TC
    off = col0 // tc
    k_taps = conv_w.shape[0]
    widths = [d.shape[1] // tc for d in dparts]
    starts = [sum(widths[:i]) for i in range(len(widths))]
    n_blocks = sum(widths)
    n_parts = len(dparts)

    def body(h_ref, w_ref, b_ref, *rest):
        part_refs = rest[:n_parts]
        dh_ref, dw_ref, db_ref = rest[n_parts + 1:]
        j = pl.program_id(0)
        da = part_refs[-1][...]
        for i in reversed(range(n_parts - 1)):
            da = jnp.where(j < starts[i + 1], part_refs[i][...], da)
        hv = h_ref[...]
        dhc = da * _dsilu(_conv_taps(hv, w_ref, k_taps) + b_ref[...])
        dh, dw = _conv_taps_bwd(hv, dhc, w_ref, k_taps)
        dh_ref[...] = dh.astype(BF16)
        db = jnp.sum(dhc, axis=0, keepdims=True)

        @pl.when(pl.program_id(1) == 0)
        def _():
            dw_ref[...] = dw
            db_ref[...] = db

        @pl.when(pl.program_id(1) > 0)
        def _():
            dw_ref[...] += dw
            db_ref[...] += db

    def part_spec(start, width):
        return pl.BlockSpec((seq, tc), lambda j, b: (b, jnp.clip(j - start, 0, width - 1)))

    n_cols = n_blocks * tc
    return pl.pallas_call(
        body,
        name=name,
        grid=(n_blocks, n_seq),
        in_specs=[
            pl.BlockSpec((seq, tc), lambda j, b: (b, j + off)),
            pl.BlockSpec((k_taps, tc), lambda j, b: (0, j)),
            pl.BlockSpec((1, tc), lambda j, b: (0, j)),
        ] + [part_spec(st, wd) for st, wd in zip(starts, widths)] + [ANY],
        out_specs=[
            pl.BlockSpec((seq, tc), lambda j, b: (b, j + off)),
            pl.BlockSpec((k_taps, tc), lambda j, b: (0, j)),
            pl.BlockSpec((1, tc), lambda j, b: (0, j)),
        ],
        out_shape=[
            jax.ShapeDtypeStruct(dproj.shape, BF16),
            jax.ShapeDtypeStruct((k_taps, n_cols), F32),
            jax.ShapeDtypeStruct((1, n_cols), F32),
        ],
        input_output_aliases={3 + n_parts: 0},
        compiler_params=_params(("parallel", "arbitrary")),
    )(proj, conv_w, conv_b, *dparts, dproj)


def _fill_cols(buf, src, col0, *, name):
    t, c = src.shape
    tm = _pick(t, (1024, 512, 256, 128))

    def body(s_ref, b_ref, o_ref):
        o_ref[...] = s_ref[...].astype(o_ref.dtype)

    return pl.pallas_call(
        body,
        name=name,
        grid=(t // tm,),
        in_specs=[pl.BlockSpec((tm, c), lambda i: (i, 0)), ANY],
        out_specs=pl.BlockSpec((tm, c), lambda i: (i, col0 // c)),
        out_shape=jax.ShapeDtypeStruct(buf.shape, buf.dtype),
        input_output_aliases={1: 0},
        compiler_params=_params(("parallel",)),
    )(src, buf)


def _softplus(x):
    return jnp.maximum(x, 0.0) + jnp.log(1.0 + jnp.exp(-jnp.abs(x)))


def _chunk_decay(dtraw, bias, alog):
    q = dtraw.shape[0]
    dt = _softplus(dtraw + bias)
    a = -jnp.exp(alog)
    rows = lax.broadcasted_iota(jnp.int32, (q, q), 0)
    cols = lax.broadcasted_iota(jnp.int32, (q, q), 1)
    lower = rows >= cols
    acum = jnp.dot(lower.astype(F32), dt * a, precision=lax.Precision.HIGHEST, preferred_element_type=F32)
    return dt, a, acum, acum.T, lower


def _dot_exact(v, sel):
    hi = v.astype(BF16)
    r1 = v - hi.astype(F32)
    mid = r1.astype(BF16)
    lo = (r1 - mid.astype(F32)).astype(BF16)
    return (jnp.dot(hi, sel, preferred_element_type=F32) + jnp.dot(mid, sel, preferred_element_type=F32)
            + jnp.dot(lo, sel, preferred_element_type=F32))


def _head_selectors(gw, p):
    sum_heads = (lax.broadcasted_iota(jnp.int32, (gw, LANE), 0) // p == lax.broadcasted_iota(jnp.int32, (gw, LANE), 1))
    spread = (lax.broadcasted_iota(jnp.int32, (LANE, gw), 0) == lax.broadcasted_iota(jnp.int32, (LANE, gw), 1) // p)
    return sum_heads.astype(BF16), spread.astype(BF16)


def _row_spread(v, spread):
    return _dot_exact(jnp.broadcast_to(v, (8, v.shape[1])), spread)[0:1, :]


def _head_pad(v, r_heads):
    lead = v.shape[:-1]
    vg = v.reshape(lead + (N_SSD_GROUPS, r_heads))
    vg = jnp.pad(vg, [(0, 0)] * len(lead) + [(0, 0), (0, LANE - r_heads)])
    out = vg.reshape(lead + (N_SSD_GROUPS * LANE,))
    return out[None] if out.ndim == 1 else out


def _head_unpad(v, r_heads):
    lead = v.shape[:-1]
    out = v.reshape(lead + (N_SSD_GROUPS, LANE))[..., :r_heads].reshape(lead + (N_SSD_GROUPS * r_heads,))
    return out[0] if (len(lead) == 1 and lead[0] == 1) else out


def _ssd_w_in_layout(w_in, d_inner, d_xbc, r_heads):
    main = w_in[:, :d_inner + d_xbc]
    return jnp.concatenate([main, _head_pad(w_in[:, d_inner + d_xbc:], r_heads)], axis=1)


def _ssd_w_in_unlayout(w, d_inner, d_xbc, r_heads):
    main = w[:, :d_inner + d_xbc]
    return jnp.concatenate([main, _head_unpad(w[:, d_inner + d_xbc:], r_heads)], axis=1)


def _ssd_dims(proj, xbc):
    d_xbc = xbc.shape[1]
    d_inner = d_xbc - 2 * N_SSD_GROUPS * D_STATE
    gw = d_inner // N_SSD_GROUPS
    return d_inner, d_xbc, gw, gw // HEAD_DIM


def _ssd_fwd(proj, xbc, bias_p, alog_p, dskip_p, norm_w, n_seq, *, name):
    t = proj.shape[0]
    d_inner, d_xbc, gw, r_heads = _ssd_dims(proj, xbc)
    q, n, n_g, p = CHUNK, D_STATE, N_SSD_GROUPS, HEAD_DIM
    seq = t // n_seq
    nc = seq // q
    dt_blk0 = (d_inner + d_xbc) // LANE

    def body(x_ref, b_ref, c_ref, z_ref, dtr_ref, bias_ref, alog_ref, dsk_ref, nw_ref, yn_ref, y_ref, hs_ref, h_scr):
        @pl.when(pl.program_id(2) == 0)
        def _():
            h_scr[...] = jnp.zeros_like(h_scr)

        dt, a, acum, acum_t, lower = _chunk_decay(dtr_ref[...], bias_ref[...], alog_ref[...])
        x = x_ref[...]
        bb = b_ref[...].astype(BF16)
        cb = c_ref[...].astype(BF16)
        g_mat = lax.dot_general(cb, bb, (((1,), (1,)), ((), ())), preferred_element_type=F32)
        h_prev = h_scr[...]
        hs_ref[...] = h_prev
        c_h = jnp.dot(cb, h_prev.astype(BF16), preferred_element_type=F32)
        _, spread = _head_selectors(gw, p)
        acum_s = _dot_exact(acum, spread)
        a_last_s = acum_s[q - 1:q, :]
        xdt = x * _dot_exact(dt, spread)
        xdt_b = xdt.astype(BF16)
        ys = []
        for h in range(r_heads):
            decay = jnp.exp(jnp.where(lower, acum[:, h:h + 1] - acum_t[h:h + 1, :], -jnp.inf))
            ys.append(jnp.dot((g_mat * decay).astype(BF16), xdt_b[:, h * p:(h + 1) * p], preferred_element_type=F32))
        y = jnp.concatenate(ys, axis=1) + jnp.exp(acum_s) * c_h + _row_spread(dsk_ref[...], spread) * x
        xd = xdt * jnp.exp(a_last_s - acum_s)
        states = lax.dot_general(bb, xd.astype(BF16), (((0,), (0,)), ((), ())), preferred_element_type=F32)
        h_scr[...] = h_prev * jnp.exp(a_last_s) + states
        y_ref[...] = y
        gated = y * _silu(z_ref[...])
        rstd = lax.rsqrt(jnp.mean(gated * gated, axis=-1, keepdims=True) + EPS)
        yn_ref[...] = (gated * rstd * nw_ref[...]).astype(BF16)

    row = lambda b, g, c: b * nc + c
    vec = pl.BlockSpec((1, LANE), lambda b, g, c: (0, g))
    return pl.pallas_call(
        body,
        name=name,
        grid=(n_seq, n_g, nc),
        in_specs=[
            pl.BlockSpec((q, gw), lambda b, g, c: (row(b, g, c), g)),
            pl.BlockSpec((q, n), lambda b, g, c: (row(b, g, c), d_inner // n + g)),
            pl.BlockSpec((q, n), lambda b, g, c: (row(b, g, c), d_inner // n + n_g + g)),
            pl.BlockSpec((q, gw), lambda b, g, c: (row(b, g, c), g)),
            pl.BlockSpec((q, LANE), lambda b, g, c: (row(b, g, c), dt_blk0 + g)),
            vec, vec, vec,
            pl.BlockSpec((1, gw), lambda b, g, c: (0, g)),
        ],
        out_specs=[
            pl.BlockSpec((q, gw), lambda b, g, c: (row(b, g, c), g)),
            pl.BlockSpec((q, gw), lambda b, g, c: (row(b, g, c), g)),
            pl.BlockSpec((n, gw), lambda b, g, c: (row(b, g, c), g)),
        ],
        out_shape=[
            jax.ShapeDtypeStruct((t, d_inner), BF16),
            jax.ShapeDtypeStruct((t, d_inner), F32),
            jax.ShapeDtypeStruct((n_seq * nc * n, d_inner), F32),
        ],
        scratch_shapes=[pltpu.VMEM((n, gw), F32)],
        compiler_params=_params(("parallel", "parallel", "arbitrary")),
    )(xbc, xbc, xbc, proj, proj, bias_p, alog_p, dskip_p, norm_w)


def _ssd_bwd(proj, xbc, hs, y, dyn, bias_p, alog_p, dskip_p, norm_w, n_seq, *, name):
    t = proj.shape[0]
    d_inner, d_xbc, gw, r_heads = _ssd_dims(proj, xbc)
    q, n, n_g, p = CHUNK, D_STATE, N_SSD_GROUPS, HEAD_DIM
    seq = t // n_seq
    nc = seq // q
    dt_blk0 = (d_inner + d_xbc) // LANE

    def body(x_ref, b_ref, c_ref, z_ref, dtr_ref, bias_ref, alog_ref, dsk_ref, nw_ref, hs_ref, y_ref, dyn_ref,
             dx_ref, db_ref, dc_ref, dz_ref, ddtr_ref, dnw_ref, dbias_ref, dalog_ref, ddsk_ref, dh_scr):
        first = jnp.logical_and(pl.program_id(1) == 0, pl.program_id(2) == 0)

        @pl.when(pl.program_id(2) == 0)
        def _():
            dh_scr[...] = jnp.zeros_like(dh_scr)

        dtraw = dtr_ref[...]
        dt, a, acum, acum_t, lower = _chunk_decay(dtraw, bias_ref[...], alog_ref[...])
        x = x_ref[...]
        bb = b_ref[...].astype(BF16)
        cb = c_ref[...].astype(BF16)
        g_mat = lax.dot_general(cb, bb, (((1,), (1,)), ((), ())), preferred_element_type=F32)

        yv = y_ref[...]
        z = z_ref[...]
        sz = _silu(z)
        gated = yv * sz
        rstd = lax.rsqrt(jnp.mean(gated * gated, axis=-1, keepdims=True) + EPS)
        gn = gated * rstd
        dynv = dyn_ref[...]
        gwt = dynv * nw_ref[...]
        dgated = rstd * (gwt - gn * jnp.mean(gwt * gn, axis=-1, keepdims=True))
        dnw = jnp.sum(dynv * gn, axis=0, keepdims=True)
        dy = dgated * sz
        dz_ref[...] = (dgated * yv * _dsilu(z)).astype(BF16)

        h_prev = hs_ref[...]
        h_prev_b = h_prev.astype(BF16)
        ds = dh_scr[...]
        ds_b = ds.astype(BF16)
        sum_heads, spread = _head_selectors(gw, p)
        acum_s = _dot_exact(acum, spread)
        a_last_s = acum_s[q - 1:q, :]
        dt_s = _dot_exact(dt, spread)
        dsk_s = _row_spread(dsk_ref[...], spread)
        dte_s = jnp.exp(a_last_s - acum_s)
        cd_s = jnp.exp(a_last_s)
        xdt = x * dt_s
        xdt_b = xdt.astype(BF16)
        dy_b = dy.astype(BF16)
        gt_mat = lax.dot_general(bb, cb, (((1,), (1,)), ((), ())), preferred_element_type=F32)
        upper = lax.broadcasted_iota(jnp.int32, (q, q), 0) <= lax.broadcasted_iota(jnp.int32, (q, q), 1)
        dg = jnp.zeros((q, q), F32)
        dxdts, w_diffs = [], []
        for h in range(r_heads):
            hsl = slice(h * p, (h + 1) * p)
            diff = acum[:, h:h + 1] - acum_t[h:h + 1, :]
            decay = jnp.exp(jnp.where(lower, diff, -jnp.inf))
            decay_t = jnp.exp(jnp.where(upper, -diff, -jnp.inf))
            mt_mat = gt_mat * decay_t
            dm = lax.dot_general(dy_b[:, hsl], xdt_b[:, hsl], (((1,), (1,)), ((), ())), preferred_element_type=F32)
            dm_t = lax.dot_general(xdt_b[:, hsl], dy_b[:, hsl], (((1,), (1,)), ((), ())), preferred_element_type=F32)
            dg = dg + dm * decay
            dxdts.append(jnp.dot(mt_mat.astype(BF16), dy_b[:, hsl], preferred_element_type=F32))
            w_diffs.append(dm * (g_mat * decay) - dm_t * mt_mat)
        sel_q = (lax.broadcasted_iota(jnp.int32, (r_heads * q, LANE), 0) // q
                 == lax.broadcasted_iota(jnp.int32, (r_heads * q, LANE), 1)).astype(BF16)
        dacum_diag = _dot_exact(jnp.concatenate(w_diffs, axis=1), sel_q)
        c_h = jnp.dot(cb, h_prev_b, preferred_element_type=F32)
        dxd = jnp.dot(bb, ds_b, preferred_element_type=F32)
        dxdt = jnp.concatenate(dxdts, axis=1) + dxd * dte_s
        dye = dy * jnp.exp(acum_s)
        dye_b = dye.astype(BF16)
        xd = xdt * dte_s
        xd_b = xd.astype(BF16)
        dg_b = dg.astype(BF16)
        dx_ref[...] = dxdt * dt_s + dsk_s * dy
        dc_ref[...] = (jnp.dot(dg_b, bb, preferred_element_type=F32)
                       + lax.dot_general(dye_b, h_prev_b, (((1,), (1,)), ((), ())), preferred_element_type=F32))
        db_ref[...] = (lax.dot_general(dg_b, cb, (((0,), (0,)), ((), ())), preferred_element_type=F32)
                       + lax.dot_general(xd_b, ds_b, (((1,), (1,)), ((), ())), preferred_element_type=F32))
        dh_scr[...] = ds * cd_s + lax.dot_general(cb, dye_b, (((0,), (0,)), ((), ())), preferred_element_type=F32)
        ddt_cols = _dot_exact(x * dxdt, sum_heads)
        dacum_y = _dot_exact(dye * c_h - dxd * xd, sum_heads)
        col_sums = jnp.concatenate([
            jnp.sum(dxd * xd, axis=0, keepdims=True) + jnp.sum(ds * h_prev, axis=0, keepdims=True) * cd_s,
            jnp.sum(dy * x, axis=0, keepdims=True),
            jnp.zeros((6, gw), F32)], axis=0)
        col_sums = _dot_exact(col_sums, sum_heads)
        ddsk = col_sums[1:2, :]
        rows_q = lax.broadcasted_iota(jnp.int32, (q, LANE), 0)
        dacum = dacum_diag + dacum_y + jnp.where(rows_q == q - 1, col_sums[0:1, :], 0.0)
        dadt = jnp.dot(upper.astype(F32), dacum, precision=lax.Precision.HIGHEST, preferred_element_type=F32)
        ddt = dadt * a + ddt_cols
        ddtr = ddt * _sigmoid(dtraw + bias_ref[...])
        ddtr_ref[...] = ddtr
        dbias = jnp.sum(ddtr, axis=0, keepdims=True)
        dalog = jnp.sum(dadt * dt, axis=0, keepdims=True) * a

        @pl.when(first)
        def _():
            dnw_ref[...] = dnw
            dbias_ref[...] = dbias
            dalog_ref[...] = dalog
            ddsk_ref[...] = ddsk

        @pl.when(jnp.logical_not(first))
        def _():
            dnw_ref[...] += dnw
            dbias_ref[...] += dbias
            dalog_ref[...] += dalog
            ddsk_ref[...] += ddsk

    row = lambda g, b, c: b * nc + (nc - 1 - c)
    vec = pl.BlockSpec((1, LANE), lambda g, b, c: (0, g))
    wide = pl.BlockSpec((q, gw), lambda g, b, c: (row(g, b, c), g))
    narrow = pl.BlockSpec((q, n), lambda g, b, c: (row(g, b, c), g))
    return pl.pallas_call(
        body,
        name=name,
        grid=(n_g, n_seq, nc),
        in_specs=[
            wide,
            pl.BlockSpec((q, n), lambda g, b, c: (row(g, b, c), d_inner // n + g)),
            pl.BlockSpec((q, n), lambda g, b, c: (row(g, b, c), d_inner // n + n_g + g)),
            wide,
            pl.BlockSpec((q, LANE), lambda g, b, c: (row(g, b, c), dt_blk0 + g)),
            vec, vec, vec,
            pl.BlockSpec((1, gw), lambda g, b, c: (0, g)),
            pl.BlockSpec((n, gw), lambda g, b, c: (row(g, b, c), g)),
            wide, wide,
        ],
        out_specs=[
            wide, narrow, narrow, wide, narrow,
            pl.BlockSpec((1, gw), lambda g, b, c: (0, g)),
            vec, vec, vec,
        ],
        out_shape=[
            jax.ShapeDtypeStruct((t, d_inner), F32),
            jax.ShapeDtypeStruct((t, n_g * n), F32),
            jax.ShapeDtypeStruct((t, n_g * n), F32),
            jax.ShapeDtypeStruct(proj.shape, BF16),
            jax.ShapeDtypeStruct((t, n_g * LANE), F32),
            jax.ShapeDtypeStruct((1, d_inner), F32),
            jax.ShapeDtypeStruct((1, n_g * LANE), F32),
            jax.ShapeDtypeStruct((1, n_g * LANE), F32),
            jax.ShapeDtypeStruct((1, n_g * LANE), F32),
        ],
        scratch_shapes=[pltpu.VMEM((n, gw), F32)],
        compiler_params=_params(("parallel", "arbitrary", "arbitrary")),
    )(xbc, xbc, xbc, proj, proj, bias_p, alog_p, dskip_p, norm_w, hs, y, dyn)


MESH_IDS = pl.DeviceIdType.MESH


def _my_index():
    return 4 * lax.axis_index("x") + 2 * lax.axis_index("y") + lax.axis_index("c")


def _all_gather(shard, *, name):
    def body(x_ref, out_ref, send_sems, recv_sems, local_sem):
        x, y, c = lax.axis_index("x"), lax.axis_index("y"), lax.axis_index("c")
        me, sibling = (x, y, c), (x, y, 1 - c)
        chips = [(1 - x, y), (x, 1 - y), (1 - x, 1 - y)]

        def blk(px, py, pc):
            return out_ref.at[4 * px + 2 * py + pc]

        def copy(k, block, to, src=None):
            return pltpu.make_async_remote_copy(
                src_ref=blk(*block) if src is None else src, dst_ref=blk(*block),
                send_sem=send_sems.at[k], recv_sem=recv_sems.at[k], device_id=to, device_id_type=MESH_IDS)

        mine = pltpu.make_async_copy(x_ref, blk(*me), local_sem)
        mine.start()
        first = [copy(0, me, sibling, src=x_ref)]
        first += [copy(1 + j, me, (*chip, c), src=x_ref) for j, chip in enumerate(chips)]
        for cp in first:
            cp.start()
        passed = [copy(4 + j, (*chip, c), sibling) for j, chip in enumerate(chips)]
        for j, chip in enumerate(chips):
            copy(1 + j, (*chip, c), me).wait_recv()
            passed[j].start()
        copy(0, sibling, me).wait_recv()
        for j, chip in enumerate(chips):
            copy(4 + j, (*chip, 1 - c), me).wait_recv()
        for cp in first + passed:
            cp.wait_send()
        mine.wait()

    return pl.pallas_call(
        body,
        name=name,
        in_specs=[ANY],
        out_specs=ANY,
        out_shape=jax.ShapeDtypeStruct((N_DEV,) + shard.shape, shard.dtype),
        scratch_shapes=[pltpu.SemaphoreType.DMA((7,)), pltpu.SemaphoreType.DMA((7,)), pltpu.SemaphoreType.DMA],
    )(shard)


def _exchange(fulls, *, name):
    n_arr = len(fulls)

    def body(*refs):
        in_refs, out_refs = refs[:n_arr], refs[n_arr:2 * n_arr]
        send_sems, recv_sems, local_sems = refs[2 * n_arr:]
        x, y, c = lax.axis_index("x"), lax.axis_index("y"), lax.axis_index("c")
        me = 4 * x + 2 * y + c
        mine = [pltpu.make_async_copy(in_refs[a].at[me], out_refs[a].at[me], local_sems.at[a]) for a in range(n_arr)]
        for cp in mine:
            cp.start()
        copies = []
        for a in range(n_arr):
            for k in range(1, N_DEV):
                px = 1 - x if k & 4 else x
                py = 1 - y if k & 2 else y
                pc = 1 - c if k & 1 else c
                peer = 4 * px + 2 * py + pc
                sem = a * (N_DEV - 1) + k - 1
                copies.append((
                    pltpu.make_async_remote_copy(
                        src_ref=in_refs[a].at[peer], dst_ref=out_refs[a].at[me], send_sem=send_sems.at[sem],
                        recv_sem=recv_sems.at[sem], device_id=(px, py, pc), device_id_type=MESH_IDS),
                    pltpu.make_async_remote_copy(
                        src_ref=in_refs[a].at[peer], dst_ref=out_refs[a].at[peer], send_sem=send_sems.at[sem],
                        recv_sem=recv_sems.at[sem], device_id=(px, py, pc), device_id_type=MESH_IDS)))
        for send, _ in copies:
            send.start()
        for send, recv in copies:
            recv.wait_recv()
            send.wait_send()
        for cp in mine:
            cp.wait()

    n_sem = n_arr * (N_DEV - 1)
    return pl.pallas_call(
        body,
        name=name,
        in_specs=[ANY] * n_arr,
        out_specs=[ANY] * n_arr,
        out_shape=[jax.ShapeDtypeStruct(f.shape, f.dtype) for f in fulls],
        scratch_shapes=[pltpu.SemaphoreType.DMA((n_sem,)), pltpu.SemaphoreType.DMA((n_sem,)),
                        pltpu.SemaphoreType.DMA((n_arr,))],
    )(*fulls)


HBM_SPEC = pl.BlockSpec(memory_space=pltpu.HBM)
SEM_SPEC = pl.BlockSpec(memory_space=pltpu.SEMAPHORE)
SPLIT_COPY_PARAMS = pltpu.CompilerParams(has_side_effects=pltpu.SideEffectType.DATAFLOW_SIDE_EFFECTING)


def _peer_list():
    x, y, c = lax.axis_index("x"), lax.axis_index("y"), lax.axis_index("c")
    peers = []
    for k in range(1, N_DEV):
        px = 1 - x if k & 4 else x
        py = 1 - y if k & 2 else y
        pc = 1 - c if k & 1 else c
        peers.append(((px, py, pc), 4 * px + 2 * py + pc))
    return 4 * x + 2 * y + c, peers


def _push_copies(src_refs, land_refs, send_sems, recv_sems, blockwise):
    me, peers = _peer_list()
    copies = []
    for a, (src_ref, land_ref) in enumerate(zip(src_refs, land_refs)):
        for k, (dev, idx) in enumerate(peers):
            sem = a * (N_DEV - 1) + k
            src = src_ref.at[idx] if blockwise else src_ref
            copies.append(tuple(
                pltpu.make_async_remote_copy(src_ref=src, dst_ref=land_ref.at[slot], send_sem=send_sems.at[sem],
                                             recv_sem=recv_sems.at[sem], device_id=dev, device_id_type=MESH_IDS)
                for slot in (me, idx)))
    return copies


def _push_start(srcs, blockwise, after, *, name):
    n = len(srcs)
    blocks = [s_.shape[1:] if blockwise else s_.shape for s_ in srcs]

    def body(*refs):
        src_refs, land_refs = refs[:n], refs[n:2 * n]
        send_sems, recv_sems = refs[2 * n + 1], refs[2 * n + 2]
        token = refs[-1]
        for send, _ in _push_copies(src_refs, land_refs, send_sems, recv_sems, blockwise):
            send.start()
        token[...] = jnp.zeros_like(token)

    n_sem = n * (N_DEV - 1)
    lands = [lax.empty((N_DEV,) + b, s_.dtype) for b, s_ in zip(blocks, srcs)]
    out = pl.pallas_call(
        body,
        name=name,
        in_specs=[HBM_SPEC] * (2 * n) + [ANY],
        out_specs=(SEM_SPEC, SEM_SPEC) + (HBM_SPEC,) * (2 * n) + (pl.BlockSpec(memory_space=pltpu.VMEM),),
        out_shape=(pltpu.SemaphoreType.DMA((n_sem,)), pltpu.SemaphoreType.DMA((n_sem,)))
        + tuple(pltpu.HBM(a.shape, a.dtype) for a in list(srcs) + lands)
        + (jax.ShapeDtypeStruct((8, LANE), F32),),
        input_output_aliases={i: 2 + i for i in range(2 * n)},
        compiler_params=SPLIT_COPY_PARAMS,
    )(*[pltpu.with_memory_space_constraint(a, pltpu.HBM) for a in list(srcs) + lands], after)
    return out[0], out[1], out[2:2 + n], out[2 + n:2 + 2 * n], out[-1]


def _push_wait(send_sems, recv_sems, srcs, lands, blockwise, after, *, name):
    n = len(srcs)

    def body(*refs):
        src_refs, land_refs = refs[:n], refs[n:2 * n]
        send_sems, recv_sems = refs[2 * n], refs[2 * n + 1]
        for send, recv in _push_copies(src_refs, land_refs, send_sems, recv_sems, blockwise):
            send.wait_send()
            recv.wait_recv()

    out = pl.pallas_call(
        body,
        name=name,
        in_specs=[HBM_SPEC] * (2 * n) + [SEM_SPEC, SEM_SPEC, ANY],
        out_specs=(HBM_SPEC,) * (2 * n),
        out_shape=tuple(pltpu.HBM(a.shape, a.dtype) for a in list(srcs) + list(lands)),
        input_output_aliases={i: i for i in range(2 * n)},
        compiler_params=SPLIT_COPY_PARAMS,
    )(*srcs, *lands, send_sems, recv_sems, after)
    return out[n:]


def _with_own_slot(landing, own):
    slot = lax.broadcasted_iota(jnp.int32, (N_DEV,) + (1,) * own.ndim, 0)
    return jnp.where(slot == _my_index(), own[None], landing)


def _sum_slots(parts, *, name):
    shape = parts.shape[1:]
    n, c = parts.shape[0], parts.shape[-1]
    r = parts.size // (n * c)
    tm = _pick(r, (256, 128, 64, 32, 16, 8))

    def body(p_ref, o_ref):
        acc = p_ref[0].astype(F32)
        for s in range(1, n):
            acc = acc + p_ref[s].astype(F32)
        o_ref[...] = acc

    return pl.pallas_call(
        body,
        name=name,
        grid=(r // tm,),
        in_specs=[pl.BlockSpec((n, tm, c), lambda i: (0, i, 0))],
        out_specs=pl.BlockSpec((tm, c), lambda i: (i, 0)),
        out_shape=jax.ShapeDtypeStruct((r, c), F32),
        compiler_params=_params(("parallel",)),
    )(parts.reshape(n, r, c)).reshape(shape)


def _row_count(shape):
    c = shape[-1]
    rows = 1
    for s in shape[:-1]:
        rows *= s
    return rows, c, c + (-c) % LANE


PACK_ROWS = 256


def _pack_rows(arrays):
    pieces = []
    for a in arrays:
        rows, c, cp = _row_count(a.shape)
        a2 = a.reshape(rows, c)
        if cp > c:
            a2 = jnp.pad(a2, ((0, 0), (0, cp - c)))
        a2 = a2.reshape(rows * cp // LANE, LANE)
        if a2.shape[0] % 8:
            a2 = jnp.pad(a2, ((0, 8 - a2.shape[0] % 8), (0, 0)))
        pieces.append(a2)
    total = sum(p.shape[0] for p in pieces)
    if total % PACK_ROWS:
        pieces.append(jnp.zeros((PACK_ROWS - total % PACK_ROWS, LANE), F32))
    return jnp.concatenate(pieces, axis=0)


def _unpack_rows(packed, shapes, lead=()):
    out, off = [], 0
    for shp in shapes:
        rows, c, cp = _row_count(shp)
        n_rows = rows * cp // LANE
        seg = packed[..., off:off + n_rows, :].reshape(lead + (rows, cp))
        out.append(seg[..., :c].reshape(lead + tuple(shp)))
        off += n_rows + (-n_rows) % 8
    return out


def _unshard(stacked, axis):
    moved = jnp.moveaxis(stacked, 0, axis)
    shp = moved.shape
    return moved.reshape(shp[:axis] + (shp[axis] * shp[axis + 1],) + shp[axis + 2:])


def _shard_major(full, axis):
    shp = full.shape
    split = full.reshape(shp[:axis] + (N_DEV, shp[axis] // N_DEV) + shp[axis + 1:])
    return jnp.moveaxis(split, axis, 0)


def _my_shard(full, axis):
    size = full.shape[axis] // N_DEV
    return lax.dynamic_slice_in_dim(full, _my_index() * size, size, axis)


def _local_step(x, target, w, fetch, emit, n_seq):
    depth, d_model = w["norm_mix_pre"].shape
    d_inner = w["ssd_norm_w"].shape[1]
    d_xbc = w["ssd_conv_w"].shape[2]
    saved = []
    for i in range(depth):
        j = i // 2
        m, token = fetch(i, x)
        mix_pre_w = w["norm_mix_pre"][i:i + 1] + jnp.tile(token[0:1, :], (1, d_model // LANE))
        s = {"x": x, "m": m, "mix_pre_w": mix_pre_w}
        if i % 2 == 0:
            u = _rms_fwd(x, mix_pre_w, out_dtype=BF16, name=f"l{i}_mix_pre")
            proj = _mm(u, m["ssd_w_in"], name=f"l{i}_ssd_in")
            xbc = _ssd_conv_fwd(proj, d_inner, d_xbc, w["ssd_conv_w"][j], w["ssd_conv_b"][j:j + 1], n_seq,
                                name=f"l{i}_ssd_conv")
            yn, y, hs = _ssd_fwd(proj, xbc, w["ssd_dt_bias"][j:j + 1], w["ssd_a_log"][j:j + 1], w["ssd_d"][j:j + 1],
                                 w["ssd_norm_w"][j:j + 1], n_seq, name=f"l{i}_ssd_scan")
            mix = _mm(yn, m["ssd_w_out"], name=f"l{i}_ssd_out")
            s.update(u=u, proj=proj, xbc=xbc, yn=yn, y=y, hs=hs)
        else:
            u = _rms_fwd(x, mix_pre_w, out_dtype=F32, name=f"l{i}_mix_pre")
            mix = _pool_fwd(u, m["pool_w"], w["pool_scale"][j:j + 1], n_seq, name=f"l{i}_pool")
            s.update(u=u)
        x1 = _res_rms_fwd(x, mix, w["norm_mix_post"][i:i + 1], name=f"l{i}_mix_post")
        n = _rms_fwd(x1, w["norm_ffn_pre"][i:i + 1], out_dtype=BF16, name=f"l{i}_ffn_pre")
        h = _mm(n, m["ffn_w_up"], out_dtype=BF16, name=f"l{i}_ffn_up")
        a, hc = _ffn_act_fwd(h, w["ffn_conv_w"][i], w["ffn_conv_b"][i:i + 1], n_seq, name=f"l{i}_ffn_act")
        f = _mm(a, m["ffn_w_down"], name=f"l{i}_ffn_down")
        x = _res_rms_fwd(x1, f, w["norm_ffn_post"][i:i + 1], name=f"l{i}_ffn_post")
        s.update(mix=mix, x1=x1, n=n, h=h, hc=hc, a=a, f=f)
        saved.append(s)

    loss, dx = _loss_head(x, target)
    grads = {k: [None] * len(w[k]) for k in SMALL}
    for i in reversed(range(depth)):
        j = i // 2
        s = saved[i]
        m, gm = s["m"], {}
        df, grads["norm_ffn_post"][i] = _rms_bwd(s["f"], w["norm_ffn_post"][i:i + 1], dx, None, name=f"l{i}_ffn_post_b")
        da = _mm(df, m["ffn_w_down"], tb=True, out_dtype=BF16, name=f"l{i}_ffn_down_bx")
        gm["ffn_w_down"] = _mm(s["a"], df, ta=True, name=f"l{i}_ffn_down_bw")
        dh, grads["ffn_conv_w"][i], grads["ffn_conv_b"][i] = _ffn_act_bwd(
            s["h"], s["hc"], w["ffn_conv_w"][i], da, n_seq, name=f"l{i}_ffn_act_b")
        dn = _mm(dh, m["ffn_w_up"], tb=True, name=f"l{i}_ffn_up_bx")
        gm["ffn_w_up"] = _mm(s["n"], dh, ta=True, name=f"l{i}_ffn_up_bw")
        dx1, grads["norm_ffn_pre"][i] = _rms_bwd(s["x1"], w["norm_ffn_pre"][i:i + 1], dn, dx, name=f"l{i}_ffn_pre_b")
        dmix, grads["norm_mix_post"][i] = _rms_bwd(s["mix"], w["norm_mix_post"][i:i + 1], dx1, None,
                                                   name=f"l{i}_mix_post_b")
        if i % 2 == 0:
            dyn = _mm(dmix, m["ssd_w_out"], tb=True, name=f"l{i}_ssd_out_bx")
            gm["ssd_w_out"] = _mm(s["yn"], dmix, ta=True, name=f"l{i}_ssd_out_bw")
            dxs, db, dc, dz, ddtr, dnw, dbias, dalog, ddsk = _ssd_bwd(
                s["proj"], s["xbc"], s["hs"], s["y"], dyn, w["ssd_dt_bias"][j:j + 1], w["ssd_a_log"][j:j + 1],
                w["ssd_d"][j:j + 1], w["ssd_norm_w"][j:j + 1], n_seq, name=f"l{i}_ssd_scan_b")
            grads["ssd_norm_w"][j], grads["ssd_dt_bias"][j], grads["ssd_a_log"][j], grads["ssd_d"][j] = (
                dnw, dbias, dalog, ddsk)
            dproj, grads["ssd_conv_w"][j], grads["ssd_conv_b"][j] = _ssd_conv_bwd(
                s["proj"], d_inner, w["ssd_conv_w"][j], w["ssd_conv_b"][j:j + 1], (dxs, db, dc), dz, n_seq,
                name=f"l{i}_ssd_conv_b")
            dproj = _fill_cols(dproj, ddtr, d_inner + d_xbc, name=f"l{i}_ssd_dt_b")
            du = _mm(dproj, m["ssd_w_in"], tb=True, name=f"l{i}_ssd_in_bx")
            gm["ssd_w_in"] = _mm(s["u"], dproj, ta=True, name=f"l{i}_ssd_in_bw")
        else:
            du, gm["pool_w"], grads["pool_scale"][j] = _pool_bwd(
                s["u"], m["pool_w"], w["pool_scale"][j:j + 1], dmix, n_seq, name=f"l{i}_pool_b")
        dx, grads["norm_mix_pre"][i] = _rms_bwd(s["x"], s["mix_pre_w"], du, dx1, name=f"l{i}_mix_pre_b")
        emit(i, gm, dx)
    return loss, dx, grads


BIG = (("ssd_w_in", 2), ("ssd_w_out", 1), ("pool_w", 2), ("ffn_w_up", 2), ("ffn_w_down", 1))
SMALL_SHARDED = (("ssd_conv_w", 2), ("ffn_conv_w", 2), ("pool_scale", 1))
SMALL = ("ssd_conv_w", "ssd_conv_b", "ssd_dt_bias", "ssd_a_log", "ssd_d", "ssd_norm_w", "pool_scale", "ffn_conv_w",
         "ffn_conv_b", "norm_mix_pre", "norm_mix_post", "norm_ffn_pre", "norm_ffn_post")
WEIGHTS = ("ssd_w_in", "ssd_conv_w", "ssd_conv_b", "ssd_dt_bias", "ssd_a_log", "ssd_d", "ssd_norm_w", "ssd_w_out",
           "pool_w", "pool_scale", "ffn_w_up", "ffn_conv_w", "ffn_conv_b", "ffn_w_down", "norm_mix_pre",
           "norm_mix_post", "norm_ffn_pre", "norm_ffn_post")


def _ssd_sizes(d_inner):
    return d_inner + 2 * N_SSD_GROUPS * D_STATE, d_inner // HEAD_DIM // N_SSD_GROUPS


def _small_compute_layout(full, d_inner):
    _, r_heads = _ssd_sizes(d_inner)
    w = {k: full[k] for k in SMALL}
    for k in ("ssd_dt_bias", "ssd_a_log", "ssd_d"):
        w[k] = _head_pad(full[k], r_heads)
    for k in ("ffn_conv_w", "ffn_conv_b"):
        w[k] = _interleave(full[k])
    return w


def _matmul_compute_layout(k, full, d_inner):
    d_xbc, r_heads = _ssd_sizes(d_inner)
    if k == "ssd_w_in":
        return _ssd_w_in_layout(full, d_inner, d_xbc, r_heads)
    if k == "ffn_w_up":
        return _interleave(full)
    return full


def _layer_matrices(i):
    mixer = (("ssd_w_in", 1, i // 2), ("ssd_w_out", 0, i // 2)) if i % 2 == 0 else (("pool_w", 1, i // 2),)
    return mixer + (("ffn_w_up", 1, i), ("ffn_w_down", 0, i))


def _matmul_grad_reference_layout(k, g, d_inner):
    d_xbc, r_heads = _ssd_sizes(d_inner)
    if k == "ssd_w_in":
        return _ssd_w_in_unlayout(g, d_inner, d_xbc, r_heads)
    if k == "ffn_w_up":
        return _deinterleave(g)
    return g


def _small_grads_reference_layout(grads, shapes, d_inner):
    _, r_heads = _ssd_sizes(d_inner)
    g = {k: jnp.stack(grads[k]) for k in SMALL}
    for k in ("ssd_dt_bias", "ssd_a_log", "ssd_d"):
        g[k] = _head_unpad(g[k][:, 0], r_heads)
    for k in ("ffn_conv_w", "ffn_conv_b"):
        g[k] = _deinterleave(g[k])
    return {k: v.reshape(shapes[k]) for k, v in g.items()}


def kernel(x, ssd_w_in, ssd_conv_w, ssd_conv_b, ssd_dt_bias, ssd_a_log, ssd_d, ssd_norm_w, ssd_w_out, pool_w, pool_scale, ffn_w_up, ffn_conv_w, ffn_conv_b, ffn_w_down, norm_mix_pre, norm_mix_post, norm_ffn_pre, norm_ffn_post, loss_target, m_ssd_w_in, m_ssd_conv_w, m_ssd_conv_b, m_ssd_dt_bias, m_ssd_a_log, m_ssd_d, m_ssd_norm_w, m_ssd_w_out, m_pool_w, m_pool_scale, m_ffn_w_up, m_ffn_conv_w, m_ffn_conv_b, m_ffn_w_down, m_norm_mix_pre, m_norm_mix_post, m_norm_ffn_pre, m_norm_ffn_post, v_ssd_w_in, v_ssd_conv_w, v_ssd_conv_b, v_ssd_dt_bias, v_ssd_a_log, v_ssd_d, v_ssd_norm_w, v_ssd_w_out, v_pool_w, v_pool_scale, v_ffn_w_up, v_ffn_conv_w, v_ffn_conv_b, v_ffn_w_down, v_norm_mix_pre, v_norm_mix_post, v_norm_ffn_pre, v_norm_ffn_post):
    shards = dict(ssd_w_in=ssd_w_in, ssd_conv_w=ssd_conv_w, ssd_conv_b=ssd_conv_b, ssd_dt_bias=ssd_dt_bias,
                  ssd_a_log=ssd_a_log, ssd_d=ssd_d, ssd_norm_w=ssd_norm_w, ssd_w_out=ssd_w_out, pool_w=pool_w,
                  pool_scale=pool_scale, ffn_w_up=ffn_w_up, ffn_conv_w=ffn_conv_w, ffn_conv_b=ffn_conv_b,
                  ffn_w_down=ffn_w_down, norm_mix_pre=norm_mix_pre, norm_mix_post=norm_mix_post,
                  norm_ffn_pre=norm_ffn_pre, norm_ffn_post=norm_ffn_post)
    moments_m = dict(zip(WEIGHTS, (m_ssd_w_in, m_ssd_conv_w, m_ssd_conv_b, m_ssd_dt_bias, m_ssd_a_log, m_ssd_d, m_ssd_norm_w, m_ssd_w_out, m_pool_w, m_pool_scale, m_ffn_w_up, m_ffn_conv_w, m_ffn_conv_b, m_ffn_w_down, m_norm_mix_pre, m_norm_mix_post, m_norm_ffn_pre, m_norm_ffn_post)))
    moments_v = dict(zip(WEIGHTS, (v_ssd_w_in, v_ssd_conv_w, v_ssd_conv_b, v_ssd_dt_bias, v_ssd_a_log, v_ssd_d, v_ssd_norm_w, v_ssd_w_out, v_pool_w, v_pool_scale, v_ffn_w_up, v_ffn_conv_w, v_ffn_conv_b, v_ffn_w_down, v_norm_mix_pre, v_norm_mix_post, v_norm_ffn_pre, v_norm_ffn_post)))
    n_seq, seq, d_model = x.shape
    t = n_seq * seq

    full = dict(shards)
    small_all = _all_gather(_pack_rows([shards[k] for k, _ in SMALL_SHARDED]), name="gather_small_weights")
    small_stacked = _unpack_rows(small_all, [shards[k].shape for k, _ in SMALL_SHARDED], lead=(N_DEV,))
    for (k, axis), st in zip(SMALL_SHARDED, small_stacked):
        full[k] = _unshard(st, axis)
    d_inner = ssd_norm_w.shape[1]
    w = _small_compute_layout(full, d_inner)
    depth = norm_mix_pre.shape[0]
    x2 = x.reshape(t, d_model)

    shard16 = {k: shards[k].astype(BF16) for k, _ in BIG}
    fetches = {}

    def start_fetch(i, after):
        fetches[i] = _push_start([shard16[k][l] for k, _, l in _layer_matrices(i)], False, after, name=f"fetch{i}_start")

    def fetch(i, x_now):
        if i == 0:
            start_fetch(0, x_now)
        send, recv, srcs, lands, _ = fetches[i]
        lands = _push_wait(send, recv, srcs, lands, False, x_now, name=f"fetch{i}_wait")
        mats = {}
        for (k, axis, l), land in zip(_layer_matrices(i), lands):
            whole = _unshard(_with_own_slot(land, shard16[k][l]), axis)
            mats[k] = _matmul_compute_layout(k, whole, d_inner)
        if i + 1 < depth:
            start_fetch(i + 1, lands[0])
            return mats, fetches[i + 1][4]
        return mats, jnp.zeros((8, LANE), F32)

    g_layers = {}

    def emit(i, gm, dx_now):
        blocks = [_shard_major(_matmul_grad_reference_layout(k, gm[k].astype(BF16), d_inner), axis)
                  for k, axis, _ in _layer_matrices(i)]
        received = _exchange(blocks, name=f"exchange{i}")
        for (k, _, l), r in zip(_layer_matrices(i), received):
            g_layers[k, l] = _sum_slots(r, name=f"sum{i}_{k}")

    loss, dx, grads = _local_step(x2, loss_target.reshape(t, d_model), w, fetch, emit, n_seq)
    loss = lax.psum(loss, ("x", "y", "c"))
    g_shard = {k: jnp.stack([g_layers[k, l] for l in range(shards[k].shape[0])]) for k, _ in BIG}

    small_shapes = {k: full[k].shape for k in SMALL}
    g_small = _small_grads_reference_layout(grads, small_shapes, d_inner)
    s_all = _all_gather(_pack_rows([g_small[k] for k in SMALL]), name="gather_small_grads")
    for k, g in zip(SMALL, _unpack_rows(_sum_slots(s_all, name="sum_small_grads"), [small_shapes[k] for k in SMALL])):
        g_shard[k] = g
    for k, axis in SMALL_SHARDED:
        g_shard[k] = _my_shard(g_shard[k], axis)

    deltas, new_m, new_v = {}, {}, {}
    for k in WEIGHTS:
        deltas[k], new_m[k], new_v[k] = _adamw(shards[k], g_shard[k], moments_m[k], moments_v[k], name=f"adamw_{k}")
    return (loss, dx.reshape(x.shape), *[g_shard[k] for k in WEIGHTS], *[deltas[k] for k in WEIGHTS],
            *[new_m[k] for k in WEIGHTS], *[new_v[k] for k in WEIGHTS])
```

```python
import functools

import jax
import jax.numpy as jnp
from jax import lax
from jax.experimental import pallas as pl
from jax.experimental.pallas import tpu as pltpu

F32 = jnp.float32
BF16 = jnp.bfloat16

N_DEV = 8
HEAD_DIM = 64
N_SSD_GROUPS = 4
D_STATE = 128
CHUNK = 128
POOL_WINDOWS = (2, 4, 8, 16)
EPS = 1e-6
LANE = 128
ADAM_LR = 0.001
ADAM_B1 = 0.9
ADAM_B2 = 0.999
ADAM_EPS = 1e-08
ADAM_WD = 0.01
ADAM_STEP = 10
VMEM_LIMIT = 56 * 1024 * 1024
ANY = pl.BlockSpec(memory_space=pl.ANY)


def _pick(n, cands):
    for c in cands:
        if n % c == 0:
            return c
    return n


def _params(sem):
    return pltpu.CompilerParams(dimension_semantics=sem, vmem_limit_bytes=VMEM_LIMIT)


def _sigmoid(x):
    return 0.5 * jnp.tanh(0.5 * x) + 0.5


def _silu(x):
    return x * _sigmoid(x)


def _dsilu(x):
    s = _sigmoid(x)
    return s * (1.0 + x * (1.0 - s))


def _shift_down(x, s):
    rows = lax.broadcasted_iota(jnp.int32, x.shape, 0)
    return jnp.where(rows >= s, pltpu.roll(x, s, 0), 0.0)


def _shift_up(x, s):
    n = x.shape[0]
    rows = lax.broadcasted_iota(jnp.int32, x.shape, 0)
    return jnp.where(rows < n - s, pltpu.roll(x, n - s, 0), 0.0)


MM_VMEM_BUDGET = 40 * 1024 * 1024
MM_STEP_BYTES = 1_300_000
MM_SUB = 512


def _mm_tiles(m, n, k, a_bytes, b_bytes, o_bytes):
    def cands(dim, sizes):
        out = [s for s in sizes if s <= dim and dim % s == 0]
        return out or [dim]

    best = None
    for tm in cands(m, (m, m // 2, 2048, 1024, 512, 256, 128)):
        if tm % LANE:
            continue
        for tn in cands(n, (n, n // 2, n // 4, 2048, 1024, 512, 256, 128)):
            if tn % (2 * LANE) and tn != n:
                continue
            for tk in cands(k, (k, k // 2, 2048, 1024, 512)):
                if tk % LANE:
                    continue
                nk = k // tk
                acc = tm * tn * 4 if (nk > 1 and o_bytes != 4) else 0
                temps = tm * min(tn, MM_SUB) * 4 + (tm * tk * 2 if a_bytes == 4 else 0) + (tk * tn * 2 if b_bytes == 4 else 0)
                vmem = 2 * (tm * tk * a_bytes + tk * tn * b_bytes + tm * tn * o_bytes) + acc + temps
                if vmem > MM_VMEM_BUDGET:
                    continue
                steps = (m // tm) * (n // tn) * nk
                cost = (m * k * a_bytes * (n // tn) + k * n * b_bytes * (m // tm) + m * n * o_bytes
                        + steps * MM_STEP_BYTES)
                if best is None or cost < best[0]:
                    best = (cost, tm, tn, tk)
    return best[1:]


def _mm(a, b, *, ta=False, tb=False, out_dtype=F32, name="mm"):
    m, k = (a.shape[1], a.shape[0]) if ta else a.shape
    n = b.shape[0] if tb else b.shape[1]
    o_bytes = jnp.dtype(out_dtype).itemsize
    tm, tn, tk = _mm_tiles(m, n, k, a.dtype.itemsize, b.dtype.itemsize, o_bytes)
    nk = k // tk
    sub = _pick(tn, (MM_SUB, 256))
    use_acc = nk > 1 and o_bytes != 4
    a_spec = pl.BlockSpec((tk, tm), lambda i, j, kk: (kk, i)) if ta else pl.BlockSpec((tm, tk), lambda i, j, kk: (i, kk))
    b_spec = pl.BlockSpec((tn, tk), lambda i, j, kk: (j, kk)) if tb else pl.BlockSpec((tk, tn), lambda i, j, kk: (kk, j))
    dims = (((1,), (1 if tb else 0,)), ((), ()))

    def body(a_ref, b_ref, o_ref, *scratch):
        kk = pl.program_id(2)
        acc_ref = scratch[0] if use_acc else o_ref
        if nk > 1:
            @pl.when(kk == 0)
            def _():
                acc_ref[...] = jnp.zeros_like(acc_ref)

        av = a_ref[...].astype(BF16)
        if ta:
            av = av.T
        for s in range(tn // sub):
            cols = slice(s * sub, (s + 1) * sub)
            bv = (b_ref[cols, :] if tb else b_ref[:, cols]).astype(BF16)
            part = lax.dot_general(av, bv, dims, preferred_element_type=F32)
            if nk == 1:
                o_ref[:, cols] = part.astype(out_dtype)
            else:
                acc_ref[:, cols] += part
        if use_acc:
            @pl.when(kk == nk - 1)
            def _():
                o_ref[...] = acc_ref[...].astype(out_dtype)

    return pl.pallas_call(
        body,
        name=name,
        grid=(m // tm, n // tn, nk),
        in_specs=[a_spec, b_spec],
        out_specs=pl.BlockSpec((tm, tn), lambda i, j, kk: (i, j)),
        out_shape=jax.ShapeDtypeStruct((m, n), out_dtype),
        scratch_shapes=[pltpu.VMEM((tm, tn), F32)] if use_acc else [],
        compiler_params=_params(("parallel", "parallel", "arbitrary")),
    )(a, b)


def _rms_fwd(x, w, *, out_dtype, name):
    t, d = x.shape
    tm = _pick(t, (512, 256, 128))

    def body(x_ref, w_ref, o_ref):
        xv = x_ref[...]
        rstd = lax.rsqrt(jnp.mean(xv * xv, axis=-1, keepdims=True) + EPS)
        o_ref[...] = (xv * rstd * w_ref[...]).astype(out_dtype)

    return pl.pallas_call(
        body,
        name=name,
        grid=(t // tm,),
        in_specs=[pl.BlockSpec((tm, d), lambda i: (i, 0)), pl.BlockSpec((1, d), lambda i: (0, 0))],
        out_specs=pl.BlockSpec((tm, d), lambda i: (i, 0)),
        out_shape=jax.ShapeDtypeStruct((t, d), out_dtype),
        compiler_params=_params(("parallel",)),
    )(x, w)


def _res_rms_fwd(x, f, w, *, name):
    t, d = x.shape
    tm = _pick(t, (512, 256, 128))

    def body(x_ref, f_ref, w_ref, o_ref):
        fv = f_ref[...]
        rstd = lax.rsqrt(jnp.mean(fv * fv, axis=-1, keepdims=True) + EPS)
        o_ref[...] = x_ref[...] + fv * rstd * w_ref[...]

    row = pl.BlockSpec((tm, d), lambda i: (i, 0))
    return pl.pallas_call(
        body,
        name=name,
        grid=(t // tm,),
        in_specs=[row, row, pl.BlockSpec((1, d), lambda i: (0, 0))],
        out_specs=row,
        out_shape=jax.ShapeDtypeStruct((t, d), F32),
        compiler_params=_params(("parallel",)),
    )(x, f, w)


def _rms_bwd(x, w, dy, resid, *, name):
    t, d = x.shape
    tm = _pick(t, (512, 256, 128))
    has_res = resid is not None

    def body(*refs):
        if has_res:
            x_ref, w_ref, dy_ref, r_ref, dx_ref, dw_ref = refs
        else:
            x_ref, w_ref, dy_ref, dx_ref, dw_ref = refs
        xv = x_ref[...]
        dyv = dy_ref[...].astype(F32)
        rstd = lax.rsqrt(jnp.mean(xv * xv, axis=-1, keepdims=True) + EPS)
        xn = xv * rstd
        g = dyv * w_ref[...]
        dx = rstd * (g - xn * jnp.mean(g * xn, axis=-1, keepdims=True))
        if has_res:
            dx = dx + r_ref[...]
        dx_ref[...] = dx
        part = jnp.sum(dyv * xn, axis=0, keepdims=True)

        @pl.when(pl.program_id(0) == 0)
        def _():
            dw_ref[...] = part

        @pl.when(pl.program_id(0) > 0)
        def _():
            dw_ref[...] += part

    row = pl.BlockSpec((tm, d), lambda i: (i, 0))
    vec = pl.BlockSpec((1, d), lambda i: (0, 0))
    ins = [x, w, dy] + ([resid] if has_res else [])
    return pl.pallas_call(
        body,
        name=name,
        grid=(t // tm,),
        in_specs=[row, vec, row] + ([row] if has_res else []),
        out_specs=[row, vec],
        out_shape=[jax.ShapeDtypeStruct((t, d), F32), jax.ShapeDtypeStruct((1, d), F32)],
        compiler_params=_params(("arbitrary",)),
    )(*ins)


def _loss_head(y, target, *, name="loss_head"):
    t, d = y.shape
    tm = _pick(t, (512, 256, 128))

    def body(y_ref, t_ref, dy_ref, l_ref):
        err = y_ref[...] - t_ref[...]
        dy_ref[...] = err * (1.0 / d)
        part = jnp.sum(jnp.sum(err * err, axis=-1, keepdims=True), axis=0, keepdims=True) * (0.5 / d)
        part = jnp.broadcast_to(part, (1, LANE))

        @pl.when(pl.program_id(0) == 0)
        def _():
            l_ref[...] = part

        @pl.when(pl.program_id(0) > 0)
        def _():
            l_ref[...] += part

    row = pl.BlockSpec((tm, d), lambda i: (i, 0))
    dy, l = pl.pallas_call(
        body,
        name=name,
        grid=(t // tm,),
        in_specs=[row, row],
        out_specs=[row, pl.BlockSpec((1, LANE), lambda i: (0, 0))],
        out_shape=[jax.ShapeDtypeStruct((t, d), F32), jax.ShapeDtypeStruct((1, LANE), F32)],
        compiler_params=_params(("arbitrary",)),
    )(y, target)
    return l[0, 0], dy


def _conv_taps(h, w_ref, k_taps):
    out = h * w_ref[k_taps - 1:k_taps, :]
    for k in range(k_taps - 1):
        out = out + _shift_down(h, k_taps - 1 - k) * w_ref[k:k + 1, :]
    return out


def _conv_taps_bwd(h, dhc, w_ref, k_taps):
    dh = dhc * w_ref[k_taps - 1:k_taps, :]
    dws = []
    for k in range(k_taps - 1):
        up = _shift_up(dhc, k_taps - 1 - k)
        dh = dh + up * w_ref[k:k + 1, :]
        dws.append(jnp.sum(up * h, axis=0, keepdims=True))
    dws.append(jnp.sum(dhc * h, axis=0, keepdims=True))
    return dh, jnp.concatenate(dws, axis=0)


FFN_TC = 256


def _interleave(w, tc=FFN_TC):
    f = w.shape[-1] // 2
    lead = w.shape[:-1]
    return jnp.swapaxes(w.reshape(lead + (2, f // tc, tc)), -3, -2).reshape(lead + (2 * f,))


def _deinterleave(w, tc=FFN_TC):
    f = w.shape[-1] // 2
    lead = w.shape[:-1]
    return jnp.swapaxes(w.reshape(lead + (f // tc, 2, tc)), -3, -2).reshape(lead + (2 * f,))


def _ffn_act_fwd(h, conv_w, conv_b, n_seq, *, name):
    t, f2 = h.shape
    seq = t // n_seq
    tc = FFN_TC
    nj = f2 // (2 * tc)
    k_taps = conv_w.shape[0]

    def body(h_ref, w_ref, b_ref, o_ref, hc_ref):
        hc = _conv_taps(h_ref[...].astype(F32), w_ref, k_taps) + b_ref[...]
        hc_ref[...] = hc.astype(BF16)
        o_ref[...] = (_silu(hc[:, :tc]) * hc[:, tc:]).astype(BF16)

    return pl.pallas_call(
        body,
        name=name,
        grid=(n_seq, nj),
        in_specs=[
            pl.BlockSpec((seq, 2 * tc), lambda b, j: (b, j)),
            pl.BlockSpec((k_taps, 2 * tc), lambda b, j: (0, j)),
            pl.BlockSpec((1, 2 * tc), lambda b, j: (0, j)),
        ],
        out_specs=[pl.BlockSpec((seq, tc), lambda b, j: (b, j)), pl.BlockSpec((seq, 2 * tc), lambda b, j: (b, j))],
        out_shape=[jax.ShapeDtypeStruct((t, f2 // 2), BF16), jax.ShapeDtypeStruct((t, f2), BF16)],
        compiler_params=_params(("parallel", "parallel")),
    )(h, conv_w, conv_b)


def _ffn_act_bwd(h, hc, conv_w, da, n_seq, *, name):
    t, f2 = h.shape
    seq = t // n_seq
    tc = FFN_TC
    nj = f2 // (2 * tc)
    k_taps = conv_w.shape[0]

    def body(h_ref, hc_ref, w_ref, da_ref, dh_ref, dw_ref, db_ref):
        hcv = hc_ref[...].astype(F32)
        gate, val = hcv[:, :tc], hcv[:, tc:]
        dav = da_ref[...].astype(F32)
        dhc = jnp.concatenate([dav * val * _dsilu(gate), dav * _silu(gate)], axis=1)
        dh, dw = _conv_taps_bwd(h_ref[...].astype(F32), dhc, w_ref, k_taps)
        dh_ref[...] = dh.astype(BF16)
        db = jnp.sum(dhc, axis=0, keepdims=True)

        @pl.when(pl.program_id(1) == 0)
        def _():
            dw_ref[...] = dw
            db_ref[...] = db

        @pl.when(pl.program_id(1) > 0)
        def _():
            dw_ref[...] += dw
            db_ref[...] += db

    wide = pl.BlockSpec((seq, 2 * tc), lambda j, b: (b, j))
    return pl.pallas_call(
        body,
        name=name,
        grid=(nj, n_seq),
        in_specs=[wide, wide, pl.BlockSpec((k_taps, 2 * tc), lambda j, b: (0, j)),
                  pl.BlockSpec((seq, tc), lambda j, b: (b, j))],
        out_specs=[
            wide,
            pl.BlockSpec((k_taps, 2 * tc), lambda j, b: (0, j)),
            pl.BlockSpec((1, 2 * tc), lambda j, b: (0, j)),
        ],
        out_shape=[
            jax.ShapeDtypeStruct((t, f2), BF16),
            jax.ShapeDtypeStruct((k_taps, f2), F32),
            jax.ShapeDtypeStruct((1, f2), F32),
        ],
        compiler_params=_params(("parallel", "arbitrary")),
    )(h, hc, conv_w, da)


def _window_mixed(u, window):
    s = u
    step = 1
    while step < window:
        s = s + _shift_down(s, step)
        step *= 2
    rows = lax.broadcasted_iota(jnp.int32, u.shape, 0)
    inv_cnt = 1.0 / jnp.minimum(rows + 1, window).astype(F32)
    return s * inv_cnt - u, inv_cnt


def _window_mixed_bwd(dmixed, inv_cnt, window):
    r = dmixed * inv_cnt
    s = r
    step = 1
    while step < window:
        s = s + _shift_up(s, step)
        step *= 2
    return s - dmixed


def _pool_fwd(u, w, scale, n_seq, *, name):
    t, d = u.shape
    seq = t // n_seq
    n_g, dg, _ = w.shape

    def body(u_ref, w_ref, s_ref, o_ref):
        for k, window in enumerate(POOL_WINDOWS):
            @pl.when(pl.program_id(1) == k)
            def _(window=window):
                mixed, _ = _window_mixed(u_ref[...], window)
                pre = jnp.dot(mixed.astype(BF16), w_ref[0].astype(BF16), preferred_element_type=F32)
                o_ref[...] = pre * s_ref[...]

    return pl.pallas_call(
        body,
        name=name,
        grid=(n_seq, n_g),
        in_specs=[
            pl.BlockSpec((seq, dg), lambda b, g: (b, g)),
            pl.BlockSpec((1, dg, dg), lambda b, g: (g, 0, 0)),
            pl.BlockSpec((1, dg), lambda b, g: (0, g)),
        ],
        out_specs=pl.BlockSpec((seq, dg), lambda b, g: (b, g)),
        out_shape=jax.ShapeDtypeStruct((t, d), F32),
        compiler_params=_params(("parallel", "parallel")),
    )(u, w, scale)


def _pool_bwd(u, w, scale, dout, n_seq, *, name):
    t, d = u.shape
    seq = t // n_seq
    n_g, dg, _ = w.shape

    def body(u_ref, w_ref, s_ref, do_ref, du_ref, dw_ref, ds_ref):
        group = pl.program_id(0)
        first = pl.program_id(1) == 0
        for k, window in enumerate(POOL_WINDOWS):
            @pl.when(group == k)
            def _(window=window):
                mixed, inv_cnt = _window_mixed(u_ref[...], window)
                mixed_b = mixed.astype(BF16)
                w_b = w_ref[0].astype(BF16)
                dov = do_ref[...]
                pre = jnp.dot(mixed_b, w_b, preferred_element_type=F32)
                dsc = jnp.sum(dov * pre, axis=0, keepdims=True)
                dpre = (dov * s_ref[...]).astype(BF16)
                dw = lax.dot_general(mixed_b, dpre, (((0,), (0,)), ((), ())), preferred_element_type=F32)
                dmixed = lax.dot_general(dpre, w_b, (((1,), (1,)), ((), ())), preferred_element_type=F32)
                du_ref[...] = _window_mixed_bwd(dmixed, inv_cnt, window)

                @pl.when(first)
                def _():
                    dw_ref[0] = dw
                    ds_ref[...] = dsc

                @pl.when(jnp.logical_not(first))
                def _():
                    dw_ref[0] += dw
                    ds_ref[...] += dsc

    return pl.pallas_call(
        body,
        name=name,
        grid=(n_g, n_seq),
        in_specs=[
            pl.BlockSpec((seq, dg), lambda g, b: (b, g)),
            pl.BlockSpec((1, dg, dg), lambda g, b: (g, 0, 0)),
            pl.BlockSpec((1, dg), lambda g, b: (0, g)),
            pl.BlockSpec((seq, dg), lambda g, b: (b, g)),
        ],
        out_specs=[
            pl.BlockSpec((seq, dg), lambda g, b: (b, g)),
            pl.BlockSpec((1, dg, dg), lambda g, b: (g, 0, 0)),
            pl.BlockSpec((1, dg), lambda g, b: (0, g)),
        ],
        out_shape=[
            jax.ShapeDtypeStruct((t, d), F32),
            jax.ShapeDtypeStruct((n_g, dg, dg), F32),
            jax.ShapeDtypeStruct((1, d), F32),
        ],
        compiler_params=_params(("parallel", "arbitrary")),
    )(u, w, scale, dout)


def _adamw(w, g, m, v, *, name):
    shape = w.shape
    c = shape[-1]
    r = w.size // c
    tm = _pick(r, (512, 256, 128, 64, 32, 16, 8))

    def body(w_ref, g_ref, m_ref, v_ref, d_ref, nm_ref, nv_ref):
        gv = g_ref[...]
        nm = ADAM_B1 * m_ref[...] + (1.0 - ADAM_B1) * gv
        nv = ADAM_B2 * v_ref[...] + (1.0 - ADAM_B2) * (gv * gv)
        m_hat = nm / (1.0 - ADAM_B1 ** ADAM_STEP)
        v_hat = nv / (1.0 - ADAM_B2 ** ADAM_STEP)
        d_ref[...] = -ADAM_LR * (m_hat / (jnp.sqrt(v_hat) + ADAM_EPS) + ADAM_WD * w_ref[...])
        nm_ref[...] = nm
        nv_ref[...] = nv

    blk = pl.BlockSpec((tm, c), lambda i: (i, 0))
    out = jax.ShapeDtypeStruct((r, c), F32)
    res = pl.pallas_call(
        body,
        name=name,
        grid=(r // tm,),
        in_specs=[blk] * 4,
        out_specs=[blk] * 3,
        out_shape=[out] * 3,
        compiler_params=_params(("parallel",)),
    )(w.reshape(r, c), g.reshape(r, c), m.reshape(r, c), v.reshape(r, c))
    return tuple(a.reshape(shape) for a in res)


CONV_TC = 256


def _ssd_conv_fwd(proj, col0, n_cols, conv_w, conv_b, n_seq, *, name):
    t = proj.shape[0]
    seq = t // n_seq
    tc = CONV_TC
    off = col0 // tc
    k_taps = conv_w.shape[0]

    def body(h_ref, w_ref, b_ref, o_ref):
        o_ref[...] = _silu(_conv_taps(h_ref[...], w_ref, k_taps) + b_ref[...])

    return pl.pallas_call(
        body,
        name=name,
        grid=(n_seq, n_cols // tc),
        in_specs=[
            pl.BlockSpec((seq, tc), lambda b, j: (b, j + off)),
            pl.BlockSpec((k_taps, tc), lambda b, j: (0, j)),
            pl.BlockSpec((1, tc), lambda b, j: (0, j)),
        ],
        out_specs=pl.BlockSpec((seq, tc), lambda b, j: (b, j)),
        out_shape=jax.ShapeDtypeStruct((t, n_cols), F32),
        compiler_params=_params(("parallel", "parallel")),
    )(proj, conv_w, conv_b)


def _ssd_conv_bwd(proj, col0, conv_w, conv_b, dparts, dproj, n_seq, *, name):
    t = proj.shape[0]
    seq = t // n_seq
    tc = CONV_TC
    off = col0 // tc
    k_taps = conv_w.shape[0]
    widths = [d.shape[1] // tc for d in dparts]
    starts = [sum(widths[:i]) for i in range(len(widths))]
    n_blocks = sum(widths)
    n_parts = len(dparts)

    def body(h_ref, w_ref, b_ref, *rest):
        part_refs = rest[:n_parts]
        dh_ref, dw_ref, db_ref = rest[n_parts + 1:]
        j = pl.program_id(0)
        da = part_refs[-1][...]
        for i in reversed(range(n_parts - 1)):
            da = jnp.where(j < starts[i + 1], part_refs[i][...], da)
        hv = h_ref[...]
        dhc = da * _dsilu(_conv_taps(hv, w_ref, k_taps) + b_ref[...])
        dh, dw = _conv_taps_bwd(hv, dhc, w_ref, k_taps)
        dh_ref[...] = dh.astype(BF16)
        db = jnp.sum(dhc, axis=0, keepdims=True)

        @pl.when(pl.program_id(1) == 0)
        def _():
            dw_ref[...] = dw
            db_ref[...] = db

        @pl.when(pl.program_id(1) > 0)
        def _():
            dw_ref[...] += dw
            db_ref[...] += db

    def part_spec(start, width):
        return pl.BlockSpec((seq, tc), lambda j, b: (b, jnp.clip(j - start, 0, width - 1)))

    n_cols = n_blocks * tc
    return pl.pallas_call(
        body,
        name=name,
        grid=(n_blocks, n_seq),
        in_specs=[
            pl.BlockSpec((seq, tc), lambda j, b: (b, j + off)),
            pl.BlockSpec((k_taps, tc), lambda j, b: (0, j)),
            pl.BlockSpec((1, tc), lambda j, b: (0, j)),
        ] + [part_spec(st, wd) for st, wd in zip(starts, widths)] + [ANY],
        out_specs=[
            pl.BlockSpec((seq, tc), lambda j, b: (b, j + off)),
            pl.BlockSpec((k_taps, tc), lambda j, b: (0, j)),
            pl.BlockSpec((1, tc), lambda j, b: (0, j)),
        ],
        out_shape=[
            jax.ShapeDtypeStruct(dproj.shape, BF16),
            jax.ShapeDtypeStruct((k_taps, n_cols), F32),
            jax.ShapeDtypeStruct((1, n_cols), F32),
        ],
        input_output_aliases={3 + n_parts: 0},
        compiler_params=_params(("parallel", "arbitrary")),
    )(proj, conv_w, conv_b, *dparts, dproj)


def _fill_cols(buf, src, col0, *, name):
    t, c = src.shape
    tm = _pick(t, (1024, 512, 256, 128))

    def body(s_ref, b_ref, o_ref):
        o_ref[...] = s_ref[...].astype(o_ref.dtype)

    return pl.pallas_call(
        body,
        name=name,
        grid=(t // tm,),
        in_specs=[pl.BlockSpec((tm, c), lambda i: (i, 0)), ANY],
        out_specs=pl.BlockSpec((tm, c), lambda i: (i, col0 // c)),
        out_shape=jax.ShapeDtypeStruct(buf.shape, buf.dtype),
        input_output_aliases={1: 0},
        compiler_params=_params(("parallel",)),
    )(src, buf)


def _softplus(x):
    return jnp.maximum(x, 0.0) + jnp.log(1.0 + jnp.exp(-jnp.abs(x)))


def _chunk_decay(dtraw, bias, alog):
    q = dtraw.shape[0]
    dt = _softplus(dtraw + bias)
    a = -jnp.exp(alog)
    rows = lax.broadcasted_iota(jnp.int32, (q, q), 0)
    cols = lax.broadcasted_iota(jnp.int32, (q, q), 1)
    lower = rows >= cols
    acum = jnp.dot(lower.astype(F32), dt * a, precision=lax.Precision.HIGHEST, preferred_element_type=F32)
    return dt, a, acum, acum.T, lower


def _dot_exact(v, sel):
    hi = v.astype(BF16)
    r1 = v - hi.astype(F32)
    mid = r1.astype(BF16)
    lo = (r1 - mid.astype(F32)).astype(BF16)
    return (jnp.dot(hi, sel, preferred_element_type=F32) + jnp.dot(mid, sel, preferred_element_type=F32)
            + jnp.dot(lo, sel, preferred_element_type=F32))


def _head_selectors(gw, p):
    sum_heads = (lax.broadcasted_iota(jnp.int32, (gw, LANE), 0) // p == lax.broadcasted_iota(jnp.int32, (gw, LANE), 1))
    spread = (lax.broadcasted_iota(jnp.int32, (LANE, gw), 0) == lax.broadcasted_iota(jnp.int32, (LANE, gw), 1) // p)
    return sum_heads.astype(BF16), spread.astype(BF16)


def _row_spread(v, spread):
    return _dot_exact(jnp.broadcast_to(v, (8, v.shape[1])), spread)[0:1, :]


def _head_pad(v, r_heads):
    lead = v.shape[:-1]
    vg = v.reshape(lead + (N_SSD_GROUPS, r_heads))
    vg = jnp.pad(vg, [(0, 0)] * len(lead) + [(0, 0), (0, LANE - r_heads)])
    out = vg.reshape(lead + (N_SSD_GROUPS * LANE,))
    return out[None] if out.ndim == 1 else out


def _head_unpad(v, r_heads):
    lead = v.shape[:-1]
    out = v.reshape(lead + (N_SSD_GROUPS, LANE))[..., :r_heads].reshape(lead + (N_SSD_GROUPS * r_heads,))
    return out[0] if (len(lead) == 1 and lead[0] == 1) else out


def _ssd_w_in_layout(w_in, d_inner, d_xbc, r_heads):
    main = w_in[:, :d_inner + d_xbc]
    return jnp.concatenate([main, _head_pad(w_in[:, d_inner + d_xbc:], r_heads)], axis=1)


def _ssd_w_in_unlayout(w, d_inner, d_xbc, r_heads):
    main = w[:, :d_inner + d_xbc]
    return jnp.concatenate([main, _head_unpad(w[:, d_inner + d_xbc:], r_heads)], axis=1)


def _ssd_dims(proj, xbc):
    d_xbc = xbc.shape[1]
    d_inner = d_xbc - 2 * N_SSD_GROUPS * D_STATE
    gw = d_inner // N_SSD_GROUPS
    return d_inner, d_xbc, gw, gw // HEAD_DIM


def _ssd_fwd(proj, xbc, bias_p, alog_p, dskip_p, norm_w, n_seq, *, name):
    t = proj.shape[0]
    d_inner, d_xbc, gw, r_heads = _ssd_dims(proj, xbc)
    q, n, n_g, p = CHUNK, D_STATE, N_SSD_GROUPS, HEAD_DIM
    seq = t // n_seq
    nc = seq // q
    dt_blk0 = (d_inner + d_xbc) // LANE

    def body(x_ref, b_ref, c_ref, z_ref, dtr_ref, bias_ref, alog_ref, dsk_ref, nw_ref, yn_ref, y_ref, hs_ref, h_scr):
        @pl.when(pl.program_id(2) == 0)
        def _():
            h_scr[...] = jnp.zeros_like(h_scr)

        dt, a, acum, acum_t, lower = _chunk_decay(dtr_ref[...], bias_ref[...], alog_ref[...])
        x = x_ref[...]
        bb = b_ref[...].astype(BF16)
        cb = c_ref[...].astype(BF16)
        g_mat = lax.dot_general(cb, bb, (((1,), (1,)), ((), ())), preferred_element_type=F32)
        h_prev = h_scr[...]
        hs_ref[...] = h_prev
        c_h = jnp.dot(cb, h_prev.astype(BF16), preferred_element_type=F32)
        _, spread = _head_selectors(gw, p)
        acum_s = _dot_exact(acum, spread)
        a_last_s = acum_s[q - 1:q, :]
        xdt = x * _dot_exact(dt, spread)
        xdt_b = xdt.astype(BF16)
        ys = []
        for h in range(r_heads):
            decay = jnp.exp(jnp.where(lower, acum[:, h:h + 1] - acum_t[h:h + 1, :], -jnp.inf))
            ys.append(jnp.dot((g_mat * decay).astype(BF16), xdt_b[:, h * p:(h + 1) * p], preferred_element_type=F32))
        y = jnp.concatenate(ys, axis=1) + jnp.exp(acum_s) * c_h + _row_spread(dsk_ref[...], spread) * x
        xd = xdt * jnp.exp(a_last_s - acum_s)
        states = lax.dot_general(bb, xd.astype(BF16), (((0,), (0,)), ((), ())), preferred_element_type=F32)
        h_scr[...] = h_prev * jnp.exp(a_last_s) + states
        y_ref[...] = y
        gated = y * _silu(z_ref[...])
        rstd = lax.rsqrt(jnp.mean(gated * gated, axis=-1, keepdims=True) + EPS)
        yn_ref[...] = (gated * rstd * nw_ref[...]).astype(BF16)

    row = lambda b, g, c: b * nc + c
    vec = pl.BlockSpec((1, LANE), lambda b, g, c: (0, g))
    return pl.pallas_call(
        body,
        name=name,
        grid=(n_seq, n_g, nc),
        in_specs=[
            pl.BlockSpec((q, gw), lambda b, g, c: (row(b, g, c), g)),
            pl.BlockSpec((q, n), lambda b, g, c: (row(b, g, c), d_inner // n + g)),
            pl.BlockSpec((q, n), lambda b, g, c: (row(b, g, c), d_inner // n + n_g + g)),
            pl.BlockSpec((q, gw), lambda b, g, c: (row(b, g, c), g)),
            pl.BlockSpec((q, LANE), lambda b, g, c: (row(b, g, c), dt_blk0 + g)),
            vec, vec, vec,
            pl.BlockSpec((1, gw), lambda b, g, c: (0, g)),
        ],
        out_specs=[
            pl.BlockSpec((q, gw), lambda b, g, c: (row(b, g, c), g)),
            pl.BlockSpec((q, gw), lambda b, g, c: (row(b, g, c), g)),
            pl.BlockSpec((n, gw), lambda b, g, c: (row(b, g, c), g)),
        ],
        out_shape=[
            jax.ShapeDtypeStruct((t, d_inner), BF16),
            jax.ShapeDtypeStruct((t, d_inner), F32),
            jax.ShapeDtypeStruct((n_seq * nc * n, d_inner), F32),
        ],
        scratch_shapes=[pltpu.VMEM((n, gw), F32)],
        compiler_params=_params(("parallel", "parallel", "arbitrary")),
    )(xbc, xbc, xbc, proj, proj, bias_p, alog_p, dskip_p, norm_w)


def _ssd_bwd(proj, xbc, hs, y, dyn, bias_p, alog_p, dskip_p, norm_w, n_seq, *, name):
    t = proj.shape[0]
    d_inner, d_xbc, gw, r_heads = _ssd_dims(proj, xbc)
    q, n, n_g, p = CHUNK, D_STATE, N_SSD_GROUPS, HEAD_DIM
    seq = t // n_seq
    nc = seq // q
    dt_blk0 = (d_inner + d_xbc) // LANE

    def body(x_ref, b_ref, c_ref, z_ref, dtr_ref, bias_ref, alog_ref, dsk_ref, nw_ref, hs_ref, y_ref, dyn_ref,
             dx_ref, db_ref, dc_ref, dz_ref, ddtr_ref, dnw_ref, dbias_ref, dalog_ref, ddsk_ref, dh_scr):
        first = jnp.logical_and(pl.program_id(1) == 0, pl.program_id(2) == 0)

        @pl.when(pl.program_id(2) == 0)
        def _():
            dh_scr[...] = jnp.zeros_like(dh_scr)

        dtraw = dtr_ref[...]
        dt, a, acum, acum_t, lower = _chunk_decay(dtraw, bias_ref[...], alog_ref[...])
        x = x_ref[...]
        bb = b_ref[...].astype(BF16)
        cb = c_ref[...].astype(BF16)
        g_mat = lax.dot_general(cb, bb, (((1,), (1,)), ((), ())), preferred_element_type=F32)

        yv = y_ref[...]
        z = z_ref[...]
        sz = _silu(z)
        gated = yv * sz
        rstd = lax.rsqrt(jnp.mean(gated * gated, axis=-1, keepdims=True) + EPS)
        gn = gated * rstd
        dynv = dyn_ref[...]
        gwt = dynv * nw_ref[...]
        dgated = rstd * (gwt - gn * jnp.mean(gwt * gn, axis=-1, keepdims=True))
        dnw = jnp.sum(dynv * gn, axis=0, keepdims=True)
        dy = dgated * sz
        dz_ref[...] = (dgated * yv * _dsilu(z)).astype(BF16)

        h_prev = hs_ref[...]
        h_prev_b = h_prev.astype(BF16)
        ds = dh_scr[...]
        ds_b = ds.astype(BF16)
        sum_heads, spread = _head_selectors(gw, p)
        acum_s = _dot_exact(acum, spread)
        a_last_s = acum_s[q - 1:q, :]
        dt_s = _dot_exact(dt, spread)
        dsk_s = _row_spread(dsk_ref[...], spread)
        dte_s = jnp.exp(a_last_s - acum_s)
        cd_s = jnp.exp(a_last_s)
        xdt = x * dt_s
        xdt_b = xdt.astype(BF16)
        dy_b = dy.astype(BF16)
        gt_mat = lax.dot_general(bb, cb, (((1,), (1,)), ((), ())), preferred_element_type=F32)
        upper = lax.broadcasted_iota(jnp.int32, (q, q), 0) <= lax.broadcasted_iota(jnp.int32, (q, q), 1)
        dg = jnp.zeros((q, q), F32)
        dxdts, w_diffs = [], []
        for h in range(r_heads):
            hsl = slice(h * p, (h + 1) * p)
            diff = acum[:, h:h + 1] - acum_t[h:h + 1, :]
            decay = jnp.exp(jnp.where(lower, diff, -jnp.inf))
            decay_t = jnp.exp(jnp.where(upper, -diff, -jnp.inf))
            mt_mat = gt_mat * decay_t
            dm = lax.dot_general(dy_b[:, hsl], xdt_b[:, hsl], (((1,), (1,)), ((), ())), preferred_element_type=F32)
            dm_t = lax.dot_general(xdt_b[:, hsl], dy_b[:, hsl], (((1,), (1,)), ((), ())), preferred_element_type=F32)
            dg = dg + dm * decay
            dxdts.append(jnp.dot(mt_mat.astype(BF16), dy_b[:, hsl], preferred_element_type=F32))
            w_diffs.append(dm * (g_mat * decay) - dm_t * mt_mat)
        sel_q = (lax.broadcasted_iota(jnp.int32, (r_heads * q, LANE), 0) // q
                 == lax.broadcasted_iota(jnp.int32, (r_heads * q, LANE), 1)).astype(BF16)
        dacum_diag = _dot_exact(jnp.concatenate(w_diffs, axis=1), sel_q)
        c_h = jnp.dot(cb, h_prev_b, preferred_element_type=F32)
        dxd = jnp.dot(bb, ds_b, preferred_element_type=F32)
        dxdt = jnp.concatenate(dxdts, axis=1) + dxd * dte_s
        dye = dy * jnp.exp(acum_s)
        dye_b = dye.astype(BF16)
        xd = xdt * dte_s
        xd_b = xd.astype(BF16)
        dg_b = dg.astype(BF16)
        dx_ref[...] = dxdt * dt_s + dsk_s * dy
        dc_ref[...] = (jnp.dot(dg_b, bb, preferred_element_type=F32)
                       + lax.dot_general(dye_b, h_prev_b, (((1,), (1,)), ((), ())), preferred_element_type=F32))
        db_ref[...] = (lax.dot_general(dg_b, cb, (((0,), (0,)), ((), ())), preferred_element_type=F32)
                       + lax.dot_general(xd_b, ds_b, (((1,), (1,)), ((), ())), preferred_element_type=F32))
        dh_scr[...] = ds * cd_s + lax.dot_general(cb, dye_b, (((0,), (0,)), ((), ())), preferred_element_type=F32)
        ddt_cols = _dot_exact(x * dxdt, sum_heads)
        dacum_y = _dot_exact(dye * c_h - dxd * xd, sum_heads)
        col_sums = jnp.concatenate([
            jnp.sum(dxd * xd, axis=0, keepdims=True) + jnp.sum(ds * h_prev, axis=0, keepdims=True) * cd_s,
            jnp.sum(dy * x, axis=0, keepdims=True),
            jnp.zeros((6, gw), F32)], axis=0)
        col_sums = _dot_exact(col_sums, sum_heads)
        ddsk = col_sums[1:2, :]
        rows_q = lax.broadcasted_iota(jnp.int32, (q, LANE), 0)
        dacum = dacum_diag + dacum_y + jnp.where(rows_q == q - 1, col_sums[0:1, :], 0.0)
        dadt = jnp.dot(upper.astype(F32), dacum, precision=lax.Precision.HIGHEST, preferred_element_type=F32)
        ddt = dadt * a + ddt_cols
        ddtr = ddt * _sigmoid(dtraw + bias_ref[...])
        ddtr_ref[...] = ddtr
        dbias = jnp.sum(ddtr, axis=0, keepdims=True)
        dalog = jnp.sum(dadt * dt, axis=0, keepdims=True) * a

        @pl.when(first)
        def _():
            dnw_ref[...] = dnw
            dbias_ref[...] = dbias
            dalog_ref[...] = dalog
            ddsk_ref[...] = ddsk

        @pl.when(jnp.logical_not(first))
        def _():
            dnw_ref[...] += dnw
            dbias_ref[...] += dbias
            dalog_ref[...] += dalog
            ddsk_ref[...] += ddsk

    row = lambda g, b, c: b * nc + (nc - 1 - c)
    vec = pl.BlockSpec((1, LANE), lambda g, b, c: (0, g))
    wide = pl.BlockSpec((q, gw), lambda g, b, c: (row(g, b, c), g))
    narrow = pl.BlockSpec((q, n), lambda g, b, c: (row(g, b, c), g))
    return pl.pallas_call(
        body,
        name=name,
        grid=(n_g, n_seq, nc),
        in_specs=[
            wide,
            pl.BlockSpec((q, n), lambda g, b, c: (row(g, b, c), d_inner // n + g)),
            pl.BlockSpec((q, n), lambda g, b, c: (row(g, b, c), d_inner // n + n_g + g)),
            wide,
            pl.BlockSpec((q, LANE), lambda g, b, c: (row(g, b, c), dt_blk0 + g)),
            vec, vec, vec,
            pl.BlockSpec((1, gw), lambda g, b, c: (0, g)),
            pl.BlockSpec((n, gw), lambda g, b, c: (row(g, b, c), g)),
            wide, wide,
        ],
        out_specs=[
            wide, narrow, narrow, wide, narrow,
            pl.BlockSpec((1, gw), lambda g, b, c: (0, g)),
            vec, vec, vec,
        ],
        out_shape=[
            jax.ShapeDtypeStruct((t, d_inner), F32),
            jax.ShapeDtypeStruct((t, n_g * n), F32),
            jax.ShapeDtypeStruct((t, n_g * n), F32),
            jax.ShapeDtypeStruct(proj.shape, BF16),
            jax.ShapeDtypeStruct((t, n_g * LANE), F32),
            jax.ShapeDtypeStruct((1, d_inner), F32),
            jax.ShapeDtypeStruct((1, n_g * LANE), F32),
            jax.ShapeDtypeStruct((1, n_g * LANE), F32),
            jax.ShapeDtypeStruct((1, n_g * LANE), F32),
        ],
        scratch_shapes=[pltpu.VMEM((n, gw), F32)],
        compiler_params=_params(("parallel", "arbitrary", "arbitrary")),
    )(xbc, xbc, xbc, proj, proj, bias_p, alog_p, dskip_p, norm_w, hs, y, dyn)


MESH_IDS = pl.DeviceIdType.MESH


def _my_index():
    return 4 * lax.axis_index("x") + 2 * lax.axis_index("y") + lax.axis_index("c")


def _all_gather(shard, *, name):
    def body(x_ref, out_ref, send_sems, recv_sems, local_sem):
        x, y, c = lax.axis_index("x"), lax.axis_index("y"), lax.axis_index("c")
        me, sibling = (x, y, c), (x, y, 1 - c)
        chips = [(1 - x, y), (x, 1 - y), (1 - x, 1 - y)]

        def blk(px, py, pc):
            return out_ref.at[4 * px + 2 * py + pc]

        def copy(k, block, to, src=None):
            return pltpu.make_async_remote_copy(
                src_ref=blk(*block) if src is None else src, dst_ref=blk(*block),
                send_sem=send_sems.at[k], recv_sem=recv_sems.at[k], device_id=to, device_id_type=MESH_IDS)

        mine = pltpu.make_async_copy(x_ref, blk(*me), local_sem)
        mine.start()
        first = [copy(0, me, sibling, src=x_ref)]
        first += [copy(1 + j, me, (*chip, c), src=x_ref) for j, chip in enumerate(chips)]
        for cp in first:
            cp.start()
        passed = [copy(4 + j, (*chip, c), sibling) for j, chip in enumerate(chips)]
        for j, chip in enumerate(chips):
            copy(1 + j, (*chip, c), me).wait_recv()
            passed[j].start()
        copy(0, sibling, me).wait_recv()
        for j, chip in enumerate(chips):
            copy(4 + j, (*chip, 1 - c), me).wait_recv()
        for cp in first + passed:
            cp.wait_send()
        mine.wait()

    return pl.pallas_call(
        body,
        name=name,
        in_specs=[ANY],
        out_specs=ANY,
        out_shape=jax.ShapeDtypeStruct((N_DEV,) + shard.shape, shard.dtype),
        scratch_shapes=[pltpu.SemaphoreType.DMA((7,)), pltpu.SemaphoreType.DMA((7,)), pltpu.SemaphoreType.DMA],
    )(shard)


HBM_SPEC = pl.BlockSpec(memory_space=pltpu.HBM)
SEM_SPEC = pl.BlockSpec(memory_space=pltpu.SEMAPHORE)
SPLIT_COPY_PARAMS = pltpu.CompilerParams(has_side_effects=pltpu.SideEffectType.DATAFLOW_SIDE_EFFECTING)


def _peer_list():
    x, y, c = lax.axis_index("x"), lax.axis_index("y"), lax.axis_index("c")
    peers = []
    for k in range(1, N_DEV):
        px = 1 - x if k & 4 else x
        py = 1 - y if k & 2 else y
        pc = 1 - c if k & 1 else c
        peers.append(((px, py, pc), 4 * px + 2 * py + pc))
    return 4 * x + 2 * y + c, peers


def _push_copies(src_refs, land_refs, send_sems, recv_sems, blockwise):
    me, peers = _peer_list()
    copies = []
    for a, (src_ref, land_ref) in enumerate(zip(src_refs, land_refs)):
        for k, (dev, idx) in enumerate(peers):
            sem = a * (N_DEV - 1) + k
            src = src_ref.at[idx] if blockwise else src_ref
            copies.append(tuple(
                pltpu.make_async_remote_copy(src_ref=src, dst_ref=land_ref.at[slot], send_sem=send_sems.at[sem],
                                             recv_sem=recv_sems.at[sem], device_id=dev, device_id_type=MESH_IDS)
                for slot in (me, idx)))
    return copies


def _push_start(srcs, blockwise, after, *, name):
    n = len(srcs)
    blocks = [s_.shape[1:] if blockwise else s_.shape for s_ in srcs]

    def body(*refs):
        src_refs, land_refs = refs[:n], refs[n:2 * n]
        send_sems, recv_sems = refs[2 * n + 1], refs[2 * n + 2]
        token = refs[-1]
        for send, _ in _push_copies(src_refs, land_refs, send_sems, recv_sems, blockwise):
            send.start()
        token[...] = jnp.zeros_like(token)

    n_sem = n * (N_DEV - 1)
    lands = [lax.empty((N_DEV,) + b, s_.dtype) for b, s_ in zip(blocks, srcs)]
    out = pl.pallas_call(
        body,
        name=name,
        in_specs=[HBM_SPEC] * (2 * n) + [ANY],
        out_specs=(SEM_SPEC, SEM_SPEC) + (HBM_SPEC,) * (2 * n) + (pl.BlockSpec(memory_space=pltpu.VMEM),),
        out_shape=(pltpu.SemaphoreType.DMA((n_sem,)), pltpu.SemaphoreType.DMA((n_sem,)))
        + tuple(pltpu.HBM(a.shape, a.dtype) for a in list(srcs) + lands)
        + (jax.ShapeDtypeStruct((8, LANE), F32),),
        input_output_aliases={i: 2 + i for i in range(2 * n)},
        compiler_params=SPLIT_COPY_PARAMS,
    )(*[pltpu.with_memory_space_constraint(a, pltpu.HBM) for a in list(srcs) + lands], after)
    return out[0], out[1], out[2:2 + n], out[2 + n:2 + 2 * n], out[-1]


def _push_wait(send_sems, recv_sems, srcs, lands, blockwise, after, *, name):
    n = len(srcs)

    def body(*refs):
        src_refs, land_refs = refs[:n], refs[n:2 * n]
        send_sems, recv_sems = refs[2 * n], refs[2 * n + 1]
        for send, recv in _push_copies(src_refs, land_refs, send_sems, recv_sems, blockwise):
            send.wait_send()
            recv.wait_recv()

    out = pl.pallas_call(
        body,
        name=name,
        in_specs=[HBM_SPEC] * (2 * n) + [SEM_SPEC, SEM_SPEC, ANY],
        out_specs=(HBM_SPEC,) * (2 * n),
        out_shape=tuple(pltpu.HBM(a.shape, a.dtype) for a in list(srcs) + list(lands)),
        input_output_aliases={i: i for i in range(2 * n)},
        compiler_params=SPLIT_COPY_PARAMS,
    )(*srcs, *lands, send_sems, recv_sems, after)
    return out[n:]


def _with_own_slot(landing, own):
    slot = lax.broadcasted_iota(jnp.int32, (N_DEV,) + (1,) * own.ndim, 0)
    return jnp.where(slot == _my_index(), own[None], landing)


def _sum_slots(parts, *, name):
    shape = parts.shape[1:]
    n, c = parts.shape[0], parts.shape[-1]
    r = parts.size // (n * c)
    tm = _pick(r, (256, 128, 64, 32, 16, 8))

    def body(p_ref, o_ref):
        acc = p_ref[0].astype(F32)
        for s in range(1, n):
            acc = acc + p_ref[s].astype(F32)
        o_ref[...] = acc

    return pl.pallas_call(
        body,
        name=name,
        grid=(r // tm,),
        in_specs=[pl.BlockSpec((n, tm, c), lambda i: (0, i, 0))],
        out_specs=pl.BlockSpec((tm, c), lambda i: (i, 0)),
        out_shape=jax.ShapeDtypeStruct((r, c), F32),
        compiler_params=_params(("parallel",)),
    )(parts.reshape(n, r, c)).reshape(shape)


def _row_count(shape):
    c = shape[-1]
    rows = 1
    for s in shape[:-1]:
        rows *= s
    return rows, c, c + (-c) % LANE


PACK_ROWS = 256


def _pack_rows(arrays):
    pieces = []
    for a in arrays:
        rows, c, cp = _row_count(a.shape)
        a2 = a.reshape(rows, c)
        if cp > c:
            a2 = jnp.pad(a2, ((0, 0), (0, cp - c)))
        a2 = a2.reshape(rows * cp // LANE, LANE)
        if a2.shape[0] % 8:
            a2 = jnp.pad(a2, ((0, 8 - a2.shape[0] % 8), (0, 0)))
        pieces.append(a2)
    total = sum(p.shape[0] for p in pieces)
    if total % PACK_ROWS:
        pieces.append(jnp.zeros((PACK_ROWS - total % PACK_ROWS, LANE), F32))
    return jnp.concatenate(pieces, axis=0)


def _unpack_rows(packed, shapes, lead=()):
    out, off = [], 0
    for shp in shapes:
        rows, c, cp = _row_count(shp)
        n_rows = rows * cp // LANE
        seg = packed[..., off:off + n_rows, :].reshape(lead + (rows, cp))
        out.append(seg[..., :c].reshape(lead + tuple(shp)))
        off += n_rows + (-n_rows) % 8
    return out


def _unshard(stacked, axis):
    moved = jnp.moveaxis(stacked, 0, axis)
    shp = moved.shape
    return moved.reshape(shp[:axis] + (shp[axis] * shp[axis + 1],) + shp[axis + 2:])


def _shard_major(full, axis):
    shp = full.shape
    split = full.reshape(shp[:axis] + (N_DEV, shp[axis] // N_DEV) + shp[axis + 1:])
    return jnp.moveaxis(split, axis, 0)


def _my_shard(full, axis):
    size = full.shape[axis] // N_DEV
    return lax.dynamic_slice_in_dim(full, _my_index() * size, size, axis)


def _local_step(x, target, w, fetch, emit, n_seq):
    def with_token(vec, token):
        return vec + jnp.tile(token[0:1, :], (1, vec.shape[1] // LANE))

    depth, d_model = w["norm_mix_pre"].shape
    d_inner = w["ssd_norm_w"].shape[1]
    d_xbc = w["ssd_conv_w"].shape[2]
    saved = []
    for i in range(depth):
        j = i // 2
        m, token = fetch(i, "mix", x)
        mix_pre_w = with_token(w["norm_mix_pre"][i:i + 1], token)
        s = {"x": x, "mix_pre_w": mix_pre_w}
        if i % 2 == 0:
            u = _rms_fwd(x, mix_pre_w, out_dtype=BF16, name=f"l{i}_mix_pre")
            proj = _mm(u, m["ssd_w_in"], name=f"l{i}_ssd_in")
            xbc = _ssd_conv_fwd(proj, d_inner, d_xbc, w["ssd_conv_w"][j], w["ssd_conv_b"][j:j + 1], n_seq,
                                name=f"l{i}_ssd_conv")
            yn, y, hs = _ssd_fwd(proj, xbc, w["ssd_dt_bias"][j:j + 1], w["ssd_a_log"][j:j + 1], w["ssd_d"][j:j + 1],
                                 w["ssd_norm_w"][j:j + 1], n_seq, name=f"l{i}_ssd_scan")
            mix = _mm(yn, m["ssd_w_out"], name=f"l{i}_ssd_out")
            s.update(u=u, proj=proj, xbc=xbc, yn=yn, y=y, hs=hs)
        else:
            u = _rms_fwd(x, mix_pre_w, out_dtype=F32, name=f"l{i}_mix_pre")
            mix = _pool_fwd(u, m["pool_w"], w["pool_scale"][j:j + 1], n_seq, name=f"l{i}_pool")
            s.update(u=u)
        x1 = _res_rms_fwd(x, mix, w["norm_mix_post"][i:i + 1], name=f"l{i}_mix_post")
        m_ffn, token = fetch(i, "ffn", x1)
        m = {**m, **m_ffn}
        ffn_pre_w = with_token(w["norm_ffn_pre"][i:i + 1], token)
        n = _rms_fwd(x1, ffn_pre_w, out_dtype=BF16, name=f"l{i}_ffn_pre")
        h = _mm(n, m["ffn_w_up"], out_dtype=BF16, name=f"l{i}_ffn_up")
        a, hc = _ffn_act_fwd(h, w["ffn_conv_w"][i], w["ffn_conv_b"][i:i + 1], n_seq, name=f"l{i}_ffn_act")
        f = _mm(a, m["ffn_w_down"], name=f"l{i}_ffn_down")
        x = _res_rms_fwd(x1, f, w["norm_ffn_post"][i:i + 1], name=f"l{i}_ffn_post")
        s.update(mix=mix, x1=x1, n=n, h=h, hc=hc, a=a, f=f, m=m, ffn_pre_w=ffn_pre_w)
        saved.append(s)

    loss, dx = _loss_head(x, target)
    grads = {k: [None] * len(w[k]) for k in SMALL}
    token = jnp.zeros((8, LANE), F32)
    for i in reversed(range(depth)):
        j = i // 2
        s = saved[i]
        m, gm = s["m"], {}
        df, grads["norm_ffn_post"][i] = _rms_bwd(s["f"], with_token(w["norm_ffn_post"][i:i + 1], token), dx, None,
                                                 name=f"l{i}_ffn_post_b")
        da = _mm(df, m["ffn_w_down"], tb=True, out_dtype=BF16, name=f"l{i}_ffn_down_bx")
        gm["ffn_w_down"] = _mm(s["a"], df, ta=True, name=f"l{i}_ffn_down_bw")
        dh, grads["ffn_conv_w"][i], grads["ffn_conv_b"][i] = _ffn_act_bwd(
            s["h"], s["hc"], w["ffn_conv_w"][i], da, n_seq, name=f"l{i}_ffn_act_b")
        dn = _mm(dh, m["ffn_w_up"], tb=True, name=f"l{i}_ffn_up_bx")
        gm["ffn_w_up"] = _mm(s["n"], dh, ta=True, name=f"l{i}_ffn_up_bw")
        dx1, grads["norm_ffn_pre"][i] = _rms_bwd(s["x1"], s["ffn_pre_w"], dn, dx, name=f"l{i}_ffn_pre_b")
        token = emit(i, "ffn", gm, dx1)
        gm = {}
        dmix, grads["norm_mix_post"][i] = _rms_bwd(s["mix"], with_token(w["norm_mix_post"][i:i + 1], token), dx1, None,
                                                   name=f"l{i}_mix_post_b")
        if i % 2 == 0:
            dyn = _mm(dmix, m["ssd_w_out"], tb=True, name=f"l{i}_ssd_out_bx")
            gm["ssd_w_out"] = _mm(s["yn"], dmix, ta=True, name=f"l{i}_ssd_out_bw")
            dxs, db, dc, dz, ddtr, dnw, dbias, dalog, ddsk = _ssd_bwd(
                s["proj"], s["xbc"], s["hs"], s["y"], dyn, w["ssd_dt_bias"][j:j + 1], w["ssd_a_log"][j:j + 1],
                w["ssd_d"][j:j + 1], w["ssd_norm_w"][j:j + 1], n_seq, name=f"l{i}_ssd_scan_b")
            grads["ssd_norm_w"][j], grads["ssd_dt_bias"][j], grads["ssd_a_log"][j], grads["ssd_d"][j] = (
                dnw, dbias, dalog, ddsk)
            dproj, grads["ssd_conv_w"][j], grads["ssd_conv_b"][j] = _ssd_conv_bwd(
                s["proj"], d_inner, w["ssd_conv_w"][j], w["ssd_conv_b"][j:j + 1], (dxs, db, dc), dz, n_seq,
                name=f"l{i}_ssd_conv_b")
            dproj = _fill_cols(dproj, ddtr, d_inner + d_xbc, name=f"l{i}_ssd_dt_b")
            du = _mm(dproj, m["ssd_w_in"], tb=True, name=f"l{i}_ssd_in_bx")
            gm["ssd_w_in"] = _mm(s["u"], dproj, ta=True, name=f"l{i}_ssd_in_bw")
        else:
            du, gm["pool_w"], grads["pool_scale"][j] = _pool_bwd(
                s["u"], m["pool_w"], w["pool_scale"][j:j + 1], dmix, n_seq, name=f"l{i}_pool_b")
        dx, grads["norm_mix_pre"][i] = _rms_bwd(s["x"], s["mix_pre_w"], du, dx1, name=f"l{i}_mix_pre_b")
        token = emit(i, "mix", gm, dx)
    return loss, dx, grads


BIG = (("ssd_w_in", 2), ("ssd_w_out", 1), ("pool_w", 2), ("ffn_w_up", 2), ("ffn_w_down", 1))
SMALL_SHARDED = (("ssd_conv_w", 2), ("ffn_conv_w", 2), ("pool_scale", 1))
SMALL = ("ssd_conv_w", "ssd_conv_b", "ssd_dt_bias", "ssd_a_log", "ssd_d", "ssd_norm_w", "pool_scale", "ffn_conv_w",
         "ffn_conv_b", "norm_mix_pre", "norm_mix_post", "norm_ffn_pre", "norm_ffn_post")
WEIGHTS = ("ssd_w_in", "ssd_conv_w", "ssd_conv_b", "ssd_dt_bias", "ssd_a_log", "ssd_d", "ssd_norm_w", "ssd_w_out",
           "pool_w", "pool_scale", "ffn_w_up", "ffn_conv_w", "ffn_conv_b", "ffn_w_down", "norm_mix_pre",
           "norm_mix_post", "norm_ffn_pre", "norm_ffn_post")


def _ssd_sizes(d_inner):
    return d_inner + 2 * N_SSD_GROUPS * D_STATE, d_inner // HEAD_DIM // N_SSD_GROUPS


def _small_compute_layout(full, d_inner):
    _, r_heads = _ssd_sizes(d_inner)
    w = {k: full[k] for k in SMALL}
    for k in ("ssd_dt_bias", "ssd_a_log", "ssd_d"):
        w[k] = _head_pad(full[k], r_heads)
    for k in ("ffn_conv_w", "ffn_conv_b"):
        w[k] = _interleave(full[k])
    return w


def _matmul_compute_layout(k, full, d_inner):
    d_xbc, r_heads = _ssd_sizes(d_inner)
    if k == "ssd_w_in":
        return _ssd_w_in_layout(full, d_inner, d_xbc, r_heads)
    if k == "ffn_w_up":
        return _interleave(full)
    return full


def _layer_matrices(i, part):
    if part == "ffn":
        return (("ffn_w_up", 1, i), ("ffn_w_down", 0, i))
    return (("ssd_w_in", 1, i // 2), ("ssd_w_out", 0, i // 2)) if i % 2 == 0 else (("pool_w", 1, i // 2),)


def _matmul_grad_reference_layout(k, g, d_inner):
    d_xbc, r_heads = _ssd_sizes(d_inner)
    if k == "ssd_w_in":
        return _ssd_w_in_unlayout(g, d_inner, d_xbc, r_heads)
    if k == "ffn_w_up":
        return _deinterleave(g)
    return g


def _small_grads_reference_layout(grads, shapes, d_inner):
    _, r_heads = _ssd_sizes(d_inner)
    g = {k: jnp.stack(grads[k]) for k in SMALL}
    for k in ("ssd_dt_bias", "ssd_a_log", "ssd_d"):
        g[k] = _head_unpad(g[k][:, 0], r_heads)
    for k in ("ffn_conv_w", "ffn_conv_b"):
        g[k] = _deinterleave(g[k])
    return {k: v.reshape(shapes[k]) for k, v in g.items()}


def kernel(x, ssd_w_in, ssd_conv_w, ssd_conv_b, ssd_dt_bias, ssd_a_log, ssd_d, ssd_norm_w, ssd_w_out, pool_w, pool_scale, ffn_w_up, ffn_conv_w, ffn_conv_b, ffn_w_down, norm_mix_pre, norm_mix_post, norm_ffn_pre, norm_ffn_post, loss_target, m_ssd_w_in, m_ssd_conv_w, m_ssd_conv_b, m_ssd_dt_bias, m_ssd_a_log, m_ssd_d, m_ssd_norm_w, m_ssd_w_out, m_pool_w, m_pool_scale, m_ffn_w_up, m_ffn_conv_w, m_ffn_conv_b, m_ffn_w_down, m_norm_mix_pre, m_norm_mix_post, m_norm_ffn_pre, m_norm_ffn_post, v_ssd_w_in, v_ssd_conv_w, v_ssd_conv_b, v_ssd_dt_bias, v_ssd_a_log, v_ssd_d, v_ssd_norm_w, v_ssd_w_out, v_pool_w, v_pool_scale, v_ffn_w_up, v_ffn_conv_w, v_ffn_conv_b, v_ffn_w_down, v_norm_mix_pre, v_norm_mix_post, v_norm_ffn_pre, v_norm_ffn_post):
    shards = dict(ssd_w_in=ssd_w_in, ssd_conv_w=ssd_conv_w, ssd_conv_b=ssd_conv_b, ssd_dt_bias=ssd_dt_bias,
                  ssd_a_log=ssd_a_log, ssd_d=ssd_d, ssd_norm_w=ssd_norm_w, ssd_w_out=ssd_w_out, pool_w=pool_w,
                  pool_scale=pool_scale, ffn_w_up=ffn_w_up, ffn_conv_w=ffn_conv_w, ffn_conv_b=ffn_conv_b,
                  ffn_w_down=ffn_w_down, norm_mix_pre=norm_mix_pre, norm_mix_post=norm_mix_post,
                  norm_ffn_pre=norm_ffn_pre, norm_ffn_post=norm_ffn_post)
    moments_m = dict(zip(WEIGHTS, (m_ssd_w_in, m_ssd_conv_w, m_ssd_conv_b, m_ssd_dt_bias, m_ssd_a_log, m_ssd_d, m_ssd_norm_w, m_ssd_w_out, m_pool_w, m_pool_scale, m_ffn_w_up, m_ffn_conv_w, m_ffn_conv_b, m_ffn_w_down, m_norm_mix_pre, m_norm_mix_post, m_norm_ffn_pre, m_norm_ffn_post)))
    moments_v = dict(zip(WEIGHTS, (v_ssd_w_in, v_ssd_conv_w, v_ssd_conv_b, v_ssd_dt_bias, v_ssd_a_log, v_ssd_d, v_ssd_norm_w, v_ssd_w_out, v_pool_w, v_pool_scale, v_ffn_w_up, v_ffn_conv_w, v_ffn_conv_b, v_ffn_w_down, v_norm_mix_pre, v_norm_mix_post, v_norm_ffn_pre, v_norm_ffn_post)))
    n_seq, seq, d_model = x.shape
    t = n_seq * seq

    full = dict(shards)
    small_all = _all_gather(_pack_rows([shards[k] for k, _ in SMALL_SHARDED]), name="gather_small_weights")
    small_stacked = _unpack_rows(small_all, [shards[k].shape for k, _ in SMALL_SHARDED], lead=(N_DEV,))
    for (k, axis), st in zip(SMALL_SHARDED, small_stacked):
        full[k] = _unshard(st, axis)
    d_inner = ssd_norm_w.shape[1]
    w = _small_compute_layout(full, d_inner)
    depth = norm_mix_pre.shape[0]
    x2 = x.reshape(t, d_model)

    shard16 = {k: shards[k].astype(BF16) for k, _ in BIG}
    order = [(i, part) for i in range(depth) for part in ("mix", "ffn")]
    fetches = {}

    def start_fetch(key, after):
        srcs = [shard16[k][l] for k, _, l in _layer_matrices(*key)]
        fetches[key] = _push_start(srcs, False, after, name=f"fetch{key[0]}{key[1]}_start")

    def fetch(i, part, x_now):
        key = (i, part)
        if key == order[0]:
            start_fetch(key, x_now)
        send, recv, srcs, lands, _ = fetches[key]
        lands = _push_wait(send, recv, srcs, lands, False, x_now, name=f"fetch{i}{part}_wait")
        mats = {}
        for (k, axis, l), land in zip(_layer_matrices(i, part), lands):
            whole = _unshard(_with_own_slot(land, shard16[k][l]), axis)
            mats[k] = _matmul_compute_layout(k, whole, d_inner)
        nxt = order.index(key) + 1
        if nxt < len(order):
            start_fetch(order[nxt], lands[0])
            return mats, fetches[order[nxt]][4]
        return mats, jnp.zeros((8, LANE), F32)

    g_layers = {}
    in_flight = []

    def finish_exchange(after):
        key, blocks, (send, recv, srcs, lands, _) = in_flight.pop()
        lands = _push_wait(send, recv, srcs, lands, True, after, name=f"exchange{key[0]}{key[1]}_wait")
        for (k, _, l), land, block in zip(_layer_matrices(*key), lands, blocks):
            own = lax.dynamic_index_in_dim(block, _my_index(), 0, keepdims=False)
            g_layers[k, l] = _sum_slots(_with_own_slot(land, own), name=f"sum{key[0]}_{k}")

    def emit(i, part, gm, dx_now):
        if in_flight:
            finish_exchange(dx_now)
        blocks = [_shard_major(_matmul_grad_reference_layout(k, gm[k].astype(BF16), d_inner), axis)
                  for k, axis, _ in _layer_matrices(i, part)]
        started = _push_start(blocks, True, dx_now, name=f"exchange{i}{part}_start")
        in_flight.append(((i, part), blocks, started))
        return started[4]

    loss, dx, grads = _local_step(x2, loss_target.reshape(t, d_model), w, fetch, emit, n_seq)
    finish_exchange(dx)
    loss = lax.psum(loss, ("x", "y", "c"))
    g_shard = {k: jnp.stack([g_layers[k, l] for l in range(shards[k].shape[0])]) for k, _ in BIG}

    small_shapes = {k: full[k].shape for k in SMALL}
    g_small = _small_grads_reference_layout(grads, small_shapes, d_inner)
    s_all = _all_gather(_pack_rows([g_small[k] for k in SMALL]), name="gather_small_grads")
    for k, g in zip(SMALL, _unpack_rows(_sum_slots(s_all, name="sum_small_grads"), [small_shapes[k] for k in SMALL])):
        g_shard[k] = g
    for k, axis in SMALL_SHARDED:
        g_shard[k] = _my_shard(g_shard[k], axis)

    deltas, new_m, new_v = {}, {}, {}
    for k in WEIGHTS:
        deltas[k], new_m[k], new_v[k] = _adamw(shards[k], g_shard[k], moments_m[k], moments_v[k], name=f"adamw_{k}")
    return (loss, dx.reshape(x.shape), *[g_shard[k] for k in WEIGHTS], *[deltas[k] for k in WEIGHTS],
            *[new_m[k] for k in WEIGHTS], *[new_v[k] for k in WEIGHTS])
```

```python
import functools

import jax
import jax.numpy as jnp
from jax import lax
from jax.experimental import pallas as pl
from jax.experimental.pallas import tpu as pltpu

F32 = jnp.float32
BF16 = jnp.bfloat16

N_DEV = 8
HEAD_DIM = 64
N_SSD_GROUPS = 4
D_STATE = 128
CHUNK = 128
POOL_WINDOWS = (2, 4, 8, 16)
EPS = 1e-6
LANE = 128
ADAM_LR = 0.001
ADAM_B1 = 0.9
ADAM_B2 = 0.999
ADAM_EPS = 1e-08
ADAM_WD = 0.01
ADAM_STEP = 10
VMEM_LIMIT = 56 * 1024 * 1024
ANY = pl.BlockSpec(memory_space=pl.ANY)


def _pick(n, cands):
    for c in cands:
        if n % c == 0:
            return c
    return n


def _params(sem):
    return pltpu.CompilerParams(dimension_semantics=sem, vmem_limit_bytes=VMEM_LIMIT)


def _sigmoid(x):
    return 0.5 * jnp.tanh(0.5 * x) + 0.5


def _silu(x):
    return x * _sigmoid(x)


def _dsilu(x):
    s = _sigmoid(x)
    return s * (1.0 + x * (1.0 - s))


def _shift_down(x, s):
    rows = lax.broadcasted_iota(jnp.int32, x.shape, 0)
    return jnp.where(rows >= s, pltpu.roll(x, s, 0), 0.0)


def _shift_up(x, s):
    n = x.shape[0]
    rows = lax.broadcasted_iota(jnp.int32, x.shape, 0)
    return jnp.where(rows < n - s, pltpu.roll(x, n - s, 0), 0.0)


MM_VMEM_BUDGET = 40 * 1024 * 1024
MM_STEP_BYTES = 1_300_000
MM_SUB = 512


def _mm_tiles(m, n, k, a_bytes, b_bytes, o_bytes):
    def cands(dim, sizes):
        out = [s for s in sizes if s <= dim and dim % s == 0]
        return out or [dim]

    best = None
    for tm in cands(m, (m, m // 2, 2048, 1024, 512, 256, 128)):
        if tm % LANE:
            continue
        for tn in cands(n, (n, n // 2, n // 4, 2048, 1024, 512, 256, 128)):
            if tn % (2 * LANE) and tn != n:
                continue
            for tk in cands(k, (k, k // 2, 2048, 1024, 512)):
                if tk % LANE:
                    continue
                nk = k // tk
                acc = tm * tn * 4 if (nk > 1 and o_bytes != 4) else 0
                temps = tm * min(tn, MM_SUB) * 4 + (tm * tk * 2 if a_bytes == 4 else 0) + (tk * tn * 2 if b_bytes == 4 else 0)
                vmem = 2 * (tm * tk * a_bytes + tk * tn * b_bytes + tm * tn * o_bytes) + acc + temps
                if vmem > MM_VMEM_BUDGET:
                    continue
                steps = (m // tm) * (n // tn) * nk
                cost = (m * k * a_bytes * (n // tn) + k * n * b_bytes * (m // tm) + m * n * o_bytes
                        + steps * MM_STEP_BYTES)
                if best is None or cost < best[0]:
                    best = (cost, tm, tn, tk)
    return best[1:]


def _mm(a, b, *, ta=False, tb=False, out_dtype=F32, name="mm"):
    m, k = (a.shape[1], a.shape[0]) if ta else a.shape
    n = b.shape[0] if tb else b.shape[1]
    o_bytes = jnp.dtype(out_dtype).itemsize
    tm, tn, tk = _mm_tiles(m, n, k, a.dtype.itemsize, b.dtype.itemsize, o_bytes)
    nk = k // tk
    sub = _pick(tn, (MM_SUB, 256))
    use_acc = nk > 1 and o_bytes != 4
    a_spec = pl.BlockSpec((tk, tm), lambda i, j, kk: (kk, i)) if ta else pl.BlockSpec((tm, tk), lambda i, j, kk: (i, kk))
    b_spec = pl.BlockSpec((tn, tk), lambda i, j, kk: (j, kk)) if tb else pl.BlockSpec((tk, tn), lambda i, j, kk: (kk, j))
    dims = (((1,), (1 if tb else 0,)), ((), ()))

    def body(a_ref, b_ref, o_ref, *scratch):
        kk = pl.program_id(2)
        acc_ref = scratch[0] if use_acc else o_ref
        if nk > 1:
            @pl.when(kk == 0)
            def _():
                acc_ref[...] = jnp.zeros_like(acc_ref)

        av = a_ref[...].astype(BF16)
        if ta:
            av = av.T
        for s in range(tn // sub):
            cols = slice(s * sub, (s + 1) * sub)
            bv = (b_ref[cols, :] if tb else b_ref[:, cols]).astype(BF16)
            part = lax.dot_general(av, bv, dims, preferred_element_type=F32)
            if nk == 1:
                o_ref[:, cols] = part.astype(out_dtype)
            else:
                acc_ref[:, cols] += part
        if use_acc:
            @pl.when(kk == nk - 1)
            def _():
                o_ref[...] = acc_ref[...].astype(out_dtype)

    return pl.pallas_call(
        body,
        name=name,
        grid=(m // tm, n // tn, nk),
        in_specs=[a_spec, b_spec],
        out_specs=pl.BlockSpec((tm, tn), lambda i, j, kk: (i, j)),
        out_shape=jax.ShapeDtypeStruct((m, n), out_dtype),
        scratch_shapes=[pltpu.VMEM((tm, tn), F32)] if use_acc else [],
        compiler_params=_params(("parallel", "parallel", "arbitrary")),
    )(a, b)


def _rms_fwd(x, w, *, out_dtype, name):
    t, d = x.shape
    tm = _pick(t, (512, 256, 128))

    def body(x_ref, w_ref, o_ref):
        xv = x_ref[...]
        rstd = lax.rsqrt(jnp.mean(xv * xv, axis=-1, keepdims=True) + EPS)
        o_ref[...] = (xv * rstd * w_ref[...]).astype(out_dtype)

    return pl.pallas_call(
        body,
        name=name,
        grid=(t // tm,),
        in_specs=[pl.BlockSpec((tm, d), lambda i: (i, 0)), pl.BlockSpec((1, d), lambda i: (0, 0))],
        out_specs=pl.BlockSpec((tm, d), lambda i: (i, 0)),
        out_shape=jax.ShapeDtypeStruct((t, d), out_dtype),
        compiler_params=_params(("parallel",)),
    )(x, w)


def _res_rms_fwd(x, f, w, *, name):
    t, d = x.shape
    tm = _pick(t, (512, 256, 128))

    def body(x_ref, f_ref, w_ref, o_ref):
        fv = f_ref[...]
        rstd = lax.rsqrt(jnp.mean(fv * fv, axis=-1, keepdims=True) + EPS)
        o_ref[...] = x_ref[...] + fv * rstd * w_ref[...]

    row = pl.BlockSpec((tm, d), lambda i: (i, 0))
    return pl.pallas_call(
        body,
        name=name,
        grid=(t // tm,),
        in_specs=[row, row, pl.BlockSpec((1, d), lambda i: (0, 0))],
        out_specs=row,
        out_shape=jax.ShapeDtypeStruct((t, d), F32),
        compiler_params=_params(("parallel",)),
    )(x, f, w)


def _rms_bwd(x, w, dy, resid, *, out_dtype=F32, name):
    t, d = x.shape
    tm = _pick(t, (512, 256, 128))
    has_res = resid is not None

    def body(*refs):
        if has_res:
            x_ref, w_ref, dy_ref, r_ref, dx_ref, dw_ref = refs
        else:
            x_ref, w_ref, dy_ref, dx_ref, dw_ref = refs
        xv = x_ref[...]
        dyv = dy_ref[...].astype(F32)
        rstd = lax.rsqrt(jnp.mean(xv * xv, axis=-1, keepdims=True) + EPS)
        xn = xv * rstd
        g = dyv * w_ref[...]
        dx = rstd * (g - xn * jnp.mean(g * xn, axis=-1, keepdims=True))
        if has_res:
            dx = dx + r_ref[...]
        dx_ref[...] = dx.astype(out_dtype)
        part = jnp.sum(dyv * xn, axis=0, keepdims=True)

        @pl.when(pl.program_id(0) == 0)
        def _():
            dw_ref[...] = part

        @pl.when(pl.program_id(0) > 0)
        def _():
            dw_ref[...] += part

    row = pl.BlockSpec((tm, d), lambda i: (i, 0))
    vec = pl.BlockSpec((1, d), lambda i: (0, 0))
    ins = [x, w, dy] + ([resid] if has_res else [])
    return pl.pallas_call(
        body,
        name=name,
        grid=(t // tm,),
        in_specs=[row, vec, row] + ([row] if has_res else []),
        out_specs=[row, vec],
        out_shape=[jax.ShapeDtypeStruct((t, d), out_dtype), jax.ShapeDtypeStruct((1, d), F32)],
        compiler_params=_params(("arbitrary",)),
    )(*ins)


def _loss_head(y, target, *, name="loss_head"):
    t, d = y.shape
    tm = _pick(t, (512, 256, 128))

    def body(y_ref, t_ref, dy_ref, l_ref):
        err = y_ref[...] - t_ref[...]
        dy_ref[...] = err * (1.0 / d)
        part = jnp.sum(jnp.sum(err * err, axis=-1, keepdims=True), axis=0, keepdims=True) * (0.5 / d)
        part = jnp.broadcast_to(part, (1, LANE))

        @pl.when(pl.program_id(0) == 0)
        def _():
            l_ref[...] = part

        @pl.when(pl.program_id(0) > 0)
        def _():
            l_ref[...] += part

    row = pl.BlockSpec((tm, d), lambda i: (i, 0))
    dy, l = pl.pallas_call(
        body,
        name=name,
        grid=(t // tm,),
        in_specs=[row, row],
        out_specs=[row, pl.BlockSpec((1, LANE), lambda i: (0, 0))],
        out_shape=[jax.ShapeDtypeStruct((t, d), F32), jax.ShapeDtypeStruct((1, LANE), F32)],
        compiler_params=_params(("arbitrary",)),
    )(y, target)
    return l[0, 0], dy


def _conv_taps(h, w_ref, k_taps):
    out = h * w_ref[k_taps - 1:k_taps, :]
    for k in range(k_taps - 1):
        out = out + _shift_down(h, k_taps - 1 - k) * w_ref[k:k + 1, :]
    return out


def _conv_taps_bwd(h, dhc, w_ref, k_taps):
    dh = dhc * w_ref[k_taps - 1:k_taps, :]
    dws = []
    for k in range(k_taps - 1):
        up = _shift_up(dhc, k_taps - 1 - k)
        dh = dh + up * w_ref[k:k + 1, :]
        dws.append(jnp.sum(up * h, axis=0, keepdims=True))
    dws.append(jnp.sum(dhc * h, axis=0, keepdims=True))
    return dh, jnp.concatenate(dws, axis=0)


FFN_TC = 256


def _interleave(w, tc=FFN_TC):
    f = w.shape[-1] // 2
    lead = w.shape[:-1]
    return jnp.swapaxes(w.reshape(lead + (2, f // tc, tc)), -3, -2).reshape(lead + (2 * f,))


def _deinterleave(w, tc=FFN_TC):
    f = w.shape[-1] // 2
    lead = w.shape[:-1]
    return jnp.swapaxes(w.reshape(lead + (f // tc, 2, tc)), -3, -2).reshape(lead + (2 * f,))


def _ffn_act_fwd(h, conv_w, conv_b, n_seq, *, name):
    t, f2 = h.shape
    seq = t // n_seq
    tc = FFN_TC
    nj = f2 // (2 * tc)
    k_taps = conv_w.shape[0]

    def body(h_ref, w_ref, b_ref, o_ref, hc_ref):
        hc = _conv_taps(h_ref[...].astype(F32), w_ref, k_taps) + b_ref[...]
        hc_ref[...] = hc.astype(BF16)
        o_ref[...] = (_silu(hc[:, :tc]) * hc[:, tc:]).astype(BF16)

    return pl.pallas_call(
        body,
        name=name,
        grid=(n_seq, nj),
        in_specs=[
            pl.BlockSpec((seq, 2 * tc), lambda b, j: (b, j)),
            pl.BlockSpec((k_taps, 2 * tc), lambda b, j: (0, j)),
            pl.BlockSpec((1, 2 * tc), lambda b, j: (0, j)),
        ],
        out_specs=[pl.BlockSpec((seq, tc), lambda b, j: (b, j)), pl.BlockSpec((seq, 2 * tc), lambda b, j: (b, j))],
        out_shape=[jax.ShapeDtypeStruct((t, f2 // 2), BF16), jax.ShapeDtypeStruct((t, f2), BF16)],
        compiler_params=_params(("parallel", "parallel")),
    )(h, conv_w, conv_b)


def _ffn_act_bwd(h, hc, conv_w, da, n_seq, *, name):
    t, f2 = h.shape
    seq = t // n_seq
    tc = FFN_TC
    nj = f2 // (2 * tc)
    k_taps = conv_w.shape[0]

    def body(h_ref, hc_ref, w_ref, da_ref, dh_ref, dw_ref, db_ref):
        hcv = hc_ref[...].astype(F32)
        gate, val = hcv[:, :tc], hcv[:, tc:]
        dav = da_ref[...].astype(F32)
        dhc = jnp.concatenate([dav * val * _dsilu(gate), dav * _silu(gate)], axis=1)
        dh, dw = _conv_taps_bwd(h_ref[...].astype(F32), dhc, w_ref, k_taps)
        dh_ref[...] = dh.astype(BF16)
        db = jnp.sum(dhc, axis=0, keepdims=True)

        @pl.when(pl.program_id(1) == 0)
        def _():
            dw_ref[...] = dw
            db_ref[...] = db

        @pl.when(pl.program_id(1) > 0)
        def _():
            dw_ref[...] += dw
            db_ref[...] += db

    wide = pl.BlockSpec((seq, 2 * tc), lambda j, b: (b, j))
    return pl.pallas_call(
        body,
        name=name,
        grid=(nj, n_seq),
        in_specs=[wide, wide, pl.BlockSpec((k_taps, 2 * tc), lambda j, b: (0, j)),
                  pl.BlockSpec((seq, tc), lambda j, b: (b, j))],
        out_specs=[
            wide,
            pl.BlockSpec((k_taps, 2 * tc), lambda j, b: (0, j)),
            pl.BlockSpec((1, 2 * tc), lambda j, b: (0, j)),
        ],
        out_shape=[
            jax.ShapeDtypeStruct((t, f2), BF16),
            jax.ShapeDtypeStruct((k_taps, f2), F32),
            jax.ShapeDtypeStruct((1, f2), F32),
        ],
        compiler_params=_params(("parallel", "arbitrary")),
    )(h, hc, conv_w, da)


def _window_mixed(u, window):
    s = u
    step = 1
    while step < window:
        s = s + _shift_down(s, step)
        step *= 2
    rows = lax.broadcasted_iota(jnp.int32, u.shape, 0)
    inv_cnt = 1.0 / jnp.minimum(rows + 1, window).astype(F32)
    return s * inv_cnt - u, inv_cnt


def _window_mixed_bwd(dmixed, inv_cnt, window):
    r = dmixed * inv_cnt
    s = r
    step = 1
    while step < window:
        s = s + _shift_up(s, step)
        step *= 2
    return s - dmixed


def _pool_fwd(u, w, scale, n_seq, *, name):
    t, d = u.shape
    seq = t // n_seq
    n_g, dg, _ = w.shape

    def body(u_ref, w_ref, s_ref, o_ref):
        for k, window in enumerate(POOL_WINDOWS):
            @pl.when(pl.program_id(1) == k)
            def _(window=window):
                mixed, _ = _window_mixed(u_ref[...], window)
                pre = jnp.dot(mixed.astype(BF16), w_ref[0].astype(BF16), preferred_element_type=F32)
                o_ref[...] = pre * s_ref[...]

    return pl.pallas_call(
        body,
        name=name,
        grid=(n_seq, n_g),
        in_specs=[
            pl.BlockSpec((seq, dg), lambda b, g: (b, g)),
            pl.BlockSpec((1, dg, dg), lambda b, g: (g, 0, 0)),
            pl.BlockSpec((1, dg), lambda b, g: (0, g)),
        ],
        out_specs=pl.BlockSpec((seq, dg), lambda b, g: (b, g)),
        out_shape=jax.ShapeDtypeStruct((t, d), F32),
        compiler_params=_params(("parallel", "parallel")),
    )(u, w, scale)


def _pool_bwd(u, w, scale, dout, n_seq, *, name):
    t, d = u.shape
    seq = t // n_seq
    n_g, dg, _ = w.shape

    def body(u_ref, w_ref, s_ref, do_ref, du_ref, dw_ref, ds_ref):
        group = pl.program_id(0)
        first = pl.program_id(1) == 0
        for k, window in enumerate(POOL_WINDOWS):
            @pl.when(group == k)
            def _(window=window):
                mixed, inv_cnt = _window_mixed(u_ref[...], window)
                mixed_b = mixed.astype(BF16)
                w_b = w_ref[0].astype(BF16)
                dov = do_ref[...]
                pre = jnp.dot(mixed_b, w_b, preferred_element_type=F32)
                dsc = jnp.sum(dov * pre, axis=0, keepdims=True)
                dpre = (dov * s_ref[...]).astype(BF16)
                dw = lax.dot_general(mixed_b, dpre, (((0,), (0,)), ((), ())), preferred_element_type=F32)
                dmixed = lax.dot_general(dpre, w_b, (((1,), (1,)), ((), ())), preferred_element_type=F32)
                du_ref[...] = _window_mixed_bwd(dmixed, inv_cnt, window)

                @pl.when(first)
                def _():
                    dw_ref[0] = dw
                    ds_ref[...] = dsc

                @pl.when(jnp.logical_not(first))
                def _():
                    dw_ref[0] += dw
                    ds_ref[...] += dsc

    return pl.pallas_call(
        body,
        name=name,
        grid=(n_g, n_seq),
        in_specs=[
            pl.BlockSpec((seq, dg), lambda g, b: (b, g)),
            pl.BlockSpec((1, dg, dg), lambda g, b: (g, 0, 0)),
            pl.BlockSpec((1, dg), lambda g, b: (0, g)),
            pl.BlockSpec((seq, dg), lambda g, b: (b, g)),
        ],
        out_specs=[
            pl.BlockSpec((seq, dg), lambda g, b: (b, g)),
            pl.BlockSpec((1, dg, dg), lambda g, b: (g, 0, 0)),
            pl.BlockSpec((1, dg), lambda g, b: (0, g)),
        ],
        out_shape=[
            jax.ShapeDtypeStruct((t, d), F32),
            jax.ShapeDtypeStruct((n_g, dg, dg), F32),
            jax.ShapeDtypeStruct((1, d), F32),
        ],
        compiler_params=_params(("parallel", "arbitrary")),
    )(u, w, scale, dout)


def _adamw(w, g, m, v, *, name):
    shape = w.shape
    c = shape[-1]
    r = w.size // c
    tm = _pick(r, (512, 256, 128, 64, 32, 16, 8))

    def body(w_ref, g_ref, m_ref, v_ref, d_ref, nm_ref, nv_ref):
        gv = g_ref[...]
        nm = ADAM_B1 * m_ref[...] + (1.0 - ADAM_B1) * gv
        nv = ADAM_B2 * v_ref[...] + (1.0 - ADAM_B2) * (gv * gv)
        m_hat = nm / (1.0 - ADAM_B1 ** ADAM_STEP)
        v_hat = nv / (1.0 - ADAM_B2 ** ADAM_STEP)
        d_ref[...] = -ADAM_LR * (m_hat / (jnp.sqrt(v_hat) + ADAM_EPS) + ADAM_WD * w_ref[...])
        nm_ref[...] = nm
        nv_ref[...] = nv

    blk = pl.BlockSpec((tm, c), lambda i: (i, 0))
    out = jax.ShapeDtypeStruct((r, c), F32)
    res = pl.pallas_call(
        body,
        name=name,
        grid=(r // tm,),
        in_specs=[blk] * 4,
        out_specs=[blk] * 3,
        out_shape=[out] * 3,
        compiler_params=_params(("parallel",)),
    )(w.reshape(r, c), g.reshape(r, c), m.reshape(r, c), v.reshape(r, c))
    return tuple(a.reshape(shape) for a in res)


CONV_TC = 256


def _ssd_conv_fwd(proj, col0, n_cols, conv_w, conv_b, n_seq, *, name):
    t = proj.shape[0]
    seq = t // n_seq
    tc = CONV_TC
    off = col0 // tc
    k_taps = conv_w.shape[0]

    def body(h_ref, w_ref, b_ref, o_ref, pre_ref):
        pre = _conv_taps(h_ref[...], w_ref, k_taps) + b_ref[...]
        pre_ref[...] = pre.astype(BF16)
        o_ref[...] = _silu(pre)

    return pl.pallas_call(
        body,
        name=name,
        grid=(n_seq, n_cols // tc),
        in_specs=[
            pl.BlockSpec((seq, tc), lambda b, j: (b, j + off)),
            pl.BlockSpec((k_taps, tc), lambda b, j: (0, j)),
            pl.BlockSpec((1, tc), lambda b, j: (0, j)),
        ],
        out_specs=[pl.BlockSpec((seq, tc), lambda b, j: (b, j))] * 2,
        out_shape=[jax.ShapeDtypeStruct((t, n_cols), F32), jax.ShapeDtypeStruct((t, n_cols), BF16)],
        compiler_params=_params(("parallel", "parallel")),
    )(proj, conv_w, conv_b)


def _ssd_conv_bwd(proj, col0, conv_w, pre, dparts, dproj, n_seq, *, name):
    t = proj.shape[0]
    seq = t // n_seq
    tc = CONV_TC
    off = col0 // tc
    k_taps = conv_w.shape[0]
    widths = [d.shape[1] // tc for d in dparts]
    starts = [sum(widths[:i]) for i in range(len(widths))]
    n_blocks = sum(widths)
    n_parts = len(dparts)

    def body(h_ref, w_ref, pre_ref, *rest):
        part_refs = rest[:n_parts]
        dh_ref, dw_ref, db_ref = rest[n_parts + 1:]
        j = pl.program_id(0)
        da = part_refs[-1][...]
        for i in reversed(range(n_parts - 1)):
            da = jnp.where(j < starts[i + 1], part_refs[i][...], da)
        dhc = da * _dsilu(pre_ref[...].astype(F32))
        dh, dw = _conv_taps_bwd(h_ref[...], dhc, w_ref, k_taps)
        dh_ref[...] = dh.astype(BF16)
        db = jnp.sum(dhc, axis=0, keepdims=True)

        @pl.when(pl.program_id(1) == 0)
        def _():
            dw_ref[...] = dw
            db_ref[...] = db

        @pl.when(pl.program_id(1) > 0)
        def _():
            dw_ref[...] += dw
            db_ref[...] += db

    def part_spec(start, width):
        return pl.BlockSpec((seq, tc), lambda j, b: (b, jnp.clip(j - start, 0, width - 1)))

    n_cols = n_blocks * tc
    return pl.pallas_call(
        body,
        name=name,
        grid=(n_blocks, n_seq),
        in_specs=[
            pl.BlockSpec((seq, tc), lambda j, b: (b, j + off)),
            pl.BlockSpec((k_taps, tc), lambda j, b: (0, j)),
            pl.BlockSpec((seq, tc), lambda j, b: (b, j)),
        ] + [part_spec(st, wd) for st, wd in zip(starts, widths)] + [ANY],
        out_specs=[
            pl.BlockSpec((seq, tc), lambda j, b: (b, j + off)),
            pl.BlockSpec((k_taps, tc), lambda j, b: (0, j)),
            pl.BlockSpec((1, tc), lambda j, b: (0, j)),
        ],
        out_shape=[
            jax.ShapeDtypeStruct(dproj.shape, BF16),
            jax.ShapeDtypeStruct((k_taps, n_cols), F32),
            jax.ShapeDtypeStruct((1, n_cols), F32),
        ],
        input_output_aliases={3 + n_parts: 0},
        compiler_params=_params(("parallel", "arbitrary")),
    )(proj, conv_w, pre, *dparts, dproj)


def _fill_cols(buf, src, col0, *, name):
    t, c = src.shape
    tm = _pick(t, (1024, 512, 256, 128))

    def body(s_ref, b_ref, o_ref):
        o_ref[...] = s_ref[...].astype(o_ref.dtype)

    return pl.pallas_call(
        body,
        name=name,
        grid=(t // tm,),
        in_specs=[pl.BlockSpec((tm, c), lambda i: (i, 0)), ANY],
        out_specs=pl.BlockSpec((tm, c), lambda i: (i, col0 // c)),
        out_shape=jax.ShapeDtypeStruct(buf.shape, buf.dtype),
        input_output_aliases={1: 0},
        compiler_params=_params(("parallel",)),
    )(src, buf)


def _softplus(x):
    return jnp.maximum(x, 0.0) + jnp.log(1.0 + jnp.exp(-jnp.abs(x)))


def _chunk_decay(dtraw, bias, alog):
    q = dtraw.shape[0]
    dt = _softplus(dtraw + bias)
    a = -jnp.exp(alog)
    rows = lax.broadcasted_iota(jnp.int32, (q, q), 0)
    cols = lax.broadcasted_iota(jnp.int32, (q, q), 1)
    lower = rows >= cols
    acum = jnp.dot(lower.astype(F32), dt * a, precision=lax.Precision.HIGHEST, preferred_element_type=F32)
    return dt, a, acum, acum.T, lower


def _dot_exact(v, sel):
    hi = v.astype(BF16)
    r1 = v - hi.astype(F32)
    mid = r1.astype(BF16)
    lo = (r1 - mid.astype(F32)).astype(BF16)
    return (jnp.dot(hi, sel, preferred_element_type=F32) + jnp.dot(mid, sel, preferred_element_type=F32)
            + jnp.dot(lo, sel, preferred_element_type=F32))


def _head_selectors(gw, p):
    sum_heads = (lax.broadcasted_iota(jnp.int32, (gw, LANE), 0) // p == lax.broadcasted_iota(jnp.int32, (gw, LANE), 1))
    spread = (lax.broadcasted_iota(jnp.int32, (LANE, gw), 0) == lax.broadcasted_iota(jnp.int32, (LANE, gw), 1) // p)
    return sum_heads.astype(BF16), spread.astype(BF16)


def _row_spread(v, spread):
    return _dot_exact(jnp.broadcast_to(v, (8, v.shape[1])), spread)[0:1, :]


def _head_pad(v, r_heads):
    lead = v.shape[:-1]
    vg = v.reshape(lead + (N_SSD_GROUPS, r_heads))
    vg = jnp.pad(vg, [(0, 0)] * len(lead) + [(0, 0), (0, LANE - r_heads)])
    out = vg.reshape(lead + (N_SSD_GROUPS * LANE,))
    return out[None] if out.ndim == 1 else out


def _head_unpad(v, r_heads):
    lead = v.shape[:-1]
    out = v.reshape(lead + (N_SSD_GROUPS, LANE))[..., :r_heads].reshape(lead + (N_SSD_GROUPS * r_heads,))
    return out[0] if (len(lead) == 1 and lead[0] == 1) else out


def _ssd_w_in_layout(w_in, d_inner, d_xbc, r_heads):
    main = w_in[:, :d_inner + d_xbc]
    return jnp.concatenate([main, _head_pad(w_in[:, d_inner + d_xbc:], r_heads)], axis=1)


def _ssd_w_in_unlayout(w, d_inner, d_xbc, r_heads):
    main = w[:, :d_inner + d_xbc]
    return jnp.concatenate([main, _head_unpad(w[:, d_inner + d_xbc:], r_heads)], axis=1)


def _ssd_dims(proj, xbc):
    d_xbc = xbc.shape[1]
    d_inner = d_xbc - 2 * N_SSD_GROUPS * D_STATE
    gw = d_inner // N_SSD_GROUPS
    return d_inner, d_xbc, gw, gw // HEAD_DIM


def _ssd_fwd(proj, xbc, bias_p, alog_p, dskip_p, norm_w, n_seq, *, name):
    t = proj.shape[0]
    d_inner, d_xbc, gw, r_heads = _ssd_dims(proj, xbc)
    q, n, n_g, p = CHUNK, D_STATE, N_SSD_GROUPS, HEAD_DIM
    seq = t // n_seq
    nc = seq // q
    dt_blk0 = (d_inner + d_xbc) // LANE

    def body(x_ref, b_ref, c_ref, z_ref, dtr_ref, bias_ref, alog_ref, dsk_ref, nw_ref, yn_ref, y_ref, hs_ref, h_scr):
        @pl.when(pl.program_id(2) == 0)
        def _():
            h_scr[...] = jnp.zeros_like(h_scr)

        dt, a, acum, acum_t, lower = _chunk_decay(dtr_ref[...], bias_ref[...], alog_ref[...])
        x = x_ref[...]
        bb = b_ref[...].astype(BF16)
        cb = c_ref[...].astype(BF16)
        g_mat = lax.dot_general(cb, bb, (((1,), (1,)), ((), ())), preferred_element_type=F32)
        h_prev = h_scr[...]
        hs_ref[...] = h_prev
        c_h = jnp.dot(cb, h_prev.astype(BF16), preferred_element_type=F32)
        _, spread = _head_selectors(gw, p)
        acum_s = _dot_exact(acum, spread)
        a_last_s = acum_s[q - 1:q, :]
        xdt = x * _dot_exact(dt, spread)
        xdt_b = xdt.astype(BF16)
        ys = []
        for h in range(r_heads):
            decay = jnp.exp(jnp.where(lower, acum[:, h:h + 1] - acum_t[h:h + 1, :], -jnp.inf))
            ys.append(jnp.dot((g_mat * decay).astype(BF16), xdt_b[:, h * p:(h + 1) * p], preferred_element_type=F32))
        y = jnp.concatenate(ys, axis=1) + jnp.exp(acum_s) * c_h + _row_spread(dsk_ref[...], spread) * x
        xd = xdt * jnp.exp(a_last_s - acum_s)
        states = lax.dot_general(bb, xd.astype(BF16), (((0,), (0,)), ((), ())), preferred_element_type=F32)
        h_scr[...] = h_prev * jnp.exp(a_last_s) + states
        y_ref[...] = y
        gated = y * _silu(z_ref[...])
        rstd = lax.rsqrt(jnp.mean(gated * gated, axis=-1, keepdims=True) + EPS)
        yn_ref[...] = (gated * rstd * nw_ref[...]).astype(BF16)

    row = lambda b, g, c: b * nc + c
    vec = pl.BlockSpec((1, LANE), lambda b, g, c: (0, g))
    return pl.pallas_call(
        body,
        name=name,
        grid=(n_seq, n_g, nc),
        in_specs=[
            pl.BlockSpec((q, gw), lambda b, g, c: (row(b, g, c), g)),
            pl.BlockSpec((q, n), lambda b, g, c: (row(b, g, c), d_inner // n + g)),
            pl.BlockSpec((q, n), lambda b, g, c: (row(b, g, c), d_inner // n + n_g + g)),
            pl.BlockSpec((q, gw), lambda b, g, c: (row(b, g, c), g)),
            pl.BlockSpec((q, LANE), lambda b, g, c: (row(b, g, c), dt_blk0 + g)),
            vec, vec, vec,
            pl.BlockSpec((1, gw), lambda b, g, c: (0, g)),
        ],
        out_specs=[
            pl.BlockSpec((q, gw), lambda b, g, c: (row(b, g, c), g)),
            pl.BlockSpec((q, gw), lambda b, g, c: (row(b, g, c), g)),
            pl.BlockSpec((n, gw), lambda b, g, c: (row(b, g, c), g)),
        ],
        out_shape=[
            jax.ShapeDtypeStruct((t, d_inner), BF16),
            jax.ShapeDtypeStruct((t, d_inner), F32),
            jax.ShapeDtypeStruct((n_seq * nc * n, d_inner), F32),
        ],
        scratch_shapes=[pltpu.VMEM((n, gw), F32)],
        compiler_params=_params(("parallel", "parallel", "arbitrary")),
    )(xbc, xbc, xbc, proj, proj, bias_p, alog_p, dskip_p, norm_w)


def _ssd_bwd(proj, xbc, hs, y, dyn, bias_p, alog_p, dskip_p, norm_w, n_seq, *, name):
    t = proj.shape[0]
    d_inner, d_xbc, gw, r_heads = _ssd_dims(proj, xbc)
    q, n, n_g, p = CHUNK, D_STATE, N_SSD_GROUPS, HEAD_DIM
    seq = t // n_seq
    nc = seq // q
    dt_blk0 = (d_inner + d_xbc) // LANE

    def body(x_ref, b_ref, c_ref, z_ref, dtr_ref, bias_ref, alog_ref, dsk_ref, nw_ref, hs_ref, y_ref, dyn_ref,
             dx_ref, db_ref, dc_ref, dz_ref, ddtr_ref, dnw_ref, dbias_ref, dalog_ref, ddsk_ref, dh_scr):
        first = jnp.logical_and(pl.program_id(1) == 0, pl.program_id(2) == 0)

        @pl.when(pl.program_id(2) == 0)
        def _():
            dh_scr[...] = jnp.zeros_like(dh_scr)

        dtraw = dtr_ref[...]
        dt, a, acum, acum_t, lower = _chunk_decay(dtraw, bias_ref[...], alog_ref[...])
        x = x_ref[...]
        bb = b_ref[...].astype(BF16)
        cb = c_ref[...].astype(BF16)
        g_mat = lax.dot_general(cb, bb, (((1,), (1,)), ((), ())), preferred_element_type=F32)

        yv = y_ref[...]
        z = z_ref[...]
        sz = _silu(z)
        gated = yv * sz
        rstd = lax.rsqrt(jnp.mean(gated * gated, axis=-1, keepdims=True) + EPS)
        gn = gated * rstd
        dynv = dyn_ref[...]
        gwt = dynv * nw_ref[...]
        dgated = rstd * (gwt - gn * jnp.mean(gwt * gn, axis=-1, keepdims=True))
        dnw = jnp.sum(dynv * gn, axis=0, keepdims=True)
        dy = dgated * sz
        dz_ref[...] = (dgated * yv * _dsilu(z)).astype(BF16)

        h_prev = hs_ref[...]
        h_prev_b = h_prev.astype(BF16)
        ds = dh_scr[...]
        ds_b = ds.astype(BF16)
        sum_heads, spread = _head_selectors(gw, p)
        acum_s = _dot_exact(acum, spread)
        a_last_s = acum_s[q - 1:q, :]
        dt_s = _dot_exact(dt, spread)
        dsk_s = _row_spread(dsk_ref[...], spread)
        dte_s = jnp.exp(a_last_s - acum_s)
        cd_s = jnp.exp(a_last_s)
        xdt = x * dt_s
        xdt_b = xdt.astype(BF16)
        dy_b = dy.astype(BF16)
        gt_mat = lax.dot_general(bb, cb, (((1,), (1,)), ((), ())), preferred_element_type=F32)
        upper = lax.broadcasted_iota(jnp.int32, (q, q), 0) <= lax.broadcasted_iota(jnp.int32, (q, q), 1)
        dg = jnp.zeros((q, q), F32)
        dxdts, w_diffs = [], []
        for h in range(r_heads):
            hsl = slice(h * p, (h + 1) * p)
            diff = acum[:, h:h + 1] - acum_t[h:h + 1, :]
            decay = jnp.exp(jnp.where(lower, diff, -jnp.inf))
            decay_t = jnp.exp(jnp.where(upper, -diff, -jnp.inf))
            mt_mat = gt_mat * decay_t
            dm = lax.dot_general(dy_b[:, hsl], xdt_b[:, hsl], (((1,), (1,)), ((), ())), preferred_element_type=F32)
            dm_t = lax.dot_general(xdt_b[:, hsl], dy_b[:, hsl], (((1,), (1,)), ((), ())), preferred_element_type=F32)
            dg = dg + dm * decay
            dxdts.append(jnp.dot(mt_mat.astype(BF16), dy_b[:, hsl], preferred_element_type=F32))
            w_diffs.append(dm * (g_mat * decay) - dm_t * mt_mat)
        sel_q = (lax.broadcasted_iota(jnp.int32, (r_heads * q, LANE), 0) // q
                 == lax.broadcasted_iota(jnp.int32, (r_heads * q, LANE), 1)).astype(BF16)
        dacum_diag = _dot_exact(jnp.concatenate(w_diffs, axis=1), sel_q)
        c_h = jnp.dot(cb, h_prev_b, preferred_element_type=F32)
        dxd = jnp.dot(bb, ds_b, preferred_element_type=F32)
        dxdt = jnp.concatenate(dxdts, axis=1) + dxd * dte_s
        dye = dy * jnp.exp(acum_s)
        dye_b = dye.astype(BF16)
        xd = xdt * dte_s
        xd_b = xd.astype(BF16)
        dg_b = dg.astype(BF16)
        dx_ref[...] = dxdt * dt_s + dsk_s * dy
        dc_ref[...] = (jnp.dot(dg_b, bb, preferred_element_type=F32)
                       + lax.dot_general(dye_b, h_prev_b, (((1,), (1,)), ((), ())), preferred_element_type=F32))
        db_ref[...] = (lax.dot_general(dg_b, cb, (((0,), (0,)), ((), ())), preferred_element_type=F32)
                       + lax.dot_general(xd_b, ds_b, (((1,), (1,)), ((), ())), preferred_element_type=F32))
        dh_scr[...] = ds * cd_s + lax.dot_general(cb, dye_b, (((0,), (0,)), ((), ())), preferred_element_type=F32)
        ddt_cols = _dot_exact(x * dxdt, sum_heads)
        dacum_y = _dot_exact(dye * c_h - dxd * xd, sum_heads)
        col_sums = jnp.concatenate([
            jnp.sum(dxd * xd, axis=0, keepdims=True) + jnp.sum(ds * h_prev, axis=0, keepdims=True) * cd_s,
            jnp.sum(dy * x, axis=0, keepdims=True),
            jnp.zeros((6, gw), F32)], axis=0)
        col_sums = _dot_exact(col_sums, sum_heads)
        ddsk = col_sums[1:2, :]
        rows_q = lax.broadcasted_iota(jnp.int32, (q, LANE), 0)
        dacum = dacum_diag + dacum_y + jnp.where(rows_q == q - 1, col_sums[0:1, :], 0.0)
        dadt = jnp.dot(upper.astype(F32), dacum, precision=lax.Precision.HIGHEST, preferred_element_type=F32)
        ddt = dadt * a + ddt_cols
        ddtr = ddt * _sigmoid(dtraw + bias_ref[...])
        ddtr_ref[...] = ddtr
        dbias = jnp.sum(ddtr, axis=0, keepdims=True)
        dalog = jnp.sum(dadt * dt, axis=0, keepdims=True) * a

        @pl.when(first)
        def _():
            dnw_ref[...] = dnw
            dbias_ref[...] = dbias
            dalog_ref[...] = dalog
            ddsk_ref[...] = ddsk

        @pl.when(jnp.logical_not(first))
        def _():
            dnw_ref[...] += dnw
            dbias_ref[...] += dbias
            dalog_ref[...] += dalog
            ddsk_ref[...] += ddsk

    row = lambda g, b, c: b * nc + (nc - 1 - c)
    vec = pl.BlockSpec((1, LANE), lambda g, b, c: (0, g))
    wide = pl.BlockSpec((q, gw), lambda g, b, c: (row(g, b, c), g))
    narrow = pl.BlockSpec((q, n), lambda g, b, c: (row(g, b, c), g))
    return pl.pallas_call(
        body,
        name=name,
        grid=(n_g, n_seq, nc),
        in_specs=[
            wide,
            pl.BlockSpec((q, n), lambda g, b, c: (row(g, b, c), d_inner // n + g)),
            pl.BlockSpec((q, n), lambda g, b, c: (row(g, b, c), d_inner // n + n_g + g)),
            wide,
            pl.BlockSpec((q, LANE), lambda g, b, c: (row(g, b, c), dt_blk0 + g)),
            vec, vec, vec,
            pl.BlockSpec((1, gw), lambda g, b, c: (0, g)),
            pl.BlockSpec((n, gw), lambda g, b, c: (row(g, b, c), g)),
            wide, wide,
        ],
        out_specs=[
            wide, narrow, narrow, wide, narrow,
            pl.BlockSpec((1, gw), lambda g, b, c: (0, g)),
            vec, vec, vec,
        ],
        out_shape=[
            jax.ShapeDtypeStruct((t, d_inner), F32),
            jax.ShapeDtypeStruct((t, n_g * n), F32),
            jax.ShapeDtypeStruct((t, n_g * n), F32),
            jax.ShapeDtypeStruct(proj.shape, BF16),
            jax.ShapeDtypeStruct((t, n_g * LANE), F32),
            jax.ShapeDtypeStruct((1, d_inner), F32),
            jax.ShapeDtypeStruct((1, n_g * LANE), F32),
            jax.ShapeDtypeStruct((1, n_g * LANE), F32),
            jax.ShapeDtypeStruct((1, n_g * LANE), F32),
        ],
        scratch_shapes=[pltpu.VMEM((n, gw), F32)],
        compiler_params=_params(("parallel", "arbitrary", "arbitrary")),
    )(xbc, xbc, xbc, proj, proj, bias_p, alog_p, dskip_p, norm_w, hs, y, dyn)


MESH_IDS = pl.DeviceIdType.MESH


def _my_index():
    return 4 * lax.axis_index("x") + 2 * lax.axis_index("y") + lax.axis_index("c")


def _all_gather(shard, *, name):
    def body(x_ref, out_ref, send_sems, recv_sems, local_sem):
        x, y, c = lax.axis_index("x"), lax.axis_index("y"), lax.axis_index("c")
        me, sibling = (x, y, c), (x, y, 1 - c)
        chips = [(1 - x, y), (x, 1 - y), (1 - x, 1 - y)]

        def blk(px, py, pc):
            return out_ref.at[4 * px + 2 * py + pc]

        def copy(k, block, to, src=None):
            return pltpu.make_async_remote_copy(
                src_ref=blk(*block) if src is None else src, dst_ref=blk(*block),
                send_sem=send_sems.at[k], recv_sem=recv_sems.at[k], device_id=to, device_id_type=MESH_IDS)

        mine = pltpu.make_async_copy(x_ref, blk(*me), local_sem)
        mine.start()
        first = [copy(0, me, sibling, src=x_ref)]
        first += [copy(1 + j, me, (*chip, c), src=x_ref) for j, chip in enumerate(chips)]
        for cp in first:
            cp.start()
        passed = [copy(4 + j, (*chip, c), sibling) for j, chip in enumerate(chips)]
        for j, chip in enumerate(chips):
            copy(1 + j, (*chip, c), me).wait_recv()
            passed[j].start()
        copy(0, sibling, me).wait_recv()
        for j, chip in enumerate(chips):
            copy(4 + j, (*chip, 1 - c), me).wait_recv()
        for cp in first + passed:
            cp.wait_send()
        mine.wait()

    return pl.pallas_call(
        body,
        name=name,
        in_specs=[ANY],
        out_specs=ANY,
        out_shape=jax.ShapeDtypeStruct((N_DEV,) + shard.shape, shard.dtype),
        scratch_shapes=[pltpu.SemaphoreType.DMA((7,)), pltpu.SemaphoreType.DMA((7,)), pltpu.SemaphoreType.DMA],
    )(shard)


HBM_SPEC = pl.BlockSpec(memory_space=pltpu.HBM)
SEM_SPEC = pl.BlockSpec(memory_space=pltpu.SEMAPHORE)
SPLIT_COPY_PARAMS = pltpu.CompilerParams(has_side_effects=pltpu.SideEffectType.DATAFLOW_SIDE_EFFECTING)


def _peer_list():
    x, y, c = lax.axis_index("x"), lax.axis_index("y"), lax.axis_index("c")
    peers = []
    for k in range(1, N_DEV):
        px = 1 - x if k & 4 else x
        py = 1 - y if k & 2 else y
        pc = 1 - c if k & 1 else c
        peers.append(((px, py, pc), 4 * px + 2 * py + pc))
    return 4 * x + 2 * y + c, peers


def _push_copies(src_refs, land_refs, send_sems, recv_sems, blockwise):
    me, peers = _peer_list()
    copies = []
    for a, (src_ref, land_ref) in enumerate(zip(src_refs, land_refs)):
        for k, (dev, idx) in enumerate(peers):
            sem = a * (N_DEV - 1) + k
            src = src_ref.at[idx] if blockwise else src_ref
            copies.append(tuple(
                pltpu.make_async_remote_copy(src_ref=src, dst_ref=land_ref.at[slot], send_sem=send_sems.at[sem],
                                             recv_sem=recv_sems.at[sem], device_id=dev, device_id_type=MESH_IDS)
                for slot in (me, idx)))
    return copies


def _push_start(srcs, blockwise, after, *, name):
    n = len(srcs)
    blocks = [s_.shape[1:] if blockwise else s_.shape for s_ in srcs]

    def body(*refs):
        src_refs, land_refs = refs[:n], refs[n:2 * n]
        send_sems, recv_sems = refs[2 * n + 1], refs[2 * n + 2]
        token = refs[-1]
        for send, _ in _push_copies(src_refs, land_refs, send_sems, recv_sems, blockwise):
            send.start()
        token[...] = jnp.zeros_like(token)

    n_sem = n * (N_DEV - 1)
    lands = [lax.empty((N_DEV,) + b, s_.dtype) for b, s_ in zip(blocks, srcs)]
    out = pl.pallas_call(
        body,
        name=name,
        in_specs=[HBM_SPEC] * (2 * n) + [ANY],
        out_specs=(SEM_SPEC, SEM_SPEC) + (HBM_SPEC,) * (2 * n) + (pl.BlockSpec(memory_space=pltpu.VMEM),),
        out_shape=(pltpu.SemaphoreType.DMA((n_sem,)), pltpu.SemaphoreType.DMA((n_sem,)))
        + tuple(pltpu.HBM(a.shape, a.dtype) for a in list(srcs) + lands)
        + (jax.ShapeDtypeStruct((8, LANE), F32),),
        input_output_aliases={i: 2 + i for i in range(2 * n)},
        compiler_params=SPLIT_COPY_PARAMS,
    )(*[pltpu.with_memory_space_constraint(a, pltpu.HBM) for a in list(srcs) + lands], after)
    return out[0], out[1], out[2:2 + n], out[2 + n:2 + 2 * n], out[-1]


def _push_wait(send_sems, recv_sems, srcs, lands, blockwise, after, *, name):
    n = len(srcs)

    def body(*refs):
        src_refs, land_refs = refs[:n], refs[n:2 * n]
        send_sems, recv_sems = refs[2 * n], refs[2 * n + 1]
        for send, recv in _push_copies(src_refs, land_refs, send_sems, recv_sems, blockwise):
            send.wait_send()
            recv.wait_recv()

    out = pl.pallas_call(
        body,
        name=name,
        in_specs=[HBM_SPEC] * (2 * n) + [SEM_SPEC, SEM_SPEC, ANY],
        out_specs=(HBM_SPEC,) * (2 * n),
        out_shape=tuple(pltpu.HBM(a.shape, a.dtype) for a in list(srcs) + list(lands)),
        input_output_aliases={i: i for i in range(2 * n)},
        compiler_params=SPLIT_COPY_PARAMS,
    )(*srcs, *lands, send_sems, recv_sems, after)
    return out[n:]


def _with_own_slot(landing, own):
    slot = lax.broadcasted_iota(jnp.int32, (N_DEV,) + (1,) * own.ndim, 0)
    return jnp.where(slot == _my_index(), own[None], landing)


def _sum_slots(parts, own=None, *, name):
    shape = parts.shape[1:]
    n, c = parts.shape[0], parts.shape[-1]
    r = parts.size // (n * c)
    tm = _pick(r, (256, 128, 64, 32, 16, 8))

    def body(p_ref, *rest):
        o_ref = rest[-1]
        me = _my_index()

        def slot(s):
            if own is None:
                return p_ref[s].astype(F32)
            return jnp.where(me == s, rest[0][...], p_ref[s]).astype(F32)

        acc = slot(0)
        for s in range(1, n):
            acc = acc + slot(s)
        o_ref[...] = acc

    tile = pl.BlockSpec((tm, c), lambda i: (i, 0))
    return pl.pallas_call(
        body,
        name=name,
        grid=(r // tm,),
        in_specs=[pl.BlockSpec((n, tm, c), lambda i: (0, i, 0))] + ([] if own is None else [tile]),
        out_specs=tile,
        out_shape=jax.ShapeDtypeStruct((r, c), F32),
        compiler_params=_params(("parallel",)),
    )(parts.reshape(n, r, c), *([] if own is None else [own.reshape(r, c)])).reshape(shape)


def _row_count(shape):
    c = shape[-1]
    rows = 1
    for s in shape[:-1]:
        rows *= s
    return rows, c, c + (-c) % LANE


PACK_ROWS = 256


def _pack_rows(arrays):
    pieces = []
    for a in arrays:
        rows, c, cp = _row_count(a.shape)
        a2 = a.reshape(rows, c)
        if cp > c:
            a2 = jnp.pad(a2, ((0, 0), (0, cp - c)))
        a2 = a2.reshape(rows * cp // LANE, LANE)
        if a2.shape[0] % 8:
            a2 = jnp.pad(a2, ((0, 8 - a2.shape[0] % 8), (0, 0)))
        pieces.append(a2)
    total = sum(p.shape[0] for p in pieces)
    if total % PACK_ROWS:
        pieces.append(jnp.zeros((PACK_ROWS - total % PACK_ROWS, LANE), F32))
    return jnp.concatenate(pieces, axis=0)


def _unpack_rows(packed, shapes, lead=()):
    out, off = [], 0
    for shp in shapes:
        rows, c, cp = _row_count(shp)
        n_rows = rows * cp // LANE
        seg = packed[..., off:off + n_rows, :].reshape(lead + (rows, cp))
        out.append(seg[..., :c].reshape(lead + tuple(shp)))
        off += n_rows + (-n_rows) % 8
    return out


def _unshard(stacked, axis):
    moved = jnp.moveaxis(stacked, 0, axis)
    shp = moved.shape
    return moved.reshape(shp[:axis] + (shp[axis] * shp[axis + 1],) + shp[axis + 2:])


def _shard_major(full, axis):
    shp = full.shape
    split = full.reshape(shp[:axis] + (N_DEV, shp[axis] // N_DEV) + shp[axis + 1:])
    return jnp.moveaxis(split, axis, 0)


def _my_shard(full, axis):
    size = full.shape[axis] // N_DEV
    return lax.dynamic_slice_in_dim(full, _my_index() * size, size, axis)


def _local_step(x, target, w, fetch, emit, n_seq):
    def with_token(vec, token):
        return vec + jnp.tile(token[0:1, :], (1, vec.shape[1] // LANE))

    depth, d_model = w["norm_mix_pre"].shape
    d_inner = w["ssd_norm_w"].shape[1]
    d_xbc = w["ssd_conv_w"].shape[2]
    saved = []
    for i in range(depth):
        j = i // 2
        m, token = fetch(i, "mix", x)
        mix_pre_w = with_token(w["norm_mix_pre"][i:i + 1], token)
        s = {"x": x, "mix_pre_w": mix_pre_w}
        if i % 2 == 0:
            u = _rms_fwd(x, mix_pre_w, out_dtype=BF16, name=f"l{i}_mix_pre")
            proj = _mm(u, m["ssd_w_in"], name=f"l{i}_ssd_in")
            xbc, xbc_pre = _ssd_conv_fwd(proj, d_inner, d_xbc, w["ssd_conv_w"][j], w["ssd_conv_b"][j:j + 1], n_seq,
                                         name=f"l{i}_ssd_conv")
            yn, y, hs = _ssd_fwd(proj, xbc, w["ssd_dt_bias"][j:j + 1], w["ssd_a_log"][j:j + 1], w["ssd_d"][j:j + 1],
                                 w["ssd_norm_w"][j:j + 1], n_seq, name=f"l{i}_ssd_scan")
            mix = _mm(yn, m["ssd_w_out"], name=f"l{i}_ssd_out")
            s.update(u=u, proj=proj, xbc=xbc, xbc_pre=xbc_pre, yn=yn, y=y, hs=hs)
        else:
            u = _rms_fwd(x, mix_pre_w, out_dtype=F32, name=f"l{i}_mix_pre")
            mix = _pool_fwd(u, m["pool_w"], w["pool_scale"][j:j + 1], n_seq, name=f"l{i}_pool")
            s.update(u=u)
        x1 = _res_rms_fwd(x, mix, w["norm_mix_post"][i:i + 1], name=f"l{i}_mix_post")
        m_ffn, token = fetch(i, "ffn", x1)
        m = {**m, **m_ffn}
        ffn_pre_w = with_token(w["norm_ffn_pre"][i:i + 1], token)
        n = _rms_fwd(x1, ffn_pre_w, out_dtype=BF16, name=f"l{i}_ffn_pre")
        h = _mm(n, m["ffn_w_up"], out_dtype=BF16, name=f"l{i}_ffn_up")
        a, hc = _ffn_act_fwd(h, w["ffn_conv_w"][i], w["ffn_conv_b"][i:i + 1], n_seq, name=f"l{i}_ffn_act")
        f = _mm(a, m["ffn_w_down"], name=f"l{i}_ffn_down")
        x = _res_rms_fwd(x1, f, w["norm_ffn_post"][i:i + 1], name=f"l{i}_ffn_post")
        s.update(mix=mix, x1=x1, n=n, h=h, hc=hc, a=a, f=f, m=m, ffn_pre_w=ffn_pre_w)
        saved.append(s)

    loss, dx = _loss_head(x, target)
    grads = {k: [None] * len(w[k]) for k in SMALL}
    token = jnp.zeros((8, LANE), F32)
    for i in reversed(range(depth)):
        j = i // 2
        s = saved[i]
        m, gm = s["m"], {}
        df, grads["norm_ffn_post"][i] = _rms_bwd(s["f"], with_token(w["norm_ffn_post"][i:i + 1], token), dx, None,
                                                 out_dtype=BF16, name=f"l{i}_ffn_post_b")
        da = _mm(df, m["ffn_w_down"], tb=True, out_dtype=BF16, name=f"l{i}_ffn_down_bx")
        gm["ffn_w_down"] = _mm(s["a"], df, ta=True, name=f"l{i}_ffn_down_bw")
        dh, grads["ffn_conv_w"][i], grads["ffn_conv_b"][i] = _ffn_act_bwd(
            s["h"], s["hc"], w["ffn_conv_w"][i], da, n_seq, name=f"l{i}_ffn_act_b")
        dn = _mm(dh, m["ffn_w_up"], tb=True, name=f"l{i}_ffn_up_bx")
        gm["ffn_w_up"] = _mm(s["n"], dh, ta=True, name=f"l{i}_ffn_up_bw")
        dx1, grads["norm_ffn_pre"][i] = _rms_bwd(s["x1"], s["ffn_pre_w"], dn, dx, name=f"l{i}_ffn_pre_b")
        token = emit(i, "ffn", gm, dx1)
        gm = {}
        dmix, grads["norm_mix_post"][i] = _rms_bwd(s["mix"], with_token(w["norm_mix_post"][i:i + 1], token), dx1, None,
                                                   out_dtype=BF16 if i % 2 == 0 else F32, name=f"l{i}_mix_post_b")
        if i % 2 == 0:
            dyn = _mm(dmix, m["ssd_w_out"], tb=True, name=f"l{i}_ssd_out_bx")
            gm["ssd_w_out"] = _mm(s["yn"], dmix, ta=True, name=f"l{i}_ssd_out_bw")
            dxs, db, dc, dz, ddtr, dnw, dbias, dalog, ddsk = _ssd_bwd(
                s["proj"], s["xbc"], s["hs"], s["y"], dyn, w["ssd_dt_bias"][j:j + 1], w["ssd_a_log"][j:j + 1],
                w["ssd_d"][j:j + 1], w["ssd_norm_w"][j:j + 1], n_seq, name=f"l{i}_ssd_scan_b")
            grads["ssd_norm_w"][j], grads["ssd_dt_bias"][j], grads["ssd_a_log"][j], grads["ssd_d"][j] = (
                dnw, dbias, dalog, ddsk)
            dproj, grads["ssd_conv_w"][j], grads["ssd_conv_b"][j] = _ssd_conv_bwd(
                s["proj"], d_inner, w["ssd_conv_w"][j], s["xbc_pre"], (dxs, db, dc), dz, n_seq,
                name=f"l{i}_ssd_conv_b")
            dproj = _fill_cols(dproj, ddtr, d_inner + d_xbc, name=f"l{i}_ssd_dt_b")
            du = _mm(dproj, m["ssd_w_in"], tb=True, name=f"l{i}_ssd_in_bx")
            gm["ssd_w_in"] = _mm(s["u"], dproj, ta=True, name=f"l{i}_ssd_in_bw")
        else:
            du, gm["pool_w"], grads["pool_scale"][j] = _pool_bwd(
                s["u"], m["pool_w"], w["pool_scale"][j:j + 1], dmix, n_seq, name=f"l{i}_pool_b")
        dx, grads["norm_mix_pre"][i] = _rms_bwd(s["x"], s["mix_pre_w"], du, dx1, name=f"l{i}_mix_pre_b")
        token = emit(i, "mix", gm, dx)
    return loss, dx, grads


BIG = (("ssd_w_in", 2), ("ssd_w_out", 1), ("pool_w", 2), ("ffn_w_up", 2), ("ffn_w_down", 1))
SMALL_SHARDED = (("ssd_conv_w", 2), ("ffn_conv_w", 2), ("pool_scale", 1))
SMALL = ("ssd_conv_w", "ssd_conv_b", "ssd_dt_bias", "ssd_a_log", "ssd_d", "ssd_norm_w", "pool_scale", "ffn_conv_w",
         "ffn_conv_b", "norm_mix_pre", "norm_mix_post", "norm_ffn_pre", "norm_ffn_post")
WEIGHTS = ("ssd_w_in", "ssd_conv_w", "ssd_conv_b", "ssd_dt_bias", "ssd_a_log", "ssd_d", "ssd_norm_w", "ssd_w_out",
           "pool_w", "pool_scale", "ffn_w_up", "ffn_conv_w", "ffn_conv_b", "ffn_w_down", "norm_mix_pre",
           "norm_mix_post", "norm_ffn_pre", "norm_ffn_post")


def _ssd_sizes(d_inner):
    return d_inner + 2 * N_SSD_GROUPS * D_STATE, d_inner // HEAD_DIM // N_SSD_GROUPS


def _small_compute_layout(full, d_inner):
    _, r_heads = _ssd_sizes(d_inner)
    w = {k: full[k] for k in SMALL}
    for k in ("ssd_dt_bias", "ssd_a_log", "ssd_d"):
        w[k] = _head_pad(full[k], r_heads)
    for k in ("ffn_conv_w", "ffn_conv_b"):
        w[k] = _interleave(full[k])
    return w


def _matmul_compute_layout(k, full, d_inner):
    d_xbc, r_heads = _ssd_sizes(d_inner)
    if k == "ssd_w_in":
        return _ssd_w_in_layout(full, d_inner, d_xbc, r_heads)
    if k == "ffn_w_up":
        return _interleave(full)
    return full


def _layer_matrices(i, part):
    if part == "ffn":
        return (("ffn_w_up", 1, i), ("ffn_w_down", 0, i))
    return (("ssd_w_in", 1, i // 2), ("ssd_w_out", 0, i // 2)) if i % 2 == 0 else (("pool_w", 1, i // 2),)


def _fetch_group(i, part):
    if i % 2 == 1:
        return _layer_matrices(i, "mix") + _layer_matrices(i, "ffn") if part == "mix" else ()
    return _layer_matrices(i, part)


def _matmul_grad_reference_layout(k, g, d_inner):
    d_xbc, r_heads = _ssd_sizes(d_inner)
    if k == "ssd_w_in":
        return _ssd_w_in_unlayout(g, d_inner, d_xbc, r_heads)
    if k == "ffn_w_up":
        return _deinterleave(g)
    return g


def _small_grads_reference_layout(grads, shapes, d_inner):
    _, r_heads = _ssd_sizes(d_inner)
    g = {k: jnp.stack(grads[k]) for k in SMALL}
    for k in ("ssd_dt_bias", "ssd_a_log", "ssd_d"):
        g[k] = _head_unpad(g[k][:, 0], r_heads)
    for k in ("ffn_conv_w", "ffn_conv_b"):
        g[k] = _deinterleave(g[k])
    return {k: v.reshape(shapes[k]) for k, v in g.items()}


def kernel(x, ssd_w_in, ssd_conv_w, ssd_conv_b, ssd_dt_bias, ssd_a_log, ssd_d, ssd_norm_w, ssd_w_out, pool_w, pool_scale, ffn_w_up, ffn_conv_w, ffn_conv_b, ffn_w_down, norm_mix_pre, norm_mix_post, norm_ffn_pre, norm_ffn_post, loss_target, m_ssd_w_in, m_ssd_conv_w, m_ssd_conv_b, m_ssd_dt_bias, m_ssd_a_log, m_ssd_d, m_ssd_norm_w, m_ssd_w_out, m_pool_w, m_pool_scale, m_ffn_w_up, m_ffn_conv_w, m_ffn_conv_b, m_ffn_w_down, m_norm_mix_pre, m_norm_mix_post, m_norm_ffn_pre, m_norm_ffn_post, v_ssd_w_in, v_ssd_conv_w, v_ssd_conv_b, v_ssd_dt_bias, v_ssd_a_log, v_ssd_d, v_ssd_norm_w, v_ssd_w_out, v_pool_w, v_pool_scale, v_ffn_w_up, v_ffn_conv_w, v_ffn_conv_b, v_ffn_w_down, v_norm_mix_pre, v_norm_mix_post, v_norm_ffn_pre, v_norm_ffn_post):
    shards = dict(ssd_w_in=ssd_w_in, ssd_conv_w=ssd_conv_w, ssd_conv_b=ssd_conv_b, ssd_dt_bias=ssd_dt_bias,
                  ssd_a_log=ssd_a_log, ssd_d=ssd_d, ssd_norm_w=ssd_norm_w, ssd_w_out=ssd_w_out, pool_w=pool_w,
                  pool_scale=pool_scale, ffn_w_up=ffn_w_up, ffn_conv_w=ffn_conv_w, ffn_conv_b=ffn_conv_b,
                  ffn_w_down=ffn_w_down, norm_mix_pre=norm_mix_pre, norm_mix_post=norm_mix_post,
                  norm_ffn_pre=norm_ffn_pre, norm_ffn_post=norm_ffn_post)
    moments_m = dict(zip(WEIGHTS, (m_ssd_w_in, m_ssd_conv_w, m_ssd_conv_b, m_ssd_dt_bias, m_ssd_a_log, m_ssd_d, m_ssd_norm_w, m_ssd_w_out, m_pool_w, m_pool_scale, m_ffn_w_up, m_ffn_conv_w, m_ffn_conv_b, m_ffn_w_down, m_norm_mix_pre, m_norm_mix_post, m_norm_ffn_pre, m_norm_ffn_post)))
    moments_v = dict(zip(WEIGHTS, (v_ssd_w_in, v_ssd_conv_w, v_ssd_conv_b, v_ssd_dt_bias, v_ssd_a_log, v_ssd_d, v_ssd_norm_w, v_ssd_w_out, v_pool_w, v_pool_scale, v_ffn_w_up, v_ffn_conv_w, v_ffn_conv_b, v_ffn_w_down, v_norm_mix_pre, v_norm_mix_post, v_norm_ffn_pre, v_norm_ffn_post)))
    n_seq, seq, d_model = x.shape
    t = n_seq * seq

    full = dict(shards)
    small_all = _all_gather(_pack_rows([shards[k] for k, _ in SMALL_SHARDED]), name="gather_small_weights")
    small_stacked = _unpack_rows(small_all, [shards[k].shape for k, _ in SMALL_SHARDED], lead=(N_DEV,))
    for (k, axis), st in zip(SMALL_SHARDED, small_stacked):
        full[k] = _unshard(st, axis)
    d_inner = ssd_norm_w.shape[1]
    w = _small_compute_layout(full, d_inner)
    depth = norm_mix_pre.shape[0]
    x2 = x.reshape(t, d_model)

    shard16 = {k: shards[k].astype(BF16) for k, _ in BIG}
    order = [(i, part) for i in range(depth) for part in ("mix", "ffn") if _fetch_group(i, part)]
    fetches = {}

    def start_fetch(key, after):
        srcs = [shard16[k][l] for k, _, l in _fetch_group(*key)]
        fetches[key] = _push_start(srcs, False, after, name=f"fetch{key[0]}{key[1]}_start")

    def fetch(i, part, x_now):
        key = (i, part)
        if key not in order:
            return {}, jnp.zeros((8, LANE), F32)
        if key == order[0]:
            start_fetch(key, x_now)
        send, recv, srcs, lands, _ = fetches[key]
        lands = _push_wait(send, recv, srcs, lands, False, x_now, name=f"fetch{i}{part}_wait")
        mats = {}
        for (k, axis, l), land in zip(_fetch_group(i, part), lands):
            whole = _unshard(_with_own_slot(land, shard16[k][l]), axis)
            mats[k] = _matmul_compute_layout(k, whole, d_inner)
        nxt = order.index(key) + 1
        if nxt < len(order):
            start_fetch(order[nxt], lands[0])
            return mats, fetches[order[nxt]][4]
        return mats, jnp.zeros((8, LANE), F32)

    g_layers = {}
    in_flight = []

    def finish_exchange(after):
        key, blocks, (send, recv, srcs, lands, _) = in_flight.pop()
        lands = _push_wait(send, recv, srcs, lands, True, after, name=f"exchange{key[0]}{key[1]}_wait")
        for (k, _, l), land, block in zip(_layer_matrices(*key), lands, blocks):
            own = lax.dynamic_index_in_dim(block, _my_index(), 0, keepdims=False)
            g_layers[k, l] = _sum_slots(land, own, name=f"sum{key[0]}_{k}")

    def emit(i, part, gm, dx_now):
        if in_flight:
            finish_exchange(dx_now)
        blocks = [_shard_major(_matmul_grad_reference_layout(k, gm[k].astype(BF16), d_inner), axis)
                  for k, axis, _ in _layer_matrices(i, part)]
        started = _push_start(blocks, True, dx_now, name=f"exchange{i}{part}_start")
        in_flight.append(((i, part), blocks, started))
        return started[4]

    loss, dx, grads = _local_step(x2, loss_target.reshape(t, d_model), w, fetch, emit, n_seq)
    finish_exchange(dx)
    loss = lax.psum(loss, ("x", "y", "c"))
    g_shard = {k: jnp.stack([g_layers[k, l] for l in range(shards[k].shape[0])]) for k, _ in BIG}

    small_shapes = {k: full[k].shape for k in SMALL}
    g_small = _small_grads_reference_layout(grads, small_shapes, d_inner)
    s_all = _all_gather(_pack_rows([g_small[k] for k in SMALL]), name="gather_small_grads")
    for k, g in zip(SMALL, _unpack_rows(_sum_slots(s_all, name="sum_small_grads"), [small_shapes[k] for k in SMALL])):
        g_shard[k] = g
    for k, axis in SMALL_SHARDED:
        g_shard[k] = _my_shard(g_shard[k], axis)

    deltas, new_m, new_v = {}, {}, {}
    for k in WEIGHTS:
        deltas[k], new_m[k], new_v[k] = _adamw(shards[k], g_shard[k], moments_m[k], moments_v[k], name=f"adamw_{k}")
    return (loss, dx.reshape(x.shape), *[g_shard[k] for k in WEIGHTS], *[deltas[k] for k in WEIGHTS],
            *[new_m[k] for k in WEIGHTS], *[new_v[k] for k in WEIGHTS])
```

```python
import functools

import jax
import jax.numpy as jnp
from jax import lax
from jax.experimental import pallas as pl
from jax.experimental.pallas import tpu as pltpu

F32 = jnp.float32
BF16 = jnp.bfloat16

N_DEV = 8
HEAD_DIM = 64
N_SSD_GROUPS = 4
D_STATE = 128
CHUNK = 128
POOL_WINDOWS = (2, 4, 8, 16)
EPS = 1e-6
LANE = 128
ADAM_LR = 0.001
ADAM_B1 = 0.9
ADAM_B2 = 0.999
ADAM_EPS = 1e-08
ADAM_WD = 0.01
ADAM_STEP = 10
VMEM_LIMIT = 56 * 1024 * 1024
ANY = pl.BlockSpec(memory_space=pl.ANY)


def _pick(n, cands):
    for c in cands:
        if n % c == 0:
            return c
    return n


def _params(sem):
    return pltpu.CompilerParams(dimension_semantics=sem, vmem_limit_bytes=VMEM_LIMIT)


def _sigmoid(x):
    return 0.5 * jnp.tanh(0.5 * x) + 0.5


def _silu(x):
    return x * _sigmoid(x)


def _dsilu(x):
    s = _sigmoid(x)
    return s * (1.0 + x * (1.0 - s))


def _shift_down(x, s):
    rows = lax.broadcasted_iota(jnp.int32, x.shape, 0)
    return jnp.where(rows >= s, pltpu.roll(x, s, 0), 0.0)


def _shift_up(x, s):
    n = x.shape[0]
    rows = lax.broadcasted_iota(jnp.int32, x.shape, 0)
    return jnp.where(rows < n - s, pltpu.roll(x, n - s, 0), 0.0)


MM_VMEM_BUDGET = 40 * 1024 * 1024
MM_STEP_BYTES = 1_300_000
MM_SUB = 512


def _mm_tiles(m, n, k, a_bytes, b_bytes, o_bytes):
    def cands(dim, sizes):
        out = [s for s in sizes if s <= dim and dim % s == 0]
        return out or [dim]

    best = None
    for tm in cands(m, (m, m // 2, 2048, 1024, 512, 256, 128)):
        if tm % LANE:
            continue
        for tn in cands(n, (n, n // 2, n // 4, 2048, 1024, 512, 256, 128)):
            if tn % (2 * LANE) and tn != n:
                continue
            for tk in cands(k, (k, k // 2, 2048, 1024, 512)):
                if tk % LANE:
                    continue
                nk = k // tk
                acc = tm * tn * 4 if (nk > 1 and o_bytes != 4) else 0
                temps = tm * min(tn, MM_SUB) * 4 + (tm * tk * 2 if a_bytes == 4 else 0) + (tk * tn * 2 if b_bytes == 4 else 0)
                vmem = 2 * (tm * tk * a_bytes + tk * tn * b_bytes + tm * tn * o_bytes) + acc + temps
                if vmem > MM_VMEM_BUDGET:
                    continue
                steps = (m // tm) * (n // tn) * nk
                cost = (m * k * a_bytes * (n // tn) + k * n * b_bytes * (m // tm) + m * n * o_bytes
                        + steps * MM_STEP_BYTES)
                if best is None or cost < best[0]:
                    best = (cost, tm, tn, tk)
    return best[1:]


def _mm(a, b, *, ta=False, tb=False, out_dtype=F32, name="mm"):
    m, k = (a.shape[1], a.shape[0]) if ta else a.shape
    n = b.shape[0] if tb else b.shape[1]
    o_bytes = jnp.dtype(out_dtype).itemsize
    tm, tn, tk = _mm_tiles(m, n, k, a.dtype.itemsize, b.dtype.itemsize, o_bytes)
    nk = k // tk
    sub = _pick(tn, (MM_SUB, 256))
    use_acc = nk > 1 and o_bytes != 4
    a_spec = pl.BlockSpec((tk, tm), lambda i, j, kk: (kk, i)) if ta else pl.BlockSpec((tm, tk), lambda i, j, kk: (i, kk))
    b_spec = pl.BlockSpec((tn, tk), lambda i, j, kk: (j, kk)) if tb else pl.BlockSpec((tk, tn), lambda i, j, kk: (kk, j))
    dims = (((1,), (1 if tb else 0,)), ((), ()))

    def body(a_ref, b_ref, o_ref, *scratch):
        kk = pl.program_id(2)
        acc_ref = scratch[0] if use_acc else o_ref
        if nk > 1:
            @pl.when(kk == 0)
            def _():
                acc_ref[...] = jnp.zeros_like(acc_ref)

        av = a_ref[...].astype(BF16)
        if ta:
            av = av.T
        for s in range(tn // sub):
            cols = slice(s * sub, (s + 1) * sub)
            bv = (b_ref[cols, :] if tb else b_ref[:, cols]).astype(BF16)
            part = lax.dot_general(av, bv, dims, preferred_element_type=F32)
            if nk == 1:
                o_ref[:, cols] = part.astype(out_dtype)
            else:
                acc_ref[:, cols] += part
        if use_acc:
            @pl.when(kk == nk - 1)
            def _():
                o_ref[...] = acc_ref[...].astype(out_dtype)

    return pl.pallas_call(
        body,
        name=name,
        grid=(m // tm, n // tn, nk),
        in_specs=[a_spec, b_spec],
        out_specs=pl.BlockSpec((tm, tn), lambda i, j, kk: (i, j)),
        out_shape=jax.ShapeDtypeStruct((m, n), out_dtype),
        scratch_shapes=[pltpu.VMEM((tm, tn), F32)] if use_acc else [],
        compiler_params=_params(("parallel", "parallel", "arbitrary")),
    )(a, b)


def _rms_fwd(x, w, *, out_dtype, name):
    t, d = x.shape
    tm = _pick(t, (512, 256, 128))

    def body(x_ref, w_ref, o_ref):
        xv = x_ref[...]
        rstd = lax.rsqrt(jnp.mean(xv * xv, axis=-1, keepdims=True) + EPS)
        o_ref[...] = (xv * rstd * w_ref[...]).astype(out_dtype)

    return pl.pallas_call(
        body,
        name=name,
        grid=(t // tm,),
        in_specs=[pl.BlockSpec((tm, d), lambda i: (i, 0)), pl.BlockSpec((1, d), lambda i: (0, 0))],
        out_specs=pl.BlockSpec((tm, d), lambda i: (i, 0)),
        out_shape=jax.ShapeDtypeStruct((t, d), out_dtype),
        compiler_params=_params(("parallel",)),
    )(x, w)


def _res_rms_fwd(x, f, w, *, name):
    t, d = x.shape
    tm = _pick(t, (512, 256, 128))

    def body(x_ref, f_ref, w_ref, o_ref):
        fv = f_ref[...]
        rstd = lax.rsqrt(jnp.mean(fv * fv, axis=-1, keepdims=True) + EPS)
        o_ref[...] = x_ref[...] + fv * rstd * w_ref[...]

    row = pl.BlockSpec((tm, d), lambda i: (i, 0))
    return pl.pallas_call(
        body,
        name=name,
        grid=(t // tm,),
        in_specs=[row, row, pl.BlockSpec((1, d), lambda i: (0, 0))],
        out_specs=row,
        out_shape=jax.ShapeDtypeStruct((t, d), F32),
        compiler_params=_params(("parallel",)),
    )(x, f, w)


def _rms_bwd(x, w, dy, resid, *, out_dtype=F32, name):
    t, d = x.shape
    tm = _pick(t, (512, 256, 128))
    has_res = resid is not None

    def body(*refs):
        if has_res:
            x_ref, w_ref, dy_ref, r_ref, dx_ref, dw_ref = refs
        else:
            x_ref, w_ref, dy_ref, dx_ref, dw_ref = refs
        xv = x_ref[...]
        dyv = dy_ref[...].astype(F32)
        rstd = lax.rsqrt(jnp.mean(xv * xv, axis=-1, keepdims=True) + EPS)
        xn = xv * rstd
        g = dyv * w_ref[...]
        dx = rstd * (g - xn * jnp.mean(g * xn, axis=-1, keepdims=True))
        if has_res:
            dx = dx + r_ref[...]
        dx_ref[...] = dx.astype(out_dtype)
        part = jnp.sum(dyv * xn, axis=0, keepdims=True)

        @pl.when(pl.program_id(0) == 0)
        def _():
            dw_ref[...] = part

        @pl.when(pl.program_id(0) > 0)
        def _():
            dw_ref[...] += part

    row = pl.BlockSpec((tm, d), lambda i: (i, 0))
    vec = pl.BlockSpec((1, d), lambda i: (0, 0))
    ins = [x, w, dy] + ([resid] if has_res else [])
    return pl.pallas_call(
        body,
        name=name,
        grid=(t // tm,),
        in_specs=[row, vec, row] + ([row] if has_res else []),
        out_specs=[row, vec],
        out_shape=[jax.ShapeDtypeStruct((t, d), out_dtype), jax.ShapeDtypeStruct((1, d), F32)],
        compiler_params=_params(("arbitrary",)),
    )(*ins)


def _loss_head(y, target, *, name="loss_head"):
    t, d = y.shape
    tm = _pick(t, (512, 256, 128))

    def body(y_ref, t_ref, dy_ref, l_ref):
        err = y_ref[...] - t_ref[...]
        dy_ref[...] = err * (1.0 / d)
        part = jnp.sum(jnp.sum(err * err, axis=-1, keepdims=True), axis=0, keepdims=True) * (0.5 / d)
        part = jnp.broadcast_to(part, (1, LANE))

        @pl.when(pl.program_id(0) == 0)
        def _():
            l_ref[...] = part

        @pl.when(pl.program_id(0) > 0)
        def _():
            l_ref[...] += part

    row = pl.BlockSpec((tm, d), lambda i: (i, 0))
    dy, l = pl.pallas_call(
        body,
        name=name,
        grid=(t // tm,),
        in_specs=[row, row],
        out_specs=[row, pl.BlockSpec((1, LANE), lambda i: (0, 0))],
        out_shape=[jax.ShapeDtypeStruct((t, d), F32), jax.ShapeDtypeStruct((1, LANE), F32)],
        compiler_params=_params(("arbitrary",)),
    )(y, target)
    return l[0, 0], dy


def _conv_taps(h, w_ref, k_taps):
    out = h * w_ref[k_taps - 1:k_taps, :]
    for k in range(k_taps - 1):
        out = out + _shift_down(h, k_taps - 1 - k) * w_ref[k:k + 1, :]
    return out


def _conv_taps_bwd(h, dhc, w_ref, k_taps):
    dh = dhc * w_ref[k_taps - 1:k_taps, :]
    dws = []
    for k in range(k_taps - 1):
        up = _shift_up(dhc, k_taps - 1 - k)
        dh = dh + up * w_ref[k:k + 1, :]
        dws.append(jnp.sum(up * h, axis=0, keepdims=True))
    dws.append(jnp.sum(dhc * h, axis=0, keepdims=True))
    return dh, jnp.concatenate(dws, axis=0)


FFN_TC = 256


def _interleave(w, tc=FFN_TC):
    f = w.shape[-1] // 2
    lead = w.shape[:-1]
    return jnp.swapaxes(w.reshape(lead + (2, f // tc, tc)), -3, -2).reshape(lead + (2 * f,))


def _deinterleave(w, tc=FFN_TC):
    f = w.shape[-1] // 2
    lead = w.shape[:-1]
    return jnp.swapaxes(w.reshape(lead + (f // tc, 2, tc)), -3, -2).reshape(lead + (2 * f,))


def _ffn_act_fwd(h, conv_w, conv_b, n_seq, *, name):
    t, f2 = h.shape
    seq = t // n_seq
    tc = FFN_TC
    nj = f2 // (2 * tc)
    k_taps = conv_w.shape[0]

    def body(h_ref, w_ref, b_ref, o_ref, hc_ref):
        hc = _conv_taps(h_ref[...].astype(F32), w_ref, k_taps) + b_ref[...]
        hc_ref[...] = hc.astype(BF16)
        o_ref[...] = (_silu(hc[:, :tc]) * hc[:, tc:]).astype(BF16)

    return pl.pallas_call(
        body,
        name=name,
        grid=(n_seq, nj),
        in_specs=[
            pl.BlockSpec((seq, 2 * tc), lambda b, j: (b, j)),
            pl.BlockSpec((k_taps, 2 * tc), lambda b, j: (0, j)),
            pl.BlockSpec((1, 2 * tc), lambda b, j: (0, j)),
        ],
        out_specs=[pl.BlockSpec((seq, tc), lambda b, j: (b, j)), pl.BlockSpec((seq, 2 * tc), lambda b, j: (b, j))],
        out_shape=[jax.ShapeDtypeStruct((t, f2 // 2), BF16), jax.ShapeDtypeStruct((t, f2), BF16)],
        compiler_params=_params(("parallel", "parallel")),
    )(h, conv_w, conv_b)


def _ffn_act_bwd(h, hc, conv_w, da, n_seq, *, name):
    t, f2 = h.shape
    seq = t // n_seq
    tc = FFN_TC
    nj = f2 // (2 * tc)
    k_taps = conv_w.shape[0]

    def body(h_ref, hc_ref, w_ref, da_ref, dh_ref, dw_ref, db_ref):
        hcv = hc_ref[...].astype(F32)
        gate, val = hcv[:, :tc], hcv[:, tc:]
        dav = da_ref[...].astype(F32)
        dhc = jnp.concatenate([dav * val * _dsilu(gate), dav * _silu(gate)], axis=1)
        dh, dw = _conv_taps_bwd(h_ref[...].astype(F32), dhc, w_ref, k_taps)
        dh_ref[...] = dh.astype(BF16)
        db = jnp.sum(dhc, axis=0, keepdims=True)

        @pl.when(pl.program_id(1) == 0)
        def _():
            dw_ref[...] = dw
            db_ref[...] = db

        @pl.when(pl.program_id(1) > 0)
        def _():
            dw_ref[...] += dw
            db_ref[...] += db

    wide = pl.BlockSpec((seq, 2 * tc), lambda j, b: (b, j))
    return pl.pallas_call(
        body,
        name=name,
        grid=(nj, n_seq),
        in_specs=[wide, wide, pl.BlockSpec((k_taps, 2 * tc), lambda j, b: (0, j)),
                  pl.BlockSpec((seq, tc), lambda j, b: (b, j))],
        out_specs=[
            wide,
            pl.BlockSpec((k_taps, 2 * tc), lambda j, b: (0, j)),
            pl.BlockSpec((1, 2 * tc), lambda j, b: (0, j)),
        ],
        out_shape=[
            jax.ShapeDtypeStruct((t, f2), BF16),
            jax.ShapeDtypeStruct((k_taps, f2), F32),
            jax.ShapeDtypeStruct((1, f2), F32),
        ],
        compiler_params=_params(("parallel", "arbitrary")),
    )(h, hc, conv_w, da)


def _window_mixed(u, window):
    s = u
    step = 1
    while step < window:
        s = s + _shift_down(s, step)
        step *= 2
    rows = lax.broadcasted_iota(jnp.int32, u.shape, 0)
    inv_cnt = 1.0 / jnp.minimum(rows + 1, window).astype(F32)
    return s * inv_cnt - u, inv_cnt


def _window_mixed_bwd(dmixed, inv_cnt, window):
    r = dmixed * inv_cnt
    s = r
    step = 1
    while step < window:
        s = s + _shift_up(s, step)
        step *= 2
    return s - dmixed


def _pool_fwd(u, w, scale, n_seq, *, name):
    t, d = u.shape
    seq = t // n_seq
    n_g, dg, _ = w.shape

    def body(u_ref, w_ref, s_ref, o_ref):
        for k, window in enumerate(POOL_WINDOWS):
            @pl.when(pl.program_id(1) == k)
            def _(window=window):
                mixed, _ = _window_mixed(u_ref[...], window)
                pre = jnp.dot(mixed.astype(BF16), w_ref[0].astype(BF16), preferred_element_type=F32)
                o_ref[...] = pre * s_ref[...]

    return pl.pallas_call(
        body,
        name=name,
        grid=(n_seq, n_g),
        in_specs=[
            pl.BlockSpec((seq, dg), lambda b, g: (b, g)),
            pl.BlockSpec((1, dg, dg), lambda b, g: (g, 0, 0)),
            pl.BlockSpec((1, dg), lambda b, g: (0, g)),
        ],
        out_specs=pl.BlockSpec((seq, dg), lambda b, g: (b, g)),
        out_shape=jax.ShapeDtypeStruct((t, d), F32),
        compiler_params=_params(("parallel", "parallel")),
    )(u, w, scale)


def _pool_bwd(u, w, scale, dout, n_seq, *, name):
    t, d = u.shape
    seq = t // n_seq
    n_g, dg, _ = w.shape

    def body(u_ref, w_ref, s_ref, do_ref, du_ref, dw_ref, ds_ref):
        group = pl.program_id(0)
        first = pl.program_id(1) == 0
        for k, window in enumerate(POOL_WINDOWS):
            @pl.when(group == k)
            def _(window=window):
                mixed, inv_cnt = _window_mixed(u_ref[...], window)
                mixed_b = mixed.astype(BF16)
                w_b = w_ref[0].astype(BF16)
                dov = do_ref[...]
                pre = jnp.dot(mixed_b, w_b, preferred_element_type=F32)
                dsc = jnp.sum(dov * pre, axis=0, keepdims=True)
                dpre = (dov * s_ref[...]).astype(BF16)
                dw = lax.dot_general(mixed_b, dpre, (((0,), (0,)), ((), ())), preferred_element_type=F32)
                dmixed = lax.dot_general(dpre, w_b, (((1,), (1,)), ((), ())), preferred_element_type=F32)
                du_ref[...] = _window_mixed_bwd(dmixed, inv_cnt, window)

                @pl.when(first)
                def _():
                    dw_ref[0] = dw
                    ds_ref[...] = dsc

                @pl.when(jnp.logical_not(first))
                def _():
                    dw_ref[0] += dw
                    ds_ref[...] += dsc

    return pl.pallas_call(
        body,
        name=name,
        grid=(n_g, n_seq),
        in_specs=[
            pl.BlockSpec((seq, dg), lambda g, b: (b, g)),
            pl.BlockSpec((1, dg, dg), lambda g, b: (g, 0, 0)),
            pl.BlockSpec((1, dg), lambda g, b: (0, g)),
            pl.BlockSpec((seq, dg), lambda g, b: (b, g)),
        ],
        out_specs=[
            pl.BlockSpec((seq, dg), lambda g, b: (b, g)),
            pl.BlockSpec((1, dg, dg), lambda g, b: (g, 0, 0)),
            pl.BlockSpec((1, dg), lambda g, b: (0, g)),
        ],
        out_shape=[
            jax.ShapeDtypeStruct((t, d), F32),
            jax.ShapeDtypeStruct((n_g, dg, dg), F32),
            jax.ShapeDtypeStruct((1, d), F32),
        ],
        compiler_params=_params(("parallel", "arbitrary")),
    )(u, w, scale, dout)


def _adamw(w, g, m, v, *, name):
    shape = w.shape
    c = shape[-1]
    r = w.size // c
    tm = _pick(r, (512, 256, 128, 64, 32, 16, 8))

    def body(w_ref, g_ref, m_ref, v_ref, d_ref, nm_ref, nv_ref):
        gv = g_ref[...]
        nm = ADAM_B1 * m_ref[...] + (1.0 - ADAM_B1) * gv
        nv = ADAM_B2 * v_ref[...] + (1.0 - ADAM_B2) * (gv * gv)
        m_hat = nm / (1.0 - ADAM_B1 ** ADAM_STEP)
        v_hat = nv / (1.0 - ADAM_B2 ** ADAM_STEP)
        d_ref[...] = -ADAM_LR * (m_hat / (jnp.sqrt(v_hat) + ADAM_EPS) + ADAM_WD * w_ref[...])
        nm_ref[...] = nm
        nv_ref[...] = nv

    blk = pl.BlockSpec((tm, c), lambda i: (i, 0))
    out = jax.ShapeDtypeStruct((r, c), F32)
    res = pl.pallas_call(
        body,
        name=name,
        grid=(r // tm,),
        in_specs=[blk] * 4,
        out_specs=[blk] * 3,
        out_shape=[out] * 3,
        compiler_params=_params(("parallel",)),
    )(w.reshape(r, c), g.reshape(r, c), m.reshape(r, c), v.reshape(r, c))
    return tuple(a.reshape(shape) for a in res)


CONV_TC = 256


def _ssd_conv_fwd(proj, col0, n_cols, conv_w, conv_b, n_seq, *, name):
    t = proj.shape[0]
    seq = t // n_seq
    tc = CONV_TC
    off = col0 // tc
    k_taps = conv_w.shape[0]

    def body(h_ref, w_ref, b_ref, o_ref, pre_ref):
        pre = _conv_taps(h_ref[...], w_ref, k_taps) + b_ref[...]
        pre_ref[...] = pre.astype(BF16)
        o_ref[...] = _silu(pre)

    return pl.pallas_call(
        body,
        name=name,
        grid=(n_seq, n_cols // tc),
        in_specs=[
            pl.BlockSpec((seq, tc), lambda b, j: (b, j + off)),
            pl.BlockSpec((k_taps, tc), lambda b, j: (0, j)),
            pl.BlockSpec((1, tc), lambda b, j: (0, j)),
        ],
        out_specs=[pl.BlockSpec((seq, tc), lambda b, j: (b, j))] * 2,
        out_shape=[jax.ShapeDtypeStruct((t, n_cols), F32), jax.ShapeDtypeStruct((t, n_cols), BF16)],
        compiler_params=_params(("parallel", "parallel")),
    )(proj, conv_w, conv_b)


def _ssd_conv_bwd(proj, col0, conv_w, pre, dparts, dproj, n_seq, *, name):
    t = proj.shape[0]
    seq = t // n_seq
    tc = CONV_TC
    off = col0 // tc
    k_taps = conv_w.shape[0]
    widths = [d.shape[1] // tc for d in dparts]
    starts = [sum(widths[:i]) for i in range(len(widths))]
    n_blocks = sum(widths)
    n_parts = len(dparts)

    def body(h_ref, w_ref, pre_ref, *rest):
        part_refs = rest[:n_parts]
        dh_ref, dw_ref, db_ref = rest[n_parts + 1:]
        j = pl.program_id(0)
        da = part_refs[-1][...]
        for i in reversed(range(n_parts - 1)):
            da = jnp.where(j < starts[i + 1], part_refs[i][...], da)
        dhc = da * _dsilu(pre_ref[...].astype(F32))
        dh, dw = _conv_taps_bwd(h_ref[...], dhc, w_ref, k_taps)
        dh_ref[...] = dh.astype(BF16)
        db = jnp.sum(dhc, axis=0, keepdims=True)

        @pl.when(pl.program_id(1) == 0)
        def _():
            dw_ref[...] = dw
            db_ref[...] = db

        @pl.when(pl.program_id(1) > 0)
        def _():
            dw_ref[...] += dw
            db_ref[...] += db

    def part_spec(start, width):
        return pl.BlockSpec((seq, tc), lambda j, b: (b, jnp.clip(j - start, 0, width - 1)))

    n_cols = n_blocks * tc
    return pl.pallas_call(
        body,
        name=name,
        grid=(n_blocks, n_seq),
        in_specs=[
            pl.BlockSpec((seq, tc), lambda j, b: (b, j + off)),
            pl.BlockSpec((k_taps, tc), lambda j, b: (0, j)),
            pl.BlockSpec((seq, tc), lambda j, b: (b, j)),
        ] + [part_spec(st, wd) for st, wd in zip(starts, widths)] + [ANY],
        out_specs=[
            pl.BlockSpec((seq, tc), lambda j, b: (b, j + off)),
            pl.BlockSpec((k_taps, tc), lambda j, b: (0, j)),
            pl.BlockSpec((1, tc), lambda j, b: (0, j)),
        ],
        out_shape=[
            jax.ShapeDtypeStruct(dproj.shape, BF16),
            jax.ShapeDtypeStruct((k_taps, n_cols), F32),
            jax.ShapeDtypeStruct((1, n_cols), F32),
        ],
        input_output_aliases={3 + n_parts: 0},
        compiler_params=_params(("parallel", "arbitrary")),
    )(proj, conv_w, pre, *dparts, dproj)


def _fill_cols(buf, src, col0, *, name):
    t, c = src.shape
    tm = _pick(t, (1024, 512, 256, 128))

    def body(s_ref, b_ref, o_ref):
        o_ref[...] = s_ref[...].astype(o_ref.dtype)

    return pl.pallas_call(
        body,
        name=name,
        grid=(t // tm,),
        in_specs=[pl.BlockSpec((tm, c), lambda i: (i, 0)), ANY],
        out_specs=pl.BlockSpec((tm, c), lambda i: (i, col0 // c)),
        out_shape=jax.ShapeDtypeStruct(buf.shape, buf.dtype),
        input_output_aliases={1: 0},
        compiler_params=_params(("parallel",)),
    )(src, buf)


def _softplus(x):
    return jnp.maximum(x, 0.0) + jnp.log(1.0 + jnp.exp(-jnp.abs(x)))


def _chunk_decay(dtraw, bias, alog):
    q = dtraw.shape[0]
    dt = _softplus(dtraw + bias)
    a = -jnp.exp(alog)
    rows = lax.broadcasted_iota(jnp.int32, (q, q), 0)
    cols = lax.broadcasted_iota(jnp.int32, (q, q), 1)
    lower = rows >= cols
    acum = jnp.dot(lower.astype(F32), dt * a, precision=lax.Precision.HIGHEST, preferred_element_type=F32)
    return dt, a, acum, acum.T, lower


def _dot_exact(v, sel):
    hi = v.astype(BF16)
    r1 = v - hi.astype(F32)
    mid = r1.astype(BF16)
    lo = (r1 - mid.astype(F32)).astype(BF16)
    return (jnp.dot(hi, sel, preferred_element_type=F32) + jnp.dot(mid, sel, preferred_element_type=F32)
            + jnp.dot(lo, sel, preferred_element_type=F32))


def _head_selectors(gw, p):
    sum_heads = (lax.broadcasted_iota(jnp.int32, (gw, LANE), 0) // p == lax.broadcasted_iota(jnp.int32, (gw, LANE), 1))
    spread = (lax.broadcasted_iota(jnp.int32, (LANE, gw), 0) == lax.broadcasted_iota(jnp.int32, (LANE, gw), 1) // p)
    return sum_heads.astype(BF16), spread.astype(BF16)


def _row_spread(v, spread):
    return _dot_exact(jnp.broadcast_to(v, (8, v.shape[1])), spread)[0:1, :]


def _head_pad(v, r_heads):
    lead = v.shape[:-1]
    vg = v.reshape(lead + (N_SSD_GROUPS, r_heads))
    vg = jnp.pad(vg, [(0, 0)] * len(lead) + [(0, 0), (0, LANE - r_heads)])
    out = vg.reshape(lead + (N_SSD_GROUPS * LANE,))
    return out[None] if out.ndim == 1 else out


def _head_unpad(v, r_heads):
    lead = v.shape[:-1]
    out = v.reshape(lead + (N_SSD_GROUPS, LANE))[..., :r_heads].reshape(lead + (N_SSD_GROUPS * r_heads,))
    return out[0] if (len(lead) == 1 and lead[0] == 1) else out


def _ssd_w_in_layout(w_in, d_inner, d_xbc, r_heads):
    main = w_in[:, :d_inner + d_xbc]
    return jnp.concatenate([main, _head_pad(w_in[:, d_inner + d_xbc:], r_heads)], axis=1)


def _ssd_w_in_unlayout(w, d_inner, d_xbc, r_heads):
    main = w[:, :d_inner + d_xbc]
    return jnp.concatenate([main, _head_unpad(w[:, d_inner + d_xbc:], r_heads)], axis=1)


def _ssd_dims(proj, xbc):
    d_xbc = xbc.shape[1]
    d_inner = d_xbc - 2 * N_SSD_GROUPS * D_STATE
    gw = d_inner // N_SSD_GROUPS
    return d_inner, d_xbc, gw, gw // HEAD_DIM


def _ssd_fwd(proj, xbc, bias_p, alog_p, dskip_p, norm_w, n_seq, *, name):
    t = proj.shape[0]
    d_inner, d_xbc, gw, r_heads = _ssd_dims(proj, xbc)
    q, n, n_g, p = CHUNK, D_STATE, N_SSD_GROUPS, HEAD_DIM
    seq = t // n_seq
    nc = seq // q
    dt_blk0 = (d_inner + d_xbc) // LANE

    def body(x_ref, b_ref, c_ref, z_ref, dtr_ref, bias_ref, alog_ref, dsk_ref, nw_ref, yn_ref, y_ref, hs_ref, h_scr):
        @pl.when(pl.program_id(2) == 0)
        def _():
            h_scr[...] = jnp.zeros_like(h_scr)

        dt, a, acum, acum_t, lower = _chunk_decay(dtr_ref[...], bias_ref[...], alog_ref[...])
        x = x_ref[...]
        bb = b_ref[...].astype(BF16)
        cb = c_ref[...].astype(BF16)
        g_mat = lax.dot_general(cb, bb, (((1,), (1,)), ((), ())), preferred_element_type=F32)
        h_prev = h_scr[...]
        hs_ref[...] = h_prev
        c_h = jnp.dot(cb, h_prev.astype(BF16), preferred_element_type=F32)
        _, spread = _head_selectors(gw, p)
        acum_s = _dot_exact(acum, spread)
        a_last_s = acum_s[q - 1:q, :]
        xdt = x * _dot_exact(dt, spread)
        xdt_b = xdt.astype(BF16)
        ys = []
        for h in range(r_heads):
            decay = jnp.exp(jnp.where(lower, acum[:, h:h + 1] - acum_t[h:h + 1, :], -jnp.inf))
            ys.append(jnp.dot((g_mat * decay).astype(BF16), xdt_b[:, h * p:(h + 1) * p], preferred_element_type=F32))
        y = jnp.concatenate(ys, axis=1) + jnp.exp(acum_s) * c_h + _row_spread(dsk_ref[...], spread) * x
        xd = xdt * jnp.exp(a_last_s - acum_s)
        states = lax.dot_general(bb, xd.astype(BF16), (((0,), (0,)), ((), ())), preferred_element_type=F32)
        h_scr[...] = h_prev * jnp.exp(a_last_s) + states
        y_ref[...] = y
        gated = y * _silu(z_ref[...])
        rstd = lax.rsqrt(jnp.mean(gated * gated, axis=-1, keepdims=True) + EPS)
        yn_ref[...] = (gated * rstd * nw_ref[...]).astype(BF16)

    row = lambda b, g, c: b * nc + c
    vec = pl.BlockSpec((1, LANE), lambda b, g, c: (0, g))
    return pl.pallas_call(
        body,
        name=name,
        grid=(n_seq, n_g, nc),
        in_specs=[
            pl.BlockSpec((q, gw), lambda b, g, c: (row(b, g, c), g)),
            pl.BlockSpec((q, n), lambda b, g, c: (row(b, g, c), d_inner // n + g)),
            pl.BlockSpec((q, n), lambda b, g, c: (row(b, g, c), d_inner // n + n_g + g)),
            pl.BlockSpec((q, gw), lambda b, g, c: (row(b, g, c), g)),
            pl.BlockSpec((q, LANE), lambda b, g, c: (row(b, g, c), dt_blk0 + g)),
            vec, vec, vec,
            pl.BlockSpec((1, gw), lambda b, g, c: (0, g)),
        ],
        out_specs=[
            pl.BlockSpec((q, gw), lambda b, g, c: (row(b, g, c), g)),
            pl.BlockSpec((q, gw), lambda b, g, c: (row(b, g, c), g)),
            pl.BlockSpec((n, gw), lambda b, g, c: (row(b, g, c), g)),
        ],
        out_shape=[
            jax.ShapeDtypeStruct((t, d_inner), BF16),
            jax.ShapeDtypeStruct((t, d_inner), F32),
            jax.ShapeDtypeStruct((n_seq * nc * n, d_inner), F32),
        ],
        scratch_shapes=[pltpu.VMEM((n, gw), F32)],
        compiler_params=_params(("parallel", "parallel", "arbitrary")),
    )(xbc, xbc, xbc, proj, proj, bias_p, alog_p, dskip_p, norm_w)


def _ssd_bwd(proj, xbc, hs, y, dyn, bias_p, alog_p, dskip_p, norm_w, n_seq, *, name):
    t = proj.shape[0]
    d_inner, d_xbc, gw, r_heads = _ssd_dims(proj, xbc)
    q, n, n_g, p = CHUNK, D_STATE, N_SSD_GROUPS, HEAD_DIM
    seq = t // n_seq
    nc = seq // q
    dt_blk0 = (d_inner + d_xbc) // LANE

    def body(x_ref, b_ref, c_ref, z_ref, dtr_ref, bias_ref, alog_ref, dsk_ref, nw_ref, hs_ref, y_ref, dyn_ref,
             dx_ref, db_ref, dc_ref, dz_ref, ddtr_ref, dnw_ref, dbias_ref, dalog_ref, ddsk_ref, dh_scr):
        first = jnp.logical_and(pl.program_id(1) == 0, pl.program_id(2) == 0)

        @pl.when(pl.program_id(2) == 0)
        def _():
            dh_scr[...] = jnp.zeros_like(dh_scr)

        dtraw = dtr_ref[...]
        dt, a, acum, acum_t, lower = _chunk_decay(dtraw, bias_ref[...], alog_ref[...])
        x = x_ref[...]
        bb = b_ref[...].astype(BF16)
        cb = c_ref[...].astype(BF16)
        g_mat = lax.dot_general(cb, bb, (((1,), (1,)), ((), ())), preferred_element_type=F32)

        yv = y_ref[...]
        z = z_ref[...]
        sz = _silu(z)
        gated = yv * sz
        rstd = lax.rsqrt(jnp.mean(gated * gated, axis=-1, keepdims=True) + EPS)
        gn = gated * rstd
        dynv = dyn_ref[...]
        gwt = dynv * nw_ref[...]
        dgated = rstd * (gwt - gn * jnp.mean(gwt * gn, axis=-1, keepdims=True))
        dnw = jnp.sum(dynv * gn, axis=0, keepdims=True)
        dy = dgated * sz
        dz_ref[...] = (dgated * yv * _dsilu(z)).astype(BF16)

        h_prev = hs_ref[...]
        h_prev_b = h_prev.astype(BF16)
        ds = dh_scr[...]
        ds_b = ds.astype(BF16)
        sum_heads, spread = _head_selectors(gw, p)
        acum_s = _dot_exact(acum, spread)
        a_last_s = acum_s[q - 1:q, :]
        dt_s = _dot_exact(dt, spread)
        dsk_s = _row_spread(dsk_ref[...], spread)
        dte_s = jnp.exp(a_last_s - acum_s)
        cd_s = jnp.exp(a_last_s)
        xdt = x * dt_s
        xdt_b = xdt.astype(BF16)
        dy_b = dy.astype(BF16)
        gt_mat = lax.dot_general(bb, cb, (((1,), (1,)), ((), ())), preferred_element_type=F32)
        upper = lax.broadcasted_iota(jnp.int32, (q, q), 0) <= lax.broadcasted_iota(jnp.int32, (q, q), 1)
        dg = jnp.zeros((q, q), F32)
        dxdts, w_diffs = [], []
        for h in range(r_heads):
            hsl = slice(h * p, (h + 1) * p)
            diff = acum[:, h:h + 1] - acum_t[h:h + 1, :]
            decay = jnp.exp(jnp.where(lower, diff, -jnp.inf))
            decay_t = jnp.exp(jnp.where(upper, -diff, -jnp.inf))
            mt_mat = gt_mat * decay_t
            dm = lax.dot_general(dy_b[:, hsl], xdt_b[:, hsl], (((1,), (1,)), ((), ())), preferred_element_type=F32)
            dm_t = lax.dot_general(xdt_b[:, hsl], dy_b[:, hsl], (((1,), (1,)), ((), ())), preferred_element_type=F32)
            dg = dg + dm * decay
            dxdts.append(jnp.dot(mt_mat.astype(BF16), dy_b[:, hsl], preferred_element_type=F32))
            w_diffs.append(dm * (g_mat * decay) - dm_t * mt_mat)
        sel_q = (lax.broadcasted_iota(jnp.int32, (r_heads * q, LANE), 0) // q
                 == lax.broadcasted_iota(jnp.int32, (r_heads * q, LANE), 1)).astype(BF16)
        dacum_diag = _dot_exact(jnp.concatenate(w_diffs, axis=1), sel_q)
        c_h = jnp.dot(cb, h_prev_b, preferred_element_type=F32)
        dxd = jnp.dot(bb, ds_b, preferred_element_type=F32)
        dxdt = jnp.concatenate(dxdts, axis=1) + dxd * dte_s
        dye = dy * jnp.exp(acum_s)
        dye_b = dye.astype(BF16)
        xd = xdt * dte_s
        xd_b = xd.astype(BF16)
        dg_b = dg.astype(BF16)
        dx_ref[...] = dxdt * dt_s + dsk_s * dy
        dc_ref[...] = (jnp.dot(dg_b, bb, preferred_element_type=F32)
                       + lax.dot_general(dye_b, h_prev_b, (((1,), (1,)), ((), ())), preferred_element_type=F32))
        db_ref[...] = (lax.dot_general(dg_b, cb, (((0,), (0,)), ((), ())), preferred_element_type=F32)
                       + lax.dot_general(xd_b, ds_b, (((1,), (1,)), ((), ())), preferred_element_type=F32))
        dh_scr[...] = ds * cd_s + lax.dot_general(cb, dye_b, (((0,), (0,)), ((), ())), preferred_element_type=F32)
        ddt_cols = _dot_exact(x * dxdt, sum_heads)
        dacum_y = _dot_exact(dye * c_h - dxd * xd, sum_heads)
        col_sums = jnp.concatenate([
            jnp.sum(dxd * xd, axis=0, keepdims=True) + jnp.sum(ds * h_prev, axis=0, keepdims=True) * cd_s,
            jnp.sum(dy * x, axis=0, keepdims=True),
            jnp.zeros((6, gw), F32)], axis=0)
        col_sums = _dot_exact(col_sums, sum_heads)
        ddsk = col_sums[1:2, :]
        rows_q = lax.broadcasted_iota(jnp.int32, (q, LANE), 0)
        dacum = dacum_diag + dacum_y + jnp.where(rows_q == q - 1, col_sums[0:1, :], 0.0)
        dadt = jnp.dot(upper.astype(F32), dacum, precision=lax.Precision.HIGHEST, preferred_element_type=F32)
        ddt = dadt * a + ddt_cols
        ddtr = ddt * _sigmoid(dtraw + bias_ref[...])
        ddtr_ref[...] = ddtr
        dbias = jnp.sum(ddtr, axis=0, keepdims=True)
        dalog = jnp.sum(dadt * dt, axis=0, keepdims=True) * a

        @pl.when(first)
        def _():
            dnw_ref[...] = dnw
            dbias_ref[...] = dbias
            dalog_ref[...] = dalog
            ddsk_ref[...] = ddsk

        @pl.when(jnp.logical_not(first))
        def _():
            dnw_ref[...] += dnw
            dbias_ref[...] += dbias
            dalog_ref[...] += dalog
            ddsk_ref[...] += ddsk

    row = lambda g, b, c: b * nc + (nc - 1 - c)
    vec = pl.BlockSpec((1, LANE), lambda g, b, c: (0, g))
    wide = pl.BlockSpec((q, gw), lambda g, b, c: (row(g, b, c), g))
    narrow = pl.BlockSpec((q, n), lambda g, b, c: (row(g, b, c), g))
    return pl.pallas_call(
        body,
        name=name,
        grid=(n_g, n_seq, nc),
        in_specs=[
            wide,
            pl.BlockSpec((q, n), lambda g, b, c: (row(g, b, c), d_inner // n + g)),
            pl.BlockSpec((q, n), lambda g, b, c: (row(g, b, c), d_inner // n + n_g + g)),
            wide,
            pl.BlockSpec((q, LANE), lambda g, b, c: (row(g, b, c), dt_blk0 + g)),
            vec, vec, vec,
            pl.BlockSpec((1, gw), lambda g, b, c: (0, g)),
            pl.BlockSpec((n, gw), lambda g, b, c: (row(g, b, c), g)),
            wide, wide,
        ],
        out_specs=[
            wide, narrow, narrow, wide, narrow,
            pl.BlockSpec((1, gw), lambda g, b, c: (0, g)),
            vec, vec, vec,
        ],
        out_shape=[
            jax.ShapeDtypeStruct((t, d_inner), F32),
            jax.ShapeDtypeStruct((t, n_g * n), F32),
            jax.ShapeDtypeStruct((t, n_g * n), F32),
            jax.ShapeDtypeStruct(proj.shape, BF16),
            jax.ShapeDtypeStruct((t, n_g * LANE), F32),
            jax.ShapeDtypeStruct((1, d_inner), F32),
            jax.ShapeDtypeStruct((1, n_g * LANE), F32),
            jax.ShapeDtypeStruct((1, n_g * LANE), F32),
            jax.ShapeDtypeStruct((1, n_g * LANE), F32),
        ],
        scratch_shapes=[pltpu.VMEM((n, gw), F32)],
        compiler_params=_params(("parallel", "arbitrary", "arbitrary")),
    )(xbc, xbc, xbc, proj, proj, bias_p, alog_p, dskip_p, norm_w, hs, y, dyn)


MESH_IDS = pl.DeviceIdType.MESH


def _my_index():
    return 4 * lax.axis_index("x") + 2 * lax.axis_index("y") + lax.axis_index("c")


def _all_gather(shard, *, name):
    def body(x_ref, out_ref, send_sems, recv_sems, local_sem):
        x, y, c = lax.axis_index("x"), lax.axis_index("y"), lax.axis_index("c")
        me, sibling = (x, y, c), (x, y, 1 - c)
        chips = [(1 - x, y), (x, 1 - y), (1 - x, 1 - y)]

        def blk(px, py, pc):
            return out_ref.at[4 * px + 2 * py + pc]

        def copy(k, block, to, src=None):
            return pltpu.make_async_remote_copy(
                src_ref=blk(*block) if src is None else src, dst_ref=blk(*block),
                send_sem=send_sems.at[k], recv_sem=recv_sems.at[k], device_id=to, device_id_type=MESH_IDS)

        mine = pltpu.make_async_copy(x_ref, blk(*me), local_sem)
        mine.start()
        first = [copy(0, me, sibling, src=x_ref)]
        first += [copy(1 + j, me, (*chip, c), src=x_ref) for j, chip in enumerate(chips)]
        for cp in first:
            cp.start()
        passed = [copy(4 + j, (*chip, c), sibling) for j, chip in enumerate(chips)]
        for j, chip in enumerate(chips):
            copy(1 + j, (*chip, c), me).wait_recv()
            passed[j].start()
        copy(0, sibling, me).wait_recv()
        for j, chip in enumerate(chips):
            copy(4 + j, (*chip, 1 - c), me).wait_recv()
        for cp in first + passed:
            cp.wait_send()
        mine.wait()

    return pl.pallas_call(
        body,
        name=name,
        in_specs=[ANY],
        out_specs=ANY,
        out_shape=jax.ShapeDtypeStruct((N_DEV,) + shard.shape, shard.dtype),
        scratch_shapes=[pltpu.SemaphoreType.DMA((7,)), pltpu.SemaphoreType.DMA((7,)), pltpu.SemaphoreType.DMA],
    )(shard)


HBM_SPEC = pl.BlockSpec(memory_space=pltpu.HBM)
SEM_SPEC = pl.BlockSpec(memory_space=pltpu.SEMAPHORE)
SPLIT_COPY_PARAMS = pltpu.CompilerParams(has_side_effects=pltpu.SideEffectType.DATAFLOW_SIDE_EFFECTING)


def _peer_list():
    x, y, c = lax.axis_index("x"), lax.axis_index("y"), lax.axis_index("c")
    peers = []
    for k in range(1, N_DEV):
        px = 1 - x if k & 4 else x
        py = 1 - y if k & 2 else y
        pc = 1 - c if k & 1 else c
        peers.append(((px, py, pc), 4 * px + 2 * py + pc))
    return 4 * x + 2 * y + c, peers


def _push_copies(src_refs, land_refs, send_sems, recv_sems, blockwise):
    me, peers = _peer_list()
    copies = []
    for a, (src_ref, land_ref) in enumerate(zip(src_refs, land_refs)):
        for k, (dev, idx) in enumerate(peers):
            sem = a * (N_DEV - 1) + k
            src = src_ref.at[idx] if blockwise else src_ref
            copies.append(tuple(
                pltpu.make_async_remote_copy(src_ref=src, dst_ref=land_ref.at[slot], send_sem=send_sems.at[sem],
                                             recv_sem=recv_sems.at[sem], device_id=dev, device_id_type=MESH_IDS)
                for slot in (me, idx)))
    return copies


def _push_start(srcs, blockwise, after, *, name):
    n = len(srcs)
    blocks = [s_.shape[1:] if blockwise else s_.shape for s_ in srcs]

    def body(*refs):
        src_refs, land_refs = refs[:n], refs[n:2 * n]
        send_sems, recv_sems = refs[2 * n + 1], refs[2 * n + 2]
        token = refs[-1]
        for send, _ in _push_copies(src_refs, land_refs, send_sems, recv_sems, blockwise):
            send.start()
        token[...] = jnp.zeros_like(token)

    n_sem = n * (N_DEV - 1)
    lands = [lax.empty((N_DEV,) + b, s_.dtype) for b, s_ in zip(blocks, srcs)]
    out = pl.pallas_call(
        body,
        name=name,
        in_specs=[HBM_SPEC] * (2 * n) + [ANY],
        out_specs=(SEM_SPEC, SEM_SPEC) + (HBM_SPEC,) * (2 * n) + (pl.BlockSpec(memory_space=pltpu.VMEM),),
        out_shape=(pltpu.SemaphoreType.DMA((n_sem,)), pltpu.SemaphoreType.DMA((n_sem,)))
        + tuple(pltpu.HBM(a.shape, a.dtype) for a in list(srcs) + lands)
        + (jax.ShapeDtypeStruct((8, LANE), F32),),
        input_output_aliases={i: 2 + i for i in range(2 * n)},
        compiler_params=SPLIT_COPY_PARAMS,
    )(*[pltpu.with_memory_space_constraint(a, pltpu.HBM) for a in list(srcs) + lands], after)
    return out[0], out[1], out[2:2 + n], out[2 + n:2 + 2 * n], out[-1]


def _push_wait(send_sems, recv_sems, srcs, lands, blockwise, after, *, name):
    n = len(srcs)

    def body(*refs):
        src_refs, land_refs = refs[:n], refs[n:2 * n]
        send_sems, recv_sems = refs[2 * n], refs[2 * n + 1]
        for send, recv in _push_copies(src_refs, land_refs, send_sems, recv_sems, blockwise):
            send.wait_send()
            recv.wait_recv()

    out = pl.pallas_call(
        body,
        name=name,
        in_specs=[HBM_SPEC] * (2 * n) + [SEM_SPEC, SEM_SPEC, ANY],
        out_specs=(HBM_SPEC,) * (2 * n),
        out_shape=tuple(pltpu.HBM(a.shape, a.dtype) for a in list(srcs) + list(lands)),
        input_output_aliases={i: i for i in range(2 * n)},
        compiler_params=SPLIT_COPY_PARAMS,
    )(*srcs, *lands, send_sems, recv_sems, after)
    return out[n:]


def _with_own_slot(landing, own):
    slot = lax.broadcasted_iota(jnp.int32, (N_DEV,) + (1,) * own.ndim, 0)
    return jnp.where(slot == _my_index(), own[None], landing)


def _sum_slots(parts, own=None, *, name):
    shape = parts.shape[1:]
    n, c = parts.shape[0], parts.shape[-1]
    r = parts.size // (n * c)
    tm = _pick(r, (256, 128, 64, 32, 16, 8))

    def body(p_ref, *rest):
        o_ref = rest[-1]
        me = _my_index()

        def slot(s):
            if own is None:
                return p_ref[s].astype(F32)
            return jnp.where(me == s, rest[0][...], p_ref[s]).astype(F32)

        acc = slot(0)
        for s in range(1, n):
            acc = acc + slot(s)
        o_ref[...] = acc

    tile = pl.BlockSpec((tm, c), lambda i: (i, 0))
    return pl.pallas_call(
        body,
        name=name,
        grid=(r // tm,),
        in_specs=[pl.BlockSpec((n, tm, c), lambda i: (0, i, 0))] + ([] if own is None else [tile]),
        out_specs=tile,
        out_shape=jax.ShapeDtypeStruct((r, c), F32),
        compiler_params=_params(("parallel",)),
    )(parts.reshape(n, r, c), *([] if own is None else [own.reshape(r, c)])).reshape(shape)


def _row_count(shape):
    c = shape[-1]
    rows = 1
    for s in shape[:-1]:
        rows *= s
    return rows, c, c + (-c) % LANE


PACK_ROWS = 256


def _pack_rows(arrays):
    pieces = []
    for a in arrays:
        rows, c, cp = _row_count(a.shape)
        a2 = a.reshape(rows, c)
        if cp > c:
            a2 = jnp.pad(a2, ((0, 0), (0, cp - c)))
        a2 = a2.reshape(rows * cp // LANE, LANE)
        if a2.shape[0] % 8:
            a2 = jnp.pad(a2, ((0, 8 - a2.shape[0] % 8), (0, 0)))
        pieces.append(a2)
    total = sum(p.shape[0] for p in pieces)
    if total % PACK_ROWS:
        pieces.append(jnp.zeros((PACK_ROWS - total % PACK_ROWS, LANE), F32))
    return jnp.concatenate(pieces, axis=0)


def _unpack_rows(packed, shapes, lead=()):
    out, off = [], 0
    for shp in shapes:
        rows, c, cp = _row_count(shp)
        n_rows = rows * cp // LANE
        seg = packed[..., off:off + n_rows, :].reshape(lead + (rows, cp))
        out.append(seg[..., :c].reshape(lead + tuple(shp)))
        off += n_rows + (-n_rows) % 8
    return out


def _unshard(stacked, axis):
    moved = jnp.moveaxis(stacked, 0, axis)
    shp = moved.shape
    return moved.reshape(shp[:axis] + (shp[axis] * shp[axis + 1],) + shp[axis + 2:])


def _shard_major(full, axis):
    shp = full.shape
    split = full.reshape(shp[:axis] + (N_DEV, shp[axis] // N_DEV) + shp[axis + 1:])
    return jnp.moveaxis(split, axis, 0)


def _my_shard(full, axis):
    size = full.shape[axis] // N_DEV
    return lax.dynamic_slice_in_dim(full, _my_index() * size, size, axis)


def _local_step(x, target, w, fetch, emit, n_seq):
    def with_token(vec, token):
        return vec + jnp.tile(token[0:1, :], (1, vec.shape[1] // LANE))

    depth, d_model = w["norm_mix_pre"].shape
    d_inner = w["ssd_norm_w"].shape[1]
    d_xbc = w["ssd_conv_w"].shape[2]
    saved = []
    for i in range(depth):
        j = i // 2
        m, token = fetch(i, "mix", x)
        mix_pre_w = with_token(w["norm_mix_pre"][i:i + 1], token)
        s = {"x": x, "mix_pre_w": mix_pre_w}
        if i % 2 == 0:
            u = _rms_fwd(x, mix_pre_w, out_dtype=BF16, name=f"l{i}_mix_pre")
            proj = _mm(u, m["ssd_w_in"], name=f"l{i}_ssd_in")
            xbc, xbc_pre = _ssd_conv_fwd(proj, d_inner, d_xbc, w["ssd_conv_w"][j], w["ssd_conv_b"][j:j + 1], n_seq,
                                         name=f"l{i}_ssd_conv")
            yn, y, hs = _ssd_fwd(proj, xbc, w["ssd_dt_bias"][j:j + 1], w["ssd_a_log"][j:j + 1], w["ssd_d"][j:j + 1],
                                 w["ssd_norm_w"][j:j + 1], n_seq, name=f"l{i}_ssd_scan")
            mix = _mm(yn, m["ssd_w_out"], name=f"l{i}_ssd_out")
            s.update(u=u, proj=proj, xbc=xbc, xbc_pre=xbc_pre, yn=yn, y=y, hs=hs)
        else:
            u = _rms_fwd(x, mix_pre_w, out_dtype=F32, name=f"l{i}_mix_pre")
            mix = _pool_fwd(u, m["pool_w"], w["pool_scale"][j:j + 1], n_seq, name=f"l{i}_pool")
            s.update(u=u)
        x1 = _res_rms_fwd(x, mix, w["norm_mix_post"][i:i + 1], name=f"l{i}_mix_post")
        m_ffn, token = fetch(i, "ffn", x1)
        m = {**m, **m_ffn}
        ffn_pre_w = with_token(w["norm_ffn_pre"][i:i + 1], token)
        n = _rms_fwd(x1, ffn_pre_w, out_dtype=BF16, name=f"l{i}_ffn_pre")
        h = _mm(n, m["ffn_w_up"], out_dtype=BF16, name=f"l{i}_ffn_up")
        a, hc = _ffn_act_fwd(h, w["ffn_conv_w"][i], w["ffn_conv_b"][i:i + 1], n_seq, name=f"l{i}_ffn_act")
        f = _mm(a, m["ffn_w_down"], name=f"l{i}_ffn_down")
        x = _res_rms_fwd(x1, f, w["norm_ffn_post"][i:i + 1], name=f"l{i}_ffn_post")
        s.update(mix=mix, x1=x1, n=n, h=h, hc=hc, a=a, f=f, m=m, ffn_pre_w=ffn_pre_w)
        saved.append(s)

    loss, dx = _loss_head(x, target)
    grads = {k: [None] * len(w[k]) for k in SMALL}
    token = jnp.zeros((8, LANE), F32)
    for i in reversed(range(depth)):
        j = i // 2
        s = saved[i]
        m, gm = s["m"], {}
        df, grads["norm_ffn_post"][i] = _rms_bwd(s["f"], with_token(w["norm_ffn_post"][i:i + 1], token), dx, None,
                                                 out_dtype=BF16, name=f"l{i}_ffn_post_b")
        da = _mm(df, m["ffn_w_down"], tb=True, out_dtype=BF16, name=f"l{i}_ffn_down_bx")
        gm["ffn_w_down"] = _mm(s["a"], df, ta=True, name=f"l{i}_ffn_down_bw")
        dh, grads["ffn_conv_w"][i], grads["ffn_conv_b"][i] = _ffn_act_bwd(
            s["h"], s["hc"], w["ffn_conv_w"][i], da, n_seq, name=f"l{i}_ffn_act_b")
        dn = _mm(dh, m["ffn_w_up"], tb=True, name=f"l{i}_ffn_up_bx")
        gm["ffn_w_up"] = _mm(s["n"], dh, ta=True, name=f"l{i}_ffn_up_bw")
        dx1, grads["norm_ffn_pre"][i] = _rms_bwd(s["x1"], s["ffn_pre_w"], dn, dx, name=f"l{i}_ffn_pre_b")
        token = emit(i, "ffn", gm, dx1)
        gm = {}
        dmix, grads["norm_mix_post"][i] = _rms_bwd(s["mix"], with_token(w["norm_mix_post"][i:i + 1], token), dx1, None,
                                                   out_dtype=BF16 if i % 2 == 0 else F32, name=f"l{i}_mix_post_b")
        if i % 2 == 0:
            dyn = _mm(dmix, m["ssd_w_out"], tb=True, name=f"l{i}_ssd_out_bx")
            gm["ssd_w_out"] = _mm(s["yn"], dmix, ta=True, name=f"l{i}_ssd_out_bw")
            dxs, db, dc, dz, ddtr, dnw, dbias, dalog, ddsk = _ssd_bwd(
                s["proj"], s["xbc"], s["hs"], s["y"], dyn, w["ssd_dt_bias"][j:j + 1], w["ssd_a_log"][j:j + 1],
                w["ssd_d"][j:j + 1], w["ssd_norm_w"][j:j + 1], n_seq, name=f"l{i}_ssd_scan_b")
            grads["ssd_norm_w"][j], grads["ssd_dt_bias"][j], grads["ssd_a_log"][j], grads["ssd_d"][j] = (
                dnw, dbias, dalog, ddsk)
            dproj, grads["ssd_conv_w"][j], grads["ssd_conv_b"][j] = _ssd_conv_bwd(
                s["proj"], d_inner, w["ssd_conv_w"][j], s["xbc_pre"], (dxs, db, dc), dz, n_seq,
                name=f"l{i}_ssd_conv_b")
            dproj = _fill_cols(dproj, ddtr, d_inner + d_xbc, name=f"l{i}_ssd_dt_b")
            du = _mm(dproj, m["ssd_w_in"], tb=True, name=f"l{i}_ssd_in_bx")
            gm["ssd_w_in"] = _mm(s["u"], dproj, ta=True, name=f"l{i}_ssd_in_bw")
        else:
            du, gm["pool_w"], grads["pool_scale"][j] = _pool_bwd(
                s["u"], m["pool_w"], w["pool_scale"][j:j + 1], dmix, n_seq, name=f"l{i}_pool_b")
        dx, grads["norm_mix_pre"][i] = _rms_bwd(s["x"], s["mix_pre_w"], du, dx1, name=f"l{i}_mix_pre_b")
        token = emit(i, "mix", gm, dx)
    return loss, dx, grads


BIG = (("ssd_w_in", 2), ("ssd_w_out", 1), ("pool_w", 2), ("ffn_w_up", 2), ("ffn_w_down", 1))
SMALL_SHARDED = (("ssd_conv_w", 2), ("ffn_conv_w", 2), ("pool_scale", 1))
SMALL = ("ssd_conv_w", "ssd_conv_b", "ssd_dt_bias", "ssd_a_log", "ssd_d", "ssd_norm_w", "pool_scale", "ffn_conv_w",
         "ffn_conv_b", "norm_mix_pre", "norm_mix_post", "norm_ffn_pre", "norm_ffn_post")
WEIGHTS = ("ssd_w_in", "ssd_conv_w", "ssd_conv_b", "ssd_dt_bias", "ssd_a_log", "ssd_d", "ssd_norm_w", "ssd_w_out",
           "pool_w", "pool_scale", "ffn_w_up", "ffn_conv_w", "ffn_conv_b", "ffn_w_down", "norm_mix_pre",
           "norm_mix_post", "norm_ffn_pre", "norm_ffn_post")


def _ssd_sizes(d_inner):
    return d_inner + 2 * N_SSD_GROUPS * D_STATE, d_inner // HEAD_DIM // N_SSD_GROUPS


def _small_compute_layout(full, d_inner):
    _, r_heads = _ssd_sizes(d_inner)
    w = {k: full[k] for k in SMALL}
    for k in ("ssd_dt_bias", "ssd_a_log", "ssd_d"):
        w[k] = _head_pad(full[k], r_heads)
    for k in ("ffn_conv_w", "ffn_conv_b"):
        w[k] = _interleave(full[k])
    return w


def _matmul_compute_layout(k, full, d_inner):
    d_xbc, r_heads = _ssd_sizes(d_inner)
    if k == "ssd_w_in":
        return _ssd_w_in_layout(full, d_inner, d_xbc, r_heads)
    if k == "ffn_w_up":
        return _interleave(full)
    return full


def _layer_matrices(i, part):
    if part == "ffn":
        return (("ffn_w_up", 1, i), ("ffn_w_down", 0, i))
    return (("ssd_w_in", 1, i // 2), ("ssd_w_out", 0, i // 2)) if i % 2 == 0 else (("pool_w", 1, i // 2),)


def _fetch_group(i, part):
    if i % 2 == 1:
        return _layer_matrices(i, "mix") + _layer_matrices(i, "ffn") if part == "mix" else ()
    return _layer_matrices(i, part)


def _matmul_grad_reference_layout(k, g, d_inner):
    d_xbc, r_heads = _ssd_sizes(d_inner)
    if k == "ssd_w_in":
        return _ssd_w_in_unlayout(g, d_inner, d_xbc, r_heads)
    if k == "ffn_w_up":
        return _deinterleave(g)
    return g


def _small_grads_reference_layout(grads, shapes, d_inner):
    _, r_heads = _ssd_sizes(d_inner)
    g = {k: jnp.stack(grads[k]) for k in SMALL}
    for k in ("ssd_dt_bias", "ssd_a_log", "ssd_d"):
        g[k] = _head_unpad(g[k][:, 0], r_heads)
    for k in ("ffn_conv_w", "ffn_conv_b"):
        g[k] = _deinterleave(g[k])
    return {k: v.reshape(shapes[k]) for k, v in g.items()}


def kernel(x, ssd_w_in, ssd_conv_w, ssd_conv_b, ssd_dt_bias, ssd_a_log, ssd_d, ssd_norm_w, ssd_w_out, pool_w, pool_scale, ffn_w_up, ffn_conv_w, ffn_conv_b, ffn_w_down, norm_mix_pre, norm_mix_post, norm_ffn_pre, norm_ffn_post, loss_target, m_ssd_w_in, m_ssd_conv_w, m_ssd_conv_b, m_ssd_dt_bias, m_ssd_a_log, m_ssd_d, m_ssd_norm_w, m_ssd_w_out, m_pool_w, m_pool_scale, m_ffn_w_up, m_ffn_conv_w, m_ffn_conv_b, m_ffn_w_down, m_norm_mix_pre, m_norm_mix_post, m_norm_ffn_pre, m_norm_ffn_post, v_ssd_w_in, v_ssd_conv_w, v_ssd_conv_b, v_ssd_dt_bias, v_ssd_a_log, v_ssd_d, v_ssd_norm_w, v_ssd_w_out, v_pool_w, v_pool_scale, v_ffn_w_up, v_ffn_conv_w, v_ffn_conv_b, v_ffn_w_down, v_norm_mix_pre, v_norm_mix_post, v_norm_ffn_pre, v_norm_ffn_post):
    shards = dict(ssd_w_in=ssd_w_in, ssd_conv_w=ssd_conv_w, ssd_conv_b=ssd_conv_b, ssd_dt_bias=ssd_dt_bias,
                  ssd_a_log=ssd_a_log, ssd_d=ssd_d, ssd_norm_w=ssd_norm_w, ssd_w_out=ssd_w_out, pool_w=pool_w,
                  pool_scale=pool_scale, ffn_w_up=ffn_w_up, ffn_conv_w=ffn_conv_w, ffn_conv_b=ffn_conv_b,
                  ffn_w_down=ffn_w_down, norm_mix_pre=norm_mix_pre, norm_mix_post=norm_mix_post,
                  norm_ffn_pre=norm_ffn_pre, norm_ffn_post=norm_ffn_post)
    moments_m = dict(zip(WEIGHTS, (m_ssd_w_in, m_ssd_conv_w, m_ssd_conv_b, m_ssd_dt_bias, m_ssd_a_log, m_ssd_d, m_ssd_norm_w, m_ssd_w_out, m_pool_w, m_pool_scale, m_ffn_w_up, m_ffn_conv_w, m_ffn_conv_b, m_ffn_w_down, m_norm_mix_pre, m_norm_mix_post, m_norm_ffn_pre, m_norm_ffn_post)))
    moments_v = dict(zip(WEIGHTS, (v_ssd_w_in, v_ssd_conv_w, v_ssd_conv_b, v_ssd_dt_bias, v_ssd_a_log, v_ssd_d, v_ssd_norm_w, v_ssd_w_out, v_pool_w, v_pool_scale, v_ffn_w_up, v_ffn_conv_w, v_ffn_conv_b, v_ffn_w_down, v_norm_mix_pre, v_norm_mix_post, v_norm_ffn_pre, v_norm_ffn_post)))
    n_seq, seq, d_model = x.shape
    t = n_seq * seq

    d_inner = ssd_norm_w.shape[1]
    depth = norm_mix_pre.shape[0]
    x2 = x.reshape(t, d_model)

    shard16 = {k: shards[k].astype(BF16) for k, _ in BIG}
    order = [(i, part) for i in range(depth) for part in ("mix", "ffn") if _fetch_group(i, part)]
    fetches = {}

    def start_fetch(key, after):
        srcs = [shard16[k][l] for k, _, l in _fetch_group(*key)]
        fetches[key] = _push_start(srcs, False, after, name=f"fetch{key[0]}{key[1]}_start")

    start_fetch(order[0], x2)
    full = dict(shards)
    small_packed = _pack_rows([shards[k] for k, _ in SMALL_SHARDED]) + fetches[order[0]][4][0:1, :]
    small_all = _all_gather(small_packed, name="gather_small_weights")
    small_stacked = _unpack_rows(small_all, [shards[k].shape for k, _ in SMALL_SHARDED], lead=(N_DEV,))
    for (k, axis), st in zip(SMALL_SHARDED, small_stacked):
        full[k] = _unshard(st, axis)
    w = _small_compute_layout(full, d_inner)

    def fetch(i, part, x_now):
        key = (i, part)
        if key not in order:
            return {}, jnp.zeros((8, LANE), F32)
        send, recv, srcs, lands, _ = fetches[key]
        after = small_all if key == order[0] else x_now
        lands = _push_wait(send, recv, srcs, lands, False, after, name=f"fetch{i}{part}_wait")
        mats = {}
        for (k, axis, l), land in zip(_fetch_group(i, part), lands):
            whole = _unshard(_with_own_slot(land, shard16[k][l]), axis)
            mats[k] = _matmul_compute_layout(k, whole, d_inner)
        nxt = order.index(key) + 1
        if nxt < len(order):
            start_fetch(order[nxt], lands[0])
            return mats, fetches[order[nxt]][4]
        return mats, jnp.zeros((8, LANE), F32)

    g_layers = {}
    in_flight = []

    def finish_exchange(after):
        key, blocks, (send, recv, srcs, lands, _) = in_flight.pop()
        lands = _push_wait(send, recv, srcs, lands, True, after, name=f"exchange{key[0]}{key[1]}_wait")
        for (k, _, l), land, block in zip(_layer_matrices(*key), lands, blocks):
            own = lax.dynamic_index_in_dim(block, _my_index(), 0, keepdims=False)
            g_layers[k, l] = _sum_slots(land, own, name=f"sum{key[0]}_{k}")

    def emit(i, part, gm, dx_now):
        if in_flight:
            finish_exchange(dx_now)
        blocks = [_shard_major(_matmul_grad_reference_layout(k, gm[k].astype(BF16), d_inner), axis)
                  for k, axis, _ in _layer_matrices(i, part)]
        started = _push_start(blocks, True, dx_now, name=f"exchange{i}{part}_start")
        in_flight.append(((i, part), blocks, started))
        return started[4]

    loss, dx, grads = _local_step(x2, loss_target.reshape(t, d_model), w, fetch, emit, n_seq)
    loss = lax.psum(loss, ("x", "y", "c"))

    g_shard = {}
    small_shapes = {k: full[k].shape for k in SMALL}
    g_small = _small_grads_reference_layout(grads, small_shapes, d_inner)
    s_all = _all_gather(_pack_rows([g_small[k] for k in SMALL]) + in_flight[0][2][4][0:1, :], name="gather_small_grads")
    for k, g in zip(SMALL, _unpack_rows(_sum_slots(s_all, name="sum_small_grads"), [small_shapes[k] for k in SMALL])):
        g_shard[k] = g
    for k, axis in SMALL_SHARDED:
        g_shard[k] = _my_shard(g_shard[k], axis)

    last = [k for k, _, _ in _layer_matrices(*in_flight[0][0])]
    deltas, new_m, new_v = {}, {}, {}
    for k in [k for k in WEIGHTS if k not in last] + last:
        if k == last[0]:
            finish_exchange(deltas["ffn_w_up"])
        if k in dict(BIG):
            g_shard[k] = jnp.stack([g_layers[k, l] for l in range(shards[k].shape[0])])
        deltas[k], new_m[k], new_v[k] = _adamw(shards[k], g_shard[k], moments_m[k], moments_v[k], name=f"adamw_{k}")
    return (loss, dx.reshape(x.shape), *[g_shard[k] for k in WEIGHTS], *[deltas[k] for k in WEIGHTS],
            *[new_m[k] for k in WEIGHTS], *[new_v[k] for k in WEIGHTS])
```

```python
import functools

import jax
import jax.numpy as jnp
from jax import lax
from jax.experimental import pallas as pl
from jax.experimental.pallas import tpu as pltpu

F32 = jnp.float32
BF16 = jnp.bfloat16

N_DEV = 8
HEAD_DIM = 64
N_SSD_GROUPS = 4
D_STATE = 128
CHUNK = 128
POOL_WINDOWS = (2, 4, 8, 16)
EPS = 1e-6
LANE = 128
ADAM_LR = 0.001
ADAM_B1 = 0.9
ADAM_B2 = 0.999
ADAM_EPS = 1e-08
ADAM_WD = 0.01
ADAM_STEP = 10
VMEM_LIMIT = 56 * 1024 * 1024
ANY = pl.BlockSpec(memory_space=pl.ANY)


def _pick(n, cands):
    for c in cands:
        if n % c == 0:
            return c
    return n


def _params(sem):
    return pltpu.CompilerParams(dimension_semantics=sem, vmem_limit_bytes=VMEM_LIMIT)


def _sigmoid(x):
    return 0.5 * jnp.tanh(0.5 * x) + 0.5


def _silu(x):
    return x * _sigmoid(x)


def _dsilu(x):
    s = _sigmoid(x)
    return s * (1.0 + x * (1.0 - s))


def _shift_down(x, s):
    rows = lax.broadcasted_iota(jnp.int32, x.shape, 0)
    return jnp.where(rows >= s, pltpu.roll(x, s, 0), 0.0)


def _shift_up(x, s):
    n = x.shape[0]
    rows = lax.broadcasted_iota(jnp.int32, x.shape, 0)
    return jnp.where(rows < n - s, pltpu.roll(x, n - s, 0), 0.0)


MM_VMEM_BUDGET = 40 * 1024 * 1024
MM_STEP_BYTES = 1_300_000
MM_SUB = 512


def _mm_tiles(m, n, k, a_bytes, b_bytes, o_bytes):
    def cands(dim, sizes):
        out = [s for s in sizes if s <= dim and dim % s == 0]
        return out or [dim]

    best = None
    for tm in cands(m, (m, m // 2, 2048, 1024, 512, 256, 128)):
        if tm % LANE:
            continue
        for tn in cands(n, (n, n // 2, n // 4, 2048, 1024, 512, 256, 128)):
            if tn % (2 * LANE) and tn != n:
                continue
            for tk in cands(k, (k, k // 2, 2048, 1024, 512)):
                if tk % LANE:
                    continue
                nk = k // tk
                acc = tm * tn * 4 if (nk > 1 and o_bytes != 4) else 0
                temps = tm * min(tn, MM_SUB) * 4 + (tm * tk * 2 if a_bytes == 4 else 0) + (tk * tn * 2 if b_bytes == 4 else 0)
                vmem = 2 * (tm * tk * a_bytes + tk * tn * b_bytes + tm * tn * o_bytes) + acc + temps
                if vmem > MM_VMEM_BUDGET:
                    continue
                steps = (m // tm) * (n // tn) * nk
                cost = (m * k * a_bytes * (n // tn) + k * n * b_bytes * (m // tm) + m * n * o_bytes
                        + steps * MM_STEP_BYTES)
                if best is None or cost < best[0]:
                    best = (cost, tm, tn, tk)
    return best[1:]


def _mm(a, b, *, ta=False, tb=False, out_dtype=F32, name="mm"):
    m, k = (a.shape[1], a.shape[0]) if ta else a.shape
    n = b.shape[0] if tb else b.shape[1]
    o_bytes = jnp.dtype(out_dtype).itemsize
    tm, tn, tk = _mm_tiles(m, n, k, a.dtype.itemsize, b.dtype.itemsize, o_bytes)
    nk = k // tk
    sub = _pick(tn, (MM_SUB, 256))
    use_acc = nk > 1 and o_bytes != 4
    a_spec = pl.BlockSpec((tk, tm), lambda i, j, kk: (kk, i)) if ta else pl.BlockSpec((tm, tk), lambda i, j, kk: (i, kk))
    b_spec = pl.BlockSpec((tn, tk), lambda i, j, kk: (j, kk)) if tb else pl.BlockSpec((tk, tn), lambda i, j, kk: (kk, j))
    dims = (((1,), (1 if tb else 0,)), ((), ()))

    def body(a_ref, b_ref, o_ref, *scratch):
        kk = pl.program_id(2)
        acc_ref = scratch[0] if use_acc else o_ref
        if nk > 1:
            @pl.when(kk == 0)
            def _():
                acc_ref[...] = jnp.zeros_like(acc_ref)

        av = a_ref[...].astype(BF16)
        if ta:
            av = av.T
        for s in range(tn // sub):
            cols = slice(s * sub, (s + 1) * sub)
            bv = (b_ref[cols, :] if tb else b_ref[:, cols]).astype(BF16)
            part = lax.dot_general(av, bv, dims, preferred_element_type=F32)
            if nk == 1:
                o_ref[:, cols] = part.astype(out_dtype)
            else:
                acc_ref[:, cols] += part
        if use_acc:
            @pl.when(kk == nk - 1)
            def _():
                o_ref[...] = acc_ref[...].astype(out_dtype)

    return pl.pallas_call(
        body,
        name=name,
        grid=(m // tm, n // tn, nk),
        in_specs=[a_spec, b_spec],
        out_specs=pl.BlockSpec((tm, tn), lambda i, j, kk: (i, j)),
        out_shape=jax.ShapeDtypeStruct((m, n), out_dtype),
        scratch_shapes=[pltpu.VMEM((tm, tn), F32)] if use_acc else [],
        compiler_params=_params(("parallel", "parallel", "arbitrary")),
    )(a, b)


def _rms_fwd(x, w, *, out_dtype, name):
    t, d = x.shape
    tm = _pick(t, (512, 256, 128))

    def body(x_ref, w_ref, o_ref):
        xv = x_ref[...]
        rstd = lax.rsqrt(jnp.mean(xv * xv, axis=-1, keepdims=True) + EPS)
        o_ref[...] = (xv * rstd * w_ref[...]).astype(out_dtype)

    return pl.pallas_call(
        body,
        name=name,
        grid=(t // tm,),
        in_specs=[pl.BlockSpec((tm, d), lambda i: (i, 0)), pl.BlockSpec((1, d), lambda i: (0, 0))],
        out_specs=pl.BlockSpec((tm, d), lambda i: (i, 0)),
        out_shape=jax.ShapeDtypeStruct((t, d), out_dtype),
        compiler_params=_params(("parallel",)),
    )(x, w)


def _res_rms_fwd(x, f, w, *, name):
    t, d = x.shape
    tm = _pick(t, (512, 256, 128))

    def body(x_ref, f_ref, w_ref, o_ref):
        fv = f_ref[...]
        rstd = lax.rsqrt(jnp.mean(fv * fv, axis=-1, keepdims=True) + EPS)
        o_ref[...] = x_ref[...] + fv * rstd * w_ref[...]

    row = pl.BlockSpec((tm, d), lambda i: (i, 0))
    return pl.pallas_call(
        body,
        name=name,
        grid=(t // tm,),
        in_specs=[row, row, pl.BlockSpec((1, d), lambda i: (0, 0))],
        out_specs=row,
        out_shape=jax.ShapeDtypeStruct((t, d), F32),
        compiler_params=_params(("parallel",)),
    )(x, f, w)


def _rms_bwd(x, w, dy, resid, *, out_dtype=F32, name):
    t, d = x.shape
    tm = _pick(t, (512, 256, 128))
    has_res = resid is not None

    def body(*refs):
        if has_res:
            x_ref, w_ref, dy_ref, r_ref, dx_ref, dw_ref = refs
        else:
            x_ref, w_ref, dy_ref, dx_ref, dw_ref = refs
        xv = x_ref[...]
        dyv = dy_ref[...].astype(F32)
        rstd = lax.rsqrt(jnp.mean(xv * xv, axis=-1, keepdims=True) + EPS)
        xn = xv * rstd
        g = dyv * w_ref[...]
        dx = rstd * (g - xn * jnp.mean(g * xn, axis=-1, keepdims=True))
        if has_res:
            dx = dx + r_ref[...]
        dx_ref[...] = dx.astype(out_dtype)
        part = jnp.sum(dyv * xn, axis=0, keepdims=True)

        @pl.when(pl.program_id(0) == 0)
        def _():
            dw_ref[...] = part

        @pl.when(pl.program_id(0) > 0)
        def _():
            dw_ref[...] += part

    row = pl.BlockSpec((tm, d), lambda i: (i, 0))
    vec = pl.BlockSpec((1, d), lambda i: (0, 0))
    ins = [x, w, dy] + ([resid] if has_res else [])
    return pl.pallas_call(
        body,
        name=name,
        grid=(t // tm,),
        in_specs=[row, vec, row] + ([row] if has_res else []),
        out_specs=[row, vec],
        out_shape=[jax.ShapeDtypeStruct((t, d), out_dtype), jax.ShapeDtypeStruct((1, d), F32)],
        compiler_params=_params(("arbitrary",)),
    )(*ins)


def _loss_head(y, target, *, name="loss_head"):
    t, d = y.shape
    tm = _pick(t, (512, 256, 128))

    def body(y_ref, t_ref, dy_ref, l_ref):
        err = y_ref[...] - t_ref[...]
        dy_ref[...] = err * (1.0 / d)
        part = jnp.sum(jnp.sum(err * err, axis=-1, keepdims=True), axis=0, keepdims=True) * (0.5 / d)
        part = jnp.broadcast_to(part, (1, LANE))

        @pl.when(pl.program_id(0) == 0)
        def _():
            l_ref[...] = part

        @pl.when(pl.program_id(0) > 0)
        def _():
            l_ref[...] += part

    row = pl.BlockSpec((tm, d), lambda i: (i, 0))
    dy, l = pl.pallas_call(
        body,
        name=name,
        grid=(t // tm,),
        in_specs=[row, row],
        out_specs=[row, pl.BlockSpec((1, LANE), lambda i: (0, 0))],
        out_shape=[jax.ShapeDtypeStruct((t, d), F32), jax.ShapeDtypeStruct((1, LANE), F32)],
        compiler_params=_params(("arbitrary",)),
    )(y, target)
    return l[0, 0], dy


def _conv_taps(h, w_ref, k_taps):
    out = h * w_ref[k_taps - 1:k_taps, :]
    for k in range(k_taps - 1):
        out = out + _shift_down(h, k_taps - 1 - k) * w_ref[k:k + 1, :]
    return out


def _conv_taps_bwd(h, dhc, w_ref, k_taps):
    dh = dhc * w_ref[k_taps - 1:k_taps, :]
    dws = []
    for k in range(k_taps - 1):
        up = _shift_up(dhc, k_taps - 1 - k)
        dh = dh + up * w_ref[k:k + 1, :]
        dws.append(jnp.sum(up * h, axis=0, keepdims=True))
    dws.append(jnp.sum(dhc * h, axis=0, keepdims=True))
    return dh, jnp.concatenate(dws, axis=0)


FFN_TC = 256


def _interleave(w, tc=FFN_TC):
    f = w.shape[-1] // 2
    lead = w.shape[:-1]
    return jnp.swapaxes(w.reshape(lead + (2, f // tc, tc)), -3, -2).reshape(lead + (2 * f,))


def _deinterleave(w, tc=FFN_TC):
    f = w.shape[-1] // 2
    lead = w.shape[:-1]
    return jnp.swapaxes(w.reshape(lead + (f // tc, 2, tc)), -3, -2).reshape(lead + (2 * f,))


def _ffn_up_act(n, w_up, conv_w, conv_b, n_seq, *, name):
    t, d = n.shape
    f2 = w_up.shape[1]
    seq = t // n_seq
    tc = FFN_TC
    nj = f2 // (2 * tc)
    k_taps = conv_w.shape[0]

    def body(n_ref, wu_ref, w_ref, b_ref, h_ref, hc_ref, o_ref):
        h = jnp.dot(n_ref[...], wu_ref[...], preferred_element_type=F32)
        h_ref[...] = h.astype(BF16)
        hc = _conv_taps(h, w_ref, k_taps) + b_ref[...]
        hc_ref[...] = hc.astype(BF16)
        o_ref[...] = (_silu(hc[:, :tc]) * hc[:, tc:]).astype(BF16)

    wide = pl.BlockSpec((seq, 2 * tc), lambda b, j: (b, j))
    return pl.pallas_call(
        body,
        name=name,
        grid=(n_seq, nj),
        in_specs=[
            pl.BlockSpec((seq, d), lambda b, j: (b, 0)),
            pl.BlockSpec((d, 2 * tc), lambda b, j: (0, j)),
            pl.BlockSpec((k_taps, 2 * tc), lambda b, j: (0, j)),
            pl.BlockSpec((1, 2 * tc), lambda b, j: (0, j)),
        ],
        out_specs=[wide, wide, pl.BlockSpec((seq, tc), lambda b, j: (b, j))],
        out_shape=[jax.ShapeDtypeStruct((t, f2), BF16), jax.ShapeDtypeStruct((t, f2), BF16),
                   jax.ShapeDtypeStruct((t, f2 // 2), BF16)],
        compiler_params=_params(("parallel", "arbitrary")),
    )(n, w_up, conv_w, conv_b)


def _ffn_down_bx_act_bwd(df, w_down, h, hc, conv_w, n_seq, *, name):
    t, d = df.shape
    f2 = h.shape[1]
    seq = t // n_seq
    tc = FFN_TC
    nj = f2 // (2 * tc)
    k_taps = conv_w.shape[0]

    def body(df_ref, wd_ref, h_ref, hc_ref, w_ref, dh_ref, dw_ref, db_ref):
        dav = lax.dot_general(df_ref[...], wd_ref[...], (((1,), (1,)), ((), ())), preferred_element_type=F32)
        hcv = hc_ref[...].astype(F32)
        gate, val = hcv[:, :tc], hcv[:, tc:]
        dhc = jnp.concatenate([dav * val * _dsilu(gate), dav * _silu(gate)], axis=1)
        dh, dw = _conv_taps_bwd(h_ref[...].astype(F32), dhc, w_ref, k_taps)
        dh_ref[...] = dh.astype(BF16)
        dw_ref[0] = dw
        db_ref[0] = jnp.sum(dhc, axis=0, keepdims=True)

    wide = pl.BlockSpec((seq, 2 * tc), lambda b, j: (b, j))
    dh, dw, db = pl.pallas_call(
        body,
        name=name,
        grid=(n_seq, nj),
        in_specs=[
            pl.BlockSpec((seq, d), lambda b, j: (b, 0)),
            pl.BlockSpec((tc, d), lambda b, j: (j, 0)),
            wide, wide,
            pl.BlockSpec((k_taps, 2 * tc), lambda b, j: (0, j)),
        ],
        out_specs=[
            wide,
            pl.BlockSpec((1, k_taps, 2 * tc), lambda b, j: (b, 0, j)),
            pl.BlockSpec((1, 1, 2 * tc), lambda b, j: (b, 0, j)),
        ],
        out_shape=[
            jax.ShapeDtypeStruct((t, f2), BF16),
            jax.ShapeDtypeStruct((n_seq, k_taps, f2), F32),
            jax.ShapeDtypeStruct((n_seq, 1, f2), F32),
        ],
        compiler_params=_params(("parallel", "arbitrary")),
    )(df, w_down, h, hc, conv_w)
    return dh, jnp.sum(dw, axis=0), jnp.sum(db, axis=0)


def _window_mixed(u, window):
    s = u
    step = 1
    while step < window:
        s = s + _shift_down(s, step)
        step *= 2
    rows = lax.broadcasted_iota(jnp.int32, u.shape, 0)
    inv_cnt = 1.0 / jnp.minimum(rows + 1, window).astype(F32)
    return s * inv_cnt - u, inv_cnt


def _window_mixed_bwd(dmixed, inv_cnt, window):
    r = dmixed * inv_cnt
    s = r
    step = 1
    while step < window:
        s = s + _shift_up(s, step)
        step *= 2
    return s - dmixed


def _pool_fwd(u, w, scale, n_seq, *, name):
    t, d = u.shape
    seq = t // n_seq
    n_g, dg, _ = w.shape

    def body(u_ref, w_ref, s_ref, o_ref):
        for k, window in enumerate(POOL_WINDOWS):
            @pl.when(pl.program_id(1) == k)
            def _(window=window):
                mixed, _ = _window_mixed(u_ref[...], window)
                pre = jnp.dot(mixed.astype(BF16), w_ref[0].astype(BF16), preferred_element_type=F32)
                o_ref[...] = pre * s_ref[...]

    return pl.pallas_call(
        body,
        name=name,
        grid=(n_seq, n_g),
        in_specs=[
            pl.BlockSpec((seq, dg), lambda b, g: (b, g)),
            pl.BlockSpec((1, dg, dg), lambda b, g: (g, 0, 0)),
            pl.BlockSpec((1, dg), lambda b, g: (0, g)),
        ],
        out_specs=pl.BlockSpec((seq, dg), lambda b, g: (b, g)),
        out_shape=jax.ShapeDtypeStruct((t, d), F32),
        compiler_params=_params(("parallel", "parallel")),
    )(u, w, scale)


def _pool_bwd(u, w, scale, dout, n_seq, *, name):
    t, d = u.shape
    seq = t // n_seq
    n_g, dg, _ = w.shape

    def body(u_ref, w_ref, s_ref, do_ref, du_ref, dw_ref, ds_ref):
        group = pl.program_id(0)
        first = pl.program_id(1) == 0
        for k, window in enumerate(POOL_WINDOWS):
            @pl.when(group == k)
            def _(window=window):
                mixed, inv_cnt = _window_mixed(u_ref[...], window)
                mixed_b = mixed.astype(BF16)
                w_b = w_ref[0].astype(BF16)
                dov = do_ref[...]
                pre = jnp.dot(mixed_b, w_b, preferred_element_type=F32)
                dsc = jnp.sum(dov * pre, axis=0, keepdims=True)
                dpre = (dov * s_ref[...]).astype(BF16)
                dw = lax.dot_general(mixed_b, dpre, (((0,), (0,)), ((), ())), preferred_element_type=F32)
                dmixed = lax.dot_general(dpre, w_b, (((1,), (1,)), ((), ())), preferred_element_type=F32)
                du_ref[...] = _window_mixed_bwd(dmixed, inv_cnt, window)

                @pl.when(first)
                def _():
                    dw_ref[0] = dw
                    ds_ref[...] = dsc

                @pl.when(jnp.logical_not(first))
                def _():
                    dw_ref[0] += dw
                    ds_ref[...] += dsc

    return pl.pallas_call(
        body,
        name=name,
        grid=(n_g, n_seq),
        in_specs=[
            pl.BlockSpec((seq, dg), lambda g, b: (b, g)),
            pl.BlockSpec((1, dg, dg), lambda g, b: (g, 0, 0)),
            pl.BlockSpec((1, dg), lambda g, b: (0, g)),
            pl.BlockSpec((seq, dg), lambda g, b: (b, g)),
        ],
        out_specs=[
            pl.BlockSpec((seq, dg), lambda g, b: (b, g)),
            pl.BlockSpec((1, dg, dg), lambda g, b: (g, 0, 0)),
            pl.BlockSpec((1, dg), lambda g, b: (0, g)),
        ],
        out_shape=[
            jax.ShapeDtypeStruct((t, d), F32),
            jax.ShapeDtypeStruct((n_g, dg, dg), F32),
            jax.ShapeDtypeStruct((1, d), F32),
        ],
        compiler_params=_params(("parallel", "arbitrary")),
    )(u, w, scale, dout)


def _adamw(w, g, m, v, *, name):
    shape = w.shape
    c = shape[-1]
    r = w.size // c
    tm = _pick(r, (512, 256, 128, 64, 32, 16, 8))

    def body(w_ref, g_ref, m_ref, v_ref, d_ref, nm_ref, nv_ref):
        gv = g_ref[...]
        nm = ADAM_B1 * m_ref[...] + (1.0 - ADAM_B1) * gv
        nv = ADAM_B2 * v_ref[...] + (1.0 - ADAM_B2) * (gv * gv)
        m_hat = nm / (1.0 - ADAM_B1 ** ADAM_STEP)
        v_hat = nv / (1.0 - ADAM_B2 ** ADAM_STEP)
        d_ref[...] = -ADAM_LR * (m_hat / (jnp.sqrt(v_hat) + ADAM_EPS) + ADAM_WD * w_ref[...])
        nm_ref[...] = nm
        nv_ref[...] = nv

    blk = pl.BlockSpec((tm, c), lambda i: (i, 0))
    out = jax.ShapeDtypeStruct((r, c), F32)
    res = pl.pallas_call(
        body,
        name=name,
        grid=(r // tm,),
        in_specs=[blk] * 4,
        out_specs=[blk] * 3,
        out_shape=[out] * 3,
        compiler_params=_params(("parallel",)),
    )(w.reshape(r, c), g.reshape(r, c), m.reshape(r, c), v.reshape(r, c))
    return tuple(a.reshape(shape) for a in res)


CONV_TC = 256


def _ssd_conv_fwd(proj, col0, n_cols, conv_w, conv_b, n_seq, *, name):
    t = proj.shape[0]
    seq = t // n_seq
    tc = CONV_TC
    off = col0 // tc
    k_taps = conv_w.shape[0]

    def body(h_ref, w_ref, b_ref, o_ref, pre_ref):
        pre = _conv_taps(h_ref[...], w_ref, k_taps) + b_ref[...]
        pre_ref[...] = pre.astype(BF16)
        o_ref[...] = _silu(pre)

    return pl.pallas_call(
        body,
        name=name,
        grid=(n_seq, n_cols // tc),
        in_specs=[
            pl.BlockSpec((seq, tc), lambda b, j: (b, j + off)),
            pl.BlockSpec((k_taps, tc), lambda b, j: (0, j)),
            pl.BlockSpec((1, tc), lambda b, j: (0, j)),
        ],
        out_specs=[pl.BlockSpec((seq, tc), lambda b, j: (b, j))] * 2,
        out_shape=[jax.ShapeDtypeStruct((t, n_cols), F32), jax.ShapeDtypeStruct((t, n_cols), BF16)],
        compiler_params=_params(("parallel", "parallel")),
    )(proj, conv_w, conv_b)


def _ssd_conv_bwd(proj, col0, conv_w, pre, dparts, dproj, n_seq, *, name):
    t = proj.shape[0]
    seq = t // n_seq
    tc = CONV_TC
    off = col0 // tc
    k_taps = conv_w.shape[0]
    widths = [d.shape[1] // tc for d in dparts]
    starts = [sum(widths[:i]) for i in range(len(widths))]
    n_blocks = sum(widths)
    n_parts = len(dparts)

    def body(h_ref, w_ref, pre_ref, *rest):
        part_refs = rest[:n_parts]
        dh_ref, dw_ref, db_ref = rest[n_parts + 1:]
        j = pl.program_id(0)
        da = part_refs[-1][...]
        for i in reversed(range(n_parts - 1)):
            da = jnp.where(j < starts[i + 1], part_refs[i][...], da)
        dhc = da * _dsilu(pre_ref[...].astype(F32))
        dh, dw = _conv_taps_bwd(h_ref[...], dhc, w_ref, k_taps)
        dh_ref[...] = dh.astype(BF16)
        db = jnp.sum(dhc, axis=0, keepdims=True)

        @pl.when(pl.program_id(1) == 0)
        def _():
            dw_ref[...] = dw
            db_ref[...] = db

        @pl.when(pl.program_id(1) > 0)
        def _():
            dw_ref[...] += dw
            db_ref[...] += db

    def part_spec(start, width):
        return pl.BlockSpec((seq, tc), lambda j, b: (b, jnp.clip(j - start, 0, width - 1)))

    n_cols = n_blocks * tc
    return pl.pallas_call(
        body,
        name=name,
        grid=(n_blocks, n_seq),
        in_specs=[
            pl.BlockSpec((seq, tc), lambda j, b: (b, j + off)),
            pl.BlockSpec((k_taps, tc), lambda j, b: (0, j)),
            pl.BlockSpec((seq, tc), lambda j, b: (b, j)),
        ] + [part_spec(st, wd) for st, wd in zip(starts, widths)] + [ANY],
        out_specs=[
            pl.BlockSpec((seq, tc), lambda j, b: (b, j + off)),
            pl.BlockSpec((k_taps, tc), lambda j, b: (0, j)),
            pl.BlockSpec((1, tc), lambda j, b: (0, j)),
        ],
        out_shape=[
            jax.ShapeDtypeStruct(dproj.shape, BF16),
            jax.ShapeDtypeStruct((k_taps, n_cols), F32),
            jax.ShapeDtypeStruct((1, n_cols), F32),
        ],
        input_output_aliases={3 + n_parts: 0},
        compiler_params=_params(("parallel", "arbitrary")),
    )(proj, conv_w, pre, *dparts, dproj)


def _fill_cols(buf, src, col0, *, name):
    t, c = src.shape
    tm = _pick(t, (1024, 512, 256, 128))

    def body(s_ref, b_ref, o_ref):
        o_ref[...] = s_ref[...].astype(o_ref.dtype)

    return pl.pallas_call(
        body,
        name=name,
        grid=(t // tm,),
        in_specs=[pl.BlockSpec((tm, c), lambda i: (i, 0)), ANY],
        out_specs=pl.BlockSpec((tm, c), lambda i: (i, col0 // c)),
        out_shape=jax.ShapeDtypeStruct(buf.shape, buf.dtype),
        input_output_aliases={1: 0},
        compiler_params=_params(("parallel",)),
    )(src, buf)


def _softplus(x):
    return jnp.maximum(x, 0.0) + jnp.log(1.0 + jnp.exp(-jnp.abs(x)))


def _chunk_decay(dtraw, bias, alog):
    q = dtraw.shape[0]
    dt = _softplus(dtraw + bias)
    a = -jnp.exp(alog)
    rows = lax.broadcasted_iota(jnp.int32, (q, q), 0)
    cols = lax.broadcasted_iota(jnp.int32, (q, q), 1)
    lower = rows >= cols
    acum = jnp.dot(lower.astype(F32), dt * a, precision=lax.Precision.HIGHEST, preferred_element_type=F32)
    return dt, a, acum, acum.T, lower


def _dot_exact(v, sel):
    hi = v.astype(BF16)
    r1 = v - hi.astype(F32)
    mid = r1.astype(BF16)
    lo = (r1 - mid.astype(F32)).astype(BF16)
    return (jnp.dot(hi, sel, preferred_element_type=F32) + jnp.dot(mid, sel, preferred_element_type=F32)
            + jnp.dot(lo, sel, preferred_element_type=F32))


def _head_selectors(gw, p):
    sum_heads = (lax.broadcasted_iota(jnp.int32, (gw, LANE), 0) // p == lax.broadcasted_iota(jnp.int32, (gw, LANE), 1))
    spread = (lax.broadcasted_iota(jnp.int32, (LANE, gw), 0) == lax.broadcasted_iota(jnp.int32, (LANE, gw), 1) // p)
    return sum_heads.astype(BF16), spread.astype(BF16)


def _row_spread(v, spread):
    return _dot_exact(jnp.broadcast_to(v, (8, v.shape[1])), spread)[0:1, :]


def _head_pad(v, r_heads):
    lead = v.shape[:-1]
    vg = v.reshape(lead + (N_SSD_GROUPS, r_heads))
    vg = jnp.pad(vg, [(0, 0)] * len(lead) + [(0, 0), (0, LANE - r_heads)])
    out = vg.reshape(lead + (N_SSD_GROUPS * LANE,))
    return out[None] if out.ndim == 1 else out


def _head_unpad(v, r_heads):
    lead = v.shape[:-1]
    out = v.reshape(lead + (N_SSD_GROUPS, LANE))[..., :r_heads].reshape(lead + (N_SSD_GROUPS * r_heads,))
    return out[0] if (len(lead) == 1 and lead[0] == 1) else out


def _ssd_w_in_layout(w_in, d_inner, d_xbc, r_heads):
    main = w_in[:, :d_inner + d_xbc]
    return jnp.concatenate([main, _head_pad(w_in[:, d_inner + d_xbc:], r_heads)], axis=1)


def _ssd_w_in_unlayout(w, d_inner, d_xbc, r_heads):
    main = w[:, :d_inner + d_xbc]
    return jnp.concatenate([main, _head_unpad(w[:, d_inner + d_xbc:], r_heads)], axis=1)


def _ssd_dims(proj, xbc):
    d_xbc = xbc.shape[1]
    d_inner = d_xbc - 2 * N_SSD_GROUPS * D_STATE
    gw = d_inner // N_SSD_GROUPS
    return d_inner, d_xbc, gw, gw // HEAD_DIM


def _ssd_fwd(proj, xbc, bias_p, alog_p, dskip_p, norm_w, n_seq, *, name):
    t = proj.shape[0]
    d_inner, d_xbc, gw, r_heads = _ssd_dims(proj, xbc)
    q, n, n_g, p = CHUNK, D_STATE, N_SSD_GROUPS, HEAD_DIM
    seq = t // n_seq
    nc = seq // q
    dt_blk0 = (d_inner + d_xbc) // LANE

    def body(x_ref, b_ref, c_ref, z_ref, dtr_ref, bias_ref, alog_ref, dsk_ref, nw_ref, yn_ref, y_ref, hs_ref, h_scr):
        @pl.when(pl.program_id(2) == 0)
        def _():
            h_scr[...] = jnp.zeros_like(h_scr)

        dt, a, acum, acum_t, lower = _chunk_decay(dtr_ref[...], bias_ref[...], alog_ref[...])
        x = x_ref[...]
        bb = b_ref[...].astype(BF16)
        cb = c_ref[...].astype(BF16)
        g_mat = lax.dot_general(cb, bb, (((1,), (1,)), ((), ())), preferred_element_type=F32)
        h_prev = h_scr[...]
        hs_ref[...] = h_prev
        c_h = jnp.dot(cb, h_prev.astype(BF16), preferred_element_type=F32)
        _, spread = _head_selectors(gw, p)
        acum_s = _dot_exact(acum, spread)
        a_last_s = acum_s[q - 1:q, :]
        xdt = x * _dot_exact(dt, spread)
        xdt_b = xdt.astype(BF16)
        ys = []
        for h in range(r_heads):
            decay = jnp.exp(jnp.where(lower, acum[:, h:h + 1] - acum_t[h:h + 1, :], -jnp.inf))
            ys.append(jnp.dot((g_mat * decay).astype(BF16), xdt_b[:, h * p:(h + 1) * p], preferred_element_type=F32))
        y = jnp.concatenate(ys, axis=1) + jnp.exp(acum_s) * c_h + _row_spread(dsk_ref[...], spread) * x
        xd = xdt * jnp.exp(a_last_s - acum_s)
        states = lax.dot_general(bb, xd.astype(BF16), (((0,), (0,)), ((), ())), preferred_element_type=F32)
        h_scr[...] = h_prev * jnp.exp(a_last_s) + states
        y_ref[...] = y
        gated = y * _silu(z_ref[...])
        rstd = lax.rsqrt(jnp.mean(gated * gated, axis=-1, keepdims=True) + EPS)
        yn_ref[...] = (gated * rstd * nw_ref[...]).astype(BF16)

    row = lambda b, g, c: b * nc + c
    vec = pl.BlockSpec((1, LANE), lambda b, g, c: (0, g))
    return pl.pallas_call(
        body,
        name=name,
        grid=(n_seq, n_g, nc),
        in_specs=[
            pl.BlockSpec((q, gw), lambda b, g, c: (row(b, g, c), g)),
            pl.BlockSpec((q, n), lambda b, g, c: (row(b, g, c), d_inner // n + g)),
            pl.BlockSpec((q, n), lambda b, g, c: (row(b, g, c), d_inner // n + n_g + g)),
            pl.BlockSpec((q, gw), lambda b, g, c: (row(b, g, c), g)),
            pl.BlockSpec((q, LANE), lambda b, g, c: (row(b, g, c), dt_blk0 + g)),
            vec, vec, vec,
            pl.BlockSpec((1, gw), lambda b, g, c: (0, g)),
        ],
        out_specs=[
            pl.BlockSpec((q, gw), lambda b, g, c: (row(b, g, c), g)),
            pl.BlockSpec((q, gw), lambda b, g, c: (row(b, g, c), g)),
            pl.BlockSpec((n, gw), lambda b, g, c: (row(b, g, c), g)),
        ],
        out_shape=[
            jax.ShapeDtypeStruct((t, d_inner), BF16),
            jax.ShapeDtypeStruct((t, d_inner), F32),
            jax.ShapeDtypeStruct((n_seq * nc * n, d_inner), F32),
        ],
        scratch_shapes=[pltpu.VMEM((n, gw), F32)],
        compiler_params=_params(("parallel", "parallel", "arbitrary")),
    )(xbc, xbc, xbc, proj, proj, bias_p, alog_p, dskip_p, norm_w)


def _ssd_bwd(proj, xbc, hs, y, dyn, bias_p, alog_p, dskip_p, norm_w, n_seq, *, name):
    t = proj.shape[0]
    d_inner, d_xbc, gw, r_heads = _ssd_dims(proj, xbc)
    q, n, n_g, p = CHUNK, D_STATE, N_SSD_GROUPS, HEAD_DIM
    seq = t // n_seq
    nc = seq // q
    dt_blk0 = (d_inner + d_xbc) // LANE

    def body(x_ref, b_ref, c_ref, z_ref, dtr_ref, bias_ref, alog_ref, dsk_ref, nw_ref, hs_ref, y_ref, dyn_ref,
             dx_ref, db_ref, dc_ref, dz_ref, ddtr_ref, dnw_ref, dbias_ref, dalog_ref, ddsk_ref, dh_scr):
        first = jnp.logical_and(pl.program_id(1) == 0, pl.program_id(2) == 0)

        @pl.when(pl.program_id(2) == 0)
        def _():
            dh_scr[...] = jnp.zeros_like(dh_scr)

        dtraw = dtr_ref[...]
        dt, a, acum, acum_t, lower = _chunk_decay(dtraw, bias_ref[...], alog_ref[...])
        x = x_ref[...]
        bb = b_ref[...].astype(BF16)
        cb = c_ref[...].astype(BF16)
        g_mat = lax.dot_general(cb, bb, (((1,), (1,)), ((), ())), preferred_element_type=F32)

        yv = y_ref[...]
        z = z_ref[...]
        sz = _silu(z)
        gated = yv * sz
        rstd = lax.rsqrt(jnp.mean(gated * gated, axis=-1, keepdims=True) + EPS)
        gn = gated * rstd
        dynv = dyn_ref[...]
        gwt = dynv * nw_ref[...]
        dgated = rstd * (gwt - gn * jnp.mean(gwt * gn, axis=-1, keepdims=True))
        dnw = jnp.sum(dynv * gn, axis=0, keepdims=True)
        dy = dgated * sz
        dz_ref[...] = (dgated * yv * _dsilu(z)).astype(BF16)

        h_prev = hs_ref[...]
        h_prev_b = h_prev.astype(BF16)
        ds = dh_scr[...]
        ds_b = ds.astype(BF16)
        sum_heads, spread = _head_selectors(gw, p)
        acum_s = _dot_exact(acum, spread)
        a_last_s = acum_s[q - 1:q, :]
        dt_s = _dot_exact(dt, spread)
        dsk_s = _row_spread(dsk_ref[...], spread)
        dte_s = jnp.exp(a_last_s - acum_s)
        cd_s = jnp.exp(a_last_s)
        xdt = x * dt_s
        xdt_b = xdt.astype(BF16)
        dy_b = dy.astype(BF16)
        gt_mat = lax.dot_general(bb, cb, (((1,), (1,)), ((), ())), preferred_element_type=F32)
        upper = lax.broadcasted_iota(jnp.int32, (q, q), 0) <= lax.broadcasted_iota(jnp.int32, (q, q), 1)
        dg = jnp.zeros((q, q), F32)
        dxdts, w_diffs = [], []
        for h in range(r_heads):
            hsl = slice(h * p, (h + 1) * p)
            diff = acum[:, h:h + 1] - acum_t[h:h + 1, :]
            decay = jnp.exp(jnp.where(lower, diff, -jnp.inf))
            decay_t = jnp.exp(jnp.where(upper, -diff, -jnp.inf))
            mt_mat = gt_mat * decay_t
            dm = lax.dot_general(dy_b[:, hsl], xdt_b[:, hsl], (((1,), (1,)), ((), ())), preferred_element_type=F32)
            dm_t = lax.dot_general(xdt_b[:, hsl], dy_b[:, hsl], (((1,), (1,)), ((), ())), preferred_element_type=F32)
            dg = dg + dm * decay
            dxdts.append(jnp.dot(mt_mat.astype(BF16), dy_b[:, hsl], preferred_element_type=F32))
            w_diffs.append(dm * (g_mat * decay) - dm_t * mt_mat)
        sel_q = (lax.broadcasted_iota(jnp.int32, (r_heads * q, LANE), 0) // q
                 == lax.broadcasted_iota(jnp.int32, (r_heads * q, LANE), 1)).astype(BF16)
        dacum_diag = _dot_exact(jnp.concatenate(w_diffs, axis=1), sel_q)
        c_h = jnp.dot(cb, h_prev_b, preferred_element_type=F32)
        dxd = jnp.dot(bb, ds_b, preferred_element_type=F32)
        dxdt = jnp.concatenate(dxdts, axis=1) + dxd * dte_s
        dye = dy * jnp.exp(acum_s)
        dye_b = dye.astype(BF16)
        xd = xdt * dte_s
        xd_b = xd.astype(BF16)
        dg_b = dg.astype(BF16)
        dx_ref[...] = dxdt * dt_s + dsk_s * dy
        dc_ref[...] = (jnp.dot(dg_b, bb, preferred_element_type=F32)
                       + lax.dot_general(dye_b, h_prev_b, (((1,), (1,)), ((), ())), preferred_element_type=F32))
        db_ref[...] = (lax.dot_general(dg_b, cb, (((0,), (0,)), ((), ())), preferred_element_type=F32)
                       + lax.dot_general(xd_b, ds_b, (((1,), (1,)), ((), ())), preferred_element_type=F32))
        dh_scr[...] = ds * cd_s + lax.dot_general(cb, dye_b, (((0,), (0,)), ((), ())), preferred_element_type=F32)
        ddt_cols = _dot_exact(x * dxdt, sum_heads)
        dacum_y = _dot_exact(dye * c_h - dxd * xd, sum_heads)
        col_sums = jnp.concatenate([
            jnp.sum(dxd * xd, axis=0, keepdims=True) + jnp.sum(ds * h_prev, axis=0, keepdims=True) * cd_s,
            jnp.sum(dy * x, axis=0, keepdims=True),
            jnp.zeros((6, gw), F32)], axis=0)
        col_sums = _dot_exact(col_sums, sum_heads)
        ddsk = col_sums[1:2, :]
        rows_q = lax.broadcasted_iota(jnp.int32, (q, LANE), 0)
        dacum = dacum_diag + dacum_y + jnp.where(rows_q == q - 1, col_sums[0:1, :], 0.0)
        dadt = jnp.dot(upper.astype(F32), dacum, precision=lax.Precision.HIGHEST, preferred_element_type=F32)
        ddt = dadt * a + ddt_cols
        ddtr = ddt * _sigmoid(dtraw + bias_ref[...])
        ddtr_ref[...] = ddtr
        dbias = jnp.sum(ddtr, axis=0, keepdims=True)
        dalog = jnp.sum(dadt * dt, axis=0, keepdims=True) * a

        @pl.when(first)
        def _():
            dnw_ref[...] = dnw
            dbias_ref[...] = dbias
            dalog_ref[...] = dalog
            ddsk_ref[...] = ddsk

        @pl.when(jnp.logical_not(first))
        def _():
            dnw_ref[...] += dnw
            dbias_ref[...] += dbias
            dalog_ref[...] += dalog
            ddsk_ref[...] += ddsk

    row = lambda g, b, c: b * nc + (nc - 1 - c)
    vec = pl.BlockSpec((1, LANE), lambda g, b, c: (0, g))
    wide = pl.BlockSpec((q, gw), lambda g, b, c: (row(g, b, c), g))
    narrow = pl.BlockSpec((q, n), lambda g, b, c: (row(g, b, c), g))
    return pl.pallas_call(
        body,
        name=name,
        grid=(n_g, n_seq, nc),
        in_specs=[
            wide,
            pl.BlockSpec((q, n), lambda g, b, c: (row(g, b, c), d_inner // n + g)),
            pl.BlockSpec((q, n), lambda g, b, c: (row(g, b, c), d_inner // n + n_g + g)),
            wide,
            pl.BlockSpec((q, LANE), lambda g, b, c: (row(g, b, c), dt_blk0 + g)),
            vec, vec, vec,
            pl.BlockSpec((1, gw), lambda g, b, c: (0, g)),
            pl.BlockSpec((n, gw), lambda g, b, c: (row(g, b, c), g)),
            wide, wide,
        ],
        out_specs=[
            wide, narrow, narrow, wide, narrow,
            pl.BlockSpec((1, gw), lambda g, b, c: (0, g)),
            vec, vec, vec,
        ],
        out_shape=[
            jax.ShapeDtypeStruct((t, d_inner), F32),
            jax.ShapeDtypeStruct((t, n_g * n), F32),
            jax.ShapeDtypeStruct((t, n_g * n), F32),
            jax.ShapeDtypeStruct(proj.shape, BF16),
            jax.ShapeDtypeStruct((t, n_g * LANE), F32),
            jax.ShapeDtypeStruct((1, d_inner), F32),
            jax.ShapeDtypeStruct((1, n_g * LANE), F32),
            jax.ShapeDtypeStruct((1, n_g * LANE), F32),
            jax.ShapeDtypeStruct((1, n_g * LANE), F32),
        ],
        scratch_shapes=[pltpu.VMEM((n, gw), F32)],
        compiler_params=_params(("parallel", "arbitrary", "arbitrary")),
    )(xbc, xbc, xbc, proj, proj, bias_p, alog_p, dskip_p, norm_w, hs, y, dyn)


MESH_IDS = pl.DeviceIdType.MESH


def _my_index():
    return 4 * lax.axis_index("x") + 2 * lax.axis_index("y") + lax.axis_index("c")


def _all_gather(shard, *, name):
    def body(x_ref, out_ref, send_sems, recv_sems, local_sem):
        x, y, c = lax.axis_index("x"), lax.axis_index("y"), lax.axis_index("c")
        me, sibling = (x, y, c), (x, y, 1 - c)
        chips = [(1 - x, y), (x, 1 - y), (1 - x, 1 - y)]

        def blk(px, py, pc):
            return out_ref.at[4 * px + 2 * py + pc]

        def copy(k, block, to, src=None):
            return pltpu.make_async_remote_copy(
                src_ref=blk(*block) if src is None else src, dst_ref=blk(*block),
                send_sem=send_sems.at[k], recv_sem=recv_sems.at[k], device_id=to, device_id_type=MESH_IDS)

        mine = pltpu.make_async_copy(x_ref, blk(*me), local_sem)
        mine.start()
        first = [copy(0, me, sibling, src=x_ref)]
        first += [copy(1 + j, me, (*chip, c), src=x_ref) for j, chip in enumerate(chips)]
        for cp in first:
            cp.start()
        passed = [copy(4 + j, (*chip, c), sibling) for j, chip in enumerate(chips)]
        for j, chip in enumerate(chips):
            copy(1 + j, (*chip, c), me).wait_recv()
            passed[j].start()
        copy(0, sibling, me).wait_recv()
        for j, chip in enumerate(chips):
            copy(4 + j, (*chip, 1 - c), me).wait_recv()
        for cp in first + passed:
            cp.wait_send()
        mine.wait()

    return pl.pallas_call(
        body,
        name=name,
        in_specs=[ANY],
        out_specs=ANY,
        out_shape=jax.ShapeDtypeStruct((N_DEV,) + shard.shape, shard.dtype),
        scratch_shapes=[pltpu.SemaphoreType.DMA((7,)), pltpu.SemaphoreType.DMA((7,)), pltpu.SemaphoreType.DMA],
    )(shard)


HBM_SPEC = pl.BlockSpec(memory_space=pltpu.HBM)
SEM_SPEC = pl.BlockSpec(memory_space=pltpu.SEMAPHORE)
SPLIT_COPY_PARAMS = pltpu.CompilerParams(has_side_effects=pltpu.SideEffectType.DATAFLOW_SIDE_EFFECTING)


def _peer_list():
    x, y, c = lax.axis_index("x"), lax.axis_index("y"), lax.axis_index("c")
    peers = []
    for k in range(1, N_DEV):
        px = 1 - x if k & 4 else x
        py = 1 - y if k & 2 else y
        pc = 1 - c if k & 1 else c
        peers.append(((px, py, pc), 4 * px + 2 * py + pc))
    return 4 * x + 2 * y + c, peers


def _push_copies(src_refs, land_refs, send_sems, recv_sems, blockwise):
    me, peers = _peer_list()
    copies = []
    for a, (src_ref, land_ref) in enumerate(zip(src_refs, land_refs)):
        for k, (dev, idx) in enumerate(peers):
            sem = a * (N_DEV - 1) + k
            src = src_ref.at[idx] if blockwise else src_ref
            copies.append(tuple(
                pltpu.make_async_remote_copy(src_ref=src, dst_ref=land_ref.at[slot], send_sem=send_sems.at[sem],
                                             recv_sem=recv_sems.at[sem], device_id=dev, device_id_type=MESH_IDS)
                for slot in (me, idx)))
    return copies


def _push_start(srcs, blockwise, after, *, name):
    n = len(srcs)
    blocks = [s_.shape[1:] if blockwise else s_.shape for s_ in srcs]

    def body(*refs):
        src_refs, land_refs = refs[:n], refs[n:2 * n]
        send_sems, recv_sems = refs[2 * n + 1], refs[2 * n + 2]
        token = refs[-1]
        for send, _ in _push_copies(src_refs, land_refs, send_sems, recv_sems, blockwise):
            send.start()
        token[...] = jnp.zeros_like(token)

    n_sem = n * (N_DEV - 1)
    lands = [lax.empty((N_DEV,) + b, s_.dtype) for b, s_ in zip(blocks, srcs)]
    out = pl.pallas_call(
        body,
        name=name,
        in_specs=[HBM_SPEC] * (2 * n) + [ANY],
        out_specs=(SEM_SPEC, SEM_SPEC) + (HBM_SPEC,) * (2 * n) + (pl.BlockSpec(memory_space=pltpu.VMEM),),
        out_shape=(pltpu.SemaphoreType.DMA((n_sem,)), pltpu.SemaphoreType.DMA((n_sem,)))
        + tuple(pltpu.HBM(a.shape, a.dtype) for a in list(srcs) + lands)
        + (jax.ShapeDtypeStruct((8, LANE), F32),),
        input_output_aliases={i: 2 + i for i in range(2 * n)},
        compiler_params=SPLIT_COPY_PARAMS,
    )(*[pltpu.with_memory_space_constraint(a, pltpu.HBM) for a in list(srcs) + lands], after)
    return out[0], out[1], out[2:2 + n], out[2 + n:2 + 2 * n], out[-1]


def _push_wait(send_sems, recv_sems, srcs, lands, blockwise, after, *, name):
    n = len(srcs)

    def body(*refs):
        src_refs, land_refs = refs[:n], refs[n:2 * n]
        send_sems, recv_sems = refs[2 * n], refs[2 * n + 1]
        for send, recv in _push_copies(src_refs, land_refs, send_sems, recv_sems, blockwise):
            send.wait_send()
            recv.wait_recv()

    out = pl.pallas_call(
        body,
        name=name,
        in_specs=[HBM_SPEC] * (2 * n) + [SEM_SPEC, SEM_SPEC, ANY],
        out_specs=(HBM_SPEC,) * (2 * n),
        out_shape=tuple(pltpu.HBM(a.shape, a.dtype) for a in list(srcs) + list(lands)),
        input_output_aliases={i: i for i in range(2 * n)},
        compiler_params=SPLIT_COPY_PARAMS,
    )(*srcs, *lands, send_sems, recv_sems, after)
    return out[n:]


def _with_own_slot(landing, own):
    slot = lax.broadcasted_iota(jnp.int32, (N_DEV,) + (1,) * own.ndim, 0)
    return jnp.where(slot == _my_index(), own[None], landing)


def _sum_slots(parts, own=None, *, name):
    shape = parts.shape[1:]
    n, c = parts.shape[0], parts.shape[-1]
    r = parts.size // (n * c)
    tm = _pick(r, (256, 128, 64, 32, 16, 8))

    def body(p_ref, *rest):
        o_ref = rest[-1]
        me = _my_index()

        def slot(s):
            if own is None:
                return p_ref[s].astype(F32)
            return jnp.where(me == s, rest[0][...], p_ref[s]).astype(F32)

        acc = slot(0)
        for s in range(1, n):
            acc = acc + slot(s)
        o_ref[...] = acc

    tile = pl.BlockSpec((tm, c), lambda i: (i, 0))
    return pl.pallas_call(
        body,
        name=name,
        grid=(r // tm,),
        in_specs=[pl.BlockSpec((n, tm, c), lambda i: (0, i, 0))] + ([] if own is None else [tile]),
        out_specs=tile,
        out_shape=jax.ShapeDtypeStruct((r, c), F32),
        compiler_params=_params(("parallel",)),
    )(parts.reshape(n, r, c), *([] if own is None else [own.reshape(r, c)])).reshape(shape)


def _row_count(shape):
    c = shape[-1]
    rows = 1
    for s in shape[:-1]:
        rows *= s
    return rows, c, c + (-c) % LANE


PACK_ROWS = 256


def _pack_rows(arrays):
    pieces = []
    for a in arrays:
        rows, c, cp = _row_count(a.shape)
        a2 = a.reshape(rows, c)
        if cp > c:
            a2 = jnp.pad(a2, ((0, 0), (0, cp - c)))
        a2 = a2.reshape(rows * cp // LANE, LANE)
        if a2.shape[0] % 8:
            a2 = jnp.pad(a2, ((0, 8 - a2.shape[0] % 8), (0, 0)))
        pieces.append(a2)
    total = sum(p.shape[0] for p in pieces)
    if total % PACK_ROWS:
        pieces.append(jnp.zeros((PACK_ROWS - total % PACK_ROWS, LANE), F32))
    return jnp.concatenate(pieces, axis=0)


def _unpack_rows(packed, shapes, lead=()):
    out, off = [], 0
    for shp in shapes:
        rows, c, cp = _row_count(shp)
        n_rows = rows * cp // LANE
        seg = packed[..., off:off + n_rows, :].reshape(lead + (rows, cp))
        out.append(seg[..., :c].reshape(lead + tuple(shp)))
        off += n_rows + (-n_rows) % 8
    return out


def _unshard(stacked, axis):
    moved = jnp.moveaxis(stacked, 0, axis)
    shp = moved.shape
    return moved.reshape(shp[:axis] + (shp[axis] * shp[axis + 1],) + shp[axis + 2:])


def _shard_major(full, axis):
    shp = full.shape
    split = full.reshape(shp[:axis] + (N_DEV, shp[axis] // N_DEV) + shp[axis + 1:])
    return jnp.moveaxis(split, axis, 0)


def _my_shard(full, axis):
    size = full.shape[axis] // N_DEV
    return lax.dynamic_slice_in_dim(full, _my_index() * size, size, axis)


def _local_step(x, target, w, fetch, emit, n_seq):
    def with_token(vec, token):
        return vec + jnp.tile(token[0:1, :], (1, vec.shape[1] // LANE))

    depth, d_model = w["norm_mix_pre"].shape
    d_inner = w["ssd_norm_w"].shape[1]
    d_xbc = w["ssd_conv_w"].shape[2]
    saved = []
    for i in range(depth):
        j = i // 2
        m, token = fetch(i, "mix", x)
        mix_pre_w = with_token(w["norm_mix_pre"][i:i + 1], token)
        s = {"x": x, "mix_pre_w": mix_pre_w}
        if i % 2 == 0:
            u = _rms_fwd(x, mix_pre_w, out_dtype=BF16, name=f"l{i}_mix_pre")
            proj = _mm(u, m["ssd_w_in"], name=f"l{i}_ssd_in")
            xbc, xbc_pre = _ssd_conv_fwd(proj, d_inner, d_xbc, w["ssd_conv_w"][j], w["ssd_conv_b"][j:j + 1], n_seq,
                                         name=f"l{i}_ssd_conv")
            yn, y, hs = _ssd_fwd(proj, xbc, w["ssd_dt_bias"][j:j + 1], w["ssd_a_log"][j:j + 1], w["ssd_d"][j:j + 1],
                                 w["ssd_norm_w"][j:j + 1], n_seq, name=f"l{i}_ssd_scan")
            mix = _mm(yn, m["ssd_w_out"], name=f"l{i}_ssd_out")
            s.update(u=u, proj=proj, xbc=xbc, xbc_pre=xbc_pre, yn=yn, y=y, hs=hs)
        else:
            u = _rms_fwd(x, mix_pre_w, out_dtype=F32, name=f"l{i}_mix_pre")
            mix = _pool_fwd(u, m["pool_w"], w["pool_scale"][j:j + 1], n_seq, name=f"l{i}_pool")
            s.update(u=u)
        x1 = _res_rms_fwd(x, mix, w["norm_mix_post"][i:i + 1], name=f"l{i}_mix_post")
        m_ffn, token = fetch(i, "ffn", x1)
        m = {**m, **m_ffn}
        ffn_pre_w = with_token(w["norm_ffn_pre"][i:i + 1], token)
        n = _rms_fwd(x1, ffn_pre_w, out_dtype=BF16, name=f"l{i}_ffn_pre")
        h, hc, a = _ffn_up_act(n, m["ffn_w_up"], w["ffn_conv_w"][i], w["ffn_conv_b"][i:i + 1], n_seq,
                               name=f"l{i}_ffn_up_act")
        f = _mm(a, m["ffn_w_down"], name=f"l{i}_ffn_down")
        x = _res_rms_fwd(x1, f, w["norm_ffn_post"][i:i + 1], name=f"l{i}_ffn_post")
        s.update(mix=mix, x1=x1, n=n, h=h, hc=hc, a=a, f=f, m=m, ffn_pre_w=ffn_pre_w)
        saved.append(s)

    loss, dx = _loss_head(x, target)
    grads = {k: [None] * len(w[k]) for k in SMALL}
    token = jnp.zeros((8, LANE), F32)
    for i in reversed(range(depth)):
        j = i // 2
        s = saved[i]
        m, gm = s["m"], {}
        df, grads["norm_ffn_post"][i] = _rms_bwd(s["f"], with_token(w["norm_ffn_post"][i:i + 1], token), dx, None,
                                                 out_dtype=BF16, name=f"l{i}_ffn_post_b")
        gm["ffn_w_down"] = _mm(s["a"], df, ta=True, name=f"l{i}_ffn_down_bw")
        dh, grads["ffn_conv_w"][i], grads["ffn_conv_b"][i] = _ffn_down_bx_act_bwd(
            df, m["ffn_w_down"], s["h"], s["hc"], w["ffn_conv_w"][i], n_seq, name=f"l{i}_ffn_act_b")
        dn = _mm(dh, m["ffn_w_up"], tb=True, name=f"l{i}_ffn_up_bx")
        gm["ffn_w_up"] = _mm(s["n"], dh, ta=True, name=f"l{i}_ffn_up_bw")
        dx1, grads["norm_ffn_pre"][i] = _rms_bwd(s["x1"], s["ffn_pre_w"], dn, dx, name=f"l{i}_ffn_pre_b")
        token = emit(i, "ffn", gm, dx1)
        gm = {}
        dmix, grads["norm_mix_post"][i] = _rms_bwd(s["mix"], with_token(w["norm_mix_post"][i:i + 1], token), dx1, None,
                                                   out_dtype=BF16 if i % 2 == 0 else F32, name=f"l{i}_mix_post_b")
        if i % 2 == 0:
            dyn = _mm(dmix, m["ssd_w_out"], tb=True, name=f"l{i}_ssd_out_bx")
            gm["ssd_w_out"] = _mm(s["yn"], dmix, ta=True, name=f"l{i}_ssd_out_bw")
            dxs, db, dc, dz, ddtr, dnw, dbias, dalog, ddsk = _ssd_bwd(
                s["proj"], s["xbc"], s["hs"], s["y"], dyn, w["ssd_dt_bias"][j:j + 1], w["ssd_a_log"][j:j + 1],
                w["ssd_d"][j:j + 1], w["ssd_norm_w"][j:j + 1], n_seq, name=f"l{i}_ssd_scan_b")
            grads["ssd_norm_w"][j], grads["ssd_dt_bias"][j], grads["ssd_a_log"][j], grads["ssd_d"][j] = (
                dnw, dbias, dalog, ddsk)
            dproj, grads["ssd_conv_w"][j], grads["ssd_conv_b"][j] = _ssd_conv_bwd(
                s["proj"], d_inner, w["ssd_conv_w"][j], s["xbc_pre"], (dxs, db, dc), dz, n_seq,
                name=f"l{i}_ssd_conv_b")
            dproj = _fill_cols(dproj, ddtr, d_inner + d_xbc, name=f"l{i}_ssd_dt_b")
            du = _mm(dproj, m["ssd_w_in"], tb=True, name=f"l{i}_ssd_in_bx")
            gm["ssd_w_in"] = _mm(s["u"], dproj, ta=True, name=f"l{i}_ssd_in_bw")
        else:
            du, gm["pool_w"], grads["pool_scale"][j] = _pool_bwd(
                s["u"], m["pool_w"], w["pool_scale"][j:j + 1], dmix, n_seq, name=f"l{i}_pool_b")
        dx, grads["norm_mix_pre"][i] = _rms_bwd(s["x"], s["mix_pre_w"], du, dx1, name=f"l{i}_mix_pre_b")
        token = emit(i, "mix", gm, dx)
    return loss, dx, grads


BIG = (("ssd_w_in", 2), ("ssd_w_out", 1), ("pool_w", 2), ("ffn_w_up", 2), ("ffn_w_down", 1))
SMALL_SHARDED = (("ssd_conv_w", 2), ("ffn_conv_w", 2), ("pool_scale", 1))
SMALL = ("ssd_conv_w", "ssd_conv_b", "ssd_dt_bias", "ssd_a_log", "ssd_d", "ssd_norm_w", "pool_scale", "ffn_conv_w",
         "ffn_conv_b", "norm_mix_pre", "norm_mix_post", "norm_ffn_pre", "norm_ffn_post")
WEIGHTS = ("ssd_w_in", "ssd_conv_w", "ssd_conv_b", "ssd_dt_bias", "ssd_a_log", "ssd_d", "ssd_norm_w", "ssd_w_out",
           "pool_w", "pool_scale", "ffn_w_up", "ffn_conv_w", "ffn_conv_b", "ffn_w_down", "norm_mix_pre",
           "norm_mix_post", "norm_ffn_pre", "norm_ffn_post")


def _ssd_sizes(d_inner):
    return d_inner + 2 * N_SSD_GROUPS * D_STATE, d_inner // HEAD_DIM // N_SSD_GROUPS


def _small_compute_layout(full, d_inner):
    _, r_heads = _ssd_sizes(d_inner)
    w = {k: full[k] for k in SMALL}
    for k in ("ssd_dt_bias", "ssd_a_log", "ssd_d"):
        w[k] = _head_pad(full[k], r_heads)
    for k in ("ffn_conv_w", "ffn_conv_b"):
        w[k] = _interleave(full[k])
    return w


def _matmul_compute_layout(k, full, d_inner):
    d_xbc, r_heads = _ssd_sizes(d_inner)
    if k == "ssd_w_in":
        return _ssd_w_in_layout(full, d_inner, d_xbc, r_heads)
    if k == "ffn_w_up":
        return _interleave(full)
    return full


def _layer_matrices(i, part):
    if part == "ffn":
        return (("ffn_w_up", 1, i), ("ffn_w_down", 0, i))
    return (("ssd_w_in", 1, i // 2), ("ssd_w_out", 0, i // 2)) if i % 2 == 0 else (("pool_w", 1, i // 2),)


def _fetch_group(i, part):
    if i % 2 == 1:
        return _layer_matrices(i, "mix") + _layer_matrices(i, "ffn") if part == "mix" else ()
    return _layer_matrices(i, part)


def _matmul_grad_reference_layout(k, g, d_inner):
    d_xbc, r_heads = _ssd_sizes(d_inner)
    if k == "ssd_w_in":
        return _ssd_w_in_unlayout(g, d_inner, d_xbc, r_heads)
    if k == "ffn_w_up":
        return _deinterleave(g)
    return g


def _small_grads_reference_layout(grads, shapes, d_inner):
    _, r_heads = _ssd_sizes(d_inner)
    g = {k: jnp.stack(grads[k]) for k in SMALL}
    for k in ("ssd_dt_bias", "ssd_a_log", "ssd_d"):
        g[k] = _head_unpad(g[k][:, 0], r_heads)
    for k in ("ffn_conv_w", "ffn_conv_b"):
        g[k] = _deinterleave(g[k])
    return {k: v.reshape(shapes[k]) for k, v in g.items()}


def kernel(x, ssd_w_in, ssd_conv_w, ssd_conv_b, ssd_dt_bias, ssd_a_log, ssd_d, ssd_norm_w, ssd_w_out, pool_w, pool_scale, ffn_w_up, ffn_conv_w, ffn_conv_b, ffn_w_down, norm_mix_pre, norm_mix_post, norm_ffn_pre, norm_ffn_post, loss_target, m_ssd_w_in, m_ssd_conv_w, m_ssd_conv_b, m_ssd_dt_bias, m_ssd_a_log, m_ssd_d, m_ssd_norm_w, m_ssd_w_out, m_pool_w, m_pool_scale, m_ffn_w_up, m_ffn_conv_w, m_ffn_conv_b, m_ffn_w_down, m_norm_mix_pre, m_norm_mix_post, m_norm_ffn_pre, m_norm_ffn_post, v_ssd_w_in, v_ssd_conv_w, v_ssd_conv_b, v_ssd_dt_bias, v_ssd_a_log, v_ssd_d, v_ssd_norm_w, v_ssd_w_out, v_pool_w, v_pool_scale, v_ffn_w_up, v_ffn_conv_w, v_ffn_conv_b, v_ffn_w_down, v_norm_mix_pre, v_norm_mix_post, v_norm_ffn_pre, v_norm_ffn_post):
    shards = dict(ssd_w_in=ssd_w_in, ssd_conv_w=ssd_conv_w, ssd_conv_b=ssd_conv_b, ssd_dt_bias=ssd_dt_bias,
                  ssd_a_log=ssd_a_log, ssd_d=ssd_d, ssd_norm_w=ssd_norm_w, ssd_w_out=ssd_w_out, pool_w=pool_w,
                  pool_scale=pool_scale, ffn_w_up=ffn_w_up, ffn_conv_w=ffn_conv_w, ffn_conv_b=ffn_conv_b,
                  ffn_w_down=ffn_w_down, norm_mix_pre=norm_mix_pre, norm_mix_post=norm_mix_post,
                  norm_ffn_pre=norm_ffn_pre, norm_ffn_post=norm_ffn_post)
    moments_m = dict(zip(WEIGHTS, (m_ssd_w_in, m_ssd_conv_w, m_ssd_conv_b, m_ssd_dt_bias, m_ssd_a_log, m_ssd_d, m_ssd_norm_w, m_ssd_w_out, m_pool_w, m_pool_scale, m_ffn_w_up, m_ffn_conv_w, m_ffn_conv_b, m_ffn_w_down, m_norm_mix_pre, m_norm_mix_post, m_norm_ffn_pre, m_norm_ffn_post)))
    moments_v = dict(zip(WEIGHTS, (v_ssd_w_in, v_ssd_conv_w, v_ssd_conv_b, v_ssd_dt_bias, v_ssd_a_log, v_ssd_d, v_ssd_norm_w, v_ssd_w_out, v_pool_w, v_pool_scale, v_ffn_w_up, v_ffn_conv_w, v_ffn_conv_b, v_ffn_w_down, v_norm_mix_pre, v_norm_mix_post, v_norm_ffn_pre, v_norm_ffn_post)))
    n_seq, seq, d_model = x.shape
    t = n_seq * seq

    d_inner = ssd_norm_w.shape[1]
    depth = norm_mix_pre.shape[0]
    x2 = x.reshape(t, d_model)

    shard16 = {k: shards[k].astype(BF16) for k, _ in BIG}
    order = [(i, part) for i in range(depth) for part in ("mix", "ffn") if _fetch_group(i, part)]
    fetches = {}

    def start_fetch(key, after):
        srcs = [shard16[k][l] for k, _, l in _fetch_group(*key)]
        fetches[key] = _push_start(srcs, False, after, name=f"fetch{key[0]}{key[1]}_start")

    start_fetch(order[0], x2)
    full = dict(shards)
    small_packed = _pack_rows([shards[k] for k, _ in SMALL_SHARDED]) + fetches[order[0]][4][0:1, :]
    small_all = _all_gather(small_packed, name="gather_small_weights")
    small_stacked = _unpack_rows(small_all, [shards[k].shape for k, _ in SMALL_SHARDED], lead=(N_DEV,))
    for (k, axis), st in zip(SMALL_SHARDED, small_stacked):
        full[k] = _unshard(st, axis)
    w = _small_compute_layout(full, d_inner)

    def fetch(i, part, x_now):
        key = (i, part)
        if key not in order:
            return {}, jnp.zeros((8, LANE), F32)
        send, recv, srcs, lands, _ = fetches[key]
        after = small_all if key == order[0] else x_now
        lands = _push_wait(send, recv, srcs, lands, False, after, name=f"fetch{i}{part}_wait")
        mats = {}
        for (k, axis, l), land in zip(_fetch_group(i, part), lands):
            whole = _unshard(_with_own_slot(land, shard16[k][l]), axis)
            mats[k] = _matmul_compute_layout(k, whole, d_inner)
        nxt = order.index(key) + 1
        if nxt < len(order):
            start_fetch(order[nxt], lands[0])
            return mats, fetches[order[nxt]][4]
        return mats, jnp.zeros((8, LANE), F32)

    g_layers = {}
    in_flight = []

    def finish_exchange(after):
        key, blocks, (send, recv, srcs, lands, _) = in_flight.pop()
        lands = _push_wait(send, recv, srcs, lands, True, after, name=f"exchange{key[0]}{key[1]}_wait")
        for (k, _, l), land, block in zip(_layer_matrices(*key), lands, blocks):
            own = lax.dynamic_index_in_dim(block, _my_index(), 0, keepdims=False)
            g_layers[k, l] = _sum_slots(land, own, name=f"sum{key[0]}_{k}")

    def emit(i, part, gm, dx_now):
        if in_flight:
            finish_exchange(dx_now)
        blocks = [_shard_major(_matmul_grad_reference_layout(k, gm[k].astype(BF16), d_inner), axis)
                  for k, axis, _ in _layer_matrices(i, part)]
        started = _push_start(blocks, True, dx_now, name=f"exchange{i}{part}_start")
        in_flight.append(((i, part), blocks, started))
        return started[4]

    loss, dx, grads = _local_step(x2, loss_target.reshape(t, d_model), w, fetch, emit, n_seq)
    loss = lax.psum(loss, ("x", "y", "c"))

    g_shard = {}
    small_shapes = {k: full[k].shape for k in SMALL}
    g_small = _small_grads_reference_layout(grads, small_shapes, d_inner)
    s_all = _all_gather(_pack_rows([g_small[k] for k in SMALL]) + in_flight[0][2][4][0:1, :], name="gather_small_grads")
    for k, g in zip(SMALL, _unpack_rows(_sum_slots(s_all, name="sum_small_grads"), [small_shapes[k] for k in SMALL])):
        g_shard[k] = g
    for k, axis in SMALL_SHARDED:
        g_shard[k] = _my_shard(g_shard[k], axis)

    last = [k for k, _, _ in _layer_matrices(*in_flight[0][0])]
    deltas, new_m, new_v = {}, {}, {}
    for k in [k for k in WEIGHTS if k not in last] + last:
        if k == last[0]:
            finish_exchange(deltas["ffn_w_up"])
        if k in dict(BIG):
            g_shard[k] = jnp.stack([g_layers[k, l] for l in range(shards[k].shape[0])])
        deltas[k], new_m[k], new_v[k] = _adamw(shards[k], g_shard[k], moments_m[k], moments_v[k], name=f"adamw_{k}")
    return (loss, dx.reshape(x.shape), *[g_shard[k] for k in WEIGHTS], *[deltas[k] for k in WEIGHTS],
            *[new_m[k] for k in WEIGHTS], *[new_v[k] for k in WEIGHTS])
```

```python
import functools

import jax
import jax.numpy as jnp
from jax import lax
from jax.experimental import pallas as pl
from jax.experimental.pallas import tpu as pltpu

F32 = jnp.float32
BF16 = jnp.bfloat16

N_DEV = 8
HEAD_DIM = 64
N_SSD_GROUPS = 4
D_STATE = 128
CHUNK = 128
POOL_WINDOWS = (2, 4, 8, 16)
EPS = 1e-6
LANE = 128
ADAM_LR = 0.001
ADAM_B1 = 0.9
ADAM_B2 = 0.999
ADAM_EPS = 1e-08
ADAM_WD = 0.01
ADAM_STEP = 10
VMEM_LIMIT = 56 * 1024 * 1024
ANY = pl.BlockSpec(memory_space=pl.ANY)


def _pick(n, cands):
    for c in cands:
        if n % c == 0:
            return c
    return n


def _params(sem):
    return pltpu.CompilerParams(dimension_semantics=sem, vmem_limit_bytes=VMEM_LIMIT)


def _sigmoid(x):
    return 0.5 * jnp.tanh(0.5 * x) + 0.5


def _silu(x):
    return x * _sigmoid(x)


def _dsilu(x):
    s = _sigmoid(x)
    return s * (1.0 + x * (1.0 - s))


def _shift_down(x, s):
    rows = lax.broadcasted_iota(jnp.int32, x.shape, 0)
    return jnp.where(rows >= s, pltpu.roll(x, s, 0), 0.0)


def _shift_up(x, s):
    n = x.shape[0]
    rows = lax.broadcasted_iota(jnp.int32, x.shape, 0)
    return jnp.where(rows < n - s, pltpu.roll(x, n - s, 0), 0.0)


MM_VMEM_BUDGET = 40 * 1024 * 1024
MM_STEP_BYTES = 1_300_000
MM_SUB = 512


def _mm_tiles(m, n, k, a_bytes, b_bytes, o_bytes):
    def cands(dim, sizes):
        out = [s for s in sizes if s <= dim and dim % s == 0]
        return out or [dim]

    best = None
    for tm in cands(m, (m, m // 2, 2048, 1024, 512, 256, 128)):
        if tm % LANE:
            continue
        for tn in cands(n, (n, n // 2, n // 4, 2048, 1024, 512, 256, 128)):
            if tn % (2 * LANE) and tn != n:
                continue
            for tk in cands(k, (k, k // 2, 2048, 1024, 512)):
                if tk % LANE:
                    continue
                nk = k // tk
                acc = tm * tn * 4 if (nk > 1 and o_bytes != 4) else 0
                temps = tm * min(tn, MM_SUB) * 4 + (tm * tk * 2 if a_bytes == 4 else 0) + (tk * tn * 2 if b_bytes == 4 else 0)
                vmem = 2 * (tm * tk * a_bytes + tk * tn * b_bytes + tm * tn * o_bytes) + acc + temps
                if vmem > MM_VMEM_BUDGET:
                    continue
                steps = (m // tm) * (n // tn) * nk
                cost = (m * k * a_bytes * (n // tn) + k * n * b_bytes * (m // tm) + m * n * o_bytes
                        + steps * MM_STEP_BYTES)
                if best is None or cost < best[0]:
                    best = (cost, tm, tn, tk)
    return best[1:]


def _mm(a, b, *, ta=False, tb=False, out_dtype=F32, name="mm"):
    m, k = (a.shape[1], a.shape[0]) if ta else a.shape
    n = b.shape[0] if tb else b.shape[1]
    o_bytes = jnp.dtype(out_dtype).itemsize
    tm, tn, tk = _mm_tiles(m, n, k, a.dtype.itemsize, b.dtype.itemsize, o_bytes)
    nk = k // tk
    sub = _pick(tn, (MM_SUB, 256))
    use_acc = nk > 1 and o_bytes != 4
    a_spec = pl.BlockSpec((tk, tm), lambda i, j, kk: (kk, i)) if ta else pl.BlockSpec((tm, tk), lambda i, j, kk: (i, kk))
    b_spec = pl.BlockSpec((tn, tk), lambda i, j, kk: (j, kk)) if tb else pl.BlockSpec((tk, tn), lambda i, j, kk: (kk, j))
    dims = (((1,), (1 if tb else 0,)), ((), ()))

    def body(a_ref, b_ref, o_ref, *scratch):
        kk = pl.program_id(2)
        acc_ref = scratch[0] if use_acc else o_ref
        if nk > 1:
            @pl.when(kk == 0)
            def _():
                acc_ref[...] = jnp.zeros_like(acc_ref)

        av = a_ref[...].astype(BF16)
        if ta:
            av = av.T
        for s in range(tn // sub):
            cols = slice(s * sub, (s + 1) * sub)
            bv = (b_ref[cols, :] if tb else b_ref[:, cols]).astype(BF16)
            part = lax.dot_general(av, bv, dims, preferred_element_type=F32)
            if nk == 1:
                o_ref[:, cols] = part.astype(out_dtype)
            else:
                acc_ref[:, cols] += part
        if use_acc:
            @pl.when(kk == nk - 1)
            def _():
                o_ref[...] = acc_ref[...].astype(out_dtype)

    return pl.pallas_call(
        body,
        name=name,
        grid=(m // tm, n // tn, nk),
        in_specs=[a_spec, b_spec],
        out_specs=pl.BlockSpec((tm, tn), lambda i, j, kk: (i, j)),
        out_shape=jax.ShapeDtypeStruct((m, n), out_dtype),
        scratch_shapes=[pltpu.VMEM((tm, tn), F32)] if use_acc else [],
        compiler_params=_params(("parallel", "parallel", "arbitrary")),
    )(a, b)


def _rms_fwd(x, w, *, out_dtype, name):
    t, d = x.shape
    tm = _pick(t, (512, 256, 128))

    def body(x_ref, w_ref, o_ref):
        xv = x_ref[...]
        rstd = lax.rsqrt(jnp.mean(xv * xv, axis=-1, keepdims=True) + EPS)
        o_ref[...] = (xv * rstd * w_ref[...]).astype(out_dtype)

    return pl.pallas_call(
        body,
        name=name,
        grid=(t // tm,),
        in_specs=[pl.BlockSpec((tm, d), lambda i: (i, 0)), pl.BlockSpec((1, d), lambda i: (0, 0))],
        out_specs=pl.BlockSpec((tm, d), lambda i: (i, 0)),
        out_shape=jax.ShapeDtypeStruct((t, d), out_dtype),
        compiler_params=_params(("parallel",)),
    )(x, w)


def _res_rms_fwd(x, f, w, *, name):
    t, d = x.shape
    tm = _pick(t, (512, 256, 128))

    def body(x_ref, f_ref, w_ref, o_ref):
        fv = f_ref[...]
        rstd = lax.rsqrt(jnp.mean(fv * fv, axis=-1, keepdims=True) + EPS)
        o_ref[...] = x_ref[...] + fv * rstd * w_ref[...]

    row = pl.BlockSpec((tm, d), lambda i: (i, 0))
    return pl.pallas_call(
        body,
        name=name,
        grid=(t // tm,),
        in_specs=[row, row, pl.BlockSpec((1, d), lambda i: (0, 0))],
        out_specs=row,
        out_shape=jax.ShapeDtypeStruct((t, d), F32),
        compiler_params=_params(("parallel",)),
    )(x, f, w)


def _rms_bwd(x, w, dy, resid, *, out_dtype=F32, name):
    t, d = x.shape
    tm = _pick(t, (512, 256, 128))
    has_res = resid is not None

    def body(*refs):
        if has_res:
            x_ref, w_ref, dy_ref, r_ref, dx_ref, dw_ref = refs
        else:
            x_ref, w_ref, dy_ref, dx_ref, dw_ref = refs
        xv = x_ref[...]
        dyv = dy_ref[...].astype(F32)
        rstd = lax.rsqrt(jnp.mean(xv * xv, axis=-1, keepdims=True) + EPS)
        xn = xv * rstd
        g = dyv * w_ref[...]
        dx = rstd * (g - xn * jnp.mean(g * xn, axis=-1, keepdims=True))
        if has_res:
            dx = dx + r_ref[...]
        dx_ref[...] = dx.astype(out_dtype)
        part = jnp.sum(dyv * xn, axis=0, keepdims=True)

        @pl.when(pl.program_id(0) == 0)
        def _():
            dw_ref[...] = part

        @pl.when(pl.program_id(0) > 0)
        def _():
            dw_ref[...] += part

    row = pl.BlockSpec((tm, d), lambda i: (i, 0))
    vec = pl.BlockSpec((1, d), lambda i: (0, 0))
    ins = [x, w, dy] + ([resid] if has_res else [])
    return pl.pallas_call(
        body,
        name=name,
        grid=(t // tm,),
        in_specs=[row, vec, row] + ([row] if has_res else []),
        out_specs=[row, vec],
        out_shape=[jax.ShapeDtypeStruct((t, d), out_dtype), jax.ShapeDtypeStruct((1, d), F32)],
        compiler_params=_params(("arbitrary",)),
    )(*ins)


def _loss_head(y, target, *, name="loss_head"):
    t, d = y.shape
    tm = _pick(t, (512, 256, 128))

    def body(y_ref, t_ref, dy_ref, l_ref):
        err = y_ref[...] - t_ref[...]
        dy_ref[...] = err * (1.0 / d)
        part = jnp.sum(jnp.sum(err * err, axis=-1, keepdims=True), axis=0, keepdims=True) * (0.5 / d)
        part = jnp.broadcast_to(part, (1, LANE))

        @pl.when(pl.program_id(0) == 0)
        def _():
            l_ref[...] = part

        @pl.when(pl.program_id(0) > 0)
        def _():
            l_ref[...] += part

    row = pl.BlockSpec((tm, d), lambda i: (i, 0))
    dy, l = pl.pallas_call(
        body,
        name=name,
        grid=(t // tm,),
        in_specs=[row, row],
        out_specs=[row, pl.BlockSpec((1, LANE), lambda i: (0, 0))],
        out_shape=[jax.ShapeDtypeStruct((t, d), F32), jax.ShapeDtypeStruct((1, LANE), F32)],
        compiler_params=_params(("arbitrary",)),
    )(y, target)
    return l[0, 0], dy


def _conv_taps(h, w_ref, k_taps):
    out = h * w_ref[k_taps - 1:k_taps, :]
    for k in range(k_taps - 1):
        out = out + _shift_down(h, k_taps - 1 - k) * w_ref[k:k + 1, :]
    return out


def _conv_taps_bwd(h, dhc, w_ref, k_taps):
    dh = dhc * w_ref[k_taps - 1:k_taps, :]
    dws = []
    for k in range(k_taps - 1):
        up = _shift_up(dhc, k_taps - 1 - k)
        dh = dh + up * w_ref[k:k + 1, :]
        dws.append(jnp.sum(up * h, axis=0, keepdims=True))
    dws.append(jnp.sum(dhc * h, axis=0, keepdims=True))
    return dh, jnp.concatenate(dws, axis=0)


FFN_TC = 256


def _interleave(w, tc=FFN_TC):
    f = w.shape[-1] // 2
    lead = w.shape[:-1]
    return jnp.swapaxes(w.reshape(lead + (2, f // tc, tc)), -3, -2).reshape(lead + (2 * f,))


def _deinterleave(w, tc=FFN_TC):
    f = w.shape[-1] // 2
    lead = w.shape[:-1]
    return jnp.swapaxes(w.reshape(lead + (f // tc, 2, tc)), -3, -2).reshape(lead + (2 * f,))


FFN_ROWS = 512
HALO = 8


def _ffn_up_act(n, w_up, conv_w, conv_b, n_seq, *, name):
    t, d = n.shape
    f2 = w_up.shape[1]
    seq = t // n_seq
    tc = FFN_TC
    nj = f2 // (2 * tc)
    k_taps = conv_w.shape[0]
    rows = min(FFN_ROWS, seq)

    def body(n_ref, wu_ref, w_ref, b_ref, h_ref, hc_ref, o_ref, h_scr):
        h_scr[0:HALO, :] = jnp.zeros((HALO, 2 * tc), F32)
        wu = wu_ref[...]
        for r in range(seq // rows):
            chunk = slice(r * rows, (r + 1) * rows)
            h = jnp.dot(n_ref[chunk, :], wu, preferred_element_type=F32)
            h_scr[HALO + r * rows:HALO + (r + 1) * rows, :] = h
            h_ref[chunk, :] = h.astype(BF16)
            ext = h_scr[r * rows:HALO + (r + 1) * rows, :]
            hc = ext * w_ref[k_taps - 1:k_taps, :]
            for k in range(k_taps - 1):
                hc = hc + pltpu.roll(ext, k_taps - 1 - k, 0) * w_ref[k:k + 1, :]
            hc = hc[HALO:, :] + b_ref[...]
            hc_ref[chunk, :] = hc.astype(BF16)
            o_ref[chunk, :] = (_silu(hc[:, :tc]) * hc[:, tc:]).astype(BF16)

    wide = pl.BlockSpec((seq, 2 * tc), lambda b, j: (b, j))
    return pl.pallas_call(
        body,
        name=name,
        grid=(n_seq, nj),
        in_specs=[
            pl.BlockSpec((seq, d), lambda b, j: (b, 0)),
            pl.BlockSpec((d, 2 * tc), lambda b, j: (0, j)),
            pl.BlockSpec((k_taps, 2 * tc), lambda b, j: (0, j)),
            pl.BlockSpec((1, 2 * tc), lambda b, j: (0, j)),
        ],
        out_specs=[wide, wide, pl.BlockSpec((seq, tc), lambda b, j: (b, j))],
        out_shape=[jax.ShapeDtypeStruct((t, f2), BF16), jax.ShapeDtypeStruct((t, f2), BF16),
                   jax.ShapeDtypeStruct((t, f2 // 2), BF16)],
        scratch_shapes=[pltpu.VMEM((HALO + seq, 2 * tc), F32)],
        compiler_params=_params(("parallel", "arbitrary")),
    )(n, w_up, conv_w, conv_b)


def _ffn_down_bx_act_bwd(df, w_down, h, hc, conv_w, n_seq, *, name):
    t, d = df.shape
    f2 = h.shape[1]
    seq = t // n_seq
    tc = FFN_TC
    nj = f2 // (2 * tc)
    k_taps = conv_w.shape[0]

    def body(df_ref, wd_ref, h_ref, hc_ref, w_ref, dh_ref, dw_ref, db_ref):
        dav = lax.dot_general(df_ref[...], wd_ref[...], (((1,), (1,)), ((), ())), preferred_element_type=F32)
        hcv = hc_ref[...].astype(F32)
        gate, val = hcv[:, :tc], hcv[:, tc:]
        dhc = jnp.concatenate([dav * val * _dsilu(gate), dav * _silu(gate)], axis=1)
        dh, dw = _conv_taps_bwd(h_ref[...].astype(F32), dhc, w_ref, k_taps)
        dh_ref[...] = dh.astype(BF16)
        dw_ref[0] = dw
        db_ref[0] = jnp.sum(dhc, axis=0, keepdims=True)

    wide = pl.BlockSpec((seq, 2 * tc), lambda b, j: (b, j))
    dh, dw, db = pl.pallas_call(
        body,
        name=name,
        grid=(n_seq, nj),
        in_specs=[
            pl.BlockSpec((seq, d), lambda b, j: (b, 0)),
            pl.BlockSpec((tc, d), lambda b, j: (j, 0)),
            wide, wide,
            pl.BlockSpec((k_taps, 2 * tc), lambda b, j: (0, j)),
        ],
        out_specs=[
            wide,
            pl.BlockSpec((1, k_taps, 2 * tc), lambda b, j: (b, 0, j)),
            pl.BlockSpec((1, 1, 2 * tc), lambda b, j: (b, 0, j)),
        ],
        out_shape=[
            jax.ShapeDtypeStruct((t, f2), BF16),
            jax.ShapeDtypeStruct((n_seq, k_taps, f2), F32),
            jax.ShapeDtypeStruct((n_seq, 1, f2), F32),
        ],
        compiler_params=_params(("parallel", "arbitrary")),
    )(df, w_down, h, hc, conv_w)
    return dh, jnp.sum(dw, axis=0), jnp.sum(db, axis=0)


def _window_mixed(u, window):
    s = u
    step = 1
    while step < window:
        s = s + _shift_down(s, step)
        step *= 2
    rows = lax.broadcasted_iota(jnp.int32, u.shape, 0)
    inv_cnt = 1.0 / jnp.minimum(rows + 1, window).astype(F32)
    return s * inv_cnt - u, inv_cnt


def _window_mixed_bwd(dmixed, inv_cnt, window):
    r = dmixed * inv_cnt
    s = r
    step = 1
    while step < window:
        s = s + _shift_up(s, step)
        step *= 2
    return s - dmixed


def _pool_fwd(u, w, scale, n_seq, *, name):
    t, d = u.shape
    seq = t // n_seq
    n_g, dg, _ = w.shape

    def body(u_ref, w_ref, s_ref, o_ref):
        for k, window in enumerate(POOL_WINDOWS):
            @pl.when(pl.program_id(1) == k)
            def _(window=window):
                mixed, _ = _window_mixed(u_ref[...], window)
                pre = jnp.dot(mixed.astype(BF16), w_ref[0].astype(BF16), preferred_element_type=F32)
                o_ref[...] = pre * s_ref[...]

    return pl.pallas_call(
        body,
        name=name,
        grid=(n_seq, n_g),
        in_specs=[
            pl.BlockSpec((seq, dg), lambda b, g: (b, g)),
            pl.BlockSpec((1, dg, dg), lambda b, g: (g, 0, 0)),
            pl.BlockSpec((1, dg), lambda b, g: (0, g)),
        ],
        out_specs=pl.BlockSpec((seq, dg), lambda b, g: (b, g)),
        out_shape=jax.ShapeDtypeStruct((t, d), F32),
        compiler_params=_params(("parallel", "parallel")),
    )(u, w, scale)


def _pool_bwd(u, w, scale, dout, n_seq, *, name):
    t, d = u.shape
    seq = t // n_seq
    n_g, dg, _ = w.shape

    def body(u_ref, w_ref, s_ref, do_ref, du_ref, dw_ref, ds_ref):
        group = pl.program_id(0)
        first = pl.program_id(1) == 0
        for k, window in enumerate(POOL_WINDOWS):
            @pl.when(group == k)
            def _(window=window):
                mixed, inv_cnt = _window_mixed(u_ref[...], window)
                mixed_b = mixed.astype(BF16)
                w_b = w_ref[0].astype(BF16)
                dov = do_ref[...]
                pre = jnp.dot(mixed_b, w_b, preferred_element_type=F32)
                dsc = jnp.sum(dov * pre, axis=0, keepdims=True)
                dpre = (dov * s_ref[...]).astype(BF16)
                dw = lax.dot_general(mixed_b, dpre, (((0,), (0,)), ((), ())), preferred_element_type=F32)
                dmixed = lax.dot_general(dpre, w_b, (((1,), (1,)), ((), ())), preferred_element_type=F32)
                du_ref[...] = _window_mixed_bwd(dmixed, inv_cnt, window)

                @pl.when(first)
                def _():
                    dw_ref[0] = dw
                    ds_ref[...] = dsc

                @pl.when(jnp.logical_not(first))
                def _():
                    dw_ref[0] += dw
                    ds_ref[...] += dsc

    return pl.pallas_call(
        body,
        name=name,
        grid=(n_g, n_seq),
        in_specs=[
            pl.BlockSpec((seq, dg), lambda g, b: (b, g)),
            pl.BlockSpec((1, dg, dg), lambda g, b: (g, 0, 0)),
            pl.BlockSpec((1, dg), lambda g, b: (0, g)),
            pl.BlockSpec((seq, dg), lambda g, b: (b, g)),
        ],
        out_specs=[
            pl.BlockSpec((seq, dg), lambda g, b: (b, g)),
            pl.BlockSpec((1, dg, dg), lambda g, b: (g, 0, 0)),
            pl.BlockSpec((1, dg), lambda g, b: (0, g)),
        ],
        out_shape=[
            jax.ShapeDtypeStruct((t, d), F32),
            jax.ShapeDtypeStruct((n_g, dg, dg), F32),
            jax.ShapeDtypeStruct((1, d), F32),
        ],
        compiler_params=_params(("parallel", "arbitrary")),
    )(u, w, scale, dout)


def _adamw(w, g, m, v, *, name):
    shape = w.shape
    c = shape[-1]
    r = w.size // c
    tm = _pick(r, (512, 256, 128, 64, 32, 16, 8))

    def body(w_ref, g_ref, m_ref, v_ref, d_ref, nm_ref, nv_ref):
        gv = g_ref[...]
        nm = ADAM_B1 * m_ref[...] + (1.0 - ADAM_B1) * gv
        nv = ADAM_B2 * v_ref[...] + (1.0 - ADAM_B2) * (gv * gv)
        m_hat = nm / (1.0 - ADAM_B1 ** ADAM_STEP)
        v_hat = nv / (1.0 - ADAM_B2 ** ADAM_STEP)
        d_ref[...] = -ADAM_LR * (m_hat / (jnp.sqrt(v_hat) + ADAM_EPS) + ADAM_WD * w_ref[...])
        nm_ref[...] = nm
        nv_ref[...] = nv

    blk = pl.BlockSpec((tm, c), lambda i: (i, 0))
    out = jax.ShapeDtypeStruct((r, c), F32)
    res = pl.pallas_call(
        body,
        name=name,
        grid=(r // tm,),
        in_specs=[blk] * 4,
        out_specs=[blk] * 3,
        out_shape=[out] * 3,
        compiler_params=_params(("parallel",)),
    )(w.reshape(r, c), g.reshape(r, c), m.reshape(r, c), v.reshape(r, c))
    return tuple(a.reshape(shape) for a in res)


CONV_TC = 256


def _ssd_conv_fwd(proj, col0, n_cols, conv_w, conv_b, n_seq, *, name):
    t = proj.shape[0]
    seq = t // n_seq
    tc = CONV_TC
    off = col0 // tc
    k_taps = conv_w.shape[0]

    def body(h_ref, w_ref, b_ref, o_ref, pre_ref):
        pre = _conv_taps(h_ref[...], w_ref, k_taps) + b_ref[...]
        pre_ref[...] = pre.astype(BF16)
        o_ref[...] = _silu(pre)

    return pl.pallas_call(
        body,
        name=name,
        grid=(n_seq, n_cols // tc),
        in_specs=[
            pl.BlockSpec((seq, tc), lambda b, j: (b, j + off)),
            pl.BlockSpec((k_taps, tc), lambda b, j: (0, j)),
            pl.BlockSpec((1, tc), lambda b, j: (0, j)),
        ],
        out_specs=[pl.BlockSpec((seq, tc), lambda b, j: (b, j))] * 2,
        out_shape=[jax.ShapeDtypeStruct((t, n_cols), F32), jax.ShapeDtypeStruct((t, n_cols), BF16)],
        compiler_params=_params(("parallel", "parallel")),
    )(proj, conv_w, conv_b)


def _ssd_conv_bwd(proj, col0, conv_w, pre, dparts, dproj, n_seq, *, name):
    t = proj.shape[0]
    seq = t // n_seq
    tc = CONV_TC
    off = col0 // tc
    k_taps = conv_w.shape[0]
    widths = [d.shape[1] // tc for d in dparts]
    starts = [sum(widths[:i]) for i in range(len(widths))]
    n_blocks = sum(widths)
    n_parts = len(dparts)

    def body(h_ref, w_ref, pre_ref, *rest):
        part_refs = rest[:n_parts]
        dh_ref, dw_ref, db_ref = rest[n_parts + 1:]
        j = pl.program_id(0)
        da = part_refs[-1][...]
        for i in reversed(range(n_parts - 1)):
            da = jnp.where(j < starts[i + 1], part_refs[i][...], da)
        dhc = da * _dsilu(pre_ref[...].astype(F32))
        dh, dw = _conv_taps_bwd(h_ref[...], dhc, w_ref, k_taps)
        dh_ref[...] = dh.astype(BF16)
        db = jnp.sum(dhc, axis=0, keepdims=True)

        @pl.when(pl.program_id(1) == 0)
        def _():
            dw_ref[...] = dw
            db_ref[...] = db

        @pl.when(pl.program_id(1) > 0)
        def _():
            dw_ref[...] += dw
            db_ref[...] += db

    def part_spec(start, width):
        return pl.BlockSpec((seq, tc), lambda j, b: (b, jnp.clip(j - start, 0, width - 1)))

    n_cols = n_blocks * tc
    return pl.pallas_call(
        body,
        name=name,
        grid=(n_blocks, n_seq),
        in_specs=[
            pl.BlockSpec((seq, tc), lambda j, b: (b, j + off)),
            pl.BlockSpec((k_taps, tc), lambda j, b: (0, j)),
            pl.BlockSpec((seq, tc), lambda j, b: (b, j)),
        ] + [part_spec(st, wd) for st, wd in zip(starts, widths)] + [ANY],
        out_specs=[
            pl.BlockSpec((seq, tc), lambda j, b: (b, j + off)),
            pl.BlockSpec((k_taps, tc), lambda j, b: (0, j)),
            pl.BlockSpec((1, tc), lambda j, b: (0, j)),
        ],
        out_shape=[
            jax.ShapeDtypeStruct(dproj.shape, BF16),
            jax.ShapeDtypeStruct((k_taps, n_cols), F32),
            jax.ShapeDtypeStruct((1, n_cols), F32),
        ],
        input_output_aliases={3 + n_parts: 0},
        compiler_params=_params(("parallel", "arbitrary")),
    )(proj, conv_w, pre, *dparts, dproj)


def _fill_cols(buf, src, col0, *, name):
    t, c = src.shape
    tm = _pick(t, (1024, 512, 256, 128))

    def body(s_ref, b_ref, o_ref):
        o_ref[...] = s_ref[...].astype(o_ref.dtype)

    return pl.pallas_call(
        body,
        name=name,
        grid=(t // tm,),
        in_specs=[pl.BlockSpec((tm, c), lambda i: (i, 0)), ANY],
        out_specs=pl.BlockSpec((tm, c), lambda i: (i, col0 // c)),
        out_shape=jax.ShapeDtypeStruct(buf.shape, buf.dtype),
        input_output_aliases={1: 0},
        compiler_params=_params(("parallel",)),
    )(src, buf)


def _softplus(x):
    return jnp.maximum(x, 0.0) + jnp.log(1.0 + jnp.exp(-jnp.abs(x)))


def _chunk_decay(dtraw, bias, alog):
    q = dtraw.shape[0]
    dt = _softplus(dtraw + bias)
    a = -jnp.exp(alog)
    rows = lax.broadcasted_iota(jnp.int32, (q, q), 0)
    cols = lax.broadcasted_iota(jnp.int32, (q, q), 1)
    lower = rows >= cols
    acum = jnp.dot(lower.astype(F32), dt * a, precision=lax.Precision.HIGHEST, preferred_element_type=F32)
    return dt, a, acum, acum.T, lower


def _dot_exact(v, sel):
    hi = v.astype(BF16)
    r1 = v - hi.astype(F32)
    mid = r1.astype(BF16)
    lo = (r1 - mid.astype(F32)).astype(BF16)
    return (jnp.dot(hi, sel, preferred_element_type=F32) + jnp.dot(mid, sel, preferred_element_type=F32)
            + jnp.dot(lo, sel, preferred_element_type=F32))


def _head_selectors(gw, p):
    sum_heads = (lax.broadcasted_iota(jnp.int32, (gw, LANE), 0) // p == lax.broadcasted_iota(jnp.int32, (gw, LANE), 1))
    spread = (lax.broadcasted_iota(jnp.int32, (LANE, gw), 0) == lax.broadcasted_iota(jnp.int32, (LANE, gw), 1) // p)
    return sum_heads.astype(BF16), spread.astype(BF16)


def _row_spread(v, spread):
    return _dot_exact(jnp.broadcast_to(v, (8, v.shape[1])), spread)[0:1, :]


def _head_pad(v, r_heads):
    lead = v.shape[:-1]
    vg = v.reshape(lead + (N_SSD_GROUPS, r_heads))
    vg = jnp.pad(vg, [(0, 0)] * len(lead) + [(0, 0), (0, LANE - r_heads)])
    out = vg.reshape(lead + (N_SSD_GROUPS * LANE,))
    return out[None] if out.ndim == 1 else out


def _head_unpad(v, r_heads):
    lead = v.shape[:-1]
    out = v.reshape(lead + (N_SSD_GROUPS, LANE))[..., :r_heads].reshape(lead + (N_SSD_GROUPS * r_heads,))
    return out[0] if (len(lead) == 1 and lead[0] == 1) else out


def _ssd_w_in_layout(w_in, d_inner, d_xbc, r_heads):
    main = w_in[:, :d_inner + d_xbc]
    return jnp.concatenate([main, _head_pad(w_in[:, d_inner + d_xbc:], r_heads)], axis=1)


def _ssd_w_in_unlayout(w, d_inner, d_xbc, r_heads):
    main = w[:, :d_inner + d_xbc]
    return jnp.concatenate([main, _head_unpad(w[:, d_inner + d_xbc:], r_heads)], axis=1)


def _ssd_dims(proj, xbc):
    d_xbc = xbc.shape[1]
    d_inner = d_xbc - 2 * N_SSD_GROUPS * D_STATE
    gw = d_inner // N_SSD_GROUPS
    return d_inner, d_xbc, gw, gw // HEAD_DIM


def _ssd_fwd(proj, xbc, bias_p, alog_p, dskip_p, norm_w, n_seq, *, name):
    t = proj.shape[0]
    d_inner, d_xbc, gw, r_heads = _ssd_dims(proj, xbc)
    q, n, n_g, p = CHUNK, D_STATE, N_SSD_GROUPS, HEAD_DIM
    seq = t // n_seq
    nc = seq // q
    dt_blk0 = (d_inner + d_xbc) // LANE

    def body(x_ref, b_ref, c_ref, z_ref, dtr_ref, bias_ref, alog_ref, dsk_ref, nw_ref, yn_ref, y_ref, hs_ref, h_scr):
        @pl.when(pl.program_id(2) == 0)
        def _():
            h_scr[...] = jnp.zeros_like(h_scr)

        dt, a, acum, acum_t, lower = _chunk_decay(dtr_ref[...], bias_ref[...], alog_ref[...])
        x = x_ref[...]
        bb = b_ref[...].astype(BF16)
        cb = c_ref[...].astype(BF16)
        g_mat = lax.dot_general(cb, bb, (((1,), (1,)), ((), ())), preferred_element_type=F32)
        h_prev = h_scr[...]
        hs_ref[...] = h_prev
        c_h = jnp.dot(cb, h_prev.astype(BF16), preferred_element_type=F32)
        _, spread = _head_selectors(gw, p)
        acum_s = _dot_exact(acum, spread)
        a_last_s = acum_s[q - 1:q, :]
        xdt = x * _dot_exact(dt, spread)
        xdt_b = xdt.astype(BF16)
        ys = []
        for h in range(r_heads):
            decay = jnp.exp(jnp.where(lower, acum[:, h:h + 1] - acum_t[h:h + 1, :], -jnp.inf))
            ys.append(jnp.dot((g_mat * decay).astype(BF16), xdt_b[:, h * p:(h + 1) * p], preferred_element_type=F32))
        y = jnp.concatenate(ys, axis=1) + jnp.exp(acum_s) * c_h + _row_spread(dsk_ref[...], spread) * x
        xd = xdt * jnp.exp(a_last_s - acum_s)
        states = lax.dot_general(bb, xd.astype(BF16), (((0,), (0,)), ((), ())), preferred_element_type=F32)
        h_scr[...] = h_prev * jnp.exp(a_last_s) + states
        y_ref[...] = y
        gated = y * _silu(z_ref[...])
        rstd = lax.rsqrt(jnp.mean(gated * gated, axis=-1, keepdims=True) + EPS)
        yn_ref[...] = (gated * rstd * nw_ref[...]).astype(BF16)

    row = lambda b, g, c: b * nc + c
    vec = pl.BlockSpec((1, LANE), lambda b, g, c: (0, g))
    return pl.pallas_call(
        body,
        name=name,
        grid=(n_seq, n_g, nc),
        in_specs=[
            pl.BlockSpec((q, gw), lambda b, g, c: (row(b, g, c), g)),
            pl.BlockSpec((q, n), lambda b, g, c: (row(b, g, c), d_inner // n + g)),
            pl.BlockSpec((q, n), lambda b, g, c: (row(b, g, c), d_inner // n + n_g + g)),
            pl.BlockSpec((q, gw), lambda b, g, c: (row(b, g, c), g)),
            pl.BlockSpec((q, LANE), lambda b, g, c: (row(b, g, c), dt_blk0 + g)),
            vec, vec, vec,
            pl.BlockSpec((1, gw), lambda b, g, c: (0, g)),
        ],
        out_specs=[
            pl.BlockSpec((q, gw), lambda b, g, c: (row(b, g, c), g)),
            pl.BlockSpec((q, gw), lambda b, g, c: (row(b, g, c), g)),
            pl.BlockSpec((n, gw), lambda b, g, c: (row(b, g, c), g)),
        ],
        out_shape=[
            jax.ShapeDtypeStruct((t, d_inner), BF16),
            jax.ShapeDtypeStruct((t, d_inner), F32),
            jax.ShapeDtypeStruct((n_seq * nc * n, d_inner), F32),
        ],
        scratch_shapes=[pltpu.VMEM((n, gw), F32)],
        compiler_params=_params(("parallel", "parallel", "arbitrary")),
    )(xbc, xbc, xbc, proj, proj, bias_p, alog_p, dskip_p, norm_w)


def _ssd_bwd(proj, xbc, hs, y, dyn, bias_p, alog_p, dskip_p, norm_w, n_seq, *, name):
    t = proj.shape[0]
    d_inner, d_xbc, gw, r_heads = _ssd_dims(proj, xbc)
    q, n, n_g, p = CHUNK, D_STATE, N_SSD_GROUPS, HEAD_DIM
    seq = t // n_seq
    nc = seq // q
    dt_blk0 = (d_inner + d_xbc) // LANE

    def body(x_ref, b_ref, c_ref, z_ref, dtr_ref, bias_ref, alog_ref, dsk_ref, nw_ref, hs_ref, y_ref, dyn_ref,
             dx_ref, db_ref, dc_ref, dz_ref, ddtr_ref, dnw_ref, dbias_ref, dalog_ref, ddsk_ref, dh_scr):
        first = jnp.logical_and(pl.program_id(1) == 0, pl.program_id(2) == 0)

        @pl.when(pl.program_id(2) == 0)
        def _():
            dh_scr[...] = jnp.zeros_like(dh_scr)

        dtraw = dtr_ref[...]
        dt, a, acum, acum_t, lower = _chunk_decay(dtraw, bias_ref[...], alog_ref[...])
        x = x_ref[...]
        bb = b_ref[...].astype(BF16)
        cb = c_ref[...].astype(BF16)
        g_mat = lax.dot_general(cb, bb, (((1,), (1,)), ((), ())), preferred_element_type=F32)

        yv = y_ref[...]
        z = z_ref[...]
        sz = _silu(z)
        gated = yv * sz
        rstd = lax.rsqrt(jnp.mean(gated * gated, axis=-1, keepdims=True) + EPS)
        gn = gated * rstd
        dynv = dyn_ref[...]
        gwt = dynv * nw_ref[...]
        dgated = rstd * (gwt - gn * jnp.mean(gwt * gn, axis=-1, keepdims=True))
        dnw = jnp.sum(dynv * gn, axis=0, keepdims=True)
        dy = dgated * sz
        dz_ref[...] = (dgated * yv * _dsilu(z)).astype(BF16)

        h_prev = hs_ref[...]
        h_prev_b = h_prev.astype(BF16)
        ds = dh_scr[...]
        ds_b = ds.astype(BF16)
        sum_heads, spread = _head_selectors(gw, p)
        acum_s = _dot_exact(acum, spread)
        a_last_s = acum_s[q - 1:q, :]
        dt_s = _dot_exact(dt, spread)
        dsk_s = _row_spread(dsk_ref[...], spread)
        dte_s = jnp.exp(a_last_s - acum_s)
        cd_s = jnp.exp(a_last_s)
        xdt = x * dt_s
        xdt_b = xdt.astype(BF16)
        dy_b = dy.astype(BF16)
        gt_mat = lax.dot_general(bb, cb, (((1,), (1,)), ((), ())), preferred_element_type=F32)
        upper = lax.broadcasted_iota(jnp.int32, (q, q), 0) <= lax.broadcasted_iota(jnp.int32, (q, q), 1)
        dg = jnp.zeros((q, q), F32)
        dxdts, w_diffs = [], []
        for h in range(r_heads):
            hsl = slice(h * p, (h + 1) * p)
            diff = acum[:, h:h + 1] - acum_t[h:h + 1, :]
            decay = jnp.exp(jnp.where(lower, diff, -jnp.inf))
            decay_t = jnp.exp(jnp.where(upper, -diff, -jnp.inf))
            mt_mat = gt_mat * decay_t
            dm = lax.dot_general(dy_b[:, hsl], xdt_b[:, hsl], (((1,), (1,)), ((), ())), preferred_element_type=F32)
            dm_t = lax.dot_general(xdt_b[:, hsl], dy_b[:, hsl], (((1,), (1,)), ((), ())), preferred_element_type=F32)
            dg = dg + dm * decay
            dxdts.append(jnp.dot(mt_mat.astype(BF16), dy_b[:, hsl], preferred_element_type=F32))
            w_diffs.append(dm * (g_mat * decay) - dm_t * mt_mat)
        sel_q = (lax.broadcasted_iota(jnp.int32, (r_heads * q, LANE), 0) // q
                 == lax.broadcasted_iota(jnp.int32, (r_heads * q, LANE), 1)).astype(BF16)
        dacum_diag = _dot_exact(jnp.concatenate(w_diffs, axis=1), sel_q)
        c_h = jnp.dot(cb, h_prev_b, preferred_element_type=F32)
        dxd = jnp.dot(bb, ds_b, preferred_element_type=F32)
        dxdt = jnp.concatenate(dxdts, axis=1) + dxd * dte_s
        dye = dy * jnp.exp(acum_s)
        dye_b = dye.astype(BF16)
        xd = xdt * dte_s
        xd_b = xd.astype(BF16)
        dg_b = dg.astype(BF16)
        dx_ref[...] = dxdt * dt_s + dsk_s * dy
        dc_ref[...] = (jnp.dot(dg_b, bb, preferred_element_type=F32)
                       + lax.dot_general(dye_b, h_prev_b, (((1,), (1,)), ((), ())), preferred_element_type=F32))
        db_ref[...] = (lax.dot_general(dg_b, cb, (((0,), (0,)), ((), ())), preferred_element_type=F32)
                       + lax.dot_general(xd_b, ds_b, (((1,), (1,)), ((), ())), preferred_element_type=F32))
        dh_scr[...] = ds * cd_s + lax.dot_general(cb, dye_b, (((0,), (0,)), ((), ())), preferred_element_type=F32)
        ddt_cols = _dot_exact(x * dxdt, sum_heads)
        dacum_y = _dot_exact(dye * c_h - dxd * xd, sum_heads)
        col_sums = jnp.concatenate([
            jnp.sum(dxd * xd, axis=0, keepdims=True) + jnp.sum(ds * h_prev, axis=0, keepdims=True) * cd_s,
            jnp.sum(dy * x, axis=0, keepdims=True),
            jnp.zeros((6, gw), F32)], axis=0)
        col_sums = _dot_exact(col_sums, sum_heads)
        ddsk = col_sums[1:2, :]
        rows_q = lax.broadcasted_iota(jnp.int32, (q, LANE), 0)
        dacum = dacum_diag + dacum_y + jnp.where(rows_q == q - 1, col_sums[0:1, :], 0.0)
        dadt = jnp.dot(upper.astype(F32), dacum, precision=lax.Precision.HIGHEST, preferred_element_type=F32)
        ddt = dadt * a + ddt_cols
        ddtr = ddt * _sigmoid(dtraw + bias_ref[...])
        ddtr_ref[...] = ddtr
        dbias = jnp.sum(ddtr, axis=0, keepdims=True)
        dalog = jnp.sum(dadt * dt, axis=0, keepdims=True) * a

        @pl.when(first)
        def _():
            dnw_ref[...] = dnw
            dbias_ref[...] = dbias
            dalog_ref[...] = dalog
            ddsk_ref[...] = ddsk

        @pl.when(jnp.logical_not(first))
        def _():
            dnw_ref[...] += dnw
            dbias_ref[...] += dbias
            dalog_ref[...] += dalog
            ddsk_ref[...] += ddsk

    row = lambda g, b, c: b * nc + (nc - 1 - c)
    vec = pl.BlockSpec((1, LANE), lambda g, b, c: (0, g))
    wide = pl.BlockSpec((q, gw), lambda g, b, c: (row(g, b, c), g))
    narrow = pl.BlockSpec((q, n), lambda g, b, c: (row(g, b, c), g))
    return pl.pallas_call(
        body,
        name=name,
        grid=(n_g, n_seq, nc),
        in_specs=[
            wide,
            pl.BlockSpec((q, n), lambda g, b, c: (row(g, b, c), d_inner // n + g)),
            pl.BlockSpec((q, n), lambda g, b, c: (row(g, b, c), d_inner // n + n_g + g)),
            wide,
            pl.BlockSpec((q, LANE), lambda g, b, c: (row(g, b, c), dt_blk0 + g)),
            vec, vec, vec,
            pl.BlockSpec((1, gw), lambda g, b, c: (0, g)),
            pl.BlockSpec((n, gw), lambda g, b, c: (row(g, b, c), g)),
            wide, wide,
        ],
        out_specs=[
            wide, narrow, narrow, wide, narrow,
            pl.BlockSpec((1, gw), lambda g, b, c: (0, g)),
            vec, vec, vec,
        ],
        out_shape=[
            jax.ShapeDtypeStruct((t, d_inner), F32),
            jax.ShapeDtypeStruct((t, n_g * n), F32),
            jax.ShapeDtypeStruct((t, n_g * n), F32),
            jax.ShapeDtypeStruct(proj.shape, BF16),
            jax.ShapeDtypeStruct((t, n_g * LANE), F32),
            jax.ShapeDtypeStruct((1, d_inner), F32),
            jax.ShapeDtypeStruct((1, n_g * LANE), F32),
            jax.ShapeDtypeStruct((1, n_g * LANE), F32),
            jax.ShapeDtypeStruct((1, n_g * LANE), F32),
        ],
        scratch_shapes=[pltpu.VMEM((n, gw), F32)],
        compiler_params=_params(("parallel", "arbitrary", "arbitrary")),
    )(xbc, xbc, xbc, proj, proj, bias_p, alog_p, dskip_p, norm_w, hs, y, dyn)


MESH_IDS = pl.DeviceIdType.MESH


def _my_index():
    return 4 * lax.axis_index("x") + 2 * lax.axis_index("y") + lax.axis_index("c")


def _all_gather(shard, *, name):
    def body(x_ref, out_ref, send_sems, recv_sems, local_sem):
        x, y, c = lax.axis_index("x"), lax.axis_index("y"), lax.axis_index("c")
        me, sibling = (x, y, c), (x, y, 1 - c)
        chips = [(1 - x, y), (x, 1 - y), (1 - x, 1 - y)]

        def blk(px, py, pc):
            return out_ref.at[4 * px + 2 * py + pc]

        def copy(k, block, to, src=None):
            return pltpu.make_async_remote_copy(
                src_ref=blk(*block) if src is None else src, dst_ref=blk(*block),
                send_sem=send_sems.at[k], recv_sem=recv_sems.at[k], device_id=to, device_id_type=MESH_IDS)

        mine = pltpu.make_async_copy(x_ref, blk(*me), local_sem)
        mine.start()
        first = [copy(0, me, sibling, src=x_ref)]
        first += [copy(1 + j, me, (*chip, c), src=x_ref) for j, chip in enumerate(chips)]
        for cp in first:
            cp.start()
        passed = [copy(4 + j, (*chip, c), sibling) for j, chip in enumerate(chips)]
        for j, chip in enumerate(chips):
            copy(1 + j, (*chip, c), me).wait_recv()
            passed[j].start()
        copy(0, sibling, me).wait_recv()
        for j, chip in enumerate(chips):
            copy(4 + j, (*chip, 1 - c), me).wait_recv()
        for cp in first + passed:
            cp.wait_send()
        mine.wait()

    return pl.pallas_call(
        body,
        name=name,
        in_specs=[ANY],
        out_specs=ANY,
        out_shape=jax.ShapeDtypeStruct((N_DEV,) + shard.shape, shard.dtype),
        scratch_shapes=[pltpu.SemaphoreType.DMA((7,)), pltpu.SemaphoreType.DMA((7,)), pltpu.SemaphoreType.DMA],
    )(shard)


HBM_SPEC = pl.BlockSpec(memory_space=pltpu.HBM)
SEM_SPEC = pl.BlockSpec(memory_space=pltpu.SEMAPHORE)
SPLIT_COPY_PARAMS = pltpu.CompilerParams(has_side_effects=pltpu.SideEffectType.DATAFLOW_SIDE_EFFECTING)


def _peer_list():
    x, y, c = lax.axis_index("x"), lax.axis_index("y"), lax.axis_index("c")
    peers = []
    for k in range(1, N_DEV):
        px = 1 - x if k & 4 else x
        py = 1 - y if k & 2 else y
        pc = 1 - c if k & 1 else c
        peers.append(((px, py, pc), 4 * px + 2 * py + pc))
    return 4 * x + 2 * y + c, peers


def _push_copies(src_refs, land_refs, send_sems, recv_sems, blockwise):
    me, peers = _peer_list()
    copies = []
    for a, (src_ref, land_ref) in enumerate(zip(src_refs, land_refs)):
        for k, (dev, idx) in enumerate(peers):
            sem = a * (N_DEV - 1) + k
            src = src_ref.at[idx] if blockwise else src_ref
            copies.append(tuple(
                pltpu.make_async_remote_copy(src_ref=src, dst_ref=land_ref.at[slot], send_sem=send_sems.at[sem],
                                             recv_sem=recv_sems.at[sem], device_id=dev, device_id_type=MESH_IDS)
                for slot in (me, idx)))
    return copies


def _push_start(srcs, blockwise, after, *, name):
    n = len(srcs)
    blocks = [s_.shape[1:] if blockwise else s_.shape for s_ in srcs]

    def body(*refs):
        src_refs, land_refs = refs[:n], refs[n:2 * n]
        send_sems, recv_sems = refs[2 * n + 1], refs[2 * n + 2]
        token = refs[-1]
        for send, _ in _push_copies(src_refs, land_refs, send_sems, recv_sems, blockwise):
            send.start()
        token[...] = jnp.zeros_like(token)

    n_sem = n * (N_DEV - 1)
    lands = [lax.empty((N_DEV,) + b, s_.dtype) for b, s_ in zip(blocks, srcs)]
    out = pl.pallas_call(
        body,
        name=name,
        in_specs=[HBM_SPEC] * (2 * n) + [ANY],
        out_specs=(SEM_SPEC, SEM_SPEC) + (HBM_SPEC,) * (2 * n) + (pl.BlockSpec(memory_space=pltpu.VMEM),),
        out_shape=(pltpu.SemaphoreType.DMA((n_sem,)), pltpu.SemaphoreType.DMA((n_sem,)))
        + tuple(pltpu.HBM(a.shape, a.dtype) for a in list(srcs) + lands)
        + (jax.ShapeDtypeStruct((8, LANE), F32),),
        input_output_aliases={i: 2 + i for i in range(2 * n)},
        compiler_params=SPLIT_COPY_PARAMS,
    )(*[pltpu.with_memory_space_constraint(a, pltpu.HBM) for a in list(srcs) + lands], after)
    return out[0], out[1], out[2:2 + n], out[2 + n:2 + 2 * n], out[-1]


def _push_wait(send_sems, recv_sems, srcs, lands, blockwise, after, *, name):
    n = len(srcs)

    def body(*refs):
        src_refs, land_refs = refs[:n], refs[n:2 * n]
        send_sems, recv_sems = refs[2 * n], refs[2 * n + 1]
        for send, recv in _push_copies(src_refs, land_refs, send_sems, recv_sems, blockwise):
            send.wait_send()
            recv.wait_recv()

    out = pl.pallas_call(
        body,
        name=name,
        in_specs=[HBM_SPEC] * (2 * n) + [SEM_SPEC, SEM_SPEC, ANY],
        out_specs=(HBM_SPEC,) * (2 * n),
        out_shape=tuple(pltpu.HBM(a.shape, a.dtype) for a in list(srcs) + list(lands)),
        input_output_aliases={i: i for i in range(2 * n)},
        compiler_params=SPLIT_COPY_PARAMS,
    )(*srcs, *lands, send_sems, recv_sems, after)
    return out[n:]


def _with_own_slot(landing, own):
    slot = lax.broadcasted_iota(jnp.int32, (N_DEV,) + (1,) * own.ndim, 0)
    return jnp.where(slot == _my_index(), own[None], landing)


def _sum_slots(parts, own=None, *, name):
    shape = parts.shape[1:]
    n, c = parts.shape[0], parts.shape[-1]
    r = parts.size // (n * c)
    tm = _pick(r, (256, 128, 64, 32, 16, 8))

    def body(p_ref, *rest):
        o_ref = rest[-1]
        me = _my_index()

        def slot(s):
            if own is None:
                return p_ref[s].astype(F32)
            return jnp.where(me == s, rest[0][...], p_ref[s]).astype(F32)

        acc = slot(0)
        for s in range(1, n):
            acc = acc + slot(s)
        o_ref[...] = acc

    tile = pl.BlockSpec((tm, c), lambda i: (i, 0))
    return pl.pallas_call(
        body,
        name=name,
        grid=(r // tm,),
        in_specs=[pl.BlockSpec((n, tm, c), lambda i: (0, i, 0))] + ([] if own is None else [tile]),
        out_specs=tile,
        out_shape=jax.ShapeDtypeStruct((r, c), F32),
        compiler_params=_params(("parallel",)),
    )(parts.reshape(n, r, c), *([] if own is None else [own.reshape(r, c)])).reshape(shape)


def _row_count(shape):
    c = shape[-1]
    rows = 1
    for s in shape[:-1]:
        rows *= s
    return rows, c, c + (-c) % LANE


PACK_ROWS = 256


def _pack_rows(arrays):
    pieces = []
    for a in arrays:
        rows, c, cp = _row_count(a.shape)
        a2 = a.reshape(rows, c)
        if cp > c:
            a2 = jnp.pad(a2, ((0, 0), (0, cp - c)))
        a2 = a2.reshape(rows * cp // LANE, LANE)
        if a2.shape[0] % 8:
            a2 = jnp.pad(a2, ((0, 8 - a2.shape[0] % 8), (0, 0)))
        pieces.append(a2)
    total = sum(p.shape[0] for p in pieces)
    if total % PACK_ROWS:
        pieces.append(jnp.zeros((PACK_ROWS - total % PACK_ROWS, LANE), F32))
    return jnp.concatenate(pieces, axis=0)


def _unpack_rows(packed, shapes, lead=()):
    out, off = [], 0
    for shp in shapes:
        rows, c, cp = _row_count(shp)
        n_rows = rows * cp // LANE
        seg = packed[..., off:off + n_rows, :].reshape(lead + (rows, cp))
        out.append(seg[..., :c].reshape(lead + tuple(shp)))
        off += n_rows + (-n_rows) % 8
    return out


def _unshard(stacked, axis):
    moved = jnp.moveaxis(stacked, 0, axis)
    shp = moved.shape
    return moved.reshape(shp[:axis] + (shp[axis] * shp[axis + 1],) + shp[axis + 2:])


def _shard_major(full, axis):
    shp = full.shape
    split = full.reshape(shp[:axis] + (N_DEV, shp[axis] // N_DEV) + shp[axis + 1:])
    return jnp.moveaxis(split, axis, 0)


def _my_shard(full, axis):
    size = full.shape[axis] // N_DEV
    return lax.dynamic_slice_in_dim(full, _my_index() * size, size, axis)


def _local_step(x, target, w, fetch, emit, n_seq):
    def with_token(vec, token):
        return vec + jnp.tile(token[0:1, :], (1, vec.shape[1] // LANE))

    depth, d_model = w["norm_mix_pre"].shape
    d_inner = w["ssd_norm_w"].shape[1]
    d_xbc = w["ssd_conv_w"].shape[2]
    saved = []
    for i in range(depth):
        j = i // 2
        m, token = fetch(i, "mix", x)
        mix_pre_w = with_token(w["norm_mix_pre"][i:i + 1], token)
        s = {"x": x, "mix_pre_w": mix_pre_w}
        if i % 2 == 0:
            u = _rms_fwd(x, mix_pre_w, out_dtype=BF16, name=f"l{i}_mix_pre")
            proj = _mm(u, m["ssd_w_in"], name=f"l{i}_ssd_in")
            xbc, xbc_pre = _ssd_conv_fwd(proj, d_inner, d_xbc, w["ssd_conv_w"][j], w["ssd_conv_b"][j:j + 1], n_seq,
                                         name=f"l{i}_ssd_conv")
            yn, y, hs = _ssd_fwd(proj, xbc, w["ssd_dt_bias"][j:j + 1], w["ssd_a_log"][j:j + 1], w["ssd_d"][j:j + 1],
                                 w["ssd_norm_w"][j:j + 1], n_seq, name=f"l{i}_ssd_scan")
            m_out, token = fetch(i, "out", yn)
            m = {**m, **m_out}
            mix = _mm(yn, m["ssd_w_out"], name=f"l{i}_ssd_out")
            s.update(u=u, proj=proj, xbc=xbc, xbc_pre=xbc_pre, yn=yn, y=y, hs=hs)
        else:
            u = _rms_fwd(x, mix_pre_w, out_dtype=F32, name=f"l{i}_mix_pre")
            mix = _pool_fwd(u, m["pool_w"], w["pool_scale"][j:j + 1], n_seq, name=f"l{i}_pool")
            s.update(u=u)
            token = jnp.zeros((8, LANE), F32)
        x1 = _res_rms_fwd(x, mix, with_token(w["norm_mix_post"][i:i + 1], token), name=f"l{i}_mix_post")
        m_ffn, token = fetch(i, "ffn", x1)
        m = {**m, **m_ffn}
        ffn_pre_w = with_token(w["norm_ffn_pre"][i:i + 1], token)
        n = _rms_fwd(x1, ffn_pre_w, out_dtype=BF16, name=f"l{i}_ffn_pre")
        h, hc, a = _ffn_up_act(n, m["ffn_w_up"], w["ffn_conv_w"][i], w["ffn_conv_b"][i:i + 1], n_seq,
                               name=f"l{i}_ffn_up_act")
        f = _mm(a, m["ffn_w_down"], name=f"l{i}_ffn_down")
        x = _res_rms_fwd(x1, f, w["norm_ffn_post"][i:i + 1], name=f"l{i}_ffn_post")
        s.update(mix=mix, x1=x1, n=n, h=h, hc=hc, a=a, f=f, m=m, ffn_pre_w=ffn_pre_w)
        saved.append(s)

    loss, dx = _loss_head(x, target)
    grads = {k: [None] * len(w[k]) for k in SMALL}
    token = jnp.zeros((8, LANE), F32)
    for i in reversed(range(depth)):
        j = i // 2
        s = saved[i]
        m, gm = s["m"], {}
        df, grads["norm_ffn_post"][i] = _rms_bwd(s["f"], with_token(w["norm_ffn_post"][i:i + 1], token), dx, None,
                                                 out_dtype=BF16, name=f"l{i}_ffn_post_b")
        gm["ffn_w_down"] = _mm(s["a"], df, ta=True, name=f"l{i}_ffn_down_bw")
        dh, grads["ffn_conv_w"][i], grads["ffn_conv_b"][i] = _ffn_down_bx_act_bwd(
            df, m["ffn_w_down"], s["h"], s["hc"], w["ffn_conv_w"][i], n_seq, name=f"l{i}_ffn_act_b")
        dn = _mm(dh, m["ffn_w_up"], tb=True, name=f"l{i}_ffn_up_bx")
        gm["ffn_w_up"] = _mm(s["n"], dh, ta=True, name=f"l{i}_ffn_up_bw")
        dx1, grads["norm_ffn_pre"][i] = _rms_bwd(s["x1"], s["ffn_pre_w"], dn, dx, name=f"l{i}_ffn_pre_b")
        token = emit(i, "ffn", gm, dx1)
        gm = {}
        dmix, grads["norm_mix_post"][i] = _rms_bwd(s["mix"], with_token(w["norm_mix_post"][i:i + 1], token), dx1, None,
                                                   out_dtype=BF16 if i % 2 == 0 else F32, name=f"l{i}_mix_post_b")
        if i % 2 == 0:
            dyn = _mm(dmix, m["ssd_w_out"], tb=True, name=f"l{i}_ssd_out_bx")
            gm["ssd_w_out"] = _mm(s["yn"], dmix, ta=True, name=f"l{i}_ssd_out_bw")
            token = emit(i, "out", gm, dyn)
            gm = {}
            dxs, db, dc, dz, ddtr, dnw, dbias, dalog, ddsk = _ssd_bwd(
                s["proj"], s["xbc"], s["hs"], s["y"], dyn, w["ssd_dt_bias"][j:j + 1], w["ssd_a_log"][j:j + 1],
                w["ssd_d"][j:j + 1], with_token(w["ssd_norm_w"][j:j + 1], token), n_seq, name=f"l{i}_ssd_scan_b")
            grads["ssd_norm_w"][j], grads["ssd_dt_bias"][j], grads["ssd_a_log"][j], grads["ssd_d"][j] = (
                dnw, dbias, dalog, ddsk)
            dproj, grads["ssd_conv_w"][j], grads["ssd_conv_b"][j] = _ssd_conv_bwd(
                s["proj"], d_inner, w["ssd_conv_w"][j], s["xbc_pre"], (dxs, db, dc), dz, n_seq,
                name=f"l{i}_ssd_conv_b")
            dproj = _fill_cols(dproj, ddtr, d_inner + d_xbc, name=f"l{i}_ssd_dt_b")
            du = _mm(dproj, m["ssd_w_in"], tb=True, name=f"l{i}_ssd_in_bx")
            gm["ssd_w_in"] = _mm(s["u"], dproj, ta=True, name=f"l{i}_ssd_in_bw")
        else:
            du, gm["pool_w"], grads["pool_scale"][j] = _pool_bwd(
                s["u"], m["pool_w"], w["pool_scale"][j:j + 1], dmix, n_seq, name=f"l{i}_pool_b")
        dx, grads["norm_mix_pre"][i] = _rms_bwd(s["x"], s["mix_pre_w"], du, dx1, name=f"l{i}_mix_pre_b")
        token = emit(i, "mix", gm, dx)
    return loss, dx, grads


BIG = (("ssd_w_in", 2), ("ssd_w_out", 1), ("pool_w", 2), ("ffn_w_up", 2), ("ffn_w_down", 1))
SMALL_SHARDED = (("ssd_conv_w", 2), ("ffn_conv_w", 2), ("pool_scale", 1))
SMALL = ("ssd_conv_w", "ssd_conv_b", "ssd_dt_bias", "ssd_a_log", "ssd_d", "ssd_norm_w", "pool_scale", "ffn_conv_w",
         "ffn_conv_b", "norm_mix_pre", "norm_mix_post", "norm_ffn_pre", "norm_ffn_post")
WEIGHTS = ("ssd_w_in", "ssd_conv_w", "ssd_conv_b", "ssd_dt_bias", "ssd_a_log", "ssd_d", "ssd_norm_w", "ssd_w_out",
           "pool_w", "pool_scale", "ffn_w_up", "ffn_conv_w", "ffn_conv_b", "ffn_w_down", "norm_mix_pre",
           "norm_mix_post", "norm_ffn_pre", "norm_ffn_post")


def _ssd_sizes(d_inner):
    return d_inner + 2 * N_SSD_GROUPS * D_STATE, d_inner // HEAD_DIM // N_SSD_GROUPS


def _small_compute_layout(full, d_inner):
    _, r_heads = _ssd_sizes(d_inner)
    w = {k: full[k] for k in SMALL}
    for k in ("ssd_dt_bias", "ssd_a_log", "ssd_d"):
        w[k] = _head_pad(full[k], r_heads)
    for k in ("ffn_conv_w", "ffn_conv_b"):
        w[k] = _interleave(full[k])
    return w


def _matmul_compute_layout(k, full, d_inner):
    d_xbc, r_heads = _ssd_sizes(d_inner)
    if k == "ssd_w_in":
        return _ssd_w_in_layout(full, d_inner, d_xbc, r_heads)
    if k == "ffn_w_up":
        return _interleave(full)
    return full


def _layer_matrices(i, part):
    if part == "ffn":
        return (("ffn_w_up", 1, i), ("ffn_w_down", 0, i))
    if i % 2 == 1:
        return (("pool_w", 1, i // 2),) if part == "mix" else ()
    return (("ssd_w_in", 1, i // 2),) if part == "mix" else (("ssd_w_out", 0, i // 2),)


def _fetch_group(i, part):
    mix, out, ffn = (_layer_matrices(i, p) for p in ("mix", "out", "ffn"))
    if i % 2 == 1:
        return mix + ffn if part == "mix" else ()
    if i == 0:
        return {"mix": mix, "out": out + ffn, "ffn": ()}[part]
    return {"mix": mix + out, "out": (), "ffn": ffn}[part]


def _matmul_grad_reference_layout(k, g, d_inner):
    d_xbc, r_heads = _ssd_sizes(d_inner)
    if k == "ssd_w_in":
        return _ssd_w_in_unlayout(g, d_inner, d_xbc, r_heads)
    if k == "ffn_w_up":
        return _deinterleave(g)
    return g


def _small_grads_reference_layout(grads, shapes, d_inner):
    _, r_heads = _ssd_sizes(d_inner)
    g = {k: jnp.stack(grads[k]) for k in SMALL}
    for k in ("ssd_dt_bias", "ssd_a_log", "ssd_d"):
        g[k] = _head_unpad(g[k][:, 0], r_heads)
    for k in ("ffn_conv_w", "ffn_conv_b"):
        g[k] = _deinterleave(g[k])
    return {k: v.reshape(shapes[k]) for k, v in g.items()}


def kernel(x, ssd_w_in, ssd_conv_w, ssd_conv_b, ssd_dt_bias, ssd_a_log, ssd_d, ssd_norm_w, ssd_w_out, pool_w, pool_scale, ffn_w_up, ffn_conv_w, ffn_conv_b, ffn_w_down, norm_mix_pre, norm_mix_post, norm_ffn_pre, norm_ffn_post, loss_target, m_ssd_w_in, m_ssd_conv_w, m_ssd_conv_b, m_ssd_dt_bias, m_ssd_a_log, m_ssd_d, m_ssd_norm_w, m_ssd_w_out, m_pool_w, m_pool_scale, m_ffn_w_up, m_ffn_conv_w, m_ffn_conv_b, m_ffn_w_down, m_norm_mix_pre, m_norm_mix_post, m_norm_ffn_pre, m_norm_ffn_post, v_ssd_w_in, v_ssd_conv_w, v_ssd_conv_b, v_ssd_dt_bias, v_ssd_a_log, v_ssd_d, v_ssd_norm_w, v_ssd_w_out, v_pool_w, v_pool_scale, v_ffn_w_up, v_ffn_conv_w, v_ffn_conv_b, v_ffn_w_down, v_norm_mix_pre, v_norm_mix_post, v_norm_ffn_pre, v_norm_ffn_post):
    shards = dict(ssd_w_in=ssd_w_in, ssd_conv_w=ssd_conv_w, ssd_conv_b=ssd_conv_b, ssd_dt_bias=ssd_dt_bias,
                  ssd_a_log=ssd_a_log, ssd_d=ssd_d, ssd_norm_w=ssd_norm_w, ssd_w_out=ssd_w_out, pool_w=pool_w,
                  pool_scale=pool_scale, ffn_w_up=ffn_w_up, ffn_conv_w=ffn_conv_w, ffn_conv_b=ffn_conv_b,
                  ffn_w_down=ffn_w_down, norm_mix_pre=norm_mix_pre, norm_mix_post=norm_mix_post,
                  norm_ffn_pre=norm_ffn_pre, norm_ffn_post=norm_ffn_post)
    moments_m = dict(zip(WEIGHTS, (m_ssd_w_in, m_ssd_conv_w, m_ssd_conv_b, m_ssd_dt_bias, m_ssd_a_log, m_ssd_d, m_ssd_norm_w, m_ssd_w_out, m_pool_w, m_pool_scale, m_ffn_w_up, m_ffn_conv_w, m_ffn_conv_b, m_ffn_w_down, m_norm_mix_pre, m_norm_mix_post, m_norm_ffn_pre, m_norm_ffn_post)))
    moments_v = dict(zip(WEIGHTS, (v_ssd_w_in, v_ssd_conv_w, v_ssd_conv_b, v_ssd_dt_bias, v_ssd_a_log, v_ssd_d, v_ssd_norm_w, v_ssd_w_out, v_pool_w, v_pool_scale, v_ffn_w_up, v_ffn_conv_w, v_ffn_conv_b, v_ffn_w_down, v_norm_mix_pre, v_norm_mix_post, v_norm_ffn_pre, v_norm_ffn_post)))
    n_seq, seq, d_model = x.shape
    t = n_seq * seq

    d_inner = ssd_norm_w.shape[1]
    depth = norm_mix_pre.shape[0]
    x2 = x.reshape(t, d_model)

    shard16 = {k: shards[k].astype(BF16) for k, _ in BIG}
    order = [(i, part) for i in range(depth) for part in ("mix", "out", "ffn") if _fetch_group(i, part)]
    fetches = {}

    def start_fetch(key, after):
        srcs = [shard16[k][l] for k, _, l in _fetch_group(*key)]
        fetches[key] = _push_start(srcs, False, after, name=f"fetch{key[0]}{key[1]}_start")

    full = dict(shards)
    small_all = _all_gather(_pack_rows([shards[k] for k, _ in SMALL_SHARDED]), name="gather_small_weights")
    small_stacked = _unpack_rows(small_all, [shards[k].shape for k, _ in SMALL_SHARDED], lead=(N_DEV,))
    for (k, axis), st in zip(SMALL_SHARDED, small_stacked):
        full[k] = _unshard(st, axis)
    w = _small_compute_layout(full, d_inner)
    ready = {}

    def fetch(i, part, x_now):
        key = (i, part)
        token = jnp.zeros((8, LANE), F32)
        if key in order:
            if key == order[0]:
                wholes = [_unshard(_all_gather(shard16[k][l], name=f"fetch0_{k}"), axis) for k, axis, l in _fetch_group(i, part)]
                nxt_after = wholes[0]
            else:
                send, recv, srcs, lands, _ = fetches[key]
                lands = _push_wait(send, recv, srcs, lands, False, x_now, name=f"fetch{i}{part}_wait")
                wholes = [_unshard(_with_own_slot(land, shard16[k][l]), axis)
                          for (k, axis, l), land in zip(_fetch_group(i, part), lands)]
                nxt_after = lands[0]
            for (k, _, l), whole in zip(_fetch_group(i, part), wholes):
                ready[k, l] = _matmul_compute_layout(k, whole, d_inner)
            nxt = order.index(key) + 1
            if nxt < len(order):
                start_fetch(order[nxt], nxt_after)
                token = fetches[order[nxt]][4]
        return {k: ready[k, l] for k, _, l in _layer_matrices(i, part)}, token

    g_layers = {}
    in_flight = []

    def finish_exchange(after):
        key, blocks, (send, recv, srcs, lands, _) = in_flight.pop()
        lands = _push_wait(send, recv, srcs, lands, True, after, name=f"exchange{key[0]}{key[1]}_wait")
        for (k, _, l), land, block in zip(_layer_matrices(*key), lands, blocks):
            own = lax.dynamic_index_in_dim(block, _my_index(), 0, keepdims=False)
            g_layers[k, l] = _sum_slots(land, own, name=f"sum{key[0]}_{k}")

    def emit(i, part, gm, dx_now):
        if in_flight:
            finish_exchange(dx_now)
        blocks = [_shard_major(_matmul_grad_reference_layout(k, gm[k].astype(BF16), d_inner), axis)
                  for k, axis, _ in _layer_matrices(i, part)]
        started = _push_start(blocks, True, dx_now, name=f"exchange{i}{part}_start")
        in_flight.append(((i, part), blocks, started))
        return started[4]

    loss, dx, grads = _local_step(x2, loss_target.reshape(t, d_model), w, fetch, emit, n_seq)
    loss = lax.psum(loss, ("x", "y", "c"))

    g_shard = {}
    small_shapes = {k: full[k].shape for k in SMALL}
    g_small = _small_grads_reference_layout(grads, small_shapes, d_inner)
    s_all = _all_gather(_pack_rows([g_small[k] for k in SMALL]) + in_flight[0][2][4][0:1, :], name="gather_small_grads")
    for k, g in zip(SMALL, _unpack_rows(_sum_slots(s_all, name="sum_small_grads"), [small_shapes[k] for k in SMALL])):
        g_shard[k] = g
    for k, axis in SMALL_SHARDED:
        g_shard[k] = _my_shard(g_shard[k], axis)

    last = [k for k, _, _ in _layer_matrices(*in_flight[0][0])]
    deltas, new_m, new_v = {}, {}, {}
    for k in [k for k in WEIGHTS if k not in last] + last:
        if k == last[0]:
            finish_exchange(deltas["ffn_w_up"])
        if k in dict(BIG):
            g_shard[k] = jnp.stack([g_layers[k, l] for l in range(shards[k].shape[0])])
        deltas[k], new_m[k], new_v[k] = _adamw(shards[k], g_shard[k], moments_m[k], moments_v[k], name=f"adamw_{k}")
    return (loss, dx.reshape(x.shape), *[g_shard[k] for k in WEIGHTS], *[deltas[k] for k in WEIGHTS],
            *[new_m[k] for k in WEIGHTS], *[new_v[k] for k in WEIGHTS])
```

```python
import functools

import jax
import jax.numpy as jnp
from jax import lax
from jax.experimental import pallas as pl
from jax.experimental.pallas import tpu as pltpu

F32 = jnp.float32
BF16 = jnp.bfloat16

N_DEV = 8
HEAD_DIM = 64
N_SSD_GROUPS = 4
D_STATE = 128
CHUNK = 128
POOL_WINDOWS = (2, 4, 8, 16)
EPS = 1e-6
LANE = 128
ADAM_LR = 0.001
ADAM_B1 = 0.9
ADAM_B2 = 0.999
ADAM_EPS = 1e-08
ADAM_WD = 0.01
ADAM_STEP = 10
VMEM_LIMIT = 56 * 1024 * 1024
ANY = pl.BlockSpec(memory_space=pl.ANY)


def _pick(n, cands):
    for c in cands:
        if n % c == 0:
            return c
    return n


def _params(sem):
    return pltpu.CompilerParams(dimension_semantics=sem, vmem_limit_bytes=VMEM_LIMIT)


def _sigmoid(x):
    return 0.5 * jnp.tanh(0.5 * x) + 0.5


def _silu(x):
    return x * _sigmoid(x)


def _dsilu(x):
    s = _sigmoid(x)
    return s * (1.0 + x * (1.0 - s))


def _shift_down(x, s):
    rows = lax.broadcasted_iota(jnp.int32, x.shape, 0)
    return jnp.where(rows >= s, pltpu.roll(x, s, 0), 0.0)


def _shift_up(x, s):
    n = x.shape[0]
    rows = lax.broadcasted_iota(jnp.int32, x.shape, 0)
    return jnp.where(rows < n - s, pltpu.roll(x, n - s, 0), 0.0)


MM_VMEM_BUDGET = 40 * 1024 * 1024
MM_STEP_BYTES = 1_300_000
MM_SUB = 512


def _mm_tiles(m, n, k, a_bytes, b_bytes, o_bytes):
    def cands(dim, sizes):
        out = [s for s in sizes if s <= dim and dim % s == 0]
        return out or [dim]

    best = None
    for tm in cands(m, (m, m // 2, 2048, 1024, 512, 256, 128)):
        if tm % LANE:
            continue
        for tn in cands(n, (n, n // 2, n // 4, 2048, 1024, 512, 256, 128)):
            if tn % (2 * LANE) and tn != n:
                continue
            for tk in cands(k, (k, k // 2, 2048, 1024, 512)):
                if tk % LANE:
                    continue
                nk = k // tk
                acc = tm * tn * 4 if (nk > 1 and o_bytes != 4) else 0
                temps = tm * min(tn, MM_SUB) * 4 + (tm * tk * 2 if a_bytes == 4 else 0) + (tk * tn * 2 if b_bytes == 4 else 0)
                vmem = 2 * (tm * tk * a_bytes + tk * tn * b_bytes + tm * tn * o_bytes) + acc + temps
                if vmem > MM_VMEM_BUDGET:
                    continue
                steps = (m // tm) * (n // tn) * nk
                cost = (m * k * a_bytes * (n // tn) + k * n * b_bytes * (m // tm) + m * n * o_bytes
                        + steps * MM_STEP_BYTES)
                if best is None or cost < best[0]:
                    best = (cost, tm, tn, tk)
    return best[1:]


def _mm(a, b, *, ta=False, tb=False, out_dtype=F32, name="mm"):
    m, k = (a.shape[1], a.shape[0]) if ta else a.shape
    n = b.shape[0] if tb else b.shape[1]
    o_bytes = jnp.dtype(out_dtype).itemsize
    tm, tn, tk = _mm_tiles(m, n, k, a.dtype.itemsize, b.dtype.itemsize, o_bytes)
    nk = k // tk
    sub = _pick(tn, (MM_SUB, 256))
    use_acc = nk > 1 and o_bytes != 4
    a_spec = pl.BlockSpec((tk, tm), lambda i, j, kk: (kk, i)) if ta else pl.BlockSpec((tm, tk), lambda i, j, kk: (i, kk))
    b_spec = pl.BlockSpec((tn, tk), lambda i, j, kk: (j, kk)) if tb else pl.BlockSpec((tk, tn), lambda i, j, kk: (kk, j))
    dims = (((1,), (1 if tb else 0,)), ((), ()))

    def body(a_ref, b_ref, o_ref, *scratch):
        kk = pl.program_id(2)
        acc_ref = scratch[0] if use_acc else o_ref
        if nk > 1:
            @pl.when(kk == 0)
            def _():
                acc_ref[...] = jnp.zeros_like(acc_ref)

        av = a_ref[...].astype(BF16)
        if ta:
            av = av.T
        for s in range(tn // sub):
            cols = slice(s * sub, (s + 1) * sub)
            bv = (b_ref[cols, :] if tb else b_ref[:, cols]).astype(BF16)
            part = lax.dot_general(av, bv, dims, preferred_element_type=F32)
            if nk == 1:
                o_ref[:, cols] = part.astype(out_dtype)
            else:
                acc_ref[:, cols] += part
        if use_acc:
            @pl.when(kk == nk - 1)
            def _():
                o_ref[...] = acc_ref[...].astype(out_dtype)

    return pl.pallas_call(
        body,
        name=name,
        grid=(m // tm, n // tn, nk),
        in_specs=[a_spec, b_spec],
        out_specs=pl.BlockSpec((tm, tn), lambda i, j, kk: (i, j)),
        out_shape=jax.ShapeDtypeStruct((m, n), out_dtype),
        scratch_shapes=[pltpu.VMEM((tm, tn), F32)] if use_acc else [],
        compiler_params=_params(("parallel", "parallel", "arbitrary")),
    )(a, b)


def _rms_fwd(x, w, *, out_dtype, name):
    t, d = x.shape
    tm = _pick(t, (512, 256, 128))

    def body(x_ref, w_ref, o_ref):
        xv = x_ref[...]
        rstd = lax.rsqrt(jnp.mean(xv * xv, axis=-1, keepdims=True) + EPS)
        o_ref[...] = (xv * rstd * w_ref[...]).astype(out_dtype)

    return pl.pallas_call(
        body,
        name=name,
        grid=(t // tm,),
        in_specs=[pl.BlockSpec((tm, d), lambda i: (i, 0)), pl.BlockSpec((1, d), lambda i: (0, 0))],
        out_specs=pl.BlockSpec((tm, d), lambda i: (i, 0)),
        out_shape=jax.ShapeDtypeStruct((t, d), out_dtype),
        compiler_params=_params(("parallel",)),
    )(x, w)


def _res_rms_fwd(x, f, w, *, name):
    t, d = x.shape
    tm = _pick(t, (512, 256, 128))

    def body(x_ref, f_ref, w_ref, o_ref):
        fv = f_ref[...]
        rstd = lax.rsqrt(jnp.mean(fv * fv, axis=-1, keepdims=True) + EPS)
        o_ref[...] = x_ref[...] + fv * rstd * w_ref[...]

    row = pl.BlockSpec((tm, d), lambda i: (i, 0))
    return pl.pallas_call(
        body,
        name=name,
        grid=(t // tm,),
        in_specs=[row, row, pl.BlockSpec((1, d), lambda i: (0, 0))],
        out_specs=row,
        out_shape=jax.ShapeDtypeStruct((t, d), F32),
        compiler_params=_params(("parallel",)),
    )(x, f, w)


def _rms_bwd(x, w, dy, resid, *, out_dtype=F32, name):
    t, d = x.shape
    tm = _pick(t, (512, 256, 128))
    has_res = resid is not None

    def body(*refs):
        if has_res:
            x_ref, w_ref, dy_ref, r_ref, dx_ref, dw_ref = refs
        else:
            x_ref, w_ref, dy_ref, dx_ref, dw_ref = refs
        xv = x_ref[...]
        dyv = dy_ref[...].astype(F32)
        rstd = lax.rsqrt(jnp.mean(xv * xv, axis=-1, keepdims=True) + EPS)
        xn = xv * rstd
        g = dyv * w_ref[...]
        dx = rstd * (g - xn * jnp.mean(g * xn, axis=-1, keepdims=True))
        if has_res:
            dx = dx + r_ref[...]
        dx_ref[...] = dx.astype(out_dtype)
        part = jnp.sum(dyv * xn, axis=0, keepdims=True)

        @pl.when(pl.program_id(0) == 0)
        def _():
            dw_ref[...] = part

        @pl.when(pl.program_id(0) > 0)
        def _():
            dw_ref[...] += part

    row = pl.BlockSpec((tm, d), lambda i: (i, 0))
    vec = pl.BlockSpec((1, d), lambda i: (0, 0))
    ins = [x, w, dy] + ([resid] if has_res else [])
    return pl.pallas_call(
        body,
        name=name,
        grid=(t // tm,),
        in_specs=[row, vec, row] + ([row] if has_res else []),
        out_specs=[row, vec],
        out_shape=[jax.ShapeDtypeStruct((t, d), out_dtype), jax.ShapeDtypeStruct((1, d), F32)],
        compiler_params=_params(("arbitrary",)),
    )(*ins)


def _loss_head(y, target, *, name="loss_head"):
    t, d = y.shape
    tm = _pick(t, (512, 256, 128))

    def body(y_ref, t_ref, dy_ref, l_ref):
        err = y_ref[...] - t_ref[...]
        dy_ref[...] = err * (1.0 / d)
        part = jnp.sum(jnp.sum(err * err, axis=-1, keepdims=True), axis=0, keepdims=True) * (0.5 / d)
        part = jnp.broadcast_to(part, (1, LANE))

        @pl.when(pl.program_id(0) == 0)
        def _():
            l_ref[...] = part

        @pl.when(pl.program_id(0) > 0)
        def _():
            l_ref[...] += part

    row = pl.BlockSpec((tm, d), lambda i: (i, 0))
    dy, l = pl.pallas_call(
        body,
        name=name,
        grid=(t // tm,),
        in_specs=[row, row],
        out_specs=[row, pl.BlockSpec((1, LANE), lambda i: (0, 0))],
        out_shape=[jax.ShapeDtypeStruct((t, d), F32), jax.ShapeDtypeStruct((1, LANE), F32)],
        compiler_params=_params(("arbitrary",)),
    )(y, target)
    return l[0, 0], dy


def _conv_taps(h, w_ref, k_taps):
    out = h * w_ref[k_taps - 1:k_taps, :]
    for k in range(k_taps - 1):
        out = out + _shift_down(h, k_taps - 1 - k) * w_ref[k:k + 1, :]
    return out


def _conv_taps_bwd(h, dhc, w_ref, k_taps):
    dh = dhc * w_ref[k_taps - 1:k_taps, :]
    dws = []
    for k in range(k_taps - 1):
        up = _shift_up(dhc, k_taps - 1 - k)
        dh = dh + up * w_ref[k:k + 1, :]
        dws.append(jnp.sum(up * h, axis=0, keepdims=True))
    dws.append(jnp.sum(dhc * h, axis=0, keepdims=True))
    return dh, jnp.concatenate(dws, axis=0)


FFN_TC = 256


def _interleave(w, tc=FFN_TC):
    f = w.shape[-1] // 2
    lead = w.shape[:-1]
    return jnp.swapaxes(w.reshape(lead + (2, f // tc, tc)), -3, -2).reshape(lead + (2 * f,))


def _deinterleave(w, tc=FFN_TC):
    f = w.shape[-1] // 2
    lead = w.shape[:-1]
    return jnp.swapaxes(w.reshape(lead + (f // tc, 2, tc)), -3, -2).reshape(lead + (2 * f,))


FFN_ROWS = 512
HALO = 8


def _ffn_up_act(n, w_up, conv_w, conv_b, n_seq, *, name):
    t, d = n.shape
    f2 = w_up.shape[1]
    seq = t // n_seq
    tc = FFN_TC
    nj = f2 // (2 * tc)
    k_taps = conv_w.shape[0]
    rows = min(FFN_ROWS, seq)

    def body(n_ref, wu_ref, w_ref, b_ref, h_ref, hc_ref, o_ref, h_scr):
        h_scr[0:HALO, :] = jnp.zeros((HALO, 2 * tc), F32)
        wu = wu_ref[...]
        for r in range(seq // rows):
            chunk = slice(r * rows, (r + 1) * rows)
            h = jnp.dot(n_ref[chunk, :], wu, preferred_element_type=F32)
            h_scr[HALO + r * rows:HALO + (r + 1) * rows, :] = h
            h_ref[chunk, :] = h.astype(BF16)
            ext = h_scr[r * rows:HALO + (r + 1) * rows, :]
            hc = ext * w_ref[k_taps - 1:k_taps, :]
            for k in range(k_taps - 1):
                hc = hc + pltpu.roll(ext, k_taps - 1 - k, 0) * w_ref[k:k + 1, :]
            hc = hc[HALO:, :] + b_ref[...]
            hc_ref[chunk, :] = hc.astype(BF16)
            o_ref[chunk, :] = (_silu(hc[:, :tc]) * hc[:, tc:]).astype(BF16)

    wide = pl.BlockSpec((seq, 2 * tc), lambda b, j: (b, j))
    return pl.pallas_call(
        body,
        name=name,
        grid=(n_seq, nj),
        in_specs=[
            pl.BlockSpec((seq, d), lambda b, j: (b, 0)),
            pl.BlockSpec((d, 2 * tc), lambda b, j: (0, j)),
            pl.BlockSpec((k_taps, 2 * tc), lambda b, j: (0, j)),
            pl.BlockSpec((1, 2 * tc), lambda b, j: (0, j)),
        ],
        out_specs=[wide, wide, pl.BlockSpec((seq, tc), lambda b, j: (b, j))],
        out_shape=[jax.ShapeDtypeStruct((t, f2), BF16), jax.ShapeDtypeStruct((t, f2), BF16),
                   jax.ShapeDtypeStruct((t, f2 // 2), BF16)],
        scratch_shapes=[pltpu.VMEM((HALO + seq, 2 * tc), F32)],
        compiler_params=_params(("parallel", "arbitrary")),
    )(n, w_up, conv_w, conv_b)


def _ffn_down_bx_act_bwd(df, w_down, h, hc, conv_w, n_seq, *, name):
    t, d = df.shape
    f2 = h.shape[1]
    seq = t // n_seq
    tc = FFN_TC
    nj = f2 // (2 * tc)
    k_taps = conv_w.shape[0]

    def body(df_ref, wd_ref, h_ref, hc_ref, w_ref, dh_ref, dw_ref, db_ref):
        dav = lax.dot_general(df_ref[...], wd_ref[...], (((1,), (1,)), ((), ())), preferred_element_type=F32)
        hcv = hc_ref[...].astype(F32)
        gate, val = hcv[:, :tc], hcv[:, tc:]
        dhc = jnp.concatenate([dav * val * _dsilu(gate), dav * _silu(gate)], axis=1)
        dh, dw = _conv_taps_bwd(h_ref[...].astype(F32), dhc, w_ref, k_taps)
        dh_ref[...] = dh.astype(BF16)
        dw_ref[0] = dw
        db_ref[0] = jnp.sum(dhc, axis=0, keepdims=True)

    wide = pl.BlockSpec((seq, 2 * tc), lambda b, j: (b, j))
    dh, dw, db = pl.pallas_call(
        body,
        name=name,
        grid=(n_seq, nj),
        in_specs=[
            pl.BlockSpec((seq, d), lambda b, j: (b, 0)),
            pl.BlockSpec((tc, d), lambda b, j: (j, 0)),
            wide, wide,
            pl.BlockSpec((k_taps, 2 * tc), lambda b, j: (0, j)),
        ],
        out_specs=[
            wide,
            pl.BlockSpec((1, k_taps, 2 * tc), lambda b, j: (b, 0, j)),
            pl.BlockSpec((1, 1, 2 * tc), lambda b, j: (b, 0, j)),
        ],
        out_shape=[
            jax.ShapeDtypeStruct((t, f2), BF16),
            jax.ShapeDtypeStruct((n_seq, k_taps, f2), F32),
            jax.ShapeDtypeStruct((n_seq, 1, f2), F32),
        ],
        compiler_params=_params(("parallel", "arbitrary")),
    )(df, w_down, h, hc, conv_w)
    return dh, jnp.sum(dw, axis=0), jnp.sum(db, axis=0)


def _window_mixed(u, window):
    s = u
    step = 1
    while step < window:
        s = s + _shift_down(s, step)
        step *= 2
    rows = lax.broadcasted_iota(jnp.int32, u.shape, 0)
    inv_cnt = 1.0 / jnp.minimum(rows + 1, window).astype(F32)
    return s * inv_cnt - u, inv_cnt


def _window_mixed_bwd(dmixed, inv_cnt, window):
    r = dmixed * inv_cnt
    s = r
    step = 1
    while step < window:
        s = s + _shift_up(s, step)
        step *= 2
    return s - dmixed


def _pool_fwd(u, w, scale, n_seq, *, name):
    t, d = u.shape
    seq = t // n_seq
    n_g, dg, _ = w.shape

    def body(u_ref, w_ref, s_ref, o_ref):
        for k, window in enumerate(POOL_WINDOWS):
            @pl.when(pl.program_id(1) == k)
            def _(window=window):
                mixed, _ = _window_mixed(u_ref[...], window)
                pre = jnp.dot(mixed.astype(BF16), w_ref[0].astype(BF16), preferred_element_type=F32)
                o_ref[...] = pre * s_ref[...]

    return pl.pallas_call(
        body,
        name=name,
        grid=(n_seq, n_g),
        in_specs=[
            pl.BlockSpec((seq, dg), lambda b, g: (b, g)),
            pl.BlockSpec((1, dg, dg), lambda b, g: (g, 0, 0)),
            pl.BlockSpec((1, dg), lambda b, g: (0, g)),
        ],
        out_specs=pl.BlockSpec((seq, dg), lambda b, g: (b, g)),
        out_shape=jax.ShapeDtypeStruct((t, d), F32),
        compiler_params=_params(("parallel", "parallel")),
    )(u, w, scale)


def _pool_bwd(u, w, scale, dout, n_seq, *, name):
    t, d = u.shape
    seq = t // n_seq
    n_g, dg, _ = w.shape

    def body(u_ref, w_ref, s_ref, do_ref, du_ref, dw_ref, ds_ref):
        group = pl.program_id(0)
        first = pl.program_id(1) == 0
        for k, window in enumerate(POOL_WINDOWS):
            @pl.when(group == k)
            def _(window=window):
                mixed, inv_cnt = _window_mixed(u_ref[...], window)
                mixed_b = mixed.astype(BF16)
                w_b = w_ref[0].astype(BF16)
                dov = do_ref[...]
                pre = jnp.dot(mixed_b, w_b, preferred_element_type=F32)
                dsc = jnp.sum(dov * pre, axis=0, keepdims=True)
                dpre = (dov * s_ref[...]).astype(BF16)
                dw = lax.dot_general(mixed_b, dpre, (((0,), (0,)), ((), ())), preferred_element_type=F32)
                dmixed = lax.dot_general(dpre, w_b, (((1,), (1,)), ((), ())), preferred_element_type=F32)
                du_ref[...] = _window_mixed_bwd(dmixed, inv_cnt, window)

                @pl.when(first)
                def _():
                    dw_ref[0] = dw
                    ds_ref[...] = dsc

                @pl.when(jnp.logical_not(first))
                def _():
                    dw_ref[0] += dw
                    ds_ref[...] += dsc

    return pl.pallas_call(
        body,
        name=name,
        grid=(n_g, n_seq),
        in_specs=[
            pl.BlockSpec((seq, dg), lambda g, b: (b, g)),
            pl.BlockSpec((1, dg, dg), lambda g, b: (g, 0, 0)),
            pl.BlockSpec((1, dg), lambda g, b: (0, g)),
            pl.BlockSpec((seq, dg), lambda g, b: (b, g)),
        ],
        out_specs=[
            pl.BlockSpec((seq, dg), lambda g, b: (b, g)),
            pl.BlockSpec((1, dg, dg), lambda g, b: (g, 0, 0)),
            pl.BlockSpec((1, dg), lambda g, b: (0, g)),
        ],
        out_shape=[
            jax.ShapeDtypeStruct((t, d), F32),
            jax.ShapeDtypeStruct((n_g, dg, dg), F32),
            jax.ShapeDtypeStruct((1, d), F32),
        ],
        compiler_params=_params(("parallel", "arbitrary")),
    )(u, w, scale, dout)


def _adamw(w, g, m, v, *, name):
    shape = w.shape
    c = shape[-1]
    r = w.size // c
    tm = _pick(r, (512, 256, 128, 64, 32, 16, 8))

    def body(w_ref, g_ref, m_ref, v_ref, d_ref, nm_ref, nv_ref):
        gv = g_ref[...]
        nm = ADAM_B1 * m_ref[...] + (1.0 - ADAM_B1) * gv
        nv = ADAM_B2 * v_ref[...] + (1.0 - ADAM_B2) * (gv * gv)
        m_hat = nm / (1.0 - ADAM_B1 ** ADAM_STEP)
        v_hat = nv / (1.0 - ADAM_B2 ** ADAM_STEP)
        d_ref[...] = -ADAM_LR * (m_hat / (jnp.sqrt(v_hat) + ADAM_EPS) + ADAM_WD * w_ref[...])
        nm_ref[...] = nm
        nv_ref[...] = nv

    blk = pl.BlockSpec((tm, c), lambda i: (i, 0))
    out = jax.ShapeDtypeStruct((r, c), F32)
    res = pl.pallas_call(
        body,
        name=name,
        grid=(r // tm,),
        in_specs=[blk] * 4,
        out_specs=[blk] * 3,
        out_shape=[out] * 3,
        compiler_params=_params(("parallel",)),
    )(w.reshape(r, c), g.reshape(r, c), m.reshape(r, c), v.reshape(r, c))
    return tuple(a.reshape(shape) for a in res)


CONV_TC = 256


def _ssd_conv_fwd(proj, col0, n_cols, conv_w, conv_b, n_seq, *, name):
    t = proj.shape[0]
    seq = t // n_seq
    tc = CONV_TC
    off = col0 // tc
    k_taps = conv_w.shape[0]

    def body(h_ref, w_ref, b_ref, o_ref, pre_ref):
        pre = _conv_taps(h_ref[...], w_ref, k_taps) + b_ref[...]
        pre_ref[...] = pre.astype(BF16)
        o_ref[...] = _silu(pre)

    return pl.pallas_call(
        body,
        name=name,
        grid=(n_seq, n_cols // tc),
        in_specs=[
            pl.BlockSpec((seq, tc), lambda b, j: (b, j + off)),
            pl.BlockSpec((k_taps, tc), lambda b, j: (0, j)),
            pl.BlockSpec((1, tc), lambda b, j: (0, j)),
        ],
        out_specs=[pl.BlockSpec((seq, tc), lambda b, j: (b, j))] * 2,
        out_shape=[jax.ShapeDtypeStruct((t, n_cols), F32), jax.ShapeDtypeStruct((t, n_cols), BF16)],
        compiler_params=_params(("parallel", "parallel")),
    )(proj, conv_w, conv_b)


def _ssd_conv_bwd(proj, col0, conv_w, pre, dparts, dproj, n_seq, *, name):
    t = proj.shape[0]
    seq = t // n_seq
    tc = CONV_TC
    off = col0 // tc
    k_taps = conv_w.shape[0]
    widths = [d.shape[1] // tc for d in dparts]
    starts = [sum(widths[:i]) for i in range(len(widths))]
    n_blocks = sum(widths)
    n_parts = len(dparts)

    def body(h_ref, w_ref, pre_ref, *rest):
        part_refs = rest[:n_parts]
        dh_ref, dw_ref, db_ref = rest[n_parts + 1:]
        j = pl.program_id(0)
        da = part_refs[-1][...]
        for i in reversed(range(n_parts - 1)):
            da = jnp.where(j < starts[i + 1], part_refs[i][...], da)
        dhc = da * _dsilu(pre_ref[...].astype(F32))
        dh, dw = _conv_taps_bwd(h_ref[...], dhc, w_ref, k_taps)
        dh_ref[...] = dh.astype(BF16)
        db = jnp.sum(dhc, axis=0, keepdims=True)

        @pl.when(pl.program_id(1) == 0)
        def _():
            dw_ref[...] = dw
            db_ref[...] = db

        @pl.when(pl.program_id(1) > 0)
        def _():
            dw_ref[...] += dw
            db_ref[...] += db

    def part_spec(start, width):
        return pl.BlockSpec((seq, tc), lambda j, b: (b, jnp.clip(j - start, 0, width - 1)))

    n_cols = n_blocks * tc
    return pl.pallas_call(
        body,
        name=name,
        grid=(n_blocks, n_seq),
        in_specs=[
            pl.BlockSpec((seq, tc), lambda j, b: (b, j + off)),
            pl.BlockSpec((k_taps, tc), lambda j, b: (0, j)),
            pl.BlockSpec((seq, tc), lambda j, b: (b, j)),
        ] + [part_spec(st, wd) for st, wd in zip(starts, widths)] + [ANY],
        out_specs=[
            pl.BlockSpec((seq, tc), lambda j, b: (b, j + off)),
            pl.BlockSpec((k_taps, tc), lambda j, b: (0, j)),
            pl.BlockSpec((1, tc), lambda j, b: (0, j)),
        ],
        out_shape=[
            jax.ShapeDtypeStruct(dproj.shape, BF16),
            jax.ShapeDtypeStruct((k_taps, n_cols), F32),
            jax.ShapeDtypeStruct((1, n_cols), F32),
        ],
        input_output_aliases={3 + n_parts: 0},
        compiler_params=_params(("parallel", "arbitrary")),
    )(proj, conv_w, pre, *dparts, dproj)


def _fill_cols(buf, src, col0, *, name):
    t, c = src.shape
    tm = _pick(t, (1024, 512, 256, 128))

    def body(s_ref, b_ref, o_ref):
        o_ref[...] = s_ref[...].astype(o_ref.dtype)

    return pl.pallas_call(
        body,
        name=name,
        grid=(t // tm,),
        in_specs=[pl.BlockSpec((tm, c), lambda i: (i, 0)), ANY],
        out_specs=pl.BlockSpec((tm, c), lambda i: (i, col0 // c)),
        out_shape=jax.ShapeDtypeStruct(buf.shape, buf.dtype),
        input_output_aliases={1: 0},
        compiler_params=_params(("parallel",)),
    )(src, buf)


def _softplus(x):
    return jnp.maximum(x, 0.0) + jnp.log(1.0 + jnp.exp(-jnp.abs(x)))


def _chunk_decay(dtraw, bias, alog):
    q = dtraw.shape[0]
    dt = _softplus(dtraw + bias)
    a = -jnp.exp(alog)
    rows = lax.broadcasted_iota(jnp.int32, (q, q), 0)
    cols = lax.broadcasted_iota(jnp.int32, (q, q), 1)
    lower = rows >= cols
    acum = jnp.dot(lower.astype(F32), dt * a, precision=lax.Precision.HIGHEST, preferred_element_type=F32)
    return dt, a, acum, acum.T, lower


def _dot_exact(v, sel):
    hi = v.astype(BF16)
    r1 = v - hi.astype(F32)
    mid = r1.astype(BF16)
    lo = (r1 - mid.astype(F32)).astype(BF16)
    return (jnp.dot(hi, sel, preferred_element_type=F32) + jnp.dot(mid, sel, preferred_element_type=F32)
            + jnp.dot(lo, sel, preferred_element_type=F32))


def _head_selectors(gw, p):
    sum_heads = (lax.broadcasted_iota(jnp.int32, (gw, LANE), 0) // p == lax.broadcasted_iota(jnp.int32, (gw, LANE), 1))
    spread = (lax.broadcasted_iota(jnp.int32, (LANE, gw), 0) == lax.broadcasted_iota(jnp.int32, (LANE, gw), 1) // p)
    return sum_heads.astype(BF16), spread.astype(BF16)


def _row_spread(v, spread):
    return _dot_exact(jnp.broadcast_to(v, (8, v.shape[1])), spread)[0:1, :]


def _head_pad(v, r_heads):
    lead = v.shape[:-1]
    vg = v.reshape(lead + (N_SSD_GROUPS, r_heads))
    vg = jnp.pad(vg, [(0, 0)] * len(lead) + [(0, 0), (0, LANE - r_heads)])
    out = vg.reshape(lead + (N_SSD_GROUPS * LANE,))
    return out[None] if out.ndim == 1 else out


def _head_unpad(v, r_heads):
    lead = v.shape[:-1]
    out = v.reshape(lead + (N_SSD_GROUPS, LANE))[..., :r_heads].reshape(lead + (N_SSD_GROUPS * r_heads,))
    return out[0] if (len(lead) == 1 and lead[0] == 1) else out


def _ssd_w_in_layout(w_in, d_inner, d_xbc, r_heads):
    main = w_in[:, :d_inner + d_xbc]
    return jnp.concatenate([main, _head_pad(w_in[:, d_inner + d_xbc:], r_heads)], axis=1)


def _ssd_w_in_unlayout(w, d_inner, d_xbc, r_heads):
    main = w[:, :d_inner + d_xbc]
    return jnp.concatenate([main, _head_unpad(w[:, d_inner + d_xbc:], r_heads)], axis=1)


def _ssd_dims(proj, xbc):
    d_xbc = xbc.shape[1]
    d_inner = d_xbc - 2 * N_SSD_GROUPS * D_STATE
    gw = d_inner // N_SSD_GROUPS
    return d_inner, d_xbc, gw, gw // HEAD_DIM


def _ssd_fwd(proj, xbc, bias_p, alog_p, dskip_p, norm_w, n_seq, *, name):
    t = proj.shape[0]
    d_inner, d_xbc, gw, r_heads = _ssd_dims(proj, xbc)
    q, n, n_g, p = CHUNK, D_STATE, N_SSD_GROUPS, HEAD_DIM
    seq = t // n_seq
    nc = seq // q
    dt_blk0 = (d_inner + d_xbc) // LANE

    def body(x_ref, b_ref, c_ref, z_ref, dtr_ref, bias_ref, alog_ref, dsk_ref, nw_ref, yn_ref, y_ref, hs_ref, h_scr):
        @pl.when(pl.program_id(2) == 0)
        def _():
            h_scr[...] = jnp.zeros_like(h_scr)

        dt, a, acum, acum_t, lower = _chunk_decay(dtr_ref[...], bias_ref[...], alog_ref[...])
        x = x_ref[...]
        bb = b_ref[...].astype(BF16)
        cb = c_ref[...].astype(BF16)
        g_mat = lax.dot_general(cb, bb, (((1,), (1,)), ((), ())), preferred_element_type=F32)
        h_prev = h_scr[...]
        hs_ref[...] = h_prev
        c_h = jnp.dot(cb, h_prev.astype(BF16), preferred_element_type=F32)
        _, spread = _head_selectors(gw, p)
        acum_s = _dot_exact(acum, spread)
        a_last_s = acum_s[q - 1:q, :]
        xdt = x * _dot_exact(dt, spread)
        xdt_b = xdt.astype(BF16)
        ys = []
        for h in range(r_heads):
            decay = jnp.exp(jnp.where(lower, acum[:, h:h + 1] - acum_t[h:h + 1, :], -jnp.inf))
            ys.append(jnp.dot((g_mat * decay).astype(BF16), xdt_b[:, h * p:(h + 1) * p], preferred_element_type=F32))
        y = jnp.concatenate(ys, axis=1) + jnp.exp(acum_s) * c_h + _row_spread(dsk_ref[...], spread) * x
        xd = xdt * jnp.exp(a_last_s - acum_s)
        states = lax.dot_general(bb, xd.astype(BF16), (((0,), (0,)), ((), ())), preferred_element_type=F32)
        h_scr[...] = h_prev * jnp.exp(a_last_s) + states
        y_ref[...] = y
        gated = y * _silu(z_ref[...])
        rstd = lax.rsqrt(jnp.mean(gated * gated, axis=-1, keepdims=True) + EPS)
        yn_ref[...] = (gated * rstd * nw_ref[...]).astype(BF16)

    row = lambda b, g, c: b * nc + c
    vec = pl.BlockSpec((1, LANE), lambda b, g, c: (0, g))
    return pl.pallas_call(
        body,
        name=name,
        grid=(n_seq, n_g, nc),
        in_specs=[
            pl.BlockSpec((q, gw), lambda b, g, c: (row(b, g, c), g)),
            pl.BlockSpec((q, n), lambda b, g, c: (row(b, g, c), d_inner // n + g)),
            pl.BlockSpec((q, n), lambda b, g, c: (row(b, g, c), d_inner // n + n_g + g)),
            pl.BlockSpec((q, gw), lambda b, g, c: (row(b, g, c), g)),
            pl.BlockSpec((q, LANE), lambda b, g, c: (row(b, g, c), dt_blk0 + g)),
            vec, vec, vec,
            pl.BlockSpec((1, gw), lambda b, g, c: (0, g)),
        ],
        out_specs=[
            pl.BlockSpec((q, gw), lambda b, g, c: (row(b, g, c), g)),
            pl.BlockSpec((q, gw), lambda b, g, c: (row(b, g, c), g)),
            pl.BlockSpec((n, gw), lambda b, g, c: (row(b, g, c), g)),
        ],
        out_shape=[
            jax.ShapeDtypeStruct((t, d_inner), BF16),
            jax.ShapeDtypeStruct((t, d_inner), F32),
            jax.ShapeDtypeStruct((n_seq * nc * n, d_inner), F32),
        ],
        scratch_shapes=[pltpu.VMEM((n, gw), F32)],
        compiler_params=_params(("parallel", "parallel", "arbitrary")),
    )(xbc, xbc, xbc, proj, proj, bias_p, alog_p, dskip_p, norm_w)


def _ssd_bwd(proj, xbc, hs, y, dyn, bias_p, alog_p, dskip_p, norm_w, n_seq, *, name):
    t = proj.shape[0]
    d_inner, d_xbc, gw, r_heads = _ssd_dims(proj, xbc)
    q, n, n_g, p = CHUNK, D_STATE, N_SSD_GROUPS, HEAD_DIM
    seq = t // n_seq
    nc = seq // q
    dt_blk0 = (d_inner + d_xbc) // LANE

    def body(x_ref, b_ref, c_ref, z_ref, dtr_ref, bias_ref, alog_ref, dsk_ref, nw_ref, hs_ref, y_ref, dyn_ref,
             dx_ref, db_ref, dc_ref, dz_ref, ddtr_ref, dnw_ref, dbias_ref, dalog_ref, ddsk_ref, dh_scr):
        first = jnp.logical_and(pl.program_id(1) == 0, pl.program_id(2) == 0)

        @pl.when(pl.program_id(2) == 0)
        def _():
            dh_scr[...] = jnp.zeros_like(dh_scr)

        dtraw = dtr_ref[...]
        dt, a, acum, acum_t, lower = _chunk_decay(dtraw, bias_ref[...], alog_ref[...])
        x = x_ref[...]
        bb = b_ref[...].astype(BF16)
        cb = c_ref[...].astype(BF16)
        g_mat = lax.dot_general(cb, bb, (((1,), (1,)), ((), ())), preferred_element_type=F32)

        yv = y_ref[...]
        z = z_ref[...]
        sz = _silu(z)
        gated = yv * sz
        rstd = lax.rsqrt(jnp.mean(gated * gated, axis=-1, keepdims=True) + EPS)
        gn = gated * rstd
        dynv = dyn_ref[...]
        gwt = dynv * nw_ref[...]
        dgated = rstd * (gwt - gn * jnp.mean(gwt * gn, axis=-1, keepdims=True))
        dnw = jnp.sum(dynv * gn, axis=0, keepdims=True)
        dy = dgated * sz
        dz_ref[...] = (dgated * yv * _dsilu(z)).astype(BF16)

        h_prev = hs_ref[...]
        h_prev_b = h_prev.astype(BF16)
        ds = dh_scr[...]
        ds_b = ds.astype(BF16)
        sum_heads, spread = _head_selectors(gw, p)
        acum_s = _dot_exact(acum, spread)
        a_last_s = acum_s[q - 1:q, :]
        dt_s = _dot_exact(dt, spread)
        dsk_s = _row_spread(dsk_ref[...], spread)
        dte_s = jnp.exp(a_last_s - acum_s)
        cd_s = jnp.exp(a_last_s)
        xdt = x * dt_s
        xdt_b = xdt.astype(BF16)
        dy_b = dy.astype(BF16)
        gt_mat = lax.dot_general(bb, cb, (((1,), (1,)), ((), ())), preferred_element_type=F32)
        upper = lax.broadcasted_iota(jnp.int32, (q, q), 0) <= lax.broadcasted_iota(jnp.int32, (q, q), 1)
        dg = jnp.zeros((q, q), F32)
        dxdts, w_diffs = [], []
        for h in range(r_heads):
            hsl = slice(h * p, (h + 1) * p)
            diff = acum[:, h:h + 1] - acum_t[h:h + 1, :]
            decay = jnp.exp(jnp.where(lower, diff, -jnp.inf))
            decay_t = jnp.exp(jnp.where(upper, -diff, -jnp.inf))
            mt_mat = gt_mat * decay_t
            dm = lax.dot_general(dy_b[:, hsl], xdt_b[:, hsl], (((1,), (1,)), ((), ())), preferred_element_type=F32)
            dm_t = lax.dot_general(xdt_b[:, hsl], dy_b[:, hsl], (((1,), (1,)), ((), ())), preferred_element_type=F32)
            dg = dg + dm * decay
            dxdts.append(jnp.dot(mt_mat.astype(BF16), dy_b[:, hsl], preferred_element_type=F32))
            w_diffs.append(dm * (g_mat * decay) - dm_t * mt_mat)
        sel_q = (lax.broadcasted_iota(jnp.int32, (r_heads * q, LANE), 0) // q
                 == lax.broadcasted_iota(jnp.int32, (r_heads * q, LANE), 1)).astype(BF16)
        dacum_diag = _dot_exact(jnp.concatenate(w_diffs, axis=1), sel_q)
        c_h = jnp.dot(cb, h_prev_b, preferred_element_type=F32)
        dxd = jnp.dot(bb, ds_b, preferred_element_type=F32)
        dxdt = jnp.concatenate(dxdts, axis=1) + dxd * dte_s
        dye = dy * jnp.exp(acum_s)
        dye_b = dye.astype(BF16)
        xd = xdt * dte_s
        xd_b = xd.astype(BF16)
        dg_b = dg.astype(BF16)
        dx_ref[...] = dxdt * dt_s + dsk_s * dy
        dc_ref[...] = (jnp.dot(dg_b, bb, preferred_element_type=F32)
                       + lax.dot_general(dye_b, h_prev_b, (((1,), (1,)), ((), ())), preferred_element_type=F32))
        db_ref[...] = (lax.dot_general(dg_b, cb, (((0,), (0,)), ((), ())), preferred_element_type=F32)
                       + lax.dot_general(xd_b, ds_b, (((1,), (1,)), ((), ())), preferred_element_type=F32))
        dh_scr[...] = ds * cd_s + lax.dot_general(cb, dye_b, (((0,), (0,)), ((), ())), preferred_element_type=F32)
        ddt_cols = _dot_exact(x * dxdt, sum_heads)
        dacum_y = _dot_exact(dye * c_h - dxd * xd, sum_heads)
        col_sums = jnp.concatenate([
            jnp.sum(dxd * xd, axis=0, keepdims=True) + jnp.sum(ds * h_prev, axis=0, keepdims=True) * cd_s,
            jnp.sum(dy * x, axis=0, keepdims=True),
            jnp.zeros((6, gw), F32)], axis=0)
        col_sums = _dot_exact(col_sums, sum_heads)
        ddsk = col_sums[1:2, :]
        rows_q = lax.broadcasted_iota(jnp.int32, (q, LANE), 0)
        dacum = dacum_diag + dacum_y + jnp.where(rows_q == q - 1, col_sums[0:1, :], 0.0)
        dadt = jnp.dot(upper.astype(F32), dacum, precision=lax.Precision.HIGHEST, preferred_element_type=F32)
        ddt = dadt * a + ddt_cols
        ddtr = ddt * _sigmoid(dtraw + bias_ref[...])
        ddtr_ref[...] = ddtr
        dbias = jnp.sum(ddtr, axis=0, keepdims=True)
        dalog = jnp.sum(dadt * dt, axis=0, keepdims=True) * a

        @pl.when(first)
        def _():
            dnw_ref[...] = dnw
            dbias_ref[...] = dbias
            dalog_ref[...] = dalog
            ddsk_ref[...] = ddsk

        @pl.when(jnp.logical_not(first))
        def _():
            dnw_ref[...] += dnw
            dbias_ref[...] += dbias
            dalog_ref[...] += dalog
            ddsk_ref[...] += ddsk

    row = lambda g, b, c: b * nc + (nc - 1 - c)
    vec = pl.BlockSpec((1, LANE), lambda g, b, c: (0, g))
    wide = pl.BlockSpec((q, gw), lambda g, b, c: (row(g, b, c), g))
    narrow = pl.BlockSpec((q, n), lambda g, b, c: (row(g, b, c), g))
    return pl.pallas_call(
        body,
        name=name,
        grid=(n_g, n_seq, nc),
        in_specs=[
            wide,
            pl.BlockSpec((q, n), lambda g, b, c: (row(g, b, c), d_inner // n + g)),
            pl.BlockSpec((q, n), lambda g, b, c: (row(g, b, c), d_inner // n + n_g + g)),
            wide,
            pl.BlockSpec((q, LANE), lambda g, b, c: (row(g, b, c), dt_blk0 + g)),
            vec, vec, vec,
            pl.BlockSpec((1, gw), lambda g, b, c: (0, g)),
            pl.BlockSpec((n, gw), lambda g, b, c: (row(g, b, c), g)),
            wide, wide,
        ],
        out_specs=[
            wide, narrow, narrow, wide, narrow,
            pl.BlockSpec((1, gw), lambda g, b, c: (0, g)),
            vec, vec, vec,
        ],
        out_shape=[
            jax.ShapeDtypeStruct((t, d_inner), F32),
            jax.ShapeDtypeStruct((t, n_g * n), F32),
            jax.ShapeDtypeStruct((t, n_g * n), F32),
            jax.ShapeDtypeStruct(proj.shape, BF16),
            jax.ShapeDtypeStruct((t, n_g * LANE), F32),
            jax.ShapeDtypeStruct((1, d_inner), F32),
            jax.ShapeDtypeStruct((1, n_g * LANE), F32),
            jax.ShapeDtypeStruct((1, n_g * LANE), F32),
            jax.ShapeDtypeStruct((1, n_g * LANE), F32),
        ],
        scratch_shapes=[pltpu.VMEM((n, gw), F32)],
        compiler_params=_params(("parallel", "arbitrary", "arbitrary")),
    )(xbc, xbc, xbc, proj, proj, bias_p, alog_p, dskip_p, norm_w, hs, y, dyn)


MESH_IDS = pl.DeviceIdType.MESH


def _my_index():
    return 4 * lax.axis_index("x") + 2 * lax.axis_index("y") + lax.axis_index("c")


def _all_gather(shard, *, name):
    def body(x_ref, out_ref, send_sems, recv_sems, local_sem):
        x, y, c = lax.axis_index("x"), lax.axis_index("y"), lax.axis_index("c")
        me, sibling = (x, y, c), (x, y, 1 - c)
        chips = [(1 - x, y), (x, 1 - y), (1 - x, 1 - y)]

        def blk(px, py, pc):
            return out_ref.at[4 * px + 2 * py + pc]

        def copy(k, block, to, src=None):
            return pltpu.make_async_remote_copy(
                src_ref=blk(*block) if src is None else src, dst_ref=blk(*block),
                send_sem=send_sems.at[k], recv_sem=recv_sems.at[k], device_id=to, device_id_type=MESH_IDS)

        mine = pltpu.make_async_copy(x_ref, blk(*me), local_sem)
        mine.start()
        first = [copy(0, me, sibling, src=x_ref)]
        first += [copy(1 + j, me, (*chip, c), src=x_ref) for j, chip in enumerate(chips)]
        for cp in first:
            cp.start()
        passed = [copy(4 + j, (*chip, c), sibling) for j, chip in enumerate(chips)]
        for j, chip in enumerate(chips):
            copy(1 + j, (*chip, c), me).wait_recv()
            passed[j].start()
        copy(0, sibling, me).wait_recv()
        for j, chip in enumerate(chips):
            copy(4 + j, (*chip, 1 - c), me).wait_recv()
        for cp in first + passed:
            cp.wait_send()
        mine.wait()

    return pl.pallas_call(
        body,
        name=name,
        in_specs=[ANY],
        out_specs=ANY,
        out_shape=jax.ShapeDtypeStruct((N_DEV,) + shard.shape, shard.dtype),
        scratch_shapes=[pltpu.SemaphoreType.DMA((7,)), pltpu.SemaphoreType.DMA((7,)), pltpu.SemaphoreType.DMA],
    )(shard)


HBM_SPEC = pl.BlockSpec(memory_space=pltpu.HBM)
SEM_SPEC = pl.BlockSpec(memory_space=pltpu.SEMAPHORE)
SPLIT_COPY_PARAMS = pltpu.CompilerParams(has_side_effects=pltpu.SideEffectType.DATAFLOW_SIDE_EFFECTING)


def _peer_list():
    x, y, c = lax.axis_index("x"), lax.axis_index("y"), lax.axis_index("c")
    peers = []
    for k in range(1, N_DEV):
        px = 1 - x if k & 4 else x
        py = 1 - y if k & 2 else y
        pc = 1 - c if k & 1 else c
        peers.append(((px, py, pc), 4 * px + 2 * py + pc))
    return 4 * x + 2 * y + c, peers


def _push_copies(src_refs, land_refs, send_sems, recv_sems, blockwise):
    me, peers = _peer_list()
    copies = []
    for a, (src_ref, land_ref) in enumerate(zip(src_refs, land_refs)):
        for k, (dev, idx) in enumerate(peers):
            sem = a * (N_DEV - 1) + k
            src = src_ref.at[idx] if blockwise else src_ref
            copies.append(tuple(
                pltpu.make_async_remote_copy(src_ref=src, dst_ref=land_ref.at[slot], send_sem=send_sems.at[sem],
                                             recv_sem=recv_sems.at[sem], device_id=dev, device_id_type=MESH_IDS)
                for slot in (me, idx)))
    return copies


def _push_start(srcs, blockwise, after, *, name):
    n = len(srcs)
    blocks = [s_.shape[1:] if blockwise else s_.shape for s_ in srcs]

    def body(*refs):
        src_refs, land_refs = refs[:n], refs[n:2 * n]
        send_sems, recv_sems = refs[2 * n + 1], refs[2 * n + 2]
        token = refs[-1]
        for send, _ in _push_copies(src_refs, land_refs, send_sems, recv_sems, blockwise):
            send.start()
        token[...] = jnp.zeros_like(token)

    n_sem = n * (N_DEV - 1)
    lands = [lax.empty((N_DEV,) + b, s_.dtype) for b, s_ in zip(blocks, srcs)]
    out = pl.pallas_call(
        body,
        name=name,
        in_specs=[HBM_SPEC] * (2 * n) + [ANY],
        out_specs=(SEM_SPEC, SEM_SPEC) + (HBM_SPEC,) * (2 * n) + (pl.BlockSpec(memory_space=pltpu.VMEM),),
        out_shape=(pltpu.SemaphoreType.DMA((n_sem,)), pltpu.SemaphoreType.DMA((n_sem,)))
        + tuple(pltpu.HBM(a.shape, a.dtype) for a in list(srcs) + lands)
        + (jax.ShapeDtypeStruct((8, LANE), F32),),
        input_output_aliases={i: 2 + i for i in range(2 * n)},
        compiler_params=SPLIT_COPY_PARAMS,
    )(*[pltpu.with_memory_space_constraint(a, pltpu.HBM) for a in list(srcs) + lands], after)
    return out[0], out[1], out[2:2 + n], out[2 + n:2 + 2 * n], out[-1]


def _push_wait(send_sems, recv_sems, srcs, lands, blockwise, after, *, name):
    n = len(srcs)

    def body(*refs):
        src_refs, land_refs = refs[:n], refs[n:2 * n]
        send_sems, recv_sems = refs[2 * n], refs[2 * n + 1]
        for send, recv in _push_copies(src_refs, land_refs, send_sems, recv_sems, blockwise):
            send.wait_send()
            recv.wait_recv()

    out = pl.pallas_call(
        body,
        name=name,
        in_specs=[HBM_SPEC] * (2 * n) + [SEM_SPEC, SEM_SPEC, ANY],
        out_specs=(HBM_SPEC,) * (2 * n),
        out_shape=tuple(pltpu.HBM(a.shape, a.dtype) for a in list(srcs) + list(lands)),
        input_output_aliases={i: i for i in range(2 * n)},
        compiler_params=SPLIT_COPY_PARAMS,
    )(*srcs, *lands, send_sems, recv_sems, after)
    return out[n:]


def _with_own_slot(landing, own):
    slot = lax.broadcasted_iota(jnp.int32, (N_DEV,) + (1,) * own.ndim, 0)
    return jnp.where(slot == _my_index(), own[None], landing)


def _sum_slots(parts, own=None, *, name):
    shape = parts.shape[1:]
    n, c = parts.shape[0], parts.shape[-1]
    r = parts.size // (n * c)
    tm = _pick(r, (256, 128, 64, 32, 16, 8))

    def body(p_ref, *rest):
        o_ref = rest[-1]
        me = _my_index()

        def slot(s):
            if own is None:
                return p_ref[s].astype(F32)
            return jnp.where(me == s, rest[0][...], p_ref[s]).astype(F32)

        acc = slot(0)
        for s in range(1, n):
            acc = acc + slot(s)
        o_ref[...] = acc

    tile = pl.BlockSpec((tm, c), lambda i: (i, 0))
    return pl.pallas_call(
        body,
        name=name,
        grid=(r // tm,),
        in_specs=[pl.BlockSpec((n, tm, c), lambda i: (0, i, 0))] + ([] if own is None else [tile]),
        out_specs=tile,
        out_shape=jax.ShapeDtypeStruct((r, c), F32),
        compiler_params=_params(("parallel",)),
    )(parts.reshape(n, r, c), *([] if own is None else [own.reshape(r, c)])).reshape(shape)


def _row_count(shape):
    c = shape[-1]
    rows = 1
    for s in shape[:-1]:
        rows *= s
    return rows, c, c + (-c) % LANE


PACK_ROWS = 256


def _pack_rows(arrays):
    pieces = []
    for a in arrays:
        rows, c, cp = _row_count(a.shape)
        a2 = a.reshape(rows, c)
        if cp > c:
            a2 = jnp.pad(a2, ((0, 0), (0, cp - c)))
        a2 = a2.reshape(rows * cp // LANE, LANE)
        if a2.shape[0] % 8:
            a2 = jnp.pad(a2, ((0, 8 - a2.shape[0] % 8), (0, 0)))
        pieces.append(a2)
    total = sum(p.shape[0] for p in pieces)
    if total % PACK_ROWS:
        pieces.append(jnp.zeros((PACK_ROWS - total % PACK_ROWS, LANE), F32))
    return jnp.concatenate(pieces, axis=0)


def _unpack_rows(packed, shapes, lead=()):
    out, off = [], 0
    for shp in shapes:
        rows, c, cp = _row_count(shp)
        n_rows = rows * cp // LANE
        seg = packed[..., off:off + n_rows, :].reshape(lead + (rows, cp))
        out.append(seg[..., :c].reshape(lead + tuple(shp)))
        off += n_rows + (-n_rows) % 8
    return out


def _unshard(stacked, axis):
    moved = jnp.moveaxis(stacked, 0, axis)
    shp = moved.shape
    return moved.reshape(shp[:axis] + (shp[axis] * shp[axis + 1],) + shp[axis + 2:])


def _shard_major(full, axis):
    shp = full.shape
    split = full.reshape(shp[:axis] + (N_DEV, shp[axis] // N_DEV) + shp[axis + 1:])
    return jnp.moveaxis(split, axis, 0)


def _my_shard(full, axis):
    size = full.shape[axis] // N_DEV
    return lax.dynamic_slice_in_dim(full, _my_index() * size, size, axis)


def _local_step(x, target, w, fetch, emit, n_seq):
    def with_token(vec, token):
        return vec + jnp.tile(token[0:1, :], (1, vec.shape[1] // LANE))

    depth, d_model = w["norm_mix_pre"].shape
    d_inner = w["ssd_norm_w"].shape[1]
    d_xbc = w["ssd_conv_w"].shape[2]
    saved = []
    for i in range(depth):
        j = i // 2
        m, token = fetch(i, "mix", x)
        mix_pre_w = with_token(w["norm_mix_pre"][i:i + 1], token)
        s = {"x": x, "mix_pre_w": mix_pre_w}
        if i % 2 == 0:
            u = _rms_fwd(x, mix_pre_w, out_dtype=BF16, name=f"l{i}_mix_pre")
            proj = _mm(u, m["ssd_w_in"], name=f"l{i}_ssd_in")
            xbc, xbc_pre = _ssd_conv_fwd(proj, d_inner, d_xbc, w["ssd_conv_w"][j], w["ssd_conv_b"][j:j + 1], n_seq,
                                         name=f"l{i}_ssd_conv")
            yn, y, hs = _ssd_fwd(proj, xbc, w["ssd_dt_bias"][j:j + 1], w["ssd_a_log"][j:j + 1], w["ssd_d"][j:j + 1],
                                 w["ssd_norm_w"][j:j + 1], n_seq, name=f"l{i}_ssd_scan")
            m_out, token = fetch(i, "out", yn)
            m = {**m, **m_out}
            mix = _mm(yn, m["ssd_w_out"], name=f"l{i}_ssd_out")
            s.update(u=u, proj=proj, xbc=xbc, xbc_pre=xbc_pre, yn=yn, y=y, hs=hs)
        else:
            u = _rms_fwd(x, mix_pre_w, out_dtype=F32, name=f"l{i}_mix_pre")
            mix = _pool_fwd(u, m["pool_w"], w["pool_scale"][j:j + 1], n_seq, name=f"l{i}_pool")
            s.update(u=u)
            token = jnp.zeros((8, LANE), F32)
        x1 = _res_rms_fwd(x, mix, with_token(w["norm_mix_post"][i:i + 1], token), name=f"l{i}_mix_post")
        m_ffn, token = fetch(i, "ffn", x1)
        m = {**m, **m_ffn}
        ffn_pre_w = with_token(w["norm_ffn_pre"][i:i + 1], token)
        n = _rms_fwd(x1, ffn_pre_w, out_dtype=BF16, name=f"l{i}_ffn_pre")
        h, hc, a = _ffn_up_act(n, m["ffn_w_up"], w["ffn_conv_w"][i], w["ffn_conv_b"][i:i + 1], n_seq,
                               name=f"l{i}_ffn_up_act")
        f = _mm(a, m["ffn_w_down"], name=f"l{i}_ffn_down")
        x = _res_rms_fwd(x1, f, w["norm_ffn_post"][i:i + 1], name=f"l{i}_ffn_post")
        s.update(mix=mix, x1=x1, n=n, h=h, hc=hc, a=a, f=f, m=m, ffn_pre_w=ffn_pre_w)
        saved.append(s)

    loss, dx = _loss_head(x, target)
    grads = {k: [None] * len(w[k]) for k in SMALL}
    token = jnp.zeros((8, LANE), F32)
    for i in reversed(range(depth)):
        j = i // 2
        s = saved[i]
        m, gm = s["m"], {}
        df, grads["norm_ffn_post"][i] = _rms_bwd(s["f"], with_token(w["norm_ffn_post"][i:i + 1], token), dx, None,
                                                 out_dtype=BF16, name=f"l{i}_ffn_post_b")
        gm["ffn_w_down"] = _mm(s["a"], df, ta=True, name=f"l{i}_ffn_down_bw")
        dh, grads["ffn_conv_w"][i], grads["ffn_conv_b"][i] = _ffn_down_bx_act_bwd(
            df, m["ffn_w_down"], s["h"], s["hc"], w["ffn_conv_w"][i], n_seq, name=f"l{i}_ffn_act_b")
        dn = _mm(dh, m["ffn_w_up"], tb=True, name=f"l{i}_ffn_up_bx")
        gm["ffn_w_up"] = _mm(s["n"], dh, ta=True, name=f"l{i}_ffn_up_bw")
        dx1, grads["norm_ffn_pre"][i] = _rms_bwd(s["x1"], s["ffn_pre_w"], dn, dx, name=f"l{i}_ffn_pre_b")
        token = emit(i, "ffn", gm, dx1)
        gm = {}
        dmix, grads["norm_mix_post"][i] = _rms_bwd(s["mix"], with_token(w["norm_mix_post"][i:i + 1], token), dx1, None,
                                                   out_dtype=BF16 if i % 2 == 0 else F32, name=f"l{i}_mix_post_b")
        if i % 2 == 0:
            dyn = _mm(dmix, m["ssd_w_out"], tb=True, name=f"l{i}_ssd_out_bx")
            gm["ssd_w_out"] = _mm(s["yn"], dmix, ta=True, name=f"l{i}_ssd_out_bw")
            token = emit(i, "out", gm, dyn)
            gm = {}
            dxs, db, dc, dz, ddtr, dnw, dbias, dalog, ddsk = _ssd_bwd(
                s["proj"], s["xbc"], s["hs"], s["y"], dyn, w["ssd_dt_bias"][j:j + 1], w["ssd_a_log"][j:j + 1],
                w["ssd_d"][j:j + 1], with_token(w["ssd_norm_w"][j:j + 1], token), n_seq, name=f"l{i}_ssd_scan_b")
            grads["ssd_norm_w"][j], grads["ssd_dt_bias"][j], grads["ssd_a_log"][j], grads["ssd_d"][j] = (
                dnw, dbias, dalog, ddsk)
            dproj, grads["ssd_conv_w"][j], grads["ssd_conv_b"][j] = _ssd_conv_bwd(
                s["proj"], d_inner, w["ssd_conv_w"][j], s["xbc_pre"], (dxs, db, dc), dz, n_seq,
                name=f"l{i}_ssd_conv_b")
            dproj = _fill_cols(dproj, ddtr, d_inner + d_xbc, name=f"l{i}_ssd_dt_b")
            du = _mm(dproj, m["ssd_w_in"], tb=True, name=f"l{i}_ssd_in_bx")
            gm["ssd_w_in"] = _mm(s["u"], dproj, ta=True, name=f"l{i}_ssd_in_bw")
        else:
            du, gm["pool_w"], grads["pool_scale"][j] = _pool_bwd(
                s["u"], m["pool_w"], w["pool_scale"][j:j + 1], dmix, n_seq, name=f"l{i}_pool_b")
        dx, grads["norm_mix_pre"][i] = _rms_bwd(s["x"], s["mix_pre_w"], du, dx1, name=f"l{i}_mix_pre_b")
        token = emit(i, "mix", gm, dx)
    return loss, dx, grads


BIG = (("ssd_w_in", 2), ("ssd_w_out", 1), ("pool_w", 2), ("ffn_w_up", 2), ("ffn_w_down", 1))
SMALL_SHARDED = (("ssd_conv_w", 2), ("ffn_conv_w", 2), ("pool_scale", 1))
SMALL = ("ssd_conv_w", "ssd_conv_b", "ssd_dt_bias", "ssd_a_log", "ssd_d", "ssd_norm_w", "pool_scale", "ffn_conv_w",
         "ffn_conv_b", "norm_mix_pre", "norm_mix_post", "norm_ffn_pre", "norm_ffn_post")
WEIGHTS = ("ssd_w_in", "ssd_conv_w", "ssd_conv_b", "ssd_dt_bias", "ssd_a_log", "ssd_d", "ssd_norm_w", "ssd_w_out",
           "pool_w", "pool_scale", "ffn_w_up", "ffn_conv_w", "ffn_conv_b", "ffn_w_down", "norm_mix_pre",
           "norm_mix_post", "norm_ffn_pre", "norm_ffn_post")


def _ssd_sizes(d_inner):
    return d_inner + 2 * N_SSD_GROUPS * D_STATE, d_inner // HEAD_DIM // N_SSD_GROUPS


def _small_compute_layout(full, d_inner):
    _, r_heads = _ssd_sizes(d_inner)
    w = {k: full[k] for k in SMALL}
    for k in ("ssd_dt_bias", "ssd_a_log", "ssd_d"):
        w[k] = _head_pad(full[k], r_heads)
    for k in ("ffn_conv_w", "ffn_conv_b"):
        w[k] = _interleave(full[k])
    return w


def _matmul_compute_layout(k, full, d_inner):
    d_xbc, r_heads = _ssd_sizes(d_inner)
    if k == "ssd_w_in":
        return _ssd_w_in_layout(full, d_inner, d_xbc, r_heads)
    if k == "ffn_w_up":
        return _interleave(full)
    return full


def _layer_matrices(i, part):
    if part == "ffn":
        return (("ffn_w_up", 1, i), ("ffn_w_down", 0, i))
    if i % 2 == 1:
        return (("pool_w", 1, i // 2),) if part == "mix" else ()
    return (("ssd_w_in", 1, i // 2),) if part == "mix" else (("ssd_w_out", 0, i // 2),)


def _fetch_group(i, part):
    mix, out, ffn = (_layer_matrices(i, p) for p in ("mix", "out", "ffn"))
    if i % 2 == 1:
        return mix + ffn if part == "mix" else ()
    if i == 0:
        return {"mix": mix, "out": out + ffn, "ffn": ()}[part]
    return {"mix": mix + out, "out": (), "ffn": ffn}[part]


def _matmul_grad_reference_layout(k, g, d_inner):
    d_xbc, r_heads = _ssd_sizes(d_inner)
    if k == "ssd_w_in":
        return _ssd_w_in_unlayout(g, d_inner, d_xbc, r_heads)
    if k == "ffn_w_up":
        return _deinterleave(g)
    return g


def _small_grads_reference_layout(grads, shapes, d_inner):
    _, r_heads = _ssd_sizes(d_inner)
    g = {k: jnp.stack(grads[k]) for k in SMALL}
    for k in ("ssd_dt_bias", "ssd_a_log", "ssd_d"):
        g[k] = _head_unpad(g[k][:, 0], r_heads)
    for k in ("ffn_conv_w", "ffn_conv_b"):
        g[k] = _deinterleave(g[k])
    return {k: v.reshape(shapes[k]) for k, v in g.items()}


def kernel(x, ssd_w_in, ssd_conv_w, ssd_conv_b, ssd_dt_bias, ssd_a_log, ssd_d, ssd_norm_w, ssd_w_out, pool_w, pool_scale, ffn_w_up, ffn_conv_w, ffn_conv_b, ffn_w_down, norm_mix_pre, norm_mix_post, norm_ffn_pre, norm_ffn_post, loss_target, m_ssd_w_in, m_ssd_conv_w, m_ssd_conv_b, m_ssd_dt_bias, m_ssd_a_log, m_ssd_d, m_ssd_norm_w, m_ssd_w_out, m_pool_w, m_pool_scale, m_ffn_w_up, m_ffn_conv_w, m_ffn_conv_b, m_ffn_w_down, m_norm_mix_pre, m_norm_mix_post, m_norm_ffn_pre, m_norm_ffn_post, v_ssd_w_in, v_ssd_conv_w, v_ssd_conv_b, v_ssd_dt_bias, v_ssd_a_log, v_ssd_d, v_ssd_norm_w, v_ssd_w_out, v_pool_w, v_pool_scale, v_ffn_w_up, v_ffn_conv_w, v_ffn_conv_b, v_ffn_w_down, v_norm_mix_pre, v_norm_mix_post, v_norm_ffn_pre, v_norm_ffn_post):
    shards = dict(ssd_w_in=ssd_w_in, ssd_conv_w=ssd_conv_w, ssd_conv_b=ssd_conv_b, ssd_dt_bias=ssd_dt_bias,
                  ssd_a_log=ssd_a_log, ssd_d=ssd_d, ssd_norm_w=ssd_norm_w, ssd_w_out=ssd_w_out, pool_w=pool_w,
                  pool_scale=pool_scale, ffn_w_up=ffn_w_up, ffn_conv_w=ffn_conv_w, ffn_conv_b=ffn_conv_b,
                  ffn_w_down=ffn_w_down, norm_mix_pre=norm_mix_pre, norm_mix_post=norm_mix_post,
                  norm_ffn_pre=norm_ffn_pre, norm_ffn_post=norm_ffn_post)
    moments_m = dict(zip(WEIGHTS, (m_ssd_w_in, m_ssd_conv_w, m_ssd_conv_b, m_ssd_dt_bias, m_ssd_a_log, m_ssd_d, m_ssd_norm_w, m_ssd_w_out, m_pool_w, m_pool_scale, m_ffn_w_up, m_ffn_conv_w, m_ffn_conv_b, m_ffn_w_down, m_norm_mix_pre, m_norm_mix_post, m_norm_ffn_pre, m_norm_ffn_post)))
    moments_v = dict(zip(WEIGHTS, (v_ssd_w_in, v_ssd_conv_w, v_ssd_conv_b, v_ssd_dt_bias, v_ssd_a_log, v_ssd_d, v_ssd_norm_w, v_ssd_w_out, v_pool_w, v_pool_scale, v_ffn_w_up, v_ffn_conv_w, v_ffn_conv_b, v_ffn_w_down, v_norm_mix_pre, v_norm_mix_post, v_norm_ffn_pre, v_norm_ffn_post)))
    n_seq, seq, d_model = x.shape
    t = n_seq * seq

    d_inner = ssd_norm_w.shape[1]
    depth = norm_mix_pre.shape[0]
    x2 = x.reshape(t, d_model)

    shard16 = {k: shards[k].astype(BF16) for k, _ in BIG}
    order = [(i, part) for i in range(depth) for part in ("mix", "out", "ffn") if _fetch_group(i, part)]
    fetches = {}

    def start_fetch(key, after):
        srcs = [shard16[k][l] for k, _, l in _fetch_group(*key)]
        fetches[key] = _push_start(srcs, False, after, name=f"fetch{key[0]}{key[1]}_start")

    full = dict(shards)
    small_all = _all_gather(_pack_rows([shards[k] for k, _ in SMALL_SHARDED]), name="gather_small_weights")
    small_stacked = _unpack_rows(small_all, [shards[k].shape for k, _ in SMALL_SHARDED], lead=(N_DEV,))
    for (k, axis), st in zip(SMALL_SHARDED, small_stacked):
        full[k] = _unshard(st, axis)
    w = _small_compute_layout(full, d_inner)
    ready = {}

    def fetch(i, part, x_now):
        key = (i, part)
        token = jnp.zeros((8, LANE), F32)
        if key in order:
            if key == order[0]:
                wholes = [_unshard(_all_gather(shard16[k][l], name=f"fetch0_{k}"), axis) for k, axis, l in _fetch_group(i, part)]
                nxt_after = wholes[0]
            else:
                send, recv, srcs, lands, _ = fetches[key]
                lands = _push_wait(send, recv, srcs, lands, False, x_now, name=f"fetch{i}{part}_wait")
                wholes = [_unshard(_with_own_slot(land, shard16[k][l]), axis)
                          for (k, axis, l), land in zip(_fetch_group(i, part), lands)]
                nxt_after = lands[0]
            for (k, _, l), whole in zip(_fetch_group(i, part), wholes):
                ready[k, l] = _matmul_compute_layout(k, whole, d_inner)
            nxt = order.index(key) + 1
            if nxt < len(order):
                start_fetch(order[nxt], nxt_after)
                token = fetches[order[nxt]][4]
        return {k: ready[k, l] for k, _, l in _layer_matrices(i, part)}, token

    g_layers = {}
    in_flight = []

    def finish_exchange(after):
        key, blocks, (send, recv, srcs, lands, _) = in_flight.pop()
        lands = _push_wait(send, recv, srcs, lands, True, after, name=f"exchange{key[0]}{key[1]}_wait")
        for (k, _, l), land, block in zip(_layer_matrices(*key), lands, blocks):
            own = lax.dynamic_index_in_dim(block, _my_index(), 0, keepdims=False)
            g_layers[k, l] = _sum_slots(land, own, name=f"sum{key[0]}_{k}")

    def start_exchange(key, gm, after):
        if in_flight:
            finish_exchange(after)
        blocks = [_shard_major(_matmul_grad_reference_layout(k, gm[k].astype(BF16), d_inner), axis)
                  for k, axis, _ in _layer_matrices(*key)]
        started = _push_start(blocks, True, after, name=f"exchange{key[0]}{key[1]}_start")
        in_flight.append((key, blocks, started))
        return started[4]

    deferred = []

    def emit(i, part, gm, dx_now):
        if (i, part) == (0, "mix"):
            deferred.append(gm)
            return jnp.zeros((8, LANE), F32)
        return start_exchange((i, part), gm, dx_now)

    loss, dx, grads = _local_step(x2, loss_target.reshape(t, d_model), w, fetch, emit, n_seq)
    loss = lax.psum(loss, ("x", "y", "c"))

    g_shard = {}
    small_shapes = {k: full[k].shape for k in SMALL}
    g_small = _small_grads_reference_layout(grads, small_shapes, d_inner)
    s_all = _all_gather(_pack_rows([g_small[k] for k in SMALL]), name="gather_small_grads")
    s_all = s_all + start_exchange((0, "mix"), deferred.pop(), s_all)[0:1, :]
    for k, g in zip(SMALL, _unpack_rows(_sum_slots(s_all, name="sum_small_grads"), [small_shapes[k] for k in SMALL])):
        g_shard[k] = g
    for k, axis in SMALL_SHARDED:
        g_shard[k] = _my_shard(g_shard[k], axis)

    last = [k for k, _, _ in _layer_matrices(*in_flight[0][0])]
    deltas, new_m, new_v = {}, {}, {}
    for k in [k for k in WEIGHTS if k not in last] + last:
        if k == last[0]:
            finish_exchange(deltas["ffn_w_up"])
        if k in dict(BIG):
            g_shard[k] = jnp.stack([g_layers[k, l] for l in range(shards[k].shape[0])])
        deltas[k], new_m[k], new_v[k] = _adamw(shards[k], g_shard[k], moments_m[k], moments_v[k], name=f"adamw_{k}")
    return (loss, dx.reshape(x.shape), *[g_shard[k] for k in WEIGHTS], *[deltas[k] for k in WEIGHTS],
            *[new_m[k] for k in WEIGHTS], *[new_v[k] for k in WEIGHTS])
```

```python
import functools

import jax
import jax.numpy as jnp
from jax import lax
from jax.experimental import pallas as pl
from jax.experimental.pallas import tpu as pltpu

F32 = jnp.float32
BF16 = jnp.bfloat16

N_DEV = 8
HEAD_DIM = 64
N_SSD_GROUPS = 4
D_STATE = 128
CHUNK = 128
POOL_WINDOWS = (2, 4, 8, 16)
EPS = 1e-6
LANE = 128
ADAM_LR = 0.001
ADAM_B1 = 0.9
ADAM_B2 = 0.999
ADAM_EPS = 1e-08
ADAM_WD = 0.01
ADAM_STEP = 10
VMEM_LIMIT = 56 * 1024 * 1024
ANY = pl.BlockSpec(memory_space=pl.ANY)


def _pick(n, cands):
    for c in cands:
        if n % c == 0:
            return c
    return n


def _params(sem):
    return pltpu.CompilerParams(dimension_semantics=sem, vmem_limit_bytes=VMEM_LIMIT)


def _sigmoid(x):
    return 0.5 * jnp.tanh(0.5 * x) + 0.5


def _silu(x):
    return x * _sigmoid(x)


def _dsilu(x):
    s = _sigmoid(x)
    return s * (1.0 + x * (1.0 - s))


def _shift_down(x, s):
    rows = lax.broadcasted_iota(jnp.int32, x.shape, 0)
    return jnp.where(rows >= s, pltpu.roll(x, s, 0), 0.0)


def _shift_up(x, s):
    n = x.shape[0]
    rows = lax.broadcasted_iota(jnp.int32, x.shape, 0)
    return jnp.where(rows < n - s, pltpu.roll(x, n - s, 0), 0.0)


MM_VMEM_BUDGET = 40 * 1024 * 1024
MM_STEP_BYTES = 1_300_000
MM_SUB = 512


def _mm_tiles(m, n, k, a_bytes, b_bytes, o_bytes):
    def cands(dim, sizes):
        out = [s for s in sizes if s <= dim and dim % s == 0]
        return out or [dim]

    best = None
    for tm in cands(m, (m, m // 2, 2048, 1024, 512, 256, 128)):
        if tm % LANE:
            continue
        for tn in cands(n, (n, n // 2, n // 4, 2048, 1024, 512, 256, 128)):
            if tn % (2 * LANE) and tn != n:
                continue
            for tk in cands(k, (k, k // 2, 2048, 1024, 512)):
                if tk % LANE:
                    continue
                nk = k // tk
                acc = tm * tn * 4 if (nk > 1 and o_bytes != 4) else 0
                temps = tm * min(tn, MM_SUB) * 4 + (tm * tk * 2 if a_bytes == 4 else 0) + (tk * tn * 2 if b_bytes == 4 else 0)
                vmem = 2 * (tm * tk * a_bytes + tk * tn * b_bytes + tm * tn * o_bytes) + acc + temps
                if vmem > MM_VMEM_BUDGET:
                    continue
                steps = (m // tm) * (n // tn) * nk
                cost = (m * k * a_bytes * (n // tn) + k * n * b_bytes * (m // tm) + m * n * o_bytes
                        + steps * MM_STEP_BYTES)
                if best is None or cost < best[0]:
                    best = (cost, tm, tn, tk)
    return best[1:]


def _mm(a, b, *, ta=False, tb=False, out_dtype=F32, name="mm"):
    m, k = (a.shape[1], a.shape[0]) if ta else a.shape
    n = b.shape[0] if tb else b.shape[1]
    o_bytes = jnp.dtype(out_dtype).itemsize
    tm, tn, tk = _mm_tiles(m, n, k, a.dtype.itemsize, b.dtype.itemsize, o_bytes)
    nk = k // tk
    sub = _pick(tn, (MM_SUB, 256))
    use_acc = nk > 1 and o_bytes != 4
    a_spec = pl.BlockSpec((tk, tm), lambda i, j, kk: (kk, i)) if ta else pl.BlockSpec((tm, tk), lambda i, j, kk: (i, kk))
    b_spec = pl.BlockSpec((tn, tk), lambda i, j, kk: (j, kk)) if tb else pl.BlockSpec((tk, tn), lambda i, j, kk: (kk, j))
    dims = (((1,), (1 if tb else 0,)), ((), ()))

    def body(a_ref, b_ref, o_ref, *scratch):
        kk = pl.program_id(2)
        acc_ref = scratch[0] if use_acc else o_ref
        if nk > 1:
            @pl.when(kk == 0)
            def _():
                acc_ref[...] = jnp.zeros_like(acc_ref)

        av = a_ref[...].astype(BF16)
        if ta:
            av = av.T
        for s in range(tn // sub):
            cols = slice(s * sub, (s + 1) * sub)
            bv = (b_ref[cols, :] if tb else b_ref[:, cols]).astype(BF16)
            part = lax.dot_general(av, bv, dims, preferred_element_type=F32)
            if nk == 1:
                o_ref[:, cols] = part.astype(out_dtype)
            else:
                acc_ref[:, cols] += part
        if use_acc:
            @pl.when(kk == nk - 1)
            def _():
                o_ref[...] = acc_ref[...].astype(out_dtype)

    return pl.pallas_call(
        body,
        name=name,
        grid=(m // tm, n // tn, nk),
        in_specs=[a_spec, b_spec],
        out_specs=pl.BlockSpec((tm, tn), lambda i, j, kk: (i, j)),
        out_shape=jax.ShapeDtypeStruct((m, n), out_dtype),
        scratch_shapes=[pltpu.VMEM((tm, tn), F32)] if use_acc else [],
        compiler_params=_params(("parallel", "parallel", "arbitrary")),
    )(a, b)


def _rms_fwd(x, w, *, out_dtype, name):
    t, d = x.shape
    tm = _pick(t, (512, 256, 128))

    def body(x_ref, w_ref, o_ref):
        xv = x_ref[...]
        rstd = lax.rsqrt(jnp.mean(xv * xv, axis=-1, keepdims=True) + EPS)
        o_ref[...] = (xv * rstd * w_ref[...]).astype(out_dtype)

    return pl.pallas_call(
        body,
        name=name,
        grid=(t // tm,),
        in_specs=[pl.BlockSpec((tm, d), lambda i: (i, 0)), pl.BlockSpec((1, d), lambda i: (0, 0))],
        out_specs=pl.BlockSpec((tm, d), lambda i: (i, 0)),
        out_shape=jax.ShapeDtypeStruct((t, d), out_dtype),
        compiler_params=_params(("parallel",)),
    )(x, w)


def _res_rms_fwd(x, f, w, *, name):
    t, d = x.shape
    tm = _pick(t, (512, 256, 128))

    def body(x_ref, f_ref, w_ref, o_ref):
        fv = f_ref[...]
        rstd = lax.rsqrt(jnp.mean(fv * fv, axis=-1, keepdims=True) + EPS)
        o_ref[...] = x_ref[...] + fv * rstd * w_ref[...]

    row = pl.BlockSpec((tm, d), lambda i: (i, 0))
    return pl.pallas_call(
        body,
        name=name,
        grid=(t // tm,),
        in_specs=[row, row, pl.BlockSpec((1, d), lambda i: (0, 0))],
        out_specs=row,
        out_shape=jax.ShapeDtypeStruct((t, d), F32),
        compiler_params=_params(("parallel",)),
    )(x, f, w)


def _rms_bwd(x, w, dy, resid, *, out_dtype=F32, name):
    t, d = x.shape
    tm = _pick(t, (512, 256, 128))
    has_res = resid is not None

    def body(*refs):
        if has_res:
            x_ref, w_ref, dy_ref, r_ref, dx_ref, dw_ref = refs
        else:
            x_ref, w_ref, dy_ref, dx_ref, dw_ref = refs
        xv = x_ref[...]
        dyv = dy_ref[...].astype(F32)
        rstd = lax.rsqrt(jnp.mean(xv * xv, axis=-1, keepdims=True) + EPS)
        xn = xv * rstd
        g = dyv * w_ref[...]
        dx = rstd * (g - xn * jnp.mean(g * xn, axis=-1, keepdims=True))
        if has_res:
            dx = dx + r_ref[...]
        dx_ref[...] = dx.astype(out_dtype)
        part = jnp.sum(dyv * xn, axis=0, keepdims=True)

        @pl.when(pl.program_id(0) == 0)
        def _():
            dw_ref[...] = part

        @pl.when(pl.program_id(0) > 0)
        def _():
            dw_ref[...] += part

    row = pl.BlockSpec((tm, d), lambda i: (i, 0))
    vec = pl.BlockSpec((1, d), lambda i: (0, 0))
    ins = [x, w, dy] + ([resid] if has_res else [])
    return pl.pallas_call(
        body,
        name=name,
        grid=(t // tm,),
        in_specs=[row, vec, row] + ([row] if has_res else []),
        out_specs=[row, vec],
        out_shape=[jax.ShapeDtypeStruct((t, d), out_dtype), jax.ShapeDtypeStruct((1, d), F32)],
        compiler_params=_params(("arbitrary",)),
    )(*ins)


def _loss_head(y, target, *, name="loss_head"):
    t, d = y.shape
    tm = _pick(t, (512, 256, 128))

    def body(y_ref, t_ref, dy_ref, l_ref):
        err = y_ref[...] - t_ref[...]
        dy_ref[...] = err * (1.0 / d)
        part = jnp.sum(jnp.sum(err * err, axis=-1, keepdims=True), axis=0, keepdims=True) * (0.5 / d)
        part = jnp.broadcast_to(part, (1, LANE))

        @pl.when(pl.program_id(0) == 0)
        def _():
            l_ref[...] = part

        @pl.when(pl.program_id(0) > 0)
        def _():
            l_ref[...] += part

    row = pl.BlockSpec((tm, d), lambda i: (i, 0))
    dy, l = pl.pallas_call(
        body,
        name=name,
        grid=(t // tm,),
        in_specs=[row, row],
        out_specs=[row, pl.BlockSpec((1, LANE), lambda i: (0, 0))],
        out_shape=[jax.ShapeDtypeStruct((t, d), F32), jax.ShapeDtypeStruct((1, LANE), F32)],
        compiler_params=_params(("arbitrary",)),
    )(y, target)
    return l[0, 0], dy


def _conv_taps(h, w_ref, k_taps):
    out = h * w_ref[k_taps - 1:k_taps, :]
    for k in range(k_taps - 1):
        out = out + _shift_down(h, k_taps - 1 - k) * w_ref[k:k + 1, :]
    return out


def _conv_taps_bwd(h, dhc, w_ref, k_taps):
    dh = dhc * w_ref[k_taps - 1:k_taps, :]
    dws = []
    for k in range(k_taps - 1):
        up = _shift_up(dhc, k_taps - 1 - k)
        dh = dh + up * w_ref[k:k + 1, :]
        dws.append(jnp.sum(up * h, axis=0, keepdims=True))
    dws.append(jnp.sum(dhc * h, axis=0, keepdims=True))
    return dh, jnp.concatenate(dws, axis=0)


FFN_TC = 256


def _interleave(w, tc=FFN_TC):
    f = w.shape[-1] // 2
    tiles = []
    for j in range(f // tc):
        tiles += [w[..., j * tc:(j + 1) * tc], w[..., f + j * tc:f + (j + 1) * tc]]
    return jnp.concatenate(tiles, axis=-1)


def _deinterleave(w, tc=FFN_TC):
    n_tiles = w.shape[-1] // tc
    return jnp.concatenate([w[..., j * tc:(j + 1) * tc] for j in list(range(0, n_tiles, 2)) + list(range(1, n_tiles, 2))],
                           axis=-1)


FFN_ROWS = 512
HALO = 8


def _ffn_up_act(n, w_up, conv_w, conv_b, n_seq, *, name):
    t, d = n.shape
    f2 = w_up.shape[1]
    seq = t // n_seq
    tc = FFN_TC
    nj = f2 // (2 * tc)
    k_taps = conv_w.shape[0]
    rows = min(FFN_ROWS, seq)

    def body(n_ref, wu_ref, w_ref, b_ref, h_ref, hc_ref, o_ref, h_scr):
        h_scr[0:HALO, :] = jnp.zeros((HALO, 2 * tc), F32)
        wu = wu_ref[...]
        for r in range(seq // rows):
            chunk = slice(r * rows, (r + 1) * rows)
            h = jnp.dot(n_ref[chunk, :], wu, preferred_element_type=F32)
            h_scr[HALO + r * rows:HALO + (r + 1) * rows, :] = h
            h_ref[chunk, :] = h.astype(BF16)
            ext = h_scr[r * rows:HALO + (r + 1) * rows, :]
            hc = ext * w_ref[k_taps - 1:k_taps, :]
            for k in range(k_taps - 1):
                hc = hc + pltpu.roll(ext, k_taps - 1 - k, 0) * w_ref[k:k + 1, :]
            hc = hc[HALO:, :] + b_ref[...]
            hc_ref[chunk, :] = hc.astype(BF16)
            o_ref[chunk, :] = (_silu(hc[:, :tc]) * hc[:, tc:]).astype(BF16)

    wide = pl.BlockSpec((seq, 2 * tc), lambda b, j: (b, j))
    return pl.pallas_call(
        body,
        name=name,
        grid=(n_seq, nj),
        in_specs=[
            pl.BlockSpec((seq, d), lambda b, j: (b, 0)),
            pl.BlockSpec((d, 2 * tc), lambda b, j: (0, j)),
            pl.BlockSpec((k_taps, 2 * tc), lambda b, j: (0, j)),
            pl.BlockSpec((1, 2 * tc), lambda b, j: (0, j)),
        ],
        out_specs=[wide, wide, pl.BlockSpec((seq, tc), lambda b, j: (b, j))],
        out_shape=[jax.ShapeDtypeStruct((t, f2), BF16), jax.ShapeDtypeStruct((t, f2), BF16),
                   jax.ShapeDtypeStruct((t, f2 // 2), BF16)],
        scratch_shapes=[pltpu.VMEM((HALO + seq, 2 * tc), F32)],
        compiler_params=_params(("parallel", "arbitrary")),
    )(n, w_up, conv_w, conv_b)


def _ffn_down_bx_act_bwd(df, w_down, h, hc, conv_w, n_seq, *, name):
    t, d = df.shape
    f2 = h.shape[1]
    seq = t // n_seq
    tc = FFN_TC
    nj = f2 // (2 * tc)
    k_taps = conv_w.shape[0]

    def body(df_ref, wd_ref, h_ref, hc_ref, w_ref, dh_ref, dw_ref, db_ref):
        dav = lax.dot_general(df_ref[...], wd_ref[...], (((1,), (1,)), ((), ())), preferred_element_type=F32)
        hcv = hc_ref[...].astype(F32)
        gate, val = hcv[:, :tc], hcv[:, tc:]
        dhc = jnp.concatenate([dav * val * _dsilu(gate), dav * _silu(gate)], axis=1)
        dh, dw = _conv_taps_bwd(h_ref[...].astype(F32), dhc, w_ref, k_taps)
        dh_ref[...] = dh.astype(BF16)
        dw_ref[0] = dw
        db_ref[0] = jnp.sum(dhc, axis=0, keepdims=True)

    wide = pl.BlockSpec((seq, 2 * tc), lambda b, j: (b, j))
    dh, dw, db = pl.pallas_call(
        body,
        name=name,
        grid=(n_seq, nj),
        in_specs=[
            pl.BlockSpec((seq, d), lambda b, j: (b, 0)),
            pl.BlockSpec((tc, d), lambda b, j: (j, 0)),
            wide, wide,
            pl.BlockSpec((k_taps, 2 * tc), lambda b, j: (0, j)),
        ],
        out_specs=[
            wide,
            pl.BlockSpec((1, k_taps, 2 * tc), lambda b, j: (b, 0, j)),
            pl.BlockSpec((1, 1, 2 * tc), lambda b, j: (b, 0, j)),
        ],
        out_shape=[
            jax.ShapeDtypeStruct((t, f2), BF16),
            jax.ShapeDtypeStruct((n_seq, k_taps, f2), F32),
            jax.ShapeDtypeStruct((n_seq, 1, f2), F32),
        ],
        compiler_params=_params(("parallel", "arbitrary")),
    )(df, w_down, h, hc, conv_w)
    return dh, jnp.sum(dw, axis=0), jnp.sum(db, axis=0)


def _window_mixed(u, window):
    s = u
    step = 1
    while step < window:
        s = s + _shift_down(s, step)
        step *= 2
    rows = lax.broadcasted_iota(jnp.int32, u.shape, 0)
    inv_cnt = 1.0 / jnp.minimum(rows + 1, window).astype(F32)
    return s * inv_cnt - u, inv_cnt


def _window_mixed_bwd(dmixed, inv_cnt, window):
    r = dmixed * inv_cnt
    s = r
    step = 1
    while step < window:
        s = s + _shift_up(s, step)
        step *= 2
    return s - dmixed


def _pool_fwd(u, w, scale, n_seq, *, name):
    t, d = u.shape
    seq = t // n_seq
    n_g, dg, _ = w.shape

    def body(u_ref, w_ref, s_ref, o_ref):
        for k, window in enumerate(POOL_WINDOWS):
            @pl.when(pl.program_id(1) == k)
            def _(window=window):
                mixed, _ = _window_mixed(u_ref[...], window)
                pre = jnp.dot(mixed.astype(BF16), w_ref[0].astype(BF16), preferred_element_type=F32)
                o_ref[...] = pre * s_ref[...]

    return pl.pallas_call(
        body,
        name=name,
        grid=(n_seq, n_g),
        in_specs=[
            pl.BlockSpec((seq, dg), lambda b, g: (b, g)),
            pl.BlockSpec((1, dg, dg), lambda b, g: (g, 0, 0)),
            pl.BlockSpec((1, dg), lambda b, g: (0, g)),
        ],
        out_specs=pl.BlockSpec((seq, dg), lambda b, g: (b, g)),
        out_shape=jax.ShapeDtypeStruct((t, d), F32),
        compiler_params=_params(("parallel", "parallel")),
    )(u, w, scale)


def _pool_bwd(u, w, scale, dout, n_seq, *, name):
    t, d = u.shape
    seq = t // n_seq
    n_g, dg, _ = w.shape

    def body(u_ref, w_ref, s_ref, do_ref, du_ref, dw_ref, ds_ref):
        group = pl.program_id(0)
        first = pl.program_id(1) == 0
        for k, window in enumerate(POOL_WINDOWS):
            @pl.when(group == k)
            def _(window=window):
                mixed, inv_cnt = _window_mixed(u_ref[...], window)
                mixed_b = mixed.astype(BF16)
                w_b = w_ref[0].astype(BF16)
                dov = do_ref[...]
                pre = jnp.dot(mixed_b, w_b, preferred_element_type=F32)
                dsc = jnp.sum(dov * pre, axis=0, keepdims=True)
                dpre = (dov * s_ref[...]).astype(BF16)
                dw = lax.dot_general(mixed_b, dpre, (((0,), (0,)), ((), ())), preferred_element_type=F32)
                dmixed = lax.dot_general(dpre, w_b, (((1,), (1,)), ((), ())), preferred_element_type=F32)
                du_ref[...] = _window_mixed_bwd(dmixed, inv_cnt, window)

                @pl.when(first)
                def _():
                    dw_ref[0] = dw
                    ds_ref[...] = dsc

                @pl.when(jnp.logical_not(first))
                def _():
                    dw_ref[0] += dw
                    ds_ref[...] += dsc

    return pl.pallas_call(
        body,
        name=name,
        grid=(n_g, n_seq),
        in_specs=[
            pl.BlockSpec((seq, dg), lambda g, b: (b, g)),
            pl.BlockSpec((1, dg, dg), lambda g, b: (g, 0, 0)),
            pl.BlockSpec((1, dg), lambda g, b: (0, g)),
            pl.BlockSpec((seq, dg), lambda g, b: (b, g)),
        ],
        out_specs=[
            pl.BlockSpec((seq, dg), lambda g, b: (b, g)),
            pl.BlockSpec((1, dg, dg), lambda g, b: (g, 0, 0)),
            pl.BlockSpec((1, dg), lambda g, b: (0, g)),
        ],
        out_shape=[
            jax.ShapeDtypeStruct((t, d), F32),
            jax.ShapeDtypeStruct((n_g, dg, dg), F32),
            jax.ShapeDtypeStruct((1, d), F32),
        ],
        compiler_params=_params(("parallel", "arbitrary")),
    )(u, w, scale, dout)


def _adamw(w, g, m, v, *, name):
    shape = w.shape
    c = shape[-1]
    r = w.size // c
    tm = _pick(r, (512, 256, 128, 64, 32, 16, 8))

    def body(w_ref, g_ref, m_ref, v_ref, d_ref, nm_ref, nv_ref):
        gv = g_ref[...]
        nm = ADAM_B1 * m_ref[...] + (1.0 - ADAM_B1) * gv
        nv = ADAM_B2 * v_ref[...] + (1.0 - ADAM_B2) * (gv * gv)
        m_hat = nm / (1.0 - ADAM_B1 ** ADAM_STEP)
        v_hat = nv / (1.0 - ADAM_B2 ** ADAM_STEP)
        d_ref[...] = -ADAM_LR * (m_hat / (jnp.sqrt(v_hat) + ADAM_EPS) + ADAM_WD * w_ref[...])
        nm_ref[...] = nm
        nv_ref[...] = nv

    blk = pl.BlockSpec((tm, c), lambda i: (i, 0))
    out = jax.ShapeDtypeStruct((r, c), F32)
    res = pl.pallas_call(
        body,
        name=name,
        grid=(r // tm,),
        in_specs=[blk] * 4,
        out_specs=[blk] * 3,
        out_shape=[out] * 3,
        compiler_params=_params(("parallel",)),
    )(w.reshape(r, c), g.reshape(r, c), m.reshape(r, c), v.reshape(r, c))
    return tuple(a.reshape(shape) for a in res)


CONV_TC = 256


def _ssd_conv_fwd(proj, col0, n_cols, conv_w, conv_b, n_seq, *, name):
    t = proj.shape[0]
    seq = t // n_seq
    tc = CONV_TC
    off = col0 // tc
    k_taps = conv_w.shape[0]

    def body(h_ref, w_ref, b_ref, o_ref, pre_ref):
        pre = _conv_taps(h_ref[...], w_ref, k_taps) + b_ref[...]
        pre_ref[...] = pre.astype(BF16)
        o_ref[...] = _silu(pre)

    return pl.pallas_call(
        body,
        name=name,
        grid=(n_seq, n_cols // tc),
        in_specs=[
            pl.BlockSpec((seq, tc), lambda b, j: (b, j + off)),
            pl.BlockSpec((k_taps, tc), lambda b, j: (0, j)),
            pl.BlockSpec((1, tc), lambda b, j: (0, j)),
        ],
        out_specs=[pl.BlockSpec((seq, tc), lambda b, j: (b, j))] * 2,
        out_shape=[jax.ShapeDtypeStruct((t, n_cols), F32), jax.ShapeDtypeStruct((t, n_cols), BF16)],
        compiler_params=_params(("parallel", "parallel")),
    )(proj, conv_w, conv_b)


def _ssd_conv_bwd(proj, col0, conv_w, pre, dparts, dproj, n_seq, *, name):
    t = proj.shape[0]
    seq = t // n_seq
    tc = CONV_TC
    off = col0 // tc
    k_taps = conv_w.shape[0]
    widths = [d.shape[1] // tc for d in dparts]
    starts = [sum(widths[:i]) for i in range(len(widths))]
    n_blocks = sum(widths)
    n_parts = len(dparts)

    def body(h_ref, w_ref, pre_ref, *rest):
        part_refs = rest[:n_parts]
        dh_ref, dw_ref, db_ref = rest[n_parts + 1:]
        j = pl.program_id(0)
        da = part_refs[-1][...]
        for i in reversed(range(n_parts - 1)):
            da = jnp.where(j < starts[i + 1], part_refs[i][...], da)
        dhc = da * _dsilu(pre_ref[...].astype(F32))
        dh, dw = _conv_taps_bwd(h_ref[...], dhc, w_ref, k_taps)
        dh_ref[...] = dh.astype(BF16)
        db = jnp.sum(dhc, axis=0, keepdims=True)

        @pl.when(pl.program_id(1) == 0)
        def _():
            dw_ref[...] = dw
            db_ref[...] = db

        @pl.when(pl.program_id(1) > 0)
        def _():
            dw_ref[...] += dw
            db_ref[...] += db

    def part_spec(start, width):
        return pl.BlockSpec((seq, tc), lambda j, b: (b, jnp.clip(j - start, 0, width - 1)))

    n_cols = n_blocks * tc
    return pl.pallas_call(
        body,
        name=name,
        grid=(n_blocks, n_seq),
        in_specs=[
            pl.BlockSpec((seq, tc), lambda j, b: (b, j + off)),
            pl.BlockSpec((k_taps, tc), lambda j, b: (0, j)),
            pl.BlockSpec((seq, tc), lambda j, b: (b, j)),
        ] + [part_spec(st, wd) for st, wd in zip(starts, widths)] + [ANY],
        out_specs=[
            pl.BlockSpec((seq, tc), lambda j, b: (b, j + off)),
            pl.BlockSpec((k_taps, tc), lambda j, b: (0, j)),
            pl.BlockSpec((1, tc), lambda j, b: (0, j)),
        ],
        out_shape=[
            jax.ShapeDtypeStruct(dproj.shape, BF16),
            jax.ShapeDtypeStruct((k_taps, n_cols), F32),
            jax.ShapeDtypeStruct((1, n_cols), F32),
        ],
        input_output_aliases={3 + n_parts: 0},
        compiler_params=_params(("parallel", "arbitrary")),
    )(proj, conv_w, pre, *dparts, dproj)


def _fill_cols(buf, src, col0, *, name):
    t, c = src.shape
    tm = _pick(t, (1024, 512, 256, 128))

    def body(s_ref, b_ref, o_ref):
        o_ref[...] = s_ref[...].astype(o_ref.dtype)

    return pl.pallas_call(
        body,
        name=name,
        grid=(t // tm,),
        in_specs=[pl.BlockSpec((tm, c), lambda i: (i, 0)), ANY],
        out_specs=pl.BlockSpec((tm, c), lambda i: (i, col0 // c)),
        out_shape=jax.ShapeDtypeStruct(buf.shape, buf.dtype),
        input_output_aliases={1: 0},
        compiler_params=_params(("parallel",)),
    )(src, buf)


def _softplus(x):
    return jnp.maximum(x, 0.0) + jnp.log(1.0 + jnp.exp(-jnp.abs(x)))


def _chunk_decay(dtraw, bias, alog):
    q = dtraw.shape[0]
    dt = _softplus(dtraw + bias)
    a = -jnp.exp(alog)
    rows = lax.broadcasted_iota(jnp.int32, (q, q), 0)
    cols = lax.broadcasted_iota(jnp.int32, (q, q), 1)
    lower = rows >= cols
    acum = jnp.dot(lower.astype(F32), dt * a, precision=lax.Precision.HIGHEST, preferred_element_type=F32)
    return dt, a, acum, acum.T, lower


def _dot_exact(v, sel):
    hi = v.astype(BF16)
    r1 = v - hi.astype(F32)
    mid = r1.astype(BF16)
    lo = (r1 - mid.astype(F32)).astype(BF16)
    return (jnp.dot(hi, sel, preferred_element_type=F32) + jnp.dot(mid, sel, preferred_element_type=F32)
            + jnp.dot(lo, sel, preferred_element_type=F32))


def _head_selectors(gw, p):
    sum_heads = (lax.broadcasted_iota(jnp.int32, (gw, LANE), 0) // p == lax.broadcasted_iota(jnp.int32, (gw, LANE), 1))
    spread = (lax.broadcasted_iota(jnp.int32, (LANE, gw), 0) == lax.broadcasted_iota(jnp.int32, (LANE, gw), 1) // p)
    return sum_heads.astype(BF16), spread.astype(BF16)


def _row_spread(v, spread):
    return _dot_exact(jnp.broadcast_to(v, (8, v.shape[1])), spread)[0:1, :]


def _head_pad(v, r_heads):
    lead = v.shape[:-1]
    vg = v.reshape(lead + (N_SSD_GROUPS, r_heads))
    vg = jnp.pad(vg, [(0, 0)] * len(lead) + [(0, 0), (0, LANE - r_heads)])
    out = vg.reshape(lead + (N_SSD_GROUPS * LANE,))
    return out[None] if out.ndim == 1 else out


def _head_unpad(v, r_heads):
    lead = v.shape[:-1]
    out = v.reshape(lead + (N_SSD_GROUPS, LANE))[..., :r_heads].reshape(lead + (N_SSD_GROUPS * r_heads,))
    return out[0] if (len(lead) == 1 and lead[0] == 1) else out


def _ssd_w_in_layout(w_in, d_inner, d_xbc, r_heads):
    main = w_in[:, :d_inner + d_xbc]
    return jnp.concatenate([main, _head_pad(w_in[:, d_inner + d_xbc:], r_heads)], axis=1)


def _ssd_w_in_unlayout(w, d_inner, d_xbc, r_heads):
    main = w[:, :d_inner + d_xbc]
    return jnp.concatenate([main, _head_unpad(w[:, d_inner + d_xbc:], r_heads)], axis=1)


def _ssd_dims(proj, xbc):
    d_xbc = xbc.shape[1]
    d_inner = d_xbc - 2 * N_SSD_GROUPS * D_STATE
    gw = d_inner // N_SSD_GROUPS
    return d_inner, d_xbc, gw, gw // HEAD_DIM


def _ssd_fwd(proj, xbc, bias_p, alog_p, dskip_p, norm_w, n_seq, *, name):
    t = proj.shape[0]
    d_inner, d_xbc, gw, r_heads = _ssd_dims(proj, xbc)
    q, n, n_g, p = CHUNK, D_STATE, N_SSD_GROUPS, HEAD_DIM
    seq = t // n_seq
    nc = seq // q
    dt_blk0 = (d_inner + d_xbc) // LANE

    def body(x_ref, b_ref, c_ref, z_ref, dtr_ref, bias_ref, alog_ref, dsk_ref, nw_ref, yn_ref, y_ref, hs_ref, h_scr):
        @pl.when(pl.program_id(2) == 0)
        def _():
            h_scr[...] = jnp.zeros_like(h_scr)

        dt, a, acum, acum_t, lower = _chunk_decay(dtr_ref[...], bias_ref[...], alog_ref[...])
        x = x_ref[...]
        bb = b_ref[...].astype(BF16)
        cb = c_ref[...].astype(BF16)
        g_mat = lax.dot_general(cb, bb, (((1,), (1,)), ((), ())), preferred_element_type=F32)
        h_prev = h_scr[...]
        hs_ref[...] = h_prev
        c_h = jnp.dot(cb, h_prev.astype(BF16), preferred_element_type=F32)
        _, spread = _head_selectors(gw, p)
        acum_s = _dot_exact(acum, spread)
        a_last_s = acum_s[q - 1:q, :]
        xdt = x * _dot_exact(dt, spread)
        xdt_b = xdt.astype(BF16)
        ys = []
        for h in range(r_heads):
            decay = jnp.exp(jnp.where(lower, acum[:, h:h + 1] - acum_t[h:h + 1, :], -jnp.inf))
            ys.append(jnp.dot((g_mat * decay).astype(BF16), xdt_b[:, h * p:(h + 1) * p], preferred_element_type=F32))
        y = jnp.concatenate(ys, axis=1) + jnp.exp(acum_s) * c_h + _row_spread(dsk_ref[...], spread) * x
        xd = xdt * jnp.exp(a_last_s - acum_s)
        states = lax.dot_general(bb, xd.astype(BF16), (((0,), (0,)), ((), ())), preferred_element_type=F32)
        h_scr[...] = h_prev * jnp.exp(a_last_s) + states
        y_ref[...] = y
        gated = y * _silu(z_ref[...])
        rstd = lax.rsqrt(jnp.mean(gated * gated, axis=-1, keepdims=True) + EPS)
        yn_ref[...] = (gated * rstd * nw_ref[...]).astype(BF16)

    row = lambda b, g, c: b * nc + c
    vec = pl.BlockSpec((1, LANE), lambda b, g, c: (0, g))
    return pl.pallas_call(
        body,
        name=name,
        grid=(n_seq, n_g, nc),
        in_specs=[
            pl.BlockSpec((q, gw), lambda b, g, c: (row(b, g, c), g)),
            pl.BlockSpec((q, n), lambda b, g, c: (row(b, g, c), d_inner // n + g)),
            pl.BlockSpec((q, n), lambda b, g, c: (row(b, g, c), d_inner // n + n_g + g)),
            pl.BlockSpec((q, gw), lambda b, g, c: (row(b, g, c), g)),
            pl.BlockSpec((q, LANE), lambda b, g, c: (row(b, g, c), dt_blk0 + g)),
            vec, vec, vec,
            pl.BlockSpec((1, gw), lambda b, g, c: (0, g)),
        ],
        out_specs=[
            pl.BlockSpec((q, gw), lambda b, g, c: (row(b, g, c), g)),
            pl.BlockSpec((q, gw), lambda b, g, c: (row(b, g, c), g)),
            pl.BlockSpec((n, gw), lambda b, g, c: (row(b, g, c), g)),
        ],
        out_shape=[
            jax.ShapeDtypeStruct((t, d_inner), BF16),
            jax.ShapeDtypeStruct((t, d_inner), F32),
            jax.ShapeDtypeStruct((n_seq * nc * n, d_inner), F32),
        ],
        scratch_shapes=[pltpu.VMEM((n, gw), F32)],
        compiler_params=_params(("parallel", "parallel", "arbitrary")),
    )(xbc, xbc, xbc, proj, proj, bias_p, alog_p, dskip_p, norm_w)


def _ssd_bwd(proj, xbc, hs, y, dyn, bias_p, alog_p, dskip_p, norm_w, n_seq, *, name):
    t = proj.shape[0]
    d_inner, d_xbc, gw, r_heads = _ssd_dims(proj, xbc)
    q, n, n_g, p = CHUNK, D_STATE, N_SSD_GROUPS, HEAD_DIM
    seq = t // n_seq
    nc = seq // q
    dt_blk0 = (d_inner + d_xbc) // LANE

    def body(x_ref, b_ref, c_ref, z_ref, dtr_ref, bias_ref, alog_ref, dsk_ref, nw_ref, hs_ref, y_ref, dyn_ref,
             dx_ref, db_ref, dc_ref, dz_ref, ddtr_ref, dnw_ref, dbias_ref, dalog_ref, ddsk_ref, dh_scr):
        first = jnp.logical_and(pl.program_id(1) == 0, pl.program_id(2) == 0)

        @pl.when(pl.program_id(2) == 0)
        def _():
            dh_scr[...] = jnp.zeros_like(dh_scr)

        dtraw = dtr_ref[...]
        dt, a, acum, acum_t, lower = _chunk_decay(dtraw, bias_ref[...], alog_ref[...])
        x = x_ref[...]
        bb = b_ref[...].astype(BF16)
        cb = c_ref[...].astype(BF16)
        g_mat = lax.dot_general(cb, bb, (((1,), (1,)), ((), ())), preferred_element_type=F32)

        yv = y_ref[...]
        z = z_ref[...]
        sz = _silu(z)
        gated = yv * sz
        rstd = lax.rsqrt(jnp.mean(gated * gated, axis=-1, keepdims=True) + EPS)
        gn = gated * rstd
        dynv = dyn_ref[...]
        gwt = dynv * nw_ref[...]
        dgated = rstd * (gwt - gn * jnp.mean(gwt * gn, axis=-1, keepdims=True))
        dnw = jnp.sum(dynv * gn, axis=0, keepdims=True)
        dy = dgated * sz
        dz_ref[...] = (dgated * yv * _dsilu(z)).astype(BF16)

        h_prev = hs_ref[...]
        h_prev_b = h_prev.astype(BF16)
        ds = dh_scr[...]
        ds_b = ds.astype(BF16)
        sum_heads, spread = _head_selectors(gw, p)
        acum_s = _dot_exact(acum, spread)
        a_last_s = acum_s[q - 1:q, :]
        dt_s = _dot_exact(dt, spread)
        dsk_s = _row_spread(dsk_ref[...], spread)
        dte_s = jnp.exp(a_last_s - acum_s)
        cd_s = jnp.exp(a_last_s)
        xdt = x * dt_s
        xdt_b = xdt.astype(BF16)
        dy_b = dy.astype(BF16)
        gt_mat = lax.dot_general(bb, cb, (((1,), (1,)), ((), ())), preferred_element_type=F32)
        upper = lax.broadcasted_iota(jnp.int32, (q, q), 0) <= lax.broadcasted_iota(jnp.int32, (q, q), 1)
        dg = jnp.zeros((q, q), F32)
        dxdts, w_diffs = [], []
        for h in range(r_heads):
            hsl = slice(h * p, (h + 1) * p)
            diff = acum[:, h:h + 1] - acum_t[h:h + 1, :]
            decay = jnp.exp(jnp.where(lower, diff, -jnp.inf))
            decay_t = jnp.exp(jnp.where(upper, -diff, -jnp.inf))
            mt_mat = gt_mat * decay_t
            dm = lax.dot_general(dy_b[:, hsl], xdt_b[:, hsl], (((1,), (1,)), ((), ())), preferred_element_type=F32)
            dm_t = lax.dot_general(xdt_b[:, hsl], dy_b[:, hsl], (((1,), (1,)), ((), ())), preferred_element_type=F32)
            dg = dg + dm * decay
            dxdts.append(jnp.dot(mt_mat.astype(BF16), dy_b[:, hsl], preferred_element_type=F32))
            w_diffs.append(dm * (g_mat * decay) - dm_t * mt_mat)
        sel_q = (lax.broadcasted_iota(jnp.int32, (r_heads * q, LANE), 0) // q
                 == lax.broadcasted_iota(jnp.int32, (r_heads * q, LANE), 1)).astype(BF16)
        dacum_diag = _dot_exact(jnp.concatenate(w_diffs, axis=1), sel_q)
        c_h = jnp.dot(cb, h_prev_b, preferred_element_type=F32)
        dxd = jnp.dot(bb, ds_b, preferred_element_type=F32)
        dxdt = jnp.concatenate(dxdts, axis=1) + dxd * dte_s
        dye = dy * jnp.exp(acum_s)
        dye_b = dye.astype(BF16)
        xd = xdt * dte_s
        xd_b = xd.astype(BF16)
        dg_b = dg.astype(BF16)
        dx_ref[...] = dxdt * dt_s + dsk_s * dy
        dc_ref[...] = (jnp.dot(dg_b, bb, preferred_element_type=F32)
                       + lax.dot_general(dye_b, h_prev_b, (((1,), (1,)), ((), ())), preferred_element_type=F32))
        db_ref[...] = (lax.dot_general(dg_b, cb, (((0,), (0,)), ((), ())), preferred_element_type=F32)
                       + lax.dot_general(xd_b, ds_b, (((1,), (1,)), ((), ())), preferred_element_type=F32))
        dh_scr[...] = ds * cd_s + lax.dot_general(cb, dye_b, (((0,), (0,)), ((), ())), preferred_element_type=F32)
        ddt_cols = _dot_exact(x * dxdt, sum_heads)
        dacum_y = _dot_exact(dye * c_h - dxd * xd, sum_heads)
        col_sums = jnp.concatenate([
            jnp.sum(dxd * xd, axis=0, keepdims=True) + jnp.sum(ds * h_prev, axis=0, keepdims=True) * cd_s,
            jnp.sum(dy * x, axis=0, keepdims=True),
            jnp.zeros((6, gw), F32)], axis=0)
        col_sums = _dot_exact(col_sums, sum_heads)
        ddsk = col_sums[1:2, :]
        rows_q = lax.broadcasted_iota(jnp.int32, (q, LANE), 0)
        dacum = dacum_diag + dacum_y + jnp.where(rows_q == q - 1, col_sums[0:1, :], 0.0)
        dadt = jnp.dot(upper.astype(F32), dacum, precision=lax.Precision.HIGHEST, preferred_element_type=F32)
        ddt = dadt * a + ddt_cols
        ddtr = ddt * _sigmoid(dtraw + bias_ref[...])
        ddtr_ref[...] = ddtr
        dbias = jnp.sum(ddtr, axis=0, keepdims=True)
        dalog = jnp.sum(dadt * dt, axis=0, keepdims=True) * a

        @pl.when(first)
        def _():
            dnw_ref[...] = dnw
            dbias_ref[...] = dbias
            dalog_ref[...] = dalog
            ddsk_ref[...] = ddsk

        @pl.when(jnp.logical_not(first))
        def _():
            dnw_ref[...] += dnw
            dbias_ref[...] += dbias
            dalog_ref[...] += dalog
            ddsk_ref[...] += ddsk

    row = lambda g, b, c: b * nc + (nc - 1 - c)
    vec = pl.BlockSpec((1, LANE), lambda g, b, c: (0, g))
    wide = pl.BlockSpec((q, gw), lambda g, b, c: (row(g, b, c), g))
    narrow = pl.BlockSpec((q, n), lambda g, b, c: (row(g, b, c), g))
    return pl.pallas_call(
        body,
        name=name,
        grid=(n_g, n_seq, nc),
        in_specs=[
            wide,
            pl.BlockSpec((q, n), lambda g, b, c: (row(g, b, c), d_inner // n + g)),
            pl.BlockSpec((q, n), lambda g, b, c: (row(g, b, c), d_inner // n + n_g + g)),
            wide,
            pl.BlockSpec((q, LANE), lambda g, b, c: (row(g, b, c), dt_blk0 + g)),
            vec, vec, vec,
            pl.BlockSpec((1, gw), lambda g, b, c: (0, g)),
            pl.BlockSpec((n, gw), lambda g, b, c: (row(g, b, c), g)),
            wide, wide,
        ],
        out_specs=[
            wide, narrow, narrow, wide, narrow,
            pl.BlockSpec((1, gw), lambda g, b, c: (0, g)),
            vec, vec, vec,
        ],
        out_shape=[
            jax.ShapeDtypeStruct((t, d_inner), F32),
            jax.ShapeDtypeStruct((t, n_g * n), F32),
            jax.ShapeDtypeStruct((t, n_g * n), F32),
            jax.ShapeDtypeStruct(proj.shape, BF16),
            jax.ShapeDtypeStruct((t, n_g * LANE), F32),
            jax.ShapeDtypeStruct((1, d_inner), F32),
            jax.ShapeDtypeStruct((1, n_g * LANE), F32),
            jax.ShapeDtypeStruct((1, n_g * LANE), F32),
            jax.ShapeDtypeStruct((1, n_g * LANE), F32),
        ],
        scratch_shapes=[pltpu.VMEM((n, gw), F32)],
        compiler_params=_params(("parallel", "arbitrary", "arbitrary")),
    )(xbc, xbc, xbc, proj, proj, bias_p, alog_p, dskip_p, norm_w, hs, y, dyn)


MESH_IDS = pl.DeviceIdType.MESH


def _my_index():
    return 4 * lax.axis_index("x") + 2 * lax.axis_index("y") + lax.axis_index("c")


def _all_gather(shard, *, name):
    def body(x_ref, out_ref, send_sems, recv_sems, local_sem):
        x, y, c = lax.axis_index("x"), lax.axis_index("y"), lax.axis_index("c")
        me, sibling = (x, y, c), (x, y, 1 - c)
        chips = [(1 - x, y), (x, 1 - y), (1 - x, 1 - y)]

        def blk(px, py, pc):
            return out_ref.at[4 * px + 2 * py + pc]

        def copy(k, block, to, src=None):
            return pltpu.make_async_remote_copy(
                src_ref=blk(*block) if src is None else src, dst_ref=blk(*block),
                send_sem=send_sems.at[k], recv_sem=recv_sems.at[k], device_id=to, device_id_type=MESH_IDS)

        mine = pltpu.make_async_copy(x_ref, blk(*me), local_sem)
        mine.start()
        first = [copy(0, me, sibling, src=x_ref)]
        first += [copy(1 + j, me, (*chip, c), src=x_ref) for j, chip in enumerate(chips)]
        for cp in first:
            cp.start()
        passed = [copy(4 + j, (*chip, c), sibling) for j, chip in enumerate(chips)]
        for j, chip in enumerate(chips):
            copy(1 + j, (*chip, c), me).wait_recv()
            passed[j].start()
        copy(0, sibling, me).wait_recv()
        for j, chip in enumerate(chips):
            copy(4 + j, (*chip, 1 - c), me).wait_recv()
        for cp in first + passed:
            cp.wait_send()
        mine.wait()

    return pl.pallas_call(
        body,
        name=name,
        in_specs=[ANY],
        out_specs=ANY,
        out_shape=jax.ShapeDtypeStruct((N_DEV,) + shard.shape, shard.dtype),
        scratch_shapes=[pltpu.SemaphoreType.DMA((7,)), pltpu.SemaphoreType.DMA((7,)), pltpu.SemaphoreType.DMA],
    )(shard)


HBM_SPEC = pl.BlockSpec(memory_space=pltpu.HBM)
SEM_SPEC = pl.BlockSpec(memory_space=pltpu.SEMAPHORE)
SPLIT_COPY_PARAMS = pltpu.CompilerParams(has_side_effects=pltpu.SideEffectType.DATAFLOW_SIDE_EFFECTING)


def _peer_list():
    x, y, c = lax.axis_index("x"), lax.axis_index("y"), lax.axis_index("c")
    peers = []
    for k in range(1, N_DEV):
        px = 1 - x if k & 4 else x
        py = 1 - y if k & 2 else y
        pc = 1 - c if k & 1 else c
        peers.append(((px, py, pc), 4 * px + 2 * py + pc))
    return 4 * x + 2 * y + c, peers


def _push_copies(src_refs, land_refs, send_sems, recv_sems, blockwise):
    me, peers = _peer_list()
    copies = []
    for a, (src_ref, land_ref) in enumerate(zip(src_refs, land_refs)):
        for k, (dev, idx) in enumerate(peers):
            sem = a * (N_DEV - 1) + k
            src = src_ref.at[idx] if blockwise else src_ref
            copies.append(tuple(
                pltpu.make_async_remote_copy(src_ref=src, dst_ref=land_ref.at[slot], send_sem=send_sems.at[sem],
                                             recv_sem=recv_sems.at[sem], device_id=dev, device_id_type=MESH_IDS)
                for slot in (me, idx)))
    return copies


def _push_start(srcs, blockwise, after, *, name):
    n = len(srcs)
    blocks = [s_.shape[1:] if blockwise else s_.shape for s_ in srcs]

    def body(*refs):
        src_refs, land_refs = refs[:n], refs[n:2 * n]
        send_sems, recv_sems = refs[2 * n + 1], refs[2 * n + 2]
        token = refs[-1]
        for send, _ in _push_copies(src_refs, land_refs, send_sems, recv_sems, blockwise):
            send.start()
        token[...] = jnp.zeros_like(token)

    n_sem = n * (N_DEV - 1)
    lands = [lax.empty((N_DEV,) + b, s_.dtype) for b, s_ in zip(blocks, srcs)]
    out = pl.pallas_call(
        body,
        name=name,
        in_specs=[HBM_SPEC] * (2 * n) + [ANY],
        out_specs=(SEM_SPEC, SEM_SPEC) + (HBM_SPEC,) * (2 * n) + (pl.BlockSpec(memory_space=pltpu.VMEM),),
        out_shape=(pltpu.SemaphoreType.DMA((n_sem,)), pltpu.SemaphoreType.DMA((n_sem,)))
        + tuple(pltpu.HBM(a.shape, a.dtype) for a in list(srcs) + lands)
        + (jax.ShapeDtypeStruct((8, LANE), F32),),
        input_output_aliases={i: 2 + i for i in range(2 * n)},
        compiler_params=SPLIT_COPY_PARAMS,
    )(*[pltpu.with_memory_space_constraint(a, pltpu.HBM) for a in list(srcs) + lands], after)
    return out[0], out[1], out[2:2 + n], out[2 + n:2 + 2 * n], out[-1]


def _push_wait(send_sems, recv_sems, srcs, lands, blockwise, after, *, name):
    n = len(srcs)

    def body(*refs):
        src_refs, land_refs = refs[:n], refs[n:2 * n]
        send_sems, recv_sems = refs[2 * n], refs[2 * n + 1]
        for send, recv in _push_copies(src_refs, land_refs, send_sems, recv_sems, blockwise):
            send.wait_send()
            recv.wait_recv()

    out = pl.pallas_call(
        body,
        name=name,
        in_specs=[HBM_SPEC] * (2 * n) + [SEM_SPEC, SEM_SPEC, ANY],
        out_specs=(HBM_SPEC,) * (2 * n),
        out_shape=tuple(pltpu.HBM(a.shape, a.dtype) for a in list(srcs) + list(lands)),
        input_output_aliases={i: i for i in range(2 * n)},
        compiler_params=SPLIT_COPY_PARAMS,
    )(*srcs, *lands, send_sems, recv_sems, after)
    return out[n:]


def _with_own_slot(landing, own):
    slot = lax.broadcasted_iota(jnp.int32, (N_DEV,) + (1,) * own.ndim, 0)
    return jnp.where(slot == _my_index(), own[None], landing)


def _sum_slots(parts, own=None, *, name):
    shape = parts.shape[1:]
    n, c = parts.shape[0], parts.shape[-1]
    r = parts.size // (n * c)
    tm = _pick(r, (256, 128, 64, 32, 16, 8))

    def body(p_ref, *rest):
        o_ref = rest[-1]
        me = _my_index()

        def slot(s):
            if own is None:
                return p_ref[s].astype(F32)
            return jnp.where(me == s, rest[0][...], p_ref[s]).astype(F32)

        acc = slot(0)
        for s in range(1, n):
            acc = acc + slot(s)
        o_ref[...] = acc

    tile = pl.BlockSpec((tm, c), lambda i: (i, 0))
    return pl.pallas_call(
        body,
        name=name,
        grid=(r // tm,),
        in_specs=[pl.BlockSpec((n, tm, c), lambda i: (0, i, 0))] + ([] if own is None else [tile]),
        out_specs=tile,
        out_shape=jax.ShapeDtypeStruct((r, c), F32),
        compiler_params=_params(("parallel",)),
    )(parts.reshape(n, r, c), *([] if own is None else [own.reshape(r, c)])).reshape(shape)


def _row_count(shape):
    c = shape[-1]
    rows = 1
    for s in shape[:-1]:
        rows *= s
    return rows, c, c + (-c) % LANE


PACK_ROWS = 256


def _pack_rows(arrays):
    pieces = []
    for a in arrays:
        rows, c, cp = _row_count(a.shape)
        a2 = a.reshape(rows, c)
        if cp > c:
            a2 = jnp.pad(a2, ((0, 0), (0, cp - c)))
        a2 = a2.reshape(rows * cp // LANE, LANE)
        if a2.shape[0] % 8:
            a2 = jnp.pad(a2, ((0, 8 - a2.shape[0] % 8), (0, 0)))
        pieces.append(a2)
    total = sum(p.shape[0] for p in pieces)
    if total % PACK_ROWS:
        pieces.append(jnp.zeros((PACK_ROWS - total % PACK_ROWS, LANE), F32))
    return jnp.concatenate(pieces, axis=0)


def _unpack_rows(packed, shapes, lead=()):
    out, off = [], 0
    for shp in shapes:
        rows, c, cp = _row_count(shp)
        n_rows = rows * cp // LANE
        seg = packed[..., off:off + n_rows, :].reshape(lead + (rows, cp))
        out.append(seg[..., :c].reshape(lead + tuple(shp)))
        off += n_rows + (-n_rows) % 8
    return out


def _unshard(stacked, axis):
    if axis == stacked.ndim - 2:
        return jnp.concatenate([stacked[d] for d in range(N_DEV)], axis=axis)
    moved = jnp.moveaxis(stacked, 0, axis)
    shp = moved.shape
    return moved.reshape(shp[:axis] + (shp[axis] * shp[axis + 1],) + shp[axis + 2:])


def _shard_major(full, axis):
    shp = full.shape
    if axis == full.ndim - 1:
        size = shp[axis] // N_DEV
        return jnp.stack([full[..., d * size:(d + 1) * size] for d in range(N_DEV)])
    split = full.reshape(shp[:axis] + (N_DEV, shp[axis] // N_DEV) + shp[axis + 1:])
    return jnp.moveaxis(split, axis, 0)


def _my_shard(full, axis):
    size = full.shape[axis] // N_DEV
    return lax.dynamic_slice_in_dim(full, _my_index() * size, size, axis)


def _local_step(x, target, w, fetch, emit, n_seq):
    def with_token(vec, token):
        return vec + jnp.tile(token[0:1, :], (1, vec.shape[1] // LANE))

    depth, d_model = w["norm_mix_pre"].shape
    d_inner = w["ssd_norm_w"].shape[1]
    d_xbc = w["ssd_conv_w"].shape[2]
    saved = []
    for i in range(depth):
        j = i // 2
        m, token = fetch(i, "mix", x)
        mix_pre_w = with_token(w["norm_mix_pre"][i:i + 1], token)
        s = {"x": x, "mix_pre_w": mix_pre_w}
        if i % 2 == 0:
            u = _rms_fwd(x, mix_pre_w, out_dtype=BF16, name=f"l{i}_mix_pre")
            proj = _mm(u, m["ssd_w_in"], name=f"l{i}_ssd_in")
            xbc, xbc_pre = _ssd_conv_fwd(proj, d_inner, d_xbc, w["ssd_conv_w"][j], w["ssd_conv_b"][j:j + 1], n_seq,
                                         name=f"l{i}_ssd_conv")
            yn, y, hs = _ssd_fwd(proj, xbc, w["ssd_dt_bias"][j:j + 1], w["ssd_a_log"][j:j + 1], w["ssd_d"][j:j + 1],
                                 w["ssd_norm_w"][j:j + 1], n_seq, name=f"l{i}_ssd_scan")
            m_out, token = fetch(i, "out", yn)
            m = {**m, **m_out}
            mix = _mm(yn, m["ssd_w_out"], name=f"l{i}_ssd_out")
            s.update(u=u, proj=proj, xbc=xbc, xbc_pre=xbc_pre, yn=yn, y=y, hs=hs)
        else:
            u = _rms_fwd(x, mix_pre_w, out_dtype=F32, name=f"l{i}_mix_pre")
            mix = _pool_fwd(u, m["pool_w"], w["pool_scale"][j:j + 1], n_seq, name=f"l{i}_pool")
            s.update(u=u)
            token = jnp.zeros((8, LANE), F32)
        x1 = _res_rms_fwd(x, mix, with_token(w["norm_mix_post"][i:i + 1], token), name=f"l{i}_mix_post")
        m_ffn, token = fetch(i, "ffn", x1)
        m = {**m, **m_ffn}
        ffn_pre_w = with_token(w["norm_ffn_pre"][i:i + 1], token)
        n = _rms_fwd(x1, ffn_pre_w, out_dtype=BF16, name=f"l{i}_ffn_pre")
        h, hc, a = _ffn_up_act(n, m["ffn_w_up"], w["ffn_conv_w"][i], w["ffn_conv_b"][i:i + 1], n_seq,
                               name=f"l{i}_ffn_up_act")
        f = _mm(a, m["ffn_w_down"], name=f"l{i}_ffn_down")
        x = _res_rms_fwd(x1, f, w["norm_ffn_post"][i:i + 1], name=f"l{i}_ffn_post")
        s.update(mix=mix, x1=x1, n=n, h=h, hc=hc, a=a, f=f, m=m, ffn_pre_w=ffn_pre_w)
        saved.append(s)

    loss, dx = _loss_head(x, target)
    grads = {k: [None] * len(w[k]) for k in SMALL}
    token = jnp.zeros((8, LANE), F32)
    for i in reversed(range(depth)):
        j = i // 2
        s = saved[i]
        m, gm = s["m"], {}
        df, grads["norm_ffn_post"][i] = _rms_bwd(s["f"], with_token(w["norm_ffn_post"][i:i + 1], token), dx, None,
                                                 out_dtype=BF16, name=f"l{i}_ffn_post_b")
        gm["ffn_w_down"] = _mm(s["a"], df, ta=True, name=f"l{i}_ffn_down_bw")
        dh, grads["ffn_conv_w"][i], grads["ffn_conv_b"][i] = _ffn_down_bx_act_bwd(
            df, m["ffn_w_down"], s["h"], s["hc"], w["ffn_conv_w"][i], n_seq, name=f"l{i}_ffn_act_b")
        dn = _mm(dh, m["ffn_w_up"], tb=True, name=f"l{i}_ffn_up_bx")
        gm["ffn_w_up"] = _mm(s["n"], dh, ta=True, name=f"l{i}_ffn_up_bw")
        dx1, grads["norm_ffn_pre"][i] = _rms_bwd(s["x1"], s["ffn_pre_w"], dn, dx, name=f"l{i}_ffn_pre_b")
        token = emit(i, "ffn", gm, dx1)
        gm = {}
        dmix, grads["norm_mix_post"][i] = _rms_bwd(s["mix"], with_token(w["norm_mix_post"][i:i + 1], token), dx1, None,
                                                   out_dtype=BF16 if i % 2 == 0 else F32, name=f"l{i}_mix_post_b")
        if i % 2 == 0:
            dyn = _mm(dmix, m["ssd_w_out"], tb=True, name=f"l{i}_ssd_out_bx")
            gm["ssd_w_out"] = _mm(s["yn"], dmix, ta=True, name=f"l{i}_ssd_out_bw")
            token = emit(i, "out", gm, dyn)
            gm = {}
            dxs, db, dc, dz, ddtr, dnw, dbias, dalog, ddsk = _ssd_bwd(
                s["proj"], s["xbc"], s["hs"], s["y"], dyn, w["ssd_dt_bias"][j:j + 1], w["ssd_a_log"][j:j + 1],
                w["ssd_d"][j:j + 1], with_token(w["ssd_norm_w"][j:j + 1], token), n_seq, name=f"l{i}_ssd_scan_b")
            grads["ssd_norm_w"][j], grads["ssd_dt_bias"][j], grads["ssd_a_log"][j], grads["ssd_d"][j] = (
                dnw, dbias, dalog, ddsk)
            dproj, grads["ssd_conv_w"][j], grads["ssd_conv_b"][j] = _ssd_conv_bwd(
                s["proj"], d_inner, w["ssd_conv_w"][j], s["xbc_pre"], (dxs, db, dc), dz, n_seq,
                name=f"l{i}_ssd_conv_b")
            dproj = _fill_cols(dproj, ddtr, d_inner + d_xbc, name=f"l{i}_ssd_dt_b")
            du = _mm(dproj, m["ssd_w_in"], tb=True, name=f"l{i}_ssd_in_bx")
            gm["ssd_w_in"] = _mm(s["u"], dproj, ta=True, name=f"l{i}_ssd_in_bw")
        else:
            du, gm["pool_w"], grads["pool_scale"][j] = _pool_bwd(
                s["u"], m["pool_w"], w["pool_scale"][j:j + 1], dmix, n_seq, name=f"l{i}_pool_b")
        dx, grads["norm_mix_pre"][i] = _rms_bwd(s["x"], s["mix_pre_w"], du, dx1, name=f"l{i}_mix_pre_b")
        token = emit(i, "mix", gm, dx)
    return loss, dx, grads


BIG = (("ssd_w_in", 2), ("ssd_w_out", 1), ("pool_w", 2), ("ffn_w_up", 2), ("ffn_w_down", 1))
SMALL_SHARDED = (("ssd_conv_w", 2), ("ffn_conv_w", 2), ("pool_scale", 1))
SMALL = ("ssd_conv_w", "ssd_conv_b", "ssd_dt_bias", "ssd_a_log", "ssd_d", "ssd_norm_w", "pool_scale", "ffn_conv_w",
         "ffn_conv_b", "norm_mix_pre", "norm_mix_post", "norm_ffn_pre", "norm_ffn_post")
WEIGHTS = ("ssd_w_in", "ssd_conv_w", "ssd_conv_b", "ssd_dt_bias", "ssd_a_log", "ssd_d", "ssd_norm_w", "ssd_w_out",
           "pool_w", "pool_scale", "ffn_w_up", "ffn_conv_w", "ffn_conv_b", "ffn_w_down", "norm_mix_pre",
           "norm_mix_post", "norm_ffn_pre", "norm_ffn_post")


def _ssd_sizes(d_inner):
    return d_inner + 2 * N_SSD_GROUPS * D_STATE, d_inner // HEAD_DIM // N_SSD_GROUPS


def _small_compute_layout(full, d_inner):
    _, r_heads = _ssd_sizes(d_inner)
    w = {k: full[k] for k in SMALL}
    for k in ("ssd_dt_bias", "ssd_a_log", "ssd_d"):
        w[k] = _head_pad(full[k], r_heads)
    for k in ("ffn_conv_w", "ffn_conv_b"):
        w[k] = _interleave(full[k])
    return w


def _matmul_compute_layout(k, full, d_inner):
    d_xbc, r_heads = _ssd_sizes(d_inner)
    if k == "ssd_w_in":
        return _ssd_w_in_layout(full, d_inner, d_xbc, r_heads)
    if k == "ffn_w_up":
        return _interleave(full)
    return full


def _layer_matrices(i, part):
    if part == "ffn":
        return (("ffn_w_up", 1, i), ("ffn_w_down", 0, i))
    if i % 2 == 1:
        return (("pool_w", 1, i // 2),) if part == "mix" else ()
    return (("ssd_w_in", 1, i // 2),) if part == "mix" else (("ssd_w_out", 0, i // 2),)


def _fetch_group(i, part):
    mix, out, ffn = (_layer_matrices(i, p) for p in ("mix", "out", "ffn"))
    if i % 2 == 1:
        return mix + ffn if part == "mix" else ()
    if i == 0:
        return {"mix": mix, "out": out + ffn, "ffn": ()}[part]
    return {"mix": mix + out, "out": (), "ffn": ffn}[part]


def _matmul_grad_reference_layout(k, g, d_inner):
    d_xbc, r_heads = _ssd_sizes(d_inner)
    if k == "ssd_w_in":
        return _ssd_w_in_unlayout(g, d_inner, d_xbc, r_heads)
    if k == "ffn_w_up":
        return _deinterleave(g)
    return g


def _small_grads_reference_layout(grads, shapes, d_inner):
    _, r_heads = _ssd_sizes(d_inner)
    g = {k: jnp.stack(grads[k]) for k in SMALL}
    for k in ("ssd_dt_bias", "ssd_a_log", "ssd_d"):
        g[k] = _head_unpad(g[k][:, 0], r_heads)
    for k in ("ffn_conv_w", "ffn_conv_b"):
        g[k] = _deinterleave(g[k])
    return {k: v.reshape(shapes[k]) for k, v in g.items()}


def kernel(x, ssd_w_in, ssd_conv_w, ssd_conv_b, ssd_dt_bias, ssd_a_log, ssd_d, ssd_norm_w, ssd_w_out, pool_w, pool_scale, ffn_w_up, ffn_conv_w, ffn_conv_b, ffn_w_down, norm_mix_pre, norm_mix_post, norm_ffn_pre, norm_ffn_post, loss_target, m_ssd_w_in, m_ssd_conv_w, m_ssd_conv_b, m_ssd_dt_bias, m_ssd_a_log, m_ssd_d, m_ssd_norm_w, m_ssd_w_out, m_pool_w, m_pool_scale, m_ffn_w_up, m_ffn_conv_w, m_ffn_conv_b, m_ffn_w_down, m_norm_mix_pre, m_norm_mix_post, m_norm_ffn_pre, m_norm_ffn_post, v_ssd_w_in, v_ssd_conv_w, v_ssd_conv_b, v_ssd_dt_bias, v_ssd_a_log, v_ssd_d, v_ssd_norm_w, v_ssd_w_out, v_pool_w, v_pool_scale, v_ffn_w_up, v_ffn_conv_w, v_ffn_conv_b, v_ffn_w_down, v_norm_mix_pre, v_norm_mix_post, v_norm_ffn_pre, v_norm_ffn_post):
    shards = dict(ssd_w_in=ssd_w_in, ssd_conv_w=ssd_conv_w, ssd_conv_b=ssd_conv_b, ssd_dt_bias=ssd_dt_bias,
                  ssd_a_log=ssd_a_log, ssd_d=ssd_d, ssd_norm_w=ssd_norm_w, ssd_w_out=ssd_w_out, pool_w=pool_w,
                  pool_scale=pool_scale, ffn_w_up=ffn_w_up, ffn_conv_w=ffn_conv_w, ffn_conv_b=ffn_conv_b,
                  ffn_w_down=ffn_w_down, norm_mix_pre=norm_mix_pre, norm_mix_post=norm_mix_post,
                  norm_ffn_pre=norm_ffn_pre, norm_ffn_post=norm_ffn_post)
    moments_m = dict(zip(WEIGHTS, (m_ssd_w_in, m_ssd_conv_w, m_ssd_conv_b, m_ssd_dt_bias, m_ssd_a_log, m_ssd_d, m_ssd_norm_w, m_ssd_w_out, m_pool_w, m_pool_scale, m_ffn_w_up, m_ffn_conv_w, m_ffn_conv_b, m_ffn_w_down, m_norm_mix_pre, m_norm_mix_post, m_norm_ffn_pre, m_norm_ffn_post)))
    moments_v = dict(zip(WEIGHTS, (v_ssd_w_in, v_ssd_conv_w, v_ssd_conv_b, v_ssd_dt_bias, v_ssd_a_log, v_ssd_d, v_ssd_norm_w, v_ssd_w_out, v_pool_w, v_pool_scale, v_ffn_w_up, v_ffn_conv_w, v_ffn_conv_b, v_ffn_w_down, v_norm_mix_pre, v_norm_mix_post, v_norm_ffn_pre, v_norm_ffn_post)))
    n_seq, seq, d_model = x.shape
    t = n_seq * seq

    d_inner = ssd_norm_w.shape[1]
    depth = norm_mix_pre.shape[0]
    x2 = x.reshape(t, d_model)

    shard16 = {k: shards[k].astype(BF16) for k, _ in BIG}
    order = [(i, part) for i in range(depth) for part in ("mix", "out", "ffn") if _fetch_group(i, part)]
    fetches = {}

    def start_fetch(key, after):
        srcs = [shard16[k][l] for k, _, l in _fetch_group(*key)]
        fetches[key] = _push_start(srcs, False, after, name=f"fetch{key[0]}{key[1]}_start")

    full = dict(shards)
    small_all = _all_gather(_pack_rows([shards[k] for k, _ in SMALL_SHARDED]), name="gather_small_weights")
    small_stacked = _unpack_rows(small_all, [shards[k].shape for k, _ in SMALL_SHARDED], lead=(N_DEV,))
    for (k, axis), st in zip(SMALL_SHARDED, small_stacked):
        full[k] = _unshard(st, axis)
    w = _small_compute_layout(full, d_inner)
    ready = {}

    def fetch(i, part, x_now):
        key = (i, part)
        token = jnp.zeros((8, LANE), F32)
        if key in order:
            if key == order[0]:
                wholes = [_unshard(_all_gather(shard16[k][l], name=f"fetch0_{k}"), axis) for k, axis, l in _fetch_group(i, part)]
                nxt_after = wholes[0]
            else:
                send, recv, srcs, lands, _ = fetches[key]
                lands = _push_wait(send, recv, srcs, lands, False, x_now, name=f"fetch{i}{part}_wait")
                wholes = [_unshard(_with_own_slot(land, shard16[k][l]), axis)
                          for (k, axis, l), land in zip(_fetch_group(i, part), lands)]
                nxt_after = lands[0]
            for (k, _, l), whole in zip(_fetch_group(i, part), wholes):
                ready[k, l] = _matmul_compute_layout(k, whole, d_inner)
            nxt = order.index(key) + 1
            if nxt < len(order):
                start_fetch(order[nxt], nxt_after)
                token = fetches[order[nxt]][4]
        return {k: ready[k, l] for k, _, l in _layer_matrices(i, part)}, token

    g_layers = {}
    in_flight = []

    def finish_exchange(after):
        key, blocks, (send, recv, srcs, lands, _) = in_flight.pop()
        lands = _push_wait(send, recv, srcs, lands, True, after, name=f"exchange{key[0]}{key[1]}_wait")
        for (k, _, l), land, block in zip(_layer_matrices(*key), lands, blocks):
            own = lax.dynamic_index_in_dim(block, _my_index(), 0, keepdims=False)
            g_layers[k, l] = _sum_slots(land, own, name=f"sum{key[0]}_{k}")

    def emit(i, part, gm, dx_now):
        if in_flight:
            finish_exchange(dx_now)
        blocks = [_shard_major(_matmul_grad_reference_layout(k, gm[k].astype(BF16), d_inner), axis)
                  for k, axis, _ in _layer_matrices(i, part)]
        started = _push_start(blocks, True, dx_now, name=f"exchange{i}{part}_start")
        in_flight.append(((i, part), blocks, started))
        return started[4]

    loss, dx, grads = _local_step(x2, loss_target.reshape(t, d_model), w, fetch, emit, n_seq)
    loss = lax.psum(loss, ("x", "y", "c"))

    g_shard = {}
    small_shapes = {k: full[k].shape for k in SMALL}
    g_small = _small_grads_reference_layout(grads, small_shapes, d_inner)
    s_all = _all_gather(_pack_rows([g_small[k] for k in SMALL]) + in_flight[0][2][4][0:1, :], name="gather_small_grads")
    for k, g in zip(SMALL, _unpack_rows(_sum_slots(s_all, name="sum_small_grads"), [small_shapes[k] for k in SMALL])):
        g_shard[k] = g
    for k, axis in SMALL_SHARDED:
        g_shard[k] = _my_shard(g_shard[k], axis)

    last = [k for k, _, _ in _layer_matrices(*in_flight[0][0])]
    deltas, new_m, new_v = {}, {}, {}
    for k in [k for k in WEIGHTS if k not in last] + last:
        if k == last[0]:
            finish_exchange(deltas["ffn_w_up"])
        if k in dict(BIG):
            g_shard[k] = jnp.stack([g_layers[k, l] for l in range(shards[k].shape[0])])
        deltas[k], new_m[k], new_v[k] = _adamw(shards[k], g_shard[k], moments_m[k], moments_v[k], name=f"adamw_{k}")
    return (loss, dx.reshape(x.shape), *[g_shard[k] for k in WEIGHTS], *[deltas[k] for k in WEIGHTS],
            *[new_m[k] for k in WEIGHTS], *[new_v[k] for k in WEIGHTS])
```

```python
import functools

import jax
import jax.numpy as jnp
from jax import lax
from jax.experimental import pallas as pl
from jax.experimental.pallas import tpu as pltpu

F32 = jnp.float32
BF16 = jnp.bfloat16

N_DEV = 8
HEAD_DIM = 64
N_SSD_GROUPS = 4
D_STATE = 128
CHUNK = 128
POOL_WINDOWS = (2, 4, 8, 16)
EPS = 1e-6
LANE = 128
ADAM_LR = 0.001
ADAM_B1 = 0.9
ADAM_B2 = 0.999
ADAM_EPS = 1e-08
ADAM_WD = 0.01
ADAM_STEP = 10
VMEM_LIMIT = 56 * 1024 * 1024
ANY = pl.BlockSpec(memory_space=pl.ANY)


def _pick(n, cands):
    for c in cands:
        if n % c == 0:
            return c
    return n


def _params(sem):
    return pltpu.CompilerParams(dimension_semantics=sem, vmem_limit_bytes=VMEM_LIMIT)


def _sigmoid(x):
    return 0.5 * jnp.tanh(0.5 * x) + 0.5


def _silu(x):
    return x * _sigmoid(x)


def _dsilu(x):
    s = _sigmoid(x)
    return s * (1.0 + x * (1.0 - s))


def _shift_down(x, s):
    rows = lax.broadcasted_iota(jnp.int32, x.shape, 0)
    return jnp.where(rows >= s, pltpu.roll(x, s, 0), 0.0)


def _shift_up(x, s):
    n = x.shape[0]
    rows = lax.broadcasted_iota(jnp.int32, x.shape, 0)
    return jnp.where(rows < n - s, pltpu.roll(x, n - s, 0), 0.0)


MM_VMEM_BUDGET = 40 * 1024 * 1024
MM_STEP_BYTES = 1_300_000
MM_SUB = 512


def _mm_tiles(m, n, k, a_bytes, b_bytes, o_bytes):
    def cands(dim, sizes):
        out = [s for s in sizes if s <= dim and dim % s == 0]
        return out or [dim]

    best = None
    for tm in cands(m, (m, m // 2, 2048, 1024, 512, 256, 128)):
        if tm % LANE:
            continue
        for tn in cands(n, (n, n // 2, n // 4, 2048, 1024, 512, 256, 128)):
            if tn % (2 * LANE) and tn != n:
                continue
            for tk in cands(k, (k, k // 2, 2048, 1024, 512)):
                if tk % LANE:
                    continue
                nk = k // tk
                acc = tm * tn * 4 if (nk > 1 and o_bytes != 4) else 0
                temps = tm * min(tn, MM_SUB) * 4 + (tm * tk * 2 if a_bytes == 4 else 0) + (tk * tn * 2 if b_bytes == 4 else 0)
                vmem = 2 * (tm * tk * a_bytes + tk * tn * b_bytes + tm * tn * o_bytes) + acc + temps
                if vmem > MM_VMEM_BUDGET:
                    continue
                steps = (m // tm) * (n // tn) * nk
                cost = (m * k * a_bytes * (n // tn) + k * n * b_bytes * (m // tm) + m * n * o_bytes
                        + steps * MM_STEP_BYTES)
                if best is None or cost < best[0]:
                    best = (cost, tm, tn, tk)
    return best[1:]


def _mm(a, b, *, ta=False, tb=False, out_dtype=F32, name="mm"):
    m, k = (a.shape[1], a.shape[0]) if ta else a.shape
    n = b.shape[0] if tb else b.shape[1]
    o_bytes = jnp.dtype(out_dtype).itemsize
    tm, tn, tk = _mm_tiles(m, n, k, a.dtype.itemsize, b.dtype.itemsize, o_bytes)
    nk = k // tk
    sub = _pick(tn, (MM_SUB, 256))
    use_acc = nk > 1 and o_bytes != 4
    a_spec = pl.BlockSpec((tk, tm), lambda i, j, kk: (kk, i)) if ta else pl.BlockSpec((tm, tk), lambda i, j, kk: (i, kk))
    b_spec = pl.BlockSpec((tn, tk), lambda i, j, kk: (j, kk)) if tb else pl.BlockSpec((tk, tn), lambda i, j, kk: (kk, j))
    dims = (((1,), (1 if tb else 0,)), ((), ()))

    def body(a_ref, b_ref, o_ref, *scratch):
        kk = pl.program_id(2)
        acc_ref = scratch[0] if use_acc else o_ref
        if nk > 1:
            @pl.when(kk == 0)
            def _():
                acc_ref[...] = jnp.zeros_like(acc_ref)

        av = a_ref[...].astype(BF16)
        if ta:
            av = av.T
        for s in range(tn // sub):
            cols = slice(s * sub, (s + 1) * sub)
            bv = (b_ref[cols, :] if tb else b_ref[:, cols]).astype(BF16)
            part = lax.dot_general(av, bv, dims, preferred_element_type=F32)
            if nk == 1:
                o_ref[:, cols] = part.astype(out_dtype)
            else:
                acc_ref[:, cols] += part
        if use_acc:
            @pl.when(kk == nk - 1)
            def _():
                o_ref[...] = acc_ref[...].astype(out_dtype)

    return pl.pallas_call(
        body,
        name=name,
        grid=(m // tm, n // tn, nk),
        in_specs=[a_spec, b_spec],
        out_specs=pl.BlockSpec((tm, tn), lambda i, j, kk: (i, j)),
        out_shape=jax.ShapeDtypeStruct((m, n), out_dtype),
        scratch_shapes=[pltpu.VMEM((tm, tn), F32)] if use_acc else [],
        compiler_params=_params(("parallel", "parallel", "arbitrary")),
    )(a, b)


def _rms_fwd(x, w, *, out_dtype, name):
    t, d = x.shape
    tm = _pick(t, (512, 256, 128))

    def body(x_ref, w_ref, o_ref):
        xv = x_ref[...]
        rstd = lax.rsqrt(jnp.mean(xv * xv, axis=-1, keepdims=True) + EPS)
        o_ref[...] = (xv * rstd * w_ref[...]).astype(out_dtype)

    return pl.pallas_call(
        body,
        name=name,
        grid=(t // tm,),
        in_specs=[pl.BlockSpec((tm, d), lambda i: (i, 0)), pl.BlockSpec((1, d), lambda i: (0, 0))],
        out_specs=pl.BlockSpec((tm, d), lambda i: (i, 0)),
        out_shape=jax.ShapeDtypeStruct((t, d), out_dtype),
        compiler_params=_params(("parallel",)),
    )(x, w)


def _res_rms_fwd(x, f, w, *, name):
    t, d = x.shape
    tm = _pick(t, (512, 256, 128))

    def body(x_ref, f_ref, w_ref, o_ref):
        fv = f_ref[...]
        rstd = lax.rsqrt(jnp.mean(fv * fv, axis=-1, keepdims=True) + EPS)
        o_ref[...] = x_ref[...] + fv * rstd * w_ref[...]

    row = pl.BlockSpec((tm, d), lambda i: (i, 0))
    return pl.pallas_call(
        body,
        name=name,
        grid=(t // tm,),
        in_specs=[row, row, pl.BlockSpec((1, d), lambda i: (0, 0))],
        out_specs=row,
        out_shape=jax.ShapeDtypeStruct((t, d), F32),
        compiler_params=_params(("parallel",)),
    )(x, f, w)


def _rms_bwd(x, w, dy, resid, *, out_dtype=F32, name):
    t, d = x.shape
    tm = _pick(t, (512, 256, 128))
    has_res = resid is not None

    def body(*refs):
        if has_res:
            x_ref, w_ref, dy_ref, r_ref, dx_ref, dw_ref = refs
        else:
            x_ref, w_ref, dy_ref, dx_ref, dw_ref = refs
        xv = x_ref[...]
        dyv = dy_ref[...].astype(F32)
        rstd = lax.rsqrt(jnp.mean(xv * xv, axis=-1, keepdims=True) + EPS)
        xn = xv * rstd
        g = dyv * w_ref[...]
        dx = rstd * (g - xn * jnp.mean(g * xn, axis=-1, keepdims=True))
        if has_res:
            dx = dx + r_ref[...]
        dx_ref[...] = dx.astype(out_dtype)
        part = jnp.sum(dyv * xn, axis=0, keepdims=True)

        @pl.when(pl.program_id(0) == 0)
        def _():
            dw_ref[...] = part

        @pl.when(pl.program_id(0) > 0)
        def _():
            dw_ref[...] += part

    row = pl.BlockSpec((tm, d), lambda i: (i, 0))
    vec = pl.BlockSpec((1, d), lambda i: (0, 0))
    ins = [x, w, dy] + ([resid] if has_res else [])
    return pl.pallas_call(
        body,
        name=name,
        grid=(t // tm,),
        in_specs=[row, vec, row] + ([row] if has_res else []),
        out_specs=[row, vec],
        out_shape=[jax.ShapeDtypeStruct((t, d), out_dtype), jax.ShapeDtypeStruct((1, d), F32)],
        compiler_params=_params(("arbitrary",)),
    )(*ins)


def _loss_head(y, target, *, name="loss_head"):
    t, d = y.shape
    tm = _pick(t, (512, 256, 128))

    def body(y_ref, t_ref, dy_ref, l_ref):
        err = y_ref[...] - t_ref[...]
        dy_ref[...] = err * (1.0 / d)
        part = jnp.sum(jnp.sum(err * err, axis=-1, keepdims=True), axis=0, keepdims=True) * (0.5 / d)
        part = jnp.broadcast_to(part, (1, LANE))

        @pl.when(pl.program_id(0) == 0)
        def _():
            l_ref[...] = part

        @pl.when(pl.program_id(0) > 0)
        def _():
            l_ref[...] += part

    row = pl.BlockSpec((tm, d), lambda i: (i, 0))
    dy, l = pl.pallas_call(
        body,
        name=name,
        grid=(t // tm,),
        in_specs=[row, row],
        out_specs=[row, pl.BlockSpec((1, LANE), lambda i: (0, 0))],
        out_shape=[jax.ShapeDtypeStruct((t, d), F32), jax.ShapeDtypeStruct((1, LANE), F32)],
        compiler_params=_params(("arbitrary",)),
    )(y, target)
    return l[0, 0], dy


def _conv_taps(h, w_ref, k_taps):
    out = h * w_ref[k_taps - 1:k_taps, :]
    for k in range(k_taps - 1):
        out = out + _shift_down(h, k_taps - 1 - k) * w_ref[k:k + 1, :]
    return out


def _conv_taps_bwd(h, dhc, w_ref, k_taps):
    dh = dhc * w_ref[k_taps - 1:k_taps, :]
    dws = []
    for k in range(k_taps - 1):
        up = _shift_up(dhc, k_taps - 1 - k)
        dh = dh + up * w_ref[k:k + 1, :]
        dws.append(jnp.sum(up * h, axis=0, keepdims=True))
    dws.append(jnp.sum(dhc * h, axis=0, keepdims=True))
    return dh, jnp.concatenate(dws, axis=0)


FFN_TC = 256


def _interleave(w, tc=FFN_TC):
    f = w.shape[-1] // 2
    tiles = []
    for j in range(f // tc):
        tiles += [w[..., j * tc:(j + 1) * tc], w[..., f + j * tc:f + (j + 1) * tc]]
    return jnp.concatenate(tiles, axis=-1)


def _deinterleave(w, tc=FFN_TC):
    n_tiles = w.shape[-1] // tc
    return jnp.concatenate([w[..., j * tc:(j + 1) * tc] for j in list(range(0, n_tiles, 2)) + list(range(1, n_tiles, 2))],
                           axis=-1)


FFN_ROWS = 512
HALO = 8


def _ffn_up_act(n, w_up, conv_w, conv_b, n_seq, *, name):
    t, d = n.shape
    f2 = w_up.shape[1]
    seq = t // n_seq
    tc = FFN_TC
    nj = f2 // (2 * tc)
    k_taps = conv_w.shape[0]
    rows = min(FFN_ROWS, seq)

    def body(n_ref, wu_ref, w_ref, b_ref, h_ref, hc_ref, o_ref, h_scr):
        h_scr[0:HALO, :] = jnp.zeros((HALO, 2 * tc), F32)
        wu = wu_ref[...]
        for r in range(seq // rows):
            chunk = slice(r * rows, (r + 1) * rows)
            h = jnp.dot(n_ref[chunk, :], wu, preferred_element_type=F32)
            h_scr[HALO + r * rows:HALO + (r + 1) * rows, :] = h
            h_ref[chunk, :] = h.astype(BF16)
            ext = h_scr[r * rows:HALO + (r + 1) * rows, :]
            hc = ext * w_ref[k_taps - 1:k_taps, :]
            for k in range(k_taps - 1):
                hc = hc + pltpu.roll(ext, k_taps - 1 - k, 0) * w_ref[k:k + 1, :]
            hc = hc[HALO:, :] + b_ref[...]
            hc_ref[chunk, :] = hc.astype(BF16)
            o_ref[chunk, :] = (_silu(hc[:, :tc]) * hc[:, tc:]).astype(BF16)

    wide = pl.BlockSpec((seq, 2 * tc), lambda b, j: (b, j))
    return pl.pallas_call(
        body,
        name=name,
        grid=(n_seq, nj),
        in_specs=[
            pl.BlockSpec((seq, d), lambda b, j: (b, 0)),
            pl.BlockSpec((d, 2 * tc), lambda b, j: (0, j)),
            pl.BlockSpec((k_taps, 2 * tc), lambda b, j: (0, j)),
            pl.BlockSpec((1, 2 * tc), lambda b, j: (0, j)),
        ],
        out_specs=[wide, wide, pl.BlockSpec((seq, tc), lambda b, j: (b, j))],
        out_shape=[jax.ShapeDtypeStruct((t, f2), BF16), jax.ShapeDtypeStruct((t, f2), BF16),
                   jax.ShapeDtypeStruct((t, f2 // 2), BF16)],
        scratch_shapes=[pltpu.VMEM((HALO + seq, 2 * tc), F32)],
        compiler_params=_params(("parallel", "arbitrary")),
    )(n, w_up, conv_w, conv_b)


def _ffn_down_bx_act_bwd(df, w_down, h, hc, conv_w, n_seq, *, name):
    t, d = df.shape
    f2 = h.shape[1]
    seq = t // n_seq
    tc = FFN_TC
    nj = f2 // (2 * tc)
    k_taps = conv_w.shape[0]

    def body(df_ref, wd_ref, h_ref, hc_ref, w_ref, dh_ref, dw_ref, db_ref):
        dav = lax.dot_general(df_ref[...], wd_ref[...], (((1,), (1,)), ((), ())), preferred_element_type=F32)
        hcv = hc_ref[...].astype(F32)
        gate, val = hcv[:, :tc], hcv[:, tc:]
        dhc = jnp.concatenate([dav * val * _dsilu(gate), dav * _silu(gate)], axis=1)
        dh, dw = _conv_taps_bwd(h_ref[...].astype(F32), dhc, w_ref, k_taps)
        dh_ref[...] = dh.astype(BF16)
        dw_ref[0] = dw
        db_ref[0] = jnp.sum(dhc, axis=0, keepdims=True)

    wide = pl.BlockSpec((seq, 2 * tc), lambda b, j: (b, j))
    dh, dw, db = pl.pallas_call(
        body,
        name=name,
        grid=(n_seq, nj),
        in_specs=[
            pl.BlockSpec((seq, d), lambda b, j: (b, 0)),
            pl.BlockSpec((tc, d), lambda b, j: (j, 0)),
            wide, wide,
            pl.BlockSpec((k_taps, 2 * tc), lambda b, j: (0, j)),
        ],
        out_specs=[
            wide,
            pl.BlockSpec((1, k_taps, 2 * tc), lambda b, j: (b, 0, j)),
            pl.BlockSpec((1, 1, 2 * tc), lambda b, j: (b, 0, j)),
        ],
        out_shape=[
            jax.ShapeDtypeStruct((t, f2), BF16),
            jax.ShapeDtypeStruct((n_seq, k_taps, f2), F32),
            jax.ShapeDtypeStruct((n_seq, 1, f2), F32),
        ],
        compiler_params=_params(("parallel", "arbitrary")),
    )(df, w_down, h, hc, conv_w)
    return dh, jnp.sum(dw, axis=0), jnp.sum(db, axis=0)


def _window_mixed(u, window):
    s = u
    step = 1
    while step < window:
        s = s + _shift_down(s, step)
        step *= 2
    rows = lax.broadcasted_iota(jnp.int32, u.shape, 0)
    inv_cnt = 1.0 / jnp.minimum(rows + 1, window).astype(F32)
    return s * inv_cnt - u, inv_cnt


def _window_mixed_bwd(dmixed, inv_cnt, window):
    r = dmixed * inv_cnt
    s = r
    step = 1
    while step < window:
        s = s + _shift_up(s, step)
        step *= 2
    return s - dmixed


def _pool_fwd(u, w, scale, n_seq, *, name):
    t, d = u.shape
    seq = t // n_seq
    n_g, dg, _ = w.shape

    def body(u_ref, w_ref, s_ref, o_ref):
        for k, window in enumerate(POOL_WINDOWS):
            @pl.when(pl.program_id(1) == k)
            def _(window=window):
                mixed, _ = _window_mixed(u_ref[...], window)
                pre = jnp.dot(mixed.astype(BF16), w_ref[0].astype(BF16), preferred_element_type=F32)
                o_ref[...] = pre * s_ref[...]

    return pl.pallas_call(
        body,
        name=name,
        grid=(n_seq, n_g),
        in_specs=[
            pl.BlockSpec((seq, dg), lambda b, g: (b, g)),
            pl.BlockSpec((1, dg, dg), lambda b, g: (g, 0, 0)),
            pl.BlockSpec((1, dg), lambda b, g: (0, g)),
        ],
        out_specs=pl.BlockSpec((seq, dg), lambda b, g: (b, g)),
        out_shape=jax.ShapeDtypeStruct((t, d), F32),
        compiler_params=_params(("parallel", "parallel")),
    )(u, w, scale)


def _pool_bwd(u, w, scale, dout, n_seq, *, name):
    t, d = u.shape
    seq = t // n_seq
    n_g, dg, _ = w.shape

    def body(u_ref, w_ref, s_ref, do_ref, du_ref, dw_ref, ds_ref):
        group = pl.program_id(0)
        first = pl.program_id(1) == 0
        for k, window in enumerate(POOL_WINDOWS):
            @pl.when(group == k)
            def _(window=window):
                mixed, inv_cnt = _window_mixed(u_ref[...], window)
                mixed_b = mixed.astype(BF16)
                w_b = w_ref[0].astype(BF16)
                dov = do_ref[...]
                pre = jnp.dot(mixed_b, w_b, preferred_element_type=F32)
                dsc = jnp.sum(dov * pre, axis=0, keepdims=True)
                dpre = (dov * s_ref[...]).astype(BF16)
                dw = lax.dot_general(mixed_b, dpre, (((0,), (0,)), ((), ())), preferred_element_type=F32)
                dmixed = lax.dot_general(dpre, w_b, (((1,), (1,)), ((), ())), preferred_element_type=F32)
                du_ref[...] = _window_mixed_bwd(dmixed, inv_cnt, window)

                @pl.when(first)
                def _():
                    dw_ref[0] = dw
                    ds_ref[...] = dsc

                @pl.when(jnp.logical_not(first))
                def _():
                    dw_ref[0] += dw
                    ds_ref[...] += dsc

    return pl.pallas_call(
        body,
        name=name,
        grid=(n_g, n_seq),
        in_specs=[
            pl.BlockSpec((seq, dg), lambda g, b: (b, g)),
            pl.BlockSpec((1, dg, dg), lambda g, b: (g, 0, 0)),
            pl.BlockSpec((1, dg), lambda g, b: (0, g)),
            pl.BlockSpec((seq, dg), lambda g, b: (b, g)),
        ],
        out_specs=[
            pl.BlockSpec((seq, dg), lambda g, b: (b, g)),
            pl.BlockSpec((1, dg, dg), lambda g, b: (g, 0, 0)),
            pl.BlockSpec((1, dg), lambda g, b: (0, g)),
        ],
        out_shape=[
            jax.ShapeDtypeStruct((t, d), F32),
            jax.ShapeDtypeStruct((n_g, dg, dg), F32),
            jax.ShapeDtypeStruct((1, d), F32),
        ],
        compiler_params=_params(("parallel", "arbitrary")),
    )(u, w, scale, dout)


def _adamw(w, g, m, v, *, name):
    shape = w.shape
    c = shape[-1]
    r = w.size // c
    tm = _pick(r, (512, 256, 128, 64, 32, 16, 8))

    def body(w_ref, g_ref, m_ref, v_ref, d_ref, nm_ref, nv_ref):
        gv = g_ref[...]
        nm = ADAM_B1 * m_ref[...] + (1.0 - ADAM_B1) * gv
        nv = ADAM_B2 * v_ref[...] + (1.0 - ADAM_B2) * (gv * gv)
        m_hat = nm / (1.0 - ADAM_B1 ** ADAM_STEP)
        v_hat = nv / (1.0 - ADAM_B2 ** ADAM_STEP)
        d_ref[...] = -ADAM_LR * (m_hat / (jnp.sqrt(v_hat) + ADAM_EPS) + ADAM_WD * w_ref[...])
        nm_ref[...] = nm
        nv_ref[...] = nv

    blk = pl.BlockSpec((tm, c), lambda i: (i, 0))
    out = jax.ShapeDtypeStruct((r, c), F32)
    res = pl.pallas_call(
        body,
        name=name,
        grid=(r // tm,),
        in_specs=[blk] * 4,
        out_specs=[blk] * 3,
        out_shape=[out] * 3,
        compiler_params=_params(("parallel",)),
    )(w.reshape(r, c), g.reshape(r, c), m.reshape(r, c), v.reshape(r, c))
    return tuple(a.reshape(shape) for a in res)


CONV_TC = 256


def _ssd_conv_fwd(proj, col0, n_cols, conv_w, conv_b, n_seq, *, name):
    t = proj.shape[0]
    seq = t // n_seq
    tc = CONV_TC
    off = col0 // tc
    k_taps = conv_w.shape[0]

    def body(h_ref, w_ref, b_ref, o_ref, pre_ref):
        pre = _conv_taps(h_ref[...], w_ref, k_taps) + b_ref[...]
        pre_ref[...] = pre.astype(BF16)
        o_ref[...] = _silu(pre)

    return pl.pallas_call(
        body,
        name=name,
        grid=(n_seq, n_cols // tc),
        in_specs=[
            pl.BlockSpec((seq, tc), lambda b, j: (b, j + off)),
            pl.BlockSpec((k_taps, tc), lambda b, j: (0, j)),
            pl.BlockSpec((1, tc), lambda b, j: (0, j)),
        ],
        out_specs=[pl.BlockSpec((seq, tc), lambda b, j: (b, j))] * 2,
        out_shape=[jax.ShapeDtypeStruct((t, n_cols), F32), jax.ShapeDtypeStruct((t, n_cols), BF16)],
        compiler_params=_params(("parallel", "parallel")),
    )(proj, conv_w, conv_b)


def _ssd_conv_bwd(proj, col0, conv_w, pre, dparts, dproj, n_seq, *, name):
    t = proj.shape[0]
    seq = t // n_seq
    tc = CONV_TC
    off = col0 // tc
    k_taps = conv_w.shape[0]
    widths = [d.shape[1] // tc for d in dparts]
    starts = [sum(widths[:i]) for i in range(len(widths))]
    n_blocks = sum(widths)
    n_parts = len(dparts)

    def body(h_ref, w_ref, pre_ref, *rest):
        part_refs = rest[:n_parts]
        dh_ref, dw_ref, db_ref = rest[n_parts + 1:]
        j = pl.program_id(0)
        da = part_refs[-1][...]
        for i in reversed(range(n_parts - 1)):
            da = jnp.where(j < starts[i + 1], part_refs[i][...], da)
        dhc = da * _dsilu(pre_ref[...].astype(F32))
        dh, dw = _conv_taps_bwd(h_ref[...], dhc, w_ref, k_taps)
        dh_ref[...] = dh.astype(BF16)
        db = jnp.sum(dhc, axis=0, keepdims=True)

        @pl.when(pl.program_id(1) == 0)
        def _():
            dw_ref[...] = dw
            db_ref[...] = db

        @pl.when(pl.program_id(1) > 0)
        def _():
            dw_ref[...] += dw
            db_ref[...] += db

    def part_spec(start, width):
        return pl.BlockSpec((seq, tc), lambda j, b: (b, jnp.clip(j - start, 0, width - 1)))

    n_cols = n_blocks * tc
    return pl.pallas_call(
        body,
        name=name,
        grid=(n_blocks, n_seq),
        in_specs=[
            pl.BlockSpec((seq, tc), lambda j, b: (b, j + off)),
            pl.BlockSpec((k_taps, tc), lambda j, b: (0, j)),
            pl.BlockSpec((seq, tc), lambda j, b: (b, j)),
        ] + [part_spec(st, wd) for st, wd in zip(starts, widths)] + [ANY],
        out_specs=[
            pl.BlockSpec((seq, tc), lambda j, b: (b, j + off)),
            pl.BlockSpec((k_taps, tc), lambda j, b: (0, j)),
            pl.BlockSpec((1, tc), lambda j, b: (0, j)),
        ],
        out_shape=[
            jax.ShapeDtypeStruct(dproj.shape, BF16),
            jax.ShapeDtypeStruct((k_taps, n_cols), F32),
            jax.ShapeDtypeStruct((1, n_cols), F32),
        ],
        input_output_aliases={3 + n_parts: 0},
        compiler_params=_params(("parallel", "arbitrary")),
    )(proj, conv_w, pre, *dparts, dproj)


def _fill_cols(buf, src, col0, *, name):
    t, c = src.shape
    tm = _pick(t, (1024, 512, 256, 128))

    def body(s_ref, b_ref, o_ref):
        o_ref[...] = s_ref[...].astype(o_ref.dtype)

    return pl.pallas_call(
        body,
        name=name,
        grid=(t // tm,),
        in_specs=[pl.BlockSpec((tm, c), lambda i: (i, 0)), ANY],
        out_specs=pl.BlockSpec((tm, c), lambda i: (i, col0 // c)),
        out_shape=jax.ShapeDtypeStruct(buf.shape, buf.dtype),
        input_output_aliases={1: 0},
        compiler_params=_params(("parallel",)),
    )(src, buf)


def _softplus(x):
    return jnp.maximum(x, 0.0) + jnp.log(1.0 + jnp.exp(-jnp.abs(x)))


def _chunk_decay(dtraw, bias, alog):
    q = dtraw.shape[0]
    dt = _softplus(dtraw + bias)
    a = -jnp.exp(alog)
    rows = lax.broadcasted_iota(jnp.int32, (q, q), 0)
    cols = lax.broadcasted_iota(jnp.int32, (q, q), 1)
    lower = rows >= cols
    acum = jnp.dot(lower.astype(F32), dt * a, precision=lax.Precision.HIGHEST, preferred_element_type=F32)
    return dt, a, acum, acum.T, lower


def _dot_exact(v, sel):
    hi = v.astype(BF16)
    r1 = v - hi.astype(F32)
    mid = r1.astype(BF16)
    lo = (r1 - mid.astype(F32)).astype(BF16)
    return (jnp.dot(hi, sel, preferred_element_type=F32) + jnp.dot(mid, sel, preferred_element_type=F32)
            + jnp.dot(lo, sel, preferred_element_type=F32))


def _head_selectors(gw, p):
    sum_heads = (lax.broadcasted_iota(jnp.int32, (gw, LANE), 0) // p == lax.broadcasted_iota(jnp.int32, (gw, LANE), 1))
    spread = (lax.broadcasted_iota(jnp.int32, (LANE, gw), 0) == lax.broadcasted_iota(jnp.int32, (LANE, gw), 1) // p)
    return sum_heads.astype(BF16), spread.astype(BF16)


def _row_spread(v, spread):
    return _dot_exact(jnp.broadcast_to(v, (8, v.shape[1])), spread)[0:1, :]


def _head_pad(v, r_heads):
    lead = v.shape[:-1]
    vg = v.reshape(lead + (N_SSD_GROUPS, r_heads))
    vg = jnp.pad(vg, [(0, 0)] * len(lead) + [(0, 0), (0, LANE - r_heads)])
    out = vg.reshape(lead + (N_SSD_GROUPS * LANE,))
    return out[None] if out.ndim == 1 else out


def _head_unpad(v, r_heads):
    lead = v.shape[:-1]
    out = v.reshape(lead + (N_SSD_GROUPS, LANE))[..., :r_heads].reshape(lead + (N_SSD_GROUPS * r_heads,))
    return out[0] if (len(lead) == 1 and lead[0] == 1) else out


def _ssd_w_in_layout(w_in, d_inner, d_xbc, r_heads):
    main = w_in[:, :d_inner + d_xbc]
    return jnp.concatenate([main, _head_pad(w_in[:, d_inner + d_xbc:], r_heads)], axis=1)


def _ssd_w_in_unlayout(w, d_inner, d_xbc, r_heads):
    main = w[:, :d_inner + d_xbc]
    return jnp.concatenate([main, _head_unpad(w[:, d_inner + d_xbc:], r_heads)], axis=1)


def _ssd_dims(proj, xbc):
    d_xbc = xbc.shape[1]
    d_inner = d_xbc - 2 * N_SSD_GROUPS * D_STATE
    gw = d_inner // N_SSD_GROUPS
    return d_inner, d_xbc, gw, gw // HEAD_DIM


def _ssd_fwd(proj, xbc, bias_p, alog_p, dskip_p, norm_w, n_seq, *, name):
    t = proj.shape[0]
    d_inner, d_xbc, gw, r_heads = _ssd_dims(proj, xbc)
    q, n, n_g, p = CHUNK, D_STATE, N_SSD_GROUPS, HEAD_DIM
    seq = t // n_seq
    nc = seq // q
    dt_blk0 = (d_inner + d_xbc) // LANE

    def body(x_ref, b_ref, c_ref, z_ref, dtr_ref, bias_ref, alog_ref, dsk_ref, nw_ref, yn_ref, y_ref, hs_ref, h_scr):
        @pl.when(pl.program_id(2) == 0)
        def _():
            h_scr[...] = jnp.zeros_like(h_scr)

        dt, a, acum, acum_t, lower = _chunk_decay(dtr_ref[...], bias_ref[...], alog_ref[...])
        x = x_ref[...]
        bb = b_ref[...].astype(BF16)
        cb = c_ref[...].astype(BF16)
        g_mat = lax.dot_general(cb, bb, (((1,), (1,)), ((), ())), preferred_element_type=F32)
        h_prev = h_scr[...]
        hs_ref[...] = h_prev
        c_h = jnp.dot(cb, h_prev.astype(BF16), preferred_element_type=F32)
        _, spread = _head_selectors(gw, p)
        acum_s = _dot_exact(acum, spread)
        a_last_s = acum_s[q - 1:q, :]
        xdt = x * _dot_exact(dt, spread)
        xdt_b = xdt.astype(BF16)
        ys = []
        for h in range(r_heads):
            decay = jnp.exp(jnp.where(lower, acum[:, h:h + 1] - acum_t[h:h + 1, :], -jnp.inf))
            ys.append(jnp.dot((g_mat * decay).astype(BF16), xdt_b[:, h * p:(h + 1) * p], preferred_element_type=F32))
        y = jnp.concatenate(ys, axis=1) + jnp.exp(acum_s) * c_h + _row_spread(dsk_ref[...], spread) * x
        xd = xdt * jnp.exp(a_last_s - acum_s)
        states = lax.dot_general(bb, xd.astype(BF16), (((0,), (0,)), ((), ())), preferred_element_type=F32)
        h_scr[...] = h_prev * jnp.exp(a_last_s) + states
        y_ref[...] = y
        gated = y * _silu(z_ref[...])
        rstd = lax.rsqrt(jnp.mean(gated * gated, axis=-1, keepdims=True) + EPS)
        yn_ref[...] = (gated * rstd * nw_ref[...]).astype(BF16)

    row = lambda b, g, c: b * nc + c
    vec = pl.BlockSpec((1, LANE), lambda b, g, c: (0, g))
    return pl.pallas_call(
        body,
        name=name,
        grid=(n_seq, n_g, nc),
        in_specs=[
            pl.BlockSpec((q, gw), lambda b, g, c: (row(b, g, c), g)),
            pl.BlockSpec((q, n), lambda b, g, c: (row(b, g, c), d_inner // n + g)),
            pl.BlockSpec((q, n), lambda b, g, c: (row(b, g, c), d_inner // n + n_g + g)),
            pl.BlockSpec((q, gw), lambda b, g, c: (row(b, g, c), g)),
            pl.BlockSpec((q, LANE), lambda b, g, c: (row(b, g, c), dt_blk0 + g)),
            vec, vec, vec,
            pl.BlockSpec((1, gw), lambda b, g, c: (0, g)),
        ],
        out_specs=[
            pl.BlockSpec((q, gw), lambda b, g, c: (row(b, g, c), g)),
            pl.BlockSpec((q, gw), lambda b, g, c: (row(b, g, c), g)),
            pl.BlockSpec((n, gw), lambda b, g, c: (row(b, g, c), g)),
        ],
        out_shape=[
            jax.ShapeDtypeStruct((t, d_inner), BF16),
            jax.ShapeDtypeStruct((t, d_inner), F32),
            jax.ShapeDtypeStruct((n_seq * nc * n, d_inner), F32),
        ],
        scratch_shapes=[pltpu.VMEM((n, gw), F32)],
        compiler_params=_params(("parallel", "parallel", "arbitrary")),
    )(xbc, xbc, xbc, proj, proj, bias_p, alog_p, dskip_p, norm_w)


def _ssd_bwd(proj, xbc, hs, y, dyn, bias_p, alog_p, dskip_p, norm_w, n_seq, *, name):
    t = proj.shape[0]
    d_inner, d_xbc, gw, r_heads = _ssd_dims(proj, xbc)
    q, n, n_g, p = CHUNK, D_STATE, N_SSD_GROUPS, HEAD_DIM
    seq = t // n_seq
    nc = seq // q
    dt_blk0 = (d_inner + d_xbc) // LANE

    def body(x_ref, b_ref, c_ref, z_ref, dtr_ref, bias_ref, alog_ref, dsk_ref, nw_ref, hs_ref, y_ref, dyn_ref,
             dx_ref, db_ref, dc_ref, dz_ref, ddtr_ref, dnw_ref, dbias_ref, dalog_ref, ddsk_ref, dh_scr):
        first = jnp.logical_and(pl.program_id(1) == 0, pl.program_id(2) == 0)

        @pl.when(pl.program_id(2) == 0)
        def _():
            dh_scr[...] = jnp.zeros_like(dh_scr)

        dtraw = dtr_ref[...]
        dt, a, acum, acum_t, lower = _chunk_decay(dtraw, bias_ref[...], alog_ref[...])
        x = x_ref[...]
        bb = b_ref[...].astype(BF16)
        cb = c_ref[...].astype(BF16)
        g_mat = lax.dot_general(cb, bb, (((1,), (1,)), ((), ())), preferred_element_type=F32)

        yv = y_ref[...]
        z = z_ref[...]
        sz = _silu(z)
        gated = yv * sz
        rstd = lax.rsqrt(jnp.mean(gated * gated, axis=-1, keepdims=True) + EPS)
        gn = gated * rstd
        dynv = dyn_ref[...]
        gwt = dynv * nw_ref[...]
        dgated = rstd * (gwt - gn * jnp.mean(gwt * gn, axis=-1, keepdims=True))
        dnw = jnp.sum(dynv * gn, axis=0, keepdims=True)
        dy = dgated * sz
        dz_ref[...] = (dgated * yv * _dsilu(z)).astype(BF16)

        h_prev = hs_ref[...]
        h_prev_b = h_prev.astype(BF16)
        ds = dh_scr[...]
        ds_b = ds.astype(BF16)
        sum_heads, spread = _head_selectors(gw, p)
        acum_s = _dot_exact(acum, spread)
        a_last_s = acum_s[q - 1:q, :]
        dt_s = _dot_exact(dt, spread)
        dsk_s = _row_spread(dsk_ref[...], spread)
        dte_s = jnp.exp(a_last_s - acum_s)
        cd_s = jnp.exp(a_last_s)
        xdt = x * dt_s
        xdt_b = xdt.astype(BF16)
        dy_b = dy.astype(BF16)
        gt_mat = lax.dot_general(bb, cb, (((1,), (1,)), ((), ())), preferred_element_type=F32)
        upper = lax.broadcasted_iota(jnp.int32, (q, q), 0) <= lax.broadcasted_iota(jnp.int32, (q, q), 1)
        dg = jnp.zeros((q, q), F32)
        dxdts, w_diffs = [], []
        for h in range(r_heads):
            hsl = slice(h * p, (h + 1) * p)
            diff = acum[:, h:h + 1] - acum_t[h:h + 1, :]
            decay = jnp.exp(jnp.where(lower, diff, -jnp.inf))
            decay_t = jnp.exp(jnp.where(upper, -diff, -jnp.inf))
            mt_mat = gt_mat * decay_t
            dm = lax.dot_general(dy_b[:, hsl], xdt_b[:, hsl], (((1,), (1,)), ((), ())), preferred_element_type=F32)
            dm_t = lax.dot_general(xdt_b[:, hsl], dy_b[:, hsl], (((1,), (1,)), ((), ())), preferred_element_type=F32)
            dg = dg + dm * decay
            dxdts.append(jnp.dot(mt_mat.astype(BF16), dy_b[:, hsl], preferred_element_type=F32))
            w_diffs.append(dm * (g_mat * decay) - dm_t * mt_mat)
        sel_q = (lax.broadcasted_iota(jnp.int32, (r_heads * q, LANE), 0) // q
                 == lax.broadcasted_iota(jnp.int32, (r_heads * q, LANE), 1)).astype(BF16)
        dacum_diag = _dot_exact(jnp.concatenate(w_diffs, axis=1), sel_q)
        c_h = jnp.dot(cb, h_prev_b, preferred_element_type=F32)
        dxd = jnp.dot(bb, ds_b, preferred_element_type=F32)
        dxdt = jnp.concatenate(dxdts, axis=1) + dxd * dte_s
        dye = dy * jnp.exp(acum_s)
        dye_b = dye.astype(BF16)
        xd = xdt * dte_s
        xd_b = xd.astype(BF16)
        dg_b = dg.astype(BF16)
        dx_ref[...] = dxdt * dt_s + dsk_s * dy
        dc_ref[...] = (jnp.dot(dg_b, bb, preferred_element_type=F32)
                       + lax.dot_general(dye_b, h_prev_b, (((1,), (1,)), ((), ())), preferred_element_type=F32))
        db_ref[...] = (lax.dot_general(dg_b, cb, (((0,), (0,)), ((), ())), preferred_element_type=F32)
                       + lax.dot_general(xd_b, ds_b, (((1,), (1,)), ((), ())), preferred_element_type=F32))
        dh_scr[...] = ds * cd_s + lax.dot_general(cb, dye_b, (((0,), (0,)), ((), ())), preferred_element_type=F32)
        ddt_cols = _dot_exact(x * dxdt, sum_heads)
        dacum_y = _dot_exact(dye * c_h - dxd * xd, sum_heads)
        col_sums = jnp.concatenate([
            jnp.sum(dxd * xd, axis=0, keepdims=True) + jnp.sum(ds * h_prev, axis=0, keepdims=True) * cd_s,
            jnp.sum(dy * x, axis=0, keepdims=True),
            jnp.zeros((6, gw), F32)], axis=0)
        col_sums = _dot_exact(col_sums, sum_heads)
        ddsk = col_sums[1:2, :]
        rows_q = lax.broadcasted_iota(jnp.int32, (q, LANE), 0)
        dacum = dacum_diag + dacum_y + jnp.where(rows_q == q - 1, col_sums[0:1, :], 0.0)
        dadt = jnp.dot(upper.astype(F32), dacum, precision=lax.Precision.HIGHEST, preferred_element_type=F32)
        ddt = dadt * a + ddt_cols
        ddtr = ddt * _sigmoid(dtraw + bias_ref[...])
        ddtr_ref[...] = ddtr
        dbias = jnp.sum(ddtr, axis=0, keepdims=True)
        dalog = jnp.sum(dadt * dt, axis=0, keepdims=True) * a

        @pl.when(first)
        def _():
            dnw_ref[...] = dnw
            dbias_ref[...] = dbias
            dalog_ref[...] = dalog
            ddsk_ref[...] = ddsk

        @pl.when(jnp.logical_not(first))
        def _():
            dnw_ref[...] += dnw
            dbias_ref[...] += dbias
            dalog_ref[...] += dalog
            ddsk_ref[...] += ddsk

    row = lambda g, b, c: b * nc + (nc - 1 - c)
    vec = pl.BlockSpec((1, LANE), lambda g, b, c: (0, g))
    wide = pl.BlockSpec((q, gw), lambda g, b, c: (row(g, b, c), g))
    narrow = pl.BlockSpec((q, n), lambda g, b, c: (row(g, b, c), g))
    return pl.pallas_call(
        body,
        name=name,
        grid=(n_g, n_seq, nc),
        in_specs=[
            wide,
            pl.BlockSpec((q, n), lambda g, b, c: (row(g, b, c), d_inner // n + g)),
            pl.BlockSpec((q, n), lambda g, b, c: (row(g, b, c), d_inner // n + n_g + g)),
            wide,
            pl.BlockSpec((q, LANE), lambda g, b, c: (row(g, b, c), dt_blk0 + g)),
            vec, vec, vec,
            pl.BlockSpec((1, gw), lambda g, b, c: (0, g)),
            pl.BlockSpec((n, gw), lambda g, b, c: (row(g, b, c), g)),
            wide, wide,
        ],
        out_specs=[
            wide, narrow, narrow, wide, narrow,
            pl.BlockSpec((1, gw), lambda g, b, c: (0, g)),
            vec, vec, vec,
        ],
        out_shape=[
            jax.ShapeDtypeStruct((t, d_inner), F32),
            jax.ShapeDtypeStruct((t, n_g * n), F32),
            jax.ShapeDtypeStruct((t, n_g * n), F32),
            jax.ShapeDtypeStruct(proj.shape, BF16),
            jax.ShapeDtypeStruct((t, n_g * LANE), F32),
            jax.ShapeDtypeStruct((1, d_inner), F32),
            jax.ShapeDtypeStruct((1, n_g * LANE), F32),
            jax.ShapeDtypeStruct((1, n_g * LANE), F32),
            jax.ShapeDtypeStruct((1, n_g * LANE), F32),
        ],
        scratch_shapes=[pltpu.VMEM((n, gw), F32)],
        compiler_params=_params(("parallel", "arbitrary", "arbitrary")),
    )(xbc, xbc, xbc, proj, proj, bias_p, alog_p, dskip_p, norm_w, hs, y, dyn)


MESH_IDS = pl.DeviceIdType.MESH


def _my_index():
    return 4 * lax.axis_index("x") + 2 * lax.axis_index("y") + lax.axis_index("c")


def _all_gather(shard, *, name):
    def body(x_ref, out_ref, send_sems, recv_sems, local_sem):
        x, y, c = lax.axis_index("x"), lax.axis_index("y"), lax.axis_index("c")
        me, sibling = (x, y, c), (x, y, 1 - c)
        chips = [(1 - x, y), (x, 1 - y), (1 - x, 1 - y)]

        def blk(px, py, pc):
            return out_ref.at[4 * px + 2 * py + pc]

        def copy(k, block, to, src=None):
            return pltpu.make_async_remote_copy(
                src_ref=blk(*block) if src is None else src, dst_ref=blk(*block),
                send_sem=send_sems.at[k], recv_sem=recv_sems.at[k], device_id=to, device_id_type=MESH_IDS)

        mine = pltpu.make_async_copy(x_ref, blk(*me), local_sem)
        mine.start()
        first = [copy(0, me, sibling, src=x_ref)]
        first += [copy(1 + j, me, (*chip, c), src=x_ref) for j, chip in enumerate(chips)]
        for cp in first:
            cp.start()
        passed = [copy(4 + j, (*chip, c), sibling) for j, chip in enumerate(chips)]
        for j, chip in enumerate(chips):
            copy(1 + j, (*chip, c), me).wait_recv()
            passed[j].start()
        copy(0, sibling, me).wait_recv()
        for j, chip in enumerate(chips):
            copy(4 + j, (*chip, 1 - c), me).wait_recv()
        for cp in first + passed:
            cp.wait_send()
        mine.wait()

    return pl.pallas_call(
        body,
        name=name,
        in_specs=[ANY],
        out_specs=ANY,
        out_shape=jax.ShapeDtypeStruct((N_DEV,) + shard.shape, shard.dtype),
        scratch_shapes=[pltpu.SemaphoreType.DMA((7,)), pltpu.SemaphoreType.DMA((7,)), pltpu.SemaphoreType.DMA],
    )(shard)


HBM_SPEC = pl.BlockSpec(memory_space=pltpu.HBM)
SEM_SPEC = pl.BlockSpec(memory_space=pltpu.SEMAPHORE)
SPLIT_COPY_PARAMS = pltpu.CompilerParams(has_side_effects=pltpu.SideEffectType.DATAFLOW_SIDE_EFFECTING)


def _peer_list():
    x, y, c = lax.axis_index("x"), lax.axis_index("y"), lax.axis_index("c")
    peers = []
    for k in range(1, N_DEV):
        px = 1 - x if k & 4 else x
        py = 1 - y if k & 2 else y
        pc = 1 - c if k & 1 else c
        peers.append(((px, py, pc), 4 * px + 2 * py + pc))
    return 4 * x + 2 * y + c, peers


def _push_copies(src_refs, land_refs, send_sems, recv_sems, blockwise):
    me, peers = _peer_list()
    copies = []
    for a, (src_ref, land_ref) in enumerate(zip(src_refs, land_refs)):
        for k, (dev, idx) in enumerate(peers):
            sem = a * (N_DEV - 1) + k
            src = src_ref.at[idx] if blockwise else src_ref
            copies.append(tuple(
                pltpu.make_async_remote_copy(src_ref=src, dst_ref=land_ref.at[slot], send_sem=send_sems.at[sem],
                                             recv_sem=recv_sems.at[sem], device_id=dev, device_id_type=MESH_IDS)
                for slot in (me, idx)))
    return copies


def _push_start(srcs, blockwise, after, *, name):
    n = len(srcs)
    blocks = [s_.shape[1:] if blockwise else s_.shape for s_ in srcs]

    def body(*refs):
        src_refs, land_refs = refs[:n], refs[n:2 * n]
        send_sems, recv_sems = refs[2 * n + 1], refs[2 * n + 2]
        token = refs[-1]
        for send, _ in _push_copies(src_refs, land_refs, send_sems, recv_sems, blockwise):
            send.start()
        token[...] = jnp.zeros_like(token)

    n_sem = n * (N_DEV - 1)
    lands = [lax.empty((N_DEV,) + b, s_.dtype) for b, s_ in zip(blocks, srcs)]
    out = pl.pallas_call(
        body,
        name=name,
        in_specs=[HBM_SPEC] * (2 * n) + [ANY],
        out_specs=(SEM_SPEC, SEM_SPEC) + (HBM_SPEC,) * (2 * n) + (pl.BlockSpec(memory_space=pltpu.VMEM),),
        out_shape=(pltpu.SemaphoreType.DMA((n_sem,)), pltpu.SemaphoreType.DMA((n_sem,)))
        + tuple(pltpu.HBM(a.shape, a.dtype) for a in list(srcs) + lands)
        + (jax.ShapeDtypeStruct((8, LANE), F32),),
        input_output_aliases={i: 2 + i for i in range(2 * n)},
        compiler_params=SPLIT_COPY_PARAMS,
    )(*[pltpu.with_memory_space_constraint(a, pltpu.HBM) for a in list(srcs) + lands], after)
    return out[0], out[1], out[2:2 + n], out[2 + n:2 + 2 * n], out[-1]


def _push_wait(send_sems, recv_sems, srcs, lands, blockwise, after, *, name):
    n = len(srcs)

    def body(*refs):
        src_refs, land_refs = refs[:n], refs[n:2 * n]
        send_sems, recv_sems = refs[2 * n], refs[2 * n + 1]
        for send, recv in _push_copies(src_refs, land_refs, send_sems, recv_sems, blockwise):
            send.wait_send()
            recv.wait_recv()

    out = pl.pallas_call(
        body,
        name=name,
        in_specs=[HBM_SPEC] * (2 * n) + [SEM_SPEC, SEM_SPEC, ANY],
        out_specs=(HBM_SPEC,) * (2 * n),
        out_shape=tuple(pltpu.HBM(a.shape, a.dtype) for a in list(srcs) + list(lands)),
        input_output_aliases={i: i for i in range(2 * n)},
        compiler_params=SPLIT_COPY_PARAMS,
    )(*srcs, *lands, send_sems, recv_sems, after)
    return out[n:]


def _with_own_slot(landing, own):
    slot = lax.broadcasted_iota(jnp.int32, (N_DEV,) + (1,) * own.ndim, 0)
    return jnp.where(slot == _my_index(), own[None], landing)


def _sum_slots(parts, own=None, *, name):
    shape = parts.shape[1:]
    n, c = parts.shape[0], parts.shape[-1]
    r = parts.size // (n * c)
    tm = _pick(r, (256, 128, 64, 32, 16, 8))

    def body(p_ref, *rest):
        o_ref = rest[-1]
        me = _my_index()

        def slot(s):
            if own is None:
                return p_ref[s].astype(F32)
            return jnp.where(me == s, rest[0][...], p_ref[s]).astype(F32)

        acc = slot(0)
        for s in range(1, n):
            acc = acc + slot(s)
        o_ref[...] = acc

    tile = pl.BlockSpec((tm, c), lambda i: (i, 0))
    return pl.pallas_call(
        body,
        name=name,
        grid=(r // tm,),
        in_specs=[pl.BlockSpec((n, tm, c), lambda i: (0, i, 0))] + ([] if own is None else [tile]),
        out_specs=tile,
        out_shape=jax.ShapeDtypeStruct((r, c), F32),
        compiler_params=_params(("parallel",)),
    )(parts.reshape(n, r, c), *([] if own is None else [own.reshape(r, c)])).reshape(shape)


def _row_count(shape):
    c = shape[-1]
    rows = 1
    for s in shape[:-1]:
        rows *= s
    return rows, c, c + (-c) % LANE


PACK_ROWS = 256


def _pack_rows(arrays):
    pieces = []
    for a in arrays:
        rows, c, cp = _row_count(a.shape)
        a2 = a.reshape(rows, c)
        if cp > c:
            a2 = jnp.pad(a2, ((0, 0), (0, cp - c)))
        a2 = a2.reshape(rows * cp // LANE, LANE)
        if a2.shape[0] % 8:
            a2 = jnp.pad(a2, ((0, 8 - a2.shape[0] % 8), (0, 0)))
        pieces.append(a2)
    total = sum(p.shape[0] for p in pieces)
    if total % PACK_ROWS:
        pieces.append(jnp.zeros((PACK_ROWS - total % PACK_ROWS, LANE), F32))
    return jnp.concatenate(pieces, axis=0)


def _unpack_rows(packed, shapes, lead=()):
    out, off = [], 0
    for shp in shapes:
        rows, c, cp = _row_count(shp)
        n_rows = rows * cp // LANE
        seg = packed[..., off:off + n_rows, :].reshape(lead + (rows, cp))
        out.append(seg[..., :c].reshape(lead + tuple(shp)))
        off += n_rows + (-n_rows) % 8
    return out


def _unshard(stacked, axis):
    if axis == stacked.ndim - 2:
        return jnp.concatenate([stacked[d] for d in range(N_DEV)], axis=axis)
    moved = jnp.moveaxis(stacked, 0, axis)
    shp = moved.shape
    return moved.reshape(shp[:axis] + (shp[axis] * shp[axis + 1],) + shp[axis + 2:])


def _shard_major(full, axis):
    shp = full.shape
    if axis == full.ndim - 1:
        size = shp[axis] // N_DEV
        return jnp.stack([full[..., d * size:(d + 1) * size] for d in range(N_DEV)])
    split = full.reshape(shp[:axis] + (N_DEV, shp[axis] // N_DEV) + shp[axis + 1:])
    return jnp.moveaxis(split, axis, 0)


def _my_shard(full, axis):
    size = full.shape[axis] // N_DEV
    return lax.dynamic_slice_in_dim(full, _my_index() * size, size, axis)


def _local_step(x, target, w, fetch, emit, n_seq):
    def with_token(vec, token):
        return vec + jnp.tile(token[0:1, :], (1, vec.shape[1] // LANE))

    depth, d_model = w["norm_mix_pre"].shape
    d_inner = w["ssd_norm_w"].shape[1]
    d_xbc = w["ssd_conv_w"].shape[2]
    saved = []
    for i in range(depth):
        j = i // 2
        m, token = fetch(i, "mix", x)
        mix_pre_w = with_token(w["norm_mix_pre"][i:i + 1], token)
        s = {"x": x, "mix_pre_w": mix_pre_w}
        if i % 2 == 0:
            u = _rms_fwd(x, mix_pre_w, out_dtype=BF16, name=f"l{i}_mix_pre")
            proj = _mm(u, m["ssd_w_in"], name=f"l{i}_ssd_in")
            xbc, xbc_pre = _ssd_conv_fwd(proj, d_inner, d_xbc, w["ssd_conv_w"][j], w["ssd_conv_b"][j:j + 1], n_seq,
                                         name=f"l{i}_ssd_conv")
            yn, y, hs = _ssd_fwd(proj, xbc, w["ssd_dt_bias"][j:j + 1], w["ssd_a_log"][j:j + 1], w["ssd_d"][j:j + 1],
                                 w["ssd_norm_w"][j:j + 1], n_seq, name=f"l{i}_ssd_scan")
            m_out, token = fetch(i, "out", yn)
            m = {**m, **m_out}
            mix = _mm(yn, m["ssd_w_out"], name=f"l{i}_ssd_out")
            s.update(u=u, proj=proj, xbc=xbc, xbc_pre=xbc_pre, yn=yn, y=y, hs=hs)
        else:
            u = _rms_fwd(x, mix_pre_w, out_dtype=F32, name=f"l{i}_mix_pre")
            mix = _pool_fwd(u, m["pool_w"], w["pool_scale"][j:j + 1], n_seq, name=f"l{i}_pool")
            s.update(u=u)
            token = jnp.zeros((8, LANE), F32)
        x1 = _res_rms_fwd(x, mix, with_token(w["norm_mix_post"][i:i + 1], token), name=f"l{i}_mix_post")
        m_ffn, token = fetch(i, "ffn", x1)
        m = {**m, **m_ffn}
        ffn_pre_w = with_token(w["norm_ffn_pre"][i:i + 1], token)
        n = _rms_fwd(x1, ffn_pre_w, out_dtype=BF16, name=f"l{i}_ffn_pre")
        h, hc, a = _ffn_up_act(n, m["ffn_w_up"], w["ffn_conv_w"][i], w["ffn_conv_b"][i:i + 1], n_seq,
                               name=f"l{i}_ffn_up_act")
        f = _mm(a, m["ffn_w_down"], name=f"l{i}_ffn_down")
        x = _res_rms_fwd(x1, f, w["norm_ffn_post"][i:i + 1], name=f"l{i}_ffn_post")
        s.update(mix=mix, x1=x1, n=n, h=h, hc=hc, a=a, f=f, m=m, ffn_pre_w=ffn_pre_w)
        saved.append(s)

    loss, dx = _loss_head(x, target)
    grads = {k: [None] * len(w[k]) for k in SMALL}
    token = jnp.zeros((8, LANE), F32)
    for i in reversed(range(depth)):
        j = i // 2
        s = saved[i]
        m, gm = s["m"], {}
        df, grads["norm_ffn_post"][i] = _rms_bwd(s["f"], with_token(w["norm_ffn_post"][i:i + 1], token), dx, None,
                                                 out_dtype=BF16, name=f"l{i}_ffn_post_b")
        gm["ffn_w_down"] = _mm(s["a"], df, ta=True, name=f"l{i}_ffn_down_bw")
        dh, grads["ffn_conv_w"][i], grads["ffn_conv_b"][i] = _ffn_down_bx_act_bwd(
            df, m["ffn_w_down"], s["h"], s["hc"], w["ffn_conv_w"][i], n_seq, name=f"l{i}_ffn_act_b")
        dn = _mm(dh, m["ffn_w_up"], tb=True, name=f"l{i}_ffn_up_bx")
        gm["ffn_w_up"] = _mm(s["n"], dh, ta=True, name=f"l{i}_ffn_up_bw")
        dx1, grads["norm_ffn_pre"][i] = _rms_bwd(s["x1"], s["ffn_pre_w"], dn, dx, name=f"l{i}_ffn_pre_b")
        token = emit(i, "ffn", gm, dx1)
        gm = {}
        dmix, grads["norm_mix_post"][i] = _rms_bwd(s["mix"], with_token(w["norm_mix_post"][i:i + 1], token), dx1, None,
                                                   out_dtype=BF16 if i % 2 == 0 else F32, name=f"l{i}_mix_post_b")
        if i % 2 == 0:
            dyn = _mm(dmix, m["ssd_w_out"], tb=True, name=f"l{i}_ssd_out_bx")
            gm["ssd_w_out"] = _mm(s["yn"], dmix, ta=True, name=f"l{i}_ssd_out_bw")
            token = emit(i, "out", gm, dyn)
            gm = {}
            dxs, db, dc, dz, ddtr, dnw, dbias, dalog, ddsk = _ssd_bwd(
                s["proj"], s["xbc"], s["hs"], s["y"], dyn, w["ssd_dt_bias"][j:j + 1], w["ssd_a_log"][j:j + 1],
                w["ssd_d"][j:j + 1], with_token(w["ssd_norm_w"][j:j + 1], token), n_seq, name=f"l{i}_ssd_scan_b")
            grads["ssd_norm_w"][j], grads["ssd_dt_bias"][j], grads["ssd_a_log"][j], grads["ssd_d"][j] = (
                dnw, dbias, dalog, ddsk)
            dproj, grads["ssd_conv_w"][j], grads["ssd_conv_b"][j] = _ssd_conv_bwd(
                s["proj"], d_inner, w["ssd_conv_w"][j], s["xbc_pre"], (dxs, db, dc), dz, n_seq,
                name=f"l{i}_ssd_conv_b")
            dproj = _fill_cols(dproj, ddtr, d_inner + d_xbc, name=f"l{i}_ssd_dt_b")
            gm["ssd_w_in"] = _mm(s["u"], dproj, ta=True, name=f"l{i}_ssd_in_bw")
            token = emit(i, "mix", gm, dproj)
            du = _mm(dproj, m["ssd_w_in"], tb=True, name=f"l{i}_ssd_in_bx")
        else:
            du, gm["pool_w"], grads["pool_scale"][j] = _pool_bwd(
                s["u"], m["pool_w"], w["pool_scale"][j:j + 1], dmix, n_seq, name=f"l{i}_pool_b")
            token = emit(i, "mix", gm, du)
        dx, grads["norm_mix_pre"][i] = _rms_bwd(s["x"], with_token(s["mix_pre_w"], token), du, dx1,
                                                name=f"l{i}_mix_pre_b")
    return loss, dx, grads


BIG = (("ssd_w_in", 2), ("ssd_w_out", 1), ("pool_w", 2), ("ffn_w_up", 2), ("ffn_w_down", 1))
SMALL_SHARDED = (("ssd_conv_w", 2), ("ffn_conv_w", 2), ("pool_scale", 1))
SMALL = ("ssd_conv_w", "ssd_conv_b", "ssd_dt_bias", "ssd_a_log", "ssd_d", "ssd_norm_w", "pool_scale", "ffn_conv_w",
         "ffn_conv_b", "norm_mix_pre", "norm_mix_post", "norm_ffn_pre", "norm_ffn_post")
WEIGHTS = ("ssd_w_in", "ssd_conv_w", "ssd_conv_b", "ssd_dt_bias", "ssd_a_log", "ssd_d", "ssd_norm_w", "ssd_w_out",
           "pool_w", "pool_scale", "ffn_w_up", "ffn_conv_w", "ffn_conv_b", "ffn_w_down", "norm_mix_pre",
           "norm_mix_post", "norm_ffn_pre", "norm_ffn_post")


def _ssd_sizes(d_inner):
    return d_inner + 2 * N_SSD_GROUPS * D_STATE, d_inner // HEAD_DIM // N_SSD_GROUPS


def _small_compute_layout(full, d_inner):
    _, r_heads = _ssd_sizes(d_inner)
    w = {k: full[k] for k in SMALL}
    for k in ("ssd_dt_bias", "ssd_a_log", "ssd_d"):
        w[k] = _head_pad(full[k], r_heads)
    for k in ("ffn_conv_w", "ffn_conv_b"):
        w[k] = _interleave(full[k])
    return w


def _matmul_compute_layout(k, full, d_inner):
    d_xbc, r_heads = _ssd_sizes(d_inner)
    if k == "ssd_w_in":
        return _ssd_w_in_layout(full, d_inner, d_xbc, r_heads)
    if k == "ffn_w_up":
        return _interleave(full)
    return full


def _layer_matrices(i, part):
    if part == "ffn":
        return (("ffn_w_up", 1, i), ("ffn_w_down", 0, i))
    if i % 2 == 1:
        return (("pool_w", 1, i // 2),) if part == "mix" else ()
    return (("ssd_w_in", 1, i // 2),) if part == "mix" else (("ssd_w_out", 0, i // 2),)


def _fetch_group(i, part):
    mix, out, ffn = (_layer_matrices(i, p) for p in ("mix", "out", "ffn"))
    if i % 2 == 1:
        return mix + ffn if part == "mix" else ()
    if i == 0:
        return {"mix": mix, "out": out + ffn, "ffn": ()}[part]
    return {"mix": mix + out, "out": (), "ffn": ffn}[part]


def _matmul_grad_reference_layout(k, g, d_inner):
    d_xbc, r_heads = _ssd_sizes(d_inner)
    if k == "ssd_w_in":
        return _ssd_w_in_unlayout(g, d_inner, d_xbc, r_heads)
    if k == "ffn_w_up":
        return _deinterleave(g)
    return g


def _small_grads_reference_layout(grads, shapes, d_inner):
    _, r_heads = _ssd_sizes(d_inner)
    g = {k: jnp.stack(grads[k]) for k in SMALL}
    for k in ("ssd_dt_bias", "ssd_a_log", "ssd_d"):
        g[k] = _head_unpad(g[k][:, 0], r_heads)
    for k in ("ffn_conv_w", "ffn_conv_b"):
        g[k] = _deinterleave(g[k])
    return {k: v.reshape(shapes[k]) for k, v in g.items()}


def kernel(x, ssd_w_in, ssd_conv_w, ssd_conv_b, ssd_dt_bias, ssd_a_log, ssd_d, ssd_norm_w, ssd_w_out, pool_w, pool_scale, ffn_w_up, ffn_conv_w, ffn_conv_b, ffn_w_down, norm_mix_pre, norm_mix_post, norm_ffn_pre, norm_ffn_post, loss_target, m_ssd_w_in, m_ssd_conv_w, m_ssd_conv_b, m_ssd_dt_bias, m_ssd_a_log, m_ssd_d, m_ssd_norm_w, m_ssd_w_out, m_pool_w, m_pool_scale, m_ffn_w_up, m_ffn_conv_w, m_ffn_conv_b, m_ffn_w_down, m_norm_mix_pre, m_norm_mix_post, m_norm_ffn_pre, m_norm_ffn_post, v_ssd_w_in, v_ssd_conv_w, v_ssd_conv_b, v_ssd_dt_bias, v_ssd_a_log, v_ssd_d, v_ssd_norm_w, v_ssd_w_out, v_pool_w, v_pool_scale, v_ffn_w_up, v_ffn_conv_w, v_ffn_conv_b, v_ffn_w_down, v_norm_mix_pre, v_norm_mix_post, v_norm_ffn_pre, v_norm_ffn_post):
    shards = dict(ssd_w_in=ssd_w_in, ssd_conv_w=ssd_conv_w, ssd_conv_b=ssd_conv_b, ssd_dt_bias=ssd_dt_bias,
                  ssd_a_log=ssd_a_log, ssd_d=ssd_d, ssd_norm_w=ssd_norm_w, ssd_w_out=ssd_w_out, pool_w=pool_w,
                  pool_scale=pool_scale, ffn_w_up=ffn_w_up, ffn_conv_w=ffn_conv_w, ffn_conv_b=ffn_conv_b,
                  ffn_w_down=ffn_w_down, norm_mix_pre=norm_mix_pre, norm_mix_post=norm_mix_post,
                  norm_ffn_pre=norm_ffn_pre, norm_ffn_post=norm_ffn_post)
    moments_m = dict(zip(WEIGHTS, (m_ssd_w_in, m_ssd_conv_w, m_ssd_conv_b, m_ssd_dt_bias, m_ssd_a_log, m_ssd_d, m_ssd_norm_w, m_ssd_w_out, m_pool_w, m_pool_scale, m_ffn_w_up, m_ffn_conv_w, m_ffn_conv_b, m_ffn_w_down, m_norm_mix_pre, m_norm_mix_post, m_norm_ffn_pre, m_norm_ffn_post)))
    moments_v = dict(zip(WEIGHTS, (v_ssd_w_in, v_ssd_conv_w, v_ssd_conv_b, v_ssd_dt_bias, v_ssd_a_log, v_ssd_d, v_ssd_norm_w, v_ssd_w_out, v_pool_w, v_pool_scale, v_ffn_w_up, v_ffn_conv_w, v_ffn_conv_b, v_ffn_w_down, v_norm_mix_pre, v_norm_mix_post, v_norm_ffn_pre, v_norm_ffn_post)))
    n_seq, seq, d_model = x.shape
    t = n_seq * seq

    d_inner = ssd_norm_w.shape[1]
    depth = norm_mix_pre.shape[0]
    x2 = x.reshape(t, d_model)

    shard16 = {k: shards[k].astype(BF16) for k, _ in BIG}
    order = [(i, part) for i in range(depth) for part in ("mix", "out", "ffn") if _fetch_group(i, part)]
    fetches = {}

    def start_fetch(key, after):
        srcs = [shard16[k][l] for k, _, l in _fetch_group(*key)]
        fetches[key] = _push_start(srcs, False, after, name=f"fetch{key[0]}{key[1]}_start")

    full = dict(shards)
    small_all = _all_gather(_pack_rows([shards[k] for k, _ in SMALL_SHARDED]), name="gather_small_weights")
    small_stacked = _unpack_rows(small_all, [shards[k].shape for k, _ in SMALL_SHARDED], lead=(N_DEV,))
    for (k, axis), st in zip(SMALL_SHARDED, small_stacked):
        full[k] = _unshard(st, axis)
    w = _small_compute_layout(full, d_inner)
    ready = {}

    def fetch(i, part, x_now):
        key = (i, part)
        token = jnp.zeros((8, LANE), F32)
        if key in order:
            if key == order[0]:
                wholes = [_unshard(_all_gather(shard16[k][l], name=f"fetch0_{k}"), axis) for k, axis, l in _fetch_group(i, part)]
                nxt_after = wholes[0]
            else:
                send, recv, srcs, lands, _ = fetches[key]
                lands = _push_wait(send, recv, srcs, lands, False, x_now, name=f"fetch{i}{part}_wait")
                wholes = [_unshard(_with_own_slot(land, shard16[k][l]), axis)
                          for (k, axis, l), land in zip(_fetch_group(i, part), lands)]
                nxt_after = lands[0]
            for (k, _, l), whole in zip(_fetch_group(i, part), wholes):
                ready[k, l] = _matmul_compute_layout(k, whole, d_inner)
            nxt = order.index(key) + 1
            if nxt < len(order):
                start_fetch(order[nxt], nxt_after)
                token = fetches[order[nxt]][4]
        return {k: ready[k, l] for k, _, l in _layer_matrices(i, part)}, token

    g_layers = {}
    in_flight = []

    def finish_exchange(after):
        key, blocks, (send, recv, srcs, lands, _) = in_flight.pop(0)
        lands = _push_wait(send, recv, srcs, lands, True, after, name=f"exchange{key[0]}{key[1]}_wait")
        for (k, _, l), land, block in zip(_layer_matrices(*key), lands, blocks):
            own = lax.dynamic_index_in_dim(block, _my_index(), 0, keepdims=False)
            g_layers[k, l] = _sum_slots(land, own, name=f"sum{key[0]}_{k}")

    def emit(i, part, gm, dx_now):
        if len(in_flight) >= 2:
            finish_exchange(dx_now)
        blocks = [_shard_major(_matmul_grad_reference_layout(k, gm[k].astype(BF16), d_inner), axis)
                  for k, axis, _ in _layer_matrices(i, part)]
        started = _push_start(blocks, True, dx_now, name=f"exchange{i}{part}_start")
        in_flight.append(((i, part), blocks, started))
        return started[4]

    loss, dx, grads = _local_step(x2, loss_target.reshape(t, d_model), w, fetch, emit, n_seq)
    loss = lax.psum(loss, ("x", "y", "c"))

    g_shard = {}
    small_shapes = {k: full[k].shape for k in SMALL}
    g_small = _small_grads_reference_layout(grads, small_shapes, d_inner)
    s_all = _all_gather(_pack_rows([g_small[k] for k in SMALL]) + in_flight[-1][2][4][0:1, :], name="gather_small_grads")
    for k, g in zip(SMALL, _unpack_rows(_sum_slots(s_all, name="sum_small_grads"), [small_shapes[k] for k in SMALL])):
        g_shard[k] = g
    for k, axis in SMALL_SHARDED:
        g_shard[k] = _my_shard(g_shard[k], axis)

    last = [k for key, _, _ in in_flight for k, _, _ in _layer_matrices(*key)]
    deltas, new_m, new_v = {}, {}, {}
    for k in [k for k in WEIGHTS if k not in last] + last:
        if k == last[0]:
            while in_flight:
                finish_exchange(deltas["ffn_w_up"])
        if k in dict(BIG):
            g_shard[k] = jnp.stack([g_layers[k, l] for l in range(shards[k].shape[0])])
        deltas[k], new_m[k], new_v[k] = _adamw(shards[k], g_shard[k], moments_m[k], moments_v[k], name=f"adamw_{k}")
    return (loss, dx.reshape(x.shape), *[g_shard[k] for k in WEIGHTS], *[deltas[k] for k in WEIGHTS],
            *[new_m[k] for k in WEIGHTS], *[new_v[k] for k in WEIGHTS])
```

```python
import functools

import jax
import jax.numpy as jnp
from jax import lax
from jax.experimental import pallas as pl
from jax.experimental.pallas import tpu as pltpu

F32 = jnp.float32
BF16 = jnp.bfloat16

N_DEV = 8
HEAD_DIM = 64
N_SSD_GROUPS = 4
D_STATE = 128
CHUNK = 128
POOL_WINDOWS = (2, 4, 8, 16)
EPS = 1e-6
LANE = 128
ADAM_LR = 0.001
ADAM_B1 = 0.9
ADAM_B2 = 0.999
ADAM_EPS = 1e-08
ADAM_WD = 0.01
ADAM_STEP = 10
VMEM_LIMIT = 56 * 1024 * 1024
ANY = pl.BlockSpec(memory_space=pl.ANY)


def _pick(n, cands):
    for c in cands:
        if n % c == 0:
            return c
    return n


def _params(sem):
    return pltpu.CompilerParams(dimension_semantics=sem, vmem_limit_bytes=VMEM_LIMIT)


def _sigmoid(x):
    return 0.5 * jnp.tanh(0.5 * x) + 0.5


def _silu(x):
    return x * _sigmoid(x)


def _dsilu(x):
    s = _sigmoid(x)
    return s * (1.0 + x * (1.0 - s))


def _shift_down(x, s):
    rows = lax.broadcasted_iota(jnp.int32, x.shape, 0)
    return jnp.where(rows >= s, pltpu.roll(x, s, 0), 0.0)


def _shift_up(x, s):
    n = x.shape[0]
    rows = lax.broadcasted_iota(jnp.int32, x.shape, 0)
    return jnp.where(rows < n - s, pltpu.roll(x, n - s, 0), 0.0)


MM_VMEM_BUDGET = 40 * 1024 * 1024
MM_STEP_BYTES = 1_300_000
MM_SUB = 512


def _mm_tiles(m, n, k, a_bytes, b_bytes, o_bytes):
    def cands(dim, sizes):
        out = [s for s in sizes if s <= dim and dim % s == 0]
        return out or [dim]

    best = None
    for tm in cands(m, (m, m // 2, 2048, 1024, 512, 256, 128)):
        if tm % LANE:
            continue
        for tn in cands(n, (n, n // 2, n // 4, 2048, 1024, 512, 256, 128)):
            if tn % (2 * LANE) and tn != n:
                continue
            for tk in cands(k, (k, k // 2, 2048, 1024, 512)):
                if tk % LANE:
                    continue
                nk = k // tk
                acc = tm * tn * 4 if (nk > 1 and o_bytes != 4) else 0
                temps = tm * min(tn, MM_SUB) * 4 + (tm * tk * 2 if a_bytes == 4 else 0) + (tk * tn * 2 if b_bytes == 4 else 0)
                vmem = 2 * (tm * tk * a_bytes + tk * tn * b_bytes + tm * tn * o_bytes) + acc + temps
                if vmem > MM_VMEM_BUDGET:
                    continue
                steps = (m // tm) * (n // tn) * nk
                cost = (m * k * a_bytes * (n // tn) + k * n * b_bytes * (m // tm) + m * n * o_bytes
                        + steps * MM_STEP_BYTES)
                if best is None or cost < best[0]:
                    best = (cost, tm, tn, tk)
    return best[1:]


def _mm(a, b, *, ta=False, tb=False, out_dtype=F32, name="mm"):
    m, k = (a.shape[1], a.shape[0]) if ta else a.shape
    n = b.shape[0] if tb else b.shape[1]
    o_bytes = jnp.dtype(out_dtype).itemsize
    tm, tn, tk = _mm_tiles(m, n, k, a.dtype.itemsize, b.dtype.itemsize, o_bytes)
    nk = k // tk
    sub = _pick(tn, (MM_SUB, 256))
    use_acc = nk > 1 and o_bytes != 4
    a_spec = pl.BlockSpec((tk, tm), lambda i, j, kk: (kk, i)) if ta else pl.BlockSpec((tm, tk), lambda i, j, kk: (i, kk))
    b_spec = pl.BlockSpec((tn, tk), lambda i, j, kk: (j, kk)) if tb else pl.BlockSpec((tk, tn), lambda i, j, kk: (kk, j))
    dims = (((1,), (1 if tb else 0,)), ((), ()))

    def body(a_ref, b_ref, o_ref, *scratch):
        kk = pl.program_id(2)
        acc_ref = scratch[0] if use_acc else o_ref
        if nk > 1:
            @pl.when(kk == 0)
            def _():
                acc_ref[...] = jnp.zeros_like(acc_ref)

        av = a_ref[...].astype(BF16)
        if ta:
            av = av.T
        for s in range(tn // sub):
            cols = slice(s * sub, (s + 1) * sub)
            bv = (b_ref[cols, :] if tb else b_ref[:, cols]).astype(BF16)
            part = lax.dot_general(av, bv, dims, preferred_element_type=F32)
            if nk == 1:
                o_ref[:, cols] = part.astype(out_dtype)
            else:
                acc_ref[:, cols] += part
        if use_acc:
            @pl.when(kk == nk - 1)
            def _():
                o_ref[...] = acc_ref[...].astype(out_dtype)

    return pl.pallas_call(
        body,
        name=name,
        grid=(m // tm, n // tn, nk),
        in_specs=[a_spec, b_spec],
        out_specs=pl.BlockSpec((tm, tn), lambda i, j, kk: (i, j)),
        out_shape=jax.ShapeDtypeStruct((m, n), out_dtype),
        scratch_shapes=[pltpu.VMEM((tm, tn), F32)] if use_acc else [],
        compiler_params=_params(("parallel", "parallel", "arbitrary")),
    )(a, b)


def _rms_fwd(x, w, *, out_dtype, name):
    t, d = x.shape
    tm = _pick(t, (512, 256, 128))

    def body(x_ref, w_ref, o_ref):
        xv = x_ref[...]
        rstd = lax.rsqrt(jnp.mean(xv * xv, axis=-1, keepdims=True) + EPS)
        o_ref[...] = (xv * rstd * w_ref[...]).astype(out_dtype)

    return pl.pallas_call(
        body,
        name=name,
        grid=(t // tm,),
        in_specs=[pl.BlockSpec((tm, d), lambda i: (i, 0)), pl.BlockSpec((1, d), lambda i: (0, 0))],
        out_specs=pl.BlockSpec((tm, d), lambda i: (i, 0)),
        out_shape=jax.ShapeDtypeStruct((t, d), out_dtype),
        compiler_params=_params(("parallel",)),
    )(x, w)


def _res_rms_fwd(x, f, w, *, name):
    t, d = x.shape
    tm = _pick(t, (512, 256, 128))

    def body(x_ref, f_ref, w_ref, o_ref):
        fv = f_ref[...]
        rstd = lax.rsqrt(jnp.mean(fv * fv, axis=-1, keepdims=True) + EPS)
        o_ref[...] = x_ref[...] + fv * rstd * w_ref[...]

    row = pl.BlockSpec((tm, d), lambda i: (i, 0))
    return pl.pallas_call(
        body,
        name=name,
        grid=(t // tm,),
        in_specs=[row, row, pl.BlockSpec((1, d), lambda i: (0, 0))],
        out_specs=row,
        out_shape=jax.ShapeDtypeStruct((t, d), F32),
        compiler_params=_params(("parallel",)),
    )(x, f, w)


def _rms_bwd(x, w, dy, resid, *, out_dtype=F32, name):
    t, d = x.shape
    tm = _pick(t, (512, 256, 128))
    has_res = resid is not None

    def body(*refs):
        if has_res:
            x_ref, w_ref, dy_ref, r_ref, dx_ref, dw_ref = refs
        else:
            x_ref, w_ref, dy_ref, dx_ref, dw_ref = refs
        xv = x_ref[...]
        dyv = dy_ref[...].astype(F32)
        rstd = lax.rsqrt(jnp.mean(xv * xv, axis=-1, keepdims=True) + EPS)
        xn = xv * rstd
        g = dyv * w_ref[...]
        dx = rstd * (g - xn * jnp.mean(g * xn, axis=-1, keepdims=True))
        if has_res:
            dx = dx + r_ref[...]
        dx_ref[...] = dx.astype(out_dtype)
        part = jnp.sum(dyv * xn, axis=0, keepdims=True)

        @pl.when(pl.program_id(0) == 0)
        def _():
            dw_ref[...] = part

        @pl.when(pl.program_id(0) > 0)
        def _():
            dw_ref[...] += part

    row = pl.BlockSpec((tm, d), lambda i: (i, 0))
    vec = pl.BlockSpec((1, d), lambda i: (0, 0))
    ins = [x, w, dy] + ([resid] if has_res else [])
    return pl.pallas_call(
        body,
        name=name,
        grid=(t // tm,),
        in_specs=[row, vec, row] + ([row] if has_res else []),
        out_specs=[row, vec],
        out_shape=[jax.ShapeDtypeStruct((t, d), out_dtype), jax.ShapeDtypeStruct((1, d), F32)],
        compiler_params=_params(("arbitrary",)),
    )(*ins)


def _loss_head(y, target, *, name="loss_head"):
    t, d = y.shape
    tm = _pick(t, (512, 256, 128))

    def body(y_ref, t_ref, dy_ref, l_ref):
        err = y_ref[...] - t_ref[...]
        dy_ref[...] = err * (1.0 / d)
        part = jnp.sum(jnp.sum(err * err, axis=-1, keepdims=True), axis=0, keepdims=True) * (0.5 / d)
        part = jnp.broadcast_to(part, (1, LANE))

        @pl.when(pl.program_id(0) == 0)
        def _():
            l_ref[...] = part

        @pl.when(pl.program_id(0) > 0)
        def _():
            l_ref[...] += part

    row = pl.BlockSpec((tm, d), lambda i: (i, 0))
    dy, l = pl.pallas_call(
        body,
        name=name,
        grid=(t // tm,),
        in_specs=[row, row],
        out_specs=[row, pl.BlockSpec((1, LANE), lambda i: (0, 0))],
        out_shape=[jax.ShapeDtypeStruct((t, d), F32), jax.ShapeDtypeStruct((1, LANE), F32)],
        compiler_params=_params(("arbitrary",)),
    )(y, target)
    return l[0, 0], dy


def _conv_taps(h, w_ref, k_taps):
    out = h * w_ref[k_taps - 1:k_taps, :]
    for k in range(k_taps - 1):
        out = out + _shift_down(h, k_taps - 1 - k) * w_ref[k:k + 1, :]
    return out


def _conv_taps_bwd(h, dhc, w_ref, k_taps):
    dh = dhc * w_ref[k_taps - 1:k_taps, :]
    dws = []
    for k in range(k_taps - 1):
        up = _shift_up(dhc, k_taps - 1 - k)
        dh = dh + up * w_ref[k:k + 1, :]
        dws.append(jnp.sum(up * h, axis=0, keepdims=True))
    dws.append(jnp.sum(dhc * h, axis=0, keepdims=True))
    return dh, jnp.concatenate(dws, axis=0)


FFN_TC = 256


def _interleave(w, tc=FFN_TC):
    f = w.shape[-1] // 2
    tiles = []
    for j in range(f // tc):
        tiles += [w[..., j * tc:(j + 1) * tc], w[..., f + j * tc:f + (j + 1) * tc]]
    return jnp.concatenate(tiles, axis=-1)


def _deinterleave(w, tc=FFN_TC):
    n_tiles = w.shape[-1] // tc
    return jnp.concatenate([w[..., j * tc:(j + 1) * tc] for j in list(range(0, n_tiles, 2)) + list(range(1, n_tiles, 2))],
                           axis=-1)


FFN_ROWS = 512
HALO = 8


def _ffn_up_act(n, w_up, conv_w, conv_b, n_seq, *, name):
    t, d = n.shape
    f2 = w_up.shape[1]
    seq = t // n_seq
    tc = FFN_TC
    nj = f2 // (2 * tc)
    k_taps = conv_w.shape[0]
    rows = min(FFN_ROWS, seq)

    def body(n_ref, wu_ref, w_ref, b_ref, h_ref, hc_ref, o_ref, h_scr):
        h_scr[0:HALO, :] = jnp.zeros((HALO, 2 * tc), F32)
        wu = wu_ref[...]
        for r in range(seq // rows):
            chunk = slice(r * rows, (r + 1) * rows)
            h = jnp.dot(n_ref[chunk, :], wu, preferred_element_type=F32)
            h_scr[HALO + r * rows:HALO + (r + 1) * rows, :] = h
            h_ref[chunk, :] = h.astype(BF16)
            ext = h_scr[r * rows:HALO + (r + 1) * rows, :]
            hc = ext * w_ref[k_taps - 1:k_taps, :]
            for k in range(k_taps - 1):
                hc = hc + pltpu.roll(ext, k_taps - 1 - k, 0) * w_ref[k:k + 1, :]
            hc = hc[HALO:, :] + b_ref[...]
            hc_ref[chunk, :] = hc.astype(BF16)
            o_ref[chunk, :] = (_silu(hc[:, :tc]) * hc[:, tc:]).astype(BF16)

    wide = pl.BlockSpec((seq, 2 * tc), lambda b, j: (b, j))
    return pl.pallas_call(
        body,
        name=name,
        grid=(n_seq, nj),
        in_specs=[
            pl.BlockSpec((seq, d), lambda b, j: (b, 0)),
            pl.BlockSpec((d, 2 * tc), lambda b, j: (0, j)),
            pl.BlockSpec((k_taps, 2 * tc), lambda b, j: (0, j)),
            pl.BlockSpec((1, 2 * tc), lambda b, j: (0, j)),
        ],
        out_specs=[wide, wide, pl.BlockSpec((seq, tc), lambda b, j: (b, j))],
        out_shape=[jax.ShapeDtypeStruct((t, f2), BF16), jax.ShapeDtypeStruct((t, f2), BF16),
                   jax.ShapeDtypeStruct((t, f2 // 2), BF16)],
        scratch_shapes=[pltpu.VMEM((HALO + seq, 2 * tc), F32)],
        compiler_params=_params(("parallel", "arbitrary")),
    )(n, w_up, conv_w, conv_b)


def _ffn_down_bx_act_bwd(df, w_down, h, hc, conv_w, n_seq, *, name):
    t, d = df.shape
    f2 = h.shape[1]
    seq = t // n_seq
    tc = FFN_TC
    nj = f2 // (2 * tc)
    k_taps = conv_w.shape[0]

    def body(df_ref, wd_ref, h_ref, hc_ref, w_ref, dh_ref, dw_ref, db_ref):
        dav = lax.dot_general(df_ref[...], wd_ref[...], (((1,), (1,)), ((), ())), preferred_element_type=F32)
        hcv = hc_ref[...].astype(F32)
        gate, val = hcv[:, :tc], hcv[:, tc:]
        dhc = jnp.concatenate([dav * val * _dsilu(gate), dav * _silu(gate)], axis=1)
        dh, dw = _conv_taps_bwd(h_ref[...].astype(F32), dhc, w_ref, k_taps)
        dh_ref[...] = dh.astype(BF16)
        dw_ref[0] = dw
        db_ref[0] = jnp.sum(dhc, axis=0, keepdims=True)

    wide = pl.BlockSpec((seq, 2 * tc), lambda b, j: (b, j))
    dh, dw, db = pl.pallas_call(
        body,
        name=name,
        grid=(n_seq, nj),
        in_specs=[
            pl.BlockSpec((seq, d), lambda b, j: (b, 0)),
            pl.BlockSpec((tc, d), lambda b, j: (j, 0)),
            wide, wide,
            pl.BlockSpec((k_taps, 2 * tc), lambda b, j: (0, j)),
        ],
        out_specs=[
            wide,
            pl.BlockSpec((1, k_taps, 2 * tc), lambda b, j: (b, 0, j)),
            pl.BlockSpec((1, 1, 2 * tc), lambda b, j: (b, 0, j)),
        ],
        out_shape=[
            jax.ShapeDtypeStruct((t, f2), BF16),
            jax.ShapeDtypeStruct((n_seq, k_taps, f2), F32),
            jax.ShapeDtypeStruct((n_seq, 1, f2), F32),
        ],
        compiler_params=_params(("parallel", "arbitrary")),
    )(df, w_down, h, hc, conv_w)
    return dh, jnp.sum(dw, axis=0), jnp.sum(db, axis=0)


def _window_mixed(u, window):
    s = u
    step = 1
    while step < window:
        s = s + _shift_down(s, step)
        step *= 2
    rows = lax.broadcasted_iota(jnp.int32, u.shape, 0)
    inv_cnt = 1.0 / jnp.minimum(rows + 1, window).astype(F32)
    return s * inv_cnt - u, inv_cnt


def _window_mixed_bwd(dmixed, inv_cnt, window):
    r = dmixed * inv_cnt
    s = r
    step = 1
    while step < window:
        s = s + _shift_up(s, step)
        step *= 2
    return s - dmixed


def _pool_fwd(u, w, scale, n_seq, *, name):
    t, d = u.shape
    seq = t // n_seq
    n_g, dg, _ = w.shape

    def body(u_ref, w_ref, s_ref, o_ref):
        for k, window in enumerate(POOL_WINDOWS):
            @pl.when(pl.program_id(1) == k)
            def _(window=window):
                mixed, _ = _window_mixed(u_ref[...], window)
                pre = jnp.dot(mixed.astype(BF16), w_ref[0].astype(BF16), preferred_element_type=F32)
                o_ref[...] = pre * s_ref[...]

    return pl.pallas_call(
        body,
        name=name,
        grid=(n_seq, n_g),
        in_specs=[
            pl.BlockSpec((seq, dg), lambda b, g: (b, g)),
            pl.BlockSpec((1, dg, dg), lambda b, g: (g, 0, 0)),
            pl.BlockSpec((1, dg), lambda b, g: (0, g)),
        ],
        out_specs=pl.BlockSpec((seq, dg), lambda b, g: (b, g)),
        out_shape=jax.ShapeDtypeStruct((t, d), F32),
        compiler_params=_params(("parallel", "parallel")),
    )(u, w, scale)


def _pool_bwd(u, w, scale, dout, n_seq, *, name):
    t, d = u.shape
    seq = t // n_seq
    n_g, dg, _ = w.shape

    def body(u_ref, w_ref, s_ref, do_ref, du_ref, dw_ref, ds_ref):
        group = pl.program_id(0)
        first = pl.program_id(1) == 0
        for k, window in enumerate(POOL_WINDOWS):
            @pl.when(group == k)
            def _(window=window):
                mixed, inv_cnt = _window_mixed(u_ref[...], window)
                mixed_b = mixed.astype(BF16)
                w_b = w_ref[0].astype(BF16)
                dov = do_ref[...]
                pre = jnp.dot(mixed_b, w_b, preferred_element_type=F32)
                dsc = jnp.sum(dov * pre, axis=0, keepdims=True)
                dpre = (dov * s_ref[...]).astype(BF16)
                dw = lax.dot_general(mixed_b, dpre, (((0,), (0,)), ((), ())), preferred_element_type=F32)
                dmixed = lax.dot_general(dpre, w_b, (((1,), (1,)), ((), ())), preferred_element_type=F32)
                du_ref[...] = _window_mixed_bwd(dmixed, inv_cnt, window)

                @pl.when(first)
                def _():
                    dw_ref[0] = dw
                    ds_ref[...] = dsc

                @pl.when(jnp.logical_not(first))
                def _():
                    dw_ref[0] += dw
                    ds_ref[...] += dsc

    return pl.pallas_call(
        body,
        name=name,
        grid=(n_g, n_seq),
        in_specs=[
            pl.BlockSpec((seq, dg), lambda g, b: (b, g)),
            pl.BlockSpec((1, dg, dg), lambda g, b: (g, 0, 0)),
            pl.BlockSpec((1, dg), lambda g, b: (0, g)),
            pl.BlockSpec((seq, dg), lambda g, b: (b, g)),
        ],
        out_specs=[
            pl.BlockSpec((seq, dg), lambda g, b: (b, g)),
            pl.BlockSpec((1, dg, dg), lambda g, b: (g, 0, 0)),
            pl.BlockSpec((1, dg), lambda g, b: (0, g)),
        ],
        out_shape=[
            jax.ShapeDtypeStruct((t, d), F32),
            jax.ShapeDtypeStruct((n_g, dg, dg), F32),
            jax.ShapeDtypeStruct((1, d), F32),
        ],
        compiler_params=_params(("parallel", "arbitrary")),
    )(u, w, scale, dout)


def _adamw(w, g, m, v, *, name):
    shape = w.shape
    c = shape[-1]
    r = w.size // c
    tm = _pick(r, (512, 256, 128, 64, 32, 16, 8))

    def body(w_ref, g_ref, m_ref, v_ref, d_ref, nm_ref, nv_ref):
        gv = g_ref[...]
        nm = ADAM_B1 * m_ref[...] + (1.0 - ADAM_B1) * gv
        nv = ADAM_B2 * v_ref[...] + (1.0 - ADAM_B2) * (gv * gv)
        m_hat = nm / (1.0 - ADAM_B1 ** ADAM_STEP)
        v_hat = nv / (1.0 - ADAM_B2 ** ADAM_STEP)
        d_ref[...] = -ADAM_LR * (m_hat / (jnp.sqrt(v_hat) + ADAM_EPS) + ADAM_WD * w_ref[...])
        nm_ref[...] = nm
        nv_ref[...] = nv

    blk = pl.BlockSpec((tm, c), lambda i: (i, 0))
    out = jax.ShapeDtypeStruct((r, c), F32)
    res = pl.pallas_call(
        body,
        name=name,
        grid=(r // tm,),
        in_specs=[blk] * 4,
        out_specs=[blk] * 3,
        out_shape=[out] * 3,
        compiler_params=_params(("parallel",)),
    )(w.reshape(r, c), g.reshape(r, c), m.reshape(r, c), v.reshape(r, c))
    return tuple(a.reshape(shape) for a in res)


CONV_TC = 256


def _ssd_conv_fwd(proj, col0, n_cols, conv_w, conv_b, n_seq, *, name):
    t = proj.shape[0]
    seq = t // n_seq
    tc = CONV_TC
    off = col0 // tc
    k_taps = conv_w.shape[0]

    def body(h_ref, w_ref, b_ref, o_ref, pre_ref):
        pre = _conv_taps(h_ref[...], w_ref, k_taps) + b_ref[...]
        pre_ref[...] = pre.astype(BF16)
        o_ref[...] = _silu(pre)

    return pl.pallas_call(
        body,
        name=name,
        grid=(n_seq, n_cols // tc),
        in_specs=[
            pl.BlockSpec((seq, tc), lambda b, j: (b, j + off)),
            pl.BlockSpec((k_taps, tc), lambda b, j: (0, j)),
            pl.BlockSpec((1, tc), lambda b, j: (0, j)),
        ],
        out_specs=[pl.BlockSpec((seq, tc), lambda b, j: (b, j))] * 2,
        out_shape=[jax.ShapeDtypeStruct((t, n_cols), F32), jax.ShapeDtypeStruct((t, n_cols), BF16)],
        compiler_params=_params(("parallel", "parallel")),
    )(proj, conv_w, conv_b)


def _ssd_conv_bwd(proj, col0, conv_w, pre, dparts, dproj, n_seq, *, name):
    t = proj.shape[0]
    seq = t // n_seq
    tc = CONV_TC
    off = col0 // tc
    k_taps = conv_w.shape[0]
    widths = [d.shape[1] // tc for d in dparts]
    starts = [sum(widths[:i]) for i in range(len(widths))]
    n_blocks = sum(widths)
    n_parts = len(dparts)

    def body(h_ref, w_ref, pre_ref, *rest):
        part_refs = rest[:n_parts]
        dh_ref, dw_ref, db_ref = rest[n_parts + 1:]
        j = pl.program_id(0)
        da = part_refs[-1][...]
        for i in reversed(range(n_parts - 1)):
            da = jnp.where(j < starts[i + 1], part_refs[i][...], da)
        dhc = da * _dsilu(pre_ref[...].astype(F32))
        dh, dw = _conv_taps_bwd(h_ref[...], dhc, w_ref, k_taps)
        dh_ref[...] = dh.astype(BF16)
        db = jnp.sum(dhc, axis=0, keepdims=True)

        @pl.when(pl.program_id(1) == 0)
        def _():
            dw_ref[...] = dw
            db_ref[...] = db

        @pl.when(pl.program_id(1) > 0)
        def _():
            dw_ref[...] += dw
            db_ref[...] += db

    def part_spec(start, width):
        return pl.BlockSpec((seq, tc), lambda j, b: (b, jnp.clip(j - start, 0, width - 1)))

    n_cols = n_blocks * tc
    return pl.pallas_call(
        body,
        name=name,
        grid=(n_blocks, n_seq),
        in_specs=[
            pl.BlockSpec((seq, tc), lambda j, b: (b, j + off)),
            pl.BlockSpec((k_taps, tc), lambda j, b: (0, j)),
            pl.BlockSpec((seq, tc), lambda j, b: (b, j)),
        ] + [part_spec(st, wd) for st, wd in zip(starts, widths)] + [ANY],
        out_specs=[
            pl.BlockSpec((seq, tc), lambda j, b: (b, j + off)),
            pl.BlockSpec((k_taps, tc), lambda j, b: (0, j)),
            pl.BlockSpec((1, tc), lambda j, b: (0, j)),
        ],
        out_shape=[
            jax.ShapeDtypeStruct(dproj.shape, BF16),
            jax.ShapeDtypeStruct((k_taps, n_cols), F32),
            jax.ShapeDtypeStruct((1, n_cols), F32),
        ],
        input_output_aliases={3 + n_parts: 0},
        compiler_params=_params(("parallel", "arbitrary")),
    )(proj, conv_w, pre, *dparts, dproj)


def _fill_cols(buf, src, col0, *, name):
    t, c = src.shape
    tm = _pick(t, (1024, 512, 256, 128))

    def body(s_ref, b_ref, o_ref):
        o_ref[...] = s_ref[...].astype(o_ref.dtype)

    return pl.pallas_call(
        body,
        name=name,
        grid=(t // tm,),
        in_specs=[pl.BlockSpec((tm, c), lambda i: (i, 0)), ANY],
        out_specs=pl.BlockSpec((tm, c), lambda i: (i, col0 // c)),
        out_shape=jax.ShapeDtypeStruct(buf.shape, buf.dtype),
        input_output_aliases={1: 0},
        compiler_params=_params(("parallel",)),
    )(src, buf)


def _softplus(x):
    return jnp.maximum(x, 0.0) + jnp.log(1.0 + jnp.exp(-jnp.abs(x)))


def _chunk_decay(dtraw, bias, alog):
    q = dtraw.shape[0]
    dt = _softplus(dtraw + bias)
    a = -jnp.exp(alog)
    rows = lax.broadcasted_iota(jnp.int32, (q, q), 0)
    cols = lax.broadcasted_iota(jnp.int32, (q, q), 1)
    lower = rows >= cols
    acum = jnp.dot(lower.astype(F32), dt * a, precision=lax.Precision.HIGHEST, preferred_element_type=F32)
    return dt, a, acum, acum.T, lower


def _dot_exact(v, sel):
    hi = v.astype(BF16)
    r1 = v - hi.astype(F32)
    mid = r1.astype(BF16)
    lo = (r1 - mid.astype(F32)).astype(BF16)
    return (jnp.dot(hi, sel, preferred_element_type=F32) + jnp.dot(mid, sel, preferred_element_type=F32)
            + jnp.dot(lo, sel, preferred_element_type=F32))


def _head_selectors(gw, p):
    sum_heads = (lax.broadcasted_iota(jnp.int32, (gw, LANE), 0) // p == lax.broadcasted_iota(jnp.int32, (gw, LANE), 1))
    spread = (lax.broadcasted_iota(jnp.int32, (LANE, gw), 0) == lax.broadcasted_iota(jnp.int32, (LANE, gw), 1) // p)
    return sum_heads.astype(BF16), spread.astype(BF16)


def _row_spread(v, spread):
    return _dot_exact(jnp.broadcast_to(v, (8, v.shape[1])), spread)[0:1, :]


def _head_pad(v, r_heads):
    lead = v.shape[:-1]
    vg = v.reshape(lead + (N_SSD_GROUPS, r_heads))
    vg = jnp.pad(vg, [(0, 0)] * len(lead) + [(0, 0), (0, LANE - r_heads)])
    out = vg.reshape(lead + (N_SSD_GROUPS * LANE,))
    return out[None] if out.ndim == 1 else out


def _head_unpad(v, r_heads):
    lead = v.shape[:-1]
    out = v.reshape(lead + (N_SSD_GROUPS, LANE))[..., :r_heads].reshape(lead + (N_SSD_GROUPS * r_heads,))
    return out[0] if (len(lead) == 1 and lead[0] == 1) else out


def _ssd_w_in_layout(w_in, d_inner, d_xbc, r_heads):
    main = w_in[:, :d_inner + d_xbc]
    return jnp.concatenate([main, _head_pad(w_in[:, d_inner + d_xbc:], r_heads)], axis=1)


def _ssd_w_in_unlayout(w, d_inner, d_xbc, r_heads):
    main = w[:, :d_inner + d_xbc]
    return jnp.concatenate([main, _head_unpad(w[:, d_inner + d_xbc:], r_heads)], axis=1)


SSD_CHUNKS = 2


def _ssd_dims(proj, xbc):
    d_xbc = xbc.shape[1]
    d_inner = d_xbc - 2 * N_SSD_GROUPS * D_STATE
    gw = d_inner // N_SSD_GROUPS
    return d_inner, d_xbc, gw, gw // HEAD_DIM


def _ssd_fwd(proj, xbc, bias_p, alog_p, dskip_p, norm_w, n_seq, *, name):
    t = proj.shape[0]
    d_inner, d_xbc, gw, r_heads = _ssd_dims(proj, xbc)
    q, n, n_g, p = CHUNK, D_STATE, N_SSD_GROUPS, HEAD_DIM
    seq = t // n_seq
    nc = seq // q
    cps = SSD_CHUNKS if nc % SSD_CHUNKS == 0 else 1
    dt_blk0 = (d_inner + d_xbc) // LANE

    def body(x_ref, b_ref, c_ref, z_ref, dtr_ref, bias_ref, alog_ref, dsk_ref, nw_ref, yn_ref, y_ref, hs_ref, h_scr):
        @pl.when(pl.program_id(2) == 0)
        def _():
            h_scr[...] = jnp.zeros_like(h_scr)

        for cc in range(cps):
            rows = pl.ds(cc * q, q)
            chunk(x_ref.at[rows, :], b_ref.at[rows, :], c_ref.at[rows, :], z_ref.at[rows, :], dtr_ref.at[rows, :],
                  bias_ref, alog_ref, dsk_ref, nw_ref, yn_ref.at[rows, :], y_ref.at[rows, :],
                  hs_ref.at[pl.ds(cc * n, n), :], h_scr)

    def chunk(x_ref, b_ref, c_ref, z_ref, dtr_ref, bias_ref, alog_ref, dsk_ref, nw_ref, yn_ref, y_ref, hs_ref, h_scr):
        dt, a, acum, acum_t, lower = _chunk_decay(dtr_ref[...], bias_ref[...], alog_ref[...])
        x = x_ref[...]
        bb = b_ref[...].astype(BF16)
        cb = c_ref[...].astype(BF16)
        g_mat = lax.dot_general(cb, bb, (((1,), (1,)), ((), ())), preferred_element_type=F32)
        h_prev = h_scr[...]
        hs_ref[...] = h_prev
        c_h = jnp.dot(cb, h_prev.astype(BF16), preferred_element_type=F32)
        _, spread = _head_selectors(gw, p)
        acum_s = _dot_exact(acum, spread)
        a_last_s = acum_s[q - 1:q, :]
        xdt = x * _dot_exact(dt, spread)
        xdt_b = xdt.astype(BF16)
        ys = []
        for h in range(r_heads):
            decay = jnp.exp(jnp.where(lower, acum[:, h:h + 1] - acum_t[h:h + 1, :], -jnp.inf))
            ys.append(jnp.dot((g_mat * decay).astype(BF16), xdt_b[:, h * p:(h + 1) * p], preferred_element_type=F32))
        y = jnp.concatenate(ys, axis=1) + jnp.exp(acum_s) * c_h + _row_spread(dsk_ref[...], spread) * x
        xd = xdt * jnp.exp(a_last_s - acum_s)
        states = lax.dot_general(bb, xd.astype(BF16), (((0,), (0,)), ((), ())), preferred_element_type=F32)
        h_scr[...] = h_prev * jnp.exp(a_last_s) + states
        y_ref[...] = y
        gated = y * _silu(z_ref[...])
        rstd = lax.rsqrt(jnp.mean(gated * gated, axis=-1, keepdims=True) + EPS)
        yn_ref[...] = (gated * rstd * nw_ref[...]).astype(BF16)

    row = lambda b, g, c: b * (nc // cps) + c
    vec = pl.BlockSpec((1, LANE), lambda b, g, c: (0, g))
    return pl.pallas_call(
        body,
        name=name,
        grid=(n_seq, n_g, nc // cps),
        in_specs=[
            pl.BlockSpec((q * cps, gw), lambda b, g, c: (row(b, g, c), g)),
            pl.BlockSpec((q * cps, n), lambda b, g, c: (row(b, g, c), d_inner // n + g)),
            pl.BlockSpec((q * cps, n), lambda b, g, c: (row(b, g, c), d_inner // n + n_g + g)),
            pl.BlockSpec((q * cps, gw), lambda b, g, c: (row(b, g, c), g)),
            pl.BlockSpec((q * cps, LANE), lambda b, g, c: (row(b, g, c), dt_blk0 + g)),
            vec, vec, vec,
            pl.BlockSpec((1, gw), lambda b, g, c: (0, g)),
        ],
        out_specs=[
            pl.BlockSpec((q * cps, gw), lambda b, g, c: (row(b, g, c), g)),
            pl.BlockSpec((q * cps, gw), lambda b, g, c: (row(b, g, c), g)),
            pl.BlockSpec((n * cps, gw), lambda b, g, c: (row(b, g, c), g)),
        ],
        out_shape=[
            jax.ShapeDtypeStruct((t, d_inner), BF16),
            jax.ShapeDtypeStruct((t, d_inner), F32),
            jax.ShapeDtypeStruct((n_seq * nc * n, d_inner), F32),
        ],
        scratch_shapes=[pltpu.VMEM((n, gw), F32)],
        compiler_params=_params(("parallel", "parallel", "arbitrary")),
    )(xbc, xbc, xbc, proj, proj, bias_p, alog_p, dskip_p, norm_w)


def _ssd_bwd(proj, xbc, hs, y, dyn, bias_p, alog_p, dskip_p, norm_w, n_seq, *, name):
    t = proj.shape[0]
    d_inner, d_xbc, gw, r_heads = _ssd_dims(proj, xbc)
    q, n, n_g, p = CHUNK, D_STATE, N_SSD_GROUPS, HEAD_DIM
    seq = t // n_seq
    nc = seq // q
    cps = SSD_CHUNKS if nc % SSD_CHUNKS == 0 else 1
    dt_blk0 = (d_inner + d_xbc) // LANE

    def body(x_ref, b_ref, c_ref, z_ref, dtr_ref, bias_ref, alog_ref, dsk_ref, nw_ref, hs_ref, y_ref, dyn_ref,
             dx_ref, db_ref, dc_ref, dz_ref, ddtr_ref, dnw_ref, dbias_ref, dalog_ref, ddsk_ref, dh_scr):
        @pl.when(pl.program_id(2) == 0)
        def _():
            dh_scr[...] = jnp.zeros_like(dh_scr)

        first_step = jnp.logical_and(pl.program_id(1) == 0, pl.program_id(2) == 0)
        for cc in reversed(range(cps)):
            rows = pl.ds(cc * q, q)
            chunk(jnp.logical_and(first_step, cc == cps - 1), x_ref.at[rows, :], b_ref.at[rows, :], c_ref.at[rows, :],
                  z_ref.at[rows, :], dtr_ref.at[rows, :], bias_ref, alog_ref, dsk_ref, nw_ref,
                  hs_ref.at[pl.ds(cc * n, n), :], y_ref.at[rows, :], dyn_ref.at[rows, :], dx_ref.at[rows, :],
                  db_ref.at[rows, :], dc_ref.at[rows, :], dz_ref.at[rows, :], ddtr_ref.at[rows, :],
                  dnw_ref, dbias_ref, dalog_ref, ddsk_ref, dh_scr)

    def chunk(first, x_ref, b_ref, c_ref, z_ref, dtr_ref, bias_ref, alog_ref, dsk_ref, nw_ref, hs_ref, y_ref, dyn_ref,
              dx_ref, db_ref, dc_ref, dz_ref, ddtr_ref, dnw_ref, dbias_ref, dalog_ref, ddsk_ref, dh_scr):
        dtraw = dtr_ref[...]
        dt, a, acum, acum_t, lower = _chunk_decay(dtraw, bias_ref[...], alog_ref[...])
        x = x_ref[...]
        bb = b_ref[...].astype(BF16)
        cb = c_ref[...].astype(BF16)
        g_mat = lax.dot_general(cb, bb, (((1,), (1,)), ((), ())), preferred_element_type=F32)

        yv = y_ref[...]
        z = z_ref[...]
        sz = _silu(z)
        gated = yv * sz
        rstd = lax.rsqrt(jnp.mean(gated * gated, axis=-1, keepdims=True) + EPS)
        gn = gated * rstd
        dynv = dyn_ref[...]
        gwt = dynv * nw_ref[...]
        dgated = rstd * (gwt - gn * jnp.mean(gwt * gn, axis=-1, keepdims=True))
        dnw = jnp.sum(dynv * gn, axis=0, keepdims=True)
        dy = dgated * sz
        dz_ref[...] = (dgated * yv * _dsilu(z)).astype(BF16)

        h_prev = hs_ref[...]
        h_prev_b = h_prev.astype(BF16)
        ds = dh_scr[...]
        ds_b = ds.astype(BF16)
        sum_heads, spread = _head_selectors(gw, p)
        acum_s = _dot_exact(acum, spread)
        a_last_s = acum_s[q - 1:q, :]
        dt_s = _dot_exact(dt, spread)
        dsk_s = _row_spread(dsk_ref[...], spread)
        dte_s = jnp.exp(a_last_s - acum_s)
        cd_s = jnp.exp(a_last_s)
        xdt = x * dt_s
        xdt_b = xdt.astype(BF16)
        dy_b = dy.astype(BF16)
        gt_mat = lax.dot_general(bb, cb, (((1,), (1,)), ((), ())), preferred_element_type=F32)
        upper = lax.broadcasted_iota(jnp.int32, (q, q), 0) <= lax.broadcasted_iota(jnp.int32, (q, q), 1)
        dg = jnp.zeros((q, q), F32)
        dxdts, w_diffs = [], []
        for h in range(r_heads):
            hsl = slice(h * p, (h + 1) * p)
            diff = acum[:, h:h + 1] - acum_t[h:h + 1, :]
            decay = jnp.exp(jnp.where(lower, diff, -jnp.inf))
            decay_t = jnp.exp(jnp.where(upper, -diff, -jnp.inf))
            mt_mat = gt_mat * decay_t
            dm = lax.dot_general(dy_b[:, hsl], xdt_b[:, hsl], (((1,), (1,)), ((), ())), preferred_element_type=F32)
            dm_t = lax.dot_general(xdt_b[:, hsl], dy_b[:, hsl], (((1,), (1,)), ((), ())), preferred_element_type=F32)
            dg = dg + dm * decay
            dxdts.append(jnp.dot(mt_mat.astype(BF16), dy_b[:, hsl], preferred_element_type=F32))
            w_diffs.append(dm * (g_mat * decay) - dm_t * mt_mat)
        sel_q = (lax.broadcasted_iota(jnp.int32, (r_heads * q, LANE), 0) // q
                 == lax.broadcasted_iota(jnp.int32, (r_heads * q, LANE), 1)).astype(BF16)
        dacum_diag = _dot_exact(jnp.concatenate(w_diffs, axis=1), sel_q)
        c_h = jnp.dot(cb, h_prev_b, preferred_element_type=F32)
        dxd = jnp.dot(bb, ds_b, preferred_element_type=F32)
        dxdt = jnp.concatenate(dxdts, axis=1) + dxd * dte_s
        dye = dy * jnp.exp(acum_s)
        dye_b = dye.astype(BF16)
        xd = xdt * dte_s
        xd_b = xd.astype(BF16)
        dg_b = dg.astype(BF16)
        dx_ref[...] = dxdt * dt_s + dsk_s * dy
        dc_ref[...] = (jnp.dot(dg_b, bb, preferred_element_type=F32)
                       + lax.dot_general(dye_b, h_prev_b, (((1,), (1,)), ((), ())), preferred_element_type=F32))
        db_ref[...] = (lax.dot_general(dg_b, cb, (((0,), (0,)), ((), ())), preferred_element_type=F32)
                       + lax.dot_general(xd_b, ds_b, (((1,), (1,)), ((), ())), preferred_element_type=F32))
        dh_scr[...] = ds * cd_s + lax.dot_general(cb, dye_b, (((0,), (0,)), ((), ())), preferred_element_type=F32)
        ddt_cols = _dot_exact(x * dxdt, sum_heads)
        dacum_y = _dot_exact(dye * c_h - dxd * xd, sum_heads)
        col_sums = jnp.concatenate([
            jnp.sum(dxd * xd, axis=0, keepdims=True) + jnp.sum(ds * h_prev, axis=0, keepdims=True) * cd_s,
            jnp.sum(dy * x, axis=0, keepdims=True),
            jnp.zeros((6, gw), F32)], axis=0)
        col_sums = _dot_exact(col_sums, sum_heads)
        ddsk = col_sums[1:2, :]
        rows_q = lax.broadcasted_iota(jnp.int32, (q, LANE), 0)
        dacum = dacum_diag + dacum_y + jnp.where(rows_q == q - 1, col_sums[0:1, :], 0.0)
        dadt = jnp.dot(upper.astype(F32), dacum, precision=lax.Precision.HIGHEST, preferred_element_type=F32)
        ddt = dadt * a + ddt_cols
        ddtr = ddt * _sigmoid(dtraw + bias_ref[...])
        ddtr_ref[...] = ddtr
        dbias = jnp.sum(ddtr, axis=0, keepdims=True)
        dalog = jnp.sum(dadt * dt, axis=0, keepdims=True) * a

        @pl.when(first)
        def _():
            dnw_ref[...] = dnw
            dbias_ref[...] = dbias
            dalog_ref[...] = dalog
            ddsk_ref[...] = ddsk

        @pl.when(jnp.logical_not(first))
        def _():
            dnw_ref[...] += dnw
            dbias_ref[...] += dbias
            dalog_ref[...] += dalog
            ddsk_ref[...] += ddsk

    row = lambda g, b, c: b * (nc // cps) + (nc // cps - 1 - c)
    vec = pl.BlockSpec((1, LANE), lambda g, b, c: (0, g))
    wide = pl.BlockSpec((q * cps, gw), lambda g, b, c: (row(g, b, c), g))
    narrow = pl.BlockSpec((q * cps, n), lambda g, b, c: (row(g, b, c), g))
    return pl.pallas_call(
        body,
        name=name,
        grid=(n_g, n_seq, nc // cps),
        in_specs=[
            wide,
            pl.BlockSpec((q * cps, n), lambda g, b, c: (row(g, b, c), d_inner // n + g)),
            pl.BlockSpec((q * cps, n), lambda g, b, c: (row(g, b, c), d_inner // n + n_g + g)),
            wide,
            pl.BlockSpec((q * cps, LANE), lambda g, b, c: (row(g, b, c), dt_blk0 + g)),
            vec, vec, vec,
            pl.BlockSpec((1, gw), lambda g, b, c: (0, g)),
            pl.BlockSpec((n * cps, gw), lambda g, b, c: (row(g, b, c), g)),
            wide, wide,
        ],
        out_specs=[
            wide, narrow, narrow, wide, narrow,
            pl.BlockSpec((1, gw), lambda g, b, c: (0, g)),
            vec, vec, vec,
        ],
        out_shape=[
            jax.ShapeDtypeStruct((t, d_inner), F32),
            jax.ShapeDtypeStruct((t, n_g * n), F32),
            jax.ShapeDtypeStruct((t, n_g * n), F32),
            jax.ShapeDtypeStruct(proj.shape, BF16),
            jax.ShapeDtypeStruct((t, n_g * LANE), F32),
            jax.ShapeDtypeStruct((1, d_inner), F32),
            jax.ShapeDtypeStruct((1, n_g * LANE), F32),
            jax.ShapeDtypeStruct((1, n_g * LANE), F32),
            jax.ShapeDtypeStruct((1, n_g * LANE), F32),
        ],
        scratch_shapes=[pltpu.VMEM((n, gw), F32)],
        compiler_params=_params(("parallel", "arbitrary", "arbitrary")),
    )(xbc, xbc, xbc, proj, proj, bias_p, alog_p, dskip_p, norm_w, hs, y, dyn)


MESH_IDS = pl.DeviceIdType.MESH


def _my_index():
    return 4 * lax.axis_index("x") + 2 * lax.axis_index("y") + lax.axis_index("c")


def _all_gather(shard, *, name):
    def body(x_ref, out_ref, send_sems, recv_sems, local_sem):
        x, y, c = lax.axis_index("x"), lax.axis_index("y"), lax.axis_index("c")
        me, sibling = (x, y, c), (x, y, 1 - c)
        chips = [(1 - x, y), (x, 1 - y), (1 - x, 1 - y)]

        def blk(px, py, pc):
            return out_ref.at[4 * px + 2 * py + pc]

        def copy(k, block, to, src=None):
            return pltpu.make_async_remote_copy(
                src_ref=blk(*block) if src is None else src, dst_ref=blk(*block),
                send_sem=send_sems.at[k], recv_sem=recv_sems.at[k], device_id=to, device_id_type=MESH_IDS)

        mine = pltpu.make_async_copy(x_ref, blk(*me), local_sem)
        mine.start()
        first = [copy(0, me, sibling, src=x_ref)]
        first += [copy(1 + j, me, (*chip, c), src=x_ref) for j, chip in enumerate(chips)]
        for cp in first:
            cp.start()
        passed = [copy(4 + j, (*chip, c), sibling) for j, chip in enumerate(chips)]
        for j, chip in enumerate(chips):
            copy(1 + j, (*chip, c), me).wait_recv()
            passed[j].start()
        copy(0, sibling, me).wait_recv()
        for j, chip in enumerate(chips):
            copy(4 + j, (*chip, 1 - c), me).wait_recv()
        for cp in first + passed:
            cp.wait_send()
        mine.wait()

    return pl.pallas_call(
        body,
        name=name,
        in_specs=[ANY],
        out_specs=ANY,
        out_shape=jax.ShapeDtypeStruct((N_DEV,) + shard.shape, shard.dtype),
        scratch_shapes=[pltpu.SemaphoreType.DMA((7,)), pltpu.SemaphoreType.DMA((7,)), pltpu.SemaphoreType.DMA],
    )(shard)


HBM_SPEC = pl.BlockSpec(memory_space=pltpu.HBM)
SEM_SPEC = pl.BlockSpec(memory_space=pltpu.SEMAPHORE)
SPLIT_COPY_PARAMS = pltpu.CompilerParams(has_side_effects=pltpu.SideEffectType.DATAFLOW_SIDE_EFFECTING)


def _peer_list():
    x, y, c = lax.axis_index("x"), lax.axis_index("y"), lax.axis_index("c")
    peers = []
    for k in range(1, N_DEV):
        px = 1 - x if k & 4 else x
        py = 1 - y if k & 2 else y
        pc = 1 - c if k & 1 else c
        peers.append(((px, py, pc), 4 * px + 2 * py + pc))
    return 4 * x + 2 * y + c, peers


def _push_copies(src_refs, land_refs, send_sems, recv_sems, blockwise):
    me, peers = _peer_list()
    copies = []
    for a, (src_ref, land_ref) in enumerate(zip(src_refs, land_refs)):
        for k, (dev, idx) in enumerate(peers):
            sem = a * (N_DEV - 1) + k
            src = src_ref.at[idx] if blockwise else src_ref
            copies.append(tuple(
                pltpu.make_async_remote_copy(src_ref=src, dst_ref=land_ref.at[slot], send_sem=send_sems.at[sem],
                                             recv_sem=recv_sems.at[sem], device_id=dev, device_id_type=MESH_IDS)
                for slot in (me, idx)))
    return copies


def _push_start(srcs, blockwise, after, *, name):
    n = len(srcs)
    blocks = [s_.shape[1:] if blockwise else s_.shape for s_ in srcs]

    def body(*refs):
        src_refs, land_refs = refs[:n], refs[n:2 * n]
        send_sems, recv_sems = refs[2 * n + 1], refs[2 * n + 2]
        token = refs[-1]
        for send, _ in _push_copies(src_refs, land_refs, send_sems, recv_sems, blockwise):
            send.start()
        token[...] = jnp.zeros_like(token)

    n_sem = n * (N_DEV - 1)
    lands = [lax.empty((N_DEV,) + b, s_.dtype) for b, s_ in zip(blocks, srcs)]
    out = pl.pallas_call(
        body,
        name=name,
        in_specs=[HBM_SPEC] * (2 * n) + [ANY],
        out_specs=(SEM_SPEC, SEM_SPEC) + (HBM_SPEC,) * (2 * n) + (pl.BlockSpec(memory_space=pltpu.VMEM),),
        out_shape=(pltpu.SemaphoreType.DMA((n_sem,)), pltpu.SemaphoreType.DMA((n_sem,)))
        + tuple(pltpu.HBM(a.shape, a.dtype) for a in list(srcs) + lands)
        + (jax.ShapeDtypeStruct((8, LANE), F32),),
        input_output_aliases={i: 2 + i for i in range(2 * n)},
        compiler_params=SPLIT_COPY_PARAMS,
    )(*[pltpu.with_memory_space_constraint(a, pltpu.HBM) for a in list(srcs) + lands], after)
    return out[0], out[1], out[2:2 + n], out[2 + n:2 + 2 * n], out[-1]


def _push_wait(send_sems, recv_sems, srcs, lands, blockwise, after, *, name):
    n = len(srcs)

    def body(*refs):
        src_refs, land_refs = refs[:n], refs[n:2 * n]
        send_sems, recv_sems = refs[2 * n], refs[2 * n + 1]
        for send, recv in _push_copies(src_refs, land_refs, send_sems, recv_sems, blockwise):
            send.wait_send()
            recv.wait_recv()

    out = pl.pallas_call(
        body,
        name=name,
        in_specs=[HBM_SPEC] * (2 * n) + [SEM_SPEC, SEM_SPEC, ANY],
        out_specs=(HBM_SPEC,) * (2 * n),
        out_shape=tuple(pltpu.HBM(a.shape, a.dtype) for a in list(srcs) + list(lands)),
        input_output_aliases={i: i for i in range(2 * n)},
        compiler_params=SPLIT_COPY_PARAMS,
    )(*srcs, *lands, send_sems, recv_sems, after)
    return out[n:]


def _with_own_slot(landing, own):
    slot = lax.broadcasted_iota(jnp.int32, (N_DEV,) + (1,) * own.ndim, 0)
    return jnp.where(slot == _my_index(), own[None], landing)


def _sum_slots(parts, own=None, *, name):
    shape = parts.shape[1:]
    n, c = parts.shape[0], parts.shape[-1]
    r = parts.size // (n * c)
    tm = _pick(r, (256, 128, 64, 32, 16, 8))

    def body(p_ref, *rest):
        o_ref = rest[-1]
        me = _my_index()

        def slot(s):
            if own is None:
                return p_ref[s].astype(F32)
            return jnp.where(me == s, rest[0][...], p_ref[s]).astype(F32)

        acc = slot(0)
        for s in range(1, n):
            acc = acc + slot(s)
        o_ref[...] = acc

    tile = pl.BlockSpec((tm, c), lambda i: (i, 0))
    return pl.pallas_call(
        body,
        name=name,
        grid=(r // tm,),
        in_specs=[pl.BlockSpec((n, tm, c), lambda i: (0, i, 0))] + ([] if own is None else [tile]),
        out_specs=tile,
        out_shape=jax.ShapeDtypeStruct((r, c), F32),
        compiler_params=_params(("parallel",)),
    )(parts.reshape(n, r, c), *([] if own is None else [own.reshape(r, c)])).reshape(shape)


def _row_count(shape):
    c = shape[-1]
    rows = 1
    for s in shape[:-1]:
        rows *= s
    return rows, c, c + (-c) % LANE


PACK_ROWS = 256


def _pack_rows(arrays):
    pieces = []
    for a in arrays:
        rows, c, cp = _row_count(a.shape)
        a2 = a.reshape(rows, c)
        if cp > c:
            a2 = jnp.pad(a2, ((0, 0), (0, cp - c)))
        a2 = a2.reshape(rows * cp // LANE, LANE)
        if a2.shape[0] % 8:
            a2 = jnp.pad(a2, ((0, 8 - a2.shape[0] % 8), (0, 0)))
        pieces.append(a2)
    total = sum(p.shape[0] for p in pieces)
    if total % PACK_ROWS:
        pieces.append(jnp.zeros((PACK_ROWS - total % PACK_ROWS, LANE), F32))
    return jnp.concatenate(pieces, axis=0)


def _unpack_rows(packed, shapes, lead=()):
    out, off = [], 0
    for shp in shapes:
        rows, c, cp = _row_count(shp)
        n_rows = rows * cp // LANE
        seg = packed[..., off:off + n_rows, :].reshape(lead + (rows, cp))
        out.append(seg[..., :c].reshape(lead + tuple(shp)))
        off += n_rows + (-n_rows) % 8
    return out


def _unshard(stacked, axis):
    if axis == stacked.ndim - 2:
        return jnp.concatenate([stacked[d] for d in range(N_DEV)], axis=axis)
    moved = jnp.moveaxis(stacked, 0, axis)
    shp = moved.shape
    return moved.reshape(shp[:axis] + (shp[axis] * shp[axis + 1],) + shp[axis + 2:])


def _shard_major(full, axis):
    shp = full.shape
    if axis == full.ndim - 1:
        size = shp[axis] // N_DEV
        return jnp.stack([full[..., d * size:(d + 1) * size] for d in range(N_DEV)])
    split = full.reshape(shp[:axis] + (N_DEV, shp[axis] // N_DEV) + shp[axis + 1:])
    return jnp.moveaxis(split, axis, 0)


def _my_shard(full, axis):
    size = full.shape[axis] // N_DEV
    return lax.dynamic_slice_in_dim(full, _my_index() * size, size, axis)


def _local_step(x, target, w, fetch, emit, n_seq):
    def with_token(vec, token):
        return vec + jnp.tile(token[0:1, :], (1, vec.shape[1] // LANE))

    depth, d_model = w["norm_mix_pre"].shape
    d_inner = w["ssd_norm_w"].shape[1]
    d_xbc = w["ssd_conv_w"].shape[2]
    saved = []
    for i in range(depth):
        j = i // 2
        m, token = fetch(i, "mix", x)
        mix_pre_w = with_token(w["norm_mix_pre"][i:i + 1], token)
        s = {"x": x, "mix_pre_w": mix_pre_w}
        if i % 2 == 0:
            u = _rms_fwd(x, mix_pre_w, out_dtype=BF16, name=f"l{i}_mix_pre")
            proj = _mm(u, m["ssd_w_in"], name=f"l{i}_ssd_in")
            xbc, xbc_pre = _ssd_conv_fwd(proj, d_inner, d_xbc, w["ssd_conv_w"][j], w["ssd_conv_b"][j:j + 1], n_seq,
                                         name=f"l{i}_ssd_conv")
            yn, y, hs = _ssd_fwd(proj, xbc, w["ssd_dt_bias"][j:j + 1], w["ssd_a_log"][j:j + 1], w["ssd_d"][j:j + 1],
                                 w["ssd_norm_w"][j:j + 1], n_seq, name=f"l{i}_ssd_scan")
            m_out, token = fetch(i, "out", yn)
            m = {**m, **m_out}
            mix = _mm(yn, m["ssd_w_out"], name=f"l{i}_ssd_out")
            s.update(u=u, proj=proj, xbc=xbc, xbc_pre=xbc_pre, yn=yn, y=y, hs=hs)
        else:
            u = _rms_fwd(x, mix_pre_w, out_dtype=F32, name=f"l{i}_mix_pre")
            mix = _pool_fwd(u, m["pool_w"], w["pool_scale"][j:j + 1], n_seq, name=f"l{i}_pool")
            s.update(u=u)
            token = jnp.zeros((8, LANE), F32)
        x1 = _res_rms_fwd(x, mix, with_token(w["norm_mix_post"][i:i + 1], token), name=f"l{i}_mix_post")
        m_ffn, token = fetch(i, "ffn", x1)
        m = {**m, **m_ffn}
        ffn_pre_w = with_token(w["norm_ffn_pre"][i:i + 1], token)
        n = _rms_fwd(x1, ffn_pre_w, out_dtype=BF16, name=f"l{i}_ffn_pre")
        h, hc, a = _ffn_up_act(n, m["ffn_w_up"], w["ffn_conv_w"][i], w["ffn_conv_b"][i:i + 1], n_seq,
                               name=f"l{i}_ffn_up_act")
        f = _mm(a, m["ffn_w_down"], name=f"l{i}_ffn_down")
        x = _res_rms_fwd(x1, f, w["norm_ffn_post"][i:i + 1], name=f"l{i}_ffn_post")
        s.update(mix=mix, x1=x1, n=n, h=h, hc=hc, a=a, f=f, m=m, ffn_pre_w=ffn_pre_w)
        saved.append(s)

    loss, dx = _loss_head(x, target)
    grads = {k: [None] * len(w[k]) for k in SMALL}
    token = jnp.zeros((8, LANE), F32)
    for i in reversed(range(depth)):
        j = i // 2
        s = saved[i]
        m, gm = s["m"], {}
        df, grads["norm_ffn_post"][i] = _rms_bwd(s["f"], with_token(w["norm_ffn_post"][i:i + 1], token), dx, None,
                                                 out_dtype=BF16, name=f"l{i}_ffn_post_b")
        gm["ffn_w_down"] = _mm(s["a"], df, ta=True, name=f"l{i}_ffn_down_bw")
        dh, grads["ffn_conv_w"][i], grads["ffn_conv_b"][i] = _ffn_down_bx_act_bwd(
            df, m["ffn_w_down"], s["h"], s["hc"], w["ffn_conv_w"][i], n_seq, name=f"l{i}_ffn_act_b")
        dn = _mm(dh, m["ffn_w_up"], tb=True, name=f"l{i}_ffn_up_bx")
        gm["ffn_w_up"] = _mm(s["n"], dh, ta=True, name=f"l{i}_ffn_up_bw")
        dx1, grads["norm_ffn_pre"][i] = _rms_bwd(s["x1"], s["ffn_pre_w"], dn, dx, name=f"l{i}_ffn_pre_b")
        token = emit(i, "ffn", gm, dx1)
        gm = {}
        dmix, grads["norm_mix_post"][i] = _rms_bwd(s["mix"], with_token(w["norm_mix_post"][i:i + 1], token), dx1, None,
                                                   out_dtype=BF16 if i % 2 == 0 else F32, name=f"l{i}_mix_post_b")
        if i % 2 == 0:
            dyn = _mm(dmix, m["ssd_w_out"], tb=True, name=f"l{i}_ssd_out_bx")
            gm["ssd_w_out"] = _mm(s["yn"], dmix, ta=True, name=f"l{i}_ssd_out_bw")
            token = emit(i, "out", gm, dyn)
            gm = {}
            dxs, db, dc, dz, ddtr, dnw, dbias, dalog, ddsk = _ssd_bwd(
                s["proj"], s["xbc"], s["hs"], s["y"], dyn, w["ssd_dt_bias"][j:j + 1], w["ssd_a_log"][j:j + 1],
                w["ssd_d"][j:j + 1], with_token(w["ssd_norm_w"][j:j + 1], token), n_seq, name=f"l{i}_ssd_scan_b")
            grads["ssd_norm_w"][j], grads["ssd_dt_bias"][j], grads["ssd_a_log"][j], grads["ssd_d"][j] = (
                dnw, dbias, dalog, ddsk)
            dproj, grads["ssd_conv_w"][j], grads["ssd_conv_b"][j] = _ssd_conv_bwd(
                s["proj"], d_inner, w["ssd_conv_w"][j], s["xbc_pre"], (dxs, db, dc), dz, n_seq,
                name=f"l{i}_ssd_conv_b")
            dproj = _fill_cols(dproj, ddtr, d_inner + d_xbc, name=f"l{i}_ssd_dt_b")
            gm["ssd_w_in"] = _mm(s["u"], dproj, ta=True, name=f"l{i}_ssd_in_bw")
            token = emit(i, "mix", gm, dproj)
            du = _mm(dproj, m["ssd_w_in"], tb=True, name=f"l{i}_ssd_in_bx")
        else:
            du, gm["pool_w"], grads["pool_scale"][j] = _pool_bwd(
                s["u"], m["pool_w"], w["pool_scale"][j:j + 1], dmix, n_seq, name=f"l{i}_pool_b")
            token = emit(i, "mix", gm, du)
        dx, grads["norm_mix_pre"][i] = _rms_bwd(s["x"], with_token(s["mix_pre_w"], token), du, dx1,
                                                name=f"l{i}_mix_pre_b")
    return loss, dx, grads


BIG = (("ssd_w_in", 2), ("ssd_w_out", 1), ("pool_w", 2), ("ffn_w_up", 2), ("ffn_w_down", 1))
SMALL_SHARDED = (("ssd_conv_w", 2), ("ffn_conv_w", 2), ("pool_scale", 1))
SMALL = ("ssd_conv_w", "ssd_conv_b", "ssd_dt_bias", "ssd_a_log", "ssd_d", "ssd_norm_w", "pool_scale", "ffn_conv_w",
         "ffn_conv_b", "norm_mix_pre", "norm_mix_post", "norm_ffn_pre", "norm_ffn_post")
WEIGHTS = ("ssd_w_in", "ssd_conv_w", "ssd_conv_b", "ssd_dt_bias", "ssd_a_log", "ssd_d", "ssd_norm_w", "ssd_w_out",
           "pool_w", "pool_scale", "ffn_w_up", "ffn_conv_w", "ffn_conv_b", "ffn_w_down", "norm_mix_pre",
           "norm_mix_post", "norm_ffn_pre", "norm_ffn_post")


def _ssd_sizes(d_inner):
    return d_inner + 2 * N_SSD_GROUPS * D_STATE, d_inner // HEAD_DIM // N_SSD_GROUPS


def _small_compute_layout(full, d_inner):
    _, r_heads = _ssd_sizes(d_inner)
    w = {k: full[k] for k in SMALL}
    for k in ("ssd_dt_bias", "ssd_a_log", "ssd_d"):
        w[k] = _head_pad(full[k], r_heads)
    for k in ("ffn_conv_w", "ffn_conv_b"):
        w[k] = _interleave(full[k])
    return w


def _matmul_compute_layout(k, full, d_inner):
    d_xbc, r_heads = _ssd_sizes(d_inner)
    if k == "ssd_w_in":
        return _ssd_w_in_layout(full, d_inner, d_xbc, r_heads)
    if k == "ffn_w_up":
        return _interleave(full)
    return full


def _layer_matrices(i, part):
    if part == "ffn":
        return (("ffn_w_up", 1, i), ("ffn_w_down", 0, i))
    if i % 2 == 1:
        return (("pool_w", 1, i // 2),) if part == "mix" else ()
    return (("ssd_w_in", 1, i // 2),) if part == "mix" else (("ssd_w_out", 0, i // 2),)


def _fetch_group(i, part):
    mix, out, ffn = (_layer_matrices(i, p) for p in ("mix", "out", "ffn"))
    if i % 2 == 1:
        return mix + ffn if part == "mix" else ()
    if i == 0:
        return {"mix": mix, "out": out + ffn, "ffn": ()}[part]
    return {"mix": mix + out, "out": (), "ffn": ffn}[part]


def _matmul_grad_reference_layout(k, g, d_inner):
    d_xbc, r_heads = _ssd_sizes(d_inner)
    if k == "ssd_w_in":
        return _ssd_w_in_unlayout(g, d_inner, d_xbc, r_heads)
    if k == "ffn_w_up":
        return _deinterleave(g)
    return g


def _small_grads_reference_layout(grads, shapes, d_inner):
    _, r_heads = _ssd_sizes(d_inner)
    g = {k: jnp.stack(grads[k]) for k in SMALL}
    for k in ("ssd_dt_bias", "ssd_a_log", "ssd_d"):
        g[k] = _head_unpad(g[k][:, 0], r_heads)
    for k in ("ffn_conv_w", "ffn_conv_b"):
        g[k] = _deinterleave(g[k])
    return {k: v.reshape(shapes[k]) for k, v in g.items()}


def kernel(x, ssd_w_in, ssd_conv_w, ssd_conv_b, ssd_dt_bias, ssd_a_log, ssd_d, ssd_norm_w, ssd_w_out, pool_w, pool_scale, ffn_w_up, ffn_conv_w, ffn_conv_b, ffn_w_down, norm_mix_pre, norm_mix_post, norm_ffn_pre, norm_ffn_post, loss_target, m_ssd_w_in, m_ssd_conv_w, m_ssd_conv_b, m_ssd_dt_bias, m_ssd_a_log, m_ssd_d, m_ssd_norm_w, m_ssd_w_out, m_pool_w, m_pool_scale, m_ffn_w_up, m_ffn_conv_w, m_ffn_conv_b, m_ffn_w_down, m_norm_mix_pre, m_norm_mix_post, m_norm_ffn_pre, m_norm_ffn_post, v_ssd_w_in, v_ssd_conv_w, v_ssd_conv_b, v_ssd_dt_bias, v_ssd_a_log, v_ssd_d, v_ssd_norm_w, v_ssd_w_out, v_pool_w, v_pool_scale, v_ffn_w_up, v_ffn_conv_w, v_ffn_conv_b, v_ffn_w_down, v_norm_mix_pre, v_norm_mix_post, v_norm_ffn_pre, v_norm_ffn_post):
    shards = dict(ssd_w_in=ssd_w_in, ssd_conv_w=ssd_conv_w, ssd_conv_b=ssd_conv_b, ssd_dt_bias=ssd_dt_bias,
                  ssd_a_log=ssd_a_log, ssd_d=ssd_d, ssd_norm_w=ssd_norm_w, ssd_w_out=ssd_w_out, pool_w=pool_w,
                  pool_scale=pool_scale, ffn_w_up=ffn_w_up, ffn_conv_w=ffn_conv_w, ffn_conv_b=ffn_conv_b,
                  ffn_w_down=ffn_w_down, norm_mix_pre=norm_mix_pre, norm_mix_post=norm_mix_post,
                  norm_ffn_pre=norm_ffn_pre, norm_ffn_post=norm_ffn_post)
    moments_m = dict(zip(WEIGHTS, (m_ssd_w_in, m_ssd_conv_w, m_ssd_conv_b, m_ssd_dt_bias, m_ssd_a_log, m_ssd_d, m_ssd_norm_w, m_ssd_w_out, m_pool_w, m_pool_scale, m_ffn_w_up, m_ffn_conv_w, m_ffn_conv_b, m_ffn_w_down, m_norm_mix_pre, m_norm_mix_post, m_norm_ffn_pre, m_norm_ffn_post)))
    moments_v = dict(zip(WEIGHTS, (v_ssd_w_in, v_ssd_conv_w, v_ssd_conv_b, v_ssd_dt_bias, v_ssd_a_log, v_ssd_d, v_ssd_norm_w, v_ssd_w_out, v_pool_w, v_pool_scale, v_ffn_w_up, v_ffn_conv_w, v_ffn_conv_b, v_ffn_w_down, v_norm_mix_pre, v_norm_mix_post, v_norm_ffn_pre, v_norm_ffn_post)))
    n_seq, seq, d_model = x.shape
    t = n_seq * seq

    d_inner = ssd_norm_w.shape[1]
    depth = norm_mix_pre.shape[0]
    x2 = x.reshape(t, d_model)

    shard16 = {k: shards[k].astype(BF16) for k, _ in BIG}
    order = [(i, part) for i in range(depth) for part in ("mix", "out", "ffn") if _fetch_group(i, part)]
    fetches = {}

    def start_fetch(key, after):
        srcs = [shard16[k][l] for k, _, l in _fetch_group(*key)]
        fetches[key] = _push_start(srcs, False, after, name=f"fetch{key[0]}{key[1]}_start")

    full = dict(shards)
    small_all = _all_gather(_pack_rows([shards[k] for k, _ in SMALL_SHARDED]), name="gather_small_weights")
    small_stacked = _unpack_rows(small_all, [shards[k].shape for k, _ in SMALL_SHARDED], lead=(N_DEV,))
    for (k, axis), st in zip(SMALL_SHARDED, small_stacked):
        full[k] = _unshard(st, axis)
    w = _small_compute_layout(full, d_inner)
    ready = {}

    def fetch(i, part, x_now):
        key = (i, part)
        token = jnp.zeros((8, LANE), F32)
        if key in order:
            if key == order[0]:
                wholes = [_unshard(_all_gather(shard16[k][l], name=f"fetch0_{k}"), axis) for k, axis, l in _fetch_group(i, part)]
                nxt_after = wholes[0]
            else:
                send, recv, srcs, lands, _ = fetches[key]
                lands = _push_wait(send, recv, srcs, lands, False, x_now, name=f"fetch{i}{part}_wait")
                wholes = [_unshard(_with_own_slot(land, shard16[k][l]), axis)
                          for (k, axis, l), land in zip(_fetch_group(i, part), lands)]
                nxt_after = lands[0]
            for (k, _, l), whole in zip(_fetch_group(i, part), wholes):
                ready[k, l] = _matmul_compute_layout(k, whole, d_inner)
            nxt = order.index(key) + 1
            if nxt < len(order):
                start_fetch(order[nxt], nxt_after)
                token = fetches[order[nxt]][4]
        return {k: ready[k, l] for k, _, l in _layer_matrices(i, part)}, token

    g_layers = {}
    in_flight = []

    def finish_exchange(after):
        key, blocks, (send, recv, srcs, lands, _) = in_flight.pop(0)
        lands = _push_wait(send, recv, srcs, lands, True, after, name=f"exchange{key[0]}{key[1]}_wait")
        for (k, _, l), land, block in zip(_layer_matrices(*key), lands, blocks):
            own = lax.dynamic_index_in_dim(block, _my_index(), 0, keepdims=False)
            g_layers[k, l] = _sum_slots(land, own, name=f"sum{key[0]}_{k}")

    def emit(i, part, gm, dx_now):
        if len(in_flight) >= 2:
            finish_exchange(dx_now)
        blocks = [_shard_major(_matmul_grad_reference_layout(k, gm[k].astype(BF16), d_inner), axis)
                  for k, axis, _ in _layer_matrices(i, part)]
        started = _push_start(blocks, True, dx_now, name=f"exchange{i}{part}_start")
        in_flight.append(((i, part), blocks, started))
        return started[4]

    loss, dx, grads = _local_step(x2, loss_target.reshape(t, d_model), w, fetch, emit, n_seq)
    loss = lax.psum(loss, ("x", "y", "c"))

    g_shard = {}
    small_shapes = {k: full[k].shape for k in SMALL}
    g_small = _small_grads_reference_layout(grads, small_shapes, d_inner)
    s_all = _all_gather(_pack_rows([g_small[k] for k in SMALL]) + in_flight[-1][2][4][0:1, :], name="gather_small_grads")
    for k, g in zip(SMALL, _unpack_rows(_sum_slots(s_all, name="sum_small_grads"), [small_shapes[k] for k in SMALL])):
        g_shard[k] = g
    for k, axis in SMALL_SHARDED:
        g_shard[k] = _my_shard(g_shard[k], axis)

    last = [k for key, _, _ in in_flight for k, _, _ in _layer_matrices(*key)]
    deltas, new_m, new_v = {}, {}, {}
    for k in [k for k in WEIGHTS if k not in last] + last:
        if k == last[0]:
            while in_flight:
                finish_exchange(deltas["ffn_w_up"])
        if k in dict(BIG):
            g_shard[k] = jnp.stack([g_layers[k, l] for l in range(shards[k].shape[0])])
        deltas[k], new_m[k], new_v[k] = _adamw(shards[k], g_shard[k], moments_m[k], moments_v[k], name=f"adamw_{k}")
    return (loss, dx.reshape(x.shape), *[g_shard[k] for k in WEIGHTS], *[deltas[k] for k in WEIGHTS],
            *[new_m[k] for k in WEIGHTS], *[new_v[k] for k in WEIGHTS])
```

```python
import functools

import jax
import jax.numpy as jnp
from jax import lax
from jax.experimental import pallas as pl
from jax.experimental.pallas import tpu as pltpu

F32 = jnp.float32
BF16 = jnp.bfloat16

N_DEV = 8
HEAD_DIM = 64
N_SSD_GROUPS = 4
D_STATE = 128
CHUNK = 128
POOL_WINDOWS = (2, 4, 8, 16)
EPS = 1e-6
LANE = 128
ADAM_LR = 0.001
ADAM_B1 = 0.9
ADAM_B2 = 0.999
ADAM_EPS = 1e-08
ADAM_WD = 0.01
ADAM_STEP = 10
VMEM_LIMIT = 56 * 1024 * 1024
ANY = pl.BlockSpec(memory_space=pl.ANY)


def _pick(n, cands):
    for c in cands:
        if n % c == 0:
            return c
    return n


def _params(sem):
    return pltpu.CompilerParams(dimension_semantics=sem, vmem_limit_bytes=VMEM_LIMIT)


def _sigmoid(x):
    return 0.5 * jnp.tanh(0.5 * x) + 0.5


def _silu(x):
    return x * _sigmoid(x)


def _dsilu(x):
    s = _sigmoid(x)
    return s * (1.0 + x * (1.0 - s))


def _shift_down(x, s):
    rows = lax.broadcasted_iota(jnp.int32, x.shape, 0)
    return jnp.where(rows >= s, pltpu.roll(x, s, 0), 0.0)


def _shift_up(x, s):
    n = x.shape[0]
    rows = lax.broadcasted_iota(jnp.int32, x.shape, 0)
    return jnp.where(rows < n - s, pltpu.roll(x, n - s, 0), 0.0)


MM_VMEM_BUDGET = 40 * 1024 * 1024
MM_STEP_BYTES = 1_300_000
MM_SUB = 512


def _mm_tiles(m, n, k, a_bytes, b_bytes, o_bytes):
    def cands(dim, sizes):
        out = [s for s in sizes if s <= dim and dim % s == 0]
        return out or [dim]

    best = None
    for tm in cands(m, (m, m // 2, 2048, 1024, 512, 256, 128)):
        if tm % LANE:
            continue
        for tn in cands(n, (n, n // 2, n // 4, 2048, 1024, 512, 256, 128)):
            if tn % (2 * LANE) and tn != n:
                continue
            for tk in cands(k, (k, k // 2, 2048, 1024, 512)):
                if tk % LANE:
                    continue
                nk = k // tk
                acc = tm * tn * 4 if (nk > 1 and o_bytes != 4) else 0
                temps = tm * min(tn, MM_SUB) * 4 + (tm * tk * 2 if a_bytes == 4 else 0) + (tk * tn * 2 if b_bytes == 4 else 0)
                vmem = 2 * (tm * tk * a_bytes + tk * tn * b_bytes + tm * tn * o_bytes) + acc + temps
                if vmem > MM_VMEM_BUDGET:
                    continue
                steps = (m // tm) * (n // tn) * nk
                cost = (m * k * a_bytes * (n // tn) + k * n * b_bytes * (m // tm) + m * n * o_bytes
                        + steps * MM_STEP_BYTES)
                if best is None or cost < best[0]:
                    best = (cost, tm, tn, tk)
    return best[1:]


def _mm(a, b, *, ta=False, tb=False, out_dtype=F32, name="mm"):
    m, k = (a.shape[1], a.shape[0]) if ta else a.shape
    n = b.shape[0] if tb else b.shape[1]
    o_bytes = jnp.dtype(out_dtype).itemsize
    tm, tn, tk = _mm_tiles(m, n, k, a.dtype.itemsize, b.dtype.itemsize, o_bytes)
    nk = k // tk
    sub = _pick(tn, (MM_SUB, 256))
    use_acc = nk > 1 and o_bytes != 4
    a_spec = pl.BlockSpec((tk, tm), lambda i, j, kk: (kk, i)) if ta else pl.BlockSpec((tm, tk), lambda i, j, kk: (i, kk))
    b_spec = pl.BlockSpec((tn, tk), lambda i, j, kk: (j, kk)) if tb else pl.BlockSpec((tk, tn), lambda i, j, kk: (kk, j))
    dims = (((1,), (1 if tb else 0,)), ((), ()))

    def body(a_ref, b_ref, o_ref, *scratch):
        kk = pl.program_id(2)
        acc_ref = scratch[0] if use_acc else o_ref
        if nk > 1:
            @pl.when(kk == 0)
            def _():
                acc_ref[...] = jnp.zeros_like(acc_ref)

        av = a_ref[...].astype(BF16)
        if ta:
            av = av.T
        for s in range(tn // sub):
            cols = slice(s * sub, (s + 1) * sub)
            bv = (b_ref[cols, :] if tb else b_ref[:, cols]).astype(BF16)
            part = lax.dot_general(av, bv, dims, preferred_element_type=F32)
            if nk == 1:
                o_ref[:, cols] = part.astype(out_dtype)
            else:
                acc_ref[:, cols] += part
        if use_acc:
            @pl.when(kk == nk - 1)
            def _():
                o_ref[...] = acc_ref[...].astype(out_dtype)

    return pl.pallas_call(
        body,
        name=name,
        grid=(m // tm, n // tn, nk),
        in_specs=[a_spec, b_spec],
        out_specs=pl.BlockSpec((tm, tn), lambda i, j, kk: (i, j)),
        out_shape=jax.ShapeDtypeStruct((m, n), out_dtype),
        scratch_shapes=[pltpu.VMEM((tm, tn), F32)] if use_acc else [],
        compiler_params=_params(("parallel", "parallel", "arbitrary")),
    )(a, b)


def _rms_fwd(x, w, *, out_dtype, name):
    t, d = x.shape
    tm = _pick(t, (512, 256, 128))

    def body(x_ref, w_ref, o_ref):
        xv = x_ref[...]
        rstd = lax.rsqrt(jnp.mean(xv * xv, axis=-1, keepdims=True) + EPS)
        o_ref[...] = (xv * rstd * w_ref[...]).astype(out_dtype)

    return pl.pallas_call(
        body,
        name=name,
        grid=(t // tm,),
        in_specs=[pl.BlockSpec((tm, d), lambda i: (i, 0)), pl.BlockSpec((1, d), lambda i: (0, 0))],
        out_specs=pl.BlockSpec((tm, d), lambda i: (i, 0)),
        out_shape=jax.ShapeDtypeStruct((t, d), out_dtype),
        compiler_params=_params(("parallel",)),
    )(x, w)


def _res_rms_fwd(x, f, w, *, name):
    t, d = x.shape
    tm = _pick(t, (512, 256, 128))

    def body(x_ref, f_ref, w_ref, o_ref):
        fv = f_ref[...]
        rstd = lax.rsqrt(jnp.mean(fv * fv, axis=-1, keepdims=True) + EPS)
        o_ref[...] = x_ref[...] + fv * rstd * w_ref[...]

    row = pl.BlockSpec((tm, d), lambda i: (i, 0))
    return pl.pallas_call(
        body,
        name=name,
        grid=(t // tm,),
        in_specs=[row, row, pl.BlockSpec((1, d), lambda i: (0, 0))],
        out_specs=row,
        out_shape=jax.ShapeDtypeStruct((t, d), F32),
        compiler_params=_params(("parallel",)),
    )(x, f, w)


def _rms_bwd(x, w, dy, resid, *, out_dtype=F32, name):
    t, d = x.shape
    tm = _pick(t, (512, 256, 128))
    has_res = resid is not None

    def body(*refs):
        if has_res:
            x_ref, w_ref, dy_ref, r_ref, dx_ref, dw_ref = refs
        else:
            x_ref, w_ref, dy_ref, dx_ref, dw_ref = refs
        xv = x_ref[...]
        dyv = dy_ref[...].astype(F32)
        rstd = lax.rsqrt(jnp.mean(xv * xv, axis=-1, keepdims=True) + EPS)
        xn = xv * rstd
        g = dyv * w_ref[...]
        dx = rstd * (g - xn * jnp.mean(g * xn, axis=-1, keepdims=True))
        if has_res:
            dx = dx + r_ref[...]
        dx_ref[...] = dx.astype(out_dtype)
        part = jnp.sum(dyv * xn, axis=0, keepdims=True)

        @pl.when(pl.program_id(0) == 0)
        def _():
            dw_ref[...] = part

        @pl.when(pl.program_id(0) > 0)
        def _():
            dw_ref[...] += part

    row = pl.BlockSpec((tm, d), lambda i: (i, 0))
    vec = pl.BlockSpec((1, d), lambda i: (0, 0))
    ins = [x, w, dy] + ([resid] if has_res else [])
    return pl.pallas_call(
        body,
        name=name,
        grid=(t // tm,),
        in_specs=[row, vec, row] + ([row] if has_res else []),
        out_specs=[row, vec],
        out_shape=[jax.ShapeDtypeStruct((t, d), out_dtype), jax.ShapeDtypeStruct((1, d), F32)],
        compiler_params=_params(("arbitrary",)),
    )(*ins)


def _res_rms_rms(x, f, w_post, w_pre, *, name):
    t, d = x.shape
    tm = _pick(t, (512, 256, 128))

    def body(x_ref, f_ref, wp_ref, wn_ref, x1_ref, n_ref):
        fv = f_ref[...]
        x1 = x_ref[...] + fv * lax.rsqrt(jnp.mean(fv * fv, axis=-1, keepdims=True) + EPS) * wp_ref[...]
        x1_ref[...] = x1
        n_ref[...] = (x1 * lax.rsqrt(jnp.mean(x1 * x1, axis=-1, keepdims=True) + EPS) * wn_ref[...]).astype(BF16)

    row = pl.BlockSpec((tm, d), lambda i: (i, 0))
    vec = pl.BlockSpec((1, d), lambda i: (0, 0))
    return pl.pallas_call(
        body,
        name=name,
        grid=(t // tm,),
        in_specs=[row, row, vec, vec],
        out_specs=[row, row],
        out_shape=[jax.ShapeDtypeStruct((t, d), F32), jax.ShapeDtypeStruct((t, d), BF16)],
        compiler_params=_params(("parallel",)),
    )(x, f, w_post, w_pre)


def _rms_bwd2(xa, wa, dya, resid, xb, wb, *, out_dtype, name):
    t, d = xa.shape
    tm = _pick(t, (512, 256, 128))

    def norm_bwd(xv, w, dyv):
        rstd = lax.rsqrt(jnp.mean(xv * xv, axis=-1, keepdims=True) + EPS)
        xn = xv * rstd
        g = dyv * w
        return rstd * (g - xn * jnp.mean(g * xn, axis=-1, keepdims=True)), jnp.sum(dyv * xn, axis=0, keepdims=True)

    def body(xa_ref, wa_ref, dya_ref, r_ref, xb_ref, wb_ref, da_ref, db_ref, dwa_ref, dwb_ref):
        da, dwa = norm_bwd(xa_ref[...], wa_ref[...], dya_ref[...].astype(F32))
        da = da + r_ref[...]
        da_ref[...] = da
        db, dwb = norm_bwd(xb_ref[...], wb_ref[...], da)
        db_ref[...] = db.astype(out_dtype)

        @pl.when(pl.program_id(0) == 0)
        def _():
            dwa_ref[...] = dwa
            dwb_ref[...] = dwb

        @pl.when(pl.program_id(0) > 0)
        def _():
            dwa_ref[...] += dwa
            dwb_ref[...] += dwb

    row = pl.BlockSpec((tm, d), lambda i: (i, 0))
    vec = pl.BlockSpec((1, d), lambda i: (0, 0))
    return pl.pallas_call(
        body,
        name=name,
        grid=(t // tm,),
        in_specs=[row, vec, row, row, row, vec],
        out_specs=[row, row, vec, vec],
        out_shape=[jax.ShapeDtypeStruct((t, d), F32), jax.ShapeDtypeStruct((t, d), out_dtype),
                   jax.ShapeDtypeStruct((1, d), F32), jax.ShapeDtypeStruct((1, d), F32)],
        compiler_params=_params(("arbitrary",)),
    )(xa, wa, dya, resid, xb, wb)


def _loss_head(y, target, *, name="loss_head"):
    t, d = y.shape
    tm = _pick(t, (512, 256, 128))

    def body(y_ref, t_ref, dy_ref, l_ref):
        err = y_ref[...] - t_ref[...]
        dy_ref[...] = err * (1.0 / d)
        part = jnp.sum(jnp.sum(err * err, axis=-1, keepdims=True), axis=0, keepdims=True) * (0.5 / d)
        part = jnp.broadcast_to(part, (1, LANE))

        @pl.when(pl.program_id(0) == 0)
        def _():
            l_ref[...] = part

        @pl.when(pl.program_id(0) > 0)
        def _():
            l_ref[...] += part

    row = pl.BlockSpec((tm, d), lambda i: (i, 0))
    dy, l = pl.pallas_call(
        body,
        name=name,
        grid=(t // tm,),
        in_specs=[row, row],
        out_specs=[row, pl.BlockSpec((1, LANE), lambda i: (0, 0))],
        out_shape=[jax.ShapeDtypeStruct((t, d), F32), jax.ShapeDtypeStruct((1, LANE), F32)],
        compiler_params=_params(("arbitrary",)),
    )(y, target)
    return l[0, 0], dy


def _conv_taps(h, w_ref, k_taps):
    out = h * w_ref[k_taps - 1:k_taps, :]
    for k in range(k_taps - 1):
        out = out + _shift_down(h, k_taps - 1 - k) * w_ref[k:k + 1, :]
    return out


def _conv_taps_bwd(h, dhc, w_ref, k_taps):
    dh = dhc * w_ref[k_taps - 1:k_taps, :]
    dws = []
    for k in range(k_taps - 1):
        up = _shift_up(dhc, k_taps - 1 - k)
        dh = dh + up * w_ref[k:k + 1, :]
        dws.append(jnp.sum(up * h, axis=0, keepdims=True))
    dws.append(jnp.sum(dhc * h, axis=0, keepdims=True))
    return dh, jnp.concatenate(dws, axis=0)


FFN_TC = 256


def _interleave(w, tc=FFN_TC):
    f = w.shape[-1] // 2
    tiles = []
    for j in range(f // tc):
        tiles += [w[..., j * tc:(j + 1) * tc], w[..., f + j * tc:f + (j + 1) * tc]]
    return jnp.concatenate(tiles, axis=-1)


def _deinterleave(w, tc=FFN_TC):
    n_tiles = w.shape[-1] // tc
    return jnp.concatenate([w[..., j * tc:(j + 1) * tc] for j in list(range(0, n_tiles, 2)) + list(range(1, n_tiles, 2))],
                           axis=-1)


FFN_ROWS = 512
HALO = 8


def _ffn_up_act(n, w_up, conv_w, conv_b, n_seq, *, name):
    t, d = n.shape
    f2 = w_up.shape[1]
    seq = t // n_seq
    tc = FFN_TC
    nj = f2 // (2 * tc)
    k_taps = conv_w.shape[0]
    rows = min(FFN_ROWS, seq)

    def body(n_ref, wu_ref, w_ref, b_ref, h_ref, hc_ref, o_ref, h_scr):
        h_scr[0:HALO, :] = jnp.zeros((HALO, 2 * tc), F32)
        wu = wu_ref[...]
        for r in range(seq // rows):
            chunk = slice(r * rows, (r + 1) * rows)
            h = jnp.dot(n_ref[chunk, :], wu, preferred_element_type=F32)
            h_scr[HALO + r * rows:HALO + (r + 1) * rows, :] = h
            h_ref[chunk, :] = h.astype(BF16)
            ext = h_scr[r * rows:HALO + (r + 1) * rows, :]
            hc = ext * w_ref[k_taps - 1:k_taps, :]
            for k in range(k_taps - 1):
                hc = hc + pltpu.roll(ext, k_taps - 1 - k, 0) * w_ref[k:k + 1, :]
            hc = hc[HALO:, :] + b_ref[...]
            hc_ref[chunk, :] = hc.astype(BF16)
            o_ref[chunk, :] = (_silu(hc[:, :tc]) * hc[:, tc:]).astype(BF16)

    wide = pl.BlockSpec((seq, 2 * tc), lambda b, j: (b, j))
    return pl.pallas_call(
        body,
        name=name,
        grid=(n_seq, nj),
        in_specs=[
            pl.BlockSpec((seq, d), lambda b, j: (b, 0)),
            pl.BlockSpec((d, 2 * tc), lambda b, j: (0, j)),
            pl.BlockSpec((k_taps, 2 * tc), lambda b, j: (0, j)),
            pl.BlockSpec((1, 2 * tc), lambda b, j: (0, j)),
        ],
        out_specs=[wide, wide, pl.BlockSpec((seq, tc), lambda b, j: (b, j))],
        out_shape=[jax.ShapeDtypeStruct((t, f2), BF16), jax.ShapeDtypeStruct((t, f2), BF16),
                   jax.ShapeDtypeStruct((t, f2 // 2), BF16)],
        scratch_shapes=[pltpu.VMEM((HALO + seq, 2 * tc), F32)],
        compiler_params=_params(("parallel", "arbitrary")),
    )(n, w_up, conv_w, conv_b)


def _ffn_down_bx_act_bwd(df, w_down, h, hc, conv_w, n_seq, *, name):
    t, d = df.shape
    f2 = h.shape[1]
    seq = t // n_seq
    tc = FFN_TC
    nj = f2 // (2 * tc)
    k_taps = conv_w.shape[0]

    def body(df_ref, wd_ref, h_ref, hc_ref, w_ref, dh_ref, dw_ref, db_ref):
        dav = lax.dot_general(df_ref[...], wd_ref[...], (((1,), (1,)), ((), ())), preferred_element_type=F32)
        hcv = hc_ref[...].astype(F32)
        gate, val = hcv[:, :tc], hcv[:, tc:]
        dhc = jnp.concatenate([dav * val * _dsilu(gate), dav * _silu(gate)], axis=1)
        dh, dw = _conv_taps_bwd(h_ref[...].astype(F32), dhc, w_ref, k_taps)
        dh_ref[...] = dh.astype(BF16)
        dw_ref[0] = dw
        db_ref[0] = jnp.sum(dhc, axis=0, keepdims=True)

    wide = pl.BlockSpec((seq, 2 * tc), lambda b, j: (b, j))
    dh, dw, db = pl.pallas_call(
        body,
        name=name,
        grid=(n_seq, nj),
        in_specs=[
            pl.BlockSpec((seq, d), lambda b, j: (b, 0)),
            pl.BlockSpec((tc, d), lambda b, j: (j, 0)),
            wide, wide,
            pl.BlockSpec((k_taps, 2 * tc), lambda b, j: (0, j)),
        ],
        out_specs=[
            wide,
            pl.BlockSpec((1, k_taps, 2 * tc), lambda b, j: (b, 0, j)),
            pl.BlockSpec((1, 1, 2 * tc), lambda b, j: (b, 0, j)),
        ],
        out_shape=[
            jax.ShapeDtypeStruct((t, f2), BF16),
            jax.ShapeDtypeStruct((n_seq, k_taps, f2), F32),
            jax.ShapeDtypeStruct((n_seq, 1, f2), F32),
        ],
        compiler_params=_params(("parallel", "arbitrary")),
    )(df, w_down, h, hc, conv_w)
    return dh, jnp.sum(dw, axis=0), jnp.sum(db, axis=0)


def _window_mixed(u, window):
    s = u
    step = 1
    while step < window:
        s = s + _shift_down(s, step)
        step *= 2
    rows = lax.broadcasted_iota(jnp.int32, u.shape, 0)
    inv_cnt = 1.0 / jnp.minimum(rows + 1, window).astype(F32)
    return s * inv_cnt - u, inv_cnt


def _window_mixed_bwd(dmixed, inv_cnt, window):
    r = dmixed * inv_cnt
    s = r
    step = 1
    while step < window:
        s = s + _shift_up(s, step)
        step *= 2
    return s - dmixed


def _pool_fwd(u, w, scale, n_seq, *, name):
    t, d = u.shape
    seq = t // n_seq
    n_g, dg, _ = w.shape

    def body(u_ref, w_ref, s_ref, o_ref):
        for k, window in enumerate(POOL_WINDOWS):
            @pl.when(pl.program_id(1) == k)
            def _(window=window):
                mixed, _ = _window_mixed(u_ref[...], window)
                pre = jnp.dot(mixed.astype(BF16), w_ref[0].astype(BF16), preferred_element_type=F32)
                o_ref[...] = pre * s_ref[...]

    return pl.pallas_call(
        body,
        name=name,
        grid=(n_seq, n_g),
        in_specs=[
            pl.BlockSpec((seq, dg), lambda b, g: (b, g)),
            pl.BlockSpec((1, dg, dg), lambda b, g: (g, 0, 0)),
            pl.BlockSpec((1, dg), lambda b, g: (0, g)),
        ],
        out_specs=pl.BlockSpec((seq, dg), lambda b, g: (b, g)),
        out_shape=jax.ShapeDtypeStruct((t, d), F32),
        compiler_params=_params(("parallel", "parallel")),
    )(u, w, scale)


def _pool_bwd(u, w, scale, dout, n_seq, *, name):
    t, d = u.shape
    seq = t // n_seq
    n_g, dg, _ = w.shape

    def body(u_ref, w_ref, s_ref, do_ref, du_ref, dw_ref, ds_ref):
        group = pl.program_id(0)
        first = pl.program_id(1) == 0
        for k, window in enumerate(POOL_WINDOWS):
            @pl.when(group == k)
            def _(window=window):
                mixed, inv_cnt = _window_mixed(u_ref[...], window)
                mixed_b = mixed.astype(BF16)
                w_b = w_ref[0].astype(BF16)
                dov = do_ref[...]
                pre = jnp.dot(mixed_b, w_b, preferred_element_type=F32)
                dsc = jnp.sum(dov * pre, axis=0, keepdims=True)
                dpre = (dov * s_ref[...]).astype(BF16)
                dw = lax.dot_general(mixed_b, dpre, (((0,), (0,)), ((), ())), preferred_element_type=F32)
                dmixed = lax.dot_general(dpre, w_b, (((1,), (1,)), ((), ())), preferred_element_type=F32)
                du_ref[...] = _window_mixed_bwd(dmixed, inv_cnt, window)

                @pl.when(first)
                def _():
                    dw_ref[0] = dw
                    ds_ref[...] = dsc

                @pl.when(jnp.logical_not(first))
                def _():
                    dw_ref[0] += dw
                    ds_ref[...] += dsc

    return pl.pallas_call(
        body,
        name=name,
        grid=(n_g, n_seq),
        in_specs=[
            pl.BlockSpec((seq, dg), lambda g, b: (b, g)),
            pl.BlockSpec((1, dg, dg), lambda g, b: (g, 0, 0)),
            pl.BlockSpec((1, dg), lambda g, b: (0, g)),
            pl.BlockSpec((seq, dg), lambda g, b: (b, g)),
        ],
        out_specs=[
            pl.BlockSpec((seq, dg), lambda g, b: (b, g)),
            pl.BlockSpec((1, dg, dg), lambda g, b: (g, 0, 0)),
            pl.BlockSpec((1, dg), lambda g, b: (0, g)),
        ],
        out_shape=[
            jax.ShapeDtypeStruct((t, d), F32),
            jax.ShapeDtypeStruct((n_g, dg, dg), F32),
            jax.ShapeDtypeStruct((1, d), F32),
        ],
        compiler_params=_params(("parallel", "arbitrary")),
    )(u, w, scale, dout)


def _adamw(w, g, m, v, *, name):
    shape = w.shape
    c = shape[-1]
    r = w.size // c
    tm = _pick(r, (512, 256, 128, 64, 32, 16, 8))

    def body(w_ref, g_ref, m_ref, v_ref, d_ref, nm_ref, nv_ref):
        gv = g_ref[...]
        nm = ADAM_B1 * m_ref[...] + (1.0 - ADAM_B1) * gv
        nv = ADAM_B2 * v_ref[...] + (1.0 - ADAM_B2) * (gv * gv)
        m_hat = nm / (1.0 - ADAM_B1 ** ADAM_STEP)
        v_hat = nv / (1.0 - ADAM_B2 ** ADAM_STEP)
        d_ref[...] = -ADAM_LR * (m_hat / (jnp.sqrt(v_hat) + ADAM_EPS) + ADAM_WD * w_ref[...])
        nm_ref[...] = nm
        nv_ref[...] = nv

    blk = pl.BlockSpec((tm, c), lambda i: (i, 0))
    out = jax.ShapeDtypeStruct((r, c), F32)
    res = pl.pallas_call(
        body,
        name=name,
        grid=(r // tm,),
        in_specs=[blk] * 4,
        out_specs=[blk] * 3,
        out_shape=[out] * 3,
        compiler_params=_params(("parallel",)),
    )(w.reshape(r, c), g.reshape(r, c), m.reshape(r, c), v.reshape(r, c))
    return tuple(a.reshape(shape) for a in res)


CONV_TC = 256


def _ssd_conv_fwd(proj, col0, n_cols, conv_w, conv_b, n_seq, *, name):
    t = proj.shape[0]
    seq = t // n_seq
    tc = CONV_TC
    off = col0 // tc
    k_taps = conv_w.shape[0]

    def body(h_ref, w_ref, b_ref, o_ref, pre_ref):
        pre = _conv_taps(h_ref[...], w_ref, k_taps) + b_ref[...]
        pre_ref[...] = pre.astype(BF16)
        o_ref[...] = _silu(pre)

    return pl.pallas_call(
        body,
        name=name,
        grid=(n_seq, n_cols // tc),
        in_specs=[
            pl.BlockSpec((seq, tc), lambda b, j: (b, j + off)),
            pl.BlockSpec((k_taps, tc), lambda b, j: (0, j)),
            pl.BlockSpec((1, tc), lambda b, j: (0, j)),
        ],
        out_specs=[pl.BlockSpec((seq, tc), lambda b, j: (b, j))] * 2,
        out_shape=[jax.ShapeDtypeStruct((t, n_cols), F32), jax.ShapeDtypeStruct((t, n_cols), BF16)],
        compiler_params=_params(("parallel", "parallel")),
    )(proj, conv_w, conv_b)


def _ssd_conv_bwd(proj, col0, conv_w, pre, dparts, dproj, n_seq, *, name):
    t = proj.shape[0]
    seq = t // n_seq
    tc = CONV_TC
    off = col0 // tc
    k_taps = conv_w.shape[0]
    widths = [d.shape[1] // tc for d in dparts]
    starts = [sum(widths[:i]) for i in range(len(widths))]
    n_blocks = sum(widths)
    n_parts = len(dparts)

    def body(h_ref, w_ref, pre_ref, *rest):
        part_refs = rest[:n_parts]
        dh_ref, dw_ref, db_ref = rest[n_parts + 1:]
        j = pl.program_id(0)
        da = part_refs[-1][...]
        for i in reversed(range(n_parts - 1)):
            da = jnp.where(j < starts[i + 1], part_refs[i][...], da)
        dhc = da * _dsilu(pre_ref[...].astype(F32))
        dh, dw = _conv_taps_bwd(h_ref[...], dhc, w_ref, k_taps)
        dh_ref[...] = dh.astype(BF16)
        db = jnp.sum(dhc, axis=0, keepdims=True)

        @pl.when(pl.program_id(1) == 0)
        def _():
            dw_ref[...] = dw
            db_ref[...] = db

        @pl.when(pl.program_id(1) > 0)
        def _():
            dw_ref[...] += dw
            db_ref[...] += db

    def part_spec(start, width):
        return pl.BlockSpec((seq, tc), lambda j, b: (b, jnp.clip(j - start, 0, width - 1)))

    n_cols = n_blocks * tc
    return pl.pallas_call(
        body,
        name=name,
        grid=(n_blocks, n_seq),
        in_specs=[
            pl.BlockSpec((seq, tc), lambda j, b: (b, j + off)),
            pl.BlockSpec((k_taps, tc), lambda j, b: (0, j)),
            pl.BlockSpec((seq, tc), lambda j, b: (b, j)),
        ] + [part_spec(st, wd) for st, wd in zip(starts, widths)] + [ANY],
        out_specs=[
            pl.BlockSpec((seq, tc), lambda j, b: (b, j + off)),
            pl.BlockSpec((k_taps, tc), lambda j, b: (0, j)),
            pl.BlockSpec((1, tc), lambda j, b: (0, j)),
        ],
        out_shape=[
            jax.ShapeDtypeStruct(dproj.shape, BF16),
            jax.ShapeDtypeStruct((k_taps, n_cols), F32),
            jax.ShapeDtypeStruct((1, n_cols), F32),
        ],
        input_output_aliases={3 + n_parts: 0},
        compiler_params=_params(("parallel", "arbitrary")),
    )(proj, conv_w, pre, *dparts, dproj)


def _fill_cols(buf, src, col0, *, name):
    t, c = src.shape
    tm = _pick(t, (1024, 512, 256, 128))

    def body(s_ref, b_ref, o_ref):
        o_ref[...] = s_ref[...].astype(o_ref.dtype)

    return pl.pallas_call(
        body,
        name=name,
        grid=(t // tm,),
        in_specs=[pl.BlockSpec((tm, c), lambda i: (i, 0)), ANY],
        out_specs=pl.BlockSpec((tm, c), lambda i: (i, col0 // c)),
        out_shape=jax.ShapeDtypeStruct(buf.shape, buf.dtype),
        input_output_aliases={1: 0},
        compiler_params=_params(("parallel",)),
    )(src, buf)


def _softplus(x):
    return jnp.maximum(x, 0.0) + jnp.log(1.0 + jnp.exp(-jnp.abs(x)))


def _chunk_decay(dtraw, bias, alog):
    q = dtraw.shape[0]
    dt = _softplus(dtraw + bias)
    a = -jnp.exp(alog)
    rows = lax.broadcasted_iota(jnp.int32, (q, q), 0)
    cols = lax.broadcasted_iota(jnp.int32, (q, q), 1)
    lower = rows >= cols
    acum = jnp.dot(lower.astype(F32), dt * a, precision=lax.Precision.HIGHEST, preferred_element_type=F32)
    return dt, a, acum, acum.T, lower


def _dot_exact(v, sel):
    hi = v.astype(BF16)
    r1 = v - hi.astype(F32)
    mid = r1.astype(BF16)
    lo = (r1 - mid.astype(F32)).astype(BF16)
    return (jnp.dot(hi, sel, preferred_element_type=F32) + jnp.dot(mid, sel, preferred_element_type=F32)
            + jnp.dot(lo, sel, preferred_element_type=F32))


def _head_selectors(gw, p):
    sum_heads = (lax.broadcasted_iota(jnp.int32, (gw, LANE), 0) // p == lax.broadcasted_iota(jnp.int32, (gw, LANE), 1))
    spread = (lax.broadcasted_iota(jnp.int32, (LANE, gw), 0) == lax.broadcasted_iota(jnp.int32, (LANE, gw), 1) // p)
    return sum_heads.astype(BF16), spread.astype(BF16)


def _row_spread(v, spread):
    return _dot_exact(jnp.broadcast_to(v, (8, v.shape[1])), spread)[0:1, :]


def _head_pad(v, r_heads):
    lead = v.shape[:-1]
    vg = v.reshape(lead + (N_SSD_GROUPS, r_heads))
    vg = jnp.pad(vg, [(0, 0)] * len(lead) + [(0, 0), (0, LANE - r_heads)])
    out = vg.reshape(lead + (N_SSD_GROUPS * LANE,))
    return out[None] if out.ndim == 1 else out


def _head_unpad(v, r_heads):
    lead = v.shape[:-1]
    out = v.reshape(lead + (N_SSD_GROUPS, LANE))[..., :r_heads].reshape(lead + (N_SSD_GROUPS * r_heads,))
    return out[0] if (len(lead) == 1 and lead[0] == 1) else out


def _ssd_w_in_layout(w_in, d_inner, d_xbc, r_heads):
    main = w_in[:, :d_inner + d_xbc]
    return jnp.concatenate([main, _head_pad(w_in[:, d_inner + d_xbc:], r_heads)], axis=1)


def _ssd_w_in_unlayout(w, d_inner, d_xbc, r_heads):
    main = w[:, :d_inner + d_xbc]
    return jnp.concatenate([main, _head_unpad(w[:, d_inner + d_xbc:], r_heads)], axis=1)


SSD_CHUNKS = 2
SSD_CHUNKS_FWD = 4


def _ssd_dims(proj, xbc):
    d_xbc = xbc.shape[1]
    d_inner = d_xbc - 2 * N_SSD_GROUPS * D_STATE
    gw = d_inner // N_SSD_GROUPS
    return d_inner, d_xbc, gw, gw // HEAD_DIM


def _ssd_fwd(proj, xbc, bias_p, alog_p, dskip_p, norm_w, n_seq, *, name):
    t = proj.shape[0]
    d_inner, d_xbc, gw, r_heads = _ssd_dims(proj, xbc)
    q, n, n_g, p = CHUNK, D_STATE, N_SSD_GROUPS, HEAD_DIM
    seq = t // n_seq
    nc = seq // q
    cps = SSD_CHUNKS_FWD if nc % SSD_CHUNKS_FWD == 0 else 1
    dt_blk0 = (d_inner + d_xbc) // LANE

    def body(x_ref, b_ref, c_ref, z_ref, dtr_ref, bias_ref, alog_ref, dsk_ref, nw_ref, yn_ref, y_ref, hs_ref, h_scr):
        @pl.when(pl.program_id(2) == 0)
        def _():
            h_scr[...] = jnp.zeros_like(h_scr)

        for cc in range(cps):
            rows = pl.ds(cc * q, q)
            chunk(x_ref.at[rows, :], b_ref.at[rows, :], c_ref.at[rows, :], z_ref.at[rows, :], dtr_ref.at[rows, :],
                  bias_ref, alog_ref, dsk_ref, nw_ref, yn_ref.at[rows, :], y_ref.at[rows, :],
                  hs_ref.at[pl.ds(cc * n, n), :], h_scr)

    def chunk(x_ref, b_ref, c_ref, z_ref, dtr_ref, bias_ref, alog_ref, dsk_ref, nw_ref, yn_ref, y_ref, hs_ref, h_scr):
        dt, a, acum, acum_t, lower = _chunk_decay(dtr_ref[...], bias_ref[...], alog_ref[...])
        x = x_ref[...]
        bb = b_ref[...].astype(BF16)
        cb = c_ref[...].astype(BF16)
        g_mat = lax.dot_general(cb, bb, (((1,), (1,)), ((), ())), preferred_element_type=F32)
        h_prev = h_scr[...]
        hs_ref[...] = h_prev
        c_h = jnp.dot(cb, h_prev.astype(BF16), preferred_element_type=F32)
        _, spread = _head_selectors(gw, p)
        acum_s = _dot_exact(acum, spread)
        a_last_s = acum_s[q - 1:q, :]
        xdt = x * _dot_exact(dt, spread)
        xdt_b = xdt.astype(BF16)
        ys = []
        for h in range(r_heads):
            decay = jnp.exp(jnp.where(lower, acum[:, h:h + 1] - acum_t[h:h + 1, :], -jnp.inf))
            ys.append(jnp.dot((g_mat * decay).astype(BF16), xdt_b[:, h * p:(h + 1) * p], preferred_element_type=F32))
        y = jnp.concatenate(ys, axis=1) + jnp.exp(acum_s) * c_h + _row_spread(dsk_ref[...], spread) * x
        xd = xdt * jnp.exp(a_last_s - acum_s)
        states = lax.dot_general(bb, xd.astype(BF16), (((0,), (0,)), ((), ())), preferred_element_type=F32)
        h_scr[...] = h_prev * jnp.exp(a_last_s) + states
        y_ref[...] = y
        gated = y * _silu(z_ref[...])
        rstd = lax.rsqrt(jnp.mean(gated * gated, axis=-1, keepdims=True) + EPS)
        yn_ref[...] = (gated * rstd * nw_ref[...]).astype(BF16)

    row = lambda b, g, c: b * (nc // cps) + c
    vec = pl.BlockSpec((1, LANE), lambda b, g, c: (0, g))
    return pl.pallas_call(
        body,
        name=name,
        grid=(n_seq, n_g, nc // cps),
        in_specs=[
            pl.BlockSpec((q * cps, gw), lambda b, g, c: (row(b, g, c), g)),
            pl.BlockSpec((q * cps, n), lambda b, g, c: (row(b, g, c), d_inner // n + g)),
            pl.BlockSpec((q * cps, n), lambda b, g, c: (row(b, g, c), d_inner // n + n_g + g)),
            pl.BlockSpec((q * cps, gw), lambda b, g, c: (row(b, g, c), g)),
            pl.BlockSpec((q * cps, LANE), lambda b, g, c: (row(b, g, c), dt_blk0 + g)),
            vec, vec, vec,
            pl.BlockSpec((1, gw), lambda b, g, c: (0, g)),
        ],
        out_specs=[
            pl.BlockSpec((q * cps, gw), lambda b, g, c: (row(b, g, c), g)),
            pl.BlockSpec((q * cps, gw), lambda b, g, c: (row(b, g, c), g)),
            pl.BlockSpec((n * cps, gw), lambda b, g, c: (row(b, g, c), g)),
        ],
        out_shape=[
            jax.ShapeDtypeStruct((t, d_inner), BF16),
            jax.ShapeDtypeStruct((t, d_inner), F32),
            jax.ShapeDtypeStruct((n_seq * nc * n, d_inner), F32),
        ],
        scratch_shapes=[pltpu.VMEM((n, gw), F32)],
        compiler_params=_params(("parallel", "parallel", "arbitrary")),
    )(xbc, xbc, xbc, proj, proj, bias_p, alog_p, dskip_p, norm_w)


def _ssd_bwd(proj, xbc, hs, y, dyn, bias_p, alog_p, dskip_p, norm_w, n_seq, *, name):
    t = proj.shape[0]
    d_inner, d_xbc, gw, r_heads = _ssd_dims(proj, xbc)
    q, n, n_g, p = CHUNK, D_STATE, N_SSD_GROUPS, HEAD_DIM
    seq = t // n_seq
    nc = seq // q
    cps = SSD_CHUNKS if nc % SSD_CHUNKS == 0 else 1
    dt_blk0 = (d_inner + d_xbc) // LANE

    def body(x_ref, b_ref, c_ref, z_ref, dtr_ref, bias_ref, alog_ref, dsk_ref, nw_ref, hs_ref, y_ref, dyn_ref,
             dx_ref, db_ref, dc_ref, dz_ref, ddtr_ref, dnw_ref, dbias_ref, dalog_ref, ddsk_ref, dh_scr):
        @pl.when(pl.program_id(2) == 0)
        def _():
            dh_scr[...] = jnp.zeros_like(dh_scr)

        first_step = jnp.logical_and(pl.program_id(1) == 0, pl.program_id(2) == 0)
        for cc in reversed(range(cps)):
            rows = pl.ds(cc * q, q)
            chunk(jnp.logical_and(first_step, cc == cps - 1), x_ref.at[rows, :], b_ref.at[rows, :], c_ref.at[rows, :],
                  z_ref.at[rows, :], dtr_ref.at[rows, :], bias_ref, alog_ref, dsk_ref, nw_ref,
                  hs_ref.at[pl.ds(cc * n, n), :], y_ref.at[rows, :], dyn_ref.at[rows, :], dx_ref.at[rows, :],
                  db_ref.at[rows, :], dc_ref.at[rows, :], dz_ref.at[rows, :], ddtr_ref.at[rows, :],
                  dnw_ref, dbias_ref, dalog_ref, ddsk_ref, dh_scr)

    def chunk(first, x_ref, b_ref, c_ref, z_ref, dtr_ref, bias_ref, alog_ref, dsk_ref, nw_ref, hs_ref, y_ref, dyn_ref,
              dx_ref, db_ref, dc_ref, dz_ref, ddtr_ref, dnw_ref, dbias_ref, dalog_ref, ddsk_ref, dh_scr):
        dtraw = dtr_ref[...]
        dt, a, acum, acum_t, lower = _chunk_decay(dtraw, bias_ref[...], alog_ref[...])
        x = x_ref[...]
        bb = b_ref[...].astype(BF16)
        cb = c_ref[...].astype(BF16)
        g_mat = lax.dot_general(cb, bb, (((1,), (1,)), ((), ())), preferred_element_type=F32)

        yv = y_ref[...]
        z = z_ref[...]
        sz = _silu(z)
        gated = yv * sz
        rstd = lax.rsqrt(jnp.mean(gated * gated, axis=-1, keepdims=True) + EPS)
        gn = gated * rstd
        dynv = dyn_ref[...]
        gwt = dynv * nw_ref[...]
        dgated = rstd * (gwt - gn * jnp.mean(gwt * gn, axis=-1, keepdims=True))
        dnw = jnp.sum(dynv * gn, axis=0, keepdims=True)
        dy = dgated * sz
        dz_ref[...] = (dgated * yv * _dsilu(z)).astype(BF16)

        h_prev = hs_ref[...]
        h_prev_b = h_prev.astype(BF16)
        ds = dh_scr[...]
        ds_b = ds.astype(BF16)
        sum_heads, spread = _head_selectors(gw, p)
        acum_s = _dot_exact(acum, spread)
        a_last_s = acum_s[q - 1:q, :]
        dt_s = _dot_exact(dt, spread)
        dsk_s = _row_spread(dsk_ref[...], spread)
        dte_s = jnp.exp(a_last_s - acum_s)
        cd_s = jnp.exp(a_last_s)
        xdt = x * dt_s
        xdt_b = xdt.astype(BF16)
        dy_b = dy.astype(BF16)
        gt_mat = lax.dot_general(bb, cb, (((1,), (1,)), ((), ())), preferred_element_type=F32)
        upper = lax.broadcasted_iota(jnp.int32, (q, q), 0) <= lax.broadcasted_iota(jnp.int32, (q, q), 1)
        dg = jnp.zeros((q, q), F32)
        dxdts, w_diffs = [], []
        for h in range(r_heads):
            hsl = slice(h * p, (h + 1) * p)
            diff = acum[:, h:h + 1] - acum_t[h:h + 1, :]
            decay = jnp.exp(jnp.where(lower, diff, -jnp.inf))
            decay_t = jnp.exp(jnp.where(upper, -diff, -jnp.inf))
            mt_mat = gt_mat * decay_t
            dm = lax.dot_general(dy_b[:, hsl], xdt_b[:, hsl], (((1,), (1,)), ((), ())), preferred_element_type=F32)
            dm_t = lax.dot_general(xdt_b[:, hsl], dy_b[:, hsl], (((1,), (1,)), ((), ())), preferred_element_type=F32)
            dg = dg + dm * decay
            dxdts.append(jnp.dot(mt_mat.astype(BF16), dy_b[:, hsl], preferred_element_type=F32))
            w_diffs.append(dm * (g_mat * decay) - dm_t * mt_mat)
        sel_q = (lax.broadcasted_iota(jnp.int32, (r_heads * q, LANE), 0) // q
                 == lax.broadcasted_iota(jnp.int32, (r_heads * q, LANE), 1)).astype(BF16)
        dacum_diag = _dot_exact(jnp.concatenate(w_diffs, axis=1), sel_q)
        c_h = jnp.dot(cb, h_prev_b, preferred_element_type=F32)
        dxd = jnp.dot(bb, ds_b, preferred_element_type=F32)
        dxdt = jnp.concatenate(dxdts, axis=1) + dxd * dte_s
        dye = dy * jnp.exp(acum_s)
        dye_b = dye.astype(BF16)
        xd = xdt * dte_s
        xd_b = xd.astype(BF16)
        dg_b = dg.astype(BF16)
        dx_ref[...] = dxdt * dt_s + dsk_s * dy
        dc_ref[...] = (jnp.dot(dg_b, bb, preferred_element_type=F32)
                       + lax.dot_general(dye_b, h_prev_b, (((1,), (1,)), ((), ())), preferred_element_type=F32))
        db_ref[...] = (lax.dot_general(dg_b, cb, (((0,), (0,)), ((), ())), preferred_element_type=F32)
                       + lax.dot_general(xd_b, ds_b, (((1,), (1,)), ((), ())), preferred_element_type=F32))
        dh_scr[...] = ds * cd_s + lax.dot_general(cb, dye_b, (((0,), (0,)), ((), ())), preferred_element_type=F32)
        ddt_cols = _dot_exact(x * dxdt, sum_heads)
        dacum_y = _dot_exact(dye * c_h - dxd * xd, sum_heads)
        col_sums = jnp.concatenate([
            jnp.sum(dxd * xd, axis=0, keepdims=True) + jnp.sum(ds * h_prev, axis=0, keepdims=True) * cd_s,
            jnp.sum(dy * x, axis=0, keepdims=True),
            jnp.zeros((6, gw), F32)], axis=0)
        col_sums = _dot_exact(col_sums, sum_heads)
        ddsk = col_sums[1:2, :]
        rows_q = lax.broadcasted_iota(jnp.int32, (q, LANE), 0)
        dacum = dacum_diag + dacum_y + jnp.where(rows_q == q - 1, col_sums[0:1, :], 0.0)
        dadt = jnp.dot(upper.astype(F32), dacum, precision=lax.Precision.HIGHEST, preferred_element_type=F32)
        ddt = dadt * a + ddt_cols
        ddtr = ddt * _sigmoid(dtraw + bias_ref[...])
        ddtr_ref[...] = ddtr
        dbias = jnp.sum(ddtr, axis=0, keepdims=True)
        dalog = jnp.sum(dadt * dt, axis=0, keepdims=True) * a

        @pl.when(first)
        def _():
            dnw_ref[...] = dnw
            dbias_ref[...] = dbias
            dalog_ref[...] = dalog
            ddsk_ref[...] = ddsk

        @pl.when(jnp.logical_not(first))
        def _():
            dnw_ref[...] += dnw
            dbias_ref[...] += dbias
            dalog_ref[...] += dalog
            ddsk_ref[...] += ddsk

    row = lambda g, b, c: b * (nc // cps) + (nc // cps - 1 - c)
    vec = pl.BlockSpec((1, LANE), lambda g, b, c: (0, g))
    wide = pl.BlockSpec((q * cps, gw), lambda g, b, c: (row(g, b, c), g))
    narrow = pl.BlockSpec((q * cps, n), lambda g, b, c: (row(g, b, c), g))
    return pl.pallas_call(
        body,
        name=name,
        grid=(n_g, n_seq, nc // cps),
        in_specs=[
            wide,
            pl.BlockSpec((q * cps, n), lambda g, b, c: (row(g, b, c), d_inner // n + g)),
            pl.BlockSpec((q * cps, n), lambda g, b, c: (row(g, b, c), d_inner // n + n_g + g)),
            wide,
            pl.BlockSpec((q * cps, LANE), lambda g, b, c: (row(g, b, c), dt_blk0 + g)),
            vec, vec, vec,
            pl.BlockSpec((1, gw), lambda g, b, c: (0, g)),
            pl.BlockSpec((n * cps, gw), lambda g, b, c: (row(g, b, c), g)),
            wide, wide,
        ],
        out_specs=[
            wide, narrow, narrow, wide, narrow,
            pl.BlockSpec((1, gw), lambda g, b, c: (0, g)),
            vec, vec, vec,
        ],
        out_shape=[
            jax.ShapeDtypeStruct((t, d_inner), F32),
            jax.ShapeDtypeStruct((t, n_g * n), F32),
            jax.ShapeDtypeStruct((t, n_g * n), F32),
            jax.ShapeDtypeStruct(proj.shape, BF16),
            jax.ShapeDtypeStruct((t, n_g * LANE), F32),
            jax.ShapeDtypeStruct((1, d_inner), F32),
            jax.ShapeDtypeStruct((1, n_g * LANE), F32),
            jax.ShapeDtypeStruct((1, n_g * LANE), F32),
            jax.ShapeDtypeStruct((1, n_g * LANE), F32),
        ],
        scratch_shapes=[pltpu.VMEM((n, gw), F32)],
        compiler_params=_params(("parallel", "arbitrary", "arbitrary")),
    )(xbc, xbc, xbc, proj, proj, bias_p, alog_p, dskip_p, norm_w, hs, y, dyn)


MESH_IDS = pl.DeviceIdType.MESH


def _my_index():
    return 4 * lax.axis_index("x") + 2 * lax.axis_index("y") + lax.axis_index("c")


def _all_gather(shard, *, name):
    def body(x_ref, out_ref, send_sems, recv_sems, local_sem):
        x, y, c = lax.axis_index("x"), lax.axis_index("y"), lax.axis_index("c")
        me, sibling = (x, y, c), (x, y, 1 - c)
        chips = [(1 - x, y), (x, 1 - y), (1 - x, 1 - y)]

        def blk(px, py, pc):
            return out_ref.at[4 * px + 2 * py + pc]

        def copy(k, block, to, src=None):
            return pltpu.make_async_remote_copy(
                src_ref=blk(*block) if src is None else src, dst_ref=blk(*block),
                send_sem=send_sems.at[k], recv_sem=recv_sems.at[k], device_id=to, device_id_type=MESH_IDS)

        mine = pltpu.make_async_copy(x_ref, blk(*me), local_sem)
        mine.start()
        first = [copy(0, me, sibling, src=x_ref)]
        first += [copy(1 + j, me, (*chip, c), src=x_ref) for j, chip in enumerate(chips)]
        for cp in first:
            cp.start()
        passed = [copy(4 + j, (*chip, c), sibling) for j, chip in enumerate(chips)]
        for j, chip in enumerate(chips):
            copy(1 + j, (*chip, c), me).wait_recv()
            passed[j].start()
        copy(0, sibling, me).wait_recv()
        for j, chip in enumerate(chips):
            copy(4 + j, (*chip, 1 - c), me).wait_recv()
        for cp in first + passed:
            cp.wait_send()
        mine.wait()

    return pl.pallas_call(
        body,
        name=name,
        in_specs=[ANY],
        out_specs=ANY,
        out_shape=jax.ShapeDtypeStruct((N_DEV,) + shard.shape, shard.dtype),
        scratch_shapes=[pltpu.SemaphoreType.DMA((7,)), pltpu.SemaphoreType.DMA((7,)), pltpu.SemaphoreType.DMA],
    )(shard)


HBM_SPEC = pl.BlockSpec(memory_space=pltpu.HBM)
SEM_SPEC = pl.BlockSpec(memory_space=pltpu.SEMAPHORE)
SPLIT_COPY_PARAMS = pltpu.CompilerParams(has_side_effects=pltpu.SideEffectType.DATAFLOW_SIDE_EFFECTING)


def _peer_list():
    x, y, c = lax.axis_index("x"), lax.axis_index("y"), lax.axis_index("c")
    peers = []
    for k in range(1, N_DEV):
        px = 1 - x if k & 4 else x
        py = 1 - y if k & 2 else y
        pc = 1 - c if k & 1 else c
        peers.append(((px, py, pc), 4 * px + 2 * py + pc))
    return 4 * x + 2 * y + c, peers


def _push_copies(src_refs, land_refs, send_sems, recv_sems, blockwise):
    me, peers = _peer_list()
    copies = []
    for a, (src_ref, land_ref) in enumerate(zip(src_refs, land_refs)):
        for k, (dev, idx) in enumerate(peers):
            sem = a * (N_DEV - 1) + k
            src = src_ref.at[idx] if blockwise else src_ref
            copies.append(tuple(
                pltpu.make_async_remote_copy(src_ref=src, dst_ref=land_ref.at[slot], send_sem=send_sems.at[sem],
                                             recv_sem=recv_sems.at[sem], device_id=dev, device_id_type=MESH_IDS)
                for slot in (me, idx)))
    return copies


def _push_start(srcs, blockwise, after, *, name):
    n = len(srcs)
    blocks = [s_.shape[1:] if blockwise else s_.shape for s_ in srcs]

    def body(*refs):
        src_refs, land_refs = refs[:n], refs[n:2 * n]
        send_sems, recv_sems = refs[2 * n + 1], refs[2 * n + 2]
        token = refs[-1]
        for send, _ in _push_copies(src_refs, land_refs, send_sems, recv_sems, blockwise):
            send.start()
        token[...] = jnp.zeros_like(token)

    n_sem = n * (N_DEV - 1)
    lands = [lax.empty((N_DEV,) + b, s_.dtype) for b, s_ in zip(blocks, srcs)]
    out = pl.pallas_call(
        body,
        name=name,
        in_specs=[HBM_SPEC] * (2 * n) + [ANY],
        out_specs=(SEM_SPEC, SEM_SPEC) + (HBM_SPEC,) * (2 * n) + (pl.BlockSpec(memory_space=pltpu.VMEM),),
        out_shape=(pltpu.SemaphoreType.DMA((n_sem,)), pltpu.SemaphoreType.DMA((n_sem,)))
        + tuple(pltpu.HBM(a.shape, a.dtype) for a in list(srcs) + lands)
        + (jax.ShapeDtypeStruct((8, LANE), F32),),
        input_output_aliases={i: 2 + i for i in range(2 * n)},
        compiler_params=SPLIT_COPY_PARAMS,
    )(*[pltpu.with_memory_space_constraint(a, pltpu.HBM) for a in list(srcs) + lands], after)
    return out[0], out[1], out[2:2 + n], out[2 + n:2 + 2 * n], out[-1]


def _push_wait(send_sems, recv_sems, srcs, lands, blockwise, after, *, name):
    n = len(srcs)

    def body(*refs):
        src_refs, land_refs = refs[:n], refs[n:2 * n]
        send_sems, recv_sems = refs[2 * n], refs[2 * n + 1]
        for send, recv in _push_copies(src_refs, land_refs, send_sems, recv_sems, blockwise):
            send.wait_send()
            recv.wait_recv()

    out = pl.pallas_call(
        body,
        name=name,
        in_specs=[HBM_SPEC] * (2 * n) + [SEM_SPEC, SEM_SPEC, ANY],
        out_specs=(HBM_SPEC,) * (2 * n),
        out_shape=tuple(pltpu.HBM(a.shape, a.dtype) for a in list(srcs) + list(lands)),
        input_output_aliases={i: i for i in range(2 * n)},
        compiler_params=SPLIT_COPY_PARAMS,
    )(*srcs, *lands, send_sems, recv_sems, after)
    return out[n:]


def _with_own_slot(landing, own):
    slot = lax.broadcasted_iota(jnp.int32, (N_DEV,) + (1,) * own.ndim, 0)
    return jnp.where(slot == _my_index(), own[None], landing)


def _sum_slots(parts, own=None, *, name):
    shape = parts.shape[1:]
    n, c = parts.shape[0], parts.shape[-1]
    r = parts.size // (n * c)
    tm = _pick(r, (256, 128, 64, 32, 16, 8))

    def body(p_ref, *rest):
        o_ref = rest[-1]
        me = _my_index()

        def slot(s):
            if own is None:
                return p_ref[s].astype(F32)
            return jnp.where(me == s, rest[0][...], p_ref[s]).astype(F32)

        acc = slot(0)
        for s in range(1, n):
            acc = acc + slot(s)
        o_ref[...] = acc

    tile = pl.BlockSpec((tm, c), lambda i: (i, 0))
    return pl.pallas_call(
        body,
        name=name,
        grid=(r // tm,),
        in_specs=[pl.BlockSpec((n, tm, c), lambda i: (0, i, 0))] + ([] if own is None else [tile]),
        out_specs=tile,
        out_shape=jax.ShapeDtypeStruct((r, c), F32),
        compiler_params=_params(("parallel",)),
    )(parts.reshape(n, r, c), *([] if own is None else [own.reshape(r, c)])).reshape(shape)


def _row_count(shape):
    c = shape[-1]
    rows = 1
    for s in shape[:-1]:
        rows *= s
    return rows, c, c + (-c) % LANE


PACK_ROWS = 256


def _pack_rows(arrays):
    pieces = []
    for a in arrays:
        rows, c, cp = _row_count(a.shape)
        a2 = a.reshape(rows, c)
        if cp > c:
            a2 = jnp.pad(a2, ((0, 0), (0, cp - c)))
        a2 = a2.reshape(rows * cp // LANE, LANE)
        if a2.shape[0] % 8:
            a2 = jnp.pad(a2, ((0, 8 - a2.shape[0] % 8), (0, 0)))
        pieces.append(a2)
    total = sum(p.shape[0] for p in pieces)
    if total % PACK_ROWS:
        pieces.append(jnp.zeros((PACK_ROWS - total % PACK_ROWS, LANE), F32))
    return jnp.concatenate(pieces, axis=0)


def _unpack_rows(packed, shapes, lead=()):
    out, off = [], 0
    for shp in shapes:
        rows, c, cp = _row_count(shp)
        n_rows = rows * cp // LANE
        seg = packed[..., off:off + n_rows, :].reshape(lead + (rows, cp))
        out.append(seg[..., :c].reshape(lead + tuple(shp)))
        off += n_rows + (-n_rows) % 8
    return out


def _unshard(stacked, axis):
    if axis == stacked.ndim - 2:
        return jnp.concatenate([stacked[d] for d in range(N_DEV)], axis=axis)
    moved = jnp.moveaxis(stacked, 0, axis)
    shp = moved.shape
    return moved.reshape(shp[:axis] + (shp[axis] * shp[axis + 1],) + shp[axis + 2:])


def _shard_major(full, axis):
    shp = full.shape
    if axis == full.ndim - 1:
        size = shp[axis] // N_DEV
        return jnp.stack([full[..., d * size:(d + 1) * size] for d in range(N_DEV)])
    split = full.reshape(shp[:axis] + (N_DEV, shp[axis] // N_DEV) + shp[axis + 1:])
    return jnp.moveaxis(split, axis, 0)


def _my_shard(full, axis):
    size = full.shape[axis] // N_DEV
    return lax.dynamic_slice_in_dim(full, _my_index() * size, size, axis)


def _local_step(x, target, w, fetch, emit, n_seq):
    def with_token(vec, token):
        return vec + jnp.tile(token[0:1, :], (1, vec.shape[1] // LANE))

    depth, d_model = w["norm_mix_pre"].shape
    d_inner = w["ssd_norm_w"].shape[1]
    d_xbc = w["ssd_conv_w"].shape[2]
    saved = []
    for i in range(depth):
        j = i // 2
        m, token = fetch(i, "mix", x)
        mix_pre_w = with_token(w["norm_mix_pre"][i:i + 1], token)
        s = {"x": x, "mix_pre_w": mix_pre_w}
        if i % 2 == 0:
            u = _rms_fwd(x, mix_pre_w, out_dtype=BF16, name=f"l{i}_mix_pre")
            proj = _mm(u, m["ssd_w_in"], name=f"l{i}_ssd_in")
            xbc, xbc_pre = _ssd_conv_fwd(proj, d_inner, d_xbc, w["ssd_conv_w"][j], w["ssd_conv_b"][j:j + 1], n_seq,
                                         name=f"l{i}_ssd_conv")
            yn, y, hs = _ssd_fwd(proj, xbc, w["ssd_dt_bias"][j:j + 1], w["ssd_a_log"][j:j + 1], w["ssd_d"][j:j + 1],
                                 w["ssd_norm_w"][j:j + 1], n_seq, name=f"l{i}_ssd_scan")
            m_out, token = fetch(i, "out", yn)
            m = {**m, **m_out}
            mix = _mm(yn, m["ssd_w_out"], name=f"l{i}_ssd_out")
            s.update(u=u, proj=proj, xbc=xbc, xbc_pre=xbc_pre, yn=yn, y=y, hs=hs)
        else:
            u = _rms_fwd(x, mix_pre_w, out_dtype=F32, name=f"l{i}_mix_pre")
            mix = _pool_fwd(u, m["pool_w"], w["pool_scale"][j:j + 1], n_seq, name=f"l{i}_pool")
            s.update(u=u)
            token = jnp.zeros((8, LANE), F32)
        mix_post_w = with_token(w["norm_mix_post"][i:i + 1], token)
        m_ffn, token = fetch(i, "ffn", mix)
        m = {**m, **m_ffn}
        ffn_pre_w = with_token(w["norm_ffn_pre"][i:i + 1], token)
        x1, n = _res_rms_rms(x, mix, mix_post_w, ffn_pre_w, name=f"l{i}_mix_post_ffn_pre")
        h, hc, a = _ffn_up_act(n, m["ffn_w_up"], w["ffn_conv_w"][i], w["ffn_conv_b"][i:i + 1], n_seq,
                               name=f"l{i}_ffn_up_act")
        f = _mm(a, m["ffn_w_down"], name=f"l{i}_ffn_down")
        x = _res_rms_fwd(x1, f, w["norm_ffn_post"][i:i + 1], name=f"l{i}_ffn_post")
        s.update(mix=mix, x1=x1, n=n, h=h, hc=hc, a=a, f=f, m=m, ffn_pre_w=ffn_pre_w)
        saved.append(s)

    loss, dx = _loss_head(x, target)
    grads = {k: [None] * len(w[k]) for k in SMALL}
    token = jnp.zeros((8, LANE), F32)
    for i in reversed(range(depth)):
        j = i // 2
        s = saved[i]
        m, gm = s["m"], {}
        df, grads["norm_ffn_post"][i] = _rms_bwd(s["f"], with_token(w["norm_ffn_post"][i:i + 1], token), dx, None,
                                                 out_dtype=BF16, name=f"l{i}_ffn_post_b")
        gm["ffn_w_down"] = _mm(s["a"], df, ta=True, name=f"l{i}_ffn_down_bw")
        dh, grads["ffn_conv_w"][i], grads["ffn_conv_b"][i] = _ffn_down_bx_act_bwd(
            df, m["ffn_w_down"], s["h"], s["hc"], w["ffn_conv_w"][i], n_seq, name=f"l{i}_ffn_act_b")
        dn = _mm(dh, m["ffn_w_up"], tb=True, name=f"l{i}_ffn_up_bx")
        gm["ffn_w_up"] = _mm(s["n"], dh, ta=True, name=f"l{i}_ffn_up_bw")
        token = emit(i, "ffn", gm, dn)
        gm = {}
        dx1, dmix, grads["norm_ffn_pre"][i], grads["norm_mix_post"][i] = _rms_bwd2(
            s["x1"], s["ffn_pre_w"], dn, dx, s["mix"], with_token(w["norm_mix_post"][i:i + 1], token),
            out_dtype=BF16 if i % 2 == 0 else F32, name=f"l{i}_ffn_pre_mix_post_b")
        if i % 2 == 0:
            dyn = _mm(dmix, m["ssd_w_out"], tb=True, name=f"l{i}_ssd_out_bx")
            gm["ssd_w_out"] = _mm(s["yn"], dmix, ta=True, name=f"l{i}_ssd_out_bw")
            token = emit(i, "out", gm, dyn)
            gm = {}
            dxs, db, dc, dz, ddtr, dnw, dbias, dalog, ddsk = _ssd_bwd(
                s["proj"], s["xbc"], s["hs"], s["y"], dyn, w["ssd_dt_bias"][j:j + 1], w["ssd_a_log"][j:j + 1],
                w["ssd_d"][j:j + 1], with_token(w["ssd_norm_w"][j:j + 1], token), n_seq, name=f"l{i}_ssd_scan_b")
            grads["ssd_norm_w"][j], grads["ssd_dt_bias"][j], grads["ssd_a_log"][j], grads["ssd_d"][j] = (
                dnw, dbias, dalog, ddsk)
            dproj, grads["ssd_conv_w"][j], grads["ssd_conv_b"][j] = _ssd_conv_bwd(
                s["proj"], d_inner, w["ssd_conv_w"][j], s["xbc_pre"], (dxs, db, dc), dz, n_seq,
                name=f"l{i}_ssd_conv_b")
            dproj = _fill_cols(dproj, ddtr, d_inner + d_xbc, name=f"l{i}_ssd_dt_b")
            gm["ssd_w_in"] = _mm(s["u"], dproj, ta=True, name=f"l{i}_ssd_in_bw")
            token = emit(i, "mix", gm, dproj)
            du = _mm(dproj, m["ssd_w_in"], tb=True, name=f"l{i}_ssd_in_bx")
        else:
            du, gm["pool_w"], grads["pool_scale"][j] = _pool_bwd(
                s["u"], m["pool_w"], w["pool_scale"][j:j + 1], dmix, n_seq, name=f"l{i}_pool_b")
            token = emit(i, "mix", gm, du)
        dx, grads["norm_mix_pre"][i] = _rms_bwd(s["x"], with_token(s["mix_pre_w"], token), du, dx1,
                                                name=f"l{i}_mix_pre_b")
    return loss, dx, grads


BIG = (("ssd_w_in", 2), ("ssd_w_out", 1), ("pool_w", 2), ("ffn_w_up", 2), ("ffn_w_down", 1))
SMALL_SHARDED = (("ssd_conv_w", 2), ("ffn_conv_w", 2), ("pool_scale", 1))
SMALL = ("ssd_conv_w", "ssd_conv_b", "ssd_dt_bias", "ssd_a_log", "ssd_d", "ssd_norm_w", "pool_scale", "ffn_conv_w",
         "ffn_conv_b", "norm_mix_pre", "norm_mix_post", "norm_ffn_pre", "norm_ffn_post")
WEIGHTS = ("ssd_w_in", "ssd_conv_w", "ssd_conv_b", "ssd_dt_bias", "ssd_a_log", "ssd_d", "ssd_norm_w", "ssd_w_out",
           "pool_w", "pool_scale", "ffn_w_up", "ffn_conv_w", "ffn_conv_b", "ffn_w_down", "norm_mix_pre",
           "norm_mix_post", "norm_ffn_pre", "norm_ffn_post")


def _ssd_sizes(d_inner):
    return d_inner + 2 * N_SSD_GROUPS * D_STATE, d_inner // HEAD_DIM // N_SSD_GROUPS


def _small_compute_layout(full, d_inner):
    _, r_heads = _ssd_sizes(d_inner)
    w = {k: full[k] for k in SMALL}
    for k in ("ssd_dt_bias", "ssd_a_log", "ssd_d"):
        w[k] = _head_pad(full[k], r_heads)
    for k in ("ffn_conv_w", "ffn_conv_b"):
        w[k] = _interleave(full[k])
    return w


def _matmul_compute_layout(k, full, d_inner):
    d_xbc, r_heads = _ssd_sizes(d_inner)
    if k == "ssd_w_in":
        return _ssd_w_in_layout(full, d_inner, d_xbc, r_heads)
    if k == "ffn_w_up":
        return _interleave(full)
    return full


def _layer_matrices(i, part):
    if part == "ffn":
        return (("ffn_w_up", 1, i), ("ffn_w_down", 0, i))
    if i % 2 == 1:
        return (("pool_w", 1, i // 2),) if part == "mix" else ()
    return (("ssd_w_in", 1, i // 2),) if part == "mix" else (("ssd_w_out", 0, i // 2),)


def _fetch_group(i, part):
    mix, out, ffn = (_layer_matrices(i, p) for p in ("mix", "out", "ffn"))
    if i % 2 == 1:
        return mix + ffn if part == "mix" else ()
    if i == 0:
        return {"mix": mix, "out": out + ffn, "ffn": ()}[part]
    return {"mix": mix + out, "out": (), "ffn": ffn}[part]


def _matmul_grad_reference_layout(k, g, d_inner):
    d_xbc, r_heads = _ssd_sizes(d_inner)
    if k == "ssd_w_in":
        return _ssd_w_in_unlayout(g, d_inner, d_xbc, r_heads)
    if k == "ffn_w_up":
        return _deinterleave(g)
    return g


def _small_grads_reference_layout(grads, shapes, d_inner):
    _, r_heads = _ssd_sizes(d_inner)
    g = {k: jnp.stack(grads[k]) for k in SMALL}
    for k in ("ssd_dt_bias", "ssd_a_log", "ssd_d"):
        g[k] = _head_unpad(g[k][:, 0], r_heads)
    for k in ("ffn_conv_w", "ffn_conv_b"):
        g[k] = _deinterleave(g[k])
    return {k: v.reshape(shapes[k]) for k, v in g.items()}


def kernel(x, ssd_w_in, ssd_conv_w, ssd_conv_b, ssd_dt_bias, ssd_a_log, ssd_d, ssd_norm_w, ssd_w_out, pool_w, pool_scale, ffn_w_up, ffn_conv_w, ffn_conv_b, ffn_w_down, norm_mix_pre, norm_mix_post, norm_ffn_pre, norm_ffn_post, loss_target, m_ssd_w_in, m_ssd_conv_w, m_ssd_conv_b, m_ssd_dt_bias, m_ssd_a_log, m_ssd_d, m_ssd_norm_w, m_ssd_w_out, m_pool_w, m_pool_scale, m_ffn_w_up, m_ffn_conv_w, m_ffn_conv_b, m_ffn_w_down, m_norm_mix_pre, m_norm_mix_post, m_norm_ffn_pre, m_norm_ffn_post, v_ssd_w_in, v_ssd_conv_w, v_ssd_conv_b, v_ssd_dt_bias, v_ssd_a_log, v_ssd_d, v_ssd_norm_w, v_ssd_w_out, v_pool_w, v_pool_scale, v_ffn_w_up, v_ffn_conv_w, v_ffn_conv_b, v_ffn_w_down, v_norm_mix_pre, v_norm_mix_post, v_norm_ffn_pre, v_norm_ffn_post):
    shards = dict(ssd_w_in=ssd_w_in, ssd_conv_w=ssd_conv_w, ssd_conv_b=ssd_conv_b, ssd_dt_bias=ssd_dt_bias,
                  ssd_a_log=ssd_a_log, ssd_d=ssd_d, ssd_norm_w=ssd_norm_w, ssd_w_out=ssd_w_out, pool_w=pool_w,
                  pool_scale=pool_scale, ffn_w_up=ffn_w_up, ffn_conv_w=ffn_conv_w, ffn_conv_b=ffn_conv_b,
                  ffn_w_down=ffn_w_down, norm_mix_pre=norm_mix_pre, norm_mix_post=norm_mix_post,
                  norm_ffn_pre=norm_ffn_pre, norm_ffn_post=norm_ffn_post)
    moments_m = dict(zip(WEIGHTS, (m_ssd_w_in, m_ssd_conv_w, m_ssd_conv_b, m_ssd_dt_bias, m_ssd_a_log, m_ssd_d, m_ssd_norm_w, m_ssd_w_out, m_pool_w, m_pool_scale, m_ffn_w_up, m_ffn_conv_w, m_ffn_conv_b, m_ffn_w_down, m_norm_mix_pre, m_norm_mix_post, m_norm_ffn_pre, m_norm_ffn_post)))
    moments_v = dict(zip(WEIGHTS, (v_ssd_w_in, v_ssd_conv_w, v_ssd_conv_b, v_ssd_dt_bias, v_ssd_a_log, v_ssd_d, v_ssd_norm_w, v_ssd_w_out, v_pool_w, v_pool_scale, v_ffn_w_up, v_ffn_conv_w, v_ffn_conv_b, v_ffn_w_down, v_norm_mix_pre, v_norm_mix_post, v_norm_ffn_pre, v_norm_ffn_post)))
    n_seq, seq, d_model = x.shape
    t = n_seq * seq

    d_inner = ssd_norm_w.shape[1]
    depth = norm_mix_pre.shape[0]
    x2 = x.reshape(t, d_model)

    shard16 = {k: shards[k].astype(BF16) for k, _ in BIG}
    order = [(i, part) for i in range(depth) for part in ("mix", "out", "ffn") if _fetch_group(i, part)]
    fetches = {}

    def start_fetch(key, after):
        srcs = [shard16[k][l] for k, _, l in _fetch_group(*key)]
        fetches[key] = _push_start(srcs, False, after, name=f"fetch{key[0]}{key[1]}_start")

    full = dict(shards)
    small_all = _all_gather(_pack_rows([shards[k] for k, _ in SMALL_SHARDED]), name="gather_small_weights")
    small_stacked = _unpack_rows(small_all, [shards[k].shape for k, _ in SMALL_SHARDED], lead=(N_DEV,))
    for (k, axis), st in zip(SMALL_SHARDED, small_stacked):
        full[k] = _unshard(st, axis)
    w = _small_compute_layout(full, d_inner)
    ready = {}

    def fetch(i, part, x_now):
        key = (i, part)
        token = jnp.zeros((8, LANE), F32)
        if key in order:
            if key == order[0]:
                wholes = [_unshard(_all_gather(shard16[k][l], name=f"fetch0_{k}"), axis) for k, axis, l in _fetch_group(i, part)]
                nxt_after = wholes[0]
            else:
                send, recv, srcs, lands, _ = fetches[key]
                lands = _push_wait(send, recv, srcs, lands, False, x_now, name=f"fetch{i}{part}_wait")
                wholes = [_unshard(_with_own_slot(land, shard16[k][l]), axis)
                          for (k, axis, l), land in zip(_fetch_group(i, part), lands)]
                nxt_after = lands[0]
            for (k, _, l), whole in zip(_fetch_group(i, part), wholes):
                ready[k, l] = _matmul_compute_layout(k, whole, d_inner)
            nxt = order.index(key) + 1
            if nxt < len(order):
                start_fetch(order[nxt], nxt_after)
                token = fetches[order[nxt]][4]
        return {k: ready[k, l] for k, _, l in _layer_matrices(i, part)}, token

    g_layers = {}
    in_flight = []

    def finish_exchange(after):
        key, blocks, (send, recv, srcs, lands, _) = in_flight.pop(0)
        lands = _push_wait(send, recv, srcs, lands, True, after, name=f"exchange{key[0]}{key[1]}_wait")
        for (k, _, l), land, block in zip(_layer_matrices(*key), lands, blocks):
            own = lax.dynamic_index_in_dim(block, _my_index(), 0, keepdims=False)
            g_layers[k, l] = _sum_slots(land, own, name=f"sum{key[0]}_{k}")

    def emit(i, part, gm, dx_now):
        if len(in_flight) >= 2:
            finish_exchange(dx_now)
        blocks = [_shard_major(_matmul_grad_reference_layout(k, gm[k].astype(BF16), d_inner), axis)
                  for k, axis, _ in _layer_matrices(i, part)]
        started = _push_start(blocks, True, dx_now, name=f"exchange{i}{part}_start")
        in_flight.append(((i, part), blocks, started))
        return started[4]

    loss, dx, grads = _local_step(x2, loss_target.reshape(t, d_model), w, fetch, emit, n_seq)
    loss = lax.psum(loss, ("x", "y", "c"))

    g_shard = {}
    small_shapes = {k: full[k].shape for k in SMALL}
    g_small = _small_grads_reference_layout(grads, small_shapes, d_inner)
    s_all = _all_gather(_pack_rows([g_small[k] for k in SMALL]) + in_flight[-1][2][4][0:1, :], name="gather_small_grads")
    for k, g in zip(SMALL, _unpack_rows(_sum_slots(s_all, name="sum_small_grads"), [small_shapes[k] for k in SMALL])):
        g_shard[k] = g
    for k, axis in SMALL_SHARDED:
        g_shard[k] = _my_shard(g_shard[k], axis)

    last = [k for key, _, _ in in_flight for k, _, _ in _layer_matrices(*key)]
    deltas, new_m, new_v = {}, {}, {}
    for k in [k for k in WEIGHTS if k not in last] + last:
        if k == last[0]:
            while in_flight:
                finish_exchange(deltas["ffn_w_up"])
        if k in dict(BIG):
            g_shard[k] = jnp.stack([g_layers[k, l] for l in range(shards[k].shape[0])])
        deltas[k], new_m[k], new_v[k] = _adamw(shards[k], g_shard[k], moments_m[k], moments_v[k], name=f"adamw_{k}")
    return (loss, dx.reshape(x.shape), *[g_shard[k] for k in WEIGHTS], *[deltas[k] for k in WEIGHTS],
            *[new_m[k] for k in WEIGHTS], *[new_v[k] for k in WEIGHTS])
```

```python
import functools

import jax
import jax.numpy as jnp
from jax import lax
from jax.experimental import pallas as pl
from jax.experimental.pallas import tpu as pltpu

F32 = jnp.float32
BF16 = jnp.bfloat16

N_DEV = 8
HEAD_DIM = 64
N_SSD_GROUPS = 4
D_STATE = 128
CHUNK = 128
POOL_WINDOWS = (2, 4, 8, 16)
EPS = 1e-6
LANE = 128
ADAM_LR = 0.001
ADAM_B1 = 0.9
ADAM_B2 = 0.999
ADAM_EPS = 1e-08
ADAM_WD = 0.01
ADAM_STEP = 10
VMEM_LIMIT = 56 * 1024 * 1024
ANY = pl.BlockSpec(memory_space=pl.ANY)


def _pick(n, cands):
    for c in cands:
        if n % c == 0:
            return c
    return n


def _params(sem):
    return pltpu.CompilerParams(dimension_semantics=sem, vmem_limit_bytes=VMEM_LIMIT)


def _sigmoid(x):
    return 0.5 * jnp.tanh(0.5 * x) + 0.5


def _silu(x):
    return x * _sigmoid(x)


def _dsilu(x):
    s = _sigmoid(x)
    return s * (1.0 + x * (1.0 - s))


def _shift_down(x, s):
    rows = lax.broadcasted_iota(jnp.int32, x.shape, 0)
    return jnp.where(rows >= s, pltpu.roll(x, s, 0), 0.0)


def _shift_up(x, s):
    n = x.shape[0]
    rows = lax.broadcasted_iota(jnp.int32, x.shape, 0)
    return jnp.where(rows < n - s, pltpu.roll(x, n - s, 0), 0.0)


MM_VMEM_BUDGET = 40 * 1024 * 1024
MM_STEP_BYTES = 1_300_000
MM_SUB = 512


def _mm_tiles(m, n, k, a_bytes, b_bytes, o_bytes):
    def cands(dim, sizes):
        out = [s for s in sizes if s <= dim and dim % s == 0]
        return out or [dim]

    best = None
    for tm in cands(m, (m, m // 2, 2048, 1024, 512, 256, 128)):
        if tm % LANE:
            continue
        for tn in cands(n, (n, n // 2, n // 4, 2048, 1024, 512, 256, 128)):
            if tn % (2 * LANE) and tn != n:
                continue
            for tk in cands(k, (k, k // 2, 2048, 1024, 512)):
                if tk % LANE:
                    continue
                nk = k // tk
                acc = tm * tn * 4 if (nk > 1 and o_bytes != 4) else 0
                temps = tm * min(tn, MM_SUB) * 4 + (tm * tk * 2 if a_bytes == 4 else 0) + (tk * tn * 2 if b_bytes == 4 else 0)
                vmem = 2 * (tm * tk * a_bytes + tk * tn * b_bytes + tm * tn * o_bytes) + acc + temps
                if vmem > MM_VMEM_BUDGET:
                    continue
                steps = (m // tm) * (n // tn) * nk
                cost = (m * k * a_bytes * (n // tn) + k * n * b_bytes * (m // tm) + m * n * o_bytes
                        + steps * MM_STEP_BYTES)
                if best is None or cost < best[0]:
                    best = (cost, tm, tn, tk)
    return best[1:]


def _mm(a, b, *, ta=False, tb=False, out_dtype=F32, name="mm"):
    m, k = (a.shape[1], a.shape[0]) if ta else a.shape
    n = b.shape[0] if tb else b.shape[1]
    o_bytes = jnp.dtype(out_dtype).itemsize
    tm, tn, tk = _mm_tiles(m, n, k, a.dtype.itemsize, b.dtype.itemsize, o_bytes)
    nk = k // tk
    sub = _pick(tn, (MM_SUB, 256))
    use_acc = nk > 1 and o_bytes != 4
    a_spec = pl.BlockSpec((tk, tm), lambda i, j, kk: (kk, i)) if ta else pl.BlockSpec((tm, tk), lambda i, j, kk: (i, kk))
    b_spec = pl.BlockSpec((tn, tk), lambda i, j, kk: (j, kk)) if tb else pl.BlockSpec((tk, tn), lambda i, j, kk: (kk, j))
    dims = (((1,), (1 if tb else 0,)), ((), ()))

    def body(a_ref, b_ref, o_ref, *scratch):
        kk = pl.program_id(2)
        acc_ref = scratch[0] if use_acc else o_ref
        if nk > 1:
            @pl.when(kk == 0)
            def _():
                acc_ref[...] = jnp.zeros_like(acc_ref)

        av = a_ref[...].astype(BF16)
        if ta:
            av = av.T
        for s in range(tn // sub):
            cols = slice(s * sub, (s + 1) * sub)
            bv = (b_ref[cols, :] if tb else b_ref[:, cols]).astype(BF16)
            part = lax.dot_general(av, bv, dims, preferred_element_type=F32)
            if nk == 1:
                o_ref[:, cols] = part.astype(out_dtype)
            else:
                acc_ref[:, cols] += part
        if use_acc:
            @pl.when(kk == nk - 1)
            def _():
                o_ref[...] = acc_ref[...].astype(out_dtype)

    return pl.pallas_call(
        body,
        name=name,
        grid=(m // tm, n // tn, nk),
        in_specs=[a_spec, b_spec],
        out_specs=pl.BlockSpec((tm, tn), lambda i, j, kk: (i, j)),
        out_shape=jax.ShapeDtypeStruct((m, n), out_dtype),
        scratch_shapes=[pltpu.VMEM((tm, tn), F32)] if use_acc else [],
        compiler_params=_params(("parallel", "parallel", "arbitrary")),
    )(a, b)


def _rms_fwd(x, w, *, out_dtype, name):
    t, d = x.shape
    tm = _pick(t, (512, 256, 128))

    def body(x_ref, w_ref, o_ref):
        xv = x_ref[...]
        rstd = lax.rsqrt(jnp.mean(xv * xv, axis=-1, keepdims=True) + EPS)
        o_ref[...] = (xv * rstd * w_ref[...]).astype(out_dtype)

    return pl.pallas_call(
        body,
        name=name,
        grid=(t // tm,),
        in_specs=[pl.BlockSpec((tm, d), lambda i: (i, 0)), pl.BlockSpec((1, d), lambda i: (0, 0))],
        out_specs=pl.BlockSpec((tm, d), lambda i: (i, 0)),
        out_shape=jax.ShapeDtypeStruct((t, d), out_dtype),
        compiler_params=_params(("parallel",)),
    )(x, w)


def _res_rms_fwd(x, f, w, *, name):
    t, d = x.shape
    tm = _pick(t, (512, 256, 128))

    def body(x_ref, f_ref, w_ref, o_ref):
        fv = f_ref[...]
        rstd = lax.rsqrt(jnp.mean(fv * fv, axis=-1, keepdims=True) + EPS)
        o_ref[...] = x_ref[...] + fv * rstd * w_ref[...]

    row = pl.BlockSpec((tm, d), lambda i: (i, 0))
    return pl.pallas_call(
        body,
        name=name,
        grid=(t // tm,),
        in_specs=[row, row, pl.BlockSpec((1, d), lambda i: (0, 0))],
        out_specs=row,
        out_shape=jax.ShapeDtypeStruct((t, d), F32),
        compiler_params=_params(("parallel",)),
    )(x, f, w)


def _rms_bwd(x, w, dy, resid, *, out_dtype=F32, name):
    t, d = x.shape
    tm = _pick(t, (512, 256, 128))
    has_res = resid is not None

    def body(*refs):
        if has_res:
            x_ref, w_ref, dy_ref, r_ref, dx_ref, dw_ref = refs
        else:
            x_ref, w_ref, dy_ref, dx_ref, dw_ref = refs
        xv = x_ref[...]
        dyv = dy_ref[...].astype(F32)
        rstd = lax.rsqrt(jnp.mean(xv * xv, axis=-1, keepdims=True) + EPS)
        xn = xv * rstd
        g = dyv * w_ref[...]
        dx = rstd * (g - xn * jnp.mean(g * xn, axis=-1, keepdims=True))
        if has_res:
            dx = dx + r_ref[...]
        dx_ref[...] = dx.astype(out_dtype)
        part = jnp.sum(dyv * xn, axis=0, keepdims=True)

        @pl.when(pl.program_id(0) == 0)
        def _():
            dw_ref[...] = part

        @pl.when(pl.program_id(0) > 0)
        def _():
            dw_ref[...] += part

    row = pl.BlockSpec((tm, d), lambda i: (i, 0))
    vec = pl.BlockSpec((1, d), lambda i: (0, 0))
    ins = [x, w, dy] + ([resid] if has_res else [])
    return pl.pallas_call(
        body,
        name=name,
        grid=(t // tm,),
        in_specs=[row, vec, row] + ([row] if has_res else []),
        out_specs=[row, vec],
        out_shape=[jax.ShapeDtypeStruct((t, d), out_dtype), jax.ShapeDtypeStruct((1, d), F32)],
        compiler_params=_params(("arbitrary",)),
    )(*ins)


def _res_rms_rms(x, f, w_post, w_pre, *, out_dtype=BF16, name):
    t, d = x.shape
    tm = _pick(t, (512, 256, 128))

    def body(x_ref, f_ref, wp_ref, wn_ref, x1_ref, n_ref):
        fv = f_ref[...]
        x1 = x_ref[...] + fv * lax.rsqrt(jnp.mean(fv * fv, axis=-1, keepdims=True) + EPS) * wp_ref[...]
        x1_ref[...] = x1
        n_ref[...] = (x1 * lax.rsqrt(jnp.mean(x1 * x1, axis=-1, keepdims=True) + EPS) * wn_ref[...]).astype(out_dtype)

    row = pl.BlockSpec((tm, d), lambda i: (i, 0))
    vec = pl.BlockSpec((1, d), lambda i: (0, 0))
    return pl.pallas_call(
        body,
        name=name,
        grid=(t // tm,),
        in_specs=[row, row, vec, vec],
        out_specs=[row, row],
        out_shape=[jax.ShapeDtypeStruct((t, d), F32), jax.ShapeDtypeStruct((t, d), out_dtype)],
        compiler_params=_params(("parallel",)),
    )(x, f, w_post, w_pre)


def _rms_bwd2(xa, wa, dya, resid, xb, wb, *, out_dtype, name):
    t, d = xa.shape
    tm = _pick(t, (512, 256, 128))

    def norm_bwd(xv, w, dyv):
        rstd = lax.rsqrt(jnp.mean(xv * xv, axis=-1, keepdims=True) + EPS)
        xn = xv * rstd
        g = dyv * w
        return rstd * (g - xn * jnp.mean(g * xn, axis=-1, keepdims=True)), jnp.sum(dyv * xn, axis=0, keepdims=True)

    def body(xa_ref, wa_ref, dya_ref, r_ref, xb_ref, wb_ref, da_ref, db_ref, dwa_ref, dwb_ref):
        da, dwa = norm_bwd(xa_ref[...], wa_ref[...], dya_ref[...].astype(F32))
        da = da + r_ref[...]
        da_ref[...] = da
        db, dwb = norm_bwd(xb_ref[...], wb_ref[...], da)
        db_ref[...] = db.astype(out_dtype)

        @pl.when(pl.program_id(0) == 0)
        def _():
            dwa_ref[...] = dwa
            dwb_ref[...] = dwb

        @pl.when(pl.program_id(0) > 0)
        def _():
            dwa_ref[...] += dwa
            dwb_ref[...] += dwb

    row = pl.BlockSpec((tm, d), lambda i: (i, 0))
    vec = pl.BlockSpec((1, d), lambda i: (0, 0))
    return pl.pallas_call(
        body,
        name=name,
        grid=(t // tm,),
        in_specs=[row, vec, row, row, row, vec],
        out_specs=[row, row, vec, vec],
        out_shape=[jax.ShapeDtypeStruct((t, d), F32), jax.ShapeDtypeStruct((t, d), out_dtype),
                   jax.ShapeDtypeStruct((1, d), F32), jax.ShapeDtypeStruct((1, d), F32)],
        compiler_params=_params(("arbitrary",)),
    )(xa, wa, dya, resid, xb, wb)


def _loss_head(y, target, *, name="loss_head"):
    t, d = y.shape
    tm = _pick(t, (512, 256, 128))

    def body(y_ref, t_ref, dy_ref, l_ref):
        err = y_ref[...] - t_ref[...]
        dy_ref[...] = err * (1.0 / d)
        part = jnp.sum(jnp.sum(err * err, axis=-1, keepdims=True), axis=0, keepdims=True) * (0.5 / d)
        part = jnp.broadcast_to(part, (1, LANE))

        @pl.when(pl.program_id(0) == 0)
        def _():
            l_ref[...] = part

        @pl.when(pl.program_id(0) > 0)
        def _():
            l_ref[...] += part

    row = pl.BlockSpec((tm, d), lambda i: (i, 0))
    dy, l = pl.pallas_call(
        body,
        name=name,
        grid=(t // tm,),
        in_specs=[row, row],
        out_specs=[row, pl.BlockSpec((1, LANE), lambda i: (0, 0))],
        out_shape=[jax.ShapeDtypeStruct((t, d), F32), jax.ShapeDtypeStruct((1, LANE), F32)],
        compiler_params=_params(("arbitrary",)),
    )(y, target)
    return l[0, 0], dy


def _conv_taps(h, w_ref, k_taps):
    out = h * w_ref[k_taps - 1:k_taps, :]
    for k in range(k_taps - 1):
        out = out + _shift_down(h, k_taps - 1 - k) * w_ref[k:k + 1, :]
    return out


def _conv_taps_bwd(h, dhc, w_ref, k_taps):
    dh = dhc * w_ref[k_taps - 1:k_taps, :]
    dws = []
    for k in range(k_taps - 1):
        up = _shift_up(dhc, k_taps - 1 - k)
        dh = dh + up * w_ref[k:k + 1, :]
        dws.append(jnp.sum(up * h, axis=0, keepdims=True))
    dws.append(jnp.sum(dhc * h, axis=0, keepdims=True))
    return dh, jnp.concatenate(dws, axis=0)


FFN_TC = 256


def _interleave(w, tc=FFN_TC):
    f = w.shape[-1] // 2
    tiles = []
    for j in range(f // tc):
        tiles += [w[..., j * tc:(j + 1) * tc], w[..., f + j * tc:f + (j + 1) * tc]]
    return jnp.concatenate(tiles, axis=-1)


def _deinterleave(w, tc=FFN_TC):
    n_tiles = w.shape[-1] // tc
    return jnp.concatenate([w[..., j * tc:(j + 1) * tc] for j in list(range(0, n_tiles, 2)) + list(range(1, n_tiles, 2))],
                           axis=-1)


FFN_ROWS = 512
HALO = 8


def _ffn_up_act(n, w_up, conv_w, conv_b, n_seq, *, name):
    t, d = n.shape
    f2 = w_up.shape[1]
    seq = t // n_seq
    tc = FFN_TC
    nj = f2 // (2 * tc)
    k_taps = conv_w.shape[0]
    rows = min(FFN_ROWS, seq)

    def body(n_ref, wu_ref, w_ref, b_ref, h_ref, hc_ref, o_ref, h_scr):
        h_scr[0:HALO, :] = jnp.zeros((HALO, 2 * tc), F32)
        wu = wu_ref[...]
        for r in range(seq // rows):
            chunk = slice(r * rows, (r + 1) * rows)
            h = jnp.dot(n_ref[chunk, :], wu, preferred_element_type=F32)
            h_scr[HALO + r * rows:HALO + (r + 1) * rows, :] = h
            h_ref[chunk, :] = h.astype(BF16)
            ext = h_scr[r * rows:HALO + (r + 1) * rows, :]
            hc = ext * w_ref[k_taps - 1:k_taps, :]
            for k in range(k_taps - 1):
                hc = hc + pltpu.roll(ext, k_taps - 1 - k, 0) * w_ref[k:k + 1, :]
            hc = hc[HALO:, :] + b_ref[...]
            hc_ref[chunk, :] = hc.astype(BF16)
            o_ref[chunk, :] = (_silu(hc[:, :tc]) * hc[:, tc:]).astype(BF16)

    wide = pl.BlockSpec((seq, 2 * tc), lambda b, j: (b, j))
    return pl.pallas_call(
        body,
        name=name,
        grid=(n_seq, nj),
        in_specs=[
            pl.BlockSpec((seq, d), lambda b, j: (b, 0)),
            pl.BlockSpec((d, 2 * tc), lambda b, j: (0, j)),
            pl.BlockSpec((k_taps, 2 * tc), lambda b, j: (0, j)),
            pl.BlockSpec((1, 2 * tc), lambda b, j: (0, j)),
        ],
        out_specs=[wide, wide, pl.BlockSpec((seq, tc), lambda b, j: (b, j))],
        out_shape=[jax.ShapeDtypeStruct((t, f2), BF16), jax.ShapeDtypeStruct((t, f2), BF16),
                   jax.ShapeDtypeStruct((t, f2 // 2), BF16)],
        scratch_shapes=[pltpu.VMEM((HALO + seq, 2 * tc), F32)],
        compiler_params=_params(("parallel", "arbitrary")),
    )(n, w_up, conv_w, conv_b)


def _ffn_down_bx_act_bwd(df, w_down, h, hc, conv_w, n_seq, *, name):
    t, d = df.shape
    f2 = h.shape[1]
    seq = t // n_seq
    tc = FFN_TC
    nj = f2 // (2 * tc)
    k_taps = conv_w.shape[0]

    def body(df_ref, wd_ref, h_ref, hc_ref, w_ref, dh_ref, dw_ref, db_ref):
        dav = lax.dot_general(df_ref[...], wd_ref[...], (((1,), (1,)), ((), ())), preferred_element_type=F32)
        hcv = hc_ref[...].astype(F32)
        gate, val = hcv[:, :tc], hcv[:, tc:]
        dhc = jnp.concatenate([dav * val * _dsilu(gate), dav * _silu(gate)], axis=1)
        dh, dw = _conv_taps_bwd(h_ref[...].astype(F32), dhc, w_ref, k_taps)
        dh_ref[...] = dh.astype(BF16)
        dw_ref[0] = dw
        db_ref[0] = jnp.sum(dhc, axis=0, keepdims=True)

    wide = pl.BlockSpec((seq, 2 * tc), lambda b, j: (b, j))
    dh, dw, db = pl.pallas_call(
        body,
        name=name,
        grid=(n_seq, nj),
        in_specs=[
            pl.BlockSpec((seq, d), lambda b, j: (b, 0)),
            pl.BlockSpec((tc, d), lambda b, j: (j, 0)),
            wide, wide,
            pl.BlockSpec((k_taps, 2 * tc), lambda b, j: (0, j)),
        ],
        out_specs=[
            wide,
            pl.BlockSpec((1, k_taps, 2 * tc), lambda b, j: (b, 0, j)),
            pl.BlockSpec((1, 1, 2 * tc), lambda b, j: (b, 0, j)),
        ],
        out_shape=[
            jax.ShapeDtypeStruct((t, f2), BF16),
            jax.ShapeDtypeStruct((n_seq, k_taps, f2), F32),
            jax.ShapeDtypeStruct((n_seq, 1, f2), F32),
        ],
        compiler_params=_params(("parallel", "arbitrary")),
    )(df, w_down, h, hc, conv_w)
    return dh, jnp.sum(dw, axis=0), jnp.sum(db, axis=0)


def _window_mixed(u, window):
    s = u
    step = 1
    while step < window:
        s = s + _shift_down(s, step)
        step *= 2
    rows = lax.broadcasted_iota(jnp.int32, u.shape, 0)
    inv_cnt = 1.0 / jnp.minimum(rows + 1, window).astype(F32)
    return s * inv_cnt - u, inv_cnt


def _window_mixed_bwd(dmixed, inv_cnt, window):
    r = dmixed * inv_cnt
    s = r
    step = 1
    while step < window:
        s = s + _shift_up(s, step)
        step *= 2
    return s - dmixed


def _pool_fwd(u, w, scale, n_seq, *, name):
    t, d = u.shape
    seq = t // n_seq
    n_g, dg, _ = w.shape

    def body(u_ref, w_ref, s_ref, o_ref):
        for k, window in enumerate(POOL_WINDOWS):
            @pl.when(pl.program_id(1) == k)
            def _(window=window):
                mixed, _ = _window_mixed(u_ref[...], window)
                pre = jnp.dot(mixed.astype(BF16), w_ref[0].astype(BF16), preferred_element_type=F32)
                o_ref[...] = pre * s_ref[...]

    return pl.pallas_call(
        body,
        name=name,
        grid=(n_seq, n_g),
        in_specs=[
            pl.BlockSpec((seq, dg), lambda b, g: (b, g)),
            pl.BlockSpec((1, dg, dg), lambda b, g: (g, 0, 0)),
            pl.BlockSpec((1, dg), lambda b, g: (0, g)),
        ],
        out_specs=pl.BlockSpec((seq, dg), lambda b, g: (b, g)),
        out_shape=jax.ShapeDtypeStruct((t, d), F32),
        compiler_params=_params(("parallel", "parallel")),
    )(u, w, scale)


def _pool_bwd(u, w, scale, dout, n_seq, *, name):
    t, d = u.shape
    seq = t // n_seq
    n_g, dg, _ = w.shape

    def body(u_ref, w_ref, s_ref, do_ref, du_ref, dw_ref, ds_ref):
        group = pl.program_id(0)
        first = pl.program_id(1) == 0
        for k, window in enumerate(POOL_WINDOWS):
            @pl.when(group == k)
            def _(window=window):
                mixed, inv_cnt = _window_mixed(u_ref[...], window)
                mixed_b = mixed.astype(BF16)
                w_b = w_ref[0].astype(BF16)
                dov = do_ref[...]
                pre = jnp.dot(mixed_b, w_b, preferred_element_type=F32)
                dsc = jnp.sum(dov * pre, axis=0, keepdims=True)
                dpre = (dov * s_ref[...]).astype(BF16)
                dw = lax.dot_general(mixed_b, dpre, (((0,), (0,)), ((), ())), preferred_element_type=F32)
                dmixed = lax.dot_general(dpre, w_b, (((1,), (1,)), ((), ())), preferred_element_type=F32)
                du_ref[...] = _window_mixed_bwd(dmixed, inv_cnt, window)

                @pl.when(first)
                def _():
                    dw_ref[0] = dw
                    ds_ref[...] = dsc

                @pl.when(jnp.logical_not(first))
                def _():
                    dw_ref[0] += dw
                    ds_ref[...] += dsc

    return pl.pallas_call(
        body,
        name=name,
        grid=(n_g, n_seq),
        in_specs=[
            pl.BlockSpec((seq, dg), lambda g, b: (b, g)),
            pl.BlockSpec((1, dg, dg), lambda g, b: (g, 0, 0)),
            pl.BlockSpec((1, dg), lambda g, b: (0, g)),
            pl.BlockSpec((seq, dg), lambda g, b: (b, g)),
        ],
        out_specs=[
            pl.BlockSpec((seq, dg), lambda g, b: (b, g)),
            pl.BlockSpec((1, dg, dg), lambda g, b: (g, 0, 0)),
            pl.BlockSpec((1, dg), lambda g, b: (0, g)),
        ],
        out_shape=[
            jax.ShapeDtypeStruct((t, d), F32),
            jax.ShapeDtypeStruct((n_g, dg, dg), F32),
            jax.ShapeDtypeStruct((1, d), F32),
        ],
        compiler_params=_params(("parallel", "arbitrary")),
    )(u, w, scale, dout)


def _adamw(w, g, m, v, *, name):
    shape = w.shape
    c = shape[-1]
    r = w.size // c
    tm = _pick(r, (512, 256, 128, 64, 32, 16, 8))

    def body(w_ref, g_ref, m_ref, v_ref, d_ref, nm_ref, nv_ref):
        gv = g_ref[...]
        nm = ADAM_B1 * m_ref[...] + (1.0 - ADAM_B1) * gv
        nv = ADAM_B2 * v_ref[...] + (1.0 - ADAM_B2) * (gv * gv)
        m_hat = nm / (1.0 - ADAM_B1 ** ADAM_STEP)
        v_hat = nv / (1.0 - ADAM_B2 ** ADAM_STEP)
        d_ref[...] = -ADAM_LR * (m_hat / (jnp.sqrt(v_hat) + ADAM_EPS) + ADAM_WD * w_ref[...])
        nm_ref[...] = nm
        nv_ref[...] = nv

    blk = pl.BlockSpec((tm, c), lambda i: (i, 0))
    out = jax.ShapeDtypeStruct((r, c), F32)
    res = pl.pallas_call(
        body,
        name=name,
        grid=(r // tm,),
        in_specs=[blk] * 4,
        out_specs=[blk] * 3,
        out_shape=[out] * 3,
        compiler_params=_params(("parallel",)),
    )(w.reshape(r, c), g.reshape(r, c), m.reshape(r, c), v.reshape(r, c))
    return tuple(a.reshape(shape) for a in res)


CONV_TC = 256


def _ssd_conv_fwd(proj, col0, n_cols, conv_w, conv_b, n_seq, *, name):
    t = proj.shape[0]
    seq = t // n_seq
    tc = CONV_TC
    off = col0 // tc
    k_taps = conv_w.shape[0]

    def body(h_ref, w_ref, b_ref, o_ref, pre_ref):
        pre = _conv_taps(h_ref[...], w_ref, k_taps) + b_ref[...]
        pre_ref[...] = pre.astype(BF16)
        o_ref[...] = _silu(pre)

    return pl.pallas_call(
        body,
        name=name,
        grid=(n_seq, n_cols // tc),
        in_specs=[
            pl.BlockSpec((seq, tc), lambda b, j: (b, j + off)),
            pl.BlockSpec((k_taps, tc), lambda b, j: (0, j)),
            pl.BlockSpec((1, tc), lambda b, j: (0, j)),
        ],
        out_specs=[pl.BlockSpec((seq, tc), lambda b, j: (b, j))] * 2,
        out_shape=[jax.ShapeDtypeStruct((t, n_cols), F32), jax.ShapeDtypeStruct((t, n_cols), BF16)],
        compiler_params=_params(("parallel", "parallel")),
    )(proj, conv_w, conv_b)


def _ssd_conv_bwd(proj, col0, conv_w, pre, dparts, dproj, n_seq, *, name):
    t = proj.shape[0]
    seq = t // n_seq
    tc = CONV_TC
    off = col0 // tc
    k_taps = conv_w.shape[0]
    widths = [d.shape[1] // tc for d in dparts]
    starts = [sum(widths[:i]) for i in range(len(widths))]
    n_blocks = sum(widths)
    n_parts = len(dparts)

    def body(h_ref, w_ref, pre_ref, *rest):
        part_refs = rest[:n_parts]
        dh_ref, dw_ref, db_ref = rest[n_parts + 1:]
        j = pl.program_id(0)
        da = part_refs[-1][...]
        for i in reversed(range(n_parts - 1)):
            da = jnp.where(j < starts[i + 1], part_refs[i][...], da)
        dhc = da * _dsilu(pre_ref[...].astype(F32))
        dh, dw = _conv_taps_bwd(h_ref[...], dhc, w_ref, k_taps)
        dh_ref[...] = dh.astype(BF16)
        db = jnp.sum(dhc, axis=0, keepdims=True)

        @pl.when(pl.program_id(1) == 0)
        def _():
            dw_ref[...] = dw
            db_ref[...] = db

        @pl.when(pl.program_id(1) > 0)
        def _():
            dw_ref[...] += dw
            db_ref[...] += db

    def part_spec(start, width):
        return pl.BlockSpec((seq, tc), lambda j, b: (b, jnp.clip(j - start, 0, width - 1)))

    n_cols = n_blocks * tc
    return pl.pallas_call(
        body,
        name=name,
        grid=(n_blocks, n_seq),
        in_specs=[
            pl.BlockSpec((seq, tc), lambda j, b: (b, j + off)),
            pl.BlockSpec((k_taps, tc), lambda j, b: (0, j)),
            pl.BlockSpec((seq, tc), lambda j, b: (b, j)),
        ] + [part_spec(st, wd) for st, wd in zip(starts, widths)] + [ANY],
        out_specs=[
            pl.BlockSpec((seq, tc), lambda j, b: (b, j + off)),
            pl.BlockSpec((k_taps, tc), lambda j, b: (0, j)),
            pl.BlockSpec((1, tc), lambda j, b: (0, j)),
        ],
        out_shape=[
            jax.ShapeDtypeStruct(dproj.shape, BF16),
            jax.ShapeDtypeStruct((k_taps, n_cols), F32),
            jax.ShapeDtypeStruct((1, n_cols), F32),
        ],
        input_output_aliases={3 + n_parts: 0},
        compiler_params=_params(("parallel", "arbitrary")),
    )(proj, conv_w, pre, *dparts, dproj)


def _fill_cols(buf, src, col0, *, name):
    t, c = src.shape
    tm = _pick(t, (1024, 512, 256, 128))

    def body(s_ref, b_ref, o_ref):
        o_ref[...] = s_ref[...].astype(o_ref.dtype)

    return pl.pallas_call(
        body,
        name=name,
        grid=(t // tm,),
        in_specs=[pl.BlockSpec((tm, c), lambda i: (i, 0)), ANY],
        out_specs=pl.BlockSpec((tm, c), lambda i: (i, col0 // c)),
        out_shape=jax.ShapeDtypeStruct(buf.shape, buf.dtype),
        input_output_aliases={1: 0},
        compiler_params=_params(("parallel",)),
    )(src, buf)


def _softplus(x):
    return jnp.maximum(x, 0.0) + jnp.log(1.0 + jnp.exp(-jnp.abs(x)))


def _chunk_decay(dtraw, bias, alog):
    q = dtraw.shape[0]
    dt = _softplus(dtraw + bias)
    a = -jnp.exp(alog)
    rows = lax.broadcasted_iota(jnp.int32, (q, q), 0)
    cols = lax.broadcasted_iota(jnp.int32, (q, q), 1)
    lower = rows >= cols
    acum = jnp.dot(lower.astype(F32), dt * a, precision=lax.Precision.HIGHEST, preferred_element_type=F32)
    return dt, a, acum, acum.T, lower


def _dot_exact(v, sel):
    hi = v.astype(BF16)
    r1 = v - hi.astype(F32)
    mid = r1.astype(BF16)
    lo = (r1 - mid.astype(F32)).astype(BF16)
    return (jnp.dot(hi, sel, preferred_element_type=F32) + jnp.dot(mid, sel, preferred_element_type=F32)
            + jnp.dot(lo, sel, preferred_element_type=F32))


def _head_selectors(gw, p):
    sum_heads = (lax.broadcasted_iota(jnp.int32, (gw, LANE), 0) // p == lax.broadcasted_iota(jnp.int32, (gw, LANE), 1))
    spread = (lax.broadcasted_iota(jnp.int32, (LANE, gw), 0) == lax.broadcasted_iota(jnp.int32, (LANE, gw), 1) // p)
    return sum_heads.astype(BF16), spread.astype(BF16)


def _row_spread(v, spread):
    return _dot_exact(jnp.broadcast_to(v, (8, v.shape[1])), spread)[0:1, :]


def _head_pad(v, r_heads):
    lead = v.shape[:-1]
    vg = v.reshape(lead + (N_SSD_GROUPS, r_heads))
    vg = jnp.pad(vg, [(0, 0)] * len(lead) + [(0, 0), (0, LANE - r_heads)])
    out = vg.reshape(lead + (N_SSD_GROUPS * LANE,))
    return out[None] if out.ndim == 1 else out


def _head_unpad(v, r_heads):
    lead = v.shape[:-1]
    out = v.reshape(lead + (N_SSD_GROUPS, LANE))[..., :r_heads].reshape(lead + (N_SSD_GROUPS * r_heads,))
    return out[0] if (len(lead) == 1 and lead[0] == 1) else out


def _ssd_w_in_layout(w_in, d_inner, d_xbc, r_heads):
    main = w_in[:, :d_inner + d_xbc]
    return jnp.concatenate([main, _head_pad(w_in[:, d_inner + d_xbc:], r_heads)], axis=1)


def _ssd_w_in_unlayout(w, d_inner, d_xbc, r_heads):
    main = w[:, :d_inner + d_xbc]
    return jnp.concatenate([main, _head_unpad(w[:, d_inner + d_xbc:], r_heads)], axis=1)


SSD_CHUNKS = 2
SSD_CHUNKS_FWD = 4


def _ssd_dims(proj, xbc):
    d_xbc = xbc.shape[1]
    d_inner = d_xbc - 2 * N_SSD_GROUPS * D_STATE
    gw = d_inner // N_SSD_GROUPS
    return d_inner, d_xbc, gw, gw // HEAD_DIM


def _ssd_fwd(proj, xbc, bias_p, alog_p, dskip_p, norm_w, n_seq, *, name):
    t = proj.shape[0]
    d_inner, d_xbc, gw, r_heads = _ssd_dims(proj, xbc)
    q, n, n_g, p = CHUNK, D_STATE, N_SSD_GROUPS, HEAD_DIM
    seq = t // n_seq
    nc = seq // q
    cps = SSD_CHUNKS_FWD if nc % SSD_CHUNKS_FWD == 0 else 1
    dt_blk0 = (d_inner + d_xbc) // LANE

    def body(x_ref, b_ref, c_ref, z_ref, dtr_ref, bias_ref, alog_ref, dsk_ref, nw_ref, yn_ref, y_ref, hs_ref, h_scr):
        @pl.when(pl.program_id(2) == 0)
        def _():
            h_scr[...] = jnp.zeros_like(h_scr)

        for cc in range(cps):
            rows = pl.ds(cc * q, q)
            chunk(x_ref.at[rows, :], b_ref.at[rows, :], c_ref.at[rows, :], z_ref.at[rows, :], dtr_ref.at[rows, :],
                  bias_ref, alog_ref, dsk_ref, nw_ref, yn_ref.at[rows, :], y_ref.at[rows, :],
                  hs_ref.at[pl.ds(cc * n, n), :], h_scr)

    def chunk(x_ref, b_ref, c_ref, z_ref, dtr_ref, bias_ref, alog_ref, dsk_ref, nw_ref, yn_ref, y_ref, hs_ref, h_scr):
        dt, a, acum, acum_t, lower = _chunk_decay(dtr_ref[...], bias_ref[...], alog_ref[...])
        x = x_ref[...]
        bb = b_ref[...].astype(BF16)
        cb = c_ref[...].astype(BF16)
        g_mat = lax.dot_general(cb, bb, (((1,), (1,)), ((), ())), preferred_element_type=F32)
        h_prev = h_scr[...]
        hs_ref[...] = h_prev
        c_h = jnp.dot(cb, h_prev.astype(BF16), preferred_element_type=F32)
        _, spread = _head_selectors(gw, p)
        acum_s = _dot_exact(acum, spread)
        a_last_s = acum_s[q - 1:q, :]
        xdt = x * _dot_exact(dt, spread)
        xdt_b = xdt.astype(BF16)
        ys = []
        for h in range(r_heads):
            decay = jnp.exp(jnp.where(lower, acum[:, h:h + 1] - acum_t[h:h + 1, :], -jnp.inf))
            ys.append(jnp.dot((g_mat * decay).astype(BF16), xdt_b[:, h * p:(h + 1) * p], preferred_element_type=F32))
        y = jnp.concatenate(ys, axis=1) + jnp.exp(acum_s) * c_h + _row_spread(dsk_ref[...], spread) * x
        xd = xdt * jnp.exp(a_last_s - acum_s)
        states = lax.dot_general(bb, xd.astype(BF16), (((0,), (0,)), ((), ())), preferred_element_type=F32)
        h_scr[...] = h_prev * jnp.exp(a_last_s) + states
        y_ref[...] = y
        gated = y * _silu(z_ref[...])
        rstd = lax.rsqrt(jnp.mean(gated * gated, axis=-1, keepdims=True) + EPS)
        yn_ref[...] = (gated * rstd * nw_ref[...]).astype(BF16)

    row = lambda b, g, c: b * (nc // cps) + c
    vec = pl.BlockSpec((1, LANE), lambda b, g, c: (0, g))
    return pl.pallas_call(
        body,
        name=name,
        grid=(n_seq, n_g, nc // cps),
        in_specs=[
            pl.BlockSpec((q * cps, gw), lambda b, g, c: (row(b, g, c), g)),
            pl.BlockSpec((q * cps, n), lambda b, g, c: (row(b, g, c), d_inner // n + g)),
            pl.BlockSpec((q * cps, n), lambda b, g, c: (row(b, g, c), d_inner // n + n_g + g)),
            pl.BlockSpec((q * cps, gw), lambda b, g, c: (row(b, g, c), g)),
            pl.BlockSpec((q * cps, LANE), lambda b, g, c: (row(b, g, c), dt_blk0 + g)),
            vec, vec, vec,
            pl.BlockSpec((1, gw), lambda b, g, c: (0, g)),
        ],
        out_specs=[
            pl.BlockSpec((q * cps, gw), lambda b, g, c: (row(b, g, c), g)),
            pl.BlockSpec((q * cps, gw), lambda b, g, c: (row(b, g, c), g)),
            pl.BlockSpec((n * cps, gw), lambda b, g, c: (row(b, g, c), g)),
        ],
        out_shape=[
            jax.ShapeDtypeStruct((t, d_inner), BF16),
            jax.ShapeDtypeStruct((t, d_inner), F32),
            jax.ShapeDtypeStruct((n_seq * nc * n, d_inner), F32),
        ],
        scratch_shapes=[pltpu.VMEM((n, gw), F32)],
        compiler_params=_params(("parallel", "parallel", "arbitrary")),
    )(xbc, xbc, xbc, proj, proj, bias_p, alog_p, dskip_p, norm_w)


def _ssd_bwd(proj, xbc, hs, y, dyn, bias_p, alog_p, dskip_p, norm_w, n_seq, *, name):
    t = proj.shape[0]
    d_inner, d_xbc, gw, r_heads = _ssd_dims(proj, xbc)
    q, n, n_g, p = CHUNK, D_STATE, N_SSD_GROUPS, HEAD_DIM
    seq = t // n_seq
    nc = seq // q
    cps = SSD_CHUNKS if nc % SSD_CHUNKS == 0 else 1
    dt_blk0 = (d_inner + d_xbc) // LANE

    def body(x_ref, b_ref, c_ref, z_ref, dtr_ref, bias_ref, alog_ref, dsk_ref, nw_ref, hs_ref, y_ref, dyn_ref,
             dx_ref, db_ref, dc_ref, dz_ref, ddtr_ref, dnw_ref, dbias_ref, dalog_ref, ddsk_ref, dh_scr):
        @pl.when(pl.program_id(2) == 0)
        def _():
            dh_scr[...] = jnp.zeros_like(dh_scr)

        first_step = jnp.logical_and(pl.program_id(1) == 0, pl.program_id(2) == 0)
        for cc in reversed(range(cps)):
            rows = pl.ds(cc * q, q)
            chunk(jnp.logical_and(first_step, cc == cps - 1), x_ref.at[rows, :], b_ref.at[rows, :], c_ref.at[rows, :],
                  z_ref.at[rows, :], dtr_ref.at[rows, :], bias_ref, alog_ref, dsk_ref, nw_ref,
                  hs_ref.at[pl.ds(cc * n, n), :], y_ref.at[rows, :], dyn_ref.at[rows, :], dx_ref.at[rows, :],
                  db_ref.at[rows, :], dc_ref.at[rows, :], dz_ref.at[rows, :], ddtr_ref.at[rows, :],
                  dnw_ref, dbias_ref, dalog_ref, ddsk_ref, dh_scr)

    def chunk(first, x_ref, b_ref, c_ref, z_ref, dtr_ref, bias_ref, alog_ref, dsk_ref, nw_ref, hs_ref, y_ref, dyn_ref,
              dx_ref, db_ref, dc_ref, dz_ref, ddtr_ref, dnw_ref, dbias_ref, dalog_ref, ddsk_ref, dh_scr):
        dtraw = dtr_ref[...]
        dt, a, acum, acum_t, lower = _chunk_decay(dtraw, bias_ref[...], alog_ref[...])
        x = x_ref[...]
        bb = b_ref[...].astype(BF16)
        cb = c_ref[...].astype(BF16)
        g_mat = lax.dot_general(cb, bb, (((1,), (1,)), ((), ())), preferred_element_type=F32)

        yv = y_ref[...]
        z = z_ref[...]
        sz = _silu(z)
        gated = yv * sz
        rstd = lax.rsqrt(jnp.mean(gated * gated, axis=-1, keepdims=True) + EPS)
        gn = gated * rstd
        dynv = dyn_ref[...]
        gwt = dynv * nw_ref[...]
        dgated = rstd * (gwt - gn * jnp.mean(gwt * gn, axis=-1, keepdims=True))
        dnw = jnp.sum(dynv * gn, axis=0, keepdims=True)
        dy = dgated * sz
        dz_ref[...] = (dgated * yv * _dsilu(z)).astype(BF16)

        h_prev = hs_ref[...]
        h_prev_b = h_prev.astype(BF16)
        ds = dh_scr[...]
        ds_b = ds.astype(BF16)
        sum_heads, spread = _head_selectors(gw, p)
        acum_s = _dot_exact(acum, spread)
        a_last_s = acum_s[q - 1:q, :]
        dt_s = _dot_exact(dt, spread)
        dsk_s = _row_spread(dsk_ref[...], spread)
        dte_s = jnp.exp(a_last_s - acum_s)
        cd_s = jnp.exp(a_last_s)
        xdt = x * dt_s
        xdt_b = xdt.astype(BF16)
        dy_b = dy.astype(BF16)
        gt_mat = lax.dot_general(bb, cb, (((1,), (1,)), ((), ())), preferred_element_type=F32)
        upper = lax.broadcasted_iota(jnp.int32, (q, q), 0) <= lax.broadcasted_iota(jnp.int32, (q, q), 1)
        dg = jnp.zeros((q, q), F32)
        dxdts, w_diffs = [], []
        for h in range(r_heads):
            hsl = slice(h * p, (h + 1) * p)
            diff = acum[:, h:h + 1] - acum_t[h:h + 1, :]
            decay = jnp.exp(jnp.where(lower, diff, -jnp.inf))
            decay_t = jnp.exp(jnp.where(upper, -diff, -jnp.inf))
            mt_mat = gt_mat * decay_t
            dm = lax.dot_general(dy_b[:, hsl], xdt_b[:, hsl], (((1,), (1,)), ((), ())), preferred_element_type=F32)
            dm_t = lax.dot_general(xdt_b[:, hsl], dy_b[:, hsl], (((1,), (1,)), ((), ())), preferred_element_type=F32)
            dg = dg + dm * decay
            dxdts.append(jnp.dot(mt_mat.astype(BF16), dy_b[:, hsl], preferred_element_type=F32))
            w_diffs.append(dm * (g_mat * decay) - dm_t * mt_mat)
        sel_q = (lax.broadcasted_iota(jnp.int32, (r_heads * q, LANE), 0) // q
                 == lax.broadcasted_iota(jnp.int32, (r_heads * q, LANE), 1)).astype(BF16)
        dacum_diag = _dot_exact(jnp.concatenate(w_diffs, axis=1), sel_q)
        c_h = jnp.dot(cb, h_prev_b, preferred_element_type=F32)
        dxd = jnp.dot(bb, ds_b, preferred_element_type=F32)
        dxdt = jnp.concatenate(dxdts, axis=1) + dxd * dte_s
        dye = dy * jnp.exp(acum_s)
        dye_b = dye.astype(BF16)
        xd = xdt * dte_s
        xd_b = xd.astype(BF16)
        dg_b = dg.astype(BF16)
        dx_ref[...] = dxdt * dt_s + dsk_s * dy
        dc_ref[...] = (jnp.dot(dg_b, bb, preferred_element_type=F32)
                       + lax.dot_general(dye_b, h_prev_b, (((1,), (1,)), ((), ())), preferred_element_type=F32))
        db_ref[...] = (lax.dot_general(dg_b, cb, (((0,), (0,)), ((), ())), preferred_element_type=F32)
                       + lax.dot_general(xd_b, ds_b, (((1,), (1,)), ((), ())), preferred_element_type=F32))
        dh_scr[...] = ds * cd_s + lax.dot_general(cb, dye_b, (((0,), (0,)), ((), ())), preferred_element_type=F32)
        ddt_cols = _dot_exact(x * dxdt, sum_heads)
        dacum_y = _dot_exact(dye * c_h - dxd * xd, sum_heads)
        col_sums = jnp.concatenate([
            jnp.sum(dxd * xd, axis=0, keepdims=True) + jnp.sum(ds * h_prev, axis=0, keepdims=True) * cd_s,
            jnp.sum(dy * x, axis=0, keepdims=True),
            jnp.zeros((6, gw), F32)], axis=0)
        col_sums = _dot_exact(col_sums, sum_heads)
        ddsk = col_sums[1:2, :]
        rows_q = lax.broadcasted_iota(jnp.int32, (q, LANE), 0)
        dacum = dacum_diag + dacum_y + jnp.where(rows_q == q - 1, col_sums[0:1, :], 0.0)
        dadt = jnp.dot(upper.astype(F32), dacum, precision=lax.Precision.HIGHEST, preferred_element_type=F32)
        ddt = dadt * a + ddt_cols
        ddtr = ddt * _sigmoid(dtraw + bias_ref[...])
        ddtr_ref[...] = ddtr
        dbias = jnp.sum(ddtr, axis=0, keepdims=True)
        dalog = jnp.sum(dadt * dt, axis=0, keepdims=True) * a

        @pl.when(first)
        def _():
            dnw_ref[...] = dnw
            dbias_ref[...] = dbias
            dalog_ref[...] = dalog
            ddsk_ref[...] = ddsk

        @pl.when(jnp.logical_not(first))
        def _():
            dnw_ref[...] += dnw
            dbias_ref[...] += dbias
            dalog_ref[...] += dalog
            ddsk_ref[...] += ddsk

    row = lambda g, b, c: b * (nc // cps) + (nc // cps - 1 - c)
    vec = pl.BlockSpec((1, LANE), lambda g, b, c: (0, g))
    wide = pl.BlockSpec((q * cps, gw), lambda g, b, c: (row(g, b, c), g))
    narrow = pl.BlockSpec((q * cps, n), lambda g, b, c: (row(g, b, c), g))
    return pl.pallas_call(
        body,
        name=name,
        grid=(n_g, n_seq, nc // cps),
        in_specs=[
            wide,
            pl.BlockSpec((q * cps, n), lambda g, b, c: (row(g, b, c), d_inner // n + g)),
            pl.BlockSpec((q * cps, n), lambda g, b, c: (row(g, b, c), d_inner // n + n_g + g)),
            wide,
            pl.BlockSpec((q * cps, LANE), lambda g, b, c: (row(g, b, c), dt_blk0 + g)),
            vec, vec, vec,
            pl.BlockSpec((1, gw), lambda g, b, c: (0, g)),
            pl.BlockSpec((n * cps, gw), lambda g, b, c: (row(g, b, c), g)),
            wide, wide,
        ],
        out_specs=[
            wide, narrow, narrow, wide, narrow,
            pl.BlockSpec((1, gw), lambda g, b, c: (0, g)),
            vec, vec, vec,
        ],
        out_shape=[
            jax.ShapeDtypeStruct((t, d_inner), F32),
            jax.ShapeDtypeStruct((t, n_g * n), F32),
            jax.ShapeDtypeStruct((t, n_g * n), F32),
            jax.ShapeDtypeStruct(proj.shape, BF16),
            jax.ShapeDtypeStruct((t, n_g * LANE), F32),
            jax.ShapeDtypeStruct((1, d_inner), F32),
            jax.ShapeDtypeStruct((1, n_g * LANE), F32),
            jax.ShapeDtypeStruct((1, n_g * LANE), F32),
            jax.ShapeDtypeStruct((1, n_g * LANE), F32),
        ],
        scratch_shapes=[pltpu.VMEM((n, gw), F32)],
        compiler_params=_params(("parallel", "arbitrary", "arbitrary")),
    )(xbc, xbc, xbc, proj, proj, bias_p, alog_p, dskip_p, norm_w, hs, y, dyn)


MESH_IDS = pl.DeviceIdType.MESH


def _my_index():
    return 4 * lax.axis_index("x") + 2 * lax.axis_index("y") + lax.axis_index("c")


def _all_gather(shard, *, name):
    def body(x_ref, out_ref, send_sems, recv_sems, local_sem):
        x, y, c = lax.axis_index("x"), lax.axis_index("y"), lax.axis_index("c")
        me, sibling = (x, y, c), (x, y, 1 - c)
        chips = [(1 - x, y), (x, 1 - y), (1 - x, 1 - y)]

        def blk(px, py, pc):
            return out_ref.at[4 * px + 2 * py + pc]

        def copy(k, block, to, src=None):
            return pltpu.make_async_remote_copy(
                src_ref=blk(*block) if src is None else src, dst_ref=blk(*block),
                send_sem=send_sems.at[k], recv_sem=recv_sems.at[k], device_id=to, device_id_type=MESH_IDS)

        mine = pltpu.make_async_copy(x_ref, blk(*me), local_sem)
        mine.start()
        first = [copy(0, me, sibling, src=x_ref)]
        first += [copy(1 + j, me, (*chip, c), src=x_ref) for j, chip in enumerate(chips)]
        for cp in first:
            cp.start()
        passed = [copy(4 + j, (*chip, c), sibling) for j, chip in enumerate(chips)]
        for j, chip in enumerate(chips):
            copy(1 + j, (*chip, c), me).wait_recv()
            passed[j].start()
        copy(0, sibling, me).wait_recv()
        for j, chip in enumerate(chips):
            copy(4 + j, (*chip, 1 - c), me).wait_recv()
        for cp in first + passed:
            cp.wait_send()
        mine.wait()

    return pl.pallas_call(
        body,
        name=name,
        in_specs=[ANY],
        out_specs=ANY,
        out_shape=jax.ShapeDtypeStruct((N_DEV,) + shard.shape, shard.dtype),
        scratch_shapes=[pltpu.SemaphoreType.DMA((7,)), pltpu.SemaphoreType.DMA((7,)), pltpu.SemaphoreType.DMA],
    )(shard)


HBM_SPEC = pl.BlockSpec(memory_space=pltpu.HBM)
SEM_SPEC = pl.BlockSpec(memory_space=pltpu.SEMAPHORE)
SPLIT_COPY_PARAMS = pltpu.CompilerParams(has_side_effects=pltpu.SideEffectType.DATAFLOW_SIDE_EFFECTING)


def _peer_list():
    x, y, c = lax.axis_index("x"), lax.axis_index("y"), lax.axis_index("c")
    peers = []
    for k in range(1, N_DEV):
        px = 1 - x if k & 4 else x
        py = 1 - y if k & 2 else y
        pc = 1 - c if k & 1 else c
        peers.append(((px, py, pc), 4 * px + 2 * py + pc))
    return 4 * x + 2 * y + c, peers


def _push_copies(src_refs, land_refs, send_sems, recv_sems, blockwise):
    me, peers = _peer_list()
    copies = []
    for a, (src_ref, land_ref) in enumerate(zip(src_refs, land_refs)):
        for k, (dev, idx) in enumerate(peers):
            sem = a * (N_DEV - 1) + k
            src = src_ref.at[idx] if blockwise else src_ref
            copies.append(tuple(
                pltpu.make_async_remote_copy(src_ref=src, dst_ref=land_ref.at[slot], send_sem=send_sems.at[sem],
                                             recv_sem=recv_sems.at[sem], device_id=dev, device_id_type=MESH_IDS)
                for slot in (me, idx)))
    return copies


def _push_start(srcs, blockwise, after, *, name):
    n = len(srcs)
    blocks = [s_.shape[1:] if blockwise else s_.shape for s_ in srcs]

    def body(*refs):
        src_refs, land_refs = refs[:n], refs[n:2 * n]
        send_sems, recv_sems = refs[2 * n + 1], refs[2 * n + 2]
        token = refs[-1]
        for send, _ in _push_copies(src_refs, land_refs, send_sems, recv_sems, blockwise):
            send.start()
        token[...] = jnp.zeros_like(token)

    n_sem = n * (N_DEV - 1)
    lands = [lax.empty((N_DEV,) + b, s_.dtype) for b, s_ in zip(blocks, srcs)]
    out = pl.pallas_call(
        body,
        name=name,
        in_specs=[HBM_SPEC] * (2 * n) + [ANY],
        out_specs=(SEM_SPEC, SEM_SPEC) + (HBM_SPEC,) * (2 * n) + (pl.BlockSpec(memory_space=pltpu.VMEM),),
        out_shape=(pltpu.SemaphoreType.DMA((n_sem,)), pltpu.SemaphoreType.DMA((n_sem,)))
        + tuple(pltpu.HBM(a.shape, a.dtype) for a in list(srcs) + lands)
        + (jax.ShapeDtypeStruct((8, LANE), F32),),
        input_output_aliases={i: 2 + i for i in range(2 * n)},
        compiler_params=SPLIT_COPY_PARAMS,
    )(*[pltpu.with_memory_space_constraint(a, pltpu.HBM) for a in list(srcs) + lands], after)
    return out[0], out[1], out[2:2 + n], out[2 + n:2 + 2 * n], out[-1]


def _push_wait(send_sems, recv_sems, srcs, lands, blockwise, after, *, name):
    n = len(srcs)

    def body(*refs):
        src_refs, land_refs = refs[:n], refs[n:2 * n]
        send_sems, recv_sems = refs[2 * n], refs[2 * n + 1]
        for send, recv in _push_copies(src_refs, land_refs, send_sems, recv_sems, blockwise):
            send.wait_send()
            recv.wait_recv()

    out = pl.pallas_call(
        body,
        name=name,
        in_specs=[HBM_SPEC] * (2 * n) + [SEM_SPEC, SEM_SPEC, ANY],
        out_specs=(HBM_SPEC,) * (2 * n),
        out_shape=tuple(pltpu.HBM(a.shape, a.dtype) for a in list(srcs) + list(lands)),
        input_output_aliases={i: i for i in range(2 * n)},
        compiler_params=SPLIT_COPY_PARAMS,
    )(*srcs, *lands, send_sems, recv_sems, after)
    return out[n:]


def _with_own_slot(landing, own):
    slot = lax.broadcasted_iota(jnp.int32, (N_DEV,) + (1,) * own.ndim, 0)
    return jnp.where(slot == _my_index(), own[None], landing)


def _sum_slots(parts, own=None, *, name):
    shape = parts.shape[1:]
    n, c = parts.shape[0], parts.shape[-1]
    r = parts.size // (n * c)
    tm = _pick(r, (256, 128, 64, 32, 16, 8))

    def body(p_ref, *rest):
        o_ref = rest[-1]
        me = _my_index()

        def slot(s):
            if own is None:
                return p_ref[s].astype(F32)
            return jnp.where(me == s, rest[0][...], p_ref[s]).astype(F32)

        acc = slot(0)
        for s in range(1, n):
            acc = acc + slot(s)
        o_ref[...] = acc

    tile = pl.BlockSpec((tm, c), lambda i: (i, 0))
    return pl.pallas_call(
        body,
        name=name,
        grid=(r // tm,),
        in_specs=[pl.BlockSpec((n, tm, c), lambda i: (0, i, 0))] + ([] if own is None else [tile]),
        out_specs=tile,
        out_shape=jax.ShapeDtypeStruct((r, c), F32),
        compiler_params=_params(("parallel",)),
    )(parts.reshape(n, r, c), *([] if own is None else [own.reshape(r, c)])).reshape(shape)


def _row_count(shape):
    c = shape[-1]
    rows = 1
    for s in shape[:-1]:
        rows *= s
    return rows, c, c + (-c) % LANE


PACK_ROWS = 256


def _pack_rows(arrays):
    pieces = []
    for a in arrays:
        rows, c, cp = _row_count(a.shape)
        a2 = a.reshape(rows, c)
        if cp > c:
            a2 = jnp.pad(a2, ((0, 0), (0, cp - c)))
        a2 = a2.reshape(rows * cp // LANE, LANE)
        if a2.shape[0] % 8:
            a2 = jnp.pad(a2, ((0, 8 - a2.shape[0] % 8), (0, 0)))
        pieces.append(a2)
    total = sum(p.shape[0] for p in pieces)
    if total % PACK_ROWS:
        pieces.append(jnp.zeros((PACK_ROWS - total % PACK_ROWS, LANE), F32))
    return jnp.concatenate(pieces, axis=0)


def _unpack_rows(packed, shapes, lead=()):
    out, off = [], 0
    for shp in shapes:
        rows, c, cp = _row_count(shp)
        n_rows = rows * cp // LANE
        seg = packed[..., off:off + n_rows, :].reshape(lead + (rows, cp))
        out.append(seg[..., :c].reshape(lead + tuple(shp)))
        off += n_rows + (-n_rows) % 8
    return out


def _unshard(stacked, axis):
    if axis == stacked.ndim - 2:
        return jnp.concatenate([stacked[d] for d in range(N_DEV)], axis=axis)
    moved = jnp.moveaxis(stacked, 0, axis)
    shp = moved.shape
    return moved.reshape(shp[:axis] + (shp[axis] * shp[axis + 1],) + shp[axis + 2:])


def _shard_major(full, axis):
    shp = full.shape
    if axis == full.ndim - 1:
        size = shp[axis] // N_DEV
        return jnp.stack([full[..., d * size:(d + 1) * size] for d in range(N_DEV)])
    split = full.reshape(shp[:axis] + (N_DEV, shp[axis] // N_DEV) + shp[axis + 1:])
    return jnp.moveaxis(split, axis, 0)


def _my_shard(full, axis):
    size = full.shape[axis] // N_DEV
    return lax.dynamic_slice_in_dim(full, _my_index() * size, size, axis)


def _local_step(x, target, w, fetch, emit, n_seq):
    def with_token(vec, token):
        return vec + jnp.tile(token[0:1, :], (1, vec.shape[1] // LANE))

    depth, d_model = w["norm_mix_pre"].shape
    d_inner = w["ssd_norm_w"].shape[1]
    d_xbc = w["ssd_conv_w"].shape[2]
    saved = []
    m, token = fetch(0, "mix", x)
    mix_pre_w = with_token(w["norm_mix_pre"][0:1], token)
    u = _rms_fwd(x, mix_pre_w, out_dtype=BF16, name="l0_mix_pre")
    for i in range(depth):
        j = i // 2
        s = {"x": x, "mix_pre_w": mix_pre_w}
        if i % 2 == 0:
            proj = _mm(u, m["ssd_w_in"], name=f"l{i}_ssd_in")
            xbc, xbc_pre = _ssd_conv_fwd(proj, d_inner, d_xbc, w["ssd_conv_w"][j], w["ssd_conv_b"][j:j + 1], n_seq,
                                         name=f"l{i}_ssd_conv")
            yn, y, hs = _ssd_fwd(proj, xbc, w["ssd_dt_bias"][j:j + 1], w["ssd_a_log"][j:j + 1], w["ssd_d"][j:j + 1],
                                 w["ssd_norm_w"][j:j + 1], n_seq, name=f"l{i}_ssd_scan")
            m_out, token = fetch(i, "out", yn)
            m = {**m, **m_out}
            mix = _mm(yn, m["ssd_w_out"], name=f"l{i}_ssd_out")
            s.update(u=u, proj=proj, xbc=xbc, xbc_pre=xbc_pre, yn=yn, y=y, hs=hs)
        else:
            mix = _pool_fwd(u, m["pool_w"], w["pool_scale"][j:j + 1], n_seq, name=f"l{i}_pool")
            s.update(u=u)
            token = jnp.zeros((8, LANE), F32)
        mix_post_w = with_token(w["norm_mix_post"][i:i + 1], token)
        m_ffn, token = fetch(i, "ffn", mix)
        m = {**m, **m_ffn}
        ffn_pre_w = with_token(w["norm_ffn_pre"][i:i + 1], token)
        x1, n = _res_rms_rms(x, mix, mix_post_w, ffn_pre_w, name=f"l{i}_mix_post_ffn_pre")
        h, hc, a = _ffn_up_act(n, m["ffn_w_up"], w["ffn_conv_w"][i], w["ffn_conv_b"][i:i + 1], n_seq,
                               name=f"l{i}_ffn_up_act")
        f = _mm(a, m["ffn_w_down"], name=f"l{i}_ffn_down")
        s.update(mix=mix, x1=x1, n=n, h=h, hc=hc, a=a, f=f, m=m, ffn_pre_w=ffn_pre_w)
        saved.append(s)
        if i + 1 < depth:
            m, token = fetch(i + 1, "mix", f)
            mix_pre_w = with_token(w["norm_mix_pre"][i + 1:i + 2], token)
            x, u = _res_rms_rms(x1, f, w["norm_ffn_post"][i:i + 1], mix_pre_w,
                                out_dtype=BF16 if (i + 1) % 2 == 0 else F32, name=f"l{i}_ffn_post_mix_pre")
        else:
            x = _res_rms_fwd(x1, f, w["norm_ffn_post"][i:i + 1], name=f"l{i}_ffn_post")

    loss, dx = _loss_head(x, target)
    grads = {k: [None] * len(w[k]) for k in SMALL}
    df, grads["norm_ffn_post"][depth - 1] = _rms_bwd(saved[-1]["f"], w["norm_ffn_post"][depth - 1:depth], dx, None,
                                                      out_dtype=BF16, name=f"l{depth - 1}_ffn_post_b")
    for i in reversed(range(depth)):
        j = i // 2
        s = saved[i]
        m, gm = s["m"], {}
        gm["ffn_w_down"] = _mm(s["a"], df, ta=True, name=f"l{i}_ffn_down_bw")
        dh, grads["ffn_conv_w"][i], grads["ffn_conv_b"][i] = _ffn_down_bx_act_bwd(
            df, m["ffn_w_down"], s["h"], s["hc"], w["ffn_conv_w"][i], n_seq, name=f"l{i}_ffn_act_b")
        dn = _mm(dh, m["ffn_w_up"], tb=True, name=f"l{i}_ffn_up_bx")
        gm["ffn_w_up"] = _mm(s["n"], dh, ta=True, name=f"l{i}_ffn_up_bw")
        token = emit(i, "ffn", gm, dn)
        gm = {}
        dx1, dmix, grads["norm_ffn_pre"][i], grads["norm_mix_post"][i] = _rms_bwd2(
            s["x1"], s["ffn_pre_w"], dn, dx, s["mix"], with_token(w["norm_mix_post"][i:i + 1], token),
            out_dtype=BF16 if i % 2 == 0 else F32, name=f"l{i}_ffn_pre_mix_post_b")
        if i % 2 == 0:
            dyn = _mm(dmix, m["ssd_w_out"], tb=True, name=f"l{i}_ssd_out_bx")
            gm["ssd_w_out"] = _mm(s["yn"], dmix, ta=True, name=f"l{i}_ssd_out_bw")
            token = emit(i, "out", gm, dyn)
            gm = {}
            dxs, db, dc, dz, ddtr, dnw, dbias, dalog, ddsk = _ssd_bwd(
                s["proj"], s["xbc"], s["hs"], s["y"], dyn, w["ssd_dt_bias"][j:j + 1], w["ssd_a_log"][j:j + 1],
                w["ssd_d"][j:j + 1], with_token(w["ssd_norm_w"][j:j + 1], token), n_seq, name=f"l{i}_ssd_scan_b")
            grads["ssd_norm_w"][j], grads["ssd_dt_bias"][j], grads["ssd_a_log"][j], grads["ssd_d"][j] = (
                dnw, dbias, dalog, ddsk)
            dproj, grads["ssd_conv_w"][j], grads["ssd_conv_b"][j] = _ssd_conv_bwd(
                s["proj"], d_inner, w["ssd_conv_w"][j], s["xbc_pre"], (dxs, db, dc), dz, n_seq,
                name=f"l{i}_ssd_conv_b")
            dproj = _fill_cols(dproj, ddtr, d_inner + d_xbc, name=f"l{i}_ssd_dt_b")
            gm["ssd_w_in"] = _mm(s["u"], dproj, ta=True, name=f"l{i}_ssd_in_bw")
            token = emit(i, "mix", gm, dproj)
            du = _mm(dproj, m["ssd_w_in"], tb=True, name=f"l{i}_ssd_in_bx")
        else:
            du, gm["pool_w"], grads["pool_scale"][j] = _pool_bwd(
                s["u"], m["pool_w"], w["pool_scale"][j:j + 1], dmix, n_seq, name=f"l{i}_pool_b")
            token = emit(i, "mix", gm, du)
        if i > 0:
            dx, df, grads["norm_mix_pre"][i], grads["norm_ffn_post"][i - 1] = _rms_bwd2(
                s["x"], with_token(s["mix_pre_w"], token), du, dx1, saved[i - 1]["f"], w["norm_ffn_post"][i - 1:i],
                out_dtype=BF16, name=f"l{i}_mix_pre_ffn_post_b")
        else:
            dx, grads["norm_mix_pre"][i] = _rms_bwd(s["x"], with_token(s["mix_pre_w"], token), du, dx1,
                                                    name=f"l{i}_mix_pre_b")
    return loss, dx, grads


BIG = (("ssd_w_in", 2), ("ssd_w_out", 1), ("pool_w", 2), ("ffn_w_up", 2), ("ffn_w_down", 1))
SMALL_SHARDED = (("ssd_conv_w", 2), ("ffn_conv_w", 2), ("pool_scale", 1))
SMALL = ("ssd_conv_w", "ssd_conv_b", "ssd_dt_bias", "ssd_a_log", "ssd_d", "ssd_norm_w", "pool_scale", "ffn_conv_w",
         "ffn_conv_b", "norm_mix_pre", "norm_mix_post", "norm_ffn_pre", "norm_ffn_post")
WEIGHTS = ("ssd_w_in", "ssd_conv_w", "ssd_conv_b", "ssd_dt_bias", "ssd_a_log", "ssd_d", "ssd_norm_w", "ssd_w_out",
           "pool_w", "pool_scale", "ffn_w_up", "ffn_conv_w", "ffn_conv_b", "ffn_w_down", "norm_mix_pre",
           "norm_mix_post", "norm_ffn_pre", "norm_ffn_post")


def _ssd_sizes(d_inner):
    return d_inner + 2 * N_SSD_GROUPS * D_STATE, d_inner // HEAD_DIM // N_SSD_GROUPS


def _small_compute_layout(full, d_inner):
    _, r_heads = _ssd_sizes(d_inner)
    w = {k: full[k] for k in SMALL}
    for k in ("ssd_dt_bias", "ssd_a_log", "ssd_d"):
        w[k] = _head_pad(full[k], r_heads)
    for k in ("ffn_conv_w", "ffn_conv_b"):
        w[k] = _interleave(full[k])
    return w


def _matmul_compute_layout(k, full, d_inner):
    d_xbc, r_heads = _ssd_sizes(d_inner)
    if k == "ssd_w_in":
        return _ssd_w_in_layout(full, d_inner, d_xbc, r_heads)
    if k == "ffn_w_up":
        return _interleave(full)
    return full


def _layer_matrices(i, part):
    if part == "ffn":
        return (("ffn_w_up", 1, i), ("ffn_w_down", 0, i))
    if i % 2 == 1:
        return (("pool_w", 1, i // 2),) if part == "mix" else ()
    return (("ssd_w_in", 1, i // 2),) if part == "mix" else (("ssd_w_out", 0, i // 2),)


def _fetch_group(i, part):
    mix, out, ffn = (_layer_matrices(i, p) for p in ("mix", "out", "ffn"))
    if i % 2 == 1:
        return mix + ffn if part == "mix" else ()
    if i == 0:
        return {"mix": mix, "out": out + ffn, "ffn": ()}[part]
    return {"mix": mix + out, "out": (), "ffn": ffn}[part]


def _matmul_grad_reference_layout(k, g, d_inner):
    d_xbc, r_heads = _ssd_sizes(d_inner)
    if k == "ssd_w_in":
        return _ssd_w_in_unlayout(g, d_inner, d_xbc, r_heads)
    if k == "ffn_w_up":
        return _deinterleave(g)
    return g


def _small_grads_reference_layout(grads, shapes, d_inner):
    _, r_heads = _ssd_sizes(d_inner)
    g = {k: jnp.stack(grads[k]) for k in SMALL}
    for k in ("ssd_dt_bias", "ssd_a_log", "ssd_d"):
        g[k] = _head_unpad(g[k][:, 0], r_heads)
    for k in ("ffn_conv_w", "ffn_conv_b"):
        g[k] = _deinterleave(g[k])
    return {k: v.reshape(shapes[k]) for k, v in g.items()}


def kernel(x, ssd_w_in, ssd_conv_w, ssd_conv_b, ssd_dt_bias, ssd_a_log, ssd_d, ssd_norm_w, ssd_w_out, pool_w, pool_scale, ffn_w_up, ffn_conv_w, ffn_conv_b, ffn_w_down, norm_mix_pre, norm_mix_post, norm_ffn_pre, norm_ffn_post, loss_target, m_ssd_w_in, m_ssd_conv_w, m_ssd_conv_b, m_ssd_dt_bias, m_ssd_a_log, m_ssd_d, m_ssd_norm_w, m_ssd_w_out, m_pool_w, m_pool_scale, m_ffn_w_up, m_ffn_conv_w, m_ffn_conv_b, m_ffn_w_down, m_norm_mix_pre, m_norm_mix_post, m_norm_ffn_pre, m_norm_ffn_post, v_ssd_w_in, v_ssd_conv_w, v_ssd_conv_b, v_ssd_dt_bias, v_ssd_a_log, v_ssd_d, v_ssd_norm_w, v_ssd_w_out, v_pool_w, v_pool_scale, v_ffn_w_up, v_ffn_conv_w, v_ffn_conv_b, v_ffn_w_down, v_norm_mix_pre, v_norm_mix_post, v_norm_ffn_pre, v_norm_ffn_post):
    shards = dict(ssd_w_in=ssd_w_in, ssd_conv_w=ssd_conv_w, ssd_conv_b=ssd_conv_b, ssd_dt_bias=ssd_dt_bias,
                  ssd_a_log=ssd_a_log, ssd_d=ssd_d, ssd_norm_w=ssd_norm_w, ssd_w_out=ssd_w_out, pool_w=pool_w,
                  pool_scale=pool_scale, ffn_w_up=ffn_w_up, ffn_conv_w=ffn_conv_w, ffn_conv_b=ffn_conv_b,
                  ffn_w_down=ffn_w_down, norm_mix_pre=norm_mix_pre, norm_mix_post=norm_mix_post,
                  norm_ffn_pre=norm_ffn_pre, norm_ffn_post=norm_ffn_post)
    moments_m = dict(zip(WEIGHTS, (m_ssd_w_in, m_ssd_conv_w, m_ssd_conv_b, m_ssd_dt_bias, m_ssd_a_log, m_ssd_d, m_ssd_norm_w, m_ssd_w_out, m_pool_w, m_pool_scale, m_ffn_w_up, m_ffn_conv_w, m_ffn_conv_b, m_ffn_w_down, m_norm_mix_pre, m_norm_mix_post, m_norm_ffn_pre, m_norm_ffn_post)))
    moments_v = dict(zip(WEIGHTS, (v_ssd_w_in, v_ssd_conv_w, v_ssd_conv_b, v_ssd_dt_bias, v_ssd_a_log, v_ssd_d, v_ssd_norm_w, v_ssd_w_out, v_pool_w, v_pool_scale, v_ffn_w_up, v_ffn_conv_w, v_ffn_conv_b, v_ffn_w_down, v_norm_mix_pre, v_norm_mix_post, v_norm_ffn_pre, v_norm_ffn_post)))
    n_seq, seq, d_model = x.shape
    t = n_seq * seq

    d_inner = ssd_norm_w.shape[1]
    depth = norm_mix_pre.shape[0]
    x2 = x.reshape(t, d_model)

    shard16 = {k: shards[k].astype(BF16) for k, _ in BIG}
    order = [(i, part) for i in range(depth) for part in ("mix", "out", "ffn") if _fetch_group(i, part)]
    fetches = {}

    def start_fetch(key, after):
        srcs = [shard16[k][l] for k, _, l in _fetch_group(*key)]
        fetches[key] = _push_start(srcs, False, after, name=f"fetch{key[0]}{key[1]}_start")

    full = dict(shards)
    small_all = _all_gather(_pack_rows([shards[k] for k, _ in SMALL_SHARDED]), name="gather_small_weights")
    small_stacked = _unpack_rows(small_all, [shards[k].shape for k, _ in SMALL_SHARDED], lead=(N_DEV,))
    for (k, axis), st in zip(SMALL_SHARDED, small_stacked):
        full[k] = _unshard(st, axis)
    w = _small_compute_layout(full, d_inner)
    ready = {}

    def fetch(i, part, x_now):
        key = (i, part)
        token = jnp.zeros((8, LANE), F32)
        if key in order:
            if key == order[0]:
                wholes = [_unshard(_all_gather(shard16[k][l], name=f"fetch0_{k}"), axis) for k, axis, l in _fetch_group(i, part)]
                nxt_after = wholes[0]
            else:
                send, recv, srcs, lands, _ = fetches[key]
                lands = _push_wait(send, recv, srcs, lands, False, x_now, name=f"fetch{i}{part}_wait")
                wholes = [_unshard(_with_own_slot(land, shard16[k][l]), axis)
                          for (k, axis, l), land in zip(_fetch_group(i, part), lands)]
                nxt_after = lands[0]
            for (k, _, l), whole in zip(_fetch_group(i, part), wholes):
                ready[k, l] = _matmul_compute_layout(k, whole, d_inner)
            nxt = order.index(key) + 1
            if nxt < len(order):
                start_fetch(order[nxt], nxt_after)
                token = fetches[order[nxt]][4]
        return {k: ready[k, l] for k, _, l in _layer_matrices(i, part)}, token

    g_layers = {}
    in_flight = []

    def finish_exchange(after):
        key, blocks, (send, recv, srcs, lands, _) = in_flight.pop(0)
        lands = _push_wait(send, recv, srcs, lands, True, after, name=f"exchange{key[0]}{key[1]}_wait")
        for (k, _, l), land, block in zip(_layer_matrices(*key), lands, blocks):
            own = lax.dynamic_index_in_dim(block, _my_index(), 0, keepdims=False)
            g_layers[k, l] = _sum_slots(land, own, name=f"sum{key[0]}_{k}")

    def emit(i, part, gm, dx_now):
        if len(in_flight) >= 2:
            finish_exchange(dx_now)
        blocks = [_shard_major(_matmul_grad_reference_layout(k, gm[k].astype(BF16), d_inner), axis)
                  for k, axis, _ in _layer_matrices(i, part)]
        started = _push_start(blocks, True, dx_now, name=f"exchange{i}{part}_start")
        in_flight.append(((i, part), blocks, started))
        return started[4]

    loss, dx, grads = _local_step(x2, loss_target.reshape(t, d_model), w, fetch, emit, n_seq)
    loss = lax.psum(loss, ("x", "y", "c"))

    g_shard = {}
    small_shapes = {k: full[k].shape for k in SMALL}
    g_small = _small_grads_reference_layout(grads, small_shapes, d_inner)
    s_all = _all_gather(_pack_rows([g_small[k] for k in SMALL]) + in_flight[-1][2][4][0:1, :], name="gather_small_grads")
    for k, g in zip(SMALL, _unpack_rows(_sum_slots(s_all, name="sum_small_grads"), [small_shapes[k] for k in SMALL])):
        g_shard[k] = g
    for k, axis in SMALL_SHARDED:
        g_shard[k] = _my_shard(g_shard[k], axis)

    last = [k for key, _, _ in in_flight for k, _, _ in _layer_matrices(*key)]
    deltas, new_m, new_v = {}, {}, {}
    for k in [k for k in WEIGHTS if k not in last] + last:
        if k == last[0]:
            while in_flight:
                finish_exchange(deltas["ffn_w_up"])
        if k in dict(BIG):
            g_shard[k] = jnp.stack([g_layers[k, l] for l in range(shards[k].shape[0])])
        deltas[k], new_m[k], new_v[k] = _adamw(shards[k], g_shard[k], moments_m[k], moments_v[k], name=f"adamw_{k}")
    return (loss, dx.reshape(x.shape), *[g_shard[k] for k in WEIGHTS], *[deltas[k] for k in WEIGHTS],
            *[new_m[k] for k in WEIGHTS], *[new_v[k] for k in WEIGHTS])
```

```python
import functools

import jax
import jax.numpy as jnp
from jax import lax
from jax.experimental import pallas as pl
from jax.experimental.pallas import tpu as pltpu

F32 = jnp.float32
BF16 = jnp.bfloat16

N_DEV = 8
HEAD_DIM = 64
N_SSD_GROUPS = 4
D_STATE = 128
CHUNK = 128
POOL_WINDOWS = (2, 4, 8, 16)
EPS = 1e-6
LANE = 128
ADAM_LR = 0.001
ADAM_B1 = 0.9
ADAM_B2 = 0.999
ADAM_EPS = 1e-08
ADAM_WD = 0.01
ADAM_STEP = 10
VMEM_LIMIT = 56 * 1024 * 1024
ANY = pl.BlockSpec(memory_space=pl.ANY)


def _pick(n, cands):
    for c in cands:
        if n % c == 0:
            return c
    return n


def _params(sem):
    return pltpu.CompilerParams(dimension_semantics=sem, vmem_limit_bytes=VMEM_LIMIT)


def _sigmoid(x):
    return 0.5 * jnp.tanh(0.5 * x) + 0.5


def _silu(x):
    return x * _sigmoid(x)


def _dsilu(x):
    s = _sigmoid(x)
    return s * (1.0 + x * (1.0 - s))


def _shift_down(x, s):
    rows = lax.broadcasted_iota(jnp.int32, x.shape, 0)
    return jnp.where(rows >= s, pltpu.roll(x, s, 0), 0.0)


def _shift_up(x, s):
    n = x.shape[0]
    rows = lax.broadcasted_iota(jnp.int32, x.shape, 0)
    return jnp.where(rows < n - s, pltpu.roll(x, n - s, 0), 0.0)


MM_VMEM_BUDGET = 40 * 1024 * 1024
MM_STEP_BYTES = 1_300_000
MM_SUB = 512


def _mm_tiles(m, n, k, a_bytes, b_bytes, o_bytes):
    def cands(dim, sizes):
        out = [s for s in sizes if s <= dim and dim % s == 0]
        return out or [dim]

    best = None
    for tm in cands(m, (m, m // 2, 2048, 1024, 512, 256, 128)):
        if tm % LANE:
            continue
        for tn in cands(n, (n, n // 2, n // 4, 2048, 1024, 512, 256, 128)):
            if tn % (2 * LANE) and tn != n:
                continue
            for tk in cands(k, (k, k // 2, 2048, 1024, 512)):
                if tk % LANE:
                    continue
                nk = k // tk
                acc = tm * tn * 4 if (nk > 1 and o_bytes != 4) else 0
                temps = tm * min(tn, MM_SUB) * 4 + (tm * tk * 2 if a_bytes == 4 else 0) + (tk * tn * 2 if b_bytes == 4 else 0)
                vmem = 2 * (tm * tk * a_bytes + tk * tn * b_bytes + tm * tn * o_bytes) + acc + temps
                if vmem > MM_VMEM_BUDGET:
                    continue
                steps = (m // tm) * (n // tn) * nk
                cost = (m * k * a_bytes * (n // tn) + k * n * b_bytes * (m // tm) + m * n * o_bytes
                        + steps * MM_STEP_BYTES)
                if best is None or cost < best[0]:
                    best = (cost, tm, tn, tk)
    return best[1:]


def _mm(a, b, *, ta=False, tb=False, out_dtype=F32, name="mm"):
    m, k = (a.shape[1], a.shape[0]) if ta else a.shape
    n = b.shape[0] if tb else b.shape[1]
    o_bytes = jnp.dtype(out_dtype).itemsize
    tm, tn, tk = _mm_tiles(m, n, k, a.dtype.itemsize, b.dtype.itemsize, o_bytes)
    nk = k // tk
    sub = _pick(tn, (MM_SUB, 256))
    use_acc = nk > 1 and o_bytes != 4
    a_spec = pl.BlockSpec((tk, tm), lambda i, j, kk: (kk, i)) if ta else pl.BlockSpec((tm, tk), lambda i, j, kk: (i, kk))
    b_spec = pl.BlockSpec((tn, tk), lambda i, j, kk: (j, kk)) if tb else pl.BlockSpec((tk, tn), lambda i, j, kk: (kk, j))
    dims = (((1,), (1 if tb else 0,)), ((), ()))

    def body(a_ref, b_ref, o_ref, *scratch):
        kk = pl.program_id(2)
        acc_ref = scratch[0] if use_acc else o_ref
        if nk > 1:
            @pl.when(kk == 0)
            def _():
                acc_ref[...] = jnp.zeros_like(acc_ref)

        av = a_ref[...].astype(BF16)
        if ta:
            av = av.T
        for s in range(tn // sub):
            cols = slice(s * sub, (s + 1) * sub)
            bv = (b_ref[cols, :] if tb else b_ref[:, cols]).astype(BF16)
            part = lax.dot_general(av, bv, dims, preferred_element_type=F32)
            if nk == 1:
                o_ref[:, cols] = part.astype(out_dtype)
            else:
                acc_ref[:, cols] += part
        if use_acc:
            @pl.when(kk == nk - 1)
            def _():
                o_ref[...] = acc_ref[...].astype(out_dtype)

    return pl.pallas_call(
        body,
        name=name,
        grid=(m // tm, n // tn, nk),
        in_specs=[a_spec, b_spec],
        out_specs=pl.BlockSpec((tm, tn), lambda i, j, kk: (i, j)),
        out_shape=jax.ShapeDtypeStruct((m, n), out_dtype),
        scratch_shapes=[pltpu.VMEM((tm, tn), F32)] if use_acc else [],
        compiler_params=_params(("parallel", "parallel", "arbitrary")),
    )(a, b)


def _rms_fwd(x, w, *, out_dtype, name):
    t, d = x.shape
    tm = _pick(t, (512, 256, 128))

    def body(x_ref, w_ref, o_ref):
        xv = x_ref[...]
        rstd = lax.rsqrt(jnp.mean(xv * xv, axis=-1, keepdims=True) + EPS)
        o_ref[...] = (xv * rstd * w_ref[...]).astype(out_dtype)

    return pl.pallas_call(
        body,
        name=name,
        grid=(t // tm,),
        in_specs=[pl.BlockSpec((tm, d), lambda i: (i, 0)), pl.BlockSpec((1, d), lambda i: (0, 0))],
        out_specs=pl.BlockSpec((tm, d), lambda i: (i, 0)),
        out_shape=jax.ShapeDtypeStruct((t, d), out_dtype),
        compiler_params=_params(("parallel",)),
    )(x, w)


def _res_rms_fwd(x, f, w, *, name):
    t, d = x.shape
    tm = _pick(t, (512, 256, 128))

    def body(x_ref, f_ref, w_ref, o_ref):
        fv = f_ref[...]
        rstd = lax.rsqrt(jnp.mean(fv * fv, axis=-1, keepdims=True) + EPS)
        o_ref[...] = x_ref[...] + fv * rstd * w_ref[...]

    row = pl.BlockSpec((tm, d), lambda i: (i, 0))
    return pl.pallas_call(
        body,
        name=name,
        grid=(t // tm,),
        in_specs=[row, row, pl.BlockSpec((1, d), lambda i: (0, 0))],
        out_specs=row,
        out_shape=jax.ShapeDtypeStruct((t, d), F32),
        compiler_params=_params(("parallel",)),
    )(x, f, w)


def _rms_bwd(x, w, dy, resid, *, out_dtype=F32, name):
    t, d = x.shape
    tm = _pick(t, (512, 256, 128))
    has_res = resid is not None

    def body(*refs):
        if has_res:
            x_ref, w_ref, dy_ref, r_ref, dx_ref, dw_ref = refs
        else:
            x_ref, w_ref, dy_ref, dx_ref, dw_ref = refs
        xv = x_ref[...]
        dyv = dy_ref[...].astype(F32)
        rstd = lax.rsqrt(jnp.mean(xv * xv, axis=-1, keepdims=True) + EPS)
        xn = xv * rstd
        g = dyv * w_ref[...]
        dx = rstd * (g - xn * jnp.mean(g * xn, axis=-1, keepdims=True))
        if has_res:
            dx = dx + r_ref[...]
        dx_ref[...] = dx.astype(out_dtype)
        part = jnp.sum(dyv * xn, axis=0, keepdims=True)

        @pl.when(pl.program_id(0) == 0)
        def _():
            dw_ref[...] = part

        @pl.when(pl.program_id(0) > 0)
        def _():
            dw_ref[...] += part

    row = pl.BlockSpec((tm, d), lambda i: (i, 0))
    vec = pl.BlockSpec((1, d), lambda i: (0, 0))
    ins = [x, w, dy] + ([resid] if has_res else [])
    return pl.pallas_call(
        body,
        name=name,
        grid=(t // tm,),
        in_specs=[row, vec, row] + ([row] if has_res else []),
        out_specs=[row, vec],
        out_shape=[jax.ShapeDtypeStruct((t, d), out_dtype), jax.ShapeDtypeStruct((1, d), F32)],
        compiler_params=_params(("arbitrary",)),
    )(*ins)


def _res_rms_rms(x, f, w_post, w_pre, *, out_dtype=BF16, name):
    t, d = x.shape
    tm = _pick(t, (512, 256, 128))

    def body(x_ref, f_ref, wp_ref, wn_ref, x1_ref, n_ref):
        fv = f_ref[...]
        x1 = x_ref[...] + fv * lax.rsqrt(jnp.mean(fv * fv, axis=-1, keepdims=True) + EPS) * wp_ref[...]
        x1_ref[...] = x1
        n_ref[...] = (x1 * lax.rsqrt(jnp.mean(x1 * x1, axis=-1, keepdims=True) + EPS) * wn_ref[...]).astype(out_dtype)

    row = pl.BlockSpec((tm, d), lambda i: (i, 0))
    vec = pl.BlockSpec((1, d), lambda i: (0, 0))
    return pl.pallas_call(
        body,
        name=name,
        grid=(t // tm,),
        in_specs=[row, row, vec, vec],
        out_specs=[row, row],
        out_shape=[jax.ShapeDtypeStruct((t, d), F32), jax.ShapeDtypeStruct((t, d), out_dtype)],
        compiler_params=_params(("parallel",)),
    )(x, f, w_post, w_pre)


def _rms_bwd2(xa, wa, dya, resid, xb, wb, *, out_dtype, name):
    t, d = xa.shape
    tm = _pick(t, (512, 256, 128))

    def norm_bwd(xv, w, dyv):
        rstd = lax.rsqrt(jnp.mean(xv * xv, axis=-1, keepdims=True) + EPS)
        xn = xv * rstd
        g = dyv * w
        return rstd * (g - xn * jnp.mean(g * xn, axis=-1, keepdims=True)), jnp.sum(dyv * xn, axis=0, keepdims=True)

    def body(xa_ref, wa_ref, dya_ref, r_ref, xb_ref, wb_ref, da_ref, db_ref, dwa_ref, dwb_ref):
        da, dwa = norm_bwd(xa_ref[...], wa_ref[...], dya_ref[...].astype(F32))
        da = da + r_ref[...]
        da_ref[...] = da
        db, dwb = norm_bwd(xb_ref[...], wb_ref[...], da)
        db_ref[...] = db.astype(out_dtype)

        @pl.when(pl.program_id(0) == 0)
        def _():
            dwa_ref[...] = dwa
            dwb_ref[...] = dwb

        @pl.when(pl.program_id(0) > 0)
        def _():
            dwa_ref[...] += dwa
            dwb_ref[...] += dwb

    row = pl.BlockSpec((tm, d), lambda i: (i, 0))
    vec = pl.BlockSpec((1, d), lambda i: (0, 0))
    return pl.pallas_call(
        body,
        name=name,
        grid=(t // tm,),
        in_specs=[row, vec, row, row, row, vec],
        out_specs=[row, row, vec, vec],
        out_shape=[jax.ShapeDtypeStruct((t, d), F32), jax.ShapeDtypeStruct((t, d), out_dtype),
                   jax.ShapeDtypeStruct((1, d), F32), jax.ShapeDtypeStruct((1, d), F32)],
        compiler_params=_params(("arbitrary",)),
    )(xa, wa, dya, resid, xb, wb)


def _loss_head(y, target, *, name="loss_head"):
    t, d = y.shape
    tm = _pick(t, (512, 256, 128))

    def body(y_ref, t_ref, dy_ref, l_ref):
        err = y_ref[...] - t_ref[...]
        dy_ref[...] = err * (1.0 / d)
        part = jnp.sum(jnp.sum(err * err, axis=-1, keepdims=True), axis=0, keepdims=True) * (0.5 / d)
        part = jnp.broadcast_to(part, (1, LANE))

        @pl.when(pl.program_id(0) == 0)
        def _():
            l_ref[...] = part

        @pl.when(pl.program_id(0) > 0)
        def _():
            l_ref[...] += part

    row = pl.BlockSpec((tm, d), lambda i: (i, 0))
    dy, l = pl.pallas_call(
        body,
        name=name,
        grid=(t // tm,),
        in_specs=[row, row],
        out_specs=[row, pl.BlockSpec((1, LANE), lambda i: (0, 0))],
        out_shape=[jax.ShapeDtypeStruct((t, d), F32), jax.ShapeDtypeStruct((1, LANE), F32)],
        compiler_params=_params(("arbitrary",)),
    )(y, target)
    return l[0, 0], dy


def _conv_taps(h, w_ref, k_taps):
    out = h * w_ref[k_taps - 1:k_taps, :]
    for k in range(k_taps - 1):
        out = out + _shift_down(h, k_taps - 1 - k) * w_ref[k:k + 1, :]
    return out


def _conv_taps_bwd(h, dhc, w_ref, k_taps):
    dh = dhc * w_ref[k_taps - 1:k_taps, :]
    dws = []
    for k in range(k_taps - 1):
        up = _shift_up(dhc, k_taps - 1 - k)
        dh = dh + up * w_ref[k:k + 1, :]
        dws.append(jnp.sum(up * h, axis=0, keepdims=True))
    dws.append(jnp.sum(dhc * h, axis=0, keepdims=True))
    return dh, jnp.concatenate(dws, axis=0)


FFN_TC = 256


def _interleave(w, tc=FFN_TC):
    f = w.shape[-1] // 2
    tiles = []
    for j in range(f // tc):
        tiles += [w[..., j * tc:(j + 1) * tc], w[..., f + j * tc:f + (j + 1) * tc]]
    return jnp.concatenate(tiles, axis=-1)


def _deinterleave(w, tc=FFN_TC):
    n_tiles = w.shape[-1] // tc
    return jnp.concatenate([w[..., j * tc:(j + 1) * tc] for j in list(range(0, n_tiles, 2)) + list(range(1, n_tiles, 2))],
                           axis=-1)


FFN_ROWS = 512
HALO = 8


def _ffn_up_act(n, w_up, conv_w, conv_b, n_seq, *, name):
    t, d = n.shape
    f2 = w_up.shape[1]
    seq = t // n_seq
    tc = FFN_TC
    nj = f2 // (2 * tc)
    k_taps = conv_w.shape[0]
    rows = min(FFN_ROWS, seq)

    def body(n_ref, wu_ref, w_ref, b_ref, h_ref, hc_ref, o_ref, h_scr):
        h_scr[0:HALO, :] = jnp.zeros((HALO, 2 * tc), F32)
        wu = wu_ref[...]
        for r in range(seq // rows):
            chunk = slice(r * rows, (r + 1) * rows)
            h = jnp.dot(n_ref[chunk, :], wu, preferred_element_type=F32)
            h_scr[HALO + r * rows:HALO + (r + 1) * rows, :] = h
            h_ref[chunk, :] = h.astype(BF16)
            ext = h_scr[r * rows:HALO + (r + 1) * rows, :]
            hc = ext * w_ref[k_taps - 1:k_taps, :]
            for k in range(k_taps - 1):
                hc = hc + pltpu.roll(ext, k_taps - 1 - k, 0) * w_ref[k:k + 1, :]
            hc = hc[HALO:, :] + b_ref[...]
            hc_ref[chunk, :] = hc.astype(BF16)
            o_ref[chunk, :] = (_silu(hc[:, :tc]) * hc[:, tc:]).astype(BF16)

    wide = pl.BlockSpec((seq, 2 * tc), lambda b, j: (b, j))
    return pl.pallas_call(
        body,
        name=name,
        grid=(n_seq, nj),
        in_specs=[
            pl.BlockSpec((seq, d), lambda b, j: (b, 0)),
            pl.BlockSpec((d, 2 * tc), lambda b, j: (0, j)),
            pl.BlockSpec((k_taps, 2 * tc), lambda b, j: (0, j)),
            pl.BlockSpec((1, 2 * tc), lambda b, j: (0, j)),
        ],
        out_specs=[wide, wide, pl.BlockSpec((seq, tc), lambda b, j: (b, j))],
        out_shape=[jax.ShapeDtypeStruct((t, f2), BF16), jax.ShapeDtypeStruct((t, f2), BF16),
                   jax.ShapeDtypeStruct((t, f2 // 2), BF16)],
        scratch_shapes=[pltpu.VMEM((HALO + seq, 2 * tc), F32)],
        compiler_params=_params(("parallel", "arbitrary")),
    )(n, w_up, conv_w, conv_b)


def _ffn_down_bx_act_bwd(df, w_down, h, hc, conv_w, n_seq, *, name):
    t, d = df.shape
    f2 = h.shape[1]
    seq = t // n_seq
    tc = FFN_TC
    nj = f2 // (2 * tc)
    k_taps = conv_w.shape[0]

    def body(df_ref, wd_ref, h_ref, hc_ref, w_ref, dh_ref, dw_ref, db_ref):
        dav = lax.dot_general(df_ref[...], wd_ref[...], (((1,), (1,)), ((), ())), preferred_element_type=F32)
        hcv = hc_ref[...].astype(F32)
        gate, val = hcv[:, :tc], hcv[:, tc:]
        dhc = jnp.concatenate([dav * val * _dsilu(gate), dav * _silu(gate)], axis=1)
        dh, dw = _conv_taps_bwd(h_ref[...].astype(F32), dhc, w_ref, k_taps)
        dh_ref[...] = dh.astype(BF16)
        dw_ref[0] = dw
        db_ref[0] = jnp.sum(dhc, axis=0, keepdims=True)

    wide = pl.BlockSpec((seq, 2 * tc), lambda b, j: (b, j))
    dh, dw, db = pl.pallas_call(
        body,
        name=name,
        grid=(n_seq, nj),
        in_specs=[
            pl.BlockSpec((seq, d), lambda b, j: (b, 0)),
            pl.BlockSpec((tc, d), lambda b, j: (j, 0)),
            wide, wide,
            pl.BlockSpec((k_taps, 2 * tc), lambda b, j: (0, j)),
        ],
        out_specs=[
            wide,
            pl.BlockSpec((1, k_taps, 2 * tc), lambda b, j: (b, 0, j)),
            pl.BlockSpec((1, 1, 2 * tc), lambda b, j: (b, 0, j)),
        ],
        out_shape=[
            jax.ShapeDtypeStruct((t, f2), BF16),
            jax.ShapeDtypeStruct((n_seq, k_taps, f2), F32),
            jax.ShapeDtypeStruct((n_seq, 1, f2), F32),
        ],
        compiler_params=_params(("parallel", "arbitrary")),
    )(df, w_down, h, hc, conv_w)
    return dh, jnp.sum(dw, axis=0), jnp.sum(db, axis=0)


def _window_mixed(u, window):
    s = u
    step = 1
    while step < window:
        s = s + _shift_down(s, step)
        step *= 2
    rows = lax.broadcasted_iota(jnp.int32, u.shape, 0)
    inv_cnt = 1.0 / jnp.minimum(rows + 1, window).astype(F32)
    return s * inv_cnt - u, inv_cnt


def _window_mixed_bwd(dmixed, inv_cnt, window):
    r = dmixed * inv_cnt
    s = r
    step = 1
    while step < window:
        s = s + _shift_up(s, step)
        step *= 2
    return s - dmixed


def _pool_fwd(u, w, scale, n_seq, *, name):
    t, d = u.shape
    seq = t // n_seq
    n_g, dg, _ = w.shape

    def body(u_ref, w_ref, s_ref, o_ref):
        for k, window in enumerate(POOL_WINDOWS):
            @pl.when(pl.program_id(1) == k)
            def _(window=window):
                mixed, _ = _window_mixed(u_ref[...], window)
                pre = jnp.dot(mixed.astype(BF16), w_ref[0].astype(BF16), preferred_element_type=F32)
                o_ref[...] = pre * s_ref[...]

    return pl.pallas_call(
        body,
        name=name,
        grid=(n_seq, n_g),
        in_specs=[
            pl.BlockSpec((seq, dg), lambda b, g: (b, g)),
            pl.BlockSpec((1, dg, dg), lambda b, g: (g, 0, 0)),
            pl.BlockSpec((1, dg), lambda b, g: (0, g)),
        ],
        out_specs=pl.BlockSpec((seq, dg), lambda b, g: (b, g)),
        out_shape=jax.ShapeDtypeStruct((t, d), F32),
        compiler_params=_params(("parallel", "parallel")),
    )(u, w, scale)


def _pool_bwd(u, w, scale, dout, n_seq, *, name):
    t, d = u.shape
    seq = t // n_seq
    n_g, dg, _ = w.shape

    def body(u_ref, w_ref, s_ref, do_ref, du_ref, dw_ref, ds_ref):
        group = pl.program_id(0)
        first = pl.program_id(1) == 0
        for k, window in enumerate(POOL_WINDOWS):
            @pl.when(group == k)
            def _(window=window):
                mixed, inv_cnt = _window_mixed(u_ref[...], window)
                mixed_b = mixed.astype(BF16)
                w_b = w_ref[0].astype(BF16)
                dov = do_ref[...]
                pre = jnp.dot(mixed_b, w_b, preferred_element_type=F32)
                dsc = jnp.sum(dov * pre, axis=0, keepdims=True)
                dpre = (dov * s_ref[...]).astype(BF16)
                dw = lax.dot_general(mixed_b, dpre, (((0,), (0,)), ((), ())), preferred_element_type=F32)
                dmixed = lax.dot_general(dpre, w_b, (((1,), (1,)), ((), ())), preferred_element_type=F32)
                du_ref[...] = _window_mixed_bwd(dmixed, inv_cnt, window)

                @pl.when(first)
                def _():
                    dw_ref[0] = dw
                    ds_ref[...] = dsc

                @pl.when(jnp.logical_not(first))
                def _():
                    dw_ref[0] += dw
                    ds_ref[...] += dsc

    return pl.pallas_call(
        body,
        name=name,
        grid=(n_g, n_seq),
        in_specs=[
            pl.BlockSpec((seq, dg), lambda g, b: (b, g)),
            pl.BlockSpec((1, dg, dg), lambda g, b: (g, 0, 0)),
            pl.BlockSpec((1, dg), lambda g, b: (0, g)),
            pl.BlockSpec((seq, dg), lambda g, b: (b, g)),
        ],
        out_specs=[
            pl.BlockSpec((seq, dg), lambda g, b: (b, g)),
            pl.BlockSpec((1, dg, dg), lambda g, b: (g, 0, 0)),
            pl.BlockSpec((1, dg), lambda g, b: (0, g)),
        ],
        out_shape=[
            jax.ShapeDtypeStruct((t, d), F32),
            jax.ShapeDtypeStruct((n_g, dg, dg), F32),
            jax.ShapeDtypeStruct((1, d), F32),
        ],
        compiler_params=_params(("parallel", "arbitrary")),
    )(u, w, scale, dout)


def _adamw(w, g, m, v, *, name):
    shape = w.shape
    c = shape[-1]
    r = w.size // c
    tm = _pick(r, (512, 256, 128, 64, 32, 16, 8))

    def body(w_ref, g_ref, m_ref, v_ref, d_ref, nm_ref, nv_ref):
        gv = g_ref[...]
        nm = ADAM_B1 * m_ref[...] + (1.0 - ADAM_B1) * gv
        nv = ADAM_B2 * v_ref[...] + (1.0 - ADAM_B2) * (gv * gv)
        m_hat = nm / (1.0 - ADAM_B1 ** ADAM_STEP)
        v_hat = nv / (1.0 - ADAM_B2 ** ADAM_STEP)
        d_ref[...] = -ADAM_LR * (m_hat / (jnp.sqrt(v_hat) + ADAM_EPS) + ADAM_WD * w_ref[...])
        nm_ref[...] = nm
        nv_ref[...] = nv

    blk = pl.BlockSpec((tm, c), lambda i: (i, 0))
    out = jax.ShapeDtypeStruct((r, c), F32)
    res = pl.pallas_call(
        body,
        name=name,
        grid=(r // tm,),
        in_specs=[blk] * 4,
        out_specs=[blk] * 3,
        out_shape=[out] * 3,
        compiler_params=_params(("parallel",)),
    )(w.reshape(r, c), g.reshape(r, c), m.reshape(r, c), v.reshape(r, c))
    return tuple(a.reshape(shape) for a in res)


CONV_TC = 256


def _ssd_conv_fwd(proj, col0, n_cols, conv_w, conv_b, n_seq, *, name):
    t = proj.shape[0]
    seq = t // n_seq
    tc = CONV_TC
    off = col0 // tc
    k_taps = conv_w.shape[0]

    def body(h_ref, w_ref, b_ref, o_ref, pre_ref):
        pre = _conv_taps(h_ref[...], w_ref, k_taps) + b_ref[...]
        pre_ref[...] = pre.astype(BF16)
        o_ref[...] = _silu(pre)

    return pl.pallas_call(
        body,
        name=name,
        grid=(n_seq, n_cols // tc),
        in_specs=[
            pl.BlockSpec((seq, tc), lambda b, j: (b, j + off)),
            pl.BlockSpec((k_taps, tc), lambda b, j: (0, j)),
            pl.BlockSpec((1, tc), lambda b, j: (0, j)),
        ],
        out_specs=[pl.BlockSpec((seq, tc), lambda b, j: (b, j))] * 2,
        out_shape=[jax.ShapeDtypeStruct((t, n_cols), F32), jax.ShapeDtypeStruct((t, n_cols), BF16)],
        compiler_params=_params(("parallel", "parallel")),
    )(proj, conv_w, conv_b)


def _ssd_conv_bwd(proj, col0, conv_w, pre, dparts, dproj, n_seq, *, name):
    t = proj.shape[0]
    seq = t // n_seq
    tc = CONV_TC
    off = col0 // tc
    k_taps = conv_w.shape[0]
    widths = [d.shape[1] // tc for d in dparts]
    starts = [sum(widths[:i]) for i in range(len(widths))]
    n_blocks = sum(widths)
    n_parts = len(dparts)

    def body(h_ref, w_ref, pre_ref, *rest):
        part_refs = rest[:n_parts]
        dh_ref, dw_ref, db_ref = rest[n_parts + 1:]
        j = pl.program_id(0)
        da = part_refs[-1][...]
        for i in reversed(range(n_parts - 1)):
            da = jnp.where(j < starts[i + 1], part_refs[i][...], da)
        dhc = da * _dsilu(pre_ref[...].astype(F32))
        dh, dw = _conv_taps_bwd(h_ref[...], dhc, w_ref, k_taps)
        dh_ref[...] = dh.astype(BF16)
        db = jnp.sum(dhc, axis=0, keepdims=True)

        @pl.when(pl.program_id(1) == 0)
        def _():
            dw_ref[...] = dw
            db_ref[...] = db

        @pl.when(pl.program_id(1) > 0)
        def _():
            dw_ref[...] += dw
            db_ref[...] += db

    def part_spec(start, width):
        return pl.BlockSpec((seq, tc), lambda j, b: (b, jnp.clip(j - start, 0, width - 1)))

    n_cols = n_blocks * tc
    return pl.pallas_call(
        body,
        name=name,
        grid=(n_blocks, n_seq),
        in_specs=[
            pl.BlockSpec((seq, tc), lambda j, b: (b, j + off)),
            pl.BlockSpec((k_taps, tc), lambda j, b: (0, j)),
            pl.BlockSpec((seq, tc), lambda j, b: (b, j)),
        ] + [part_spec(st, wd) for st, wd in zip(starts, widths)] + [ANY],
        out_specs=[
            pl.BlockSpec((seq, tc), lambda j, b: (b, j + off)),
            pl.BlockSpec((k_taps, tc), lambda j, b: (0, j)),
            pl.BlockSpec((1, tc), lambda j, b: (0, j)),
        ],
        out_shape=[
            jax.ShapeDtypeStruct(dproj.shape, BF16),
            jax.ShapeDtypeStruct((k_taps, n_cols), F32),
            jax.ShapeDtypeStruct((1, n_cols), F32),
        ],
        input_output_aliases={3 + n_parts: 0},
        compiler_params=_params(("parallel", "arbitrary")),
    )(proj, conv_w, pre, *dparts, dproj)


def _fill_cols(buf, src, col0, *, name):
    t, c = src.shape
    tm = _pick(t, (1024, 512, 256, 128))

    def body(s_ref, b_ref, o_ref):
        o_ref[...] = s_ref[...].astype(o_ref.dtype)

    return pl.pallas_call(
        body,
        name=name,
        grid=(t // tm,),
        in_specs=[pl.BlockSpec((tm, c), lambda i: (i, 0)), ANY],
        out_specs=pl.BlockSpec((tm, c), lambda i: (i, col0 // c)),
        out_shape=jax.ShapeDtypeStruct(buf.shape, buf.dtype),
        input_output_aliases={1: 0},
        compiler_params=_params(("parallel",)),
    )(src, buf)


def _softplus(x):
    return jnp.maximum(x, 0.0) + jnp.log(1.0 + jnp.exp(-jnp.abs(x)))


def _chunk_decay(dtraw, bias, alog):
    q = dtraw.shape[0]
    dt = _softplus(dtraw + bias)
    a = -jnp.exp(alog)
    rows = lax.broadcasted_iota(jnp.int32, (q, q), 0)
    cols = lax.broadcasted_iota(jnp.int32, (q, q), 1)
    lower = rows >= cols
    acum = jnp.dot(lower.astype(F32), dt * a, precision=lax.Precision.HIGHEST, preferred_element_type=F32)
    return dt, a, acum, acum.T, lower


def _dot_exact(v, sel):
    hi = v.astype(BF16)
    r1 = v - hi.astype(F32)
    mid = r1.astype(BF16)
    lo = (r1 - mid.astype(F32)).astype(BF16)
    return (jnp.dot(hi, sel, preferred_element_type=F32) + jnp.dot(mid, sel, preferred_element_type=F32)
            + jnp.dot(lo, sel, preferred_element_type=F32))


def _head_selectors(gw, p):
    sum_heads = (lax.broadcasted_iota(jnp.int32, (gw, LANE), 0) // p == lax.broadcasted_iota(jnp.int32, (gw, LANE), 1))
    spread = (lax.broadcasted_iota(jnp.int32, (LANE, gw), 0) == lax.broadcasted_iota(jnp.int32, (LANE, gw), 1) // p)
    return sum_heads.astype(BF16), spread.astype(BF16)


def _row_spread(v, spread):
    return _dot_exact(jnp.broadcast_to(v, (8, v.shape[1])), spread)[0:1, :]


def _head_pad(v, r_heads):
    lead = v.shape[:-1]
    vg = v.reshape(lead + (N_SSD_GROUPS, r_heads))
    vg = jnp.pad(vg, [(0, 0)] * len(lead) + [(0, 0), (0, LANE - r_heads)])
    out = vg.reshape(lead + (N_SSD_GROUPS * LANE,))
    return out[None] if out.ndim == 1 else out


def _head_unpad(v, r_heads):
    lead = v.shape[:-1]
    out = v.reshape(lead + (N_SSD_GROUPS, LANE))[..., :r_heads].reshape(lead + (N_SSD_GROUPS * r_heads,))
    return out[0] if (len(lead) == 1 and lead[0] == 1) else out


def _ssd_w_in_layout(w_in, d_inner, d_xbc, r_heads):
    main = w_in[:, :d_inner + d_xbc]
    return jnp.concatenate([main, _head_pad(w_in[:, d_inner + d_xbc:], r_heads)], axis=1)


def _ssd_w_in_unlayout(w, d_inner, d_xbc, r_heads):
    main = w[:, :d_inner + d_xbc]
    return jnp.concatenate([main, _head_unpad(w[:, d_inner + d_xbc:], r_heads)], axis=1)


SSD_CHUNKS = 8
SSD_CHUNKS_FWD = 8


def _ssd_dims(proj, xbc):
    d_xbc = xbc.shape[1]
    d_inner = d_xbc - 2 * N_SSD_GROUPS * D_STATE
    gw = d_inner // N_SSD_GROUPS
    return d_inner, d_xbc, gw, gw // HEAD_DIM


def _ssd_fwd(proj, xbc, bias_p, alog_p, dskip_p, norm_w, n_seq, *, name):
    t = proj.shape[0]
    d_inner, d_xbc, gw, r_heads = _ssd_dims(proj, xbc)
    q, n, n_g, p = CHUNK, D_STATE, N_SSD_GROUPS, HEAD_DIM
    seq = t // n_seq
    nc = seq // q
    cps = SSD_CHUNKS_FWD if nc % SSD_CHUNKS_FWD == 0 else 1
    dt_blk0 = (d_inner + d_xbc) // LANE

    def body(x_ref, b_ref, c_ref, z_ref, dtr_ref, bias_ref, alog_ref, dsk_ref, nw_ref, yn_ref, y_ref, hs_ref, h_scr):
        @pl.when(pl.program_id(2) == 0)
        def _():
            h_scr[...] = jnp.zeros_like(h_scr)

        for cc in range(cps):
            rows = pl.ds(cc * q, q)
            chunk(x_ref.at[rows, :], b_ref.at[rows, :], c_ref.at[rows, :], z_ref.at[rows, :], dtr_ref.at[rows, :],
                  bias_ref, alog_ref, dsk_ref, nw_ref, yn_ref.at[rows, :], y_ref.at[rows, :],
                  hs_ref.at[pl.ds(cc * n, n), :], h_scr)

    def chunk(x_ref, b_ref, c_ref, z_ref, dtr_ref, bias_ref, alog_ref, dsk_ref, nw_ref, yn_ref, y_ref, hs_ref, h_scr):
        dt, a, acum, acum_t, lower = _chunk_decay(dtr_ref[...], bias_ref[...], alog_ref[...])
        x = x_ref[...]
        bb = b_ref[...].astype(BF16)
        cb = c_ref[...].astype(BF16)
        g_mat = lax.dot_general(cb, bb, (((1,), (1,)), ((), ())), preferred_element_type=F32)
        h_prev = h_scr[...]
        hs_ref[...] = h_prev
        c_h = jnp.dot(cb, h_prev.astype(BF16), preferred_element_type=F32)
        _, spread = _head_selectors(gw, p)
        acum_s = _dot_exact(acum, spread)
        a_last_s = acum_s[q - 1:q, :]
        xdt = x * _dot_exact(dt, spread)
        xdt_b = xdt.astype(BF16)
        ys = []
        for h in range(r_heads):
            decay = jnp.exp(jnp.where(lower, acum[:, h:h + 1] - acum_t[h:h + 1, :], -jnp.inf))
            ys.append(jnp.dot((g_mat * decay).astype(BF16), xdt_b[:, h * p:(h + 1) * p], preferred_element_type=F32))
        y = jnp.concatenate(ys, axis=1) + jnp.exp(acum_s) * c_h + _row_spread(dsk_ref[...], spread) * x
        xd = xdt * jnp.exp(a_last_s - acum_s)
        states = lax.dot_general(bb, xd.astype(BF16), (((0,), (0,)), ((), ())), preferred_element_type=F32)
        h_scr[...] = h_prev * jnp.exp(a_last_s) + states
        y_ref[...] = y
        gated = y * _silu(z_ref[...])
        rstd = lax.rsqrt(jnp.mean(gated * gated, axis=-1, keepdims=True) + EPS)
        yn_ref[...] = (gated * rstd * nw_ref[...]).astype(BF16)

    row = lambda b, g, c: b * (nc // cps) + c
    vec = pl.BlockSpec((1, LANE), lambda b, g, c: (0, g))
    return pl.pallas_call(
        body,
        name=name,
        grid=(n_seq, n_g, nc // cps),
        in_specs=[
            pl.BlockSpec((q * cps, gw), lambda b, g, c: (row(b, g, c), g)),
            pl.BlockSpec((q * cps, n), lambda b, g, c: (row(b, g, c), d_inner // n + g)),
            pl.BlockSpec((q * cps, n), lambda b, g, c: (row(b, g, c), d_inner // n + n_g + g)),
            pl.BlockSpec((q * cps, gw), lambda b, g, c: (row(b, g, c), g)),
            pl.BlockSpec((q * cps, LANE), lambda b, g, c: (row(b, g, c), dt_blk0 + g)),
            vec, vec, vec,
            pl.BlockSpec((1, gw), lambda b, g, c: (0, g)),
        ],
        out_specs=[
            pl.BlockSpec((q * cps, gw), lambda b, g, c: (row(b, g, c), g)),
            pl.BlockSpec((q * cps, gw), lambda b, g, c: (row(b, g, c), g)),
            pl.BlockSpec((n * cps, gw), lambda b, g, c: (row(b, g, c), g)),
        ],
        out_shape=[
            jax.ShapeDtypeStruct((t, d_inner), BF16),
            jax.ShapeDtypeStruct((t, d_inner), F32),
            jax.ShapeDtypeStruct((n_seq * nc * n, d_inner), F32),
        ],
        scratch_shapes=[pltpu.VMEM((n, gw), F32)],
        compiler_params=_params(("parallel", "parallel", "arbitrary")),
    )(xbc, xbc, xbc, proj, proj, bias_p, alog_p, dskip_p, norm_w)


def _ssd_bwd(proj, xbc, hs, y, dyn, bias_p, alog_p, dskip_p, norm_w, n_seq, *, name):
    t = proj.shape[0]
    d_inner, d_xbc, gw, r_heads = _ssd_dims(proj, xbc)
    q, n, n_g, p = CHUNK, D_STATE, N_SSD_GROUPS, HEAD_DIM
    seq = t // n_seq
    nc = seq // q
    cps = SSD_CHUNKS if nc % SSD_CHUNKS == 0 else 1
    dt_blk0 = (d_inner + d_xbc) // LANE

    def body(x_ref, b_ref, c_ref, z_ref, dtr_ref, bias_ref, alog_ref, dsk_ref, nw_ref, hs_ref, y_ref, dyn_ref,
             dx_ref, db_ref, dc_ref, dz_ref, ddtr_ref, dnw_ref, dbias_ref, dalog_ref, ddsk_ref, dh_scr):
        @pl.when(pl.program_id(2) == 0)
        def _():
            dh_scr[...] = jnp.zeros_like(dh_scr)

        first_step = jnp.logical_and(pl.program_id(1) == 0, pl.program_id(2) == 0)
        for cc in reversed(range(cps)):
            rows = pl.ds(cc * q, q)
            chunk(jnp.logical_and(first_step, cc == cps - 1), x_ref.at[rows, :], b_ref.at[rows, :], c_ref.at[rows, :],
                  z_ref.at[rows, :], dtr_ref.at[rows, :], bias_ref, alog_ref, dsk_ref, nw_ref,
                  hs_ref.at[pl.ds(cc * n, n), :], y_ref.at[rows, :], dyn_ref.at[rows, :], dx_ref.at[rows, :],
                  db_ref.at[rows, :], dc_ref.at[rows, :], dz_ref.at[rows, :], ddtr_ref.at[rows, :],
                  dnw_ref, dbias_ref, dalog_ref, ddsk_ref, dh_scr)

    def chunk(first, x_ref, b_ref, c_ref, z_ref, dtr_ref, bias_ref, alog_ref, dsk_ref, nw_ref, hs_ref, y_ref, dyn_ref,
              dx_ref, db_ref, dc_ref, dz_ref, ddtr_ref, dnw_ref, dbias_ref, dalog_ref, ddsk_ref, dh_scr):
        dtraw = dtr_ref[...]
        dt, a, acum, acum_t, lower = _chunk_decay(dtraw, bias_ref[...], alog_ref[...])
        x = x_ref[...]
        bb = b_ref[...].astype(BF16)
        cb = c_ref[...].astype(BF16)
        g_mat = lax.dot_general(cb, bb, (((1,), (1,)), ((), ())), preferred_element_type=F32)

        yv = y_ref[...]
        z = z_ref[...]
        sz = _silu(z)
        gated = yv * sz
        rstd = lax.rsqrt(jnp.mean(gated * gated, axis=-1, keepdims=True) + EPS)
        gn = gated * rstd
        dynv = dyn_ref[...]
        gwt = dynv * nw_ref[...]
        dgated = rstd * (gwt - gn * jnp.mean(gwt * gn, axis=-1, keepdims=True))
        dnw = jnp.sum(dynv * gn, axis=0, keepdims=True)
        dy = dgated * sz
        dz_ref[...] = (dgated * yv * _dsilu(z)).astype(BF16)

        h_prev = hs_ref[...]
        h_prev_b = h_prev.astype(BF16)
        ds = dh_scr[...]
        ds_b = ds.astype(BF16)
        sum_heads, spread = _head_selectors(gw, p)
        acum_s = _dot_exact(acum, spread)
        a_last_s = acum_s[q - 1:q, :]
        dt_s = _dot_exact(dt, spread)
        dsk_s = _row_spread(dsk_ref[...], spread)
        dte_s = jnp.exp(a_last_s - acum_s)
        cd_s = jnp.exp(a_last_s)
        xdt = x * dt_s
        xdt_b = xdt.astype(BF16)
        dy_b = dy.astype(BF16)
        gt_mat = lax.dot_general(bb, cb, (((1,), (1,)), ((), ())), preferred_element_type=F32)
        upper = lax.broadcasted_iota(jnp.int32, (q, q), 0) <= lax.broadcasted_iota(jnp.int32, (q, q), 1)
        dg = jnp.zeros((q, q), F32)
        dxdts, w_diffs = [], []
        for h in range(r_heads):
            hsl = slice(h * p, (h + 1) * p)
            diff = acum[:, h:h + 1] - acum_t[h:h + 1, :]
            decay = jnp.exp(jnp.where(lower, diff, -jnp.inf))
            decay_t = jnp.exp(jnp.where(upper, -diff, -jnp.inf))
            mt_mat = gt_mat * decay_t
            dm = lax.dot_general(dy_b[:, hsl], xdt_b[:, hsl], (((1,), (1,)), ((), ())), preferred_element_type=F32)
            dm_t = lax.dot_general(xdt_b[:, hsl], dy_b[:, hsl], (((1,), (1,)), ((), ())), preferred_element_type=F32)
            dg = dg + dm * decay
            dxdts.append(jnp.dot(mt_mat.astype(BF16), dy_b[:, hsl], preferred_element_type=F32))
            w_diffs.append(dm * (g_mat * decay) - dm_t * mt_mat)
        sel_q = (lax.broadcasted_iota(jnp.int32, (r_heads * q, LANE), 0) // q
                 == lax.broadcasted_iota(jnp.int32, (r_heads * q, LANE), 1)).astype(BF16)
        dacum_diag = _dot_exact(jnp.concatenate(w_diffs, axis=1), sel_q)
        c_h = jnp.dot(cb, h_prev_b, preferred_element_type=F32)
        dxd = jnp.dot(bb, ds_b, preferred_element_type=F32)
        dxdt = jnp.concatenate(dxdts, axis=1) + dxd * dte_s
        dye = dy * jnp.exp(acum_s)
        dye_b = dye.astype(BF16)
        xd = xdt * dte_s
        xd_b = xd.astype(BF16)
        dg_b = dg.astype(BF16)
        dx_ref[...] = dxdt * dt_s + dsk_s * dy
        dc_ref[...] = (jnp.dot(dg_b, bb, preferred_element_type=F32)
                       + lax.dot_general(dye_b, h_prev_b, (((1,), (1,)), ((), ())), preferred_element_type=F32))
        db_ref[...] = (lax.dot_general(dg_b, cb, (((0,), (0,)), ((), ())), preferred_element_type=F32)
                       + lax.dot_general(xd_b, ds_b, (((1,), (1,)), ((), ())), preferred_element_type=F32))
        dh_scr[...] = ds * cd_s + lax.dot_general(cb, dye_b, (((0,), (0,)), ((), ())), preferred_element_type=F32)
        ddt_cols = _dot_exact(x * dxdt, sum_heads)
        dacum_y = _dot_exact(dye * c_h - dxd * xd, sum_heads)
        col_sums = jnp.concatenate([
            jnp.sum(dxd * xd, axis=0, keepdims=True) + jnp.sum(ds * h_prev, axis=0, keepdims=True) * cd_s,
            jnp.sum(dy * x, axis=0, keepdims=True),
            jnp.zeros((6, gw), F32)], axis=0)
        col_sums = _dot_exact(col_sums, sum_heads)
        ddsk = col_sums[1:2, :]
        rows_q = lax.broadcasted_iota(jnp.int32, (q, LANE), 0)
        dacum = dacum_diag + dacum_y + jnp.where(rows_q == q - 1, col_sums[0:1, :], 0.0)
        dadt = jnp.dot(upper.astype(F32), dacum, precision=lax.Precision.HIGHEST, preferred_element_type=F32)
        ddt = dadt * a + ddt_cols
        ddtr = ddt * _sigmoid(dtraw + bias_ref[...])
        ddtr_ref[...] = ddtr
        dbias = jnp.sum(ddtr, axis=0, keepdims=True)
        dalog = jnp.sum(dadt * dt, axis=0, keepdims=True) * a

        @pl.when(first)
        def _():
            dnw_ref[...] = dnw
            dbias_ref[...] = dbias
            dalog_ref[...] = dalog
            ddsk_ref[...] = ddsk

        @pl.when(jnp.logical_not(first))
        def _():
            dnw_ref[...] += dnw
            dbias_ref[...] += dbias
            dalog_ref[...] += dalog
            ddsk_ref[...] += ddsk

    row = lambda g, b, c: b * (nc // cps) + (nc // cps - 1 - c)
    vec = pl.BlockSpec((1, LANE), lambda g, b, c: (0, g))
    wide = pl.BlockSpec((q * cps, gw), lambda g, b, c: (row(g, b, c), g))
    narrow = pl.BlockSpec((q * cps, n), lambda g, b, c: (row(g, b, c), g))
    return pl.pallas_call(
        body,
        name=name,
        grid=(n_g, n_seq, nc // cps),
        in_specs=[
            wide,
            pl.BlockSpec((q * cps, n), lambda g, b, c: (row(g, b, c), d_inner // n + g)),
            pl.BlockSpec((q * cps, n), lambda g, b, c: (row(g, b, c), d_inner // n + n_g + g)),
            wide,
            pl.BlockSpec((q * cps, LANE), lambda g, b, c: (row(g, b, c), dt_blk0 + g)),
            vec, vec, vec,
            pl.BlockSpec((1, gw), lambda g, b, c: (0, g)),
            pl.BlockSpec((n * cps, gw), lambda g, b, c: (row(g, b, c), g)),
            wide, wide,
        ],
        out_specs=[
            wide, narrow, narrow, wide, narrow,
            pl.BlockSpec((1, gw), lambda g, b, c: (0, g)),
            vec, vec, vec,
        ],
        out_shape=[
            jax.ShapeDtypeStruct((t, d_inner), F32),
            jax.ShapeDtypeStruct((t, n_g * n), F32),
            jax.ShapeDtypeStruct((t, n_g * n), F32),
            jax.ShapeDtypeStruct(proj.shape, BF16),
            jax.ShapeDtypeStruct((t, n_g * LANE), F32),
            jax.ShapeDtypeStruct((1, d_inner), F32),
            jax.ShapeDtypeStruct((1, n_g * LANE), F32),
            jax.ShapeDtypeStruct((1, n_g * LANE), F32),
            jax.ShapeDtypeStruct((1, n_g * LANE), F32),
        ],
        scratch_shapes=[pltpu.VMEM((n, gw), F32)],
        compiler_params=_params(("parallel", "arbitrary", "arbitrary")),
    )(xbc, xbc, xbc, proj, proj, bias_p, alog_p, dskip_p, norm_w, hs, y, dyn)


MESH_IDS = pl.DeviceIdType.MESH


def _my_index():
    return 4 * lax.axis_index("x") + 2 * lax.axis_index("y") + lax.axis_index("c")


def _all_gather(shard, *, name):
    def body(x_ref, out_ref, send_sems, recv_sems, local_sem):
        x, y, c = lax.axis_index("x"), lax.axis_index("y"), lax.axis_index("c")
        me, sibling = (x, y, c), (x, y, 1 - c)
        chips = [(1 - x, y), (x, 1 - y), (1 - x, 1 - y)]

        def blk(px, py, pc):
            return out_ref.at[4 * px + 2 * py + pc]

        def copy(k, block, to, src=None):
            return pltpu.make_async_remote_copy(
                src_ref=blk(*block) if src is None else src, dst_ref=blk(*block),
                send_sem=send_sems.at[k], recv_sem=recv_sems.at[k], device_id=to, device_id_type=MESH_IDS)

        mine = pltpu.make_async_copy(x_ref, blk(*me), local_sem)
        mine.start()
        first = [copy(0, me, sibling, src=x_ref)]
        first += [copy(1 + j, me, (*chip, c), src=x_ref) for j, chip in enumerate(chips)]
        for cp in first:
            cp.start()
        passed = [copy(4 + j, (*chip, c), sibling) for j, chip in enumerate(chips)]
        for j, chip in enumerate(chips):
            copy(1 + j, (*chip, c), me).wait_recv()
            passed[j].start()
        copy(0, sibling, me).wait_recv()
        for j, chip in enumerate(chips):
            copy(4 + j, (*chip, 1 - c), me).wait_recv()
        for cp in first + passed:
            cp.wait_send()
        mine.wait()

    return pl.pallas_call(
        body,
        name=name,
        in_specs=[ANY],
        out_specs=ANY,
        out_shape=jax.ShapeDtypeStruct((N_DEV,) + shard.shape, shard.dtype),
        scratch_shapes=[pltpu.SemaphoreType.DMA((7,)), pltpu.SemaphoreType.DMA((7,)), pltpu.SemaphoreType.DMA],
    )(shard)


HBM_SPEC = pl.BlockSpec(memory_space=pltpu.HBM)
SEM_SPEC = pl.BlockSpec(memory_space=pltpu.SEMAPHORE)
SPLIT_COPY_PARAMS = pltpu.CompilerParams(has_side_effects=pltpu.SideEffectType.DATAFLOW_SIDE_EFFECTING)


def _peer_list():
    x, y, c = lax.axis_index("x"), lax.axis_index("y"), lax.axis_index("c")
    peers = []
    for k in range(1, N_DEV):
        px = 1 - x if k & 4 else x
        py = 1 - y if k & 2 else y
        pc = 1 - c if k & 1 else c
        peers.append(((px, py, pc), 4 * px + 2 * py + pc))
    return 4 * x + 2 * y + c, peers


def _push_copies(src_refs, land_refs, send_sems, recv_sems, blockwise):
    me, peers = _peer_list()
    copies = []
    for a, (src_ref, land_ref) in enumerate(zip(src_refs, land_refs)):
        for k, (dev, idx) in enumerate(peers):
            sem = a * (N_DEV - 1) + k
            src = src_ref.at[idx] if blockwise else src_ref
            copies.append(tuple(
                pltpu.make_async_remote_copy(src_ref=src, dst_ref=land_ref.at[slot], send_sem=send_sems.at[sem],
                                             recv_sem=recv_sems.at[sem], device_id=dev, device_id_type=MESH_IDS)
                for slot in (me, idx)))
    return copies


def _push_start(srcs, blockwise, after, *, name):
    n = len(srcs)
    blocks = [s_.shape[1:] if blockwise else s_.shape for s_ in srcs]

    def body(*refs):
        src_refs, land_refs = refs[:n], refs[n:2 * n]
        send_sems, recv_sems = refs[2 * n + 1], refs[2 * n + 2]
        token = refs[-1]
        for send, _ in _push_copies(src_refs, land_refs, send_sems, recv_sems, blockwise):
            send.start()
        token[...] = jnp.zeros_like(token)

    n_sem = n * (N_DEV - 1)
    lands = [lax.empty((N_DEV,) + b, s_.dtype) for b, s_ in zip(blocks, srcs)]
    out = pl.pallas_call(
        body,
        name=name,
        in_specs=[HBM_SPEC] * (2 * n) + [ANY],
        out_specs=(SEM_SPEC, SEM_SPEC) + (HBM_SPEC,) * (2 * n) + (pl.BlockSpec(memory_space=pltpu.VMEM),),
        out_shape=(pltpu.SemaphoreType.DMA((n_sem,)), pltpu.SemaphoreType.DMA((n_sem,)))
        + tuple(pltpu.HBM(a.shape, a.dtype) for a in list(srcs) + lands)
        + (jax.ShapeDtypeStruct((8, LANE), F32),),
        input_output_aliases={i: 2 + i for i in range(2 * n)},
        compiler_params=SPLIT_COPY_PARAMS,
    )(*[pltpu.with_memory_space_constraint(a, pltpu.HBM) for a in list(srcs) + lands], after)
    return out[0], out[1], out[2:2 + n], out[2 + n:2 + 2 * n], out[-1]


def _push_wait(send_sems, recv_sems, srcs, lands, blockwise, after, *, name):
    n = len(srcs)

    def body(*refs):
        src_refs, land_refs = refs[:n], refs[n:2 * n]
        send_sems, recv_sems = refs[2 * n], refs[2 * n + 1]
        for send, recv in _push_copies(src_refs, land_refs, send_sems, recv_sems, blockwise):
            send.wait_send()
            recv.wait_recv()

    out = pl.pallas_call(
        body,
        name=name,
        in_specs=[HBM_SPEC] * (2 * n) + [SEM_SPEC, SEM_SPEC, ANY],
        out_specs=(HBM_SPEC,) * (2 * n),
        out_shape=tuple(pltpu.HBM(a.shape, a.dtype) for a in list(srcs) + list(lands)),
        input_output_aliases={i: i for i in range(2 * n)},
        compiler_params=SPLIT_COPY_PARAMS,
    )(*srcs, *lands, send_sems, recv_sems, after)
    return out[n:]


def _with_own_slot(landing, own):
    slot = lax.broadcasted_iota(jnp.int32, (N_DEV,) + (1,) * own.ndim, 0)
    return jnp.where(slot == _my_index(), own[None], landing)


def _sum_slots(parts, own=None, *, name):
    shape = parts.shape[1:]
    n, c = parts.shape[0], parts.shape[-1]
    r = parts.size // (n * c)
    tm = _pick(r, (256, 128, 64, 32, 16, 8))

    def body(p_ref, *rest):
        o_ref = rest[-1]
        me = _my_index()

        def slot(s):
            if own is None:
                return p_ref[s].astype(F32)
            return jnp.where(me == s, rest[0][...], p_ref[s]).astype(F32)

        acc = slot(0)
        for s in range(1, n):
            acc = acc + slot(s)
        o_ref[...] = acc

    tile = pl.BlockSpec((tm, c), lambda i: (i, 0))
    return pl.pallas_call(
        body,
        name=name,
        grid=(r // tm,),
        in_specs=[pl.BlockSpec((n, tm, c), lambda i: (0, i, 0))] + ([] if own is None else [tile]),
        out_specs=tile,
        out_shape=jax.ShapeDtypeStruct((r, c), F32),
        compiler_params=_params(("parallel",)),
    )(parts.reshape(n, r, c), *([] if own is None else [own.reshape(r, c)])).reshape(shape)


def _row_count(shape):
    c = shape[-1]
    rows = 1
    for s in shape[:-1]:
        rows *= s
    return rows, c, c + (-c) % LANE


PACK_ROWS = 256


def _pack_rows(arrays):
    pieces = []
    for a in arrays:
        rows, c, cp = _row_count(a.shape)
        a2 = a.reshape(rows, c)
        if cp > c:
            a2 = jnp.pad(a2, ((0, 0), (0, cp - c)))
        a2 = a2.reshape(rows * cp // LANE, LANE)
        if a2.shape[0] % 8:
            a2 = jnp.pad(a2, ((0, 8 - a2.shape[0] % 8), (0, 0)))
        pieces.append(a2)
    total = sum(p.shape[0] for p in pieces)
    if total % PACK_ROWS:
        pieces.append(jnp.zeros((PACK_ROWS - total % PACK_ROWS, LANE), F32))
    return jnp.concatenate(pieces, axis=0)


def _unpack_rows(packed, shapes, lead=()):
    out, off = [], 0
    for shp in shapes:
        rows, c, cp = _row_count(shp)
        n_rows = rows * cp // LANE
        seg = packed[..., off:off + n_rows, :].reshape(lead + (rows, cp))
        out.append(seg[..., :c].reshape(lead + tuple(shp)))
        off += n_rows + (-n_rows) % 8
    return out


def _unshard(stacked, axis):
    if axis == stacked.ndim - 2:
        return jnp.concatenate([stacked[d] for d in range(N_DEV)], axis=axis)
    moved = jnp.moveaxis(stacked, 0, axis)
    shp = moved.shape
    return moved.reshape(shp[:axis] + (shp[axis] * shp[axis + 1],) + shp[axis + 2:])


def _shard_major(full, axis):
    shp = full.shape
    if axis == full.ndim - 1:
        size = shp[axis] // N_DEV
        return jnp.stack([full[..., d * size:(d + 1) * size] for d in range(N_DEV)])
    split = full.reshape(shp[:axis] + (N_DEV, shp[axis] // N_DEV) + shp[axis + 1:])
    return jnp.moveaxis(split, axis, 0)


def _my_shard(full, axis):
    size = full.shape[axis] // N_DEV
    return lax.dynamic_slice_in_dim(full, _my_index() * size, size, axis)


def _local_step(x, target, w, fetch, emit, n_seq):
    def with_token(vec, token):
        return vec + jnp.tile(token[0:1, :], (1, vec.shape[1] // LANE))

    depth, d_model = w["norm_mix_pre"].shape
    d_inner = w["ssd_norm_w"].shape[1]
    d_xbc = w["ssd_conv_w"].shape[2]
    saved = []
    m, token = fetch(0, "mix", x)
    mix_pre_w = with_token(w["norm_mix_pre"][0:1], token)
    u = _rms_fwd(x, mix_pre_w, out_dtype=BF16, name="l0_mix_pre")
    for i in range(depth):
        j = i // 2
        s = {"x": x, "mix_pre_w": mix_pre_w}
        if i % 2 == 0:
            proj = _mm(u, m["ssd_w_in"], name=f"l{i}_ssd_in")
            xbc, xbc_pre = _ssd_conv_fwd(proj, d_inner, d_xbc, w["ssd_conv_w"][j], w["ssd_conv_b"][j:j + 1], n_seq,
                                         name=f"l{i}_ssd_conv")
            yn, y, hs = _ssd_fwd(proj, xbc, w["ssd_dt_bias"][j:j + 1], w["ssd_a_log"][j:j + 1], w["ssd_d"][j:j + 1],
                                 w["ssd_norm_w"][j:j + 1], n_seq, name=f"l{i}_ssd_scan")
            m_out, token = fetch(i, "out", yn)
            m = {**m, **m_out}
            mix = _mm(yn, m["ssd_w_out"], name=f"l{i}_ssd_out")
            s.update(u=u, proj=proj, xbc=xbc, xbc_pre=xbc_pre, yn=yn, y=y, hs=hs)
        else:
            mix = _pool_fwd(u, m["pool_w"], w["pool_scale"][j:j + 1], n_seq, name=f"l{i}_pool")
            s.update(u=u)
            token = jnp.zeros((8, LANE), F32)
        mix_post_w = with_token(w["norm_mix_post"][i:i + 1], token)
        m_ffn, token = fetch(i, "ffn", mix)
        m = {**m, **m_ffn}
        ffn_pre_w = with_token(w["norm_ffn_pre"][i:i + 1], token)
        x1, n = _res_rms_rms(x, mix, mix_post_w, ffn_pre_w, name=f"l{i}_mix_post_ffn_pre")
        h, hc, a = _ffn_up_act(n, m["ffn_w_up"], w["ffn_conv_w"][i], w["ffn_conv_b"][i:i + 1], n_seq,
                               name=f"l{i}_ffn_up_act")
        f = _mm(a, m["ffn_w_down"], name=f"l{i}_ffn_down")
        s.update(mix=mix, x1=x1, n=n, h=h, hc=hc, a=a, f=f, m=m, ffn_pre_w=ffn_pre_w)
        saved.append(s)
        if i + 1 < depth:
            m, token = fetch(i + 1, "mix", f)
            mix_pre_w = with_token(w["norm_mix_pre"][i + 1:i + 2], token)
            x, u = _res_rms_rms(x1, f, w["norm_ffn_post"][i:i + 1], mix_pre_w,
                                out_dtype=BF16 if (i + 1) % 2 == 0 else F32, name=f"l{i}_ffn_post_mix_pre")
        else:
            x = _res_rms_fwd(x1, f, w["norm_ffn_post"][i:i + 1], name=f"l{i}_ffn_post")

    loss, dx = _loss_head(x, target)
    grads = {k: [None] * len(w[k]) for k in SMALL}
    df, grads["norm_ffn_post"][depth - 1] = _rms_bwd(saved[-1]["f"], w["norm_ffn_post"][depth - 1:depth], dx, None,
                                                      out_dtype=BF16, name=f"l{depth - 1}_ffn_post_b")
    for i in reversed(range(depth)):
        j = i // 2
        s = saved[i]
        m, gm = s["m"], {}
        gm["ffn_w_down"] = _mm(s["a"], df, ta=True, name=f"l{i}_ffn_down_bw")
        dh, grads["ffn_conv_w"][i], grads["ffn_conv_b"][i] = _ffn_down_bx_act_bwd(
            df, m["ffn_w_down"], s["h"], s["hc"], w["ffn_conv_w"][i], n_seq, name=f"l{i}_ffn_act_b")
        dn = _mm(dh, m["ffn_w_up"], tb=True, name=f"l{i}_ffn_up_bx")
        gm["ffn_w_up"] = _mm(s["n"], dh, ta=True, name=f"l{i}_ffn_up_bw")
        token = emit(i, "ffn", gm, dn)
        gm = {}
        dx1, dmix, grads["norm_ffn_pre"][i], grads["norm_mix_post"][i] = _rms_bwd2(
            s["x1"], s["ffn_pre_w"], dn, dx, s["mix"], with_token(w["norm_mix_post"][i:i + 1], token),
            out_dtype=BF16 if i % 2 == 0 else F32, name=f"l{i}_ffn_pre_mix_post_b")
        if i % 2 == 0:
            dyn = _mm(dmix, m["ssd_w_out"], tb=True, name=f"l{i}_ssd_out_bx")
            gm["ssd_w_out"] = _mm(s["yn"], dmix, ta=True, name=f"l{i}_ssd_out_bw")
            token = emit(i, "out", gm, dyn)
            gm = {}
            dxs, db, dc, dz, ddtr, dnw, dbias, dalog, ddsk = _ssd_bwd(
                s["proj"], s["xbc"], s["hs"], s["y"], dyn, w["ssd_dt_bias"][j:j + 1], w["ssd_a_log"][j:j + 1],
                w["ssd_d"][j:j + 1], with_token(w["ssd_norm_w"][j:j + 1], token), n_seq, name=f"l{i}_ssd_scan_b")
            grads["ssd_norm_w"][j], grads["ssd_dt_bias"][j], grads["ssd_a_log"][j], grads["ssd_d"][j] = (
                dnw, dbias, dalog, ddsk)
            dproj, grads["ssd_conv_w"][j], grads["ssd_conv_b"][j] = _ssd_conv_bwd(
                s["proj"], d_inner, w["ssd_conv_w"][j], s["xbc_pre"], (dxs, db, dc), dz, n_seq,
                name=f"l{i}_ssd_conv_b")
            dproj = _fill_cols(dproj, ddtr, d_inner + d_xbc, name=f"l{i}_ssd_dt_b")
            gm["ssd_w_in"] = _mm(s["u"], dproj, ta=True, name=f"l{i}_ssd_in_bw")
            token = emit(i, "mix", gm, dproj)
            du = _mm(dproj, m["ssd_w_in"], tb=True, name=f"l{i}_ssd_in_bx")
        else:
            du, gm["pool_w"], grads["pool_scale"][j] = _pool_bwd(
                s["u"], m["pool_w"], w["pool_scale"][j:j + 1], dmix, n_seq, name=f"l{i}_pool_b")
            token = emit(i, "mix", gm, du)
        if i > 0:
            dx, df, grads["norm_mix_pre"][i], grads["norm_ffn_post"][i - 1] = _rms_bwd2(
                s["x"], with_token(s["mix_pre_w"], token), du, dx1, saved[i - 1]["f"], w["norm_ffn_post"][i - 1:i],
                out_dtype=BF16, name=f"l{i}_mix_pre_ffn_post_b")
        else:
            dx, grads["norm_mix_pre"][i] = _rms_bwd(s["x"], with_token(s["mix_pre_w"], token), du, dx1,
                                                    name=f"l{i}_mix_pre_b")
    return loss, dx, grads


BIG = (("ssd_w_in", 2), ("ssd_w_out", 1), ("pool_w", 2), ("ffn_w_up", 2), ("ffn_w_down", 1))
SMALL_SHARDED = (("ssd_conv_w", 2), ("ffn_conv_w", 2), ("pool_scale", 1))
SMALL = ("ssd_conv_w", "ssd_conv_b", "ssd_dt_bias", "ssd_a_log", "ssd_d", "ssd_norm_w", "pool_scale", "ffn_conv_w",
         "ffn_conv_b", "norm_mix_pre", "norm_mix_post", "norm_ffn_pre", "norm_ffn_post")
WEIGHTS = ("ssd_w_in", "ssd_conv_w", "ssd_conv_b", "ssd_dt_bias", "ssd_a_log", "ssd_d", "ssd_norm_w", "ssd_w_out",
           "pool_w", "pool_scale", "ffn_w_up", "ffn_conv_w", "ffn_conv_b", "ffn_w_down", "norm_mix_pre",
           "norm_mix_post", "norm_ffn_pre", "norm_ffn_post")


def _ssd_sizes(d_inner):
    return d_inner + 2 * N_SSD_GROUPS * D_STATE, d_inner // HEAD_DIM // N_SSD_GROUPS


def _small_compute_layout(full, d_inner):
    _, r_heads = _ssd_sizes(d_inner)
    w = {k: full[k] for k in SMALL}
    for k in ("ssd_dt_bias", "ssd_a_log", "ssd_d"):
        w[k] = _head_pad(full[k], r_heads)
    for k in ("ffn_conv_w", "ffn_conv_b"):
        w[k] = _interleave(full[k])
    return w


def _matmul_compute_layout(k, full, d_inner):
    d_xbc, r_heads = _ssd_sizes(d_inner)
    if k == "ssd_w_in":
        return _ssd_w_in_layout(full, d_inner, d_xbc, r_heads)
    if k == "ffn_w_up":
        return _interleave(full)
    return full


def _layer_matrices(i, part):
    if part == "ffn":
        return (("ffn_w_up", 1, i), ("ffn_w_down", 0, i))
    if i % 2 == 1:
        return (("pool_w", 1, i // 2),) if part == "mix" else ()
    return (("ssd_w_in", 1, i // 2),) if part == "mix" else (("ssd_w_out", 0, i // 2),)


def _fetch_group(i, part):
    mix, out, ffn = (_layer_matrices(i, p) for p in ("mix", "out", "ffn"))
    if i % 2 == 1:
        return mix + ffn if part == "mix" else ()
    if i == 0:
        return {"mix": mix, "out": out + ffn, "ffn": ()}[part]
    return {"mix": mix + out, "out": (), "ffn": ffn}[part]


def _matmul_grad_reference_layout(k, g, d_inner):
    d_xbc, r_heads = _ssd_sizes(d_inner)
    if k == "ssd_w_in":
        return _ssd_w_in_unlayout(g, d_inner, d_xbc, r_heads)
    if k == "ffn_w_up":
        return _deinterleave(g)
    return g


def _small_grads_reference_layout(grads, shapes, d_inner):
    _, r_heads = _ssd_sizes(d_inner)
    g = {k: jnp.stack(grads[k]) for k in SMALL}
    for k in ("ssd_dt_bias", "ssd_a_log", "ssd_d"):
        g[k] = _head_unpad(g[k][:, 0], r_heads)
    for k in ("ffn_conv_w", "ffn_conv_b"):
        g[k] = _deinterleave(g[k])
    return {k: v.reshape(shapes[k]) for k, v in g.items()}


def kernel(x, ssd_w_in, ssd_conv_w, ssd_conv_b, ssd_dt_bias, ssd_a_log, ssd_d, ssd_norm_w, ssd_w_out, pool_w, pool_scale, ffn_w_up, ffn_conv_w, ffn_conv_b, ffn_w_down, norm_mix_pre, norm_mix_post, norm_ffn_pre, norm_ffn_post, loss_target, m_ssd_w_in, m_ssd_conv_w, m_ssd_conv_b, m_ssd_dt_bias, m_ssd_a_log, m_ssd_d, m_ssd_norm_w, m_ssd_w_out, m_pool_w, m_pool_scale, m_ffn_w_up, m_ffn_conv_w, m_ffn_conv_b, m_ffn_w_down, m_norm_mix_pre, m_norm_mix_post, m_norm_ffn_pre, m_norm_ffn_post, v_ssd_w_in, v_ssd_conv_w, v_ssd_conv_b, v_ssd_dt_bias, v_ssd_a_log, v_ssd_d, v_ssd_norm_w, v_ssd_w_out, v_pool_w, v_pool_scale, v_ffn_w_up, v_ffn_conv_w, v_ffn_conv_b, v_ffn_w_down, v_norm_mix_pre, v_norm_mix_post, v_norm_ffn_pre, v_norm_ffn_post):
    shards = dict(ssd_w_in=ssd_w_in, ssd_conv_w=ssd_conv_w, ssd_conv_b=ssd_conv_b, ssd_dt_bias=ssd_dt_bias,
                  ssd_a_log=ssd_a_log, ssd_d=ssd_d, ssd_norm_w=ssd_norm_w, ssd_w_out=ssd_w_out, pool_w=pool_w,
                  pool_scale=pool_scale, ffn_w_up=ffn_w_up, ffn_conv_w=ffn_conv_w, ffn_conv_b=ffn_conv_b,
                  ffn_w_down=ffn_w_down, norm_mix_pre=norm_mix_pre, norm_mix_post=norm_mix_post,
                  norm_ffn_pre=norm_ffn_pre, norm_ffn_post=norm_ffn_post)
    moments_m = dict(zip(WEIGHTS, (m_ssd_w_in, m_ssd_conv_w, m_ssd_conv_b, m_ssd_dt_bias, m_ssd_a_log, m_ssd_d, m_ssd_norm_w, m_ssd_w_out, m_pool_w, m_pool_scale, m_ffn_w_up, m_ffn_conv_w, m_ffn_conv_b, m_ffn_w_down, m_norm_mix_pre, m_norm_mix_post, m_norm_ffn_pre, m_norm_ffn_post)))
    moments_v = dict(zip(WEIGHTS, (v_ssd_w_in, v_ssd_conv_w, v_ssd_conv_b, v_ssd_dt_bias, v_ssd_a_log, v_ssd_d, v_ssd_norm_w, v_ssd_w_out, v_pool_w, v_pool_scale, v_ffn_w_up, v_ffn_conv_w, v_ffn_conv_b, v_ffn_w_down, v_norm_mix_pre, v_norm_mix_post, v_norm_ffn_pre, v_norm_ffn_post)))
    n_seq, seq, d_model = x.shape
    t = n_seq * seq

    d_inner = ssd_norm_w.shape[1]
    depth = norm_mix_pre.shape[0]
    x2 = x.reshape(t, d_model)

    shard16 = {k: shards[k].astype(BF16) for k, _ in BIG}
    order = [(i, part) for i in range(depth) for part in ("mix", "out", "ffn") if _fetch_group(i, part)]
    fetches = {}

    def start_fetch(key, after):
        srcs = [shard16[k][l] for k, _, l in _fetch_group(*key)]
        fetches[key] = _push_start(srcs, False, after, name=f"fetch{key[0]}{key[1]}_start")

    full = dict(shards)
    small_all = _all_gather(_pack_rows([shards[k] for k, _ in SMALL_SHARDED]), name="gather_small_weights")
    small_stacked = _unpack_rows(small_all, [shards[k].shape for k, _ in SMALL_SHARDED], lead=(N_DEV,))
    for (k, axis), st in zip(SMALL_SHARDED, small_stacked):
        full[k] = _unshard(st, axis)
    w = _small_compute_layout(full, d_inner)
    ready = {}

    def fetch(i, part, x_now):
        key = (i, part)
        token = jnp.zeros((8, LANE), F32)
        if key in order:
            if key == order[0]:
                wholes = [_unshard(_all_gather(shard16[k][l], name=f"fetch0_{k}"), axis) for k, axis, l in _fetch_group(i, part)]
                nxt_after = wholes[0]
            else:
                send, recv, srcs, lands, _ = fetches[key]
                lands = _push_wait(send, recv, srcs, lands, False, x_now, name=f"fetch{i}{part}_wait")
                wholes = [_unshard(_with_own_slot(land, shard16[k][l]), axis)
                          for (k, axis, l), land in zip(_fetch_group(i, part), lands)]
                nxt_after = lands[0]
            for (k, _, l), whole in zip(_fetch_group(i, part), wholes):
                ready[k, l] = _matmul_compute_layout(k, whole, d_inner)
            nxt = order.index(key) + 1
            if nxt < len(order):
                start_fetch(order[nxt], nxt_after)
                token = fetches[order[nxt]][4]
        return {k: ready[k, l] for k, _, l in _layer_matrices(i, part)}, token

    g_layers = {}
    in_flight = []

    def finish_exchange(after):
        key, blocks, (send, recv, srcs, lands, _) = in_flight.pop(0)
        lands = _push_wait(send, recv, srcs, lands, True, after, name=f"exchange{key[0]}{key[1]}_wait")
        for (k, _, l), land, block in zip(_layer_matrices(*key), lands, blocks):
            own = lax.dynamic_index_in_dim(block, _my_index(), 0, keepdims=False)
            g_layers[k, l] = _sum_slots(land, own, name=f"sum{key[0]}_{k}")

    def emit(i, part, gm, dx_now):
        if len(in_flight) >= 2:
            finish_exchange(dx_now)
        blocks = [_shard_major(_matmul_grad_reference_layout(k, gm[k].astype(BF16), d_inner), axis)
                  for k, axis, _ in _layer_matrices(i, part)]
        started = _push_start(blocks, True, dx_now, name=f"exchange{i}{part}_start")
        in_flight.append(((i, part), blocks, started))
        return started[4]

    loss, dx, grads = _local_step(x2, loss_target.reshape(t, d_model), w, fetch, emit, n_seq)
    loss = lax.psum(loss, ("x", "y", "c"))

    g_shard = {}
    small_shapes = {k: full[k].shape for k in SMALL}
    g_small = _small_grads_reference_layout(grads, small_shapes, d_inner)
    s_all = _all_gather(_pack_rows([g_small[k] for k in SMALL]) + in_flight[-1][2][4][0:1, :], name="gather_small_grads")
    for k, g in zip(SMALL, _unpack_rows(_sum_slots(s_all, name="sum_small_grads"), [small_shapes[k] for k in SMALL])):
        g_shard[k] = g
    for k, axis in SMALL_SHARDED:
        g_shard[k] = _my_shard(g_shard[k], axis)

    last = [k for key, _, _ in in_flight for k, _, _ in _layer_matrices(*key)]
    deltas, new_m, new_v = {}, {}, {}
    for k in [k for k in WEIGHTS if k not in last] + last:
        if k == last[0]:
            while in_flight:
                finish_exchange(deltas["ffn_w_up"])
        if k in dict(BIG):
            g_shard[k] = jnp.stack([g_layers[k, l] for l in range(shards[k].shape[0])])
        deltas[k], new_m[k], new_v[k] = _adamw(shards[k], g_shard[k], moments_m[k], moments_v[k], name=f"adamw_{k}")
    return (loss, dx.reshape(x.shape), *[g_shard[k] for k in WEIGHTS], *[deltas[k] for k in WEIGHTS],
            *[new_m[k] for k in WEIGHTS], *[new_v[k] for k in WEIGHTS])
```

```python
import functools

import jax
import jax.numpy as jnp
from jax import lax
from jax.experimental import pallas as pl
from jax.experimental.pallas import tpu as pltpu

F32 = jnp.float32
BF16 = jnp.bfloat16

N_DEV = 8
HEAD_DIM = 64
N_SSD_GROUPS = 4
D_STATE = 128
CHUNK = 128
POOL_WINDOWS = (2, 4, 8, 16)
EPS = 1e-6
LANE = 128
ADAM_LR = 0.001
ADAM_B1 = 0.9
ADAM_B2 = 0.999
ADAM_EPS = 1e-08
ADAM_WD = 0.01
ADAM_STEP = 10
VMEM_LIMIT = 56 * 1024 * 1024
ANY = pl.BlockSpec(memory_space=pl.ANY)


def _pick(n, cands):
    for c in cands:
        if n % c == 0:
            return c
    return n


def _params(sem):
    return pltpu.CompilerParams(dimension_semantics=sem, vmem_limit_bytes=VMEM_LIMIT)


def _sigmoid(x):
    return 0.5 * jnp.tanh(0.5 * x) + 0.5


def _silu(x):
    return x * _sigmoid(x)


def _dsilu(x):
    s = _sigmoid(x)
    return s * (1.0 + x * (1.0 - s))


def _shift_down(x, s):
    rows = lax.broadcasted_iota(jnp.int32, x.shape, 0)
    return jnp.where(rows >= s, pltpu.roll(x, s, 0), 0.0)


def _shift_up(x, s):
    n = x.shape[0]
    rows = lax.broadcasted_iota(jnp.int32, x.shape, 0)
    return jnp.where(rows < n - s, pltpu.roll(x, n - s, 0), 0.0)


MM_VMEM_BUDGET = 40 * 1024 * 1024
MM_STEP_BYTES = 1_300_000
MM_SUB = 512


def _mm_tiles(m, n, k, a_bytes, b_bytes, o_bytes):
    def cands(dim, sizes):
        out = [s for s in sizes if s <= dim and dim % s == 0]
        return out or [dim]

    best = None
    for tm in cands(m, (m, m // 2, 2048, 1024, 512, 256, 128)):
        if tm % LANE:
            continue
        for tn in cands(n, (n, n // 2, n // 4, 2048, 1024, 512, 256, 128)):
            if tn % (2 * LANE) and tn != n:
                continue
            for tk in cands(k, (k, k // 2, 2048, 1024, 512)):
                if tk % LANE:
                    continue
                nk = k // tk
                acc = tm * tn * 4 if (nk > 1 and o_bytes != 4) else 0
                temps = tm * min(tn, MM_SUB) * 4 + (tm * tk * 2 if a_bytes == 4 else 0) + (tk * tn * 2 if b_bytes == 4 else 0)
                vmem = 2 * (tm * tk * a_bytes + tk * tn * b_bytes + tm * tn * o_bytes) + acc + temps
                if vmem > MM_VMEM_BUDGET:
                    continue
                steps = (m // tm) * (n // tn) * nk
                cost = (m * k * a_bytes * (n // tn) + k * n * b_bytes * (m // tm) + m * n * o_bytes
                        + steps * MM_STEP_BYTES)
                if best is None or cost < best[0]:
                    best = (cost, tm, tn, tk)
    return best[1:]


def _mm(a, b, *, ta=False, tb=False, out_dtype=F32, name="mm"):
    m, k = (a.shape[1], a.shape[0]) if ta else a.shape
    n = b.shape[0] if tb else b.shape[1]
    o_bytes = jnp.dtype(out_dtype).itemsize
    tm, tn, tk = _mm_tiles(m, n, k, a.dtype.itemsize, b.dtype.itemsize, o_bytes)
    nk = k // tk
    sub = _pick(tn, (MM_SUB, 256))
    use_acc = nk > 1 and o_bytes != 4
    a_spec = pl.BlockSpec((tk, tm), lambda i, j, kk: (kk, i)) if ta else pl.BlockSpec((tm, tk), lambda i, j, kk: (i, kk))
    b_spec = pl.BlockSpec((tn, tk), lambda i, j, kk: (j, kk)) if tb else pl.BlockSpec((tk, tn), lambda i, j, kk: (kk, j))
    dims = (((1,), (1 if tb else 0,)), ((), ()))

    def body(a_ref, b_ref, o_ref, *scratch):
        kk = pl.program_id(2)
        acc_ref = scratch[0] if use_acc else o_ref
        if nk > 1:
            @pl.when(kk == 0)
            def _():
                acc_ref[...] = jnp.zeros_like(acc_ref)

        av = a_ref[...].astype(BF16)
        if ta:
            av = av.T
        for s in range(tn // sub):
            cols = slice(s * sub, (s + 1) * sub)
            bv = (b_ref[cols, :] if tb else b_ref[:, cols]).astype(BF16)
            part = lax.dot_general(av, bv, dims, preferred_element_type=F32)
            if nk == 1:
                o_ref[:, cols] = part.astype(out_dtype)
            else:
                acc_ref[:, cols] += part
        if use_acc:
            @pl.when(kk == nk - 1)
            def _():
                o_ref[...] = acc_ref[...].astype(out_dtype)

    return pl.pallas_call(
        body,
        name=name,
        grid=(m // tm, n // tn, nk),
        in_specs=[a_spec, b_spec],
        out_specs=pl.BlockSpec((tm, tn), lambda i, j, kk: (i, j)),
        out_shape=jax.ShapeDtypeStruct((m, n), out_dtype),
        scratch_shapes=[pltpu.VMEM((tm, tn), F32)] if use_acc else [],
        compiler_params=_params(("parallel", "parallel", "arbitrary")),
    )(a, b)


def _rms_fwd(x, w, *, out_dtype, name):
    t, d = x.shape
    tm = _pick(t, (512, 256, 128))

    def body(x_ref, w_ref, o_ref):
        xv = x_ref[...]
        rstd = lax.rsqrt(jnp.mean(xv * xv, axis=-1, keepdims=True) + EPS)
        o_ref[...] = (xv * rstd * w_ref[...]).astype(out_dtype)

    return pl.pallas_call(
        body,
        name=name,
        grid=(t // tm,),
        in_specs=[pl.BlockSpec((tm, d), lambda i: (i, 0)), pl.BlockSpec((1, d), lambda i: (0, 0))],
        out_specs=pl.BlockSpec((tm, d), lambda i: (i, 0)),
        out_shape=jax.ShapeDtypeStruct((t, d), out_dtype),
        compiler_params=_params(("parallel",)),
    )(x, w)


def _res_rms_fwd(x, f, w, *, name):
    t, d = x.shape
    tm = _pick(t, (512, 256, 128))

    def body(x_ref, f_ref, w_ref, o_ref):
        fv = f_ref[...]
        rstd = lax.rsqrt(jnp.mean(fv * fv, axis=-1, keepdims=True) + EPS)
        o_ref[...] = x_ref[...] + fv * rstd * w_ref[...]

    row = pl.BlockSpec((tm, d), lambda i: (i, 0))
    return pl.pallas_call(
        body,
        name=name,
        grid=(t // tm,),
        in_specs=[row, row, pl.BlockSpec((1, d), lambda i: (0, 0))],
        out_specs=row,
        out_shape=jax.ShapeDtypeStruct((t, d), F32),
        compiler_params=_params(("parallel",)),
    )(x, f, w)


def _rms_bwd(x, w, dy, resid, *, out_dtype=F32, name):
    t, d = x.shape
    tm = _pick(t, (512, 256, 128))
    has_res = resid is not None

    def body(*refs):
        if has_res:
            x_ref, w_ref, dy_ref, r_ref, dx_ref, dw_ref = refs
        else:
            x_ref, w_ref, dy_ref, dx_ref, dw_ref = refs
        xv = x_ref[...]
        dyv = dy_ref[...].astype(F32)
        rstd = lax.rsqrt(jnp.mean(xv * xv, axis=-1, keepdims=True) + EPS)
        xn = xv * rstd
        g = dyv * w_ref[...]
        dx = rstd * (g - xn * jnp.mean(g * xn, axis=-1, keepdims=True))
        if has_res:
            dx = dx + r_ref[...]
        dx_ref[...] = dx.astype(out_dtype)
        part = jnp.sum(dyv * xn, axis=0, keepdims=True)

        @pl.when(pl.program_id(0) == 0)
        def _():
            dw_ref[...] = part

        @pl.when(pl.program_id(0) > 0)
        def _():
            dw_ref[...] += part

    row = pl.BlockSpec((tm, d), lambda i: (i, 0))
    vec = pl.BlockSpec((1, d), lambda i: (0, 0))
    ins = [x, w, dy] + ([resid] if has_res else [])
    return pl.pallas_call(
        body,
        name=name,
        grid=(t // tm,),
        in_specs=[row, vec, row] + ([row] if has_res else []),
        out_specs=[row, vec],
        out_shape=[jax.ShapeDtypeStruct((t, d), out_dtype), jax.ShapeDtypeStruct((1, d), F32)],
        compiler_params=_params(("arbitrary",)),
    )(*ins)


def _res_rms_rms(x, f, w_post, w_pre, *, out_dtype=BF16, name):
    t, d = x.shape
    tm = _pick(t, (512, 256, 128))

    def body(x_ref, f_ref, wp_ref, wn_ref, x1_ref, n_ref):
        fv = f_ref[...]
        x1 = x_ref[...] + fv * lax.rsqrt(jnp.mean(fv * fv, axis=-1, keepdims=True) + EPS) * wp_ref[...]
        x1_ref[...] = x1
        n_ref[...] = (x1 * lax.rsqrt(jnp.mean(x1 * x1, axis=-1, keepdims=True) + EPS) * wn_ref[...]).astype(out_dtype)

    row = pl.BlockSpec((tm, d), lambda i: (i, 0))
    vec = pl.BlockSpec((1, d), lambda i: (0, 0))
    return pl.pallas_call(
        body,
        name=name,
        grid=(t // tm,),
        in_specs=[row, row, vec, vec],
        out_specs=[row, row],
        out_shape=[jax.ShapeDtypeStruct((t, d), F32), jax.ShapeDtypeStruct((t, d), out_dtype)],
        compiler_params=_params(("parallel",)),
    )(x, f, w_post, w_pre)


def _rms_bwd2(xa, wa, dya, resid, xb, wb, *, out_dtype, name):
    t, d = xa.shape
    tm = _pick(t, (512, 256, 128))

    def norm_bwd(xv, w, dyv):
        rstd = lax.rsqrt(jnp.mean(xv * xv, axis=-1, keepdims=True) + EPS)
        xn = xv * rstd
        g = dyv * w
        return rstd * (g - xn * jnp.mean(g * xn, axis=-1, keepdims=True)), jnp.sum(dyv * xn, axis=0, keepdims=True)

    def body(xa_ref, wa_ref, dya_ref, r_ref, xb_ref, wb_ref, da_ref, db_ref, dwa_ref, dwb_ref):
        da, dwa = norm_bwd(xa_ref[...], wa_ref[...], dya_ref[...].astype(F32))
        da = da + r_ref[...]
        da_ref[...] = da
        db, dwb = norm_bwd(xb_ref[...], wb_ref[...], da)
        db_ref[...] = db.astype(out_dtype)

        @pl.when(pl.program_id(0) == 0)
        def _():
            dwa_ref[...] = dwa
            dwb_ref[...] = dwb

        @pl.when(pl.program_id(0) > 0)
        def _():
            dwa_ref[...] += dwa
            dwb_ref[...] += dwb

    row = pl.BlockSpec((tm, d), lambda i: (i, 0))
    vec = pl.BlockSpec((1, d), lambda i: (0, 0))
    return pl.pallas_call(
        body,
        name=name,
        grid=(t // tm,),
        in_specs=[row, vec, row, row, row, vec],
        out_specs=[row, row, vec, vec],
        out_shape=[jax.ShapeDtypeStruct((t, d), F32), jax.ShapeDtypeStruct((t, d), out_dtype),
                   jax.ShapeDtypeStruct((1, d), F32), jax.ShapeDtypeStruct((1, d), F32)],
        compiler_params=_params(("arbitrary",)),
    )(xa, wa, dya, resid, xb, wb)


def _loss_head(y, target, *, name="loss_head"):
    t, d = y.shape
    tm = _pick(t, (512, 256, 128))

    def body(y_ref, t_ref, dy_ref, l_ref):
        err = y_ref[...] - t_ref[...]
        dy_ref[...] = err * (1.0 / d)
        part = jnp.sum(jnp.sum(err * err, axis=-1, keepdims=True), axis=0, keepdims=True) * (0.5 / d)
        part = jnp.broadcast_to(part, (1, LANE))

        @pl.when(pl.program_id(0) == 0)
        def _():
            l_ref[...] = part

        @pl.when(pl.program_id(0) > 0)
        def _():
            l_ref[...] += part

    row = pl.BlockSpec((tm, d), lambda i: (i, 0))
    dy, l = pl.pallas_call(
        body,
        name=name,
        grid=(t // tm,),
        in_specs=[row, row],
        out_specs=[row, pl.BlockSpec((1, LANE), lambda i: (0, 0))],
        out_shape=[jax.ShapeDtypeStruct((t, d), F32), jax.ShapeDtypeStruct((1, LANE), F32)],
        compiler_params=_params(("arbitrary",)),
    )(y, target)
    return l[0, 0], dy


def _conv_taps(h, w_ref, k_taps):
    out = h * w_ref[k_taps - 1:k_taps, :]
    for k in range(k_taps - 1):
        out = out + _shift_down(h, k_taps - 1 - k) * w_ref[k:k + 1, :]
    return out


def _conv_taps_bwd(h, dhc, w_ref, k_taps):
    dh = dhc * w_ref[k_taps - 1:k_taps, :]
    dws = []
    for k in range(k_taps - 1):
        up = _shift_up(dhc, k_taps - 1 - k)
        dh = dh + up * w_ref[k:k + 1, :]
        dws.append(jnp.sum(up * h, axis=0, keepdims=True))
    dws.append(jnp.sum(dhc * h, axis=0, keepdims=True))
    return dh, jnp.concatenate(dws, axis=0)


FFN_TC = 256


def _interleave(w, tc=FFN_TC):
    f = w.shape[-1] // 2
    tiles = []
    for j in range(f // tc):
        tiles += [w[..., j * tc:(j + 1) * tc], w[..., f + j * tc:f + (j + 1) * tc]]
    return jnp.concatenate(tiles, axis=-1)


def _deinterleave(w, tc=FFN_TC):
    n_tiles = w.shape[-1] // tc
    return jnp.concatenate([w[..., j * tc:(j + 1) * tc] for j in list(range(0, n_tiles, 2)) + list(range(1, n_tiles, 2))],
                           axis=-1)


FFN_ROWS = 512
HALO = 8


def _ffn_up_act(n, w_up, conv_w, conv_b, n_seq, *, name):
    t, d = n.shape
    f2 = w_up.shape[1]
    seq = t // n_seq
    tc = FFN_TC
    nj = f2 // (2 * tc)
    k_taps = conv_w.shape[0]
    rows = min(FFN_ROWS, seq)

    def body(n_ref, wu_ref, w_ref, b_ref, h_ref, hc_ref, o_ref, h_scr):
        h_scr[0:HALO, :] = jnp.zeros((HALO, 2 * tc), F32)
        wu = wu_ref[...]
        for r in range(seq // rows):
            chunk = slice(r * rows, (r + 1) * rows)
            h = jnp.dot(n_ref[chunk, :], wu, preferred_element_type=F32)
            h_scr[HALO + r * rows:HALO + (r + 1) * rows, :] = h
            h_ref[chunk, :] = h.astype(BF16)
            ext = h_scr[r * rows:HALO + (r + 1) * rows, :]
            hc = ext * w_ref[k_taps - 1:k_taps, :]
            for k in range(k_taps - 1):
                hc = hc + pltpu.roll(ext, k_taps - 1 - k, 0) * w_ref[k:k + 1, :]
            hc = hc[HALO:, :] + b_ref[...]
            hc_ref[chunk, :] = hc.astype(BF16)
            o_ref[chunk, :] = (_silu(hc[:, :tc]) * hc[:, tc:]).astype(BF16)

    wide = pl.BlockSpec((seq, 2 * tc), lambda b, j: (b, j))
    return pl.pallas_call(
        body,
        name=name,
        grid=(n_seq, nj),
        in_specs=[
            pl.BlockSpec((seq, d), lambda b, j: (b, 0)),
            pl.BlockSpec((d, 2 * tc), lambda b, j: (0, j)),
            pl.BlockSpec((k_taps, 2 * tc), lambda b, j: (0, j)),
            pl.BlockSpec((1, 2 * tc), lambda b, j: (0, j)),
        ],
        out_specs=[wide, wide, pl.BlockSpec((seq, tc), lambda b, j: (b, j))],
        out_shape=[jax.ShapeDtypeStruct((t, f2), BF16), jax.ShapeDtypeStruct((t, f2), BF16),
                   jax.ShapeDtypeStruct((t, f2 // 2), BF16)],
        scratch_shapes=[pltpu.VMEM((HALO + seq, 2 * tc), F32)],
        compiler_params=_params(("parallel", "arbitrary")),
    )(n, w_up, conv_w, conv_b)


def _ffn_down_bx_act_bwd(df, w_down, h, hc, conv_w, n_seq, *, name):
    t, d = df.shape
    f2 = h.shape[1]
    seq = t // n_seq
    tc = FFN_TC
    nj = f2 // (2 * tc)
    k_taps = conv_w.shape[0]

    def body(df_ref, wd_ref, h_ref, hc_ref, w_ref, dh_ref, dw_ref, db_ref):
        dav = lax.dot_general(df_ref[...], wd_ref[...], (((1,), (1,)), ((), ())), preferred_element_type=F32)
        hcv = hc_ref[...].astype(F32)
        gate, val = hcv[:, :tc], hcv[:, tc:]
        dhc = jnp.concatenate([dav * val * _dsilu(gate), dav * _silu(gate)], axis=1)
        dh, dw = _conv_taps_bwd(h_ref[...].astype(F32), dhc, w_ref, k_taps)
        dh_ref[...] = dh.astype(BF16)
        dw_ref[0] = dw
        db_ref[0] = jnp.sum(dhc, axis=0, keepdims=True)

    wide = pl.BlockSpec((seq, 2 * tc), lambda b, j: (b, j))
    dh, dw, db = pl.pallas_call(
        body,
        name=name,
        grid=(n_seq, nj),
        in_specs=[
            pl.BlockSpec((seq, d), lambda b, j: (b, 0)),
            pl.BlockSpec((tc, d), lambda b, j: (j, 0)),
            wide, wide,
            pl.BlockSpec((k_taps, 2 * tc), lambda b, j: (0, j)),
        ],
        out_specs=[
            wide,
            pl.BlockSpec((1, k_taps, 2 * tc), lambda b, j: (b, 0, j)),
            pl.BlockSpec((1, 1, 2 * tc), lambda b, j: (b, 0, j)),
        ],
        out_shape=[
            jax.ShapeDtypeStruct((t, f2), BF16),
            jax.ShapeDtypeStruct((n_seq, k_taps, f2), F32),
            jax.ShapeDtypeStruct((n_seq, 1, f2), F32),
        ],
        compiler_params=_params(("parallel", "arbitrary")),
    )(df, w_down, h, hc, conv_w)
    return dh, jnp.sum(dw, axis=0), jnp.sum(db, axis=0)


def _window_mixed(u, window):
    s = u
    step = 1
    while step < window:
        s = s + _shift_down(s, step)
        step *= 2
    rows = lax.broadcasted_iota(jnp.int32, u.shape, 0)
    inv_cnt = 1.0 / jnp.minimum(rows + 1, window).astype(F32)
    return s * inv_cnt - u, inv_cnt


def _window_mixed_bwd(dmixed, inv_cnt, window):
    r = dmixed * inv_cnt
    s = r
    step = 1
    while step < window:
        s = s + _shift_up(s, step)
        step *= 2
    return s - dmixed


def _pool_fwd(u, w, scale, n_seq, *, name):
    t, d = u.shape
    seq = t // n_seq
    n_g, dg, _ = w.shape

    def body(u_ref, w_ref, s_ref, o_ref):
        for k, window in enumerate(POOL_WINDOWS):
            @pl.when(pl.program_id(1) == k)
            def _(window=window):
                mixed, _ = _window_mixed(u_ref[...], window)
                pre = jnp.dot(mixed.astype(BF16), w_ref[0].astype(BF16), preferred_element_type=F32)
                o_ref[...] = pre * s_ref[...]

    return pl.pallas_call(
        body,
        name=name,
        grid=(n_seq, n_g),
        in_specs=[
            pl.BlockSpec((seq, dg), lambda b, g: (b, g)),
            pl.BlockSpec((1, dg, dg), lambda b, g: (g, 0, 0)),
            pl.BlockSpec((1, dg), lambda b, g: (0, g)),
        ],
        out_specs=pl.BlockSpec((seq, dg), lambda b, g: (b, g)),
        out_shape=jax.ShapeDtypeStruct((t, d), F32),
        compiler_params=_params(("parallel", "parallel")),
    )(u, w, scale)


def _pool_bwd(u, w, scale, dout, n_seq, *, name):
    t, d = u.shape
    seq = t // n_seq
    n_g, dg, _ = w.shape

    def body(u_ref, w_ref, s_ref, do_ref, du_ref, dw_ref, ds_ref):
        group = pl.program_id(0)
        first = pl.program_id(1) == 0
        for k, window in enumerate(POOL_WINDOWS):
            @pl.when(group == k)
            def _(window=window):
                mixed, inv_cnt = _window_mixed(u_ref[...], window)
                mixed_b = mixed.astype(BF16)
                w_b = w_ref[0].astype(BF16)
                dov = do_ref[...]
                pre = jnp.dot(mixed_b, w_b, preferred_element_type=F32)
                dsc = jnp.sum(dov * pre, axis=0, keepdims=True)
                dpre = (dov * s_ref[...]).astype(BF16)
                dw = lax.dot_general(mixed_b, dpre, (((0,), (0,)), ((), ())), preferred_element_type=F32)
                dmixed = lax.dot_general(dpre, w_b, (((1,), (1,)), ((), ())), preferred_element_type=F32)
                du_ref[...] = _window_mixed_bwd(dmixed, inv_cnt, window)

                @pl.when(first)
                def _():
                    dw_ref[0] = dw
                    ds_ref[...] = dsc

                @pl.when(jnp.logical_not(first))
                def _():
                    dw_ref[0] += dw
                    ds_ref[...] += dsc

    return pl.pallas_call(
        body,
        name=name,
        grid=(n_g, n_seq),
        in_specs=[
            pl.BlockSpec((seq, dg), lambda g, b: (b, g)),
            pl.BlockSpec((1, dg, dg), lambda g, b: (g, 0, 0)),
            pl.BlockSpec((1, dg), lambda g, b: (0, g)),
            pl.BlockSpec((seq, dg), lambda g, b: (b, g)),
        ],
        out_specs=[
            pl.BlockSpec((seq, dg), lambda g, b: (b, g)),
            pl.BlockSpec((1, dg, dg), lambda g, b: (g, 0, 0)),
            pl.BlockSpec((1, dg), lambda g, b: (0, g)),
        ],
        out_shape=[
            jax.ShapeDtypeStruct((t, d), F32),
            jax.ShapeDtypeStruct((n_g, dg, dg), F32),
            jax.ShapeDtypeStruct((1, d), F32),
        ],
        compiler_params=_params(("parallel", "arbitrary")),
    )(u, w, scale, dout)


def _adamw(w, g, m, v, *, name):
    shape = w.shape
    c = shape[-1]
    r = w.size // c
    tm = _pick(r, (512, 256, 128, 64, 32, 16, 8))

    def body(w_ref, g_ref, m_ref, v_ref, d_ref, nm_ref, nv_ref):
        gv = g_ref[...]
        nm = ADAM_B1 * m_ref[...] + (1.0 - ADAM_B1) * gv
        nv = ADAM_B2 * v_ref[...] + (1.0 - ADAM_B2) * (gv * gv)
        m_hat = nm / (1.0 - ADAM_B1 ** ADAM_STEP)
        v_hat = nv / (1.0 - ADAM_B2 ** ADAM_STEP)
        d_ref[...] = -ADAM_LR * (m_hat / (jnp.sqrt(v_hat) + ADAM_EPS) + ADAM_WD * w_ref[...])
        nm_ref[...] = nm
        nv_ref[...] = nv

    blk = pl.BlockSpec((tm, c), lambda i: (i, 0))
    out = jax.ShapeDtypeStruct((r, c), F32)
    res = pl.pallas_call(
        body,
        name=name,
        grid=(r // tm,),
        in_specs=[blk] * 4,
        out_specs=[blk] * 3,
        out_shape=[out] * 3,
        compiler_params=_params(("parallel",)),
    )(w.reshape(r, c), g.reshape(r, c), m.reshape(r, c), v.reshape(r, c))
    return tuple(a.reshape(shape) for a in res)


CONV_TC = 256


def _ssd_conv_fwd(proj, col0, n_cols, conv_w, conv_b, n_seq, *, name):
    t = proj.shape[0]
    seq = t // n_seq
    tc = CONV_TC
    off = col0 // tc
    k_taps = conv_w.shape[0]

    def body(h_ref, w_ref, b_ref, o_ref, pre_ref):
        pre = _conv_taps(h_ref[...], w_ref, k_taps) + b_ref[...]
        pre_ref[...] = pre.astype(BF16)
        o_ref[...] = _silu(pre)

    return pl.pallas_call(
        body,
        name=name,
        grid=(n_seq, n_cols // tc),
        in_specs=[
            pl.BlockSpec((seq, tc), lambda b, j: (b, j + off)),
            pl.BlockSpec((k_taps, tc), lambda b, j: (0, j)),
            pl.BlockSpec((1, tc), lambda b, j: (0, j)),
        ],
        out_specs=[pl.BlockSpec((seq, tc), lambda b, j: (b, j))] * 2,
        out_shape=[jax.ShapeDtypeStruct((t, n_cols), F32), jax.ShapeDtypeStruct((t, n_cols), BF16)],
        compiler_params=_params(("parallel", "parallel")),
    )(proj, conv_w, conv_b)


def _ssd_conv_bwd(proj, col0, conv_w, pre, dparts, dproj, n_seq, *, name):
    t = proj.shape[0]
    seq = t // n_seq
    tc = CONV_TC
    off = col0 // tc
    k_taps = conv_w.shape[0]
    widths = [d.shape[1] // tc for d in dparts]
    starts = [sum(widths[:i]) for i in range(len(widths))]
    n_blocks = sum(widths)
    n_parts = len(dparts)

    def body(h_ref, w_ref, pre_ref, *rest):
        part_refs = rest[:n_parts]
        dh_ref, dw_ref, db_ref = rest[n_parts + 1:]
        j = pl.program_id(0)
        da = part_refs[-1][...]
        for i in reversed(range(n_parts - 1)):
            da = jnp.where(j < starts[i + 1], part_refs[i][...], da)
        dhc = da * _dsilu(pre_ref[...].astype(F32))
        dh, dw = _conv_taps_bwd(h_ref[...], dhc, w_ref, k_taps)
        dh_ref[...] = dh.astype(BF16)
        db = jnp.sum(dhc, axis=0, keepdims=True)

        @pl.when(pl.program_id(1) == 0)
        def _():
            dw_ref[...] = dw
            db_ref[...] = db

        @pl.when(pl.program_id(1) > 0)
        def _():
            dw_ref[...] += dw
            db_ref[...] += db

    def part_spec(start, width):
        return pl.BlockSpec((seq, tc), lambda j, b: (b, jnp.clip(j - start, 0, width - 1)))

    n_cols = n_blocks * tc
    return pl.pallas_call(
        body,
        name=name,
        grid=(n_blocks, n_seq),
        in_specs=[
            pl.BlockSpec((seq, tc), lambda j, b: (b, j + off)),
            pl.BlockSpec((k_taps, tc), lambda j, b: (0, j)),
            pl.BlockSpec((seq, tc), lambda j, b: (b, j)),
        ] + [part_spec(st, wd) for st, wd in zip(starts, widths)] + [ANY],
        out_specs=[
            pl.BlockSpec((seq, tc), lambda j, b: (b, j + off)),
            pl.BlockSpec((k_taps, tc), lambda j, b: (0, j)),
            pl.BlockSpec((1, tc), lambda j, b: (0, j)),
        ],
        out_shape=[
            jax.ShapeDtypeStruct(dproj.shape, BF16),
            jax.ShapeDtypeStruct((k_taps, n_cols), F32),
            jax.ShapeDtypeStruct((1, n_cols), F32),
        ],
        input_output_aliases={3 + n_parts: 0},
        compiler_params=_params(("parallel", "arbitrary")),
    )(proj, conv_w, pre, *dparts, dproj)


def _fill_cols(buf, src, col0, *, name):
    t, c = src.shape
    tm = _pick(t, (1024, 512, 256, 128))

    def body(s_ref, b_ref, o_ref):
        o_ref[...] = s_ref[...].astype(o_ref.dtype)

    return pl.pallas_call(
        body,
        name=name,
        grid=(t // tm,),
        in_specs=[pl.BlockSpec((tm, c), lambda i: (i, 0)), ANY],
        out_specs=pl.BlockSpec((tm, c), lambda i: (i, col0 // c)),
        out_shape=jax.ShapeDtypeStruct(buf.shape, buf.dtype),
        input_output_aliases={1: 0},
        compiler_params=_params(("parallel",)),
    )(src, buf)


def _softplus(x):
    return jnp.maximum(x, 0.0) + jnp.log(1.0 + jnp.exp(-jnp.abs(x)))


def _chunk_decay(dtraw, bias, alog):
    q = dtraw.shape[0]
    dt = _softplus(dtraw + bias)
    a = -jnp.exp(alog)
    rows = lax.broadcasted_iota(jnp.int32, (q, q), 0)
    cols = lax.broadcasted_iota(jnp.int32, (q, q), 1)
    lower = rows >= cols
    acum = jnp.dot(lower.astype(F32), dt * a, precision=lax.Precision.HIGHEST, preferred_element_type=F32)
    return dt, a, acum, acum.T, lower


def _dot_exact(v, sel, terms=3):
    hi = v.astype(BF16)
    r1 = v - hi.astype(F32)
    mid = r1.astype(BF16)
    out = jnp.dot(hi, sel, preferred_element_type=F32) + jnp.dot(mid, sel, preferred_element_type=F32)
    if terms == 3:
        lo = (r1 - mid.astype(F32)).astype(BF16)
        out = out + jnp.dot(lo, sel, preferred_element_type=F32)
    return out


def _head_selectors(gw, p):
    sum_heads = (lax.broadcasted_iota(jnp.int32, (gw, LANE), 0) // p == lax.broadcasted_iota(jnp.int32, (gw, LANE), 1))
    spread = (lax.broadcasted_iota(jnp.int32, (LANE, gw), 0) == lax.broadcasted_iota(jnp.int32, (LANE, gw), 1) // p)
    return sum_heads.astype(BF16), spread.astype(BF16)


def _row_spread(v, spread):
    return _dot_exact(jnp.broadcast_to(v, (8, v.shape[1])), spread)[0:1, :]


def _head_pad(v, r_heads):
    lead = v.shape[:-1]
    vg = v.reshape(lead + (N_SSD_GROUPS, r_heads))
    vg = jnp.pad(vg, [(0, 0)] * len(lead) + [(0, 0), (0, LANE - r_heads)])
    out = vg.reshape(lead + (N_SSD_GROUPS * LANE,))
    return out[None] if out.ndim == 1 else out


def _head_unpad(v, r_heads):
    lead = v.shape[:-1]
    out = v.reshape(lead + (N_SSD_GROUPS, LANE))[..., :r_heads].reshape(lead + (N_SSD_GROUPS * r_heads,))
    return out[0] if (len(lead) == 1 and lead[0] == 1) else out


def _ssd_w_in_layout(w_in, d_inner, d_xbc, r_heads):
    main = w_in[:, :d_inner + d_xbc]
    return jnp.concatenate([main, _head_pad(w_in[:, d_inner + d_xbc:], r_heads)], axis=1)


def _ssd_w_in_unlayout(w, d_inner, d_xbc, r_heads):
    main = w[:, :d_inner + d_xbc]
    return jnp.concatenate([main, _head_unpad(w[:, d_inner + d_xbc:], r_heads)], axis=1)


SSD_CHUNKS = 8
SSD_CHUNKS_FWD = 8


def _ssd_dims(proj, xbc):
    d_xbc = xbc.shape[1]
    d_inner = d_xbc - 2 * N_SSD_GROUPS * D_STATE
    gw = d_inner // N_SSD_GROUPS
    return d_inner, d_xbc, gw, gw // HEAD_DIM


def _ssd_fwd(proj, xbc, bias_p, alog_p, dskip_p, norm_w, n_seq, *, name):
    t = proj.shape[0]
    d_inner, d_xbc, gw, r_heads = _ssd_dims(proj, xbc)
    q, n, n_g, p = CHUNK, D_STATE, N_SSD_GROUPS, HEAD_DIM
    seq = t // n_seq
    nc = seq // q
    cps = SSD_CHUNKS_FWD if nc % SSD_CHUNKS_FWD == 0 else 1
    dt_blk0 = (d_inner + d_xbc) // LANE

    def body(x_ref, b_ref, c_ref, z_ref, dtr_ref, bias_ref, alog_ref, dsk_ref, nw_ref, yn_ref, y_ref, hs_ref, h_scr):
        @pl.when(pl.program_id(2) == 0)
        def _():
            h_scr[...] = jnp.zeros_like(h_scr)

        for cc in range(cps):
            rows = pl.ds(cc * q, q)
            chunk(x_ref.at[rows, :], b_ref.at[rows, :], c_ref.at[rows, :], z_ref.at[rows, :], dtr_ref.at[rows, :],
                  bias_ref, alog_ref, dsk_ref, nw_ref, yn_ref.at[rows, :], y_ref.at[rows, :],
                  hs_ref.at[pl.ds(cc * n, n), :], h_scr)

    def chunk(x_ref, b_ref, c_ref, z_ref, dtr_ref, bias_ref, alog_ref, dsk_ref, nw_ref, yn_ref, y_ref, hs_ref, h_scr):
        dt, a, acum, acum_t, lower = _chunk_decay(dtr_ref[...], bias_ref[...], alog_ref[...])
        x = x_ref[...]
        bb = b_ref[...].astype(BF16)
        cb = c_ref[...].astype(BF16)
        g_mat = lax.dot_general(cb, bb, (((1,), (1,)), ((), ())), preferred_element_type=F32)
        h_prev = h_scr[...]
        hs_ref[...] = h_prev
        c_h = jnp.dot(cb, h_prev.astype(BF16), preferred_element_type=F32)
        _, spread = _head_selectors(gw, p)
        acum_s = _dot_exact(acum, spread)
        a_last_s = acum_s[q - 1:q, :]
        xdt = x * _dot_exact(dt, spread, 2)
        xdt_b = xdt.astype(BF16)
        ys = []
        for h in range(r_heads):
            decay = jnp.exp(jnp.where(lower, acum[:, h:h + 1] - acum_t[h:h + 1, :], -jnp.inf))
            ys.append(jnp.dot((g_mat * decay).astype(BF16), xdt_b[:, h * p:(h + 1) * p], preferred_element_type=F32))
        y = jnp.concatenate(ys, axis=1) + jnp.exp(acum_s) * c_h + _row_spread(dsk_ref[...], spread) * x
        xd = xdt * jnp.exp(a_last_s - acum_s)
        states = lax.dot_general(bb, xd.astype(BF16), (((0,), (0,)), ((), ())), preferred_element_type=F32)
        h_scr[...] = h_prev * jnp.exp(a_last_s) + states
        y_ref[...] = y
        gated = y * _silu(z_ref[...])
        rstd = lax.rsqrt(jnp.mean(gated * gated, axis=-1, keepdims=True) + EPS)
        yn_ref[...] = (gated * rstd * nw_ref[...]).astype(BF16)

    row = lambda b, g, c: b * (nc // cps) + c
    vec = pl.BlockSpec((1, LANE), lambda b, g, c: (0, g))
    return pl.pallas_call(
        body,
        name=name,
        grid=(n_seq, n_g, nc // cps),
        in_specs=[
            pl.BlockSpec((q * cps, gw), lambda b, g, c: (row(b, g, c), g)),
            pl.BlockSpec((q * cps, n), lambda b, g, c: (row(b, g, c), d_inner // n + g)),
            pl.BlockSpec((q * cps, n), lambda b, g, c: (row(b, g, c), d_inner // n + n_g + g)),
            pl.BlockSpec((q * cps, gw), lambda b, g, c: (row(b, g, c), g)),
            pl.BlockSpec((q * cps, LANE), lambda b, g, c: (row(b, g, c), dt_blk0 + g)),
            vec, vec, vec,
            pl.BlockSpec((1, gw), lambda b, g, c: (0, g)),
        ],
        out_specs=[
            pl.BlockSpec((q * cps, gw), lambda b, g, c: (row(b, g, c), g)),
            pl.BlockSpec((q * cps, gw), lambda b, g, c: (row(b, g, c), g)),
            pl.BlockSpec((n * cps, gw), lambda b, g, c: (row(b, g, c), g)),
        ],
        out_shape=[
            jax.ShapeDtypeStruct((t, d_inner), BF16),
            jax.ShapeDtypeStruct((t, d_inner), F32),
            jax.ShapeDtypeStruct((n_seq * nc * n, d_inner), F32),
        ],
        scratch_shapes=[pltpu.VMEM((n, gw), F32)],
        compiler_params=_params(("parallel", "parallel", "arbitrary")),
    )(xbc, xbc, xbc, proj, proj, bias_p, alog_p, dskip_p, norm_w)


def _ssd_bwd(proj, xbc, hs, y, dyn, bias_p, alog_p, dskip_p, norm_w, n_seq, *, name):
    t = proj.shape[0]
    d_inner, d_xbc, gw, r_heads = _ssd_dims(proj, xbc)
    q, n, n_g, p = CHUNK, D_STATE, N_SSD_GROUPS, HEAD_DIM
    seq = t // n_seq
    nc = seq // q
    cps = SSD_CHUNKS if nc % SSD_CHUNKS == 0 else 1
    dt_blk0 = (d_inner + d_xbc) // LANE

    def body(x_ref, b_ref, c_ref, z_ref, dtr_ref, bias_ref, alog_ref, dsk_ref, nw_ref, hs_ref, y_ref, dyn_ref,
             dx_ref, db_ref, dc_ref, dz_ref, ddtr_ref, dnw_ref, dbias_ref, dalog_ref, ddsk_ref, dh_scr):
        @pl.when(pl.program_id(2) == 0)
        def _():
            dh_scr[...] = jnp.zeros_like(dh_scr)

        first_step = jnp.logical_and(pl.program_id(1) == 0, pl.program_id(2) == 0)
        for cc in reversed(range(cps)):
            rows = pl.ds(cc * q, q)
            chunk(jnp.logical_and(first_step, cc == cps - 1), x_ref.at[rows, :], b_ref.at[rows, :], c_ref.at[rows, :],
                  z_ref.at[rows, :], dtr_ref.at[rows, :], bias_ref, alog_ref, dsk_ref, nw_ref,
                  hs_ref.at[pl.ds(cc * n, n), :], y_ref.at[rows, :], dyn_ref.at[rows, :], dx_ref.at[rows, :],
                  db_ref.at[rows, :], dc_ref.at[rows, :], dz_ref.at[rows, :], ddtr_ref.at[rows, :],
                  dnw_ref, dbias_ref, dalog_ref, ddsk_ref, dh_scr)

    def chunk(first, x_ref, b_ref, c_ref, z_ref, dtr_ref, bias_ref, alog_ref, dsk_ref, nw_ref, hs_ref, y_ref, dyn_ref,
              dx_ref, db_ref, dc_ref, dz_ref, ddtr_ref, dnw_ref, dbias_ref, dalog_ref, ddsk_ref, dh_scr):
        dtraw = dtr_ref[...]
        dt, a, acum, acum_t, lower = _chunk_decay(dtraw, bias_ref[...], alog_ref[...])
        x = x_ref[...]
        bb = b_ref[...].astype(BF16)
        cb = c_ref[...].astype(BF16)
        g_mat = lax.dot_general(cb, bb, (((1,), (1,)), ((), ())), preferred_element_type=F32)

        yv = y_ref[...]
        z = z_ref[...]
        sz = _silu(z)
        gated = yv * sz
        rstd = lax.rsqrt(jnp.mean(gated * gated, axis=-1, keepdims=True) + EPS)
        gn = gated * rstd
        dynv = dyn_ref[...]
        gwt = dynv * nw_ref[...]
        dgated = rstd * (gwt - gn * jnp.mean(gwt * gn, axis=-1, keepdims=True))
        dnw = jnp.sum(dynv * gn, axis=0, keepdims=True)
        dy = dgated * sz
        dz_ref[...] = (dgated * yv * _dsilu(z)).astype(BF16)

        h_prev = hs_ref[...]
        h_prev_b = h_prev.astype(BF16)
        ds = dh_scr[...]
        ds_b = ds.astype(BF16)
        sum_heads, spread = _head_selectors(gw, p)
        acum_s = _dot_exact(acum, spread)
        a_last_s = acum_s[q - 1:q, :]
        dt_s = _dot_exact(dt, spread, 2)
        dsk_s = _row_spread(dsk_ref[...], spread)
        dte_s = jnp.exp(a_last_s - acum_s)
        cd_s = jnp.exp(a_last_s)
        xdt = x * dt_s
        xdt_b = xdt.astype(BF16)
        dy_b = dy.astype(BF16)
        gt_mat = lax.dot_general(bb, cb, (((1,), (1,)), ((), ())), preferred_element_type=F32)
        upper = lax.broadcasted_iota(jnp.int32, (q, q), 0) <= lax.broadcasted_iota(jnp.int32, (q, q), 1)
        dg = jnp.zeros((q, q), F32)
        dxdts, w_diffs = [], []
        for h in range(r_heads):
            hsl = slice(h * p, (h + 1) * p)
            diff = acum[:, h:h + 1] - acum_t[h:h + 1, :]
            decay = jnp.exp(jnp.where(lower, diff, -jnp.inf))
            decay_t = jnp.exp(jnp.where(upper, -diff, -jnp.inf))
            mt_mat = gt_mat * decay_t
            dm = lax.dot_general(dy_b[:, hsl], xdt_b[:, hsl], (((1,), (1,)), ((), ())), preferred_element_type=F32)
            dm_t = lax.dot_general(xdt_b[:, hsl], dy_b[:, hsl], (((1,), (1,)), ((), ())), preferred_element_type=F32)
            dg = dg + dm * decay
            dxdts.append(jnp.dot(mt_mat.astype(BF16), dy_b[:, hsl], preferred_element_type=F32))
            w_diffs.append(dm * (g_mat * decay) - dm_t * mt_mat)
        sel_q = (lax.broadcasted_iota(jnp.int32, (r_heads * q, LANE), 0) // q
                 == lax.broadcasted_iota(jnp.int32, (r_heads * q, LANE), 1)).astype(BF16)
        dacum_diag = _dot_exact(jnp.concatenate(w_diffs, axis=1), sel_q, 2)
        c_h = jnp.dot(cb, h_prev_b, preferred_element_type=F32)
        dxd = jnp.dot(bb, ds_b, preferred_element_type=F32)
        dxdt = jnp.concatenate(dxdts, axis=1) + dxd * dte_s
        dye = dy * jnp.exp(acum_s)
        dye_b = dye.astype(BF16)
        xd = xdt * dte_s
        xd_b = xd.astype(BF16)
        dg_b = dg.astype(BF16)
        dx_ref[...] = dxdt * dt_s + dsk_s * dy
        dc_ref[...] = (jnp.dot(dg_b, bb, preferred_element_type=F32)
                       + lax.dot_general(dye_b, h_prev_b, (((1,), (1,)), ((), ())), preferred_element_type=F32))
        db_ref[...] = (lax.dot_general(dg_b, cb, (((0,), (0,)), ((), ())), preferred_element_type=F32)
                       + lax.dot_general(xd_b, ds_b, (((1,), (1,)), ((), ())), preferred_element_type=F32))
        dh_scr[...] = ds * cd_s + lax.dot_general(cb, dye_b, (((0,), (0,)), ((), ())), preferred_element_type=F32)
        ddt_cols = _dot_exact(x * dxdt, sum_heads, 2)
        dacum_y = _dot_exact(dye * c_h - dxd * xd, sum_heads, 2)
        col_sums = jnp.concatenate([
            jnp.sum(dxd * xd, axis=0, keepdims=True) + jnp.sum(ds * h_prev, axis=0, keepdims=True) * cd_s,
            jnp.sum(dy * x, axis=0, keepdims=True),
            jnp.zeros((6, gw), F32)], axis=0)
        col_sums = _dot_exact(col_sums, sum_heads, 2)
        ddsk = col_sums[1:2, :]
        rows_q = lax.broadcasted_iota(jnp.int32, (q, LANE), 0)
        dacum = dacum_diag + dacum_y + jnp.where(rows_q == q - 1, col_sums[0:1, :], 0.0)
        dadt = jnp.dot(upper.astype(F32), dacum, precision=lax.Precision.HIGHEST, preferred_element_type=F32)
        ddt = dadt * a + ddt_cols
        ddtr = ddt * _sigmoid(dtraw + bias_ref[...])
        ddtr_ref[...] = ddtr
        dbias = jnp.sum(ddtr, axis=0, keepdims=True)
        dalog = jnp.sum(dadt * dt, axis=0, keepdims=True) * a

        @pl.when(first)
        def _():
            dnw_ref[...] = dnw
            dbias_ref[...] = dbias
            dalog_ref[...] = dalog
            ddsk_ref[...] = ddsk

        @pl.when(jnp.logical_not(first))
        def _():
            dnw_ref[...] += dnw
            dbias_ref[...] += dbias
            dalog_ref[...] += dalog
            ddsk_ref[...] += ddsk

    row = lambda g, b, c: b * (nc // cps) + (nc // cps - 1 - c)
    vec = pl.BlockSpec((1, LANE), lambda g, b, c: (0, g))
    wide = pl.BlockSpec((q * cps, gw), lambda g, b, c: (row(g, b, c), g))
    narrow = pl.BlockSpec((q * cps, n), lambda g, b, c: (row(g, b, c), g))
    return pl.pallas_call(
        body,
        name=name,
        grid=(n_g, n_seq, nc // cps),
        in_specs=[
            wide,
            pl.BlockSpec((q * cps, n), lambda g, b, c: (row(g, b, c), d_inner // n + g)),
            pl.BlockSpec((q * cps, n), lambda g, b, c: (row(g, b, c), d_inner // n + n_g + g)),
            wide,
            pl.BlockSpec((q * cps, LANE), lambda g, b, c: (row(g, b, c), dt_blk0 + g)),
            vec, vec, vec,
            pl.BlockSpec((1, gw), lambda g, b, c: (0, g)),
            pl.BlockSpec((n * cps, gw), lambda g, b, c: (row(g, b, c), g)),
            wide, wide,
        ],
        out_specs=[
            wide, narrow, narrow, wide, narrow,
            pl.BlockSpec((1, gw), lambda g, b, c: (0, g)),
            vec, vec, vec,
        ],
        out_shape=[
            jax.ShapeDtypeStruct((t, d_inner), F32),
            jax.ShapeDtypeStruct((t, n_g * n), F32),
            jax.ShapeDtypeStruct((t, n_g * n), F32),
            jax.ShapeDtypeStruct(proj.shape, BF16),
            jax.ShapeDtypeStruct((t, n_g * LANE), F32),
            jax.ShapeDtypeStruct((1, d_inner), F32),
            jax.ShapeDtypeStruct((1, n_g * LANE), F32),
            jax.ShapeDtypeStruct((1, n_g * LANE), F32),
            jax.ShapeDtypeStruct((1, n_g * LANE), F32),
        ],
        scratch_shapes=[pltpu.VMEM((n, gw), F32)],
        compiler_params=_params(("parallel", "arbitrary", "arbitrary")),
    )(xbc, xbc, xbc, proj, proj, bias_p, alog_p, dskip_p, norm_w, hs, y, dyn)


MESH_IDS = pl.DeviceIdType.MESH


def _my_index():
    return 4 * lax.axis_index("x") + 2 * lax.axis_index("y") + lax.axis_index("c")


def _all_gather(shard, *, name):
    def body(x_ref, out_ref, send_sems, recv_sems, local_sem):
        x, y, c = lax.axis_index("x"), lax.axis_index("y"), lax.axis_index("c")
        me, sibling = (x, y, c), (x, y, 1 - c)
        chips = [(1 - x, y), (x, 1 - y), (1 - x, 1 - y)]

        def blk(px, py, pc):
            return out_ref.at[4 * px + 2 * py + pc]

        def copy(k, block, to, src=None):
            return pltpu.make_async_remote_copy(
                src_ref=blk(*block) if src is None else src, dst_ref=blk(*block),
                send_sem=send_sems.at[k], recv_sem=recv_sems.at[k], device_id=to, device_id_type=MESH_IDS)

        mine = pltpu.make_async_copy(x_ref, blk(*me), local_sem)
        mine.start()
        first = [copy(0, me, sibling, src=x_ref)]
        first += [copy(1 + j, me, (*chip, c), src=x_ref) for j, chip in enumerate(chips)]
        for cp in first:
            cp.start()
        passed = [copy(4 + j, (*chip, c), sibling) for j, chip in enumerate(chips)]
        for j, chip in enumerate(chips):
            copy(1 + j, (*chip, c), me).wait_recv()
            passed[j].start()
        copy(0, sibling, me).wait_recv()
        for j, chip in enumerate(chips):
            copy(4 + j, (*chip, 1 - c), me).wait_recv()
        for cp in first + passed:
            cp.wait_send()
        mine.wait()

    return pl.pallas_call(
        body,
        name=name,
        in_specs=[ANY],
        out_specs=ANY,
        out_shape=jax.ShapeDtypeStruct((N_DEV,) + shard.shape, shard.dtype),
        scratch_shapes=[pltpu.SemaphoreType.DMA((7,)), pltpu.SemaphoreType.DMA((7,)), pltpu.SemaphoreType.DMA],
    )(shard)


HBM_SPEC = pl.BlockSpec(memory_space=pltpu.HBM)
SEM_SPEC = pl.BlockSpec(memory_space=pltpu.SEMAPHORE)
SPLIT_COPY_PARAMS = pltpu.CompilerParams(has_side_effects=pltpu.SideEffectType.DATAFLOW_SIDE_EFFECTING)


def _peer_list():
    x, y, c = lax.axis_index("x"), lax.axis_index("y"), lax.axis_index("c")
    peers = []
    for k in range(1, N_DEV):
        px = 1 - x if k & 4 else x
        py = 1 - y if k & 2 else y
        pc = 1 - c if k & 1 else c
        peers.append(((px, py, pc), 4 * px + 2 * py + pc))
    return 4 * x + 2 * y + c, peers


def _push_copies(src_refs, land_refs, send_sems, recv_sems, blockwise):
    me, peers = _peer_list()
    copies = []
    for a, (src_ref, land_ref) in enumerate(zip(src_refs, land_refs)):
        for k, (dev, idx) in enumerate(peers):
            sem = a * (N_DEV - 1) + k
            src = src_ref.at[idx] if blockwise else src_ref
            copies.append(tuple(
                pltpu.make_async_remote_copy(src_ref=src, dst_ref=land_ref.at[slot], send_sem=send_sems.at[sem],
                                             recv_sem=recv_sems.at[sem], device_id=dev, device_id_type=MESH_IDS)
                for slot in (me, idx)))
    return copies


def _push_start(srcs, blockwise, after, *, name):
    n = len(srcs)
    blocks = [s_.shape[1:] if blockwise else s_.shape for s_ in srcs]

    def body(*refs):
        src_refs, land_refs = refs[:n], refs[n:2 * n]
        send_sems, recv_sems = refs[2 * n + 1], refs[2 * n + 2]
        token = refs[-1]
        for send, _ in _push_copies(src_refs, land_refs, send_sems, recv_sems, blockwise):
            send.start()
        token[...] = jnp.zeros_like(token)

    n_sem = n * (N_DEV - 1)
    lands = [lax.empty((N_DEV,) + b, s_.dtype) for b, s_ in zip(blocks, srcs)]
    out = pl.pallas_call(
        body,
        name=name,
        in_specs=[HBM_SPEC] * (2 * n) + [ANY],
        out_specs=(SEM_SPEC, SEM_SPEC) + (HBM_SPEC,) * (2 * n) + (pl.BlockSpec(memory_space=pltpu.VMEM),),
        out_shape=(pltpu.SemaphoreType.DMA((n_sem,)), pltpu.SemaphoreType.DMA((n_sem,)))
        + tuple(pltpu.HBM(a.shape, a.dtype) for a in list(srcs) + lands)
        + (jax.ShapeDtypeStruct((8, LANE), F32),),
        input_output_aliases={i: 2 + i for i in range(2 * n)},
        compiler_params=SPLIT_COPY_PARAMS,
    )(*[pltpu.with_memory_space_constraint(a, pltpu.HBM) for a in list(srcs) + lands], after)
    return out[0], out[1], out[2:2 + n], out[2 + n:2 + 2 * n], out[-1]


def _push_wait(send_sems, recv_sems, srcs, lands, blockwise, after, *, name):
    n = len(srcs)

    def body(*refs):
        src_refs, land_refs = refs[:n], refs[n:2 * n]
        send_sems, recv_sems = refs[2 * n], refs[2 * n + 1]
        for send, recv in _push_copies(src_refs, land_refs, send_sems, recv_sems, blockwise):
            send.wait_send()
            recv.wait_recv()

    out = pl.pallas_call(
        body,
        name=name,
        in_specs=[HBM_SPEC] * (2 * n) + [SEM_SPEC, SEM_SPEC, ANY],
        out_specs=(HBM_SPEC,) * (2 * n),
        out_shape=tuple(pltpu.HBM(a.shape, a.dtype) for a in list(srcs) + list(lands)),
        input_output_aliases={i: i for i in range(2 * n)},
        compiler_params=SPLIT_COPY_PARAMS,
    )(*srcs, *lands, send_sems, recv_sems, after)
    return out[n:]


def _with_own_slot(landing, own):
    slot = lax.broadcasted_iota(jnp.int32, (N_DEV,) + (1,) * own.ndim, 0)
    return jnp.where(slot == _my_index(), own[None], landing)


def _sum_slots(parts, own=None, *, name):
    shape = parts.shape[1:]
    n, c = parts.shape[0], parts.shape[-1]
    r = parts.size // (n * c)
    tm = _pick(r, (256, 128, 64, 32, 16, 8))

    def body(p_ref, *rest):
        o_ref = rest[-1]
        me = _my_index()

        def slot(s):
            if own is None:
                return p_ref[s].astype(F32)
            return jnp.where(me == s, rest[0][...], p_ref[s]).astype(F32)

        acc = slot(0)
        for s in range(1, n):
            acc = acc + slot(s)
        o_ref[...] = acc

    tile = pl.BlockSpec((tm, c), lambda i: (i, 0))
    return pl.pallas_call(
        body,
        name=name,
        grid=(r // tm,),
        in_specs=[pl.BlockSpec((n, tm, c), lambda i: (0, i, 0))] + ([] if own is None else [tile]),
        out_specs=tile,
        out_shape=jax.ShapeDtypeStruct((r, c), F32),
        compiler_params=_params(("parallel",)),
    )(parts.reshape(n, r, c), *([] if own is None else [own.reshape(r, c)])).reshape(shape)


def _row_count(shape):
    c = shape[-1]
    rows = 1
    for s in shape[:-1]:
        rows *= s
    return rows, c, c + (-c) % LANE


PACK_ROWS = 256


def _pack_rows(arrays):
    pieces = []
    for a in arrays:
        rows, c, cp = _row_count(a.shape)
        a2 = a.reshape(rows, c)
        if cp > c:
            a2 = jnp.pad(a2, ((0, 0), (0, cp - c)))
        a2 = a2.reshape(rows * cp // LANE, LANE)
        if a2.shape[0] % 8:
            a2 = jnp.pad(a2, ((0, 8 - a2.shape[0] % 8), (0, 0)))
        pieces.append(a2)
    total = sum(p.shape[0] for p in pieces)
    if total % PACK_ROWS:
        pieces.append(jnp.zeros((PACK_ROWS - total % PACK_ROWS, LANE), F32))
    return jnp.concatenate(pieces, axis=0)


def _unpack_rows(packed, shapes, lead=()):
    out, off = [], 0
    for shp in shapes:
        rows, c, cp = _row_count(shp)
        n_rows = rows * cp // LANE
        seg = packed[..., off:off + n_rows, :].reshape(lead + (rows, cp))
        out.append(seg[..., :c].reshape(lead + tuple(shp)))
        off += n_rows + (-n_rows) % 8
    return out


def _unshard(stacked, axis):
    if axis == stacked.ndim - 2:
        return jnp.concatenate([stacked[d] for d in range(N_DEV)], axis=axis)
    moved = jnp.moveaxis(stacked, 0, axis)
    shp = moved.shape
    return moved.reshape(shp[:axis] + (shp[axis] * shp[axis + 1],) + shp[axis + 2:])


def _shard_major(full, axis):
    shp = full.shape
    if axis == full.ndim - 1:
        size = shp[axis] // N_DEV
        return jnp.stack([full[..., d * size:(d + 1) * size] for d in range(N_DEV)])
    split = full.reshape(shp[:axis] + (N_DEV, shp[axis] // N_DEV) + shp[axis + 1:])
    return jnp.moveaxis(split, axis, 0)


def _my_shard(full, axis):
    size = full.shape[axis] // N_DEV
    return lax.dynamic_slice_in_dim(full, _my_index() * size, size, axis)


def _local_step(x, target, w, fetch, emit, n_seq):
    def with_token(vec, token):
        return vec + jnp.tile(token[0:1, :], (1, vec.shape[1] // LANE))

    depth, d_model = w["norm_mix_pre"].shape
    d_inner = w["ssd_norm_w"].shape[1]
    d_xbc = w["ssd_conv_w"].shape[2]
    saved = []
    m, token = fetch(0, "mix", x)
    mix_pre_w = with_token(w["norm_mix_pre"][0:1], token)
    u = _rms_fwd(x, mix_pre_w, out_dtype=BF16, name="l0_mix_pre")
    for i in range(depth):
        j = i // 2
        s = {"x": x, "mix_pre_w": mix_pre_w}
        if i % 2 == 0:
            proj = _mm(u, m["ssd_w_in"], name=f"l{i}_ssd_in")
            xbc, xbc_pre = _ssd_conv_fwd(proj, d_inner, d_xbc, w["ssd_conv_w"][j], w["ssd_conv_b"][j:j + 1], n_seq,
                                         name=f"l{i}_ssd_conv")
            yn, y, hs = _ssd_fwd(proj, xbc, w["ssd_dt_bias"][j:j + 1], w["ssd_a_log"][j:j + 1], w["ssd_d"][j:j + 1],
                                 w["ssd_norm_w"][j:j + 1], n_seq, name=f"l{i}_ssd_scan")
            m_out, token = fetch(i, "out", yn)
            m = {**m, **m_out}
            mix = _mm(yn, m["ssd_w_out"], name=f"l{i}_ssd_out")
            s.update(u=u, proj=proj, xbc=xbc, xbc_pre=xbc_pre, yn=yn, y=y, hs=hs)
        else:
            mix = _pool_fwd(u, m["pool_w"], w["pool_scale"][j:j + 1], n_seq, name=f"l{i}_pool")
            s.update(u=u)
            token = jnp.zeros((8, LANE), F32)
        mix_post_w = with_token(w["norm_mix_post"][i:i + 1], token)
        m_ffn, token = fetch(i, "ffn", mix)
        m = {**m, **m_ffn}
        ffn_pre_w = with_token(w["norm_ffn_pre"][i:i + 1], token)
        x1, n = _res_rms_rms(x, mix, mix_post_w, ffn_pre_w, name=f"l{i}_mix_post_ffn_pre")
        h, hc, a = _ffn_up_act(n, m["ffn_w_up"], w["ffn_conv_w"][i], w["ffn_conv_b"][i:i + 1], n_seq,
                               name=f"l{i}_ffn_up_act")
        f = _mm(a, m["ffn_w_down"], name=f"l{i}_ffn_down")
        s.update(mix=mix, x1=x1, n=n, h=h, hc=hc, a=a, f=f, m=m, ffn_pre_w=ffn_pre_w)
        saved.append(s)
        if i + 1 < depth:
            m, token = fetch(i + 1, "mix", f)
            mix_pre_w = with_token(w["norm_mix_pre"][i + 1:i + 2], token)
            x, u = _res_rms_rms(x1, f, w["norm_ffn_post"][i:i + 1], mix_pre_w,
                                out_dtype=BF16 if (i + 1) % 2 == 0 else F32, name=f"l{i}_ffn_post_mix_pre")
        else:
            x = _res_rms_fwd(x1, f, w["norm_ffn_post"][i:i + 1], name=f"l{i}_ffn_post")

    loss, dx = _loss_head(x, target)
    grads = {k: [None] * len(w[k]) for k in SMALL}
    df, grads["norm_ffn_post"][depth - 1] = _rms_bwd(saved[-1]["f"], w["norm_ffn_post"][depth - 1:depth], dx, None,
                                                      out_dtype=BF16, name=f"l{depth - 1}_ffn_post_b")
    for i in reversed(range(depth)):
        j = i // 2
        s = saved[i]
        m, gm = s["m"], {}
        gm["ffn_w_down"] = _mm(s["a"], df, ta=True, name=f"l{i}_ffn_down_bw")
        dh, grads["ffn_conv_w"][i], grads["ffn_conv_b"][i] = _ffn_down_bx_act_bwd(
            df, m["ffn_w_down"], s["h"], s["hc"], w["ffn_conv_w"][i], n_seq, name=f"l{i}_ffn_act_b")
        dn = _mm(dh, m["ffn_w_up"], tb=True, name=f"l{i}_ffn_up_bx")
        gm["ffn_w_up"] = _mm(s["n"], dh, ta=True, name=f"l{i}_ffn_up_bw")
        token = emit(i, "ffn", gm, dn)
        gm = {}
        dx1, dmix, grads["norm_ffn_pre"][i], grads["norm_mix_post"][i] = _rms_bwd2(
            s["x1"], s["ffn_pre_w"], dn, dx, s["mix"], with_token(w["norm_mix_post"][i:i + 1], token),
            out_dtype=BF16 if i % 2 == 0 else F32, name=f"l{i}_ffn_pre_mix_post_b")
        if i % 2 == 0:
            dyn = _mm(dmix, m["ssd_w_out"], tb=True, name=f"l{i}_ssd_out_bx")
            gm["ssd_w_out"] = _mm(s["yn"], dmix, ta=True, name=f"l{i}_ssd_out_bw")
            token = emit(i, "out", gm, dyn)
            gm = {}
            dxs, db, dc, dz, ddtr, dnw, dbias, dalog, ddsk = _ssd_bwd(
                s["proj"], s["xbc"], s["hs"], s["y"], dyn, w["ssd_dt_bias"][j:j + 1], w["ssd_a_log"][j:j + 1],
                w["ssd_d"][j:j + 1], with_token(w["ssd_norm_w"][j:j + 1], token), n_seq, name=f"l{i}_ssd_scan_b")
            grads["ssd_norm_w"][j], grads["ssd_dt_bias"][j], grads["ssd_a_log"][j], grads["ssd_d"][j] = (
                dnw, dbias, dalog, ddsk)
            dproj, grads["ssd_conv_w"][j], grads["ssd_conv_b"][j] = _ssd_conv_bwd(
                s["proj"], d_inner, w["ssd_conv_w"][j], s["xbc_pre"], (dxs, db, dc), dz, n_seq,
                name=f"l{i}_ssd_conv_b")
            dproj = _fill_cols(dproj, ddtr, d_inner + d_xbc, name=f"l{i}_ssd_dt_b")
            gm["ssd_w_in"] = _mm(s["u"], dproj, ta=True, name=f"l{i}_ssd_in_bw")
            token = emit(i, "mix", gm, dproj)
            du = _mm(dproj, m["ssd_w_in"], tb=True, name=f"l{i}_ssd_in_bx")
        else:
            du, gm["pool_w"], grads["pool_scale"][j] = _pool_bwd(
                s["u"], m["pool_w"], w["pool_scale"][j:j + 1], dmix, n_seq, name=f"l{i}_pool_b")
            token = emit(i, "mix", gm, du)
        if i > 0:
            dx, df, grads["norm_mix_pre"][i], grads["norm_ffn_post"][i - 1] = _rms_bwd2(
                s["x"], with_token(s["mix_pre_w"], token), du, dx1, saved[i - 1]["f"], w["norm_ffn_post"][i - 1:i],
                out_dtype=BF16, name=f"l{i}_mix_pre_ffn_post_b")
        else:
            dx, grads["norm_mix_pre"][i] = _rms_bwd(s["x"], with_token(s["mix_pre_w"], token), du, dx1,
                                                    name=f"l{i}_mix_pre_b")
    return loss, dx, grads


BIG = (("ssd_w_in", 2), ("ssd_w_out", 1), ("pool_w", 2), ("ffn_w_up", 2), ("ffn_w_down", 1))
SMALL_SHARDED = (("ssd_conv_w", 2), ("ffn_conv_w", 2), ("pool_scale", 1))
SMALL = ("ssd_conv_w", "ssd_conv_b", "ssd_dt_bias", "ssd_a_log", "ssd_d", "ssd_norm_w", "pool_scale", "ffn_conv_w",
         "ffn_conv_b", "norm_mix_pre", "norm_mix_post", "norm_ffn_pre", "norm_ffn_post")
WEIGHTS = ("ssd_w_in", "ssd_conv_w", "ssd_conv_b", "ssd_dt_bias", "ssd_a_log", "ssd_d", "ssd_norm_w", "ssd_w_out",
           "pool_w", "pool_scale", "ffn_w_up", "ffn_conv_w", "ffn_conv_b", "ffn_w_down", "norm_mix_pre",
           "norm_mix_post", "norm_ffn_pre", "norm_ffn_post")


def _ssd_sizes(d_inner):
    return d_inner + 2 * N_SSD_GROUPS * D_STATE, d_inner // HEAD_DIM // N_SSD_GROUPS


def _small_compute_layout(full, d_inner):
    _, r_heads = _ssd_sizes(d_inner)
    w = {k: full[k] for k in SMALL}
    for k in ("ssd_dt_bias", "ssd_a_log", "ssd_d"):
        w[k] = _head_pad(full[k], r_heads)
    for k in ("ffn_conv_w", "ffn_conv_b"):
        w[k] = _interleave(full[k])
    return w


def _matmul_compute_layout(k, full, d_inner):
    d_xbc, r_heads = _ssd_sizes(d_inner)
    if k == "ssd_w_in":
        return _ssd_w_in_layout(full, d_inner, d_xbc, r_heads)
    if k == "ffn_w_up":
        return _interleave(full)
    return full


def _layer_matrices(i, part):
    if part == "ffn":
        return (("ffn_w_up", 1, i), ("ffn_w_down", 0, i))
    if i % 2 == 1:
        return (("pool_w", 1, i // 2),) if part == "mix" else ()
    return (("ssd_w_in", 1, i // 2),) if part == "mix" else (("ssd_w_out", 0, i // 2),)


def _fetch_group(i, part):
    mix, out, ffn = (_layer_matrices(i, p) for p in ("mix", "out", "ffn"))
    if i % 2 == 1:
        return mix + ffn if part == "mix" else ()
    if i == 0:
        return {"mix": mix, "out": out + ffn, "ffn": ()}[part]
    return {"mix": mix + out, "out": (), "ffn": ffn}[part]


def _matmul_grad_reference_layout(k, g, d_inner):
    d_xbc, r_heads = _ssd_sizes(d_inner)
    if k == "ssd_w_in":
        return _ssd_w_in_unlayout(g, d_inner, d_xbc, r_heads)
    if k == "ffn_w_up":
        return _deinterleave(g)
    return g


def _small_grads_reference_layout(grads, shapes, d_inner):
    _, r_heads = _ssd_sizes(d_inner)
    g = {k: jnp.stack(grads[k]) for k in SMALL}
    for k in ("ssd_dt_bias", "ssd_a_log", "ssd_d"):
        g[k] = _head_unpad(g[k][:, 0], r_heads)
    for k in ("ffn_conv_w", "ffn_conv_b"):
        g[k] = _deinterleave(g[k])
    return {k: v.reshape(shapes[k]) for k, v in g.items()}


def kernel(x, ssd_w_in, ssd_conv_w, ssd_conv_b, ssd_dt_bias, ssd_a_log, ssd_d, ssd_norm_w, ssd_w_out, pool_w, pool_scale, ffn_w_up, ffn_conv_w, ffn_conv_b, ffn_w_down, norm_mix_pre, norm_mix_post, norm_ffn_pre, norm_ffn_post, loss_target, m_ssd_w_in, m_ssd_conv_w, m_ssd_conv_b, m_ssd_dt_bias, m_ssd_a_log, m_ssd_d, m_ssd_norm_w, m_ssd_w_out, m_pool_w, m_pool_scale, m_ffn_w_up, m_ffn_conv_w, m_ffn_conv_b, m_ffn_w_down, m_norm_mix_pre, m_norm_mix_post, m_norm_ffn_pre, m_norm_ffn_post, v_ssd_w_in, v_ssd_conv_w, v_ssd_conv_b, v_ssd_dt_bias, v_ssd_a_log, v_ssd_d, v_ssd_norm_w, v_ssd_w_out, v_pool_w, v_pool_scale, v_ffn_w_up, v_ffn_conv_w, v_ffn_conv_b, v_ffn_w_down, v_norm_mix_pre, v_norm_mix_post, v_norm_ffn_pre, v_norm_ffn_post):
    shards = dict(ssd_w_in=ssd_w_in, ssd_conv_w=ssd_conv_w, ssd_conv_b=ssd_conv_b, ssd_dt_bias=ssd_dt_bias,
                  ssd_a_log=ssd_a_log, ssd_d=ssd_d, ssd_norm_w=ssd_norm_w, ssd_w_out=ssd_w_out, pool_w=pool_w,
                  pool_scale=pool_scale, ffn_w_up=ffn_w_up, ffn_conv_w=ffn_conv_w, ffn_conv_b=ffn_conv_b,
                  ffn_w_down=ffn_w_down, norm_mix_pre=norm_mix_pre, norm_mix_post=norm_mix_post,
                  norm_ffn_pre=norm_ffn_pre, norm_ffn_post=norm_ffn_post)
    moments_m = dict(zip(WEIGHTS, (m_ssd_w_in, m_ssd_conv_w, m_ssd_conv_b, m_ssd_dt_bias, m_ssd_a_log, m_ssd_d, m_ssd_norm_w, m_ssd_w_out, m_pool_w, m_pool_scale, m_ffn_w_up, m_ffn_conv_w, m_ffn_conv_b, m_ffn_w_down, m_norm_mix_pre, m_norm_mix_post, m_norm_ffn_pre, m_norm_ffn_post)))
    moments_v = dict(zip(WEIGHTS, (v_ssd_w_in, v_ssd_conv_w, v_ssd_conv_b, v_ssd_dt_bias, v_ssd_a_log, v_ssd_d, v_ssd_norm_w, v_ssd_w_out, v_pool_w, v_pool_scale, v_ffn_w_up, v_ffn_conv_w, v_ffn_conv_b, v_ffn_w_down, v_norm_mix_pre, v_norm_mix_post, v_norm_ffn_pre, v_norm_ffn_post)))
    n_seq, seq, d_model = x.shape
    t = n_seq * seq

    d_inner = ssd_norm_w.shape[1]
    depth = norm_mix_pre.shape[0]
    x2 = x.reshape(t, d_model)

    shard16 = {k: shards[k].astype(BF16) for k, _ in BIG}
    order = [(i, part) for i in range(depth) for part in ("mix", "out", "ffn") if _fetch_group(i, part)]
    fetches = {}

    def start_fetch(key, after):
        srcs = [shard16[k][l] for k, _, l in _fetch_group(*key)]
        fetches[key] = _push_start(srcs, False, after, name=f"fetch{key[0]}{key[1]}_start")

    full = dict(shards)
    small_all = _all_gather(_pack_rows([shards[k] for k, _ in SMALL_SHARDED]), name="gather_small_weights")
    small_stacked = _unpack_rows(small_all, [shards[k].shape for k, _ in SMALL_SHARDED], lead=(N_DEV,))
    for (k, axis), st in zip(SMALL_SHARDED, small_stacked):
        full[k] = _unshard(st, axis)
    w = _small_compute_layout(full, d_inner)
    ready = {}

    def fetch(i, part, x_now):
        key = (i, part)
        token = jnp.zeros((8, LANE), F32)
        if key in order:
            if key == order[0]:
                wholes = [_unshard(_all_gather(shard16[k][l], name=f"fetch0_{k}"), axis) for k, axis, l in _fetch_group(i, part)]
                nxt_after = wholes[0]
            else:
                send, recv, srcs, lands, _ = fetches[key]
                lands = _push_wait(send, recv, srcs, lands, False, x_now, name=f"fetch{i}{part}_wait")
                wholes = [_unshard(_with_own_slot(land, shard16[k][l]), axis)
                          for (k, axis, l), land in zip(_fetch_group(i, part), lands)]
                nxt_after = lands[0]
            for (k, _, l), whole in zip(_fetch_group(i, part), wholes):
                ready[k, l] = _matmul_compute_layout(k, whole, d_inner)
            nxt = order.index(key) + 1
            if nxt < len(order):
                start_fetch(order[nxt], nxt_after)
                token = fetches[order[nxt]][4]
        return {k: ready[k, l] for k, _, l in _layer_matrices(i, part)}, token

    g_layers = {}
    in_flight = []

    def finish_exchange(after):
        key, blocks, (send, recv, srcs, lands, _) = in_flight.pop(0)
        lands = _push_wait(send, recv, srcs, lands, True, after, name=f"exchange{key[0]}{key[1]}_wait")
        for (k, _, l), land, block in zip(_layer_matrices(*key), lands, blocks):
            own = lax.dynamic_index_in_dim(block, _my_index(), 0, keepdims=False)
            g_layers[k, l] = _sum_slots(land, own, name=f"sum{key[0]}_{k}")

    def emit(i, part, gm, dx_now):
        if len(in_flight) >= 2:
            finish_exchange(dx_now)
        blocks = [_shard_major(_matmul_grad_reference_layout(k, gm[k].astype(BF16), d_inner), axis)
                  for k, axis, _ in _layer_matrices(i, part)]
        started = _push_start(blocks, True, dx_now, name=f"exchange{i}{part}_start")
        in_flight.append(((i, part), blocks, started))
        return started[4]

    loss, dx, grads = _local_step(x2, loss_target.reshape(t, d_model), w, fetch, emit, n_seq)
    loss = lax.psum(loss, ("x", "y", "c"))

    g_shard = {}
    small_shapes = {k: full[k].shape for k in SMALL}
    g_small = _small_grads_reference_layout(grads, small_shapes, d_inner)
    s_all = _all_gather(_pack_rows([g_small[k] for k in SMALL]) + in_flight[-1][2][4][0:1, :], name="gather_small_grads")
    for k, g in zip(SMALL, _unpack_rows(_sum_slots(s_all, name="sum_small_grads"), [small_shapes[k] for k in SMALL])):
        g_shard[k] = g
    for k, axis in SMALL_SHARDED:
        g_shard[k] = _my_shard(g_shard[k], axis)

    last = [k for key, _, _ in in_flight for k, _, _ in _layer_matrices(*key)]
    deltas, new_m, new_v = {}, {}, {}
    for k in [k for k in WEIGHTS if k not in last] + last:
        if k == last[0]:
            while in_flight:
                finish_exchange(deltas["ffn_w_up"])
        if k in dict(BIG):
            g_shard[k] = jnp.stack([g_layers[k, l] for l in range(shards[k].shape[0])])
        deltas[k], new_m[k], new_v[k] = _adamw(shards[k], g_shard[k], moments_m[k], moments_v[k], name=f"adamw_{k}")
    return (loss, dx.reshape(x.shape), *[g_shard[k] for k in WEIGHTS], *[deltas[k] for k in WEIGHTS],
            *[new_m[k] for k in WEIGHTS], *[new_v[k] for k in WEIGHTS])
```

```python
import functools

import jax
import jax.numpy as jnp
from jax import lax
from jax.experimental import pallas as pl
from jax.experimental.pallas import tpu as pltpu

F32 = jnp.float32
BF16 = jnp.bfloat16

N_DEV = 8
HEAD_DIM = 64
N_SSD_GROUPS = 4
D_STATE = 128
CHUNK = 128
POOL_WINDOWS = (2, 4, 8, 16)
EPS = 1e-6
LANE = 128
ADAM_LR = 0.001
ADAM_B1 = 0.9
ADAM_B2 = 0.999
ADAM_EPS = 1e-08
ADAM_WD = 0.01
ADAM_STEP = 10
VMEM_LIMIT = 56 * 1024 * 1024
ANY = pl.BlockSpec(memory_space=pl.ANY)


def _pick(n, cands):
    for c in cands:
        if n % c == 0:
            return c
    return n


def _params(sem):
    return pltpu.CompilerParams(dimension_semantics=sem, vmem_limit_bytes=VMEM_LIMIT)


def _sigmoid(x):
    return 0.5 * jnp.tanh(0.5 * x) + 0.5


def _silu(x):
    return x * _sigmoid(x)


def _dsilu(x):
    s = _sigmoid(x)
    return s * (1.0 + x * (1.0 - s))


def _shift_down(x, s):
    rows = lax.broadcasted_iota(jnp.int32, x.shape, 0)
    return jnp.where(rows >= s, pltpu.roll(x, s, 0), 0.0)


def _shift_up(x, s):
    n = x.shape[0]
    rows = lax.broadcasted_iota(jnp.int32, x.shape, 0)
    return jnp.where(rows < n - s, pltpu.roll(x, n - s, 0), 0.0)


MM_VMEM_BUDGET = 40 * 1024 * 1024
MM_STEP_BYTES = 1_300_000
MM_SUB = 512


def _mm_tiles(m, n, k, a_bytes, b_bytes, o_bytes):
    def cands(dim, sizes):
        out = [s for s in sizes if s <= dim and dim % s == 0]
        return out or [dim]

    best = None
    for tm in cands(m, (m, m // 2, 2048, 1024, 512, 256, 128)):
        if tm % LANE:
            continue
        for tn in cands(n, (n, n // 2, n // 4, 2048, 1024, 512, 256, 128)):
            if tn % (2 * LANE) and tn != n:
                continue
            for tk in cands(k, (k, k // 2, 2048, 1024, 512)):
                if tk % LANE:
                    continue
                nk = k // tk
                acc = tm * tn * 4 if (nk > 1 and o_bytes != 4) else 0
                temps = tm * min(tn, MM_SUB) * 4 + (tm * tk * 2 if a_bytes == 4 else 0) + (tk * tn * 2 if b_bytes == 4 else 0)
                vmem = 2 * (tm * tk * a_bytes + tk * tn * b_bytes + tm * tn * o_bytes) + acc + temps
                if vmem > MM_VMEM_BUDGET:
                    continue
                steps = (m // tm) * (n // tn) * nk
                acc_pass = steps * tm * tn * 2 if nk > 1 else 0
                cost = (m * k * a_bytes * (n // tn) + k * n * b_bytes * (m // tm) + m * n * o_bytes
                        + steps * MM_STEP_BYTES + acc_pass)
                if best is None or cost < best[0]:
                    best = (cost, tm, tn, tk)
    return best[1:]


def _mm(a, b, *, ta=False, tb=False, out_dtype=F32, name="mm"):
    m, k = (a.shape[1], a.shape[0]) if ta else a.shape
    n = b.shape[0] if tb else b.shape[1]
    o_bytes = jnp.dtype(out_dtype).itemsize
    tm, tn, tk = _mm_tiles(m, n, k, a.dtype.itemsize, b.dtype.itemsize, o_bytes)
    nk = k // tk
    sub = _pick(tn, (MM_SUB, 256))
    use_acc = nk > 1 and o_bytes != 4
    a_spec = pl.BlockSpec((tk, tm), lambda i, j, kk: (kk, i)) if ta else pl.BlockSpec((tm, tk), lambda i, j, kk: (i, kk))
    b_spec = pl.BlockSpec((tn, tk), lambda i, j, kk: (j, kk)) if tb else pl.BlockSpec((tk, tn), lambda i, j, kk: (kk, j))
    dims = (((1,), (1 if tb else 0,)), ((), ()))

    def body(a_ref, b_ref, o_ref, *scratch):
        kk = pl.program_id(2)
        acc_ref = scratch[0] if use_acc else o_ref
        if nk > 1:
            @pl.when(kk == 0)
            def _():
                acc_ref[...] = jnp.zeros_like(acc_ref)

        av = a_ref[...].astype(BF16)
        if ta:
            av = av.T
        for s in range(tn // sub):
            cols = slice(s * sub, (s + 1) * sub)
            bv = (b_ref[cols, :] if tb else b_ref[:, cols]).astype(BF16)
            part = lax.dot_general(av, bv, dims, preferred_element_type=F32)
            if nk == 1:
                o_ref[:, cols] = part.astype(out_dtype)
            else:
                acc_ref[:, cols] += part
        if use_acc:
            @pl.when(kk == nk - 1)
            def _():
                o_ref[...] = acc_ref[...].astype(out_dtype)

    return pl.pallas_call(
        body,
        name=name,
        grid=(m // tm, n // tn, nk),
        in_specs=[a_spec, b_spec],
        out_specs=pl.BlockSpec((tm, tn), lambda i, j, kk: (i, j)),
        out_shape=jax.ShapeDtypeStruct((m, n), out_dtype),
        scratch_shapes=[pltpu.VMEM((tm, tn), F32)] if use_acc else [],
        compiler_params=_params(("parallel", "parallel", "arbitrary")),
    )(a, b)


def _rms_fwd(x, w, *, out_dtype, name):
    t, d = x.shape
    tm = _pick(t, (512, 256, 128))

    def body(x_ref, w_ref, o_ref):
        xv = x_ref[...]
        rstd = lax.rsqrt(jnp.mean(xv * xv, axis=-1, keepdims=True) + EPS)
        o_ref[...] = (xv * rstd * w_ref[...]).astype(out_dtype)

    return pl.pallas_call(
        body,
        name=name,
        grid=(t // tm,),
        in_specs=[pl.BlockSpec((tm, d), lambda i: (i, 0)), pl.BlockSpec((1, d), lambda i: (0, 0))],
        out_specs=pl.BlockSpec((tm, d), lambda i: (i, 0)),
        out_shape=jax.ShapeDtypeStruct((t, d), out_dtype),
        compiler_params=_params(("parallel",)),
    )(x, w)


def _res_rms_fwd(x, f, w, *, name):
    t, d = x.shape
    tm = _pick(t, (512, 256, 128))

    def body(x_ref, f_ref, w_ref, o_ref):
        fv = f_ref[...]
        rstd = lax.rsqrt(jnp.mean(fv * fv, axis=-1, keepdims=True) + EPS)
        o_ref[...] = x_ref[...] + fv * rstd * w_ref[...]

    row = pl.BlockSpec((tm, d), lambda i: (i, 0))
    return pl.pallas_call(
        body,
        name=name,
        grid=(t // tm,),
        in_specs=[row, row, pl.BlockSpec((1, d), lambda i: (0, 0))],
        out_specs=row,
        out_shape=jax.ShapeDtypeStruct((t, d), F32),
        compiler_params=_params(("parallel",)),
    )(x, f, w)


def _rms_bwd(x, w, dy, resid, *, out_dtype=F32, name):
    t, d = x.shape
    tm = _pick(t, (512, 256, 128))
    has_res = resid is not None

    def body(*refs):
        if has_res:
            x_ref, w_ref, dy_ref, r_ref, dx_ref, dw_ref = refs
        else:
            x_ref, w_ref, dy_ref, dx_ref, dw_ref = refs
        xv = x_ref[...]
        dyv = dy_ref[...].astype(F32)
        rstd = lax.rsqrt(jnp.mean(xv * xv, axis=-1, keepdims=True) + EPS)
        xn = xv * rstd
        g = dyv * w_ref[...]
        dx = rstd * (g - xn * jnp.mean(g * xn, axis=-1, keepdims=True))
        if has_res:
            dx = dx + r_ref[...]
        dx_ref[...] = dx.astype(out_dtype)
        part = jnp.sum(dyv * xn, axis=0, keepdims=True)

        @pl.when(pl.program_id(0) == 0)
        def _():
            dw_ref[...] = part

        @pl.when(pl.program_id(0) > 0)
        def _():
            dw_ref[...] += part

    row = pl.BlockSpec((tm, d), lambda i: (i, 0))
    vec = pl.BlockSpec((1, d), lambda i: (0, 0))
    ins = [x, w, dy] + ([resid] if has_res else [])
    return pl.pallas_call(
        body,
        name=name,
        grid=(t // tm,),
        in_specs=[row, vec, row] + ([row] if has_res else []),
        out_specs=[row, vec],
        out_shape=[jax.ShapeDtypeStruct((t, d), out_dtype), jax.ShapeDtypeStruct((1, d), F32)],
        compiler_params=_params(("arbitrary",)),
    )(*ins)


def _res_rms_rms(x, f, w_post, w_pre, *, out_dtype=BF16, name):
    t, d = x.shape
    tm = _pick(t, (512, 256, 128))

    def body(x_ref, f_ref, wp_ref, wn_ref, x1_ref, n_ref):
        fv = f_ref[...]
        x1 = x_ref[...] + fv * lax.rsqrt(jnp.mean(fv * fv, axis=-1, keepdims=True) + EPS) * wp_ref[...]
        x1_ref[...] = x1
        n_ref[...] = (x1 * lax.rsqrt(jnp.mean(x1 * x1, axis=-1, keepdims=True) + EPS) * wn_ref[...]).astype(out_dtype)

    row = pl.BlockSpec((tm, d), lambda i: (i, 0))
    vec = pl.BlockSpec((1, d), lambda i: (0, 0))
    return pl.pallas_call(
        body,
        name=name,
        grid=(t // tm,),
        in_specs=[row, row, vec, vec],
        out_specs=[row, row],
        out_shape=[jax.ShapeDtypeStruct((t, d), F32), jax.ShapeDtypeStruct((t, d), out_dtype)],
        compiler_params=_params(("parallel",)),
    )(x, f, w_post, w_pre)


def _rms_bwd2(xa, wa, dya, resid, xb, wb, *, out_dtype, name):
    t, d = xa.shape
    tm = _pick(t, (512, 256, 128))

    def norm_bwd(xv, w, dyv):
        rstd = lax.rsqrt(jnp.mean(xv * xv, axis=-1, keepdims=True) + EPS)
        xn = xv * rstd
        g = dyv * w
        return rstd * (g - xn * jnp.mean(g * xn, axis=-1, keepdims=True)), jnp.sum(dyv * xn, axis=0, keepdims=True)

    def body(xa_ref, wa_ref, dya_ref, r_ref, xb_ref, wb_ref, da_ref, db_ref, dwa_ref, dwb_ref):
        da, dwa = norm_bwd(xa_ref[...], wa_ref[...], dya_ref[...].astype(F32))
        da = da + r_ref[...]
        da_ref[...] = da
        db, dwb = norm_bwd(xb_ref[...], wb_ref[...], da)
        db_ref[...] = db.astype(out_dtype)

        @pl.when(pl.program_id(0) == 0)
        def _():
            dwa_ref[...] = dwa
            dwb_ref[...] = dwb

        @pl.when(pl.program_id(0) > 0)
        def _():
            dwa_ref[...] += dwa
            dwb_ref[...] += dwb

    row = pl.BlockSpec((tm, d), lambda i: (i, 0))
    vec = pl.BlockSpec((1, d), lambda i: (0, 0))
    return pl.pallas_call(
        body,
        name=name,
        grid=(t // tm,),
        in_specs=[row, vec, row, row, row, vec],
        out_specs=[row, row, vec, vec],
        out_shape=[jax.ShapeDtypeStruct((t, d), F32), jax.ShapeDtypeStruct((t, d), out_dtype),
                   jax.ShapeDtypeStruct((1, d), F32), jax.ShapeDtypeStruct((1, d), F32)],
        compiler_params=_params(("arbitrary",)),
    )(xa, wa, dya, resid, xb, wb)


def _loss_head(y, target, *, name="loss_head"):
    t, d = y.shape
    tm = _pick(t, (512, 256, 128))

    def body(y_ref, t_ref, dy_ref, l_ref):
        err = y_ref[...] - t_ref[...]
        dy_ref[...] = err * (1.0 / d)
        part = jnp.sum(jnp.sum(err * err, axis=-1, keepdims=True), axis=0, keepdims=True) * (0.5 / d)
        part = jnp.broadcast_to(part, (1, LANE))

        @pl.when(pl.program_id(0) == 0)
        def _():
            l_ref[...] = part

        @pl.when(pl.program_id(0) > 0)
        def _():
            l_ref[...] += part

    row = pl.BlockSpec((tm, d), lambda i: (i, 0))
    dy, l = pl.pallas_call(
        body,
        name=name,
        grid=(t // tm,),
        in_specs=[row, row],
        out_specs=[row, pl.BlockSpec((1, LANE), lambda i: (0, 0))],
        out_shape=[jax.ShapeDtypeStruct((t, d), F32), jax.ShapeDtypeStruct((1, LANE), F32)],
        compiler_params=_params(("arbitrary",)),
    )(y, target)
    return l[0, 0], dy


def _conv_taps(h, w_ref, k_taps):
    out = h * w_ref[k_taps - 1:k_taps, :]
    for k in range(k_taps - 1):
        out = out + _shift_down(h, k_taps - 1 - k) * w_ref[k:k + 1, :]
    return out


def _conv_taps_bwd(h, dhc, w_ref, k_taps):
    dh = dhc * w_ref[k_taps - 1:k_taps, :]
    dws = []
    for k in range(k_taps - 1):
        up = _shift_up(dhc, k_taps - 1 - k)
        dh = dh + up * w_ref[k:k + 1, :]
        dws.append(jnp.sum(up * h, axis=0, keepdims=True))
    dws.append(jnp.sum(dhc * h, axis=0, keepdims=True))
    return dh, jnp.concatenate(dws, axis=0)


FFN_TC = 256


def _interleave(w, tc=FFN_TC):
    f = w.shape[-1] // 2
    tiles = []
    for j in range(f // tc):
        tiles += [w[..., j * tc:(j + 1) * tc], w[..., f + j * tc:f + (j + 1) * tc]]
    return jnp.concatenate(tiles, axis=-1)


def _deinterleave(w, tc=FFN_TC):
    n_tiles = w.shape[-1] // tc
    return jnp.concatenate([w[..., j * tc:(j + 1) * tc] for j in list(range(0, n_tiles, 2)) + list(range(1, n_tiles, 2))],
                           axis=-1)


FFN_ROWS = 512
HALO = 8


def _ffn_up_act(n, w_up, conv_w, conv_b, n_seq, *, name):
    t, d = n.shape
    f2 = w_up.shape[1]
    seq = t // n_seq
    tc = FFN_TC
    nj = f2 // (2 * tc)
    k_taps = conv_w.shape[0]
    rows = min(FFN_ROWS, seq)

    def body(n_ref, wu_ref, w_ref, b_ref, h_ref, hc_ref, o_ref, h_scr):
        h_scr[0:HALO, :] = jnp.zeros((HALO, 2 * tc), F32)
        wu = wu_ref[...]
        for r in range(seq // rows):
            chunk = slice(r * rows, (r + 1) * rows)
            h = jnp.dot(n_ref[chunk, :], wu, preferred_element_type=F32)
            h_scr[HALO + r * rows:HALO + (r + 1) * rows, :] = h
            h_ref[chunk, :] = h.astype(BF16)
            ext = h_scr[r * rows:HALO + (r + 1) * rows, :]
            hc = ext * w_ref[k_taps - 1:k_taps, :]
            for k in range(k_taps - 1):
                hc = hc + pltpu.roll(ext, k_taps - 1 - k, 0) * w_ref[k:k + 1, :]
            hc = hc[HALO:, :] + b_ref[...]
            hc_ref[chunk, :] = hc.astype(BF16)
            o_ref[chunk, :] = (_silu(hc[:, :tc]) * hc[:, tc:]).astype(BF16)

    wide = pl.BlockSpec((seq, 2 * tc), lambda b, j: (b, j))
    return pl.pallas_call(
        body,
        name=name,
        grid=(n_seq, nj),
        in_specs=[
            pl.BlockSpec((seq, d), lambda b, j: (b, 0)),
            pl.BlockSpec((d, 2 * tc), lambda b, j: (0, j)),
            pl.BlockSpec((k_taps, 2 * tc), lambda b, j: (0, j)),
            pl.BlockSpec((1, 2 * tc), lambda b, j: (0, j)),
        ],
        out_specs=[wide, wide, pl.BlockSpec((seq, tc), lambda b, j: (b, j))],
        out_shape=[jax.ShapeDtypeStruct((t, f2), BF16), jax.ShapeDtypeStruct((t, f2), BF16),
                   jax.ShapeDtypeStruct((t, f2 // 2), BF16)],
        scratch_shapes=[pltpu.VMEM((HALO + seq, 2 * tc), F32)],
        compiler_params=_params(("parallel", "arbitrary")),
    )(n, w_up, conv_w, conv_b)


def _ffn_down_bx_act_bwd(df, w_down, h, hc, conv_w, n_seq, *, name):
    t, d = df.shape
    f2 = h.shape[1]
    seq = t // n_seq
    tc = FFN_TC
    nj = f2 // (2 * tc)
    k_taps = conv_w.shape[0]

    def body(df_ref, wd_ref, h_ref, hc_ref, w_ref, dh_ref, dw_ref, db_ref):
        dav = lax.dot_general(df_ref[...], wd_ref[...], (((1,), (1,)), ((), ())), preferred_element_type=F32)
        hcv = hc_ref[...].astype(F32)
        gate, val = hcv[:, :tc], hcv[:, tc:]
        dhc = jnp.concatenate([dav * val * _dsilu(gate), dav * _silu(gate)], axis=1)
        dh, dw = _conv_taps_bwd(h_ref[...].astype(F32), dhc, w_ref, k_taps)
        dh_ref[...] = dh.astype(BF16)
        dw_ref[0] = dw
        db_ref[0] = jnp.sum(dhc, axis=0, keepdims=True)

    wide = pl.BlockSpec((seq, 2 * tc), lambda b, j: (b, j))
    dh, dw, db = pl.pallas_call(
        body,
        name=name,
        grid=(n_seq, nj),
        in_specs=[
            pl.BlockSpec((seq, d), lambda b, j: (b, 0)),
            pl.BlockSpec((tc, d), lambda b, j: (j, 0)),
            wide, wide,
            pl.BlockSpec((k_taps, 2 * tc), lambda b, j: (0, j)),
        ],
        out_specs=[
            wide,
            pl.BlockSpec((1, k_taps, 2 * tc), lambda b, j: (b, 0, j)),
            pl.BlockSpec((1, 1, 2 * tc), lambda b, j: (b, 0, j)),
        ],
        out_shape=[
            jax.ShapeDtypeStruct((t, f2), BF16),
            jax.ShapeDtypeStruct((n_seq, k_taps, f2), F32),
            jax.ShapeDtypeStruct((n_seq, 1, f2), F32),
        ],
        compiler_params=_params(("parallel", "arbitrary")),
    )(df, w_down, h, hc, conv_w)
    return dh, jnp.sum(dw, axis=0), jnp.sum(db, axis=0)


def _window_mixed(u, window):
    s = u
    step = 1
    while step < window:
        s = s + _shift_down(s, step)
        step *= 2
    rows = lax.broadcasted_iota(jnp.int32, u.shape, 0)
    inv_cnt = 1.0 / jnp.minimum(rows + 1, window).astype(F32)
    return s * inv_cnt - u, inv_cnt


def _window_mixed_bwd(dmixed, inv_cnt, window):
    r = dmixed * inv_cnt
    s = r
    step = 1
    while step < window:
        s = s + _shift_up(s, step)
        step *= 2
    return s - dmixed


def _pool_fwd(u, w, scale, n_seq, *, name):
    t, d = u.shape
    seq = t // n_seq
    n_g, dg, _ = w.shape

    def body(u_ref, w_ref, s_ref, o_ref):
        for k, window in enumerate(POOL_WINDOWS):
            @pl.when(pl.program_id(1) == k)
            def _(window=window):
                mixed, _ = _window_mixed(u_ref[...], window)
                pre = jnp.dot(mixed.astype(BF16), w_ref[0].astype(BF16), preferred_element_type=F32)
                o_ref[...] = pre * s_ref[...]

    return pl.pallas_call(
        body,
        name=name,
        grid=(n_seq, n_g),
        in_specs=[
            pl.BlockSpec((seq, dg), lambda b, g: (b, g)),
            pl.BlockSpec((1, dg, dg), lambda b, g: (g, 0, 0)),
            pl.BlockSpec((1, dg), lambda b, g: (0, g)),
        ],
        out_specs=pl.BlockSpec((seq, dg), lambda b, g: (b, g)),
        out_shape=jax.ShapeDtypeStruct((t, d), F32),
        compiler_params=_params(("parallel", "parallel")),
    )(u, w, scale)


def _pool_bwd(u, w, scale, dout, n_seq, *, name):
    t, d = u.shape
    seq = t // n_seq
    n_g, dg, _ = w.shape

    def body(u_ref, w_ref, s_ref, do_ref, du_ref, dw_ref, ds_ref):
        group = pl.program_id(0)
        first = pl.program_id(1) == 0
        for k, window in enumerate(POOL_WINDOWS):
            @pl.when(group == k)
            def _(window=window):
                mixed, inv_cnt = _window_mixed(u_ref[...], window)
                mixed_b = mixed.astype(BF16)
                w_b = w_ref[0].astype(BF16)
                dov = do_ref[...]
                pre = jnp.dot(mixed_b, w_b, preferred_element_type=F32)
                dsc = jnp.sum(dov * pre, axis=0, keepdims=True)
                dpre = (dov * s_ref[...]).astype(BF16)
                dw = lax.dot_general(mixed_b, dpre, (((0,), (0,)), ((), ())), preferred_element_type=F32)
                dmixed = lax.dot_general(dpre, w_b, (((1,), (1,)), ((), ())), preferred_element_type=F32)
                du_ref[...] = _window_mixed_bwd(dmixed, inv_cnt, window)

                @pl.when(first)
                def _():
                    dw_ref[0] = dw
                    ds_ref[...] = dsc

                @pl.when(jnp.logical_not(first))
                def _():
                    dw_ref[0] += dw
                    ds_ref[...] += dsc

    return pl.pallas_call(
        body,
        name=name,
        grid=(n_g, n_seq),
        in_specs=[
            pl.BlockSpec((seq, dg), lambda g, b: (b, g)),
            pl.BlockSpec((1, dg, dg), lambda g, b: (g, 0, 0)),
            pl.BlockSpec((1, dg), lambda g, b: (0, g)),
            pl.BlockSpec((seq, dg), lambda g, b: (b, g)),
        ],
        out_specs=[
            pl.BlockSpec((seq, dg), lambda g, b: (b, g)),
            pl.BlockSpec((1, dg, dg), lambda g, b: (g, 0, 0)),
            pl.BlockSpec((1, dg), lambda g, b: (0, g)),
        ],
        out_shape=[
            jax.ShapeDtypeStruct((t, d), F32),
            jax.ShapeDtypeStruct((n_g, dg, dg), F32),
            jax.ShapeDtypeStruct((1, d), F32),
        ],
        compiler_params=_params(("parallel", "arbitrary")),
    )(u, w, scale, dout)


def _adamw(w, g, m, v, *, name):
    shape = w.shape
    c = shape[-1]
    r = w.size // c
    tm = _pick(r, (512, 256, 128, 64, 32, 16, 8))

    def body(w_ref, g_ref, m_ref, v_ref, d_ref, nm_ref, nv_ref):
        gv = g_ref[...]
        nm = ADAM_B1 * m_ref[...] + (1.0 - ADAM_B1) * gv
        nv = ADAM_B2 * v_ref[...] + (1.0 - ADAM_B2) * (gv * gv)
        m_hat = nm / (1.0 - ADAM_B1 ** ADAM_STEP)
        v_hat = nv / (1.0 - ADAM_B2 ** ADAM_STEP)
        d_ref[...] = -ADAM_LR * (m_hat / (jnp.sqrt(v_hat) + ADAM_EPS) + ADAM_WD * w_ref[...])
        nm_ref[...] = nm
        nv_ref[...] = nv

    blk = pl.BlockSpec((tm, c), lambda i: (i, 0))
    out = jax.ShapeDtypeStruct((r, c), F32)
    res = pl.pallas_call(
        body,
        name=name,
        grid=(r // tm,),
        in_specs=[blk] * 4,
        out_specs=[blk] * 3,
        out_shape=[out] * 3,
        compiler_params=_params(("parallel",)),
    )(w.reshape(r, c), g.reshape(r, c), m.reshape(r, c), v.reshape(r, c))
    return tuple(a.reshape(shape) for a in res)


CONV_TC = 256


def _ssd_conv_fwd(proj, col0, n_cols, conv_w, conv_b, n_seq, *, name):
    t = proj.shape[0]
    seq = t // n_seq
    tc = CONV_TC
    off = col0 // tc
    k_taps = conv_w.shape[0]

    def body(h_ref, w_ref, b_ref, o_ref, pre_ref):
        pre = _conv_taps(h_ref[...], w_ref, k_taps) + b_ref[...]
        pre_ref[...] = pre.astype(BF16)
        o_ref[...] = _silu(pre)

    return pl.pallas_call(
        body,
        name=name,
        grid=(n_seq, n_cols // tc),
        in_specs=[
            pl.BlockSpec((seq, tc), lambda b, j: (b, j + off)),
            pl.BlockSpec((k_taps, tc), lambda b, j: (0, j)),
            pl.BlockSpec((1, tc), lambda b, j: (0, j)),
        ],
        out_specs=[pl.BlockSpec((seq, tc), lambda b, j: (b, j))] * 2,
        out_shape=[jax.ShapeDtypeStruct((t, n_cols), F32), jax.ShapeDtypeStruct((t, n_cols), BF16)],
        compiler_params=_params(("parallel", "parallel")),
    )(proj, conv_w, conv_b)


def _ssd_conv_bwd(proj, col0, conv_w, pre, dparts, dproj, n_seq, *, name):
    t = proj.shape[0]
    seq = t // n_seq
    tc = CONV_TC
    off = col0 // tc
    k_taps = conv_w.shape[0]
    widths = [d.shape[1] // tc for d in dparts]
    starts = [sum(widths[:i]) for i in range(len(widths))]
    n_blocks = sum(widths)
    n_parts = len(dparts)

    def body(h_ref, w_ref, pre_ref, *rest):
        part_refs = rest[:n_parts]
        dh_ref, dw_ref, db_ref = rest[n_parts + 1:]
        j = pl.program_id(0)
        da = part_refs[-1][...]
        for i in reversed(range(n_parts - 1)):
            da = jnp.where(j < starts[i + 1], part_refs[i][...], da)
        dhc = da * _dsilu(pre_ref[...].astype(F32))
        dh, dw = _conv_taps_bwd(h_ref[...], dhc, w_ref, k_taps)
        dh_ref[...] = dh.astype(BF16)
        db = jnp.sum(dhc, axis=0, keepdims=True)

        @pl.when(pl.program_id(1) == 0)
        def _():
            dw_ref[...] = dw
            db_ref[...] = db

        @pl.when(pl.program_id(1) > 0)
        def _():
            dw_ref[...] += dw
            db_ref[...] += db

    def part_spec(start, width):
        return pl.BlockSpec((seq, tc), lambda j, b: (b, jnp.clip(j - start, 0, width - 1)))

    n_cols = n_blocks * tc
    return pl.pallas_call(
        body,
        name=name,
        grid=(n_blocks, n_seq),
        in_specs=[
            pl.BlockSpec((seq, tc), lambda j, b: (b, j + off)),
            pl.BlockSpec((k_taps, tc), lambda j, b: (0, j)),
            pl.BlockSpec((seq, tc), lambda j, b: (b, j)),
        ] + [part_spec(st, wd) for st, wd in zip(starts, widths)] + [ANY],
        out_specs=[
            pl.BlockSpec((seq, tc), lambda j, b: (b, j + off)),
            pl.BlockSpec((k_taps, tc), lambda j, b: (0, j)),
            pl.BlockSpec((1, tc), lambda j, b: (0, j)),
        ],
        out_shape=[
            jax.ShapeDtypeStruct(dproj.shape, BF16),
            jax.ShapeDtypeStruct((k_taps, n_cols), F32),
            jax.ShapeDtypeStruct((1, n_cols), F32),
        ],
        input_output_aliases={3 + n_parts: 0},
        compiler_params=_params(("parallel", "arbitrary")),
    )(proj, conv_w, pre, *dparts, dproj)


def _fill_cols(buf, src, col0, *, name):
    t, c = src.shape
    tm = _pick(t, (1024, 512, 256, 128))

    def body(s_ref, b_ref, o_ref):
        o_ref[...] = s_ref[...].astype(o_ref.dtype)

    return pl.pallas_call(
        body,
        name=name,
        grid=(t // tm,),
        in_specs=[pl.BlockSpec((tm, c), lambda i: (i, 0)), ANY],
        out_specs=pl.BlockSpec((tm, c), lambda i: (i, col0 // c)),
        out_shape=jax.ShapeDtypeStruct(buf.shape, buf.dtype),
        input_output_aliases={1: 0},
        compiler_params=_params(("parallel",)),
    )(src, buf)


def _softplus(x):
    return jnp.maximum(x, 0.0) + jnp.log(1.0 + jnp.exp(-jnp.abs(x)))


def _chunk_decay(dtraw, bias, alog):
    q = dtraw.shape[0]
    dt = _softplus(dtraw + bias)
    a = -jnp.exp(alog)
    rows = lax.broadcasted_iota(jnp.int32, (q, q), 0)
    cols = lax.broadcasted_iota(jnp.int32, (q, q), 1)
    lower = rows >= cols
    acum = jnp.dot(lower.astype(F32), dt * a, precision=lax.Precision.HIGHEST, preferred_element_type=F32)
    return dt, a, acum, acum.T, lower


def _dot_exact(v, sel, terms=3):
    hi = v.astype(BF16)
    r1 = v - hi.astype(F32)
    mid = r1.astype(BF16)
    out = jnp.dot(hi, sel, preferred_element_type=F32) + jnp.dot(mid, sel, preferred_element_type=F32)
    if terms == 3:
        lo = (r1 - mid.astype(F32)).astype(BF16)
        out = out + jnp.dot(lo, sel, preferred_element_type=F32)
    return out


def _head_selectors(gw, p):
    sum_heads = (lax.broadcasted_iota(jnp.int32, (gw, LANE), 0) // p == lax.broadcasted_iota(jnp.int32, (gw, LANE), 1))
    spread = (lax.broadcasted_iota(jnp.int32, (LANE, gw), 0) == lax.broadcasted_iota(jnp.int32, (LANE, gw), 1) // p)
    return sum_heads.astype(BF16), spread.astype(BF16)


def _row_spread(v, spread):
    return _dot_exact(jnp.broadcast_to(v, (8, v.shape[1])), spread)[0:1, :]


def _head_pad(v, r_heads):
    lead = v.shape[:-1]
    vg = v.reshape(lead + (N_SSD_GROUPS, r_heads))
    vg = jnp.pad(vg, [(0, 0)] * len(lead) + [(0, 0), (0, LANE - r_heads)])
    out = vg.reshape(lead + (N_SSD_GROUPS * LANE,))
    return out[None] if out.ndim == 1 else out


def _head_unpad(v, r_heads):
    lead = v.shape[:-1]
    out = v.reshape(lead + (N_SSD_GROUPS, LANE))[..., :r_heads].reshape(lead + (N_SSD_GROUPS * r_heads,))
    return out[0] if (len(lead) == 1 and lead[0] == 1) else out


def _ssd_w_in_layout(w_in, d_inner, d_xbc, r_heads):
    main = w_in[:, :d_inner + d_xbc]
    return jnp.concatenate([main, _head_pad(w_in[:, d_inner + d_xbc:], r_heads)], axis=1)


def _ssd_w_in_unlayout(w, d_inner, d_xbc, r_heads):
    main = w[:, :d_inner + d_xbc]
    return jnp.concatenate([main, _head_unpad(w[:, d_inner + d_xbc:], r_heads)], axis=1)


SSD_CHUNKS = 8
SSD_CHUNKS_FWD = 8


def _ssd_dims(proj, xbc):
    d_xbc = xbc.shape[1]
    d_inner = d_xbc - 2 * N_SSD_GROUPS * D_STATE
    gw = d_inner // N_SSD_GROUPS
    return d_inner, d_xbc, gw, gw // HEAD_DIM


def _ssd_fwd(proj, xbc, bias_p, alog_p, dskip_p, norm_w, n_seq, *, name):
    t = proj.shape[0]
    d_inner, d_xbc, gw, r_heads = _ssd_dims(proj, xbc)
    q, n, n_g, p = CHUNK, D_STATE, N_SSD_GROUPS, HEAD_DIM
    seq = t // n_seq
    nc = seq // q
    cps = SSD_CHUNKS_FWD if nc % SSD_CHUNKS_FWD == 0 else 1
    dt_blk0 = (d_inner + d_xbc) // LANE

    def body(x_ref, b_ref, c_ref, z_ref, dtr_ref, bias_ref, alog_ref, dsk_ref, nw_ref, yn_ref, y_ref, hs_ref, h_scr):
        @pl.when(pl.program_id(2) == 0)
        def _():
            h_scr[...] = jnp.zeros_like(h_scr)

        for cc in range(cps):
            rows = pl.ds(cc * q, q)
            chunk(x_ref.at[rows, :], b_ref.at[rows, :], c_ref.at[rows, :], z_ref.at[rows, :], dtr_ref.at[rows, :],
                  bias_ref, alog_ref, dsk_ref, nw_ref, yn_ref.at[rows, :], y_ref.at[rows, :],
                  hs_ref.at[pl.ds(cc * n, n), :], h_scr)

    def chunk(x_ref, b_ref, c_ref, z_ref, dtr_ref, bias_ref, alog_ref, dsk_ref, nw_ref, yn_ref, y_ref, hs_ref, h_scr):
        dt, a, acum, acum_t, lower = _chunk_decay(dtr_ref[...], bias_ref[...], alog_ref[...])
        x = x_ref[...]
        bb = b_ref[...].astype(BF16)
        cb = c_ref[...].astype(BF16)
        g_mat = lax.dot_general(cb, bb, (((1,), (1,)), ((), ())), preferred_element_type=F32)
        h_prev = h_scr[...]
        hs_ref[...] = h_prev
        c_h = jnp.dot(cb, h_prev.astype(BF16), preferred_element_type=F32)
        _, spread = _head_selectors(gw, p)
        acum_s = _dot_exact(acum, spread)
        a_last_s = acum_s[q - 1:q, :]
        xdt = x * _dot_exact(dt, spread, 2)
        xdt_b = xdt.astype(BF16)
        ys = []
        for h in range(r_heads):
            decay = jnp.exp(jnp.where(lower, acum[:, h:h + 1] - acum_t[h:h + 1, :], -jnp.inf))
            ys.append(jnp.dot((g_mat * decay).astype(BF16), xdt_b[:, h * p:(h + 1) * p], preferred_element_type=F32))
        y = jnp.concatenate(ys, axis=1) + jnp.exp(acum_s) * c_h + _row_spread(dsk_ref[...], spread) * x
        xd = xdt * jnp.exp(a_last_s - acum_s)
        states = lax.dot_general(bb, xd.astype(BF16), (((0,), (0,)), ((), ())), preferred_element_type=F32)
        h_scr[...] = h_prev * jnp.exp(a_last_s) + states
        y_ref[...] = y
        gated = y * _silu(z_ref[...])
        rstd = lax.rsqrt(jnp.mean(gated * gated, axis=-1, keepdims=True) + EPS)
        yn_ref[...] = (gated * rstd * nw_ref[...]).astype(BF16)

    row = lambda b, g, c: b * (nc // cps) + c
    vec = pl.BlockSpec((1, LANE), lambda b, g, c: (0, g))
    return pl.pallas_call(
        body,
        name=name,
        grid=(n_seq, n_g, nc // cps),
        in_specs=[
            pl.BlockSpec((q * cps, gw), lambda b, g, c: (row(b, g, c), g)),
            pl.BlockSpec((q * cps, n), lambda b, g, c: (row(b, g, c), d_inner // n + g)),
            pl.BlockSpec((q * cps, n), lambda b, g, c: (row(b, g, c), d_inner // n + n_g + g)),
            pl.BlockSpec((q * cps, gw), lambda b, g, c: (row(b, g, c), g)),
            pl.BlockSpec((q * cps, LANE), lambda b, g, c: (row(b, g, c), dt_blk0 + g)),
            vec, vec, vec,
            pl.BlockSpec((1, gw), lambda b, g, c: (0, g)),
        ],
        out_specs=[
            pl.BlockSpec((q * cps, gw), lambda b, g, c: (row(b, g, c), g)),
            pl.BlockSpec((q * cps, gw), lambda b, g, c: (row(b, g, c), g)),
            pl.BlockSpec((n * cps, gw), lambda b, g, c: (row(b, g, c), g)),
        ],
        out_shape=[
            jax.ShapeDtypeStruct((t, d_inner), BF16),
            jax.ShapeDtypeStruct((t, d_inner), F32),
            jax.ShapeDtypeStruct((n_seq * nc * n, d_inner), F32),
        ],
        scratch_shapes=[pltpu.VMEM((n, gw), F32)],
        compiler_params=_params(("parallel", "parallel", "arbitrary")),
    )(xbc, xbc, xbc, proj, proj, bias_p, alog_p, dskip_p, norm_w)


def _ssd_bwd(proj, xbc, hs, y, dyn, bias_p, alog_p, dskip_p, norm_w, n_seq, *, name):
    t = proj.shape[0]
    d_inner, d_xbc, gw, r_heads = _ssd_dims(proj, xbc)
    q, n, n_g, p = CHUNK, D_STATE, N_SSD_GROUPS, HEAD_DIM
    seq = t // n_seq
    nc = seq // q
    cps = SSD_CHUNKS if nc % SSD_CHUNKS == 0 else 1
    dt_blk0 = (d_inner + d_xbc) // LANE

    def body(x_ref, b_ref, c_ref, z_ref, dtr_ref, bias_ref, alog_ref, dsk_ref, nw_ref, hs_ref, y_ref, dyn_ref,
             dx_ref, db_ref, dc_ref, dz_ref, ddtr_ref, dnw_ref, dbias_ref, dalog_ref, ddsk_ref, dh_scr):
        @pl.when(pl.program_id(2) == 0)
        def _():
            dh_scr[...] = jnp.zeros_like(dh_scr)

        first_step = jnp.logical_and(pl.program_id(1) == 0, pl.program_id(2) == 0)
        for cc in reversed(range(cps)):
            rows = pl.ds(cc * q, q)
            chunk(jnp.logical_and(first_step, cc == cps - 1), x_ref.at[rows, :], b_ref.at[rows, :], c_ref.at[rows, :],
                  z_ref.at[rows, :], dtr_ref.at[rows, :], bias_ref, alog_ref, dsk_ref, nw_ref,
                  hs_ref.at[pl.ds(cc * n, n), :], y_ref.at[rows, :], dyn_ref.at[rows, :], dx_ref.at[rows, :],
                  db_ref.at[rows, :], dc_ref.at[rows, :], dz_ref.at[rows, :], ddtr_ref.at[rows, :],
                  dnw_ref, dbias_ref, dalog_ref, ddsk_ref, dh_scr)

    def chunk(first, x_ref, b_ref, c_ref, z_ref, dtr_ref, bias_ref, alog_ref, dsk_ref, nw_ref, hs_ref, y_ref, dyn_ref,
              dx_ref, db_ref, dc_ref, dz_ref, ddtr_ref, dnw_ref, dbias_ref, dalog_ref, ddsk_ref, dh_scr):
        dtraw = dtr_ref[...]
        dt, a, acum, acum_t, lower = _chunk_decay(dtraw, bias_ref[...], alog_ref[...])
        x = x_ref[...]
        bb = b_ref[...].astype(BF16)
        cb = c_ref[...].astype(BF16)
        g_mat = lax.dot_general(cb, bb, (((1,), (1,)), ((), ())), preferred_element_type=F32)

        yv = y_ref[...]
        z = z_ref[...]
        sz = _silu(z)
        gated = yv * sz
        rstd = lax.rsqrt(jnp.mean(gated * gated, axis=-1, keepdims=True) + EPS)
        gn = gated * rstd
        dynv = dyn_ref[...]
        gwt = dynv * nw_ref[...]
        dgated = rstd * (gwt - gn * jnp.mean(gwt * gn, axis=-1, keepdims=True))
        dnw = jnp.sum(dynv * gn, axis=0, keepdims=True)
        dy = dgated * sz
        dz_ref[...] = (dgated * yv * _dsilu(z)).astype(BF16)

        h_prev = hs_ref[...]
        h_prev_b = h_prev.astype(BF16)
        ds = dh_scr[...]
        ds_b = ds.astype(BF16)
        sum_heads, spread = _head_selectors(gw, p)
        acum_s = _dot_exact(acum, spread)
        a_last_s = acum_s[q - 1:q, :]
        dt_s = _dot_exact(dt, spread, 2)
        dsk_s = _row_spread(dsk_ref[...], spread)
        dte_s = jnp.exp(a_last_s - acum_s)
        cd_s = jnp.exp(a_last_s)
        xdt = x * dt_s
        xdt_b = xdt.astype(BF16)
        dy_b = dy.astype(BF16)
        gt_mat = lax.dot_general(bb, cb, (((1,), (1,)), ((), ())), preferred_element_type=F32)
        upper = lax.broadcasted_iota(jnp.int32, (q, q), 0) <= lax.broadcasted_iota(jnp.int32, (q, q), 1)
        dg = jnp.zeros((q, q), F32)
        dxdts, w_diffs = [], []
        for h in range(r_heads):
            hsl = slice(h * p, (h + 1) * p)
            diff = acum[:, h:h + 1] - acum_t[h:h + 1, :]
            decay = jnp.exp(jnp.where(lower, diff, -jnp.inf))
            decay_t = jnp.exp(jnp.where(upper, -diff, -jnp.inf))
            mt_mat = gt_mat * decay_t
            dm = lax.dot_general(dy_b[:, hsl], xdt_b[:, hsl], (((1,), (1,)), ((), ())), preferred_element_type=F32)
            dm_t = lax.dot_general(xdt_b[:, hsl], dy_b[:, hsl], (((1,), (1,)), ((), ())), preferred_element_type=F32)
            dg = dg + dm * decay
            dxdts.append(jnp.dot(mt_mat.astype(BF16), dy_b[:, hsl], preferred_element_type=F32))
            w_diffs.append(dm * (g_mat * decay) - dm_t * mt_mat)
        sel_q = (lax.broadcasted_iota(jnp.int32, (r_heads * q, LANE), 0) // q
                 == lax.broadcasted_iota(jnp.int32, (r_heads * q, LANE), 1)).astype(BF16)
        dacum_diag = _dot_exact(jnp.concatenate(w_diffs, axis=1), sel_q, 2)
        c_h = jnp.dot(cb, h_prev_b, preferred_element_type=F32)
        dxd = jnp.dot(bb, ds_b, preferred_element_type=F32)
        dxdt = jnp.concatenate(dxdts, axis=1) + dxd * dte_s
        dye = dy * jnp.exp(acum_s)
        dye_b = dye.astype(BF16)
        xd = xdt * dte_s
        xd_b = xd.astype(BF16)
        dg_b = dg.astype(BF16)
        dx_ref[...] = dxdt * dt_s + dsk_s * dy
        dc_ref[...] = (jnp.dot(dg_b, bb, preferred_element_type=F32)
                       + lax.dot_general(dye_b, h_prev_b, (((1,), (1,)), ((), ())), preferred_element_type=F32))
        db_ref[...] = (lax.dot_general(dg_b, cb, (((0,), (0,)), ((), ())), preferred_element_type=F32)
                       + lax.dot_general(xd_b, ds_b, (((1,), (1,)), ((), ())), preferred_element_type=F32))
        dh_scr[...] = ds * cd_s + lax.dot_general(cb, dye_b, (((0,), (0,)), ((), ())), preferred_element_type=F32)
        ddt_cols = _dot_exact(x * dxdt, sum_heads, 2)
        dacum_y = _dot_exact(dye * c_h - dxd * xd, sum_heads, 2)
        col_sums = jnp.concatenate([
            jnp.sum(dxd * xd, axis=0, keepdims=True) + jnp.sum(ds * h_prev, axis=0, keepdims=True) * cd_s,
            jnp.sum(dy * x, axis=0, keepdims=True),
            jnp.zeros((6, gw), F32)], axis=0)
        col_sums = _dot_exact(col_sums, sum_heads, 2)
        ddsk = col_sums[1:2, :]
        rows_q = lax.broadcasted_iota(jnp.int32, (q, LANE), 0)
        dacum = dacum_diag + dacum_y + jnp.where(rows_q == q - 1, col_sums[0:1, :], 0.0)
        dadt = jnp.dot(upper.astype(F32), dacum, precision=lax.Precision.HIGHEST, preferred_element_type=F32)
        ddt = dadt * a + ddt_cols
        ddtr = ddt * _sigmoid(dtraw + bias_ref[...])
        ddtr_ref[...] = ddtr
        dbias = jnp.sum(ddtr, axis=0, keepdims=True)
        dalog = jnp.sum(dadt * dt, axis=0, keepdims=True) * a

        @pl.when(first)
        def _():
            dnw_ref[...] = dnw
            dbias_ref[...] = dbias
            dalog_ref[...] = dalog
            ddsk_ref[...] = ddsk

        @pl.when(jnp.logical_not(first))
        def _():
            dnw_ref[...] += dnw
            dbias_ref[...] += dbias
            dalog_ref[...] += dalog
            ddsk_ref[...] += ddsk

    row = lambda g, b, c: b * (nc // cps) + (nc // cps - 1 - c)
    vec = pl.BlockSpec((1, LANE), lambda g, b, c: (0, g))
    wide = pl.BlockSpec((q * cps, gw), lambda g, b, c: (row(g, b, c), g))
    narrow = pl.BlockSpec((q * cps, n), lambda g, b, c: (row(g, b, c), g))
    return pl.pallas_call(
        body,
        name=name,
        grid=(n_g, n_seq, nc // cps),
        in_specs=[
            wide,
            pl.BlockSpec((q * cps, n), lambda g, b, c: (row(g, b, c), d_inner // n + g)),
            pl.BlockSpec((q * cps, n), lambda g, b, c: (row(g, b, c), d_inner // n + n_g + g)),
            wide,
            pl.BlockSpec((q * cps, LANE), lambda g, b, c: (row(g, b, c), dt_blk0 + g)),
            vec, vec, vec,
            pl.BlockSpec((1, gw), lambda g, b, c: (0, g)),
            pl.BlockSpec((n * cps, gw), lambda g, b, c: (row(g, b, c), g)),
            wide, wide,
        ],
        out_specs=[
            wide, narrow, narrow, wide, narrow,
            pl.BlockSpec((1, gw), lambda g, b, c: (0, g)),
            vec, vec, vec,
        ],
        out_shape=[
            jax.ShapeDtypeStruct((t, d_inner), F32),
            jax.ShapeDtypeStruct((t, n_g * n), F32),
            jax.ShapeDtypeStruct((t, n_g * n), F32),
            jax.ShapeDtypeStruct(proj.shape, BF16),
            jax.ShapeDtypeStruct((t, n_g * LANE), F32),
            jax.ShapeDtypeStruct((1, d_inner), F32),
            jax.ShapeDtypeStruct((1, n_g * LANE), F32),
            jax.ShapeDtypeStruct((1, n_g * LANE), F32),
            jax.ShapeDtypeStruct((1, n_g * LANE), F32),
        ],
        scratch_shapes=[pltpu.VMEM((n, gw), F32)],
        compiler_params=_params(("parallel", "arbitrary", "arbitrary")),
    )(xbc, xbc, xbc, proj, proj, bias_p, alog_p, dskip_p, norm_w, hs, y, dyn)


MESH_IDS = pl.DeviceIdType.MESH


def _my_index():
    return 4 * lax.axis_index("x") + 2 * lax.axis_index("y") + lax.axis_index("c")


def _all_gather(shard, *, name):
    def body(x_ref, out_ref, send_sems, recv_sems, local_sem):
        x, y, c = lax.axis_index("x"), lax.axis_index("y"), lax.axis_index("c")
        me, sibling = (x, y, c), (x, y, 1 - c)
        chips = [(1 - x, y), (x, 1 - y), (1 - x, 1 - y)]

        def blk(px, py, pc):
            return out_ref.at[4 * px + 2 * py + pc]

        def copy(k, block, to, src=None):
            return pltpu.make_async_remote_copy(
                src_ref=blk(*block) if src is None else src, dst_ref=blk(*block),
                send_sem=send_sems.at[k], recv_sem=recv_sems.at[k], device_id=to, device_id_type=MESH_IDS)

        mine = pltpu.make_async_copy(x_ref, blk(*me), local_sem)
        mine.start()
        first = [copy(0, me, sibling, src=x_ref)]
        first += [copy(1 + j, me, (*chip, c), src=x_ref) for j, chip in enumerate(chips)]
        for cp in first:
            cp.start()
        passed = [copy(4 + j, (*chip, c), sibling) for j, chip in enumerate(chips)]
        for j, chip in enumerate(chips):
            copy(1 + j, (*chip, c), me).wait_recv()
            passed[j].start()
        copy(0, sibling, me).wait_recv()
        for j, chip in enumerate(chips):
            copy(4 + j, (*chip, 1 - c), me).wait_recv()
        for cp in first + passed:
            cp.wait_send()
        mine.wait()

    return pl.pallas_call(
        body,
        name=name,
        in_specs=[ANY],
        out_specs=ANY,
        out_shape=jax.ShapeDtypeStruct((N_DEV,) + shard.shape, shard.dtype),
        scratch_shapes=[pltpu.SemaphoreType.DMA((7,)), pltpu.SemaphoreType.DMA((7,)), pltpu.SemaphoreType.DMA],
    )(shard)


HBM_SPEC = pl.BlockSpec(memory_space=pltpu.HBM)
SEM_SPEC = pl.BlockSpec(memory_space=pltpu.SEMAPHORE)
SPLIT_COPY_PARAMS = pltpu.CompilerParams(has_side_effects=pltpu.SideEffectType.DATAFLOW_SIDE_EFFECTING)


def _peer_list():
    x, y, c = lax.axis_index("x"), lax.axis_index("y"), lax.axis_index("c")
    peers = []
    for k in range(1, N_DEV):
        px = 1 - x if k & 4 else x
        py = 1 - y if k & 2 else y
        pc = 1 - c if k & 1 else c
        peers.append(((px, py, pc), 4 * px + 2 * py + pc))
    return 4 * x + 2 * y + c, peers


def _push_copies(src_refs, land_refs, send_sems, recv_sems, blockwise):
    me, peers = _peer_list()
    copies = []
    for a, (src_ref, land_ref) in enumerate(zip(src_refs, land_refs)):
        for k, (dev, idx) in enumerate(peers):
            sem = a * (N_DEV - 1) + k
            src = src_ref.at[idx] if blockwise else src_ref
            copies.append(tuple(
                pltpu.make_async_remote_copy(src_ref=src, dst_ref=land_ref.at[slot], send_sem=send_sems.at[sem],
                                             recv_sem=recv_sems.at[sem], device_id=dev, device_id_type=MESH_IDS)
                for slot in (me, idx)))
    return copies


def _push_start(srcs, blockwise, after, *, name):
    n = len(srcs)
    blocks = [s_.shape[1:] if blockwise else s_.shape for s_ in srcs]

    def body(*refs):
        src_refs, land_refs = refs[:n], refs[n:2 * n]
        send_sems, recv_sems = refs[2 * n + 1], refs[2 * n + 2]
        token = refs[-1]
        for send, _ in _push_copies(src_refs, land_refs, send_sems, recv_sems, blockwise):
            send.start()
        token[...] = jnp.zeros_like(token)

    n_sem = n * (N_DEV - 1)
    lands = [lax.empty((N_DEV,) + b, s_.dtype) for b, s_ in zip(blocks, srcs)]
    out = pl.pallas_call(
        body,
        name=name,
        in_specs=[HBM_SPEC] * (2 * n) + [ANY],
        out_specs=(SEM_SPEC, SEM_SPEC) + (HBM_SPEC,) * (2 * n) + (pl.BlockSpec(memory_space=pltpu.VMEM),),
        out_shape=(pltpu.SemaphoreType.DMA((n_sem,)), pltpu.SemaphoreType.DMA((n_sem,)))
        + tuple(pltpu.HBM(a.shape, a.dtype) for a in list(srcs) + lands)
        + (jax.ShapeDtypeStruct((8, LANE), F32),),
        input_output_aliases={i: 2 + i for i in range(2 * n)},
        compiler_params=SPLIT_COPY_PARAMS,
    )(*[pltpu.with_memory_space_constraint(a, pltpu.HBM) for a in list(srcs) + lands], after)
    return out[0], out[1], out[2:2 + n], out[2 + n:2 + 2 * n], out[-1]


def _push_wait(send_sems, recv_sems, srcs, lands, blockwise, after, *, name):
    n = len(srcs)

    def body(*refs):
        src_refs, land_refs = refs[:n], refs[n:2 * n]
        send_sems, recv_sems = refs[2 * n], refs[2 * n + 1]
        for send, recv in _push_copies(src_refs, land_refs, send_sems, recv_sems, blockwise):
            send.wait_send()
            recv.wait_recv()

    out = pl.pallas_call(
        body,
        name=name,
        in_specs=[HBM_SPEC] * (2 * n) + [SEM_SPEC, SEM_SPEC, ANY],
        out_specs=(HBM_SPEC,) * (2 * n),
        out_shape=tuple(pltpu.HBM(a.shape, a.dtype) for a in list(srcs) + list(lands)),
        input_output_aliases={i: i for i in range(2 * n)},
        compiler_params=SPLIT_COPY_PARAMS,
    )(*srcs, *lands, send_sems, recv_sems, after)
    return out[n:]


def _with_own_slot(landing, own):
    slot = lax.broadcasted_iota(jnp.int32, (N_DEV,) + (1,) * own.ndim, 0)
    return jnp.where(slot == _my_index(), own[None], landing)


def _sum_slots(parts, own=None, *, name):
    shape = parts.shape[1:]
    n, c = parts.shape[0], parts.shape[-1]
    r = parts.size // (n * c)
    tm = _pick(r, (256, 128, 64, 32, 16, 8))

    def body(p_ref, *rest):
        o_ref = rest[-1]
        me = _my_index()

        def slot(s):
            if own is None:
                return p_ref[s].astype(F32)
            return jnp.where(me == s, rest[0][...], p_ref[s]).astype(F32)

        acc = slot(0)
        for s in range(1, n):
            acc = acc + slot(s)
        o_ref[...] = acc

    tile = pl.BlockSpec((tm, c), lambda i: (i, 0))
    return pl.pallas_call(
        body,
        name=name,
        grid=(r // tm,),
        in_specs=[pl.BlockSpec((n, tm, c), lambda i: (0, i, 0))] + ([] if own is None else [tile]),
        out_specs=tile,
        out_shape=jax.ShapeDtypeStruct((r, c), F32),
        compiler_params=_params(("parallel",)),
    )(parts.reshape(n, r, c), *([] if own is None else [own.reshape(r, c)])).reshape(shape)


def _row_count(shape):
    c = shape[-1]
    rows = 1
    for s in shape[:-1]:
        rows *= s
    return rows, c, c + (-c) % LANE


PACK_ROWS = 256


def _pack_rows(arrays):
    pieces = []
    for a in arrays:
        rows, c, cp = _row_count(a.shape)
        a2 = a.reshape(rows, c)
        if cp > c:
            a2 = jnp.pad(a2, ((0, 0), (0, cp - c)))
        a2 = a2.reshape(rows * cp // LANE, LANE)
        if a2.shape[0] % 8:
            a2 = jnp.pad(a2, ((0, 8 - a2.shape[0] % 8), (0, 0)))
        pieces.append(a2)
    total = sum(p.shape[0] for p in pieces)
    if total % PACK_ROWS:
        pieces.append(jnp.zeros((PACK_ROWS - total % PACK_ROWS, LANE), F32))
    return jnp.concatenate(pieces, axis=0)


def _unpack_rows(packed, shapes, lead=()):
    out, off = [], 0
    for shp in shapes:
        rows, c, cp = _row_count(shp)
        n_rows = rows * cp // LANE
        seg = packed[..., off:off + n_rows, :].reshape(lead + (rows, cp))
        out.append(seg[..., :c].reshape(lead + tuple(shp)))
        off += n_rows + (-n_rows) % 8
    return out


def _unshard(stacked, axis):
    if axis == stacked.ndim - 2:
        return jnp.concatenate([stacked[d] for d in range(N_DEV)], axis=axis)
    moved = jnp.moveaxis(stacked, 0, axis)
    shp = moved.shape
    return moved.reshape(shp[:axis] + (shp[axis] * shp[axis + 1],) + shp[axis + 2:])


def _shard_major(full, axis):
    shp = full.shape
    if axis == full.ndim - 1:
        size = shp[axis] // N_DEV
        return jnp.stack([full[..., d * size:(d + 1) * size] for d in range(N_DEV)])
    split = full.reshape(shp[:axis] + (N_DEV, shp[axis] // N_DEV) + shp[axis + 1:])
    return jnp.moveaxis(split, axis, 0)


def _my_shard(full, axis):
    size = full.shape[axis] // N_DEV
    return lax.dynamic_slice_in_dim(full, _my_index() * size, size, axis)


def _local_step(x, target, w, fetch, emit, n_seq):
    def with_token(vec, token):
        return vec + jnp.tile(token[0:1, :], (1, vec.shape[1] // LANE))

    depth, d_model = w["norm_mix_pre"].shape
    d_inner = w["ssd_norm_w"].shape[1]
    d_xbc = w["ssd_conv_w"].shape[2]
    saved = []
    m, token = fetch(0, "mix", x)
    mix_pre_w = with_token(w["norm_mix_pre"][0:1], token)
    u = _rms_fwd(x, mix_pre_w, out_dtype=BF16, name="l0_mix_pre")
    for i in range(depth):
        j = i // 2
        s = {"x": x, "mix_pre_w": mix_pre_w}
        if i % 2 == 0:
            proj = _mm(u, m["ssd_w_in"], name=f"l{i}_ssd_in")
            xbc, xbc_pre = _ssd_conv_fwd(proj, d_inner, d_xbc, w["ssd_conv_w"][j], w["ssd_conv_b"][j:j + 1], n_seq,
                                         name=f"l{i}_ssd_conv")
            yn, y, hs = _ssd_fwd(proj, xbc, w["ssd_dt_bias"][j:j + 1], w["ssd_a_log"][j:j + 1], w["ssd_d"][j:j + 1],
                                 w["ssd_norm_w"][j:j + 1], n_seq, name=f"l{i}_ssd_scan")
            m_out, token = fetch(i, "out", yn)
            m = {**m, **m_out}
            mix = _mm(yn, m["ssd_w_out"], name=f"l{i}_ssd_out")
            s.update(u=u, proj=proj, xbc=xbc, xbc_pre=xbc_pre, yn=yn, y=y, hs=hs)
        else:
            mix = _pool_fwd(u, m["pool_w"], w["pool_scale"][j:j + 1], n_seq, name=f"l{i}_pool")
            s.update(u=u)
            token = jnp.zeros((8, LANE), F32)
        mix_post_w = with_token(w["norm_mix_post"][i:i + 1], token)
        m_ffn, token = fetch(i, "ffn", mix)
        m = {**m, **m_ffn}
        ffn_pre_w = with_token(w["norm_ffn_pre"][i:i + 1], token)
        x1, n = _res_rms_rms(x, mix, mix_post_w, ffn_pre_w, name=f"l{i}_mix_post_ffn_pre")
        h, hc, a = _ffn_up_act(n, m["ffn_w_up"], w["ffn_conv_w"][i], w["ffn_conv_b"][i:i + 1], n_seq,
                               name=f"l{i}_ffn_up_act")
        f = _mm(a, m["ffn_w_down"], name=f"l{i}_ffn_down")
        s.update(mix=mix, x1=x1, n=n, h=h, hc=hc, a=a, f=f, m=m, ffn_pre_w=ffn_pre_w)
        saved.append(s)
        if i + 1 < depth:
            m, token = fetch(i + 1, "mix", f)
            mix_pre_w = with_token(w["norm_mix_pre"][i + 1:i + 2], token)
            x, u = _res_rms_rms(x1, f, w["norm_ffn_post"][i:i + 1], mix_pre_w,
                                out_dtype=BF16 if (i + 1) % 2 == 0 else F32, name=f"l{i}_ffn_post_mix_pre")
        else:
            x = _res_rms_fwd(x1, f, w["norm_ffn_post"][i:i + 1], name=f"l{i}_ffn_post")

    loss, dx = _loss_head(x, target)
    grads = {k: [None] * len(w[k]) for k in SMALL}
    df, grads["norm_ffn_post"][depth - 1] = _rms_bwd(saved[-1]["f"], w["norm_ffn_post"][depth - 1:depth], dx, None,
                                                      out_dtype=BF16, name=f"l{depth - 1}_ffn_post_b")
    for i in reversed(range(depth)):
        j = i // 2
        s = saved[i]
        m, gm = s["m"], {}
        gm["ffn_w_down"] = _mm(s["a"], df, ta=True, name=f"l{i}_ffn_down_bw")
        dh, grads["ffn_conv_w"][i], grads["ffn_conv_b"][i] = _ffn_down_bx_act_bwd(
            df, m["ffn_w_down"], s["h"], s["hc"], w["ffn_conv_w"][i], n_seq, name=f"l{i}_ffn_act_b")
        dn = _mm(dh, m["ffn_w_up"], tb=True, name=f"l{i}_ffn_up_bx")
        gm["ffn_w_up"] = _mm(s["n"], dh, ta=True, name=f"l{i}_ffn_up_bw")
        token = emit(i, "ffn", gm, dn)
        gm = {}
        dx1, dmix, grads["norm_ffn_pre"][i], grads["norm_mix_post"][i] = _rms_bwd2(
            s["x1"], s["ffn_pre_w"], dn, dx, s["mix"], with_token(w["norm_mix_post"][i:i + 1], token),
            out_dtype=BF16 if i % 2 == 0 else F32, name=f"l{i}_ffn_pre_mix_post_b")
        if i % 2 == 0:
            dyn = _mm(dmix, m["ssd_w_out"], tb=True, name=f"l{i}_ssd_out_bx")
            gm["ssd_w_out"] = _mm(s["yn"], dmix, ta=True, name=f"l{i}_ssd_out_bw")
            token = emit(i, "out", gm, dyn)
            gm = {}
            dxs, db, dc, dz, ddtr, dnw, dbias, dalog, ddsk = _ssd_bwd(
                s["proj"], s["xbc"], s["hs"], s["y"], dyn, w["ssd_dt_bias"][j:j + 1], w["ssd_a_log"][j:j + 1],
                w["ssd_d"][j:j + 1], with_token(w["ssd_norm_w"][j:j + 1], token), n_seq, name=f"l{i}_ssd_scan_b")
            grads["ssd_norm_w"][j], grads["ssd_dt_bias"][j], grads["ssd_a_log"][j], grads["ssd_d"][j] = (
                dnw, dbias, dalog, ddsk)
            dproj, grads["ssd_conv_w"][j], grads["ssd_conv_b"][j] = _ssd_conv_bwd(
                s["proj"], d_inner, w["ssd_conv_w"][j], s["xbc_pre"], (dxs, db, dc), dz, n_seq,
                name=f"l{i}_ssd_conv_b")
            dproj = _fill_cols(dproj, ddtr, d_inner + d_xbc, name=f"l{i}_ssd_dt_b")
            gm["ssd_w_in"] = _mm(s["u"], dproj, ta=True, name=f"l{i}_ssd_in_bw")
            token = emit(i, "mix", gm, dproj)
            du = _mm(dproj, m["ssd_w_in"], tb=True, name=f"l{i}_ssd_in_bx")
        else:
            du, gm["pool_w"], grads["pool_scale"][j] = _pool_bwd(
                s["u"], m["pool_w"], w["pool_scale"][j:j + 1], dmix, n_seq, name=f"l{i}_pool_b")
            token = emit(i, "mix", gm, du)
        if i > 0:
            dx, df, grads["norm_mix_pre"][i], grads["norm_ffn_post"][i - 1] = _rms_bwd2(
                s["x"], with_token(s["mix_pre_w"], token), du, dx1, saved[i - 1]["f"], w["norm_ffn_post"][i - 1:i],
                out_dtype=BF16, name=f"l{i}_mix_pre_ffn_post_b")
        else:
            dx, grads["norm_mix_pre"][i] = _rms_bwd(s["x"], with_token(s["mix_pre_w"], token), du, dx1,
                                                    name=f"l{i}_mix_pre_b")
    return loss, dx, grads


BIG = (("ssd_w_in", 2), ("ssd_w_out", 1), ("pool_w", 2), ("ffn_w_up", 2), ("ffn_w_down", 1))
SMALL_SHARDED = (("ssd_conv_w", 2), ("ffn_conv_w", 2), ("pool_scale", 1))
SMALL = ("ssd_conv_w", "ssd_conv_b", "ssd_dt_bias", "ssd_a_log", "ssd_d", "ssd_norm_w", "pool_scale", "ffn_conv_w",
         "ffn_conv_b", "norm_mix_pre", "norm_mix_post", "norm_ffn_pre", "norm_ffn_post")
WEIGHTS = ("ssd_w_in", "ssd_conv_w", "ssd_conv_b", "ssd_dt_bias", "ssd_a_log", "ssd_d", "ssd_norm_w", "ssd_w_out",
           "pool_w", "pool_scale", "ffn_w_up", "ffn_conv_w", "ffn_conv_b", "ffn_w_down", "norm_mix_pre",
           "norm_mix_post", "norm_ffn_pre", "norm_ffn_post")


def _ssd_sizes(d_inner):
    return d_inner + 2 * N_SSD_GROUPS * D_STATE, d_inner // HEAD_DIM // N_SSD_GROUPS


def _small_compute_layout(full, d_inner):
    _, r_heads = _ssd_sizes(d_inner)
    w = {k: full[k] for k in SMALL}
    for k in ("ssd_dt_bias", "ssd_a_log", "ssd_d"):
        w[k] = _head_pad(full[k], r_heads)
    for k in ("ffn_conv_w", "ffn_conv_b"):
        w[k] = _interleave(full[k])
    return w


def _matmul_compute_layout(k, full, d_inner):
    d_xbc, r_heads = _ssd_sizes(d_inner)
    if k == "ssd_w_in":
        return _ssd_w_in_layout(full, d_inner, d_xbc, r_heads)
    if k == "ffn_w_up":
        return _interleave(full)
    return full


def _layer_matrices(i, part):
    if part == "ffn":
        return (("ffn_w_up", 1, i), ("ffn_w_down", 0, i))
    if i % 2 == 1:
        return (("pool_w", 1, i // 2),) if part == "mix" else ()
    return (("ssd_w_in", 1, i // 2),) if part == "mix" else (("ssd_w_out", 0, i // 2),)


def _fetch_group(i, part):
    mix, out, ffn = (_layer_matrices(i, p) for p in ("mix", "out", "ffn"))
    if i % 2 == 1:
        return mix + ffn if part == "mix" else ()
    if i == 0:
        return {"mix": mix, "out": out + ffn, "ffn": ()}[part]
    return {"mix": mix + out, "out": (), "ffn": ffn}[part]


def _matmul_grad_reference_layout(k, g, d_inner):
    d_xbc, r_heads = _ssd_sizes(d_inner)
    if k == "ssd_w_in":
        return _ssd_w_in_unlayout(g, d_inner, d_xbc, r_heads)
    if k == "ffn_w_up":
        return _deinterleave(g)
    return g


def _small_grads_reference_layout(grads, shapes, d_inner):
    _, r_heads = _ssd_sizes(d_inner)
    g = {k: jnp.stack(grads[k]) for k in SMALL}
    for k in ("ssd_dt_bias", "ssd_a_log", "ssd_d"):
        g[k] = _head_unpad(g[k][:, 0], r_heads)
    for k in ("ffn_conv_w", "ffn_conv_b"):
        g[k] = _deinterleave(g[k])
    return {k: v.reshape(shapes[k]) for k, v in g.items()}


def kernel(x, ssd_w_in, ssd_conv_w, ssd_conv_b, ssd_dt_bias, ssd_a_log, ssd_d, ssd_norm_w, ssd_w_out, pool_w, pool_scale, ffn_w_up, ffn_conv_w, ffn_conv_b, ffn_w_down, norm_mix_pre, norm_mix_post, norm_ffn_pre, norm_ffn_post, loss_target, m_ssd_w_in, m_ssd_conv_w, m_ssd_conv_b, m_ssd_dt_bias, m_ssd_a_log, m_ssd_d, m_ssd_norm_w, m_ssd_w_out, m_pool_w, m_pool_scale, m_ffn_w_up, m_ffn_conv_w, m_ffn_conv_b, m_ffn_w_down, m_norm_mix_pre, m_norm_mix_post, m_norm_ffn_pre, m_norm_ffn_post, v_ssd_w_in, v_ssd_conv_w, v_ssd_conv_b, v_ssd_dt_bias, v_ssd_a_log, v_ssd_d, v_ssd_norm_w, v_ssd_w_out, v_pool_w, v_pool_scale, v_ffn_w_up, v_ffn_conv_w, v_ffn_conv_b, v_ffn_w_down, v_norm_mix_pre, v_norm_mix_post, v_norm_ffn_pre, v_norm_ffn_post):
    shards = dict(ssd_w_in=ssd_w_in, ssd_conv_w=ssd_conv_w, ssd_conv_b=ssd_conv_b, ssd_dt_bias=ssd_dt_bias,
                  ssd_a_log=ssd_a_log, ssd_d=ssd_d, ssd_norm_w=ssd_norm_w, ssd_w_out=ssd_w_out, pool_w=pool_w,
                  pool_scale=pool_scale, ffn_w_up=ffn_w_up, ffn_conv_w=ffn_conv_w, ffn_conv_b=ffn_conv_b,
                  ffn_w_down=ffn_w_down, norm_mix_pre=norm_mix_pre, norm_mix_post=norm_mix_post,
                  norm_ffn_pre=norm_ffn_pre, norm_ffn_post=norm_ffn_post)
    moments_m = dict(zip(WEIGHTS, (m_ssd_w_in, m_ssd_conv_w, m_ssd_conv_b, m_ssd_dt_bias, m_ssd_a_log, m_ssd_d, m_ssd_norm_w, m_ssd_w_out, m_pool_w, m_pool_scale, m_ffn_w_up, m_ffn_conv_w, m_ffn_conv_b, m_ffn_w_down, m_norm_mix_pre, m_norm_mix_post, m_norm_ffn_pre, m_norm_ffn_post)))
    moments_v = dict(zip(WEIGHTS, (v_ssd_w_in, v_ssd_conv_w, v_ssd_conv_b, v_ssd_dt_bias, v_ssd_a_log, v_ssd_d, v_ssd_norm_w, v_ssd_w_out, v_pool_w, v_pool_scale, v_ffn_w_up, v_ffn_conv_w, v_ffn_conv_b, v_ffn_w_down, v_norm_mix_pre, v_norm_mix_post, v_norm_ffn_pre, v_norm_ffn_post)))
    n_seq, seq, d_model = x.shape
    t = n_seq * seq

    d_inner = ssd_norm_w.shape[1]
    depth = norm_mix_pre.shape[0]
    x2 = x.reshape(t, d_model)

    shard16 = {k: shards[k].astype(BF16) for k, _ in BIG}
    order = [(i, part) for i in range(depth) for part in ("mix", "out", "ffn") if _fetch_group(i, part)]
    fetches = {}

    def start_fetch(key, after):
        srcs = [shard16[k][l] for k, _, l in _fetch_group(*key)]
        fetches[key] = _push_start(srcs, False, after, name=f"fetch{key[0]}{key[1]}_start")

    full = dict(shards)
    small_all = _all_gather(_pack_rows([shards[k] for k, _ in SMALL_SHARDED]), name="gather_small_weights")
    small_stacked = _unpack_rows(small_all, [shards[k].shape for k, _ in SMALL_SHARDED], lead=(N_DEV,))
    for (k, axis), st in zip(SMALL_SHARDED, small_stacked):
        full[k] = _unshard(st, axis)
    w = _small_compute_layout(full, d_inner)
    ready = {}

    def fetch(i, part, x_now):
        key = (i, part)
        token = jnp.zeros((8, LANE), F32)
        if key in order:
            if key == order[0]:
                wholes = [_unshard(_all_gather(shard16[k][l], name=f"fetch0_{k}"), axis) for k, axis, l in _fetch_group(i, part)]
                nxt_after = wholes[0]
            else:
                send, recv, srcs, lands, _ = fetches[key]
                lands = _push_wait(send, recv, srcs, lands, False, x_now, name=f"fetch{i}{part}_wait")
                wholes = [_unshard(_with_own_slot(land, shard16[k][l]), axis)
                          for (k, axis, l), land in zip(_fetch_group(i, part), lands)]
                nxt_after = lands[0]
            for (k, _, l), whole in zip(_fetch_group(i, part), wholes):
                ready[k, l] = _matmul_compute_layout(k, whole, d_inner)
            nxt = order.index(key) + 1
            if nxt < len(order):
                start_fetch(order[nxt], nxt_after)
                token = fetches[order[nxt]][4]
        return {k: ready[k, l] for k, _, l in _layer_matrices(i, part)}, token

    g_layers = {}
    in_flight = []

    def finish_exchange(after):
        key, blocks, (send, recv, srcs, lands, _) = in_flight.pop(0)
        lands = _push_wait(send, recv, srcs, lands, True, after, name=f"exchange{key[0]}{key[1]}_wait")
        for (k, _, l), land, block in zip(_layer_matrices(*key), lands, blocks):
            own = lax.dynamic_index_in_dim(block, _my_index(), 0, keepdims=False)
            g_layers[k, l] = _sum_slots(land, own, name=f"sum{key[0]}_{k}")

    def emit(i, part, gm, dx_now):
        if len(in_flight) >= 2:
            finish_exchange(dx_now)
        blocks = [_shard_major(_matmul_grad_reference_layout(k, gm[k].astype(BF16), d_inner), axis)
                  for k, axis, _ in _layer_matrices(i, part)]
        started = _push_start(blocks, True, dx_now, name=f"exchange{i}{part}_start")
        in_flight.append(((i, part), blocks, started))
        return started[4]

    loss, dx, grads = _local_step(x2, loss_target.reshape(t, d_model), w, fetch, emit, n_seq)
    loss = lax.psum(loss, ("x", "y", "c"))

    g_shard = {}
    small_shapes = {k: full[k].shape for k in SMALL}
    g_small = _small_grads_reference_layout(grads, small_shapes, d_inner)
    s_all = _all_gather(_pack_rows([g_small[k] for k in SMALL]) + in_flight[-1][2][4][0:1, :], name="gather_small_grads")
    for k, g in zip(SMALL, _unpack_rows(_sum_slots(s_all, name="sum_small_grads"), [small_shapes[k] for k in SMALL])):
        g_shard[k] = g
    for k, axis in SMALL_SHARDED:
        g_shard[k] = _my_shard(g_shard[k], axis)

    last = [k for key, _, _ in in_flight for k, _, _ in _layer_matrices(*key)]
    deltas, new_m, new_v = {}, {}, {}
    for k in [k for k in WEIGHTS if k not in last] + last:
        if k == last[0]:
            while in_flight:
                finish_exchange(deltas["ffn_w_up"])
        if k in dict(BIG):
            g_shard[k] = jnp.stack([g_layers[k, l] for l in range(shards[k].shape[0])])
        deltas[k], new_m[k], new_v[k] = _adamw(shards[k], g_shard[k], moments_m[k], moments_v[k], name=f"adamw_{k}")
    return (loss, dx.reshape(x.shape), *[g_shard[k] for k in WEIGHTS], *[deltas[k] for k in WEIGHTS],
            *[new_m[k] for k in WEIGHTS], *[new_v[k] for k in WEIGHTS])
```

```python
import functools

import jax
import jax.numpy as jnp
from jax import lax
from jax.experimental import pallas as pl
from jax.experimental.pallas import tpu as pltpu

F32 = jnp.float32
BF16 = jnp.bfloat16

N_DEV = 8
HEAD_DIM = 64
N_SSD_GROUPS = 4
D_STATE = 128
CHUNK = 128
POOL_WINDOWS = (2, 4, 8, 16)
EPS = 1e-6
LANE = 128
ADAM_LR = 0.001
ADAM_B1 = 0.9
ADAM_B2 = 0.999
ADAM_EPS = 1e-08
ADAM_WD = 0.01
ADAM_STEP = 10
VMEM_LIMIT = 56 * 1024 * 1024
ANY = pl.BlockSpec(memory_space=pl.ANY)


def _pick(n, cands):
    for c in cands:
        if n % c == 0:
            return c
    return n


def _params(sem):
    return pltpu.CompilerParams(dimension_semantics=sem, vmem_limit_bytes=VMEM_LIMIT)


def _sigmoid(x):
    return 0.5 * jnp.tanh(0.5 * x) + 0.5


def _silu(x):
    return x * _sigmoid(x)


def _dsilu(x):
    s = _sigmoid(x)
    return s * (1.0 + x * (1.0 - s))


def _shift_down(x, s):
    rows = lax.broadcasted_iota(jnp.int32, x.shape, 0)
    return jnp.where(rows >= s, pltpu.roll(x, s, 0), 0.0)


def _shift_up(x, s):
    n = x.shape[0]
    rows = lax.broadcasted_iota(jnp.int32, x.shape, 0)
    return jnp.where(rows < n - s, pltpu.roll(x, n - s, 0), 0.0)


MM_VMEM_BUDGET = 40 * 1024 * 1024
MM_STEP_BYTES = 1_300_000
MM_SUB = 512


def _mm_tiles(m, n, k, a_bytes, b_bytes, o_bytes):
    def cands(dim, sizes):
        out = [s for s in sizes if s <= dim and dim % s == 0]
        return out or [dim]

    best = None
    for tm in cands(m, (m, m // 2, 2048, 1024, 512, 256, 128)):
        if tm % LANE:
            continue
        for tn in cands(n, (n, n // 2, n // 4, 2048, 1024, 512, 256, 128)):
            if tn % (2 * LANE) and tn != n:
                continue
            for tk in cands(k, (k, k // 2, 2048, 1024, 512)):
                if tk % LANE:
                    continue
                nk = k // tk
                acc = tm * tn * 4 if (nk > 1 and o_bytes != 4) else 0
                temps = tm * min(tn, MM_SUB) * 4 + (tm * tk * 2 if a_bytes == 4 else 0) + (tk * tn * 2 if b_bytes == 4 else 0)
                vmem = 2 * (tm * tk * a_bytes + tk * tn * b_bytes + tm * tn * o_bytes) + acc + temps
                if vmem > MM_VMEM_BUDGET:
                    continue
                steps = (m // tm) * (n // tn) * nk
                acc_pass = steps * tm * tn * 2 if nk > 1 else 0
                cost = (m * k * a_bytes * (n // tn) + k * n * b_bytes * (m // tm) + m * n * o_bytes
                        + steps * MM_STEP_BYTES + acc_pass)
                if best is None or cost < best[0]:
                    best = (cost, tm, tn, tk)
    return best[1:]


def _mm(a, b, *, ta=False, tb=False, out_dtype=F32, name="mm"):
    m, k = (a.shape[1], a.shape[0]) if ta else a.shape
    n = b.shape[0] if tb else b.shape[1]
    o_bytes = jnp.dtype(out_dtype).itemsize
    tm, tn, tk = _mm_tiles(m, n, k, a.dtype.itemsize, b.dtype.itemsize, o_bytes)
    nk = k // tk
    sub = _pick(tn, (MM_SUB, 256))
    use_acc = nk > 1 and o_bytes != 4
    a_spec = pl.BlockSpec((tk, tm), lambda i, j, kk: (kk, i)) if ta else pl.BlockSpec((tm, tk), lambda i, j, kk: (i, kk))
    b_spec = pl.BlockSpec((tn, tk), lambda i, j, kk: (j, kk)) if tb else pl.BlockSpec((tk, tn), lambda i, j, kk: (kk, j))
    dims = (((1,), (1 if tb else 0,)), ((), ()))

    def body(a_ref, b_ref, o_ref, *scratch):
        kk = pl.program_id(2)
        acc_ref = scratch[0] if use_acc else o_ref
        if nk > 1:
            @pl.when(kk == 0)
            def _():
                acc_ref[...] = jnp.zeros_like(acc_ref)

        av = a_ref[...].astype(BF16)
        if ta:
            av = av.T
        for s in range(tn // sub):
            cols = slice(s * sub, (s + 1) * sub)
            bv = (b_ref[cols, :] if tb else b_ref[:, cols]).astype(BF16)
            part = lax.dot_general(av, bv, dims, preferred_element_type=F32)
            if nk == 1:
                o_ref[:, cols] = part.astype(out_dtype)
            else:
                acc_ref[:, cols] += part
        if use_acc:
            @pl.when(kk == nk - 1)
            def _():
                o_ref[...] = acc_ref[...].astype(out_dtype)

    return pl.pallas_call(
        body,
        name=name,
        grid=(m // tm, n // tn, nk),
        in_specs=[a_spec, b_spec],
        out_specs=pl.BlockSpec((tm, tn), lambda i, j, kk: (i, j)),
        out_shape=jax.ShapeDtypeStruct((m, n), out_dtype),
        scratch_shapes=[pltpu.VMEM((tm, tn), F32)] if use_acc else [],
        compiler_params=_params(("parallel", "parallel", "arbitrary")),
    )(a, b)


def _rms_fwd(x, w, *, out_dtype, name):
    t, d = x.shape
    tm = _pick(t, (512, 256, 128))

    def body(x_ref, w_ref, o_ref):
        xv = x_ref[...]
        rstd = lax.rsqrt(jnp.mean(xv * xv, axis=-1, keepdims=True) + EPS)
        o_ref[...] = (xv * rstd * w_ref[...]).astype(out_dtype)

    return pl.pallas_call(
        body,
        name=name,
        grid=(t // tm,),
        in_specs=[pl.BlockSpec((tm, d), lambda i: (i, 0)), pl.BlockSpec((1, d), lambda i: (0, 0))],
        out_specs=pl.BlockSpec((tm, d), lambda i: (i, 0)),
        out_shape=jax.ShapeDtypeStruct((t, d), out_dtype),
        compiler_params=_params(("parallel",)),
    )(x, w)


def _res_rms_fwd(x, f, w, *, name):
    t, d = x.shape
    tm = _pick(t, (512, 256, 128))

    def body(x_ref, f_ref, w_ref, o_ref):
        fv = f_ref[...]
        rstd = lax.rsqrt(jnp.mean(fv * fv, axis=-1, keepdims=True) + EPS)
        o_ref[...] = x_ref[...] + fv * rstd * w_ref[...]

    row = pl.BlockSpec((tm, d), lambda i: (i, 0))
    return pl.pallas_call(
        body,
        name=name,
        grid=(t // tm,),
        in_specs=[row, row, pl.BlockSpec((1, d), lambda i: (0, 0))],
        out_specs=row,
        out_shape=jax.ShapeDtypeStruct((t, d), F32),
        compiler_params=_params(("parallel",)),
    )(x, f, w)


def _rms_bwd(x, w, dy, resid, *, out_dtype=F32, name):
    t, d = x.shape
    tm = _pick(t, (512, 256, 128))
    has_res = resid is not None

    def body(*refs):
        if has_res:
            x_ref, w_ref, dy_ref, r_ref, dx_ref, dw_ref = refs
        else:
            x_ref, w_ref, dy_ref, dx_ref, dw_ref = refs
        xv = x_ref[...]
        dyv = dy_ref[...].astype(F32)
        rstd = lax.rsqrt(jnp.mean(xv * xv, axis=-1, keepdims=True) + EPS)
        xn = xv * rstd
        g = dyv * w_ref[...]
        dx = rstd * (g - xn * jnp.mean(g * xn, axis=-1, keepdims=True))
        if has_res:
            dx = dx + r_ref[...]
        dx_ref[...] = dx.astype(out_dtype)
        part = jnp.sum(dyv * xn, axis=0, keepdims=True)

        @pl.when(pl.program_id(0) == 0)
        def _():
            dw_ref[...] = part

        @pl.when(pl.program_id(0) > 0)
        def _():
            dw_ref[...] += part

    row = pl.BlockSpec((tm, d), lambda i: (i, 0))
    vec = pl.BlockSpec((1, d), lambda i: (0, 0))
    ins = [x, w, dy] + ([resid] if has_res else [])
    return pl.pallas_call(
        body,
        name=name,
        grid=(t // tm,),
        in_specs=[row, vec, row] + ([row] if has_res else []),
        out_specs=[row, vec],
        out_shape=[jax.ShapeDtypeStruct((t, d), out_dtype), jax.ShapeDtypeStruct((1, d), F32)],
        compiler_params=_params(("arbitrary",)),
    )(*ins)


def _res_rms_rms(x, f, w_post, w_pre, *, out_dtype=BF16, name):
    t, d = x.shape
    tm = _pick(t, (512, 256, 128))

    def body(x_ref, f_ref, wp_ref, wn_ref, x1_ref, n_ref):
        fv = f_ref[...]
        x1 = x_ref[...] + fv * lax.rsqrt(jnp.mean(fv * fv, axis=-1, keepdims=True) + EPS) * wp_ref[...]
        x1_ref[...] = x1
        n_ref[...] = (x1 * lax.rsqrt(jnp.mean(x1 * x1, axis=-1, keepdims=True) + EPS) * wn_ref[...]).astype(out_dtype)

    row = pl.BlockSpec((tm, d), lambda i: (i, 0))
    vec = pl.BlockSpec((1, d), lambda i: (0, 0))
    return pl.pallas_call(
        body,
        name=name,
        grid=(t // tm,),
        in_specs=[row, row, vec, vec],
        out_specs=[row, row],
        out_shape=[jax.ShapeDtypeStruct((t, d), F32), jax.ShapeDtypeStruct((t, d), out_dtype)],
        compiler_params=_params(("parallel",)),
    )(x, f, w_post, w_pre)


def _rms_bwd2(xa, wa, dya, resid, xb, wb, *, out_dtype, name):
    t, d = xa.shape
    tm = _pick(t, (512, 256, 128))

    def norm_bwd(xv, w, dyv):
        rstd = lax.rsqrt(jnp.mean(xv * xv, axis=-1, keepdims=True) + EPS)
        xn = xv * rstd
        g = dyv * w
        return rstd * (g - xn * jnp.mean(g * xn, axis=-1, keepdims=True)), jnp.sum(dyv * xn, axis=0, keepdims=True)

    def body(xa_ref, wa_ref, dya_ref, r_ref, xb_ref, wb_ref, da_ref, db_ref, dwa_ref, dwb_ref):
        da, dwa = norm_bwd(xa_ref[...], wa_ref[...], dya_ref[...].astype(F32))
        da = da + r_ref[...]
        da_ref[...] = da
        db, dwb = norm_bwd(xb_ref[...], wb_ref[...], da)
        db_ref[...] = db.astype(out_dtype)

        @pl.when(pl.program_id(0) == 0)
        def _():
            dwa_ref[...] = dwa
            dwb_ref[...] = dwb

        @pl.when(pl.program_id(0) > 0)
        def _():
            dwa_ref[...] += dwa
            dwb_ref[...] += dwb

    row = pl.BlockSpec((tm, d), lambda i: (i, 0))
    vec = pl.BlockSpec((1, d), lambda i: (0, 0))
    return pl.pallas_call(
        body,
        name=name,
        grid=(t // tm,),
        in_specs=[row, vec, row, row, row, vec],
        out_specs=[row, row, vec, vec],
        out_shape=[jax.ShapeDtypeStruct((t, d), F32), jax.ShapeDtypeStruct((t, d), out_dtype),
                   jax.ShapeDtypeStruct((1, d), F32), jax.ShapeDtypeStruct((1, d), F32)],
        compiler_params=_params(("arbitrary",)),
    )(xa, wa, dya, resid, xb, wb)


def _loss_head(y, target, *, name="loss_head"):
    t, d = y.shape
    tm = _pick(t, (512, 256, 128))

    def body(y_ref, t_ref, dy_ref, l_ref):
        err = y_ref[...] - t_ref[...]
        dy_ref[...] = err * (1.0 / d)
        part = jnp.sum(jnp.sum(err * err, axis=-1, keepdims=True), axis=0, keepdims=True) * (0.5 / d)
        part = jnp.broadcast_to(part, (1, LANE))

        @pl.when(pl.program_id(0) == 0)
        def _():
            l_ref[...] = part

        @pl.when(pl.program_id(0) > 0)
        def _():
            l_ref[...] += part

    row = pl.BlockSpec((tm, d), lambda i: (i, 0))
    dy, l = pl.pallas_call(
        body,
        name=name,
        grid=(t // tm,),
        in_specs=[row, row],
        out_specs=[row, pl.BlockSpec((1, LANE), lambda i: (0, 0))],
        out_shape=[jax.ShapeDtypeStruct((t, d), F32), jax.ShapeDtypeStruct((1, LANE), F32)],
        compiler_params=_params(("arbitrary",)),
    )(y, target)
    return l[0, 0], dy


def _conv_taps(h, w_ref, k_taps):
    out = h * w_ref[k_taps - 1:k_taps, :]
    for k in range(k_taps - 1):
        out = out + _shift_down(h, k_taps - 1 - k) * w_ref[k:k + 1, :]
    return out


def _conv_taps_bwd(h, dhc, w_ref, k_taps):
    dh = dhc * w_ref[k_taps - 1:k_taps, :]
    dws = []
    for k in range(k_taps - 1):
        up = _shift_up(dhc, k_taps - 1 - k)
        dh = dh + up * w_ref[k:k + 1, :]
        dws.append(jnp.sum(up * h, axis=0, keepdims=True))
    dws.append(jnp.sum(dhc * h, axis=0, keepdims=True))
    return dh, jnp.concatenate(dws, axis=0)


FFN_TC = 256


def _interleave(w, tc=FFN_TC):
    f = w.shape[-1] // 2
    tiles = []
    for j in range(f // tc):
        tiles += [w[..., j * tc:(j + 1) * tc], w[..., f + j * tc:f + (j + 1) * tc]]
    return jnp.concatenate(tiles, axis=-1)


def _deinterleave(w, tc=FFN_TC):
    n_tiles = w.shape[-1] // tc
    return jnp.concatenate([w[..., j * tc:(j + 1) * tc] for j in list(range(0, n_tiles, 2)) + list(range(1, n_tiles, 2))],
                           axis=-1)


FFN_ROWS = 512
HALO = 8


def _ffn_up_act(n, w_up, conv_w, conv_b, n_seq, *, name):
    t, d = n.shape
    f2 = w_up.shape[1]
    seq = t // n_seq
    tc = FFN_TC
    nj = f2 // (2 * tc)
    k_taps = conv_w.shape[0]
    rows = min(FFN_ROWS, seq)

    def body(n_ref, wu_ref, w_ref, b_ref, h_ref, hc_ref, o_ref, h_scr):
        h_scr[0:HALO, :] = jnp.zeros((HALO, 2 * tc), F32)
        wu = wu_ref[...]
        for r in range(seq // rows):
            chunk = slice(r * rows, (r + 1) * rows)
            h = jnp.dot(n_ref[chunk, :], wu, preferred_element_type=F32)
            h_scr[HALO + r * rows:HALO + (r + 1) * rows, :] = h
            h_ref[chunk, :] = h.astype(BF16)
            ext = h_scr[r * rows:HALO + (r + 1) * rows, :]
            hc = ext * w_ref[k_taps - 1:k_taps, :]
            for k in range(k_taps - 1):
                hc = hc + pltpu.roll(ext, k_taps - 1 - k, 0) * w_ref[k:k + 1, :]
            hc = hc[HALO:, :] + b_ref[...]
            hc_ref[chunk, :] = hc.astype(BF16)
            o_ref[chunk, :] = (_silu(hc[:, :tc]) * hc[:, tc:]).astype(BF16)

    wide = pl.BlockSpec((seq, 2 * tc), lambda b, j: (b, j))
    return pl.pallas_call(
        body,
        name=name,
        grid=(n_seq, nj),
        in_specs=[
            pl.BlockSpec((seq, d), lambda b, j: (b, 0)),
            pl.BlockSpec((d, 2 * tc), lambda b, j: (0, j)),
            pl.BlockSpec((k_taps, 2 * tc), lambda b, j: (0, j)),
            pl.BlockSpec((1, 2 * tc), lambda b, j: (0, j)),
        ],
        out_specs=[wide, wide, pl.BlockSpec((seq, tc), lambda b, j: (b, j))],
        out_shape=[jax.ShapeDtypeStruct((t, f2), BF16), jax.ShapeDtypeStruct((t, f2), BF16),
                   jax.ShapeDtypeStruct((t, f2 // 2), BF16)],
        scratch_shapes=[pltpu.VMEM((HALO + seq, 2 * tc), F32)],
        compiler_params=_params(("parallel", "arbitrary")),
    )(n, w_up, conv_w, conv_b)


def _ffn_down_bx_act_bwd(df, w_down, h, hc, conv_w, n_seq, *, name):
    t, d = df.shape
    f2 = h.shape[1]
    seq = t // n_seq
    tc = FFN_TC
    nj = f2 // (2 * tc)
    k_taps = conv_w.shape[0]

    def body(df_ref, wd_ref, h_ref, hc_ref, w_ref, dh_ref, dw_ref, db_ref):
        dav = lax.dot_general(df_ref[...], wd_ref[...], (((1,), (1,)), ((), ())), preferred_element_type=F32)
        hcv = hc_ref[...].astype(F32)
        gate, val = hcv[:, :tc], hcv[:, tc:]
        dhc = jnp.concatenate([dav * val * _dsilu(gate), dav * _silu(gate)], axis=1)
        dh, dw = _conv_taps_bwd(h_ref[...].astype(F32), dhc, w_ref, k_taps)
        dh_ref[...] = dh.astype(BF16)
        dw_ref[0] = dw
        db_ref[0] = jnp.sum(dhc, axis=0, keepdims=True)

    wide = pl.BlockSpec((seq, 2 * tc), lambda b, j: (b, j))
    dh, dw, db = pl.pallas_call(
        body,
        name=name,
        grid=(n_seq, nj),
        in_specs=[
            pl.BlockSpec((seq, d), lambda b, j: (b, 0)),
            pl.BlockSpec((tc, d), lambda b, j: (j, 0)),
            wide, wide,
            pl.BlockSpec((k_taps, 2 * tc), lambda b, j: (0, j)),
        ],
        out_specs=[
            wide,
            pl.BlockSpec((1, k_taps, 2 * tc), lambda b, j: (b, 0, j)),
            pl.BlockSpec((1, 1, 2 * tc), lambda b, j: (b, 0, j)),
        ],
        out_shape=[
            jax.ShapeDtypeStruct((t, f2), BF16),
            jax.ShapeDtypeStruct((n_seq, k_taps, f2), F32),
            jax.ShapeDtypeStruct((n_seq, 1, f2), F32),
        ],
        compiler_params=_params(("parallel", "arbitrary")),
    )(df, w_down, h, hc, conv_w)
    return dh, jnp.sum(dw, axis=0), jnp.sum(db, axis=0)


def _window_mixed(u, window):
    s = u
    step = 1
    while step < window:
        s = s + _shift_down(s, step)
        step *= 2
    rows = lax.broadcasted_iota(jnp.int32, u.shape, 0)
    inv_cnt = 1.0 / jnp.minimum(rows + 1, window).astype(F32)
    return s * inv_cnt - u, inv_cnt


def _window_mixed_bwd(dmixed, inv_cnt, window):
    r = dmixed * inv_cnt
    s = r
    step = 1
    while step < window:
        s = s + _shift_up(s, step)
        step *= 2
    return s - dmixed


def _pool_fwd(u, w, scale, n_seq, *, name):
    t, d = u.shape
    seq = t // n_seq
    n_g, dg, _ = w.shape

    def body(u_ref, w_ref, s_ref, o_ref):
        for k, window in enumerate(POOL_WINDOWS):
            @pl.when(pl.program_id(1) == k)
            def _(window=window):
                mixed, _ = _window_mixed(u_ref[...], window)
                pre = jnp.dot(mixed.astype(BF16), w_ref[0].astype(BF16), preferred_element_type=F32)
                o_ref[...] = pre * s_ref[...]

    return pl.pallas_call(
        body,
        name=name,
        grid=(n_seq, n_g),
        in_specs=[
            pl.BlockSpec((seq, dg), lambda b, g: (b, g)),
            pl.BlockSpec((1, dg, dg), lambda b, g: (g, 0, 0)),
            pl.BlockSpec((1, dg), lambda b, g: (0, g)),
        ],
        out_specs=pl.BlockSpec((seq, dg), lambda b, g: (b, g)),
        out_shape=jax.ShapeDtypeStruct((t, d), F32),
        compiler_params=_params(("parallel", "parallel")),
    )(u, w, scale)


def _pool_bwd(u, w, scale, dout, n_seq, *, name):
    t, d = u.shape
    seq = t // n_seq
    n_g, dg, _ = w.shape

    def body(u_ref, w_ref, s_ref, do_ref, du_ref, dw_ref, ds_ref):
        group = pl.program_id(0)
        first = pl.program_id(1) == 0
        for k, window in enumerate(POOL_WINDOWS):
            @pl.when(group == k)
            def _(window=window):
                mixed, inv_cnt = _window_mixed(u_ref[...], window)
                mixed_b = mixed.astype(BF16)
                w_b = w_ref[0].astype(BF16)
                dov = do_ref[...]
                pre = jnp.dot(mixed_b, w_b, preferred_element_type=F32)
                dsc = jnp.sum(dov * pre, axis=0, keepdims=True)
                dpre = (dov * s_ref[...]).astype(BF16)
                dw = lax.dot_general(mixed_b, dpre, (((0,), (0,)), ((), ())), preferred_element_type=F32)
                dmixed = lax.dot_general(dpre, w_b, (((1,), (1,)), ((), ())), preferred_element_type=F32)
                du_ref[...] = _window_mixed_bwd(dmixed, inv_cnt, window)

                @pl.when(first)
                def _():
                    dw_ref[0] = dw
                    ds_ref[...] = dsc

                @pl.when(jnp.logical_not(first))
                def _():
                    dw_ref[0] += dw
                    ds_ref[...] += dsc

    return pl.pallas_call(
        body,
        name=name,
        grid=(n_g, n_seq),
        in_specs=[
            pl.BlockSpec((seq, dg), lambda g, b: (b, g)),
            pl.BlockSpec((1, dg, dg), lambda g, b: (g, 0, 0)),
            pl.BlockSpec((1, dg), lambda g, b: (0, g)),
            pl.BlockSpec((seq, dg), lambda g, b: (b, g)),
        ],
        out_specs=[
            pl.BlockSpec((seq, dg), lambda g, b: (b, g)),
            pl.BlockSpec((1, dg, dg), lambda g, b: (g, 0, 0)),
            pl.BlockSpec((1, dg), lambda g, b: (0, g)),
        ],
        out_shape=[
            jax.ShapeDtypeStruct((t, d), F32),
            jax.ShapeDtypeStruct((n_g, dg, dg), F32),
            jax.ShapeDtypeStruct((1, d), F32),
        ],
        compiler_params=_params(("parallel", "arbitrary")),
    )(u, w, scale, dout)


def _adamw(w, g, m, v, *, name):
    shape = w.shape
    c = shape[-1]
    r = w.size // c
    tm = _pick(r, (512, 256, 128, 64, 32, 16, 8))

    def body(w_ref, g_ref, m_ref, v_ref, d_ref, nm_ref, nv_ref):
        gv = g_ref[...]
        nm = ADAM_B1 * m_ref[...] + (1.0 - ADAM_B1) * gv
        nv = ADAM_B2 * v_ref[...] + (1.0 - ADAM_B2) * (gv * gv)
        m_hat = nm / (1.0 - ADAM_B1 ** ADAM_STEP)
        v_hat = nv / (1.0 - ADAM_B2 ** ADAM_STEP)
        d_ref[...] = -ADAM_LR * (m_hat / (jnp.sqrt(v_hat) + ADAM_EPS) + ADAM_WD * w_ref[...])
        nm_ref[...] = nm
        nv_ref[...] = nv

    blk = pl.BlockSpec((tm, c), lambda i: (i, 0))
    out = jax.ShapeDtypeStruct((r, c), F32)
    res = pl.pallas_call(
        body,
        name=name,
        grid=(r // tm,),
        in_specs=[blk] * 4,
        out_specs=[blk] * 3,
        out_shape=[out] * 3,
        compiler_params=_params(("parallel",)),
    )(w.reshape(r, c), g.reshape(r, c), m.reshape(r, c), v.reshape(r, c))
    return tuple(a.reshape(shape) for a in res)


CONV_TC = 256


def _ssd_conv_fwd(proj, col0, n_cols, conv_w, conv_b, n_seq, *, name):
    t = proj.shape[0]
    seq = t // n_seq
    tc = CONV_TC
    off = col0 // tc
    k_taps = conv_w.shape[0]

    def body(h_ref, w_ref, b_ref, o_ref, pre_ref):
        pre = _conv_taps(h_ref[...], w_ref, k_taps) + b_ref[...]
        pre_ref[...] = pre.astype(BF16)
        o_ref[...] = _silu(pre)

    return pl.pallas_call(
        body,
        name=name,
        grid=(n_seq, n_cols // tc),
        in_specs=[
            pl.BlockSpec((seq, tc), lambda b, j: (b, j + off)),
            pl.BlockSpec((k_taps, tc), lambda b, j: (0, j)),
            pl.BlockSpec((1, tc), lambda b, j: (0, j)),
        ],
        out_specs=[pl.BlockSpec((seq, tc), lambda b, j: (b, j))] * 2,
        out_shape=[jax.ShapeDtypeStruct((t, n_cols), F32), jax.ShapeDtypeStruct((t, n_cols), BF16)],
        compiler_params=_params(("parallel", "parallel")),
    )(proj, conv_w, conv_b)


def _ssd_conv_bwd(proj, col0, conv_w, pre, dparts, dproj, n_seq, *, name):
    t = proj.shape[0]
    seq = t // n_seq
    tc = CONV_TC
    off = col0 // tc
    k_taps = conv_w.shape[0]
    widths = [d.shape[1] // tc for d in dparts]
    starts = [sum(widths[:i]) for i in range(len(widths))]
    n_blocks = sum(widths)
    n_parts = len(dparts)

    def body(h_ref, w_ref, pre_ref, *rest):
        part_refs = rest[:n_parts]
        dh_ref, dw_ref, db_ref = rest[n_parts + 1:]
        j = pl.program_id(0)
        da = part_refs[-1][...]
        for i in reversed(range(n_parts - 1)):
            da = jnp.where(j < starts[i + 1], part_refs[i][...], da)
        dhc = da * _dsilu(pre_ref[...].astype(F32))
        dh, dw = _conv_taps_bwd(h_ref[...], dhc, w_ref, k_taps)
        dh_ref[...] = dh.astype(BF16)
        db = jnp.sum(dhc, axis=0, keepdims=True)

        @pl.when(pl.program_id(1) == 0)
        def _():
            dw_ref[...] = dw
            db_ref[...] = db

        @pl.when(pl.program_id(1) > 0)
        def _():
            dw_ref[...] += dw
            db_ref[...] += db

    def part_spec(start, width):
        return pl.BlockSpec((seq, tc), lambda j, b: (b, jnp.clip(j - start, 0, width - 1)))

    n_cols = n_blocks * tc
    return pl.pallas_call(
        body,
        name=name,
        grid=(n_blocks, n_seq),
        in_specs=[
            pl.BlockSpec((seq, tc), lambda j, b: (b, j + off)),
            pl.BlockSpec((k_taps, tc), lambda j, b: (0, j)),
            pl.BlockSpec((seq, tc), lambda j, b: (b, j)),
        ] + [part_spec(st, wd) for st, wd in zip(starts, widths)] + [ANY],
        out_specs=[
            pl.BlockSpec((seq, tc), lambda j, b: (b, j + off)),
            pl.BlockSpec((k_taps, tc), lambda j, b: (0, j)),
            pl.BlockSpec((1, tc), lambda j, b: (0, j)),
        ],
        out_shape=[
            jax.ShapeDtypeStruct(dproj.shape, BF16),
            jax.ShapeDtypeStruct((k_taps, n_cols), F32),
            jax.ShapeDtypeStruct((1, n_cols), F32),
        ],
        input_output_aliases={3 + n_parts: 0},
        compiler_params=_params(("parallel", "arbitrary")),
    )(proj, conv_w, pre, *dparts, dproj)


def _fill_cols(buf, src, col0, *, name):
    t, c = src.shape
    tm = _pick(t, (1024, 512, 256, 128))

    def body(s_ref, b_ref, o_ref):
        o_ref[...] = s_ref[...].astype(o_ref.dtype)

    return pl.pallas_call(
        body,
        name=name,
        grid=(t // tm,),
        in_specs=[pl.BlockSpec((tm, c), lambda i: (i, 0)), ANY],
        out_specs=pl.BlockSpec((tm, c), lambda i: (i, col0 // c)),
        out_shape=jax.ShapeDtypeStruct(buf.shape, buf.dtype),
        input_output_aliases={1: 0},
        compiler_params=_params(("parallel",)),
    )(src, buf)


def _softplus(x):
    return jnp.maximum(x, 0.0) + jnp.log(1.0 + jnp.exp(-jnp.abs(x)))


def _chunk_decay(dtraw, bias, alog):
    q = dtraw.shape[0]
    dt = _softplus(dtraw + bias)
    a = -jnp.exp(alog)
    rows = lax.broadcasted_iota(jnp.int32, (q, q), 0)
    cols = lax.broadcasted_iota(jnp.int32, (q, q), 1)
    lower = rows >= cols
    acum = jnp.dot(lower.astype(F32), dt * a, precision=lax.Precision.HIGHEST, preferred_element_type=F32)
    return dt, a, acum, acum.T, lower


def _dot_exact(v, sel, terms=3):
    hi = v.astype(BF16)
    r1 = v - hi.astype(F32)
    mid = r1.astype(BF16)
    out = jnp.dot(hi, sel, preferred_element_type=F32) + jnp.dot(mid, sel, preferred_element_type=F32)
    if terms == 3:
        lo = (r1 - mid.astype(F32)).astype(BF16)
        out = out + jnp.dot(lo, sel, preferred_element_type=F32)
    return out


def _head_selectors(gw, p):
    sum_heads = (lax.broadcasted_iota(jnp.int32, (gw, LANE), 0) // p == lax.broadcasted_iota(jnp.int32, (gw, LANE), 1))
    spread = (lax.broadcasted_iota(jnp.int32, (LANE, gw), 0) == lax.broadcasted_iota(jnp.int32, (LANE, gw), 1) // p)
    return sum_heads.astype(BF16), spread.astype(BF16)


def _row_spread(v, spread):
    return _dot_exact(jnp.broadcast_to(v, (8, v.shape[1])), spread)[0:1, :]


def _head_pad(v, r_heads):
    lead = v.shape[:-1]
    vg = v.reshape(lead + (N_SSD_GROUPS, r_heads))
    vg = jnp.pad(vg, [(0, 0)] * len(lead) + [(0, 0), (0, LANE - r_heads)])
    out = vg.reshape(lead + (N_SSD_GROUPS * LANE,))
    return out[None] if out.ndim == 1 else out


def _head_unpad(v, r_heads):
    lead = v.shape[:-1]
    out = v.reshape(lead + (N_SSD_GROUPS, LANE))[..., :r_heads].reshape(lead + (N_SSD_GROUPS * r_heads,))
    return out[0] if (len(lead) == 1 and lead[0] == 1) else out


def _ssd_w_in_layout(w_in, d_inner, d_xbc, r_heads):
    main = w_in[:, :d_inner + d_xbc]
    return jnp.concatenate([main, _head_pad(w_in[:, d_inner + d_xbc:], r_heads)], axis=1)


def _ssd_w_in_unlayout(w, d_inner, d_xbc, r_heads):
    main = w[:, :d_inner + d_xbc]
    return jnp.concatenate([main, _head_unpad(w[:, d_inner + d_xbc:], r_heads)], axis=1)


SSD_CHUNKS = 8
SSD_CHUNKS_FWD = 16


def _ssd_dims(proj, xbc):
    d_xbc = xbc.shape[1]
    d_inner = d_xbc - 2 * N_SSD_GROUPS * D_STATE
    gw = d_inner // N_SSD_GROUPS
    return d_inner, d_xbc, gw, gw // HEAD_DIM


def _ssd_fwd(proj, xbc, bias_p, alog_p, dskip_p, norm_w, n_seq, *, name):
    t = proj.shape[0]
    d_inner, d_xbc, gw, r_heads = _ssd_dims(proj, xbc)
    q, n, n_g, p = CHUNK, D_STATE, N_SSD_GROUPS, HEAD_DIM
    seq = t // n_seq
    nc = seq // q
    cps = SSD_CHUNKS_FWD if nc % SSD_CHUNKS_FWD == 0 else 1
    dt_blk0 = (d_inner + d_xbc) // LANE

    def body(x_ref, b_ref, c_ref, z_ref, dtr_ref, bias_ref, alog_ref, dsk_ref, nw_ref, yn_ref, y_ref, hs_ref, h_scr):
        @pl.when(pl.program_id(2) == 0)
        def _():
            h_scr[...] = jnp.zeros_like(h_scr)

        for cc in range(cps):
            rows = pl.ds(cc * q, q)
            chunk(x_ref.at[rows, :], b_ref.at[rows, :], c_ref.at[rows, :], z_ref.at[rows, :], dtr_ref.at[rows, :],
                  bias_ref, alog_ref, dsk_ref, nw_ref, yn_ref.at[rows, :], y_ref.at[rows, :],
                  hs_ref.at[pl.ds(cc * n, n), :], h_scr)

    def chunk(x_ref, b_ref, c_ref, z_ref, dtr_ref, bias_ref, alog_ref, dsk_ref, nw_ref, yn_ref, y_ref, hs_ref, h_scr):
        dt, a, acum, acum_t, lower = _chunk_decay(dtr_ref[...], bias_ref[...], alog_ref[...])
        x = x_ref[...]
        bb = b_ref[...].astype(BF16)
        cb = c_ref[...].astype(BF16)
        g_mat = lax.dot_general(cb, bb, (((1,), (1,)), ((), ())), preferred_element_type=F32)
        h_prev = h_scr[...]
        hs_ref[...] = h_prev
        c_h = jnp.dot(cb, h_prev.astype(BF16), preferred_element_type=F32)
        _, spread = _head_selectors(gw, p)
        acum_s = _dot_exact(acum, spread)
        a_last_s = acum_s[q - 1:q, :]
        xdt = x * _dot_exact(dt, spread, 2)
        xdt_b = xdt.astype(BF16)
        ys = []
        for h in range(r_heads):
            decay = jnp.exp(jnp.where(lower, acum[:, h:h + 1] - acum_t[h:h + 1, :], -jnp.inf))
            ys.append(jnp.dot((g_mat * decay).astype(BF16), xdt_b[:, h * p:(h + 1) * p], preferred_element_type=F32))
        y = jnp.concatenate(ys, axis=1) + jnp.exp(acum_s) * c_h + _row_spread(dsk_ref[...], spread) * x
        xd = xdt * jnp.exp(a_last_s - acum_s)
        states = lax.dot_general(bb, xd.astype(BF16), (((0,), (0,)), ((), ())), preferred_element_type=F32)
        h_scr[...] = h_prev * jnp.exp(a_last_s) + states
        y_ref[...] = y
        gated = y * _silu(z_ref[...])
        rstd = lax.rsqrt(jnp.mean(gated * gated, axis=-1, keepdims=True) + EPS)
        yn_ref[...] = (gated * rstd * nw_ref[...]).astype(BF16)

    row = lambda b, g, c: b * (nc // cps) + c
    vec = pl.BlockSpec((1, LANE), lambda b, g, c: (0, g))
    return pl.pallas_call(
        body,
        name=name,
        grid=(n_seq, n_g, nc // cps),
        in_specs=[
            pl.BlockSpec((q * cps, gw), lambda b, g, c: (row(b, g, c), g)),
            pl.BlockSpec((q * cps, n), lambda b, g, c: (row(b, g, c), d_inner // n + g)),
            pl.BlockSpec((q * cps, n), lambda b, g, c: (row(b, g, c), d_inner // n + n_g + g)),
            pl.BlockSpec((q * cps, gw), lambda b, g, c: (row(b, g, c), g)),
            pl.BlockSpec((q * cps, LANE), lambda b, g, c: (row(b, g, c), dt_blk0 + g)),
            vec, vec, vec,
            pl.BlockSpec((1, gw), lambda b, g, c: (0, g)),
        ],
        out_specs=[
            pl.BlockSpec((q * cps, gw), lambda b, g, c: (row(b, g, c), g)),
            pl.BlockSpec((q * cps, gw), lambda b, g, c: (row(b, g, c), g)),
            pl.BlockSpec((n * cps, gw), lambda b, g, c: (row(b, g, c), g)),
        ],
        out_shape=[
            jax.ShapeDtypeStruct((t, d_inner), BF16),
            jax.ShapeDtypeStruct((t, d_inner), F32),
            jax.ShapeDtypeStruct((n_seq * nc * n, d_inner), F32),
        ],
        scratch_shapes=[pltpu.VMEM((n, gw), F32)],
        compiler_params=_params(("parallel", "parallel", "arbitrary")),
    )(xbc, xbc, xbc, proj, proj, bias_p, alog_p, dskip_p, norm_w)


def _ssd_bwd(proj, xbc, hs, y, dyn, bias_p, alog_p, dskip_p, norm_w, n_seq, *, name):
    t = proj.shape[0]
    d_inner, d_xbc, gw, r_heads = _ssd_dims(proj, xbc)
    q, n, n_g, p = CHUNK, D_STATE, N_SSD_GROUPS, HEAD_DIM
    seq = t // n_seq
    nc = seq // q
    cps = SSD_CHUNKS if nc % SSD_CHUNKS == 0 else 1
    dt_blk0 = (d_inner + d_xbc) // LANE

    def body(x_ref, b_ref, c_ref, z_ref, dtr_ref, bias_ref, alog_ref, dsk_ref, nw_ref, hs_ref, y_ref, dyn_ref,
             dx_ref, db_ref, dc_ref, dz_ref, ddtr_ref, dnw_ref, dbias_ref, dalog_ref, ddsk_ref, dh_scr):
        @pl.when(pl.program_id(2) == 0)
        def _():
            dh_scr[...] = jnp.zeros_like(dh_scr)

        first_step = jnp.logical_and(pl.program_id(1) == 0, pl.program_id(2) == 0)
        for cc in reversed(range(cps)):
            rows = pl.ds(cc * q, q)
            chunk(jnp.logical_and(first_step, cc == cps - 1), x_ref.at[rows, :], b_ref.at[rows, :], c_ref.at[rows, :],
                  z_ref.at[rows, :], dtr_ref.at[rows, :], bias_ref, alog_ref, dsk_ref, nw_ref,
                  hs_ref.at[pl.ds(cc * n, n), :], y_ref.at[rows, :], dyn_ref.at[rows, :], dx_ref.at[rows, :],
                  db_ref.at[rows, :], dc_ref.at[rows, :], dz_ref.at[rows, :], ddtr_ref.at[rows, :],
                  dnw_ref, dbias_ref, dalog_ref, ddsk_ref, dh_scr)

    def chunk(first, x_ref, b_ref, c_ref, z_ref, dtr_ref, bias_ref, alog_ref, dsk_ref, nw_ref, hs_ref, y_ref, dyn_ref,
              dx_ref, db_ref, dc_ref, dz_ref, ddtr_ref, dnw_ref, dbias_ref, dalog_ref, ddsk_ref, dh_scr):
        dtraw = dtr_ref[...]
        dt, a, acum, acum_t, lower = _chunk_decay(dtraw, bias_ref[...], alog_ref[...])
        x = x_ref[...]
        bb = b_ref[...].astype(BF16)
        cb = c_ref[...].astype(BF16)
        g_mat = lax.dot_general(cb, bb, (((1,), (1,)), ((), ())), preferred_element_type=F32)

        yv = y_ref[...]
        z = z_ref[...]
        sz = _silu(z)
        gated = yv * sz
        rstd = lax.rsqrt(jnp.mean(gated * gated, axis=-1, keepdims=True) + EPS)
        gn = gated * rstd
        dynv = dyn_ref[...]
        gwt = dynv * nw_ref[...]
        dgated = rstd * (gwt - gn * jnp.mean(gwt * gn, axis=-1, keepdims=True))
        dnw = jnp.sum(dynv * gn, axis=0, keepdims=True)
        dy = dgated * sz
        dz_ref[...] = (dgated * yv * _dsilu(z)).astype(BF16)

        h_prev = hs_ref[...]
        h_prev_b = h_prev.astype(BF16)
        ds = dh_scr[...]
        ds_b = ds.astype(BF16)
        sum_heads, spread = _head_selectors(gw, p)
        acum_s = _dot_exact(acum, spread)
        a_last_s = acum_s[q - 1:q, :]
        dt_s = _dot_exact(dt, spread, 2)
        dsk_s = _row_spread(dsk_ref[...], spread)
        dte_s = jnp.exp(a_last_s - acum_s)
        cd_s = jnp.exp(a_last_s)
        xdt = x * dt_s
        xdt_b = xdt.astype(BF16)
        dy_b = dy.astype(BF16)
        gt_mat = lax.dot_general(bb, cb, (((1,), (1,)), ((), ())), preferred_element_type=F32)
        upper = lax.broadcasted_iota(jnp.int32, (q, q), 0) <= lax.broadcasted_iota(jnp.int32, (q, q), 1)
        dg = jnp.zeros((q, q), F32)
        dxdts, w_diffs = [], []
        for h in range(r_heads):
            hsl = slice(h * p, (h + 1) * p)
            diff = acum[:, h:h + 1] - acum_t[h:h + 1, :]
            decay = jnp.exp(jnp.where(lower, diff, -jnp.inf))
            decay_t = jnp.exp(jnp.where(upper, -diff, -jnp.inf))
            mt_mat = gt_mat * decay_t
            dm = lax.dot_general(dy_b[:, hsl], xdt_b[:, hsl], (((1,), (1,)), ((), ())), preferred_element_type=F32)
            dm_t = lax.dot_general(xdt_b[:, hsl], dy_b[:, hsl], (((1,), (1,)), ((), ())), preferred_element_type=F32)
            dg = dg + dm * decay
            dxdts.append(jnp.dot(mt_mat.astype(BF16), dy_b[:, hsl], preferred_element_type=F32))
            w_diffs.append(dm * (g_mat * decay) - dm_t * mt_mat)
        sel_q = (lax.broadcasted_iota(jnp.int32, (r_heads * q, LANE), 0) // q
                 == lax.broadcasted_iota(jnp.int32, (r_heads * q, LANE), 1)).astype(BF16)
        dacum_diag = _dot_exact(jnp.concatenate(w_diffs, axis=1), sel_q, 2)
        c_h = jnp.dot(cb, h_prev_b, preferred_element_type=F32)
        dxd = jnp.dot(bb, ds_b, preferred_element_type=F32)
        dxdt = jnp.concatenate(dxdts, axis=1) + dxd * dte_s
        dye = dy * jnp.exp(acum_s)
        dye_b = dye.astype(BF16)
        xd = xdt * dte_s
        xd_b = xd.astype(BF16)
        dg_b = dg.astype(BF16)
        dx_ref[...] = dxdt * dt_s + dsk_s * dy
        dc_ref[...] = (jnp.dot(dg_b, bb, preferred_element_type=F32)
                       + lax.dot_general(dye_b, h_prev_b, (((1,), (1,)), ((), ())), preferred_element_type=F32))
        db_ref[...] = (lax.dot_general(dg_b, cb, (((0,), (0,)), ((), ())), preferred_element_type=F32)
                       + lax.dot_general(xd_b, ds_b, (((1,), (1,)), ((), ())), preferred_element_type=F32))
        dh_scr[...] = ds * cd_s + lax.dot_general(cb, dye_b, (((0,), (0,)), ((), ())), preferred_element_type=F32)
        ddt_cols = _dot_exact(x * dxdt, sum_heads, 2)
        dacum_y = _dot_exact(dye * c_h - dxd * xd, sum_heads, 2)
        col_sums = jnp.concatenate([
            jnp.sum(dxd * xd, axis=0, keepdims=True) + jnp.sum(ds * h_prev, axis=0, keepdims=True) * cd_s,
            jnp.sum(dy * x, axis=0, keepdims=True),
            jnp.zeros((6, gw), F32)], axis=0)
        col_sums = _dot_exact(col_sums, sum_heads, 2)
        ddsk = col_sums[1:2, :]
        rows_q = lax.broadcasted_iota(jnp.int32, (q, LANE), 0)
        dacum = dacum_diag + dacum_y + jnp.where(rows_q == q - 1, col_sums[0:1, :], 0.0)
        dadt = jnp.dot(upper.astype(F32), dacum, precision=lax.Precision.HIGHEST, preferred_element_type=F32)
        ddt = dadt * a + ddt_cols
        ddtr = ddt * _sigmoid(dtraw + bias_ref[...])
        ddtr_ref[...] = ddtr
        dbias = jnp.sum(ddtr, axis=0, keepdims=True)
        dalog = jnp.sum(dadt * dt, axis=0, keepdims=True) * a

        @pl.when(first)
        def _():
            dnw_ref[...] = dnw
            dbias_ref[...] = dbias
            dalog_ref[...] = dalog
            ddsk_ref[...] = ddsk

        @pl.when(jnp.logical_not(first))
        def _():
            dnw_ref[...] += dnw
            dbias_ref[...] += dbias
            dalog_ref[...] += dalog
            ddsk_ref[...] += ddsk

    row = lambda g, b, c: b * (nc // cps) + (nc // cps - 1 - c)
    vec = pl.BlockSpec((1, LANE), lambda g, b, c: (0, g))
    wide = pl.BlockSpec((q * cps, gw), lambda g, b, c: (row(g, b, c), g))
    narrow = pl.BlockSpec((q * cps, n), lambda g, b, c: (row(g, b, c), g))
    return pl.pallas_call(
        body,
        name=name,
        grid=(n_g, n_seq, nc // cps),
        in_specs=[
            wide,
            pl.BlockSpec((q * cps, n), lambda g, b, c: (row(g, b, c), d_inner // n + g)),
            pl.BlockSpec((q * cps, n), lambda g, b, c: (row(g, b, c), d_inner // n + n_g + g)),
            wide,
            pl.BlockSpec((q * cps, LANE), lambda g, b, c: (row(g, b, c), dt_blk0 + g)),
            vec, vec, vec,
            pl.BlockSpec((1, gw), lambda g, b, c: (0, g)),
            pl.BlockSpec((n * cps, gw), lambda g, b, c: (row(g, b, c), g)),
            wide, wide,
        ],
        out_specs=[
            wide, narrow, narrow, wide, narrow,
            pl.BlockSpec((1, gw), lambda g, b, c: (0, g)),
            vec, vec, vec,
        ],
        out_shape=[
            jax.ShapeDtypeStruct((t, d_inner), F32),
            jax.ShapeDtypeStruct((t, n_g * n), F32),
            jax.ShapeDtypeStruct((t, n_g * n), F32),
            jax.ShapeDtypeStruct(proj.shape, BF16),
            jax.ShapeDtypeStruct((t, n_g * LANE), F32),
            jax.ShapeDtypeStruct((1, d_inner), F32),
            jax.ShapeDtypeStruct((1, n_g * LANE), F32),
            jax.ShapeDtypeStruct((1, n_g * LANE), F32),
            jax.ShapeDtypeStruct((1, n_g * LANE), F32),
        ],
        scratch_shapes=[pltpu.VMEM((n, gw), F32)],
        compiler_params=_params(("parallel", "arbitrary", "arbitrary")),
    )(xbc, xbc, xbc, proj, proj, bias_p, alog_p, dskip_p, norm_w, hs, y, dyn)


MESH_IDS = pl.DeviceIdType.MESH


def _my_index():
    return 4 * lax.axis_index("x") + 2 * lax.axis_index("y") + lax.axis_index("c")


def _all_gather(shard, *, name):
    def body(x_ref, out_ref, send_sems, recv_sems, local_sem):
        x, y, c = lax.axis_index("x"), lax.axis_index("y"), lax.axis_index("c")
        me, sibling = (x, y, c), (x, y, 1 - c)
        chips = [(1 - x, y), (x, 1 - y), (1 - x, 1 - y)]

        def blk(px, py, pc):
            return out_ref.at[4 * px + 2 * py + pc]

        def copy(k, block, to, src=None):
            return pltpu.make_async_remote_copy(
                src_ref=blk(*block) if src is None else src, dst_ref=blk(*block),
                send_sem=send_sems.at[k], recv_sem=recv_sems.at[k], device_id=to, device_id_type=MESH_IDS)

        mine = pltpu.make_async_copy(x_ref, blk(*me), local_sem)
        mine.start()
        first = [copy(0, me, sibling, src=x_ref)]
        first += [copy(1 + j, me, (*chip, c), src=x_ref) for j, chip in enumerate(chips)]
        for cp in first:
            cp.start()
        passed = [copy(4 + j, (*chip, c), sibling) for j, chip in enumerate(chips)]
        for j, chip in enumerate(chips):
            copy(1 + j, (*chip, c), me).wait_recv()
            passed[j].start()
        copy(0, sibling, me).wait_recv()
        for j, chip in enumerate(chips):
            copy(4 + j, (*chip, 1 - c), me).wait_recv()
        for cp in first + passed:
            cp.wait_send()
        mine.wait()

    return pl.pallas_call(
        body,
        name=name,
        in_specs=[ANY],
        out_specs=ANY,
        out_shape=jax.ShapeDtypeStruct((N_DEV,) + shard.shape, shard.dtype),
        scratch_shapes=[pltpu.SemaphoreType.DMA((7,)), pltpu.SemaphoreType.DMA((7,)), pltpu.SemaphoreType.DMA],
    )(shard)


HBM_SPEC = pl.BlockSpec(memory_space=pltpu.HBM)
SEM_SPEC = pl.BlockSpec(memory_space=pltpu.SEMAPHORE)
SPLIT_COPY_PARAMS = pltpu.CompilerParams(has_side_effects=pltpu.SideEffectType.DATAFLOW_SIDE_EFFECTING)


def _peer_list():
    x, y, c = lax.axis_index("x"), lax.axis_index("y"), lax.axis_index("c")
    peers = []
    for k in range(1, N_DEV):
        px = 1 - x if k & 4 else x
        py = 1 - y if k & 2 else y
        pc = 1 - c if k & 1 else c
        peers.append(((px, py, pc), 4 * px + 2 * py + pc))
    return 4 * x + 2 * y + c, peers


def _push_copies(src_refs, land_refs, send_sems, recv_sems, blockwise):
    me, peers = _peer_list()
    copies = []
    for a, (src_ref, land_ref) in enumerate(zip(src_refs, land_refs)):
        for k, (dev, idx) in enumerate(peers):
            sem = a * (N_DEV - 1) + k
            src = src_ref.at[idx] if blockwise else src_ref
            copies.append(tuple(
                pltpu.make_async_remote_copy(src_ref=src, dst_ref=land_ref.at[slot], send_sem=send_sems.at[sem],
                                             recv_sem=recv_sems.at[sem], device_id=dev, device_id_type=MESH_IDS)
                for slot in (me, idx)))
    return copies


def _push_start(srcs, blockwise, after, *, name):
    n = len(srcs)
    blocks = [s_.shape[1:] if blockwise else s_.shape for s_ in srcs]

    def body(*refs):
        src_refs, land_refs = refs[:n], refs[n:2 * n]
        send_sems, recv_sems = refs[2 * n + 1], refs[2 * n + 2]
        token = refs[-1]
        for send, _ in _push_copies(src_refs, land_refs, send_sems, recv_sems, blockwise):
            send.start()
        token[...] = jnp.zeros_like(token)

    n_sem = n * (N_DEV - 1)
    lands = [lax.empty((N_DEV,) + b, s_.dtype) for b, s_ in zip(blocks, srcs)]
    out = pl.pallas_call(
        body,
        name=name,
        in_specs=[HBM_SPEC] * (2 * n) + [ANY],
        out_specs=(SEM_SPEC, SEM_SPEC) + (HBM_SPEC,) * (2 * n) + (pl.BlockSpec(memory_space=pltpu.VMEM),),
        out_shape=(pltpu.SemaphoreType.DMA((n_sem,)), pltpu.SemaphoreType.DMA((n_sem,)))
        + tuple(pltpu.HBM(a.shape, a.dtype) for a in list(srcs) + lands)
        + (jax.ShapeDtypeStruct((8, LANE), F32),),
        input_output_aliases={i: 2 + i for i in range(2 * n)},
        compiler_params=SPLIT_COPY_PARAMS,
    )(*[pltpu.with_memory_space_constraint(a, pltpu.HBM) for a in list(srcs) + lands], after)
    return out[0], out[1], out[2:2 + n], out[2 + n:2 + 2 * n], out[-1]


def _push_wait(send_sems, recv_sems, srcs, lands, blockwise, after, *, name):
    n = len(srcs)

    def body(*refs):
        src_refs, land_refs = refs[:n], refs[n:2 * n]
        send_sems, recv_sems = refs[2 * n], refs[2 * n + 1]
        for send, recv in _push_copies(src_refs, land_refs, send_sems, recv_sems, blockwise):
            send.wait_send()
            recv.wait_recv()

    out = pl.pallas_call(
        body,
        name=name,
        in_specs=[HBM_SPEC] * (2 * n) + [SEM_SPEC, SEM_SPEC, ANY],
        out_specs=(HBM_SPEC,) * (2 * n),
        out_shape=tuple(pltpu.HBM(a.shape, a.dtype) for a in list(srcs) + list(lands)),
        input_output_aliases={i: i for i in range(2 * n)},
        compiler_params=SPLIT_COPY_PARAMS,
    )(*srcs, *lands, send_sems, recv_sems, after)
    return out[n:]


def _with_own_slot(landing, own):
    slot = lax.broadcasted_iota(jnp.int32, (N_DEV,) + (1,) * own.ndim, 0)
    return jnp.where(slot == _my_index(), own[None], landing)


def _sum_slots(parts, own=None, *, name):
    shape = parts.shape[1:]
    n, c = parts.shape[0], parts.shape[-1]
    r = parts.size // (n * c)
    tm = _pick(r, (256, 128, 64, 32, 16, 8))

    def body(p_ref, *rest):
        o_ref = rest[-1]
        me = _my_index()

        def slot(s):
            if own is None:
                return p_ref[s].astype(F32)
            return jnp.where(me == s, rest[0][...], p_ref[s]).astype(F32)

        acc = slot(0)
        for s in range(1, n):
            acc = acc + slot(s)
        o_ref[...] = acc

    tile = pl.BlockSpec((tm, c), lambda i: (i, 0))
    return pl.pallas_call(
        body,
        name=name,
        grid=(r // tm,),
        in_specs=[pl.BlockSpec((n, tm, c), lambda i: (0, i, 0))] + ([] if own is None else [tile]),
        out_specs=tile,
        out_shape=jax.ShapeDtypeStruct((r, c), F32),
        compiler_params=_params(("parallel",)),
    )(parts.reshape(n, r, c), *([] if own is None else [own.reshape(r, c)])).reshape(shape)


def _row_count(shape):
    c = shape[-1]
    rows = 1
    for s in shape[:-1]:
        rows *= s
    return rows, c, c + (-c) % LANE


PACK_ROWS = 256


def _pack_rows(arrays):
    pieces = []
    for a in arrays:
        rows, c, cp = _row_count(a.shape)
        a2 = a.reshape(rows, c)
        if cp > c:
            a2 = jnp.pad(a2, ((0, 0), (0, cp - c)))
        a2 = a2.reshape(rows * cp // LANE, LANE)
        if a2.shape[0] % 8:
            a2 = jnp.pad(a2, ((0, 8 - a2.shape[0] % 8), (0, 0)))
        pieces.append(a2)
    total = sum(p.shape[0] for p in pieces)
    if total % PACK_ROWS:
        pieces.append(jnp.zeros((PACK_ROWS - total % PACK_ROWS, LANE), F32))
    return jnp.concatenate(pieces, axis=0)


def _unpack_rows(packed, shapes, lead=()):
    out, off = [], 0
    for shp in shapes:
        rows, c, cp = _row_count(shp)
        n_rows = rows * cp // LANE
        seg = packed[..., off:off + n_rows, :].reshape(lead + (rows, cp))
        out.append(seg[..., :c].reshape(lead + tuple(shp)))
        off += n_rows + (-n_rows) % 8
    return out


def _unshard(stacked, axis):
    if axis == stacked.ndim - 2:
        return jnp.concatenate([stacked[d] for d in range(N_DEV)], axis=axis)
    moved = jnp.moveaxis(stacked, 0, axis)
    shp = moved.shape
    return moved.reshape(shp[:axis] + (shp[axis] * shp[axis + 1],) + shp[axis + 2:])


def _shard_major(full, axis):
    shp = full.shape
    if axis == full.ndim - 1:
        size = shp[axis] // N_DEV
        return jnp.stack([full[..., d * size:(d + 1) * size] for d in range(N_DEV)])
    split = full.reshape(shp[:axis] + (N_DEV, shp[axis] // N_DEV) + shp[axis + 1:])
    return jnp.moveaxis(split, axis, 0)


def _my_shard(full, axis):
    size = full.shape[axis] // N_DEV
    return lax.dynamic_slice_in_dim(full, _my_index() * size, size, axis)


def _local_step(x, target, w, fetch, emit, n_seq):
    def with_token(vec, token):
        return vec + jnp.tile(token[0:1, :], (1, vec.shape[1] // LANE))

    depth, d_model = w["norm_mix_pre"].shape
    d_inner = w["ssd_norm_w"].shape[1]
    d_xbc = w["ssd_conv_w"].shape[2]
    saved = []
    m, token = fetch(0, "mix", x)
    mix_pre_w = with_token(w["norm_mix_pre"][0:1], token)
    u = _rms_fwd(x, mix_pre_w, out_dtype=BF16, name="l0_mix_pre")
    for i in range(depth):
        j = i // 2
        s = {"x": x, "mix_pre_w": mix_pre_w}
        if i % 2 == 0:
            proj = _mm(u, m["ssd_w_in"], name=f"l{i}_ssd_in")
            xbc, xbc_pre = _ssd_conv_fwd(proj, d_inner, d_xbc, w["ssd_conv_w"][j], w["ssd_conv_b"][j:j + 1], n_seq,
                                         name=f"l{i}_ssd_conv")
            yn, y, hs = _ssd_fwd(proj, xbc, w["ssd_dt_bias"][j:j + 1], w["ssd_a_log"][j:j + 1], w["ssd_d"][j:j + 1],
                                 w["ssd_norm_w"][j:j + 1], n_seq, name=f"l{i}_ssd_scan")
            m_out, token = fetch(i, "out", yn)
            m = {**m, **m_out}
            mix = _mm(yn, m["ssd_w_out"], name=f"l{i}_ssd_out")
            s.update(u=u, proj=proj, xbc=xbc, xbc_pre=xbc_pre, yn=yn, y=y, hs=hs)
        else:
            mix = _pool_fwd(u, m["pool_w"], w["pool_scale"][j:j + 1], n_seq, name=f"l{i}_pool")
            s.update(u=u)
            token = jnp.zeros((8, LANE), F32)
        mix_post_w = with_token(w["norm_mix_post"][i:i + 1], token)
        m_ffn, token = fetch(i, "ffn", mix)
        m = {**m, **m_ffn}
        ffn_pre_w = with_token(w["norm_ffn_pre"][i:i + 1], token)
        x1, n = _res_rms_rms(x, mix, mix_post_w, ffn_pre_w, name=f"l{i}_mix_post_ffn_pre")
        h, hc, a = _ffn_up_act(n, m["ffn_w_up"], w["ffn_conv_w"][i], w["ffn_conv_b"][i:i + 1], n_seq,
                               name=f"l{i}_ffn_up_act")
        f = _mm(a, m["ffn_w_down"], name=f"l{i}_ffn_down")
        s.update(mix=mix, x1=x1, n=n, h=h, hc=hc, a=a, f=f, m=m, ffn_pre_w=ffn_pre_w)
        saved.append(s)
        if i + 1 < depth:
            m, token = fetch(i + 1, "mix", f)
            mix_pre_w = with_token(w["norm_mix_pre"][i + 1:i + 2], token)
            x, u = _res_rms_rms(x1, f, w["norm_ffn_post"][i:i + 1], mix_pre_w,
                                out_dtype=BF16 if (i + 1) % 2 == 0 else F32, name=f"l{i}_ffn_post_mix_pre")
        else:
            x = _res_rms_fwd(x1, f, w["norm_ffn_post"][i:i + 1], name=f"l{i}_ffn_post")

    loss, dx = _loss_head(x, target)
    grads = {k: [None] * len(w[k]) for k in SMALL}
    df, grads["norm_ffn_post"][depth - 1] = _rms_bwd(saved[-1]["f"], w["norm_ffn_post"][depth - 1:depth], dx, None,
                                                      out_dtype=BF16, name=f"l{depth - 1}_ffn_post_b")
    for i in reversed(range(depth)):
        j = i // 2
        s = saved[i]
        m, gm = s["m"], {}
        gm["ffn_w_down"] = _mm(s["a"], df, ta=True, name=f"l{i}_ffn_down_bw")
        dh, grads["ffn_conv_w"][i], grads["ffn_conv_b"][i] = _ffn_down_bx_act_bwd(
            df, m["ffn_w_down"], s["h"], s["hc"], w["ffn_conv_w"][i], n_seq, name=f"l{i}_ffn_act_b")
        dn = _mm(dh, m["ffn_w_up"], tb=True, name=f"l{i}_ffn_up_bx")
        gm["ffn_w_up"] = _mm(s["n"], dh, ta=True, name=f"l{i}_ffn_up_bw")
        token = emit(i, "ffn", gm, dn)
        gm = {}
        dx1, dmix, grads["norm_ffn_pre"][i], grads["norm_mix_post"][i] = _rms_bwd2(
            s["x1"], s["ffn_pre_w"], dn, dx, s["mix"], with_token(w["norm_mix_post"][i:i + 1], token),
            out_dtype=BF16 if i % 2 == 0 else F32, name=f"l{i}_ffn_pre_mix_post_b")
        if i % 2 == 0:
            dyn = _mm(dmix, m["ssd_w_out"], tb=True, name=f"l{i}_ssd_out_bx")
            gm["ssd_w_out"] = _mm(s["yn"], dmix, ta=True, name=f"l{i}_ssd_out_bw")
            token = emit(i, "out", gm, dyn)
            gm = {}
            dxs, db, dc, dz, ddtr, dnw, dbias, dalog, ddsk = _ssd_bwd(
                s["proj"], s["xbc"], s["hs"], s["y"], dyn, w["ssd_dt_bias"][j:j + 1], w["ssd_a_log"][j:j + 1],
                w["ssd_d"][j:j + 1], with_token(w["ssd_norm_w"][j:j + 1], token), n_seq, name=f"l{i}_ssd_scan_b")
            grads["ssd_norm_w"][j], grads["ssd_dt_bias"][j], grads["ssd_a_log"][j], grads["ssd_d"][j] = (
                dnw, dbias, dalog, ddsk)
            dproj, grads["ssd_conv_w"][j], grads["ssd_conv_b"][j] = _ssd_conv_bwd(
                s["proj"], d_inner, w["ssd_conv_w"][j], s["xbc_pre"], (dxs, db, dc), dz, n_seq,
                name=f"l{i}_ssd_conv_b")
            dproj = _fill_cols(dproj, ddtr, d_inner + d_xbc, name=f"l{i}_ssd_dt_b")
            gm["ssd_w_in"] = _mm(s["u"], dproj, ta=True, name=f"l{i}_ssd_in_bw")
            token = emit(i, "mix", gm, dproj)
            du = _mm(dproj, m["ssd_w_in"], tb=True, name=f"l{i}_ssd_in_bx")
        else:
            du, gm["pool_w"], grads["pool_scale"][j] = _pool_bwd(
                s["u"], m["pool_w"], w["pool_scale"][j:j + 1], dmix, n_seq, name=f"l{i}_pool_b")
            token = emit(i, "mix", gm, du)
        if i > 0:
            dx, df, grads["norm_mix_pre"][i], grads["norm_ffn_post"][i - 1] = _rms_bwd2(
                s["x"], with_token(s["mix_pre_w"], token), du, dx1, saved[i - 1]["f"], w["norm_ffn_post"][i - 1:i],
                out_dtype=BF16, name=f"l{i}_mix_pre_ffn_post_b")
        else:
            dx, grads["norm_mix_pre"][i] = _rms_bwd(s["x"], with_token(s["mix_pre_w"], token), du, dx1,
                                                    name=f"l{i}_mix_pre_b")
    return loss, dx, grads


BIG = (("ssd_w_in", 2), ("ssd_w_out", 1), ("pool_w", 2), ("ffn_w_up", 2), ("ffn_w_down", 1))
SMALL_SHARDED = (("ssd_conv_w", 2), ("ffn_conv_w", 2), ("pool_scale", 1))
SMALL = ("ssd_conv_w", "ssd_conv_b", "ssd_dt_bias", "ssd_a_log", "ssd_d", "ssd_norm_w", "pool_scale", "ffn_conv_w",
         "ffn_conv_b", "norm_mix_pre", "norm_mix_post", "norm_ffn_pre", "norm_ffn_post")
WEIGHTS = ("ssd_w_in", "ssd_conv_w", "ssd_conv_b", "ssd_dt_bias", "ssd_a_log", "ssd_d", "ssd_norm_w", "ssd_w_out",
           "pool_w", "pool_scale", "ffn_w_up", "ffn_conv_w", "ffn_conv_b", "ffn_w_down", "norm_mix_pre",
           "norm_mix_post", "norm_ffn_pre", "norm_ffn_post")


def _ssd_sizes(d_inner):
    return d_inner + 2 * N_SSD_GROUPS * D_STATE, d_inner // HEAD_DIM // N_SSD_GROUPS


def _small_compute_layout(full, d_inner):
    _, r_heads = _ssd_sizes(d_inner)
    w = {k: full[k] for k in SMALL}
    for k in ("ssd_dt_bias", "ssd_a_log", "ssd_d"):
        w[k] = _head_pad(full[k], r_heads)
    for k in ("ffn_conv_w", "ffn_conv_b"):
        w[k] = _interleave(full[k])
    return w


def _matmul_compute_layout(k, full, d_inner):
    d_xbc, r_heads = _ssd_sizes(d_inner)
    if k == "ssd_w_in":
        return _ssd_w_in_layout(full, d_inner, d_xbc, r_heads)
    if k == "ffn_w_up":
        return _interleave(full)
    return full


def _layer_matrices(i, part):
    if part == "ffn":
        return (("ffn_w_up", 1, i), ("ffn_w_down", 0, i))
    if i % 2 == 1:
        return (("pool_w", 1, i // 2),) if part == "mix" else ()
    return (("ssd_w_in", 1, i // 2),) if part == "mix" else (("ssd_w_out", 0, i // 2),)


def _fetch_group(i, part):
    mix, out, ffn = (_layer_matrices(i, p) for p in ("mix", "out", "ffn"))
    if i % 2 == 1:
        return mix + ffn if part == "mix" else ()
    if i == 0:
        return {"mix": mix, "out": out + ffn, "ffn": ()}[part]
    return {"mix": mix + out, "out": (), "ffn": ffn}[part]


def _matmul_grad_reference_layout(k, g, d_inner):
    d_xbc, r_heads = _ssd_sizes(d_inner)
    if k == "ssd_w_in":
        return _ssd_w_in_unlayout(g, d_inner, d_xbc, r_heads)
    if k == "ffn_w_up":
        return _deinterleave(g)
    return g


def _small_grads_reference_layout(grads, shapes, d_inner):
    _, r_heads = _ssd_sizes(d_inner)
    g = {k: jnp.stack(grads[k]) for k in SMALL}
    for k in ("ssd_dt_bias", "ssd_a_log", "ssd_d"):
        g[k] = _head_unpad(g[k][:, 0], r_heads)
    for k in ("ffn_conv_w", "ffn_conv_b"):
        g[k] = _deinterleave(g[k])
    return {k: v.reshape(shapes[k]) for k, v in g.items()}


def kernel(x, ssd_w_in, ssd_conv_w, ssd_conv_b, ssd_dt_bias, ssd_a_log, ssd_d, ssd_norm_w, ssd_w_out, pool_w, pool_scale, ffn_w_up, ffn_conv_w, ffn_conv_b, ffn_w_down, norm_mix_pre, norm_mix_post, norm_ffn_pre, norm_ffn_post, loss_target, m_ssd_w_in, m_ssd_conv_w, m_ssd_conv_b, m_ssd_dt_bias, m_ssd_a_log, m_ssd_d, m_ssd_norm_w, m_ssd_w_out, m_pool_w, m_pool_scale, m_ffn_w_up, m_ffn_conv_w, m_ffn_conv_b, m_ffn_w_down, m_norm_mix_pre, m_norm_mix_post, m_norm_ffn_pre, m_norm_ffn_post, v_ssd_w_in, v_ssd_conv_w, v_ssd_conv_b, v_ssd_dt_bias, v_ssd_a_log, v_ssd_d, v_ssd_norm_w, v_ssd_w_out, v_pool_w, v_pool_scale, v_ffn_w_up, v_ffn_conv_w, v_ffn_conv_b, v_ffn_w_down, v_norm_mix_pre, v_norm_mix_post, v_norm_ffn_pre, v_norm_ffn_post):
    shards = dict(ssd_w_in=ssd_w_in, ssd_conv_w=ssd_conv_w, ssd_conv_b=ssd_conv_b, ssd_dt_bias=ssd_dt_bias,
                  ssd_a_log=ssd_a_log, ssd_d=ssd_d, ssd_norm_w=ssd_norm_w, ssd_w_out=ssd_w_out, pool_w=pool_w,
                  pool_scale=pool_scale, ffn_w_up=ffn_w_up, ffn_conv_w=ffn_conv_w, ffn_conv_b=ffn_conv_b,
                  ffn_w_down=ffn_w_down, norm_mix_pre=norm_mix_pre, norm_mix_post=norm_mix_post,
                  norm_ffn_pre=norm_ffn_pre, norm_ffn_post=norm_ffn_post)
    moments_m = dict(zip(WEIGHTS, (m_ssd_w_in, m_ssd_conv_w, m_ssd_conv_b, m_ssd_dt_bias, m_ssd_a_log, m_ssd_d, m_ssd_norm_w, m_ssd_w_out, m_pool_w, m_pool_scale, m_ffn_w_up, m_ffn_conv_w, m_ffn_conv_b, m_ffn_w_down, m_norm_mix_pre, m_norm_mix_post, m_norm_ffn_pre, m_norm_ffn_post)))
    moments_v = dict(zip(WEIGHTS, (v_ssd_w_in, v_ssd_conv_w, v_ssd_conv_b, v_ssd_dt_bias, v_ssd_a_log, v_ssd_d, v_ssd_norm_w, v_ssd_w_out, v_pool_w, v_pool_scale, v_ffn_w_up, v_ffn_conv_w, v_ffn_conv_b, v_ffn_w_down, v_norm_mix_pre, v_norm_mix_post, v_norm_ffn_pre, v_norm_ffn_post)))
    n_seq, seq, d_model = x.shape
    t = n_seq * seq

    d_inner = ssd_norm_w.shape[1]
    depth = norm_mix_pre.shape[0]
    x2 = x.reshape(t, d_model)

    shard16 = {k: shards[k].astype(BF16) for k, _ in BIG}
    order = [(i, part) for i in range(depth) for part in ("mix", "out", "ffn") if _fetch_group(i, part)]
    fetches = {}

    def start_fetch(key, after):
        srcs = [shard16[k][l] for k, _, l in _fetch_group(*key)]
        fetches[key] = _push_start(srcs, False, after, name=f"fetch{key[0]}{key[1]}_start")

    full = dict(shards)
    small_all = _all_gather(_pack_rows([shards[k] for k, _ in SMALL_SHARDED]), name="gather_small_weights")
    small_stacked = _unpack_rows(small_all, [shards[k].shape for k, _ in SMALL_SHARDED], lead=(N_DEV,))
    for (k, axis), st in zip(SMALL_SHARDED, small_stacked):
        full[k] = _unshard(st, axis)
    w = _small_compute_layout(full, d_inner)
    ready = {}

    def fetch(i, part, x_now):
        key = (i, part)
        token = jnp.zeros((8, LANE), F32)
        if key in order:
            if key == order[0]:
                wholes = [_unshard(_all_gather(shard16[k][l], name=f"fetch0_{k}"), axis) for k, axis, l in _fetch_group(i, part)]
                nxt_after = wholes[0]
            else:
                send, recv, srcs, lands, _ = fetches[key]
                lands = _push_wait(send, recv, srcs, lands, False, x_now, name=f"fetch{i}{part}_wait")
                wholes = [_unshard(_with_own_slot(land, shard16[k][l]), axis)
                          for (k, axis, l), land in zip(_fetch_group(i, part), lands)]
                nxt_after = lands[0]
            for (k, _, l), whole in zip(_fetch_group(i, part), wholes):
                ready[k, l] = _matmul_compute_layout(k, whole, d_inner)
            nxt = order.index(key) + 1
            if nxt < len(order):
                start_fetch(order[nxt], nxt_after)
                token = fetches[order[nxt]][4]
        return {k: ready[k, l] for k, _, l in _layer_matrices(i, part)}, token

    g_layers = {}
    in_flight = []

    def finish_exchange(after):
        key, blocks, (send, recv, srcs, lands, _) = in_flight.pop(0)
        lands = _push_wait(send, recv, srcs, lands, True, after, name=f"exchange{key[0]}{key[1]}_wait")
        for (k, _, l), land, block in zip(_layer_matrices(*key), lands, blocks):
            own = lax.dynamic_index_in_dim(block, _my_index(), 0, keepdims=False)
            g_layers[k, l] = _sum_slots(land, own, name=f"sum{key[0]}_{k}")

    def emit(i, part, gm, dx_now):
        if len(in_flight) >= 2:
            finish_exchange(dx_now)
        blocks = [_shard_major(_matmul_grad_reference_layout(k, gm[k].astype(BF16), d_inner), axis)
                  for k, axis, _ in _layer_matrices(i, part)]
        started = _push_start(blocks, True, dx_now, name=f"exchange{i}{part}_start")
        in_flight.append(((i, part), blocks, started))
        return started[4]

    loss, dx, grads = _local_step(x2, loss_target.reshape(t, d_model), w, fetch, emit, n_seq)
    loss = lax.psum(loss, ("x", "y", "c"))

    g_shard = {}
    small_shapes = {k: full[k].shape for k in SMALL}
    g_small = _small_grads_reference_layout(grads, small_shapes, d_inner)
    s_all = _all_gather(_pack_rows([g_small[k] for k in SMALL]) + in_flight[-1][2][4][0:1, :], name="gather_small_grads")
    for k, g in zip(SMALL, _unpack_rows(_sum_slots(s_all, name="sum_small_grads"), [small_shapes[k] for k in SMALL])):
        g_shard[k] = g
    for k, axis in SMALL_SHARDED:
        g_shard[k] = _my_shard(g_shard[k], axis)

    last = [k for key, _, _ in in_flight for k, _, _ in _layer_matrices(*key)]
    deltas, new_m, new_v = {}, {}, {}
    for k in [k for k in WEIGHTS if k not in last] + last:
        if k == last[0]:
            while in_flight:
                finish_exchange(deltas["ffn_w_up"])
        if k in dict(BIG):
            g_shard[k] = jnp.stack([g_layers[k, l] for l in range(shards[k].shape[0])])
        deltas[k], new_m[k], new_v[k] = _adamw(shards[k], g_shard[k], moments_m[k], moments_v[k], name=f"adamw_{k}")
    return (loss, dx.reshape(x.shape), *[g_shard[k] for k in WEIGHTS], *[deltas[k] for k in WEIGHTS],
            *[new_m[k] for k in WEIGHTS], *[new_v[k] for k in WEIGHTS])
```
